```python
import math
import jax, jax.numpy as jnp
from jax import lax
import numpy as np

D_MODEL = 1024
BATCH = 2
SEQ = 16384
DEPTH = 1
DEC_BATCH = 32
DEC_SEQ = 64
PAST_LEN = 4096

CHUNK = 64
D_MIX = D_MODEL
WL = D_MIX // 2
LRU_BLOCKS = 8
LRU_BW = WL // LRU_BLOCKS
CONV_W = 4
LRU_C = 8.0
WA = D_MIX - WL
N_HEADS = 8
HEAD_DIM = WA // N_HEADS
Q_BLOCK = 128
N_EXPERTS = 64
TOP_K = 8
N_GROUPS = 8
TOPK_GROUPS = 4
D_EXPERT = 256
D_SHARED = 256
ROUTED_SCALE = 2.5
MOE_BLOCK = 512
EPS = 1e-6
D_IN = 2 * WL + 3 * WA + N_HEADS
SPLITS = [WL, 2 * WL, 2 * WL + WA, 2 * WL + 2 * WA, 2 * WL + 3 * WA]

kernel_name = 'hymba_rglru_fox_moe_stream_step'


def _rmsnorm(x, g):
    xf = x.astype(jnp.float32)
    y = xf * lax.rsqrt(jnp.mean(xf * xf, axis=-1, keepdims=True) + EPS)
    return (y * g.astype(jnp.float32)).astype(x.dtype)


def _rglru_group(xl, yl, conv_prev, h0, p):
    n, t, _ = xl.shape
    xpad = jnp.concatenate([conv_prev.astype(xl.dtype), xl], axis=1)
    xc = p['conv_b'].astype(xl.dtype)
    for j in range(CONV_W):
        xc = xc + xpad[:, j:j + t] * p['conv_w'][j].astype(xl.dtype)
    conv_new = xpad[:, -(CONV_W - 1):]
    xf = xc.astype(jnp.float32)
    xh = xf.reshape(n, t, LRU_BLOCKS, LRU_BW)
    r = jax.nn.sigmoid(jnp.einsum('ntgi,gij->ntgj', xh, p['w_r'].astype(jnp.float32)).reshape(n, t, WL)
                       + p['b_r'].astype(jnp.float32))
    i = jax.nn.sigmoid(jnp.einsum('ntgi,gij->ntgj', xh, p['w_i'].astype(jnp.float32)).reshape(n, t, WL)
                       + p['b_i'].astype(jnp.float32))
    log_a = -LRU_C * r * jax.nn.softplus(-p['lru_lambda'].astype(jnp.float32))
    a = jnp.exp(log_a)
    b = jnp.sqrt(-jnp.expm1(2.0 * log_a)) * (i * xf)
    b = b.at[:, 0].add(a[:, 0] * h0.astype(jnp.float32))

    def combine(e1, e2):
        return (e1[0] * e2[0], e2[0] * e1[1] + e2[1])

    _, h = lax.associative_scan(combine, (a, b), axis=1)
    out = h.astype(xl.dtype) * jax.nn.gelu(yl, approximate=True)
    return out, conv_new, h[:, -1]


def _fox_block(q, k, v, dq, dk, qpos, kpos):
    s = jnp.einsum('nqhd,nkhd->nhqk', q, k).astype(jnp.float32) * (HEAD_DIM ** -0.5)
    bias = jnp.swapaxes(dq, 1, 2)[:, :, :, None] - jnp.swapaxes(dk, 1, 2)[:, :, None, :]
    mask = kpos[None, :] <= qpos[:, None]
    s = jnp.where(mask, s + bias, -jnp.inf)
    pr = jax.nn.softmax(s, axis=-1)
    return jnp.einsum('nhqk,nkhd->nqhd', pr.astype(v.dtype), v)


def _fox_prompt(q, k, v, lf):
    n, t, h, dh = q.shape
    dcum = jnp.cumsum(lf, axis=1)
    nb = t // Q_BLOCK
    qb = jnp.swapaxes(q.reshape(n, nb, Q_BLOCK, h, dh), 0, 1)
    db = jnp.swapaxes(dcum.reshape(n, nb, Q_BLOCK, h), 0, 1)
    kpos = jnp.arange(t)

    def one(args):
        qi, di, bi = args
        qpos = bi * Q_BLOCK + jnp.arange(Q_BLOCK)
        return _fox_block(qi, k, v, di, dcum, qpos, kpos)

    o = lax.map(one, (qb, db, jnp.arange(nb)))
    return jnp.swapaxes(o, 0, 1).reshape(n, t, h, dh)


def _fox_sample(q, k, v, lf, k_past, v_past, lf_past):
    t = q.shape[1]
    past = k_past.shape[1]
    kf = jnp.concatenate([k_past.astype(k.dtype), k], axis=1)
    vf = jnp.concatenate([v_past.astype(v.dtype), v], axis=1)
    dcum = jnp.cumsum(jnp.concatenate([lf_past.astype(jnp.float32), lf], axis=1), axis=1)
    return _fox_block(q, kf, vf, dcum[:, past:], dcum, past + jnp.arange(t), jnp.arange(past + t))


def _moe(h, p):
    n, t, d = h.shape
    m = n * t
    xt = h.reshape(m, d)
    s = jax.nn.sigmoid((xt @ p['router_w']).astype(jnp.float32))
    sb = s + p['router_bias'].astype(jnp.float32)
    grp = lax.top_k(sb.reshape(m, N_GROUPS, N_EXPERTS // N_GROUPS), 2)[0].sum(-1)
    _, gidx = lax.top_k(grp, TOPK_GROUPS)
    gmask = jnp.any(gidx[..., None] == jnp.arange(N_GROUPS), axis=-2)
    emask = jnp.repeat(gmask, N_EXPERTS // N_GROUPS, axis=1)
    _, eidx = lax.top_k(jnp.where(emask, sb, -jnp.inf), TOP_K)
    wsel = jnp.take_along_axis(s, eidx, axis=-1)
    wsel = wsel / jnp.sum(wsel, axis=-1, keepdims=True) * ROUTED_SCALE
    gates = jnp.sum(jnp.where(eidx[..., None] == jnp.arange(N_EXPERTS), wsel[..., None], 0.0), axis=1)
    pad = (-m) % MOE_BLOCK
    xp = jnp.pad(xt, ((0, pad), (0, 0))).reshape(-1, MOE_BLOCK, d)
    gp = jnp.pad(gates, ((0, pad), (0, 0))).reshape(-1, MOE_BLOCK, N_EXPERTS)
    w_gate, w_up, w_down = p['w_gate'], p['w_up'], p['w_down']

    def blk(args):
        xb, gb = args
        hg = jnp.einsum('md,edf->mef', xb, w_gate)
        hu = jnp.einsum('md,edf->mef', xb, w_up)
        hh = jax.nn.silu(hg) * hu * gb[..., None].astype(xb.dtype)
        return jnp.einsum('mef,efd->md', hh, w_down)

    routed = lax.map(blk, (xp, gp)).reshape(-1, d)[:m]
    shared = (jax.nn.silu(xt @ p['ws_gate']) * (xt @ p['ws_up'])) @ p['ws_down']
    return (routed + shared).reshape(n, t, d)


def _layer(x, c, conv_prev, h0, k_past, v_past, lf_past, p):
    n, t, _ = x.shape
    mod = (jax.nn.silu(c.astype(jnp.float32)) @ p['w_mod'].astype(jnp.float32)
           + p['b_mod'].astype(jnp.float32)).reshape(n, 6, D_MODEL).astype(x.dtype)
    sh_a, sc_a, gt_a = mod[:, 0, None, :], mod[:, 1, None, :], mod[:, 2, None, :]
    sh_f, sc_f, gt_f = mod[:, 3, None, :], mod[:, 4, None, :], mod[:, 5, None, :]

    hn = _rmsnorm(x, p['g_pre_mix']) * (1 + sc_a) + sh_a
    proj = hn @ p['w_in']
    xl, yl, q, k, v, fl = jnp.split(proj, SPLITS, axis=-1)

    lru_out, conv_new, h_new = _rglru_group(xl, yl, conv_prev, h0, p)

    q = q.reshape(n, t, N_HEADS, HEAD_DIM)
    k = k.reshape(n, t, N_HEADS, HEAD_DIM)
    v = v.reshape(n, t, N_HEADS, HEAD_DIM)
    lf = jax.nn.log_sigmoid(fl.astype(jnp.float32) + p['b_f'].astype(jnp.float32))
    if k_past is None:
        o = _fox_prompt(q, k, v, lf)
    else:
        o = _fox_sample(q, k, v, lf, k_past, v_past, lf_past)
    att_out = o.reshape(n, t, WA)

    mix = jnp.concatenate([_rmsnorm(lru_out, p['g_lru_out']), _rmsnorm(att_out, p['g_att_out'])], axis=-1) @ p['w_out']
    x = x + gt_a * _rmsnorm(mix, p['g_post_mix'])

    hf = _rmsnorm(x, p['g_pre_ffn']) * (1 + sc_f) + sh_f
    x = x + gt_f * _rmsnorm(_moe(hf, p), p['g_post_ffn'])
    return x, (k, v, lf, conv_new, h_new)


def setup_inputs(seed: int = 0) -> dict:
    key = jax.random.key(seed)
    ks = list(jax.random.split(key, 48))
    f32 = jnp.float32
    L = DEPTH

    def nrm(i, shape, scale):
        return scale * jax.random.normal(ks[i], shape, f32)

    u = jax.random.uniform(ks[22], (L, WL), f32, minval=0.9, maxval=0.999)
    a0 = u ** (1.0 / LRU_C)
    return {
        'x_prompt': nrm(0, (BATCH, SEQ, D_MODEL), 1.0),
        'x_sample': nrm(1, (DEC_BATCH, DEC_SEQ, D_MODEL), 1.0),
        'c_prompt': nrm(2, (BATCH, D_MODEL), 1.0),
        'c_sample': nrm(3, (DEC_BATCH, D_MODEL), 1.0),
        'cache_k': nrm(4, (L, DEC_BATCH, PAST_LEN, N_HEADS, HEAD_DIM), 1.0),
        'cache_v': nrm(5, (L, DEC_BATCH, PAST_LEN, N_HEADS, HEAD_DIM), 1.0),
        'cache_logf': jax.nn.log_sigmoid(2.0 + nrm(6, (L, DEC_BATCH, PAST_LEN, N_HEADS), 1.0)),
        'state_conv': nrm(7, (L, DEC_BATCH, CONV_W - 1, WL), 1.0),
        'state_lru': nrm(8, (L, DEC_BATCH, WL), 0.5),
        'w_mod': nrm(9, (L, D_MODEL, 6 * D_MODEL), 0.5 * D_MODEL ** -0.5),
        'b_mod': nrm(10, (L, 6 * D_MODEL), 0.02),
        'g_pre_mix': 1.0 + nrm(11, (L, D_MODEL), 0.05),
        'g_post_mix': 1.0 + nrm(12, (L, D_MODEL), 0.05),
        'g_pre_ffn': 1.0 + nrm(13, (L, D_MODEL), 0.05),
        'g_post_ffn': 1.0 + nrm(14, (L, D_MODEL), 0.05),
        'w_in': nrm(15, (L, D_MODEL, D_IN), D_MODEL ** -0.5),
        'conv_w': nrm(16, (L, CONV_W, WL), CONV_W ** -0.5),
        'conv_b': nrm(17, (L, WL), 0.02),
        'w_r': nrm(18, (L, LRU_BLOCKS, LRU_BW, LRU_BW), LRU_BW ** -0.5),
        'b_r': nrm(19, (L, WL), 0.02),
        'w_i': nrm(20, (L, LRU_BLOCKS, LRU_BW, LRU_BW), LRU_BW ** -0.5),
        'b_i': nrm(21, (L, WL), 0.02),
        'lru_lambda': jnp.log(a0) - jnp.log1p(-a0),
        'b_f': 2.0 + nrm(23, (L, N_HEADS), 0.5),
        'g_lru_out': 1.0 + nrm(24, (L, WL), 0.05),
        'g_att_out': 1.0 + nrm(25, (L, WA), 0.05),
        'w_out': nrm(26, (L, D_MIX, D_MODEL), D_MIX ** -0.5),
        'router_w': nrm(27, (L, D_MODEL, N_EXPERTS), D_MODEL ** -0.5),
        'router_bias': nrm(28, (L, N_EXPERTS), 0.01),
        'w_gate': nrm(29, (L, N_EXPERTS, D_MODEL, D_EXPERT), D_MODEL ** -0.5),
        'w_up': nrm(30, (L, N_EXPERTS, D_MODEL, D_EXPERT), D_MODEL ** -0.5),
        'w_down': nrm(31, (L, N_EXPERTS, D_EXPERT, D_MODEL), D_EXPERT ** -0.5),
        'ws_gate': nrm(32, (L, D_MODEL, D_SHARED), D_MODEL ** -0.5),
        'ws_up': nrm(33, (L, D_MODEL, D_SHARED), D_MODEL ** -0.5),
        'ws_down': nrm(34, (L, D_SHARED, D_MODEL), D_SHARED ** -0.5),
    }


def reference(x_prompt, x_sample, c_prompt, c_sample, cache_k, cache_v, cache_logf, state_conv, state_lru,
              w_mod, b_mod, g_pre_mix, g_post_mix, g_pre_ffn, g_post_ffn, w_in, conv_w, conv_b,
              w_r, b_r, w_i, b_i, lru_lambda, b_f, g_lru_out, g_att_out, w_out,
              router_w, router_bias, w_gate, w_up, w_down, ws_gate, ws_up, ws_down):
    yp, ys = x_prompt, x_sample
    st_p, st_s = [], []
    for l in range(DEPTH):
        p = dict(w_mod=w_mod[l], b_mod=b_mod[l], g_pre_mix=g_pre_mix[l], g_post_mix=g_post_mix[l],
                 g_pre_ffn=g_pre_ffn[l], g_post_ffn=g_post_ffn[l], w_in=w_in[l], conv_w=conv_w[l],
                 conv_b=conv_b[l], w_r=w_r[l], b_r=b_r[l], w_i=w_i[l], b_i=b_i[l], lru_lambda=lru_lambda[l],
                 b_f=b_f[l], g_lru_out=g_lru_out[l], g_att_out=g_att_out[l], w_out=w_out[l],
                 router_w=router_w[l], router_bias=router_bias[l], w_gate=w_gate[l], w_up=w_up[l],
                 w_down=w_down[l], ws_gate=ws_gate[l], ws_up=ws_up[l], ws_down=ws_down[l])
        conv0 = jnp.zeros((yp.shape[0], CONV_W - 1, WL), yp.dtype)
        h0 = jnp.zeros((yp.shape[0], WL), jnp.float32)
        yp, sp = _layer(yp, c_prompt, conv0, h0, None, None, None, p)
        ys, ss = _layer(ys, c_sample, state_conv[l], state_lru[l], cache_k[l], cache_v[l], cache_logf[l], p)
        st_p.append(sp)
        st_s.append(ss)
    k_prompt = jnp.stack([s[0] for s in st_p])
    v_prompt = jnp.stack([s[1] for s in st_p])
    logf_prompt = jnp.stack([s[2] for s in st_p])
    conv_prompt = jnp.stack([s[3] for s in st_p])
    lru_prompt = jnp.stack([s[4] for s in st_p])
    k_sample = jnp.stack([s[0] for s in st_s])
    v_sample = jnp.stack([s[1] for s in st_s])
    logf_sample = jnp.stack([s[2] for s in st_s])
    conv_sample = jnp.stack([s[3] for s in st_s])
    lru_sample = jnp.stack([s[4] for s in st_s])
    return (yp, ys, k_prompt, v_prompt, logf_prompt, conv_prompt, lru_prompt,
            k_sample, v_sample, logf_sample, conv_sample, lru_sample)
```

```python
import functools

import jax
import jax.numpy as jnp
from jax import lax
from jax.experimental import pallas as pl
from jax.experimental.pallas import tpu as pltpu

F32 = jnp.float32
BF16 = jnp.bfloat16

D_MODEL = 1024
WL = 512
WA = 512
N_HEADS = 8
HEAD_DIM = 64
N_PAIRS = N_HEADS // 2
PAIR_W = 2 * HEAD_DIM
LANES = 128
CONV_W = 4
LRU_BLOCKS = 8
LRU_C = 8.0
N_EXPERTS = 64
N_GROUPS = 8
GROUP_SIZE = N_EXPERTS // N_GROUPS
TOPK_GROUPS = 4
TOP_K = 8
D_EXPERT = 256
ROUTED_SCALE = 2.5
EPS = 1e-6
NEG_INF = float("-inf")

ROW_TILE = 512
ATT_TQ = 512
ATT_TK = 512
SAMPLE_TK = 1024
MOE_TILE = 1024
VMEM_LIMIT = 56 * 1024 * 1024


def _params(*sem):
    return pltpu.CompilerParams(dimension_semantics=sem, vmem_limit_bytes=VMEM_LIMIT)


def _dot(a, b):
    return jnp.dot(a, b, preferred_element_type=F32)


def _dot_nt(a, b):
    return lax.dot_general(a, b, (((1,), (1,)), ((), ())), preferred_element_type=F32)


def _split3(x):
    hi = x.astype(BF16)
    r1 = x - hi.astype(F32)
    mid = r1.astype(BF16)
    lo = (r1 - mid.astype(F32)).astype(BF16)
    return hi, mid, lo


def _rms(x, g):
    return x * lax.rsqrt(jnp.mean(x * x, axis=-1, keepdims=True) + EPS) * g


def _sigmoid(x):
    return 1.0 / (1.0 + jnp.exp(-x))


def _silu(x):
    return x * _sigmoid(x)


def _gelu_tanh(x):
    return 0.5 * x * (1.0 + jnp.tanh(0.7978845608028654 * (x + 0.044715 * (x * x * x))))


def _log_sigmoid(x):
    return jnp.minimum(x, 0.0) - jnp.log1p(jnp.exp(-jnp.abs(x)))


def _seq_blocks(n, t):
    if t >= ROW_TILE:
        assert t % ROW_TILE == 0
        return 1, ROW_TILE
    nb = ROW_TILE // t
    assert nb * t == ROW_TILE and n % nb == 0
    return nb, t


def _mod_kernel(c_ref, w_ref, b_ref, o_ref):
    c = _silu(c_ref[...])
    c_hi = c.astype(BF16)
    c_lo = (c - c_hi.astype(F32)).astype(BF16)
    w = w_ref[...]
    w_hi = w.astype(BF16)
    w_lo = (w - w_hi.astype(F32)).astype(BF16)
    o_ref[...] = _dot(c_hi, w_hi) + _dot(c_lo, w_hi) + _dot(c_hi, w_lo) + b_ref[...]


def _modulation(c, w_mod, b_mod):
    rows = c.shape[0]
    n = -(-rows // 8) * 8
    c = jnp.pad(c, ((0, n - rows), (0, 0)))
    d6 = w_mod.shape[1]
    return pl.pallas_call(
        _mod_kernel,
        grid=(d6 // D_MODEL,),
        in_specs=[pl.BlockSpec((n, D_MODEL), lambda j: (0, 0)),
                  pl.BlockSpec((D_MODEL, D_MODEL), lambda j: (0, j)),
                  pl.BlockSpec((1, D_MODEL), lambda j: (0, j))],
        out_specs=pl.BlockSpec((n, D_MODEL), lambda j: (0, j)),
        out_shape=jax.ShapeDtypeStruct((n, d6), F32),
        compiler_params=_params("arbitrary"),
        name="modulation",
    )(c, w_mod, b_mod.reshape(1, d6))[:rows]


def _inproj_kernel(x_ref, mod_ref, g_ref, w_ref, wf_ref, bf_ref,
                   xl_ref, gy_ref, qb_ref, kb_ref, vb_ref, k_ref, v_ref, lf_ref, lfp_ref):
    nb, tt, d = x_ref.shape
    x = x_ref[...]
    mod = mod_ref[...]
    hn = _rms(x, g_ref[...]) * (1.0 + mod[:, 1:2, :]) + mod[:, 0:1, :]
    hb = hn.reshape(nb * tt, d).astype(BF16)

    def proj(col):
        return _dot(hb, w_ref[:, col * WL:(col + 1) * WL]).reshape(nb, tt, WL)

    xl_ref[...] = proj(0)
    gy_ref[...] = _gelu_tanh(proj(1))
    qb_ref[...] = (proj(2) * (HEAD_DIM ** -0.5)).astype(BF16)
    k = proj(3)
    k_ref[...] = k
    kb_ref[...] = k.astype(BF16)
    v = proj(4)
    v_ref[...] = v
    vb_ref[...] = v.astype(BF16)
    fl = _dot(hb, wf_ref[...]) + bf_ref[...]
    lane = lax.broadcasted_iota(jnp.int32, fl.shape, 1)
    lf = jnp.where(lane < N_HEADS, _log_sigmoid(fl), 0.0).reshape(nb, tt, LANES)
    lfp_ref[...] = lf
    lf_ref[...] = lf[:, :, :N_HEADS]


def _inproj(x, mod, g_pre, w_main, w_f, b_f):
    n, t, d = x.shape
    nb, tt = _seq_blocks(n, t)
    blk = lambda w: pl.BlockSpec((nb, tt, w), lambda i, j: (i, j, 0))
    const = lambda shape: pl.BlockSpec(shape, lambda i, j: (0,) * len(shape))
    f32 = lambda w: jax.ShapeDtypeStruct((n, t, w), F32)
    b16 = lambda w: jax.ShapeDtypeStruct((n, t, w), BF16)
    return pl.pallas_call(
        _inproj_kernel,
        grid=(n // nb, t // tt),
        in_specs=[blk(d),
                  pl.BlockSpec((nb, 6, d), lambda i, j: (i, 0, 0)),
                  const((1, d)), const(w_main.shape), const(w_f.shape), const((1, LANES))],
        out_specs=[blk(WL), blk(WL), blk(WA), blk(WA), blk(WA), blk(WA), blk(WA), blk(N_HEADS), blk(LANES)],
        out_shape=[f32(WL), f32(WL), b16(WA), b16(WA), b16(WA), f32(WA), f32(WA), f32(N_HEADS), f32(LANES)],
        compiler_params=_params("parallel", "arbitrary"),
        name="inproj",
    )(x, mod, g_pre, w_main, w_f, b_f)


def _expm1_neg(x):
    poly = x * (1.0 + x * (0.5 + x * (1.0 / 6.0 + x * (1.0 / 24.0 + x * (1.0 / 120.0)))))
    return jnp.where(x > -0.1, poly, jnp.exp(x) - 1.0)


def _lru_kernel(xl_ref, gy_ref, conv0_ref, h0_ref, cw_ref, cb_ref, wr_ref, br_ref, wi_ref, bi_ref,
                lam_ref, g_ref, out_ref, conv_ref, hlast_ref, tail_ref, carry_ref):
    nb, tt, w = xl_ref.shape
    j = pl.program_id(1)

    @pl.when(j == 0)
    def _():
        tail_ref[:, 8 - (CONV_W - 1):, :] = conv0_ref[...]
        carry_ref[...] = h0_ref[...]

    xl = xl_ref[...]
    xpad = jnp.concatenate([tail_ref[...], xl], axis=1)
    cw = cw_ref[...]
    xc = jnp.zeros_like(xl) + cb_ref[...]
    for k in range(CONV_W):
        off = 8 - (CONV_W - 1) + k
        xc = xc + xpad[:, off:off + tt, :] * cw[k:k + 1, :]
    conv_ref[...] = xpad[:, tt + 8 - (CONV_W - 1):, :]
    tail_ref[...] = xpad[:, tt:, :]

    m = nb * tt
    xf = xc.reshape(m, w)
    xb = xf.astype(BF16)
    r = _sigmoid(_dot(xb, wr_ref[...]) + br_ref[...])
    gi = _sigmoid(_dot(xb, wi_ref[...]) + bi_ref[...])
    lam = lam_ref[...]
    softplus = jnp.maximum(-lam, 0.0) + jnp.log1p(jnp.exp(-jnp.abs(lam)))
    log_a = (-LRU_C) * r * softplus
    a = jnp.exp(log_a)
    b = jnp.sqrt(-_expm1_neg(2.0 * log_a)) * (gi * xf)

    pos = lax.broadcasted_iota(jnp.int32, (m, w), 0) % tt
    d = 1
    while d < tt:
        keep = pos >= d
        a_prev = jnp.where(keep, pltpu.roll(a, d, 0), 1.0)
        b_prev = jnp.where(keep, pltpu.roll(b, d, 0), 0.0)
        b = a * b_prev + b
        a = a * a_prev
        d *= 2
    h = a.reshape(nb, tt, w) * carry_ref[...] + b.reshape(nb, tt, w)
    h_last = h[:, tt - 1:tt, :]
    carry_ref[...] = h_last
    hlast_ref[...] = h_last
    out_ref[...] = _rms(h * gy_ref[...], g_ref[...]).astype(BF16)


def _lru(xl, gy, conv0, h0, conv_w, conv_b, wr_bd, b_r, wi_bd, b_i, lam, g_lru):
    n, t, w = xl.shape
    nb, tt = _seq_blocks(n, t)
    blk = pl.BlockSpec((nb, tt, w), lambda i, j: (i, j, 0))
    per_seq = lambda rows: pl.BlockSpec((nb, rows, w), lambda i, j: (i, 0, 0))
    const = lambda shape: pl.BlockSpec(shape, lambda i, j: (0,) * len(shape))
    row = const((1, w))
    return pl.pallas_call(
        _lru_kernel,
        grid=(n // nb, t // tt),
        in_specs=[blk, blk, per_seq(CONV_W - 1), per_seq(1),
                  const((CONV_W, w)), row, const((w, w)), row, const((w, w)), row, row, row],
        out_specs=[blk, per_seq(CONV_W - 1), per_seq(1)],
        out_shape=[jax.ShapeDtypeStruct((n, t, w), BF16),
                   jax.ShapeDtypeStruct((n, CONV_W - 1, w), F32),
                   jax.ShapeDtypeStruct((n, 1, w), F32)],
        scratch_shapes=[pltpu.VMEM((nb, 8, w), F32), pltpu.VMEM((nb, 1, w), F32)],
        compiler_params=_params("parallel", "arbitrary"),
        name="rglru",
    )(xl, gy, conv0, h0, conv_w, conv_b, wr_bd, b_r, wi_bd, b_i, lam, g_lru)


def _cumsum_kernel(lf_ref, o_ref, carry_ref):
    tb = lf_ref.shape[1]

    @pl.when(pl.program_id(1) == 0)
    def _():
        carry_ref[...] = jnp.zeros_like(carry_ref)

    row = lax.broadcasted_iota(jnp.int32, (tb, tb), 0)
    col = lax.broadcasted_iota(jnp.int32, (tb, tb), 1)
    tril = (col <= row).astype(BF16)
    hi, mid, lo = _split3(lf_ref[0])
    d = _dot(tril, hi) + _dot(tril, mid) + _dot(tril, lo) + carry_ref[...]
    carry_ref[...] = d[tb - 1:tb, :]
    o_ref[0] = d.T[:N_HEADS, :]


def _cumsum_heads(lfp, tb):
    n, t, _ = lfp.shape
    return pl.pallas_call(
        _cumsum_kernel,
        grid=(n, t // tb),
        in_specs=[pl.BlockSpec((1, tb, LANES), lambda i, j: (i, j, 0))],
        out_specs=pl.BlockSpec((1, N_HEADS, tb), lambda i, j: (i, 0, j)),
        out_shape=jax.ShapeDtypeStruct((n, N_HEADS, t), F32),
        scratch_shapes=[pltpu.VMEM((1, LANES), F32)],
        compiler_params=_params("parallel", "arbitrary"),
        name="logf_cumsum",
    )(lfp)


def _pair_masks(x):
    lane = lax.broadcasted_iota(jnp.int32, x.shape, 1)
    zero = jnp.zeros_like(x)
    return jnp.where(lane < HEAD_DIM, x, zero), jnp.where(lane >= HEAD_DIM, x, zero)


def _online_update(s, m_prev, l_prev):
    m_new = jnp.maximum(m_prev, jnp.max(s, axis=1, keepdims=True))
    alpha = jnp.exp(m_prev - m_new)
    p = jnp.exp(s - m_new)
    l_new = alpha * l_prev + jnp.sum(p, axis=1, keepdims=True)
    return p, alpha, m_new, l_new


def _pair_step(q_a, q_b, k, v_a, v_b, bias_a, bias_b, mask, state):
    m_a, l_a, m_b, l_b, acc = state
    s_a = _dot_nt(q_a, k) + bias_a
    s_b = _dot_nt(q_b, k) + bias_b
    if mask is not None:
        s_a = jnp.where(mask, s_a, NEG_INF)
        s_b = jnp.where(mask, s_b, NEG_INF)
    p_a, al_a, m_a, l_a = _online_update(s_a, m_a, l_a)
    p_b, al_b, m_b, l_b = _online_update(s_b, m_b, l_b)
    lane = lax.broadcasted_iota(jnp.int32, acc.shape, 1)
    alpha = jnp.where(lane < HEAD_DIM, al_a, al_b)
    acc = alpha * acc + (_dot(p_a.astype(BF16), v_a) + _dot(p_b.astype(BF16), v_b))
    return m_a, l_a, m_b, l_b, acc


def _pair_finish(state):
    m_a, l_a, m_b, l_b, acc = state
    lane = lax.broadcasted_iota(jnp.int32, acc.shape, 1)
    return acc / jnp.where(lane < HEAD_DIM, l_a, l_b)


def _att_prompt_kernel(q_ref, k_ref, v_ref, dk_ref, dq_ref, o_ref):
    tq = q_ref.shape[1]
    i = pl.program_id(2)
    q_a, q_b = _pair_masks(q_ref[0])
    d_ref = dq_ref[0, 0][:, 0:1]

    def block(j, mask, state):
        start = pl.multiple_of(j * ATT_TK, ATT_TK)
        k = k_ref[0, pl.ds(start, ATT_TK), :]
        v_a, v_b = _pair_masks(v_ref[0, pl.ds(start, ATT_TK), :])
        bias = d_ref - dk_ref[0, 0, :, pl.ds(start, ATT_TK)]
        return _pair_step(q_a, q_b, k, v_a, v_b, bias[0:1], bias[1:2], mask, state)

    col = jnp.full((tq, 1), NEG_INF, F32)
    zero = jnp.zeros((tq, 1), F32)
    state = (col, zero, col, zero, jnp.zeros((tq, PAIR_W), F32))
    state = lax.fori_loop(0, i, lambda j, st: block(j, None, st), state)
    qpos = lax.broadcasted_iota(jnp.int32, (tq, ATT_TK), 0)
    kpos = lax.broadcasted_iota(jnp.int32, (tq, ATT_TK), 1)
    state = block(i, kpos <= qpos, state)
    o_ref[0] = _pair_finish(state)


def _att_prompt(qb, kb, vb, dcum):
    n, t, _ = qb.shape
    assert ATT_TQ == ATT_TK and t % ATT_TQ == 0
    dpair = dcum.reshape(n, N_PAIRS, 2, t)
    return pl.pallas_call(
        _att_prompt_kernel,
        grid=(n, N_PAIRS, t // ATT_TQ),
        in_specs=[pl.BlockSpec((1, ATT_TQ, PAIR_W), lambda b, h, i: (b, i, h)),
                  pl.BlockSpec((1, t, PAIR_W), lambda b, h, i: (b, 0, h)),
                  pl.BlockSpec((1, t, PAIR_W), lambda b, h, i: (b, 0, h)),
                  pl.BlockSpec((1, 1, 2, t), lambda b, h, i: (b, h, 0, 0)),
                  pl.BlockSpec((1, 1, 2, ATT_TQ), lambda b, h, i: (b, h, 0, i))],
        out_specs=pl.BlockSpec((1, ATT_TQ, PAIR_W), lambda b, h, i: (b, i, h)),
        out_shape=jax.ShapeDtypeStruct((n, t, WA), F32),
        compiler_params=_params("parallel", "parallel", "arbitrary"),
        name="att_prompt",
    )(qb, kb, vb, dpair, dpair)


def _att_sample_kernel(q_ref, kp_ref, vp_ref, dp_ref, dend_ref, kn_ref, vn_ref, dn_ref, o_ref,
                       m_ref, l_ref, acc_ref):
    t = q_ref.shape[1]
    j = pl.program_id(1)

    @pl.when(j == 0)
    def _():
        m_ref[...] = jnp.full_like(m_ref, NEG_INF)
        l_ref[...] = jnp.zeros_like(l_ref)
        acc_ref[...] = jnp.zeros_like(acc_ref)

    def pair(h, k, v, bias, mask):
        sl = slice(h * PAIR_W, (h + 1) * PAIR_W)
        q_a, q_b = _pair_masks(q_ref[0, :, sl])
        v_a, v_b = _pair_masks(v[:, sl].astype(BF16))
        state = (m_ref[2 * h], l_ref[2 * h], m_ref[2 * h + 1], l_ref[2 * h + 1], acc_ref[:, sl])
        state = _pair_step(q_a, q_b, k[:, sl].astype(BF16), v_a, v_b,
                           bias[2 * h:2 * h + 1], bias[2 * h + 1:2 * h + 2], mask, state)
        m_ref[2 * h], l_ref[2 * h], m_ref[2 * h + 1], l_ref[2 * h + 1], acc_ref[:, sl] = state
        return state

    bias_past = dend_ref[0] - dp_ref[0]
    for h in range(N_PAIRS):
        pair(h, kp_ref[0], vp_ref[0], bias_past, None)

    @pl.when(j == pl.num_programs(1) - 1)
    def _():
        bias_new = -dn_ref[0]
        qpos = lax.broadcasted_iota(jnp.int32, (t, t), 0)
        kpos = lax.broadcasted_iota(jnp.int32, (t, t), 1)
        for h in range(N_PAIRS):
            state = pair(h, kn_ref[0], vn_ref[0], bias_new, kpos <= qpos)
            o_ref[0, :, h * PAIR_W:(h + 1) * PAIR_W] = _pair_finish(state)


def _att_sample(qb, k_past, v_past, d_past, kb_new, vb_new, d_new):
    n, t, _ = qb.shape
    past = k_past.shape[1]
    assert past % SAMPLE_TK == 0
    d_end = d_past[:, :, past - 1:]
    new = lambda dt: pl.BlockSpec((1, t, WA), lambda b, j: (b, 0, 0))
    return pl.pallas_call(
        _att_sample_kernel,
        grid=(n, past // SAMPLE_TK),
        in_specs=[new(BF16),
                  pl.BlockSpec((1, SAMPLE_TK, WA), lambda b, j: (b, j, 0)),
                  pl.BlockSpec((1, SAMPLE_TK, WA), lambda b, j: (b, j, 0)),
                  pl.BlockSpec((1, N_HEADS, SAMPLE_TK), lambda b, j: (b, 0, j)),
                  pl.BlockSpec((1, N_HEADS, 1), lambda b, j: (b, 0, 0)),
                  new(BF16), new(BF16),
                  pl.BlockSpec((1, N_HEADS, t), lambda b, j: (b, 0, 0))],
        out_specs=pl.BlockSpec((1, t, WA), lambda b, j: (b, 0, 0)),
        out_shape=jax.ShapeDtypeStruct((n, t, WA), F32),
        scratch_shapes=[pltpu.VMEM((N_HEADS, t, 1), F32), pltpu.VMEM((N_HEADS, t, 1), F32),
                        pltpu.VMEM((t, WA), F32)],
        compiler_params=_params("parallel", "arbitrary"),
        name="att_sample",
    )(qb, k_past, v_past, d_past, d_end, kb_new, vb_new, d_new)


def _first_index_of_max(x, axis):
    mx = jnp.max(x, axis=axis, keepdims=True)
    idx = lax.broadcasted_iota(jnp.int32, x.shape, axis)
    first = jnp.min(jnp.where(x == mx, idx, x.shape[axis]), axis=axis, keepdims=True)
    return mx, idx == first


def _route(s, bias):
    m = s.shape[1]
    sb = (s + bias).reshape(N_GROUPS, GROUP_SIZE, m)
    top1, is_top1 = _first_index_of_max(sb, 1)
    top2 = jnp.max(jnp.where(is_top1, NEG_INF, sb), axis=1, keepdims=True)
    grp = (top1 + top2).reshape(N_GROUPS, m)
    gi = lax.broadcasted_iota(jnp.int32, (N_GROUPS, N_GROUPS, m), 0)
    gj = lax.broadcasted_iota(jnp.int32, (N_GROUPS, N_GROUPS, m), 1)
    other, mine = grp[None, :, :], grp[:, None, :]
    beats = (other > mine) | ((other == mine) & (gj < gi))
    g_rank = jnp.sum(beats.astype(jnp.int32), axis=1)
    g_keep = (g_rank < TOPK_GROUPS)[:, None, :]
    cand = jnp.where(g_keep, sb, NEG_INF).reshape(N_EXPERTS, m)
    sel = jnp.zeros((N_EXPERTS, m), jnp.bool_)
    for _ in range(TOP_K):
        _, pick = _first_index_of_max(cand, 0)
        sel = sel | pick
        cand = jnp.where(pick, NEG_INF, cand)
    w = jnp.where(sel, s, 0.0)
    return w / jnp.sum(w, axis=0, keepdims=True) * ROUTED_SCALE


def _outproj_kernel(x_ref, lru_ref, att_ref, mod_ref, gatt_ref, wtop_ref, wbot_ref, gpost_ref, gpre_ref,
                    rwh_ref, rwl_ref, rb_ref, x1_ref, hf_ref, gates_ref):
    nb, tt, d = x_ref.shape
    m = nb * tt
    mod = mod_ref[...]
    att_n = _rms(att_ref[...], gatt_ref[...]).reshape(m, WA).astype(BF16)
    mix = _dot(lru_ref[...].reshape(m, WL), wtop_ref[...]) + _dot(att_n, wbot_ref[...])
    x1 = x_ref[...] + mod[:, 2:3, :] * _rms(mix, gpost_ref[...]).reshape(nb, tt, d)
    x1_ref[...] = x1
    hf = (_rms(x1, gpre_ref[...]) * (1.0 + mod[:, 4:5, :]) + mod[:, 3:4, :]).reshape(m, d)
    hf_hi = hf.astype(BF16)
    hf_ref[...] = hf_hi.reshape(nb, tt, d)
    hf_lo = (hf - hf_hi.astype(F32)).astype(BF16)
    rwh = rwh_ref[...]
    logits = _dot_nt(rwh, hf_hi) + _dot_nt(rwh, hf_lo) + _dot_nt(rwl_ref[...], hf_hi)
    gates = _route(_sigmoid(logits), rb_ref[...])
    gates = jnp.concatenate([gates, jnp.zeros((LANES - N_EXPERTS, m), F32)], axis=0)
    gates_ref[...] = gates.T.reshape(nb, tt, LANES)


def _outproj(x, lru_n, att, mod, g_att, w_top, w_bot, g_post, g_pre, rw_hi, rw_lo, r_bias):
    n, t, d = x.shape
    nb, tt = _seq_blocks(n, t)
    blk = lambda w: pl.BlockSpec((nb, tt, w), lambda i, j: (i, j, 0))
    const = lambda shape: pl.BlockSpec(shape, lambda i, j: (0,) * len(shape))
    return pl.pallas_call(
        _outproj_kernel,
        grid=(n // nb, t // tt),
        in_specs=[blk(d), blk(WL), blk(WA), pl.BlockSpec((nb, 6, d), lambda i, j: (i, 0, 0)),
                  const((1, WA)), const((WL, d)), const((WA, d)), const((1, d)), const((1, d)),
                  const((N_EXPERTS, d)), const((N_EXPERTS, d)), const((N_EXPERTS, 1))],
        out_specs=[blk(d), blk(d), blk(LANES)],
        out_shape=[jax.ShapeDtypeStruct((n, t, d), F32), jax.ShapeDtypeStruct((n, t, d), BF16),
                   jax.ShapeDtypeStruct((n, t, LANES), F32)],
        compiler_params=_params("parallel", "parallel"),
        name="outproj_router",
    )(x, lru_n, att, mod, g_att, w_top, w_bot, g_post, g_pre, rw_hi, rw_lo, r_bias)


def _swiglu(x, wg, wu, wd, gate):
    h = _silu(_dot(x, wg)) * _dot(x, wu)
    if gate is not None:
        h = h * gate
    return _dot(h.astype(BF16), wd)


def _moe_kernel(hf_ref, gates_ref, x1_ref, mod_ref, wg_ref, wu_ref, wd_ref, sg_ref, su_ref, sd_ref,
                gpost_ref, y_ref, acc_ref):
    nb, tt, d = hf_ref.shape
    m = nb * tt
    e = pl.program_id(2)
    x = hf_ref[...].reshape(m, d)

    @pl.when(e == 0)
    def _():
        acc_ref[...] = _swiglu(x, sg_ref[...], su_ref[...], sd_ref[...], None)

    gates = gates_ref[...].reshape(m, LANES)
    lane = lax.broadcasted_iota(jnp.int32, gates.shape, 1)
    gate = jnp.sum(jnp.where(lane == e, gates, 0.0), axis=1, keepdims=True)
    acc_ref[...] += _swiglu(x, wg_ref[0], wu_ref[0], wd_ref[0], gate)

    @pl.when(e == pl.num_programs(2) - 1)
    def _():
        z = _rms(acc_ref[...], gpost_ref[...]).reshape(nb, tt, d)
        y_ref[...] = x1_ref[...] + mod_ref[...][:, 5:6, :] * z


def _moe_blocks(n, t):
    if t >= MOE_TILE:
        assert t % MOE_TILE == 0
        return 1, MOE_TILE
    nb = MOE_TILE // t
    assert nb * t == MOE_TILE and n % nb == 0
    return nb, t


def _moe(hf, gates, x1, mod, wg, wu, wd, sg, su, sd, g_post):
    n, t, d = hf.shape
    nb, tt = _moe_blocks(n, t)
    blk = lambda w: pl.BlockSpec((nb, tt, w), lambda i, j, e: (i, j, 0))
    const = lambda shape: pl.BlockSpec(shape, lambda i, j, e: (0,) * len(shape))
    expert = lambda shape: pl.BlockSpec((1,) + shape, lambda i, j, e: (e, 0, 0))
    return pl.pallas_call(
        _moe_kernel,
        grid=(n // nb, t // tt, N_EXPERTS),
        in_specs=[blk(d), blk(LANES), blk(d), pl.BlockSpec((nb, 6, d), lambda i, j, e: (i, 0, 0)),
                  expert((d, D_EXPERT)), expert((d, D_EXPERT)), expert((D_EXPERT, d)),
                  const((d, D_EXPERT)), const((d, D_EXPERT)), const((D_EXPERT, d)), const((1, d))],
        out_specs=blk(d),
        out_shape=jax.ShapeDtypeStruct((n, t, d), F32),
        scratch_shapes=[pltpu.VMEM((nb * tt, d), F32)],
        compiler_params=_params("parallel", "parallel", "arbitrary"),
        name="moe",
    )(hf, gates, x1, mod, wg, wu, wd, sg, su, sd, g_post)


def _block_diag(w):
    g, bw, _ = w.shape
    eye = jnp.eye(g, dtype=w.dtype)
    return (eye[:, None, :, None] * w[:, :, None, :]).reshape(g * bw, g * bw)


def _prep_weights(p):
    d_main = 2 * WL + 3 * WA
    w_in = p["w_in"]
    rw_t = p["router_w"].T
    rw_hi = rw_t.astype(BF16)
    row = lambda v: v.reshape(1, -1)
    return dict(
        w_mod=p["w_mod"], b_mod=p["b_mod"],
        g_pre_mix=row(p["g_pre_mix"]), g_post_mix=row(p["g_post_mix"]),
        g_pre_ffn=row(p["g_pre_ffn"]), g_post_ffn=row(p["g_post_ffn"]),
        w_main=w_in[:, :d_main].astype(BF16),
        w_f=jnp.pad(w_in[:, d_main:], ((0, 0), (0, LANES - N_HEADS))).astype(BF16),
        b_f=jnp.pad(p["b_f"], (0, LANES - N_HEADS)).reshape(1, LANES),
        conv_w=p["conv_w"], conv_b=row(p["conv_b"]),
        wr_bd=_block_diag(p["w_r"]).astype(BF16), b_r=row(p["b_r"]),
        wi_bd=_block_diag(p["w_i"]).astype(BF16), b_i=row(p["b_i"]),
        lam=row(p["lru_lambda"]), g_lru=row(p["g_lru_out"]), g_att=row(p["g_att_out"]),
        w_top=p["w_out"][:WL].astype(BF16), w_bot=p["w_out"][WL:].astype(BF16),
        rw_hi=rw_hi, rw_lo=(rw_t - rw_hi.astype(F32)).astype(BF16),
        r_bias=p["router_bias"].reshape(N_EXPERTS, 1),
        wg=p["w_gate"].astype(BF16), wu=p["w_up"].astype(BF16), wd=p["w_down"].astype(BF16),
        sg=p["ws_gate"].astype(BF16), su=p["ws_up"].astype(BF16), sd=p["ws_down"].astype(BF16),
    )


def _layer(x, mod, conv0, h0, past, w):
    n, t, _ = x.shape
    xl, gy, qb, kb, vb, k, v, lf, lfp = _inproj(x, mod, w["g_pre_mix"], w["w_main"], w["w_f"], w["b_f"])
    lru_n, conv_new, h_new = _lru(xl, gy, conv0, h0.reshape(n, 1, WL), w["conv_w"], w["conv_b"],
                                  w["wr_bd"], w["b_r"], w["wi_bd"], w["b_i"], w["lam"], w["g_lru"])
    d_new = _cumsum_heads(lfp, min(t, ROW_TILE))
    if past is None:
        att = _att_prompt(qb, kb, vb, d_new)
    else:
        k_past, v_past, lf_past = past
        plen = k_past.shape[1]
        lf_past_p = jnp.pad(lf_past, ((0, 0), (0, 0), (0, LANES - N_HEADS)))
        d_past = _cumsum_heads(lf_past_p, ROW_TILE)
        att = _att_sample(qb, k_past.reshape(n, plen, WA), v_past.reshape(n, plen, WA), d_past, kb, vb, d_new)
    x1, hf, gates = _outproj(x, lru_n, att, mod, w["g_att"], w["w_top"], w["w_bot"], w["g_post_mix"],
                             w["g_pre_ffn"], w["rw_hi"], w["rw_lo"], w["r_bias"])
    y = _moe(hf, gates, x1, mod, w["wg"], w["wu"], w["wd"], w["sg"], w["su"], w["sd"], w["g_post_ffn"])
    state = (k.reshape(n, t, N_HEADS, HEAD_DIM), v.reshape(n, t, N_HEADS, HEAD_DIM), lf,
             conv_new, h_new.reshape(n, WL))
    return y, state


def kernel(x_prompt, x_sample, c_prompt, c_sample, cache_k, cache_v, cache_logf, state_conv, state_lru, w_mod, b_mod, g_pre_mix, g_post_mix, g_pre_ffn, g_post_ffn, w_in, conv_w, conv_b, w_r, b_r, w_i, b_i, lru_lambda, b_f, g_lru_out, g_att_out, w_out, router_w, router_bias, w_gate, w_up, w_down, ws_gate, ws_up, ws_down):
    names = ("w_mod", "b_mod", "g_pre_mix", "g_post_mix", "g_pre_ffn", "g_post_ffn", "w_in", "conv_w", "conv_b",
             "w_r", "b_r", "w_i", "b_i", "lru_lambda", "b_f", "g_lru_out", "g_att_out", "w_out", "router_w",
             "router_bias", "w_gate", "w_up", "w_down", "ws_gate", "ws_up", "ws_down")
    stacked = (w_mod, b_mod, g_pre_mix, g_post_mix, g_pre_ffn, g_post_ffn, w_in, conv_w, conv_b, w_r, b_r, w_i, b_i,
               lru_lambda, b_f, g_lru_out, g_att_out, w_out, router_w, router_bias, w_gate, w_up, w_down,
               ws_gate, ws_up, ws_down)
    depth = w_mod.shape[0]
    n_p, n_s = x_prompt.shape[0], x_sample.shape[0]
    yp, ys = x_prompt, x_sample
    st_p, st_s = [], []
    for l in range(depth):
        w = _prep_weights({k: v[l] for k, v in zip(names, stacked)})
        mod = _modulation(jnp.concatenate([c_prompt, c_sample], axis=0), w["w_mod"], w["b_mod"])
        mod = mod.reshape(n_p + n_s, 6, D_MODEL)
        conv0 = jnp.zeros((n_p, CONV_W - 1, WL), F32)
        h0 = jnp.zeros((n_p, WL), F32)
        yp, sp = _layer(yp, mod[:n_p], conv0, h0, None, w)
        ys, ss = _layer(ys, mod[n_p:], state_conv[l], state_lru[l],
                        (cache_k[l], cache_v[l], cache_logf[l]), w)
        st_p.append(sp)
        st_s.append(ss)
    stack = lambda sts, i: jnp.stack([s[i] for s in sts])
    return (yp, ys) + tuple(stack(st_p, i) for i in range(5)) + tuple(stack(st_s, i) for i in range(5))
```

```python
import functools

import jax
import jax.numpy as jnp
import numpy as np
from jax import lax
from jax.experimental import pallas as pl
from jax.experimental.pallas import tpu as pltpu

F32 = jnp.float32
BF16 = jnp.bfloat16

D_MODEL = 1024
WL = 512
WA = 512
N_HEADS = 8
HEAD_DIM = 64
N_PAIRS = N_HEADS // 2
PAIR_W = 2 * HEAD_DIM
LANES = 128
CONV_W = 4
LRU_BLOCKS = 8
LRU_C = 8.0
N_EXPERTS = 64
N_GROUPS = 8
GROUP_SIZE = N_EXPERTS // N_GROUPS
TOPK_GROUPS = 4
TOP_K = 8
D_EXPERT = 256
ROUTED_SCALE = 2.5
EPS = 1e-6
NEG_INF = float("-inf")

ROW_TILE = 512
ATT_TQ = 512
ATT_TK = 512
ATT_CHUNK = 64
SAMPLE_TK = 1024
MOE_TILE = 1024
VMEM_LIMIT = 56 * 1024 * 1024


def _params(*sem):
    return pltpu.CompilerParams(dimension_semantics=sem, vmem_limit_bytes=VMEM_LIMIT)


def _dot(a, b):
    return jnp.dot(a, b, preferred_element_type=F32)


def _dot_nt(a, b):
    return lax.dot_general(a, b, (((1,), (1,)), ((), ())), preferred_element_type=F32)


def _split3(x):
    hi = x.astype(BF16)
    r1 = x - hi.astype(F32)
    mid = r1.astype(BF16)
    lo = (r1 - mid.astype(F32)).astype(BF16)
    return hi, mid, lo


def _rms(x, g):
    return x * lax.rsqrt(jnp.mean(x * x, axis=-1, keepdims=True) + EPS) * g


def _sigmoid(x):
    return 1.0 / (1.0 + jnp.exp(-x))


def _silu(x):
    return x * _sigmoid(x)


def _gelu_tanh(x):
    return 0.5 * x * (1.0 + jnp.tanh(0.7978845608028654 * (x + 0.044715 * (x * x * x))))


def _log_sigmoid(x):
    return jnp.minimum(x, 0.0) - jnp.log1p(jnp.exp(-jnp.abs(x)))


def _seq_blocks(n, t):
    if t >= ROW_TILE:
        assert t % ROW_TILE == 0
        return 1, ROW_TILE
    nb = ROW_TILE // t
    assert nb * t == ROW_TILE and n % nb == 0
    return nb, t


def _mod_kernel(c_ref, w_ref, b_ref, o_ref):
    c = _silu(c_ref[...])
    c_hi = c.astype(BF16)
    c_lo = (c - c_hi.astype(F32)).astype(BF16)
    w = w_ref[...]
    w_hi = w.astype(BF16)
    w_lo = (w - w_hi.astype(F32)).astype(BF16)
    o_ref[...] = _dot(c_hi, w_hi) + _dot(c_lo, w_hi) + _dot(c_hi, w_lo) + b_ref[...]


def _modulation(c, w_mod, b_mod):
    rows = c.shape[0]
    n = -(-rows // 8) * 8
    c = jnp.pad(c, ((0, n - rows), (0, 0)))
    d6 = w_mod.shape[1]
    return pl.pallas_call(
        _mod_kernel,
        grid=(d6 // D_MODEL,),
        in_specs=[pl.BlockSpec((n, D_MODEL), lambda j: (0, 0)),
                  pl.BlockSpec((D_MODEL, D_MODEL), lambda j: (0, j)),
                  pl.BlockSpec((1, D_MODEL), lambda j: (0, j))],
        out_specs=pl.BlockSpec((n, D_MODEL), lambda j: (0, j)),
        out_shape=jax.ShapeDtypeStruct((n, d6), F32),
        compiler_params=_params("arbitrary"),
        name="modulation",
    )(c, w_mod, b_mod.reshape(1, d6))[:rows]


def _inproj_kernel(x_ref, mod_ref, g_ref, w_ref, wf_ref, bf_ref,
                   xl_ref, gy_ref, qb_ref, kb_ref, vb_ref, k_ref, v_ref, lf_ref, lfp_ref):
    nb, tt, d = x_ref.shape
    x = x_ref[...]
    mod = mod_ref[...]
    hn = _rms(x, g_ref[...]) * (1.0 + mod[:, 1:2, :]) + mod[:, 0:1, :]
    hb = hn.reshape(nb * tt, d).astype(BF16)

    def proj(col):
        return _dot(hb, w_ref[:, col * WL:(col + 1) * WL]).reshape(nb, tt, WL)

    xl_ref[...] = proj(0)
    gy_ref[...] = _gelu_tanh(proj(1))
    qb_ref[...] = (proj(2) * (HEAD_DIM ** -0.5)).astype(BF16)
    k = proj(3)
    k_ref[...] = k
    kb_ref[...] = k.astype(BF16)
    v = proj(4)
    v_ref[...] = v
    vb_ref[...] = v.astype(BF16)
    fl = _dot(hb, wf_ref[...]) + bf_ref[...]
    lane = lax.broadcasted_iota(jnp.int32, fl.shape, 1)
    lf = jnp.where(lane < N_HEADS, _log_sigmoid(fl), 0.0).reshape(nb, tt, LANES)
    lfp_ref[...] = lf
    lf_ref[...] = lf[:, :, :N_HEADS]


def _inproj(x, mod, g_pre, w_main, w_f, b_f):
    n, t, d = x.shape
    nb, tt = _seq_blocks(n, t)
    blk = lambda w: pl.BlockSpec((nb, tt, w), lambda i, j: (i, j, 0))
    const = lambda shape: pl.BlockSpec(shape, lambda i, j: (0,) * len(shape))
    f32 = lambda w: jax.ShapeDtypeStruct((n, t, w), F32)
    b16 = lambda w: jax.ShapeDtypeStruct((n, t, w), BF16)
    return pl.pallas_call(
        _inproj_kernel,
        grid=(n // nb, t // tt),
        in_specs=[blk(d),
                  pl.BlockSpec((nb, 6, d), lambda i, j: (i, 0, 0)),
                  const((1, d)), const(w_main.shape), const(w_f.shape), const((1, LANES))],
        out_specs=[blk(WL), blk(WL), blk(WA), blk(WA), blk(WA), blk(WA), blk(WA), blk(N_HEADS), blk(LANES)],
        out_shape=[f32(WL), f32(WL), b16(WA), b16(WA), b16(WA), f32(WA), f32(WA), f32(N_HEADS), f32(LANES)],
        compiler_params=_params("parallel", "arbitrary"),
        name="inproj",
    )(x, mod, g_pre, w_main, w_f, b_f)


def _aug_lane(h):
    return HEAD_DIM if h % 2 == 0 else 0


def _inproj_prompt_kernel(x_ref, mod_ref, g_ref, w_ref, wf_ref, bf_ref, place_ref,
                          xl_ref, gy_ref, k_ref, v_ref, lf_ref, qt_ref, ka_ref, vt_ref, bpre_ref, carry_ref):
    _, tt, d = x_ref.shape

    @pl.when(pl.program_id(1) == 0)
    def _():
        carry_ref[...] = jnp.zeros_like(carry_ref)

    mod = mod_ref[0]
    hb = (_rms(x_ref[0], g_ref[...]) * (1.0 + mod[1:2, :]) + mod[0:1, :]).astype(BF16)

    def proj(col):
        return _dot(hb, w_ref[:, col * WL:(col + 1) * WL])

    xl_ref[0] = proj(0)
    gy_ref[0] = _gelu_tanh(proj(1))
    q = proj(2) * (HEAD_DIM ** -0.5)
    k = proj(3)
    k_ref[0] = k
    v = proj(4)
    v_ref[0] = v
    fl = _dot(hb, wf_ref[...]) + bf_ref[...]
    lane = lax.broadcasted_iota(jnp.int32, (tt, LANES), 1)
    lf = jnp.where(lane < N_HEADS, _log_sigmoid(fl), 0.0)
    lf_ref[0] = lf[:, :N_HEADS]

    row = lax.broadcasted_iota(jnp.int32, (tt, tt), 0)
    col = lax.broadcasted_iota(jnp.int32, (tt, tt), 1)
    tril = (col <= row).astype(BF16)
    hi, mid, lo = _split3(lf)
    e = _dot(tril, hi) + _dot(tril, mid) + _dot(tril, lo)
    bpre_ref[0, 0] = carry_ref[...]
    carry_ref[...] += e[tt - 1:tt, :]
    e_hi, e_mid, e_lo = _split3(-e)
    aug_k = _dot(e_hi, place_ref[0]) + _dot(e_mid, place_ref[1]) + _dot(e_lo, place_ref[2])

    for h in range(N_HEADS):
        pair = slice((h // 2) * PAIR_W, (h // 2 + 1) * PAIR_W)
        dims = (lane < HEAD_DIM) if h % 2 == 0 else (lane >= HEAD_DIM)
        a0 = _aug_lane(h)
        ones3 = ((lane >= a0) & (lane < a0 + 3)).astype(F32)
        one1 = (lane == a0).astype(F32)
        qt_ref[0, h] = (jnp.where(dims, q[:, pair], 0.0) + ones3).T.astype(BF16)
        ka_ref[0, h] = (jnp.where(dims, k[:, pair], 0.0) + aug_k[:, h * LANES:(h + 1) * LANES]).astype(BF16)
        vt_ref[0, h] = (jnp.where(dims, v[:, pair], 0.0) + one1).T.astype(BF16)


def _placement():
    pl_mat = np.zeros((3, LANES, N_HEADS * LANES), np.float32)
    for p in range(3):
        for h in range(N_HEADS):
            pl_mat[p, h, h * LANES + _aug_lane(h) + p] = 1.0
    return jnp.asarray(pl_mat, BF16)


def _inproj_prompt(x, mod, g_pre, w_main, w_f, b_f):
    n, t, d = x.shape
    tt = ROW_TILE
    assert t % tt == 0
    nblk = t // tt
    blk = lambda w: pl.BlockSpec((1, tt, w), lambda i, j: (i, j, 0))
    const = lambda shape: pl.BlockSpec(shape, lambda i, j: (0,) * len(shape))
    f32 = lambda w: jax.ShapeDtypeStruct((n, t, w), F32)
    place = _placement()
    return pl.pallas_call(
        _inproj_prompt_kernel,
        grid=(n, nblk),
        in_specs=[blk(d), pl.BlockSpec((1, 6, d), lambda i, j: (i, 0, 0)),
                  const((1, d)), const(w_main.shape), const(w_f.shape), const((1, LANES)), const(place.shape)],
        out_specs=[blk(WL), blk(WL), blk(WA), blk(WA), blk(N_HEADS),
                   pl.BlockSpec((1, N_HEADS, LANES, tt), lambda i, j: (i, 0, 0, j)),
                   pl.BlockSpec((1, N_HEADS, tt, LANES), lambda i, j: (i, 0, j, 0)),
                   pl.BlockSpec((1, N_HEADS, LANES, tt), lambda i, j: (i, 0, 0, j)),
                   pl.BlockSpec((1, 1, 1, LANES), lambda i, j: (i, j, 0, 0))],
        out_shape=[f32(WL), f32(WL), f32(WA), f32(WA), f32(N_HEADS),
                   jax.ShapeDtypeStruct((n, N_HEADS, LANES, t), BF16),
                   jax.ShapeDtypeStruct((n, N_HEADS, t, LANES), BF16),
                   jax.ShapeDtypeStruct((n, N_HEADS, LANES, t), BF16),
                   jax.ShapeDtypeStruct((n, nblk, 1, LANES), F32)],
        scratch_shapes=[pltpu.VMEM((1, LANES), F32)],
        compiler_params=_params("parallel", "arbitrary"),
        name="inproj_prompt",
    )(x, mod, g_pre, w_main, w_f, b_f, place)


def _expm1_neg(x):
    poly = x * (1.0 + x * (0.5 + x * (1.0 / 6.0 + x * (1.0 / 24.0 + x * (1.0 / 120.0)))))
    return jnp.where(x > -0.1, poly, jnp.exp(x) - 1.0)


def _lru_kernel(xl_ref, gy_ref, conv0_ref, h0_ref, cw_ref, cb_ref, wr_ref, br_ref, wi_ref, bi_ref,
                lam_ref, g_ref, out_ref, conv_ref, hlast_ref, tail_ref, carry_ref):
    nb, tt, w = xl_ref.shape
    j = pl.program_id(1)

    @pl.when(j == 0)
    def _():
        tail_ref[:, 8 - (CONV_W - 1):, :] = conv0_ref[...]
        carry_ref[...] = h0_ref[...]

    xl = xl_ref[...]
    xpad = jnp.concatenate([tail_ref[...], xl], axis=1)
    cw = cw_ref[...]
    xc = jnp.zeros_like(xl) + cb_ref[...]
    for k in range(CONV_W):
        off = 8 - (CONV_W - 1) + k
        xc = xc + xpad[:, off:off + tt, :] * cw[k:k + 1, :]
    conv_ref[...] = xpad[:, tt + 8 - (CONV_W - 1):, :]
    tail_ref[...] = xpad[:, tt:, :]

    m = nb * tt
    xf = xc.reshape(m, w)
    xb = xf.astype(BF16)
    r = _sigmoid(_dot(xb, wr_ref[...]) + br_ref[...])
    gi = _sigmoid(_dot(xb, wi_ref[...]) + bi_ref[...])
    lam = lam_ref[...]
    softplus = jnp.maximum(-lam, 0.0) + jnp.log1p(jnp.exp(-jnp.abs(lam)))
    log_a = (-LRU_C) * r * softplus
    a = jnp.exp(log_a)
    b = jnp.sqrt(-_expm1_neg(2.0 * log_a)) * (gi * xf)

    pos = lax.broadcasted_iota(jnp.int32, (m, w), 0) % tt
    d = 1
    while d < tt:
        keep = pos >= d
        a_prev = jnp.where(keep, pltpu.roll(a, d, 0), 1.0)
        b_prev = jnp.where(keep, pltpu.roll(b, d, 0), 0.0)
        b = a * b_prev + b
        a = a * a_prev
        d *= 2
    h = a.reshape(nb, tt, w) * carry_ref[...] + b.reshape(nb, tt, w)
    h_last = h[:, tt - 1:tt, :]
    carry_ref[...] = h_last
    hlast_ref[...] = h_last
    out_ref[...] = _rms(h * gy_ref[...], g_ref[...]).astype(BF16)


def _lru(xl, gy, conv0, h0, conv_w, conv_b, wr_bd, b_r, wi_bd, b_i, lam, g_lru):
    n, t, w = xl.shape
    nb, tt = _seq_blocks(n, t)
    blk = pl.BlockSpec((nb, tt, w), lambda i, j: (i, j, 0))
    per_seq = lambda rows: pl.BlockSpec((nb, rows, w), lambda i, j: (i, 0, 0))
    const = lambda shape: pl.BlockSpec(shape, lambda i, j: (0,) * len(shape))
    row = const((1, w))
    return pl.pallas_call(
        _lru_kernel,
        grid=(n // nb, t // tt),
        in_specs=[blk, blk, per_seq(CONV_W - 1), per_seq(1),
                  const((CONV_W, w)), row, const((w, w)), row, const((w, w)), row, row, row],
        out_specs=[blk, per_seq(CONV_W - 1), per_seq(1)],
        out_shape=[jax.ShapeDtypeStruct((n, t, w), BF16),
                   jax.ShapeDtypeStruct((n, CONV_W - 1, w), F32),
                   jax.ShapeDtypeStruct((n, 1, w), F32)],
        scratch_shapes=[pltpu.VMEM((nb, 8, w), F32), pltpu.VMEM((nb, 1, w), F32)],
        compiler_params=_params("parallel", "arbitrary"),
        name="rglru",
    )(xl, gy, conv0, h0, conv_w, conv_b, wr_bd, b_r, wi_bd, b_i, lam, g_lru)


def _cumsum_kernel(lf_ref, o_ref, carry_ref):
    tb = lf_ref.shape[1]

    @pl.when(pl.program_id(1) == 0)
    def _():
        carry_ref[...] = jnp.zeros_like(carry_ref)

    row = lax.broadcasted_iota(jnp.int32, (tb, tb), 0)
    col = lax.broadcasted_iota(jnp.int32, (tb, tb), 1)
    tril = (col <= row).astype(BF16)
    hi, mid, lo = _split3(lf_ref[0])
    d = _dot(tril, hi) + _dot(tril, mid) + _dot(tril, lo) + carry_ref[...]
    carry_ref[...] = d[tb - 1:tb, :]
    o_ref[0] = d.T[:N_HEADS, :]


def _cumsum_heads(lfp, tb):
    n, t, _ = lfp.shape
    return pl.pallas_call(
        _cumsum_kernel,
        grid=(n, t // tb),
        in_specs=[pl.BlockSpec((1, tb, LANES), lambda i, j: (i, j, 0))],
        out_specs=pl.BlockSpec((1, N_HEADS, tb), lambda i, j: (i, 0, j)),
        out_shape=jax.ShapeDtypeStruct((n, N_HEADS, t), F32),
        scratch_shapes=[pltpu.VMEM((1, LANES), F32)],
        compiler_params=_params("parallel", "arbitrary"),
        name="logf_cumsum",
    )(lfp)


def _pair_masks(x):
    lane = lax.broadcasted_iota(jnp.int32, x.shape, 1)
    zero = jnp.zeros_like(x)
    return jnp.where(lane < HEAD_DIM, x, zero), jnp.where(lane >= HEAD_DIM, x, zero)


def _online_update(s, m_prev, l_prev):
    m_new = jnp.maximum(m_prev, jnp.max(s, axis=1, keepdims=True))
    alpha = jnp.exp(m_prev - m_new)
    p = jnp.exp(s - m_new)
    l_new = alpha * l_prev + jnp.sum(p, axis=1, keepdims=True)
    return p, alpha, m_new, l_new


def _pair_step(q_a, q_b, k, v_a, v_b, bias_a, bias_b, mask, state):
    m_a, l_a, m_b, l_b, acc = state
    s_a = _dot_nt(q_a, k) + bias_a
    s_b = _dot_nt(q_b, k) + bias_b
    if mask is not None:
        s_a = jnp.where(mask, s_a, NEG_INF)
        s_b = jnp.where(mask, s_b, NEG_INF)
    p_a, al_a, m_a, l_a = _online_update(s_a, m_a, l_a)
    p_b, al_b, m_b, l_b = _online_update(s_b, m_b, l_b)
    lane = lax.broadcasted_iota(jnp.int32, acc.shape, 1)
    alpha = jnp.where(lane < HEAD_DIM, al_a, al_b)
    acc = alpha * acc + (_dot(p_a.astype(BF16), v_a) + _dot(p_b.astype(BF16), v_b))
    return m_a, l_a, m_b, l_b, acc


def _pair_finish(state):
    m_a, l_a, m_b, l_b, acc = state
    lane = lax.broadcasted_iota(jnp.int32, acc.shape, 1)
    return acc / jnp.where(lane < HEAD_DIM, l_a, l_b)


def _att_prompt_kernel(bpre_ref, qt_ref, ka_ref, vt_ref, o_ref, s_ref, p_ref, acc_ref):
    tq = qt_ref.shape[2]
    tk = ATT_TK
    nblk = ka_ref.shape[1] // tk
    b, hp, i = pl.program_id(0), pl.program_id(1), pl.program_id(2)
    q0 = i * tq
    jd = q0 // tk
    kpos = lax.broadcasted_iota(jnp.int32, (tk, tq), 0)
    qpos = lax.broadcasted_iota(jnp.int32, (tk, tq), 1)
    rows = lax.broadcasted_iota(jnp.int32, (LANES, tq), 0)
    base = [((b * N_PAIRS + hp) * 2 + hh) * nblk for hh in range(2)]

    def scores(j, masked):
        start = pl.multiple_of(j * tk, tk)
        col_max = []
        for hh in range(2):
            s = _dot(ka_ref[hh, pl.ds(start, tk), :], qt_ref[hh])
            if masked:
                s = jnp.where(kpos + start <= qpos + q0, s, NEG_INF)
            s_ref[hh] = s
            col_max.append(jnp.max(s, axis=0, keepdims=True))
        return tuple(col_max)

    def softmax_pv(j, col_max, m):
        start = pl.multiple_of(j * tk, tk)
        m_out = []
        for hh in range(2):
            c = bpre_ref[base[hh] + jd] - bpre_ref[base[hh] + j]
            m_new = jnp.maximum(m[hh], col_max[hh] + c)
            alpha = jnp.exp(m[hh] - m_new)
            shift = m_new - c
            for ch in range(tk // ATT_CHUNK):
                sl = slice(ch * ATT_CHUNK, (ch + 1) * ATT_CHUNK)
                p_ref[hh, sl, :] = jnp.exp(s_ref[hh, sl, :] - shift).astype(BF16)
            m_out.append((m_new, alpha))
        return tuple(m_out), start

    def accumulate(m_alpha, start):
        for hh in range(2):
            pv = _dot(vt_ref[hh, :, pl.ds(start, tk)], p_ref[hh])
            acc_ref[hh] = m_alpha[hh][1] * acc_ref[hh] + pv
        return tuple(ma[0] for ma in m_alpha)

    def step(j, next_masked, carry):
        col_max, m = carry
        m_alpha, start = softmax_pv(j, col_max, m)
        col_max_next = scores(j + 1, next_masked)
        return col_max_next, accumulate(m_alpha, start)

    acc_ref[...] = jnp.zeros_like(acc_ref)
    neg = jnp.full((1, tq), NEG_INF, F32)
    carry = (scores(0, True), (neg, neg))
    carry = lax.fori_loop(0, jd - 1, lambda j, cr: step(j, False, cr), carry)
    carry = lax.cond(jd > 0, lambda cr: step(jd - 1, True, cr), lambda cr: cr, carry)
    col_max, m = carry
    m_alpha, start = softmax_pv(jd, col_max, m)
    accumulate(m_alpha, start)

    acc_a, acc_b = acc_ref[0], acc_ref[1]
    out_a = acc_a / acc_a[_aug_lane(0):_aug_lane(0) + 1, :]
    out_b = acc_b / acc_b[_aug_lane(1):_aug_lane(1) + 1, :]
    o_ref[0] = jnp.where(rows < HEAD_DIM, out_a, out_b).T


def _att_prompt(qt, ka, vt, bpre):
    n, _, _, t = qt.shape
    assert t % ATT_TQ == 0 and ATT_TK % ATT_TQ == 0 and ATT_TK == ROW_TILE
    pair = lambda q: (N_PAIRS, 2) + q.shape[2:]
    qt, ka, vt = (a.reshape((n,) + pair(a)) for a in (qt, ka, vt))
    bflat = jnp.transpose(bpre[:, :, 0, :N_HEADS], (0, 2, 1)).reshape(-1)
    return pl.pallas_call(
        _att_prompt_kernel,
        grid=(n, N_PAIRS, t // ATT_TQ),
        in_specs=[pl.BlockSpec(memory_space=pltpu.SMEM),
                  pl.BlockSpec((None, None, 2, LANES, ATT_TQ), lambda b, h, i: (b, h, 0, 0, i)),
                  pl.BlockSpec((None, None, 2, t, LANES), lambda b, h, i: (b, h, 0, 0, 0),
                               pipeline_mode=pl.Buffered(1)),
                  pl.BlockSpec((None, None, 2, LANES, t), lambda b, h, i: (b, h, 0, 0, 0),
                               pipeline_mode=pl.Buffered(1))],
        out_specs=pl.BlockSpec((1, ATT_TQ, PAIR_W), lambda b, h, i: (b, i, h)),
        out_shape=jax.ShapeDtypeStruct((n, t, WA), F32),
        scratch_shapes=[pltpu.VMEM((2, ATT_TK, ATT_TQ), F32), pltpu.VMEM((2, ATT_TK, ATT_TQ), BF16),
                        pltpu.VMEM((2, LANES, ATT_TQ), F32)],
        compiler_params=_params("parallel", "parallel", "arbitrary"),
        name="att_prompt",
    )(bflat, qt, ka, vt)


def _att_sample_kernel(q_ref, kp_ref, vp_ref, dp_ref, dend_ref, kn_ref, vn_ref, dn_ref, o_ref,
                       m_ref, l_ref, acc_ref):
    t = q_ref.shape[1]
    j = pl.program_id(1)

    @pl.when(j == 0)
    def _():
        m_ref[...] = jnp.full_like(m_ref, NEG_INF)
        l_ref[...] = jnp.zeros_like(l_ref)
        acc_ref[...] = jnp.zeros_like(acc_ref)

    def pair(h, k, v, bias, mask):
        sl = slice(h * PAIR_W, (h + 1) * PAIR_W)
        q_a, q_b = _pair_masks(q_ref[0, :, sl])
        v_a, v_b = _pair_masks(v[:, sl].astype(BF16))
        state = (m_ref[2 * h], l_ref[2 * h], m_ref[2 * h + 1], l_ref[2 * h + 1], acc_ref[:, sl])
        state = _pair_step(q_a, q_b, k[:, sl].astype(BF16), v_a, v_b,
                           bias[2 * h:2 * h + 1], bias[2 * h + 1:2 * h + 2], mask, state)
        m_ref[2 * h], l_ref[2 * h], m_ref[2 * h + 1], l_ref[2 * h + 1], acc_ref[:, sl] = state
        return state

    bias_past = dend_ref[0] - dp_ref[0]
    for h in range(N_PAIRS):
        pair(h, kp_ref[0], vp_ref[0], bias_past, None)

    @pl.when(j == pl.num_programs(1) - 1)
    def _():
        bias_new = -dn_ref[0]
        qpos = lax.broadcasted_iota(jnp.int32, (t, t), 0)
        kpos = lax.broadcasted_iota(jnp.int32, (t, t), 1)
        for h in range(N_PAIRS):
            state = pair(h, kn_ref[0], vn_ref[0], bias_new, kpos <= qpos)
            o_ref[0, :, h * PAIR_W:(h + 1) * PAIR_W] = _pair_finish(state)


def _att_sample(qb, k_past, v_past, d_past, kb_new, vb_new, d_new):
    n, t, _ = qb.shape
    past = k_past.shape[1]
    assert past % SAMPLE_TK == 0
    d_end = d_past[:, :, past - 1:]
    new = lambda dt: pl.BlockSpec((1, t, WA), lambda b, j: (b, 0, 0))
    return pl.pallas_call(
        _att_sample_kernel,
        grid=(n, past // SAMPLE_TK),
        in_specs=[new(BF16),
                  pl.BlockSpec((1, SAMPLE_TK, WA), lambda b, j: (b, j, 0)),
                  pl.BlockSpec((1, SAMPLE_TK, WA), lambda b, j: (b, j, 0)),
                  pl.BlockSpec((1, N_HEADS, SAMPLE_TK), lambda b, j: (b, 0, j)),
                  pl.BlockSpec((1, N_HEADS, 1), lambda b, j: (b, 0, 0)),
                  new(BF16), new(BF16),
                  pl.BlockSpec((1, N_HEADS, t), lambda b, j: (b, 0, 0))],
        out_specs=pl.BlockSpec((1, t, WA), lambda b, j: (b, 0, 0)),
        out_shape=jax.ShapeDtypeStruct((n, t, WA), F32),
        scratch_shapes=[pltpu.VMEM((N_HEADS, t, 1), F32), pltpu.VMEM((N_HEADS, t, 1), F32),
                        pltpu.VMEM((t, WA), F32)],
        compiler_params=_params("parallel", "arbitrary"),
        name="att_sample",
    )(qb, k_past, v_past, d_past, d_end, kb_new, vb_new, d_new)


def _first_index_of_max(x, axis):
    mx = jnp.max(x, axis=axis, keepdims=True)
    idx = lax.broadcasted_iota(jnp.int32, x.shape, axis)
    first = jnp.min(jnp.where(x == mx, idx, x.shape[axis]), axis=axis, keepdims=True)
    return mx, idx == first


def _route(s, bias):
    m = s.shape[1]
    sb = (s + bias).reshape(N_GROUPS, GROUP_SIZE, m)
    top1, is_top1 = _first_index_of_max(sb, 1)
    top2 = jnp.max(jnp.where(is_top1, NEG_INF, sb), axis=1, keepdims=True)
    grp = (top1 + top2).reshape(N_GROUPS, m)
    gi = lax.broadcasted_iota(jnp.int32, (N_GROUPS, N_GROUPS, m), 0)
    gj = lax.broadcasted_iota(jnp.int32, (N_GROUPS, N_GROUPS, m), 1)
    other, mine = grp[None, :, :], grp[:, None, :]
    beats = (other > mine) | ((other == mine) & (gj < gi))
    g_rank = jnp.sum(beats.astype(jnp.int32), axis=1)
    g_keep = (g_rank < TOPK_GROUPS)[:, None, :]
    cand = jnp.where(g_keep, sb, NEG_INF).reshape(N_EXPERTS, m)
    sel = jnp.zeros((N_EXPERTS, m), jnp.bool_)
    for _ in range(TOP_K):
        _, pick = _first_index_of_max(cand, 0)
        sel = sel | pick
        cand = jnp.where(pick, NEG_INF, cand)
    w = jnp.where(sel, s, 0.0)
    return w / jnp.sum(w, axis=0, keepdims=True) * ROUTED_SCALE


def _outproj_kernel(x_ref, lru_ref, att_ref, mod_ref, gatt_ref, wtop_ref, wbot_ref, gpost_ref, gpre_ref,
                    rwh_ref, rwl_ref, rb_ref, x1_ref, hf_ref, gates_ref):
    nb, tt, d = x_ref.shape
    m = nb * tt
    mod = mod_ref[...]
    att_n = _rms(att_ref[...], gatt_ref[...]).reshape(m, WA).astype(BF16)
    mix = _dot(lru_ref[...].reshape(m, WL), wtop_ref[...]) + _dot(att_n, wbot_ref[...])
    x1 = x_ref[...] + mod[:, 2:3, :] * _rms(mix, gpost_ref[...]).reshape(nb, tt, d)
    x1_ref[...] = x1
    hf = (_rms(x1, gpre_ref[...]) * (1.0 + mod[:, 4:5, :]) + mod[:, 3:4, :]).reshape(m, d)
    hf_hi = hf.astype(BF16)
    hf_ref[...] = hf_hi.reshape(nb, tt, d)
    hf_lo = (hf - hf_hi.astype(F32)).astype(BF16)
    rwh = rwh_ref[...]
    logits = _dot_nt(rwh, hf_hi) + _dot_nt(rwh, hf_lo) + _dot_nt(rwl_ref[...], hf_hi)
    gates = _route(_sigmoid(logits), rb_ref[...])
    gates = jnp.concatenate([gates, jnp.zeros((LANES - N_EXPERTS, m), F32)], axis=0)
    gates_ref[...] = gates.T.reshape(nb, tt, LANES)


def _outproj(x, lru_n, att, mod, g_att, w_top, w_bot, g_post, g_pre, rw_hi, rw_lo, r_bias):
    n, t, d = x.shape
    nb, tt = _seq_blocks(n, t)
    blk = lambda w: pl.BlockSpec((nb, tt, w), lambda i, j: (i, j, 0))
    const = lambda shape: pl.BlockSpec(shape, lambda i, j: (0,) * len(shape))
    return pl.pallas_call(
        _outproj_kernel,
        grid=(n // nb, t // tt),
        in_specs=[blk(d), blk(WL), blk(WA), pl.BlockSpec((nb, 6, d), lambda i, j: (i, 0, 0)),
                  const((1, WA)), const((WL, d)), const((WA, d)), const((1, d)), const((1, d)),
                  const((N_EXPERTS, d)), const((N_EXPERTS, d)), const((N_EXPERTS, 1))],
        out_specs=[blk(d), blk(d), blk(LANES)],
        out_shape=[jax.ShapeDtypeStruct((n, t, d), F32), jax.ShapeDtypeStruct((n, t, d), BF16),
                   jax.ShapeDtypeStruct((n, t, LANES), F32)],
        compiler_params=_params("parallel", "parallel"),
        name="outproj_router",
    )(x, lru_n, att, mod, g_att, w_top, w_bot, g_post, g_pre, rw_hi, rw_lo, r_bias)


def _swiglu(x, wg, wu, wd, gate):
    h = _silu(_dot(x, wg)) * _dot(x, wu)
    if gate is not None:
        h = h * gate
    return _dot(h.astype(BF16), wd)


def _moe_kernel(hf_ref, gates_ref, x1_ref, mod_ref, wg_ref, wu_ref, wd_ref, sg_ref, su_ref, sd_ref,
                gpost_ref, y_ref, acc_ref):
    nb, tt, d = hf_ref.shape
    m = nb * tt
    e = pl.program_id(2)
    x = hf_ref[...].reshape(m, d)

    @pl.when(e == 0)
    def _():
        acc_ref[...] = _swiglu(x, sg_ref[...], su_ref[...], sd_ref[...], None)

    gates = gates_ref[...].reshape(m, LANES)
    lane = lax.broadcasted_iota(jnp.int32, gates.shape, 1)
    gate = jnp.sum(jnp.where(lane == e, gates, 0.0), axis=1, keepdims=True)
    acc_ref[...] += _swiglu(x, wg_ref[0], wu_ref[0], wd_ref[0], gate)

    @pl.when(e == pl.num_programs(2) - 1)
    def _():
        z = _rms(acc_ref[...], gpost_ref[...]).reshape(nb, tt, d)
        y_ref[...] = x1_ref[...] + mod_ref[...][:, 5:6, :] * z


def _moe_blocks(n, t):
    if t >= MOE_TILE:
        assert t % MOE_TILE == 0
        return 1, MOE_TILE
    nb = MOE_TILE // t
    assert nb * t == MOE_TILE and n % nb == 0
    return nb, t


def _moe(hf, gates, x1, mod, wg, wu, wd, sg, su, sd, g_post):
    n, t, d = hf.shape
    nb, tt = _moe_blocks(n, t)
    blk = lambda w: pl.BlockSpec((nb, tt, w), lambda i, j, e: (i, j, 0))
    const = lambda shape: pl.BlockSpec(shape, lambda i, j, e: (0,) * len(shape))
    expert = lambda shape: pl.BlockSpec((1,) + shape, lambda i, j, e: (e, 0, 0))
    return pl.pallas_call(
        _moe_kernel,
        grid=(n // nb, t // tt, N_EXPERTS),
        in_specs=[blk(d), blk(LANES), blk(d), pl.BlockSpec((nb, 6, d), lambda i, j, e: (i, 0, 0)),
                  expert((d, D_EXPERT)), expert((d, D_EXPERT)), expert((D_EXPERT, d)),
                  const((d, D_EXPERT)), const((d, D_EXPERT)), const((D_EXPERT, d)), const((1, d))],
        out_specs=blk(d),
        out_shape=jax.ShapeDtypeStruct((n, t, d), F32),
        scratch_shapes=[pltpu.VMEM((nb * tt, d), F32)],
        compiler_params=_params("parallel", "parallel", "arbitrary"),
        name="moe",
    )(hf, gates, x1, mod, wg, wu, wd, sg, su, sd, g_post)


def _block_diag(w):
    g, bw, _ = w.shape
    eye = jnp.eye(g, dtype=w.dtype)
    return (eye[:, None, :, None] * w[:, :, None, :]).reshape(g * bw, g * bw)


def _prep_weights(p):
    d_main = 2 * WL + 3 * WA
    w_in = p["w_in"]
    rw_t = p["router_w"].T
    rw_hi = rw_t.astype(BF16)
    row = lambda v: v.reshape(1, -1)
    return dict(
        w_mod=p["w_mod"], b_mod=p["b_mod"],
        g_pre_mix=row(p["g_pre_mix"]), g_post_mix=row(p["g_post_mix"]),
        g_pre_ffn=row(p["g_pre_ffn"]), g_post_ffn=row(p["g_post_ffn"]),
        w_main=w_in[:, :d_main].astype(BF16),
        w_f=jnp.pad(w_in[:, d_main:], ((0, 0), (0, LANES - N_HEADS))).astype(BF16),
        b_f=jnp.pad(p["b_f"], (0, LANES - N_HEADS)).reshape(1, LANES),
        conv_w=p["conv_w"], conv_b=row(p["conv_b"]),
        wr_bd=_block_diag(p["w_r"]).astype(BF16), b_r=row(p["b_r"]),
        wi_bd=_block_diag(p["w_i"]).astype(BF16), b_i=row(p["b_i"]),
        lam=row(p["lru_lambda"]), g_lru=row(p["g_lru_out"]), g_att=row(p["g_att_out"]),
        w_top=p["w_out"][:WL].astype(BF16), w_bot=p["w_out"][WL:].astype(BF16),
        rw_hi=rw_hi, rw_lo=(rw_t - rw_hi.astype(F32)).astype(BF16),
        r_bias=p["router_bias"].reshape(N_EXPERTS, 1),
        wg=p["w_gate"].astype(BF16), wu=p["w_up"].astype(BF16), wd=p["w_down"].astype(BF16),
        sg=p["ws_gate"].astype(BF16), su=p["ws_up"].astype(BF16), sd=p["ws_down"].astype(BF16),
    )


def _layer(x, mod, conv0, h0, past, w):
    n, t, _ = x.shape
    proj_args = (x, mod, w["g_pre_mix"], w["w_main"], w["w_f"], w["b_f"])
    if past is None:
        xl, gy, k, v, lf, qt, ka, vt, bpre = _inproj_prompt(*proj_args)
        att = _att_prompt(qt, ka, vt, bpre)
    else:
        xl, gy, qb, kb, vb, k, v, lf, lfp = _inproj(*proj_args)
        d_new = _cumsum_heads(lfp, t)
        k_past, v_past, lf_past = past
        plen = k_past.shape[1]
        lf_past_p = jnp.pad(lf_past, ((0, 0), (0, 0), (0, LANES - N_HEADS)))
        d_past = _cumsum_heads(lf_past_p, ROW_TILE)
        att = _att_sample(qb, k_past.reshape(n, plen, WA), v_past.reshape(n, plen, WA), d_past, kb, vb, d_new)
    lru_n, conv_new, h_new = _lru(xl, gy, conv0, h0.reshape(n, 1, WL), w["conv_w"], w["conv_b"],
                                  w["wr_bd"], w["b_r"], w["wi_bd"], w["b_i"], w["lam"], w["g_lru"])
    x1, hf, gates = _outproj(x, lru_n, att, mod, w["g_att"], w["w_top"], w["w_bot"], w["g_post_mix"],
                             w["g_pre_ffn"], w["rw_hi"], w["rw_lo"], w["r_bias"])
    y = _moe(hf, gates, x1, mod, w["wg"], w["wu"], w["wd"], w["sg"], w["su"], w["sd"], w["g_post_ffn"])
    state = (k.reshape(n, t, N_HEADS, HEAD_DIM), v.reshape(n, t, N_HEADS, HEAD_DIM), lf,
             conv_new, h_new.reshape(n, WL))
    return y, state


def kernel(x_prompt, x_sample, c_prompt, c_sample, cache_k, cache_v, cache_logf, state_conv, state_lru, w_mod, b_mod, g_pre_mix, g_post_mix, g_pre_ffn, g_post_ffn, w_in, conv_w, conv_b, w_r, b_r, w_i, b_i, lru_lambda, b_f, g_lru_out, g_att_out, w_out, router_w, router_bias, w_gate, w_up, w_down, ws_gate, ws_up, ws_down):
    names = ("w_mod", "b_mod", "g_pre_mix", "g_post_mix", "g_pre_ffn", "g_post_ffn", "w_in", "conv_w", "conv_b",
             "w_r", "b_r", "w_i", "b_i", "lru_lambda", "b_f", "g_lru_out", "g_att_out", "w_out", "router_w",
             "router_bias", "w_gate", "w_up", "w_down", "ws_gate", "ws_up", "ws_down")
    stacked = (w_mod, b_mod, g_pre_mix, g_post_mix, g_pre_ffn, g_post_ffn, w_in, conv_w, conv_b, w_r, b_r, w_i, b_i,
               lru_lambda, b_f, g_lru_out, g_att_out, w_out, router_w, router_bias, w_gate, w_up, w_down,
               ws_gate, ws_up, ws_down)
    depth = w_mod.shape[0]
    n_p, n_s = x_prompt.shape[0], x_sample.shape[0]
    yp, ys = x_prompt, x_sample
    st_p, st_s = [], []
    for l in range(depth):
        w = _prep_weights({k: v[l] for k, v in zip(names, stacked)})
        mod = _modulation(jnp.concatenate([c_prompt, c_sample], axis=0), w["w_mod"], w["b_mod"])
        mod = mod.reshape(n_p + n_s, 6, D_MODEL)
        conv0 = jnp.zeros((n_p, CONV_W - 1, WL), F32)
        h0 = jnp.zeros((n_p, WL), F32)
        yp, sp = _layer(yp, mod[:n_p], conv0, h0, None, w)
        ys, ss = _layer(ys, mod[n_p:], state_conv[l], state_lru[l],
                        (cache_k[l], cache_v[l], cache_logf[l]), w)
        st_p.append(sp)
        st_s.append(ss)
    stack = lambda sts, i: jnp.stack([s[i] for s in sts])
    return (yp, ys) + tuple(stack(st_p, i) for i in range(5)) + tuple(stack(st_s, i) for i in range(5))
```

```python
import functools

import jax
import jax.numpy as jnp
import numpy as np
from jax import lax
from jax.experimental import pallas as pl
from jax.experimental.pallas import tpu as pltpu

F32 = jnp.float32
BF16 = jnp.bfloat16

D_MODEL = 1024
WL = 512
WA = 512
N_HEADS = 8
HEAD_DIM = 64
N_PAIRS = N_HEADS // 2
PAIR_W = 2 * HEAD_DIM
LANES = 128
CONV_W = 4
LRU_BLOCKS = 8
LRU_C = 8.0
N_EXPERTS = 64
N_GROUPS = 8
GROUP_SIZE = N_EXPERTS // N_GROUPS
TOPK_GROUPS = 4
TOP_K = 8
D_EXPERT = 256
ROUTED_SCALE = 2.5
EPS = 1e-6
NEG_INF = float("-inf")

ROW_TILE = 512
ATT_TQ = 512
ATT_TK = 512
ATT_CHUNK = 64
SAMPLE_TK = 1024
EXPERT_TILE = 512
VMEM_LIMIT = 56 * 1024 * 1024


def _params(*sem):
    return pltpu.CompilerParams(dimension_semantics=sem, vmem_limit_bytes=VMEM_LIMIT)


def _dot(a, b):
    return jnp.dot(a, b, preferred_element_type=F32)


def _dot_nt(a, b):
    return lax.dot_general(a, b, (((1,), (1,)), ((), ())), preferred_element_type=F32)


def _split3(x):
    hi = x.astype(BF16)
    r1 = x - hi.astype(F32)
    mid = r1.astype(BF16)
    lo = (r1 - mid.astype(F32)).astype(BF16)
    return hi, mid, lo


def _rms(x, g):
    return x * lax.rsqrt(jnp.mean(x * x, axis=-1, keepdims=True) + EPS) * g


def _sigmoid(x):
    return 1.0 / (1.0 + jnp.exp(-x))


def _silu(x):
    return x * _sigmoid(x)


def _gelu_tanh(x):
    return 0.5 * x * (1.0 + jnp.tanh(0.7978845608028654 * (x + 0.044715 * (x * x * x))))


def _log_sigmoid(x):
    return jnp.minimum(x, 0.0) - jnp.log1p(jnp.exp(-jnp.abs(x)))


def _seq_blocks(n, t):
    if t >= ROW_TILE:
        assert t % ROW_TILE == 0
        return 1, ROW_TILE
    nb = ROW_TILE // t
    assert nb * t == ROW_TILE and n % nb == 0
    return nb, t


def _mod_kernel(c_ref, w_ref, b_ref, o_ref):
    c = _silu(c_ref[...])
    c_hi = c.astype(BF16)
    c_lo = (c - c_hi.astype(F32)).astype(BF16)
    w = w_ref[...]
    w_hi = w.astype(BF16)
    w_lo = (w - w_hi.astype(F32)).astype(BF16)
    o_ref[...] = _dot(c_hi, w_hi) + _dot(c_lo, w_hi) + _dot(c_hi, w_lo) + b_ref[...]


def _modulation(c, w_mod, b_mod):
    rows = c.shape[0]
    n = -(-rows // 8) * 8
    c = jnp.pad(c, ((0, n - rows), (0, 0)))
    d6 = w_mod.shape[1]
    return pl.pallas_call(
        _mod_kernel,
        grid=(d6 // D_MODEL,),
        in_specs=[pl.BlockSpec((n, D_MODEL), lambda j: (0, 0)),
                  pl.BlockSpec((D_MODEL, D_MODEL), lambda j: (0, j)),
                  pl.BlockSpec((1, D_MODEL), lambda j: (0, j))],
        out_specs=pl.BlockSpec((n, D_MODEL), lambda j: (0, j)),
        out_shape=jax.ShapeDtypeStruct((n, d6), F32),
        compiler_params=_params("arbitrary"),
        name="modulation",
    )(c, w_mod, b_mod.reshape(1, d6))[:rows]


def _inproj_kernel(x_ref, mod_ref, g_ref, w_ref, wf_ref, bf_ref,
                   xl_ref, gy_ref, qb_ref, kb_ref, vb_ref, k_ref, v_ref, lf_ref, lfp_ref):
    nb, tt, d = x_ref.shape
    x = x_ref[...]
    mod = mod_ref[...]
    hn = _rms(x, g_ref[...]) * (1.0 + mod[:, 1:2, :]) + mod[:, 0:1, :]
    hb = hn.reshape(nb * tt, d).astype(BF16)

    def proj(col):
        return _dot(hb, w_ref[:, col * WL:(col + 1) * WL]).reshape(nb, tt, WL)

    xl_ref[...] = proj(0)
    gy_ref[...] = _gelu_tanh(proj(1))
    qb_ref[...] = (proj(2) * (HEAD_DIM ** -0.5)).astype(BF16)
    k = proj(3)
    k_ref[...] = k
    kb_ref[...] = k.astype(BF16)
    v = proj(4)
    v_ref[...] = v
    vb_ref[...] = v.astype(BF16)
    fl = _dot(hb, wf_ref[...]) + bf_ref[...]
    lane = lax.broadcasted_iota(jnp.int32, fl.shape, 1)
    lf = jnp.where(lane < N_HEADS, _log_sigmoid(fl), 0.0).reshape(nb, tt, LANES)
    lfp_ref[...] = lf
    lf_ref[...] = lf[:, :, :N_HEADS]


def _inproj(x, mod, g_pre, w_main, w_f, b_f):
    n, t, d = x.shape
    nb, tt = _seq_blocks(n, t)
    blk = lambda w: pl.BlockSpec((nb, tt, w), lambda i, j: (i, j, 0))
    const = lambda shape: pl.BlockSpec(shape, lambda i, j: (0,) * len(shape))
    f32 = lambda w: jax.ShapeDtypeStruct((n, t, w), F32)
    b16 = lambda w: jax.ShapeDtypeStruct((n, t, w), BF16)
    return pl.pallas_call(
        _inproj_kernel,
        grid=(n // nb, t // tt),
        in_specs=[blk(d),
                  pl.BlockSpec((nb, 6, d), lambda i, j: (i, 0, 0)),
                  const((1, d)), const(w_main.shape), const(w_f.shape), const((1, LANES))],
        out_specs=[blk(WL), blk(WL), blk(WA), blk(WA), blk(WA), blk(WA), blk(WA), blk(N_HEADS), blk(LANES)],
        out_shape=[f32(WL), f32(WL), b16(WA), b16(WA), b16(WA), f32(WA), f32(WA), f32(N_HEADS), f32(LANES)],
        compiler_params=_params("parallel", "arbitrary"),
        name="inproj",
    )(x, mod, g_pre, w_main, w_f, b_f)


def _aug_lane(h):
    return HEAD_DIM if h % 2 == 0 else 0


def _inproj_prompt_kernel(x_ref, mod_ref, g_ref, w_ref, wf_ref, bf_ref, place_ref,
                          xl_ref, gy_ref, k_ref, v_ref, lf_ref, qt_ref, ka_ref, vt_ref, bpre_ref, carry_ref):
    _, tt, d = x_ref.shape

    @pl.when(pl.program_id(1) == 0)
    def _():
        carry_ref[...] = jnp.zeros_like(carry_ref)

    mod = mod_ref[0]
    hb = (_rms(x_ref[0], g_ref[...]) * (1.0 + mod[1:2, :]) + mod[0:1, :]).astype(BF16)

    def proj(col):
        return _dot(hb, w_ref[:, col * WL:(col + 1) * WL])

    xl_ref[0] = proj(0)
    gy_ref[0] = _gelu_tanh(proj(1))
    q = proj(2) * (HEAD_DIM ** -0.5)
    k = proj(3)
    k_ref[0] = k
    v = proj(4)
    v_ref[0] = v
    fl = _dot(hb, wf_ref[...]) + bf_ref[...]
    lane = lax.broadcasted_iota(jnp.int32, (tt, LANES), 1)
    lf = jnp.where(lane < N_HEADS, _log_sigmoid(fl), 0.0)
    lf_ref[0] = lf[:, :N_HEADS]

    row = lax.broadcasted_iota(jnp.int32, (tt, tt), 0)
    col = lax.broadcasted_iota(jnp.int32, (tt, tt), 1)
    tril = (col <= row).astype(BF16)
    hi, mid, lo = _split3(lf)
    e = _dot(tril, hi) + _dot(tril, mid) + _dot(tril, lo)
    bpre_ref[0, 0] = carry_ref[...]
    carry_ref[...] += e[tt - 1:tt, :]
    e_hi, e_mid, e_lo = _split3(-e)
    aug_k = _dot(e_hi, place_ref[0]) + _dot(e_mid, place_ref[1]) + _dot(e_lo, place_ref[2])

    for h in range(N_HEADS):
        pair = slice((h // 2) * PAIR_W, (h // 2 + 1) * PAIR_W)
        dims = (lane < HEAD_DIM) if h % 2 == 0 else (lane >= HEAD_DIM)
        a0 = _aug_lane(h)
        ones3 = ((lane >= a0) & (lane < a0 + 3)).astype(F32)
        one1 = (lane == a0).astype(F32)
        qt_ref[0, h] = (jnp.where(dims, q[:, pair], 0.0) + ones3).T.astype(BF16)
        ka_ref[0, h] = (jnp.where(dims, k[:, pair], 0.0) + aug_k[:, h * LANES:(h + 1) * LANES]).astype(BF16)
        vt_ref[0, h] = (jnp.where(dims, v[:, pair], 0.0) + one1).T.astype(BF16)


def _placement():
    pl_mat = np.zeros((3, LANES, N_HEADS * LANES), np.float32)
    for p in range(3):
        for h in range(N_HEADS):
            pl_mat[p, h, h * LANES + _aug_lane(h) + p] = 1.0
    return jnp.asarray(pl_mat, BF16)


def _inproj_prompt(x, mod, g_pre, w_main, w_f, b_f):
    n, t, d = x.shape
    tt = ROW_TILE
    assert t % tt == 0
    nblk = t // tt
    blk = lambda w: pl.BlockSpec((1, tt, w), lambda i, j: (i, j, 0))
    const = lambda shape: pl.BlockSpec(shape, lambda i, j: (0,) * len(shape))
    f32 = lambda w: jax.ShapeDtypeStruct((n, t, w), F32)
    place = _placement()
    return pl.pallas_call(
        _inproj_prompt_kernel,
        grid=(n, nblk),
        in_specs=[blk(d), pl.BlockSpec((1, 6, d), lambda i, j: (i, 0, 0)),
                  const((1, d)), const(w_main.shape), const(w_f.shape), const((1, LANES)), const(place.shape)],
        out_specs=[blk(WL), blk(WL), blk(WA), blk(WA), blk(N_HEADS),
                   pl.BlockSpec((1, N_HEADS, LANES, tt), lambda i, j: (i, 0, 0, j)),
                   pl.BlockSpec((1, N_HEADS, tt, LANES), lambda i, j: (i, 0, j, 0)),
                   pl.BlockSpec((1, N_HEADS, LANES, tt), lambda i, j: (i, 0, 0, j)),
                   pl.BlockSpec((1, 1, 1, LANES), lambda i, j: (i, j, 0, 0))],
        out_shape=[f32(WL), f32(WL), f32(WA), f32(WA), f32(N_HEADS),
                   jax.ShapeDtypeStruct((n, N_HEADS, LANES, t), BF16),
                   jax.ShapeDtypeStruct((n, N_HEADS, t, LANES), BF16),
                   jax.ShapeDtypeStruct((n, N_HEADS, LANES, t), BF16),
                   jax.ShapeDtypeStruct((n, nblk, 1, LANES), F32)],
        scratch_shapes=[pltpu.VMEM((1, LANES), F32)],
        compiler_params=_params("parallel", "arbitrary"),
        name="inproj_prompt",
    )(x, mod, g_pre, w_main, w_f, b_f, place)


def _expm1_neg(x):
    poly = x * (1.0 + x * (0.5 + x * (1.0 / 6.0 + x * (1.0 / 24.0 + x * (1.0 / 120.0)))))
    return jnp.where(x > -0.1, poly, jnp.exp(x) - 1.0)


def _lru_kernel(xl_ref, gy_ref, conv0_ref, h0_ref, cw_ref, cb_ref, wr_ref, br_ref, wi_ref, bi_ref,
                lam_ref, g_ref, out_ref, conv_ref, hlast_ref, tail_ref, carry_ref):
    nb, tt, w = xl_ref.shape
    j = pl.program_id(1)

    @pl.when(j == 0)
    def _():
        tail_ref[:, 8 - (CONV_W - 1):, :] = conv0_ref[...]
        carry_ref[...] = h0_ref[...]

    xl = xl_ref[...]
    xpad = jnp.concatenate([tail_ref[...], xl], axis=1)
    cw = cw_ref[...]
    xc = jnp.zeros_like(xl) + cb_ref[...]
    for k in range(CONV_W):
        off = 8 - (CONV_W - 1) + k
        xc = xc + xpad[:, off:off + tt, :] * cw[k:k + 1, :]
    conv_ref[...] = xpad[:, tt + 8 - (CONV_W - 1):, :]
    tail_ref[...] = xpad[:, tt:, :]

    m = nb * tt
    xf = xc.reshape(m, w)
    xb = xf.astype(BF16)
    r = _sigmoid(_dot(xb, wr_ref[...]) + br_ref[...])
    gi = _sigmoid(_dot(xb, wi_ref[...]) + bi_ref[...])
    lam = lam_ref[...]
    softplus = jnp.maximum(-lam, 0.0) + jnp.log1p(jnp.exp(-jnp.abs(lam)))
    log_a = (-LRU_C) * r * softplus
    a = jnp.exp(log_a)
    b = jnp.sqrt(-_expm1_neg(2.0 * log_a)) * (gi * xf)

    pos = lax.broadcasted_iota(jnp.int32, (m, w), 0) % tt
    d = 1
    while d < tt:
        keep = pos >= d
        a_prev = jnp.where(keep, pltpu.roll(a, d, 0), 1.0)
        b_prev = jnp.where(keep, pltpu.roll(b, d, 0), 0.0)
        b = a * b_prev + b
        a = a * a_prev
        d *= 2
    h = a.reshape(nb, tt, w) * carry_ref[...] + b.reshape(nb, tt, w)
    h_last = h[:, tt - 1:tt, :]
    carry_ref[...] = h_last
    hlast_ref[...] = h_last
    out_ref[...] = _rms(h * gy_ref[...], g_ref[...]).astype(BF16)


def _lru(xl, gy, conv0, h0, conv_w, conv_b, wr_bd, b_r, wi_bd, b_i, lam, g_lru):
    n, t, w = xl.shape
    nb, tt = _seq_blocks(n, t)
    blk = pl.BlockSpec((nb, tt, w), lambda i, j: (i, j, 0))
    per_seq = lambda rows: pl.BlockSpec((nb, rows, w), lambda i, j: (i, 0, 0))
    const = lambda shape: pl.BlockSpec(shape, lambda i, j: (0,) * len(shape))
    row = const((1, w))
    return pl.pallas_call(
        _lru_kernel,
        grid=(n // nb, t // tt),
        in_specs=[blk, blk, per_seq(CONV_W - 1), per_seq(1),
                  const((CONV_W, w)), row, const((w, w)), row, const((w, w)), row, row, row],
        out_specs=[blk, per_seq(CONV_W - 1), per_seq(1)],
        out_shape=[jax.ShapeDtypeStruct((n, t, w), BF16),
                   jax.ShapeDtypeStruct((n, CONV_W - 1, w), F32),
                   jax.ShapeDtypeStruct((n, 1, w), F32)],
        scratch_shapes=[pltpu.VMEM((nb, 8, w), F32), pltpu.VMEM((nb, 1, w), F32)],
        compiler_params=_params("parallel", "arbitrary"),
        name="rglru",
    )(xl, gy, conv0, h0, conv_w, conv_b, wr_bd, b_r, wi_bd, b_i, lam, g_lru)


def _cumsum_kernel(lf_ref, o_ref, carry_ref):
    tb = lf_ref.shape[1]

    @pl.when(pl.program_id(1) == 0)
    def _():
        carry_ref[...] = jnp.zeros_like(carry_ref)

    row = lax.broadcasted_iota(jnp.int32, (tb, tb), 0)
    col = lax.broadcasted_iota(jnp.int32, (tb, tb), 1)
    tril = (col <= row).astype(BF16)
    hi, mid, lo = _split3(lf_ref[0])
    d = _dot(tril, hi) + _dot(tril, mid) + _dot(tril, lo) + carry_ref[...]
    carry_ref[...] = d[tb - 1:tb, :]
    o_ref[0] = d.T[:N_HEADS, :]


def _cumsum_heads(lfp, tb):
    n, t, _ = lfp.shape
    return pl.pallas_call(
        _cumsum_kernel,
        grid=(n, t // tb),
        in_specs=[pl.BlockSpec((1, tb, LANES), lambda i, j: (i, j, 0))],
        out_specs=pl.BlockSpec((1, N_HEADS, tb), lambda i, j: (i, 0, j)),
        out_shape=jax.ShapeDtypeStruct((n, N_HEADS, t), F32),
        scratch_shapes=[pltpu.VMEM((1, LANES), F32)],
        compiler_params=_params("parallel", "arbitrary"),
        name="logf_cumsum",
    )(lfp)


def _pair_masks(x):
    lane = lax.broadcasted_iota(jnp.int32, x.shape, 1)
    zero = jnp.zeros_like(x)
    return jnp.where(lane < HEAD_DIM, x, zero), jnp.where(lane >= HEAD_DIM, x, zero)


def _online_update(s, m_prev, l_prev):
    m_new = jnp.maximum(m_prev, jnp.max(s, axis=1, keepdims=True))
    alpha = jnp.exp(m_prev - m_new)
    p = jnp.exp(s - m_new)
    l_new = alpha * l_prev + jnp.sum(p, axis=1, keepdims=True)
    return p, alpha, m_new, l_new


def _pair_step(q_a, q_b, k, v_a, v_b, bias_a, bias_b, mask, state):
    m_a, l_a, m_b, l_b, acc = state
    s_a = _dot_nt(q_a, k) + bias_a
    s_b = _dot_nt(q_b, k) + bias_b
    if mask is not None:
        s_a = jnp.where(mask, s_a, NEG_INF)
        s_b = jnp.where(mask, s_b, NEG_INF)
    p_a, al_a, m_a, l_a = _online_update(s_a, m_a, l_a)
    p_b, al_b, m_b, l_b = _online_update(s_b, m_b, l_b)
    lane = lax.broadcasted_iota(jnp.int32, acc.shape, 1)
    alpha = jnp.where(lane < HEAD_DIM, al_a, al_b)
    acc = alpha * acc + (_dot(p_a.astype(BF16), v_a) + _dot(p_b.astype(BF16), v_b))
    return m_a, l_a, m_b, l_b, acc


def _pair_finish(state):
    m_a, l_a, m_b, l_b, acc = state
    lane = lax.broadcasted_iota(jnp.int32, acc.shape, 1)
    return acc / jnp.where(lane < HEAD_DIM, l_a, l_b)


def _att_prompt_kernel(bpre_ref, qt_ref, ka_ref, vt_ref, o_ref, s_ref, p_ref, acc_ref):
    tq = qt_ref.shape[2]
    tk = ATT_TK
    nblk = ka_ref.shape[1] // tk
    b, hp, i = pl.program_id(0), pl.program_id(1), pl.program_id(2)
    q0 = i * tq
    jd = q0 // tk
    kpos = lax.broadcasted_iota(jnp.int32, (tk, tq), 0)
    qpos = lax.broadcasted_iota(jnp.int32, (tk, tq), 1)
    rows = lax.broadcasted_iota(jnp.int32, (LANES, tq), 0)
    base = [((b * N_PAIRS + hp) * 2 + hh) * nblk for hh in range(2)]

    def scores(j, masked):
        start = pl.multiple_of(j * tk, tk)
        col_max = []
        for hh in range(2):
            s = _dot(ka_ref[hh, pl.ds(start, tk), :], qt_ref[hh])
            if masked:
                s = jnp.where(kpos + start <= qpos + q0, s, NEG_INF)
            s_ref[hh] = s
            col_max.append(jnp.max(s, axis=0, keepdims=True))
        return tuple(col_max)

    def softmax_pv(j, col_max, m):
        start = pl.multiple_of(j * tk, tk)
        m_out = []
        for hh in range(2):
            c = bpre_ref[base[hh] + jd] - bpre_ref[base[hh] + j]
            m_new = jnp.maximum(m[hh], col_max[hh] + c)
            alpha = jnp.exp(m[hh] - m_new)
            shift = m_new - c
            for ch in range(tk // ATT_CHUNK):
                sl = slice(ch * ATT_CHUNK, (ch + 1) * ATT_CHUNK)
                p_ref[hh, sl, :] = jnp.exp(s_ref[hh, sl, :] - shift).astype(BF16)
            m_out.append((m_new, alpha))
        return tuple(m_out), start

    def accumulate(m_alpha, start):
        for hh in range(2):
            pv = _dot(vt_ref[hh, :, pl.ds(start, tk)], p_ref[hh])
            acc_ref[hh] = m_alpha[hh][1] * acc_ref[hh] + pv
        return tuple(ma[0] for ma in m_alpha)

    def step(j, next_masked, carry):
        col_max, m = carry
        m_alpha, start = softmax_pv(j, col_max, m)
        col_max_next = scores(j + 1, next_masked)
        return col_max_next, accumulate(m_alpha, start)

    acc_ref[...] = jnp.zeros_like(acc_ref)
    neg = jnp.full((1, tq), NEG_INF, F32)
    carry = (scores(0, True), (neg, neg))
    carry = lax.fori_loop(0, jd - 1, lambda j, cr: step(j, False, cr), carry)
    carry = lax.cond(jd > 0, lambda cr: step(jd - 1, True, cr), lambda cr: cr, carry)
    col_max, m = carry
    m_alpha, start = softmax_pv(jd, col_max, m)
    accumulate(m_alpha, start)

    acc_a, acc_b = acc_ref[0], acc_ref[1]
    out_a = acc_a / acc_a[_aug_lane(0):_aug_lane(0) + 1, :]
    out_b = acc_b / acc_b[_aug_lane(1):_aug_lane(1) + 1, :]
    o_ref[0] = jnp.where(rows < HEAD_DIM, out_a, out_b).T


def _att_prompt(qt, ka, vt, bpre):
    n, _, _, t = qt.shape
    assert t % ATT_TQ == 0 and ATT_TK % ATT_TQ == 0 and ATT_TK == ROW_TILE
    pair = lambda q: (N_PAIRS, 2) + q.shape[2:]
    qt, ka, vt = (a.reshape((n,) + pair(a)) for a in (qt, ka, vt))
    bflat = jnp.transpose(bpre[:, :, 0, :N_HEADS], (0, 2, 1)).reshape(-1)
    return pl.pallas_call(
        _att_prompt_kernel,
        grid=(n, N_PAIRS, t // ATT_TQ),
        in_specs=[pl.BlockSpec(memory_space=pltpu.SMEM),
                  pl.BlockSpec((None, None, 2, LANES, ATT_TQ), lambda b, h, i: (b, h, 0, 0, i)),
                  pl.BlockSpec((None, None, 2, t, LANES), lambda b, h, i: (b, h, 0, 0, 0),
                               pipeline_mode=pl.Buffered(1)),
                  pl.BlockSpec((None, None, 2, LANES, t), lambda b, h, i: (b, h, 0, 0, 0),
                               pipeline_mode=pl.Buffered(1))],
        out_specs=pl.BlockSpec((1, ATT_TQ, PAIR_W), lambda b, h, i: (b, i, h)),
        out_shape=jax.ShapeDtypeStruct((n, t, WA), F32),
        scratch_shapes=[pltpu.VMEM((2, ATT_TK, ATT_TQ), F32), pltpu.VMEM((2, ATT_TK, ATT_TQ), BF16),
                        pltpu.VMEM((2, LANES, ATT_TQ), F32)],
        compiler_params=_params("parallel", "parallel", "arbitrary"),
        name="att_prompt",
    )(bflat, qt, ka, vt)


def _att_sample_kernel(q_ref, kp_ref, vp_ref, dp_ref, dend_ref, kn_ref, vn_ref, dn_ref, o_ref,
                       m_ref, l_ref, acc_ref):
    t = q_ref.shape[1]
    j = pl.program_id(1)

    @pl.when(j == 0)
    def _():
        m_ref[...] = jnp.full_like(m_ref, NEG_INF)
        l_ref[...] = jnp.zeros_like(l_ref)
        acc_ref[...] = jnp.zeros_like(acc_ref)

    def pair(h, k, v, bias, mask):
        sl = slice(h * PAIR_W, (h + 1) * PAIR_W)
        q_a, q_b = _pair_masks(q_ref[0, :, sl])
        v_a, v_b = _pair_masks(v[:, sl].astype(BF16))
        state = (m_ref[2 * h], l_ref[2 * h], m_ref[2 * h + 1], l_ref[2 * h + 1], acc_ref[:, sl])
        state = _pair_step(q_a, q_b, k[:, sl].astype(BF16), v_a, v_b,
                           bias[2 * h:2 * h + 1], bias[2 * h + 1:2 * h + 2], mask, state)
        m_ref[2 * h], l_ref[2 * h], m_ref[2 * h + 1], l_ref[2 * h + 1], acc_ref[:, sl] = state
        return state

    bias_past = dend_ref[0] - dp_ref[0]
    for h in range(N_PAIRS):
        pair(h, kp_ref[0], vp_ref[0], bias_past, None)

    @pl.when(j == pl.num_programs(1) - 1)
    def _():
        bias_new = -dn_ref[0]
        qpos = lax.broadcasted_iota(jnp.int32, (t, t), 0)
        kpos = lax.broadcasted_iota(jnp.int32, (t, t), 1)
        for h in range(N_PAIRS):
            state = pair(h, kn_ref[0], vn_ref[0], bias_new, kpos <= qpos)
            o_ref[0, :, h * PAIR_W:(h + 1) * PAIR_W] = _pair_finish(state)


def _att_sample(qb, k_past, v_past, d_past, kb_new, vb_new, d_new):
    n, t, _ = qb.shape
    past = k_past.shape[1]
    assert past % SAMPLE_TK == 0
    d_end = d_past[:, :, past - 1:]
    new = lambda dt: pl.BlockSpec((1, t, WA), lambda b, j: (b, 0, 0))
    return pl.pallas_call(
        _att_sample_kernel,
        grid=(n, past // SAMPLE_TK),
        in_specs=[new(BF16),
                  pl.BlockSpec((1, SAMPLE_TK, WA), lambda b, j: (b, j, 0)),
                  pl.BlockSpec((1, SAMPLE_TK, WA), lambda b, j: (b, j, 0)),
                  pl.BlockSpec((1, N_HEADS, SAMPLE_TK), lambda b, j: (b, 0, j)),
                  pl.BlockSpec((1, N_HEADS, 1), lambda b, j: (b, 0, 0)),
                  new(BF16), new(BF16),
                  pl.BlockSpec((1, N_HEADS, t), lambda b, j: (b, 0, 0))],
        out_specs=pl.BlockSpec((1, t, WA), lambda b, j: (b, 0, 0)),
        out_shape=jax.ShapeDtypeStruct((n, t, WA), F32),
        scratch_shapes=[pltpu.VMEM((N_HEADS, t, 1), F32), pltpu.VMEM((N_HEADS, t, 1), F32),
                        pltpu.VMEM((t, WA), F32)],
        compiler_params=_params("parallel", "arbitrary"),
        name="att_sample",
    )(qb, k_past, v_past, d_past, d_end, kb_new, vb_new, d_new)


def _first_index_of_max(x, axis):
    mx = jnp.max(x, axis=axis, keepdims=True)
    idx = lax.broadcasted_iota(jnp.int32, x.shape, axis)
    first = jnp.min(jnp.where(x == mx, idx, x.shape[axis]), axis=axis, keepdims=True)
    return mx, idx == first


def _route(s, bias):
    m = s.shape[1]
    sb = (s + bias).reshape(N_GROUPS, GROUP_SIZE, m)
    top1, is_top1 = _first_index_of_max(sb, 1)
    top2 = jnp.max(jnp.where(is_top1, NEG_INF, sb), axis=1, keepdims=True)
    grp = (top1 + top2).reshape(N_GROUPS, m)
    gi = lax.broadcasted_iota(jnp.int32, (N_GROUPS, N_GROUPS, m), 0)
    gj = lax.broadcasted_iota(jnp.int32, (N_GROUPS, N_GROUPS, m), 1)
    other, mine = grp[None, :, :], grp[:, None, :]
    beats = (other > mine) | ((other == mine) & (gj < gi))
    g_rank = jnp.sum(beats.astype(jnp.int32), axis=1)
    g_keep = (g_rank < TOPK_GROUPS)[:, None, :]
    cand = jnp.where(g_keep, sb, NEG_INF).reshape(N_EXPERTS, m)
    picks = []
    for _ in range(TOP_K):
        _, pick = _first_index_of_max(cand, 0)
        picks.append(pick)
        cand = jnp.where(pick, NEG_INF, cand)
    w = jnp.concatenate([jnp.sum(jnp.where(pk, s, 0.0), axis=0, keepdims=True) for pk in picks], axis=0)
    return picks, w / jnp.sum(w, axis=0, keepdims=True) * ROUTED_SCALE


HALF_MASK = 0xFFFF0000


def _pack_halves(x):
    c = x.shape[1] // 2
    lo = lax.bitcast_convert_type(x[:, :c].astype(BF16).astype(F32), jnp.uint32) >> jnp.uint32(16)
    hi = lax.bitcast_convert_type(x[:, c:].astype(BF16).astype(F32), jnp.uint32) & jnp.uint32(HALF_MASK)
    return lo | hi


def _unpack_halves(w):
    lo = lax.bitcast_convert_type(w << jnp.uint32(16), F32)
    hi = lax.bitcast_convert_type(w & jnp.uint32(HALF_MASK), F32)
    return lo, hi


def _outproj_kernel(x_ref, lru_ref, att_ref, mod_ref, gatt_ref, wtop_ref, wbot_ref, gpost_ref, gpre_ref,
                    rwh_ref, rwl_ref, rb_ref, cnt_in_ref,
                    x1_ref, hf_ref, xw_ref, ids_ref, ranks_ref, gates_ref, cnt_out_ref, carry_ref):
    nb, tt, d = x_ref.shape
    m = nb * tt

    @pl.when((pl.program_id(0) == 0) & (pl.program_id(1) == 0))
    def _():
        carry_ref[...] = cnt_in_ref[...]

    mod = mod_ref[...]
    att_n = _rms(att_ref[...], gatt_ref[...]).reshape(m, WA).astype(BF16)
    mix = _dot(lru_ref[...].reshape(m, WL), wtop_ref[...]) + _dot(att_n, wbot_ref[...])
    x1 = x_ref[...] + mod[:, 2:3, :] * _rms(mix, gpost_ref[...]).reshape(nb, tt, d)
    x1_ref[...] = x1
    hf = (_rms(x1, gpre_ref[...]) * (1.0 + mod[:, 4:5, :]) + mod[:, 3:4, :]).reshape(m, d)
    hf_hi = hf.astype(BF16)
    hf_ref[...] = hf_hi.reshape(nb, tt, d)
    hf_lo = (hf - hf_hi.astype(F32)).astype(BF16)
    rwh = rwh_ref[...]
    logits = _dot_nt(rwh, hf_hi) + _dot_nt(rwh, hf_lo) + _dot_nt(rwl_ref[...], hf_hi)
    picks, gates = _route(_sigmoid(logits), rb_ref[...])
    xw_ref[...] = _pack_halves(hf).reshape(nb, tt, d // 2)

    sel = jnp.zeros((N_EXPERTS, m), F32)
    for pk in picks:
        sel = sel + pk.astype(F32)
    before = (lax.broadcasted_iota(jnp.int32, (m, m), 0) < lax.broadcasted_iota(jnp.int32, (m, m), 1)).astype(BF16)
    prior = _dot(sel.astype(BF16), before) + carry_ref[...]
    expert = lax.broadcasted_iota(jnp.int32, (N_EXPERTS, m), 0).astype(F32)
    take = lambda pk, v: jnp.sum(jnp.where(pk, v, 0.0), axis=0, keepdims=True)
    ids_ref[...] = jnp.concatenate([take(pk, expert) for pk in picks], axis=0).astype(jnp.int32)
    ranks_ref[...] = jnp.concatenate([take(pk, prior) for pk in picks], axis=0).astype(jnp.int32)
    carry_ref[...] += jnp.sum(sel, axis=1, keepdims=True)
    cnt_out_ref[...] = carry_ref[...]
    gates = jnp.concatenate([gates, jnp.zeros((LANES - TOP_K, m), F32)], axis=0)
    gates_ref[...] = gates.T.reshape(nb, tt, LANES)


def _outproj(x, lru_n, att, mod, g_att, w_top, w_bot, g_post, g_pre, rw_hi, rw_lo, r_bias, cnt_in):
    n, t, d = x.shape
    nb, tt = _seq_blocks(n, t)
    m = nb * tt
    steps_t = t // tt
    blk = lambda w: pl.BlockSpec((nb, tt, w), lambda i, j: (i, j, 0))
    const = lambda shape: pl.BlockSpec(shape, lambda i, j: (0,) * len(shape))
    per_tok = pl.BlockSpec((TOP_K, m), lambda i, j: (0, i * steps_t + j))
    return pl.pallas_call(
        _outproj_kernel,
        grid=(n // nb, steps_t),
        in_specs=[blk(d), blk(WL), blk(WA), pl.BlockSpec((nb, 6, d), lambda i, j: (i, 0, 0)),
                  const((1, WA)), const((WL, d)), const((WA, d)), const((1, d)), const((1, d)),
                  const((N_EXPERTS, d)), const((N_EXPERTS, d)), const((N_EXPERTS, 1)), const((N_EXPERTS, 1))],
        out_specs=[blk(d), blk(d), blk(d // 2), per_tok, per_tok, blk(LANES), const((N_EXPERTS, 1))],
        out_shape=[jax.ShapeDtypeStruct((n, t, d), F32), jax.ShapeDtypeStruct((n, t, d), BF16),
                   jax.ShapeDtypeStruct((n, t, d // 2), jnp.uint32),
                   jax.ShapeDtypeStruct((TOP_K, n * t), jnp.int32), jax.ShapeDtypeStruct((TOP_K, n * t), jnp.int32),
                   jax.ShapeDtypeStruct((n, t, LANES), F32), jax.ShapeDtypeStruct((N_EXPERTS, 1), F32)],
        scratch_shapes=[pltpu.VMEM((N_EXPERTS, 1), F32)],
        compiler_params=_params("arbitrary", "arbitrary"),
        name="outproj_router",
    )(x, lru_n, att, mod, g_att, w_top, w_bot, g_post, g_pre, rw_hi, rw_lo, r_bias, cnt_in)


def _row_copies(pos_ref, n_rows, make):
    def each(method):
        def body(t8, carry):
            for k in range(8):
                t = t8 * 8 + k
                for r in range(TOP_K):
                    getattr(make(r, t, pos_ref[r, t]), method)()
            return carry
        lax.fori_loop(0, n_rows // 8, body, 0)
    each("start")
    each("wait")


def _dispatch_kernel(pos_ref, x_ref, xs_ref, sem):
    m = x_ref.shape[0]
    _row_copies(pos_ref, m, lambda r, t, slot: pltpu.make_async_copy(
        x_ref.at[pl.ds(t, 1), :], xs_ref.at[pl.ds(slot, 1), :], sem))


def _dispatch(xw, pos, n_slots):
    m_tot, c = xw.shape
    return pl.pallas_call(
        _dispatch_kernel,
        grid=(m_tot // ROW_TILE,),
        in_specs=[pl.BlockSpec((TOP_K, ROW_TILE), lambda i: (0, i), memory_space=pltpu.SMEM),
                  pl.BlockSpec((ROW_TILE, c), lambda i: (i, 0))],
        out_specs=pl.BlockSpec(memory_space=pl.ANY),
        out_shape=jax.ShapeDtypeStruct((n_slots, c), jnp.uint32),
        scratch_shapes=[pltpu.SemaphoreType.DMA],
        compiler_params=_params("arbitrary"),
        name="moe_dispatch",
    )(pos, xw)


def _swiglu_halves(lo, hi, wg, wu, wd):
    c = lo.shape[1]
    hg = _dot(lo, wg[:c]) + _dot(hi, wg[c:])
    hu = _dot(lo, wu[:c]) + _dot(hi, wu[c:])
    return _dot((_silu(hg) * hu).astype(BF16), wd)


def _expert_kernel(te_ref, valid_ref, x_ref, wg_ref, wu_ref, wd_ref, y_ref):
    i = pl.program_id(0)
    valid = valid_ref[i]

    @pl.when(valid > 0)
    def _():
        w = x_ref[...]
        row = lax.broadcasted_iota(jnp.int32, w.shape, 0)
        lo, hi = _unpack_halves(jnp.where(row < valid, w, jnp.uint32(0)))
        y_ref[...] = _pack_halves(_swiglu_halves(lo.astype(BF16), hi.astype(BF16), wg_ref[0], wu_ref[0], wd_ref[0]))


def _experts(xs, tile_expert, tile_valid, wg, wu, wd):
    n_slots, c = xs.shape
    n_tiles = n_slots // EXPERT_TILE
    d = 2 * c
    rows = pl.BlockSpec((EXPERT_TILE, c), lambda i, te, tv: (i, 0))
    weight = lambda shape: pl.BlockSpec((1,) + shape, lambda i, te, tv: (te[i], 0, 0))
    return pl.pallas_call(
        _expert_kernel,
        grid_spec=pltpu.PrefetchScalarGridSpec(
            num_scalar_prefetch=2, grid=(n_tiles,),
            in_specs=[rows, weight((d, D_EXPERT)), weight((d, D_EXPERT)), weight((D_EXPERT, d))],
            out_specs=rows),
        out_shape=jax.ShapeDtypeStruct((n_slots, c), jnp.uint32),
        compiler_params=_params("arbitrary"),
        name="moe_experts",
    )(tile_expert, tile_valid, xs, wg, wu, wd)


def _combine_kernel(pos_ref, ys_ref, gates_ref, hf_ref, x1_ref, mod_ref, sg_ref, su_ref, sd_ref, gpost_ref,
                    y_ref, buf_ref, sem):
    nb, tt, d = hf_ref.shape
    m = nb * tt
    c = d // 2
    _row_copies(pos_ref, m, lambda r, t, slot: pltpu.make_async_copy(
        ys_ref.at[pl.ds(slot, 1), :], buf_ref.at[r, pl.ds(t, 1), :], sem))
    x = hf_ref[...].reshape(m, d)
    shared = _swiglu_halves(x[:, :c], x[:, c:], sg_ref[...], su_ref[...], sd_ref[...])
    gates = gates_ref[...].reshape(m, LANES)
    acc_lo = shared[:, :c]
    acc_hi = shared[:, c:]
    for r in range(TOP_K):
        lo, hi = _unpack_halves(buf_ref[r])
        g = gates[:, r:r + 1]
        acc_lo = acc_lo + g * lo
        acc_hi = acc_hi + g * hi
    z = _rms(jnp.concatenate([acc_lo, acc_hi], axis=1), gpost_ref[...]).reshape(nb, tt, d)
    y_ref[...] = x1_ref[...] + mod_ref[...][:, 5:6, :] * z


def _combine(pos, ys, gates_t, hf, x1, mod, sg, su, sd, g_post):
    n, t, d = hf.shape
    nb, tt = _seq_blocks(n, t)
    m = nb * tt
    steps_t = t // tt
    blk = lambda w: pl.BlockSpec((nb, tt, w), lambda i, j: (i, j, 0))
    const = lambda shape: pl.BlockSpec(shape, lambda i, j: (0,) * len(shape))
    return pl.pallas_call(
        _combine_kernel,
        grid=(n // nb, steps_t),
        in_specs=[pl.BlockSpec((TOP_K, m), lambda i, j: (0, i * steps_t + j), memory_space=pltpu.SMEM),
                  pl.BlockSpec(memory_space=pl.ANY),
                  blk(LANES), blk(d), blk(d), pl.BlockSpec((nb, 6, d), lambda i, j: (i, 0, 0)),
                  const((d, D_EXPERT)), const((d, D_EXPERT)), const((D_EXPERT, d)), const((1, d))],
        out_specs=blk(d),
        out_shape=jax.ShapeDtypeStruct((n, t, d), F32),
        scratch_shapes=[pltpu.VMEM((TOP_K, m, d // 2), jnp.uint32), pltpu.SemaphoreType.DMA],
        compiler_params=_params("arbitrary", "arbitrary"),
        name="moe_combine",
    )(pos, ys, gates_t, hf, x1, mod, sg, su, sd, g_post)


def _slot_plan(ids, ranks, counts):
    n_pairs = ids.shape[0] * ids.shape[1]
    n_tiles = -(-(n_pairs + N_EXPERTS * (EXPERT_TILE - 1)) // EXPERT_TILE)
    cnt = counts.reshape(N_EXPERTS).astype(jnp.int32)
    padded = (cnt + EXPERT_TILE - 1) // EXPERT_TILE * EXPERT_TILE
    ends = jnp.cumsum(padded)
    starts = ends - padded
    pos = starts[ids] + ranks
    tile_start = jnp.arange(n_tiles, dtype=jnp.int32) * EXPERT_TILE
    tile_expert = jnp.minimum(jnp.searchsorted(ends, tile_start, side="right"), N_EXPERTS - 1).astype(jnp.int32)
    tile_valid = jnp.clip(starts[tile_expert] + cnt[tile_expert] - tile_start, 0, EXPERT_TILE)
    tile_valid = jnp.where(tile_start < ends[-1], tile_valid, 0).astype(jnp.int32)
    return pos, tile_expert, tile_valid, n_tiles * EXPERT_TILE


def _block_diag(w):
    g, bw, _ = w.shape
    eye = jnp.eye(g, dtype=w.dtype)
    return (eye[:, None, :, None] * w[:, :, None, :]).reshape(g * bw, g * bw)


def _prep_weights(p):
    d_main = 2 * WL + 3 * WA
    w_in = p["w_in"]
    rw_t = p["router_w"].T
    rw_hi = rw_t.astype(BF16)
    row = lambda v: v.reshape(1, -1)
    return dict(
        w_mod=p["w_mod"], b_mod=p["b_mod"],
        g_pre_mix=row(p["g_pre_mix"]), g_post_mix=row(p["g_post_mix"]),
        g_pre_ffn=row(p["g_pre_ffn"]), g_post_ffn=row(p["g_post_ffn"]),
        w_main=w_in[:, :d_main].astype(BF16),
        w_f=jnp.pad(w_in[:, d_main:], ((0, 0), (0, LANES - N_HEADS))).astype(BF16),
        b_f=jnp.pad(p["b_f"], (0, LANES - N_HEADS)).reshape(1, LANES),
        conv_w=p["conv_w"], conv_b=row(p["conv_b"]),
        wr_bd=_block_diag(p["w_r"]).astype(BF16), b_r=row(p["b_r"]),
        wi_bd=_block_diag(p["w_i"]).astype(BF16), b_i=row(p["b_i"]),
        lam=row(p["lru_lambda"]), g_lru=row(p["g_lru_out"]), g_att=row(p["g_att_out"]),
        w_top=p["w_out"][:WL].astype(BF16), w_bot=p["w_out"][WL:].astype(BF16),
        rw_hi=rw_hi, rw_lo=(rw_t - rw_hi.astype(F32)).astype(BF16),
        r_bias=p["router_bias"].reshape(N_EXPERTS, 1),
        wg=p["w_gate"].astype(BF16), wu=p["w_up"].astype(BF16), wd=p["w_down"].astype(BF16),
        sg=p["ws_gate"].astype(BF16), su=p["ws_up"].astype(BF16), sd=p["ws_down"].astype(BF16),
    )


def _mixers(x, mod, conv0, h0, past, w, cnt_in):
    n, t, _ = x.shape
    proj_args = (x, mod, w["g_pre_mix"], w["w_main"], w["w_f"], w["b_f"])
    if past is None:
        xl, gy, k, v, lf, qt, ka, vt, bpre = _inproj_prompt(*proj_args)
        att = _att_prompt(qt, ka, vt, bpre)
    else:
        xl, gy, qb, kb, vb, k, v, lf, lfp = _inproj(*proj_args)
        d_new = _cumsum_heads(lfp, t)
        k_past, v_past, lf_past = past
        plen = k_past.shape[1]
        lf_past_p = jnp.pad(lf_past, ((0, 0), (0, 0), (0, LANES - N_HEADS)))
        d_past = _cumsum_heads(lf_past_p, ROW_TILE)
        att = _att_sample(qb, k_past.reshape(n, plen, WA), v_past.reshape(n, plen, WA), d_past, kb, vb, d_new)
    lru_n, conv_new, h_new = _lru(xl, gy, conv0, h0.reshape(n, 1, WL), w["conv_w"], w["conv_b"],
                                  w["wr_bd"], w["b_r"], w["wi_bd"], w["b_i"], w["lam"], w["g_lru"])
    routed = _outproj(x, lru_n, att, mod, w["g_att"], w["w_top"], w["w_bot"], w["g_post_mix"],
                      w["g_pre_ffn"], w["rw_hi"], w["rw_lo"], w["r_bias"], cnt_in)
    state = (k.reshape(n, t, N_HEADS, HEAD_DIM), v.reshape(n, t, N_HEADS, HEAD_DIM), lf,
             conv_new, h_new.reshape(n, WL))
    return routed, state


def _layer(xp, xs, mod_p, mod_s, conv_s, h_s, past_s, w):
    n_p = xp.shape[0]
    conv0 = jnp.zeros((n_p, CONV_W - 1, WL), F32)
    h0 = jnp.zeros((n_p, WL), F32)
    zero_cnt = jnp.zeros((N_EXPERTS, 1), F32)
    (x1_p, hf_p, xw_p, ids_p, rk_p, g_p, cnt_p), st_p = _mixers(xp, mod_p, conv0, h0, None, w, zero_cnt)
    (x1_s, hf_s, xw_s, ids_s, rk_s, g_s, cnt), st_s = _mixers(xs, mod_s, conv_s, h_s, past_s, w, cnt_p)

    half = xw_p.shape[-1]
    xw = jnp.concatenate([xw_p.reshape(-1, half), xw_s.reshape(-1, half)], axis=0)
    ids = jnp.concatenate([ids_p, ids_s], axis=1)
    ranks = jnp.concatenate([rk_p, rk_s], axis=1)
    pos, tile_expert, tile_valid, n_slots = _slot_plan(ids, ranks, cnt)
    ys = _experts(_dispatch(xw, pos, n_slots), tile_expert, tile_valid, w["wg"], w["wu"], w["wd"])
    m_p = ids_p.shape[1]
    shared = (w["sg"], w["su"], w["sd"], w["g_post_ffn"])
    yp = _combine(pos[:, :m_p], ys, g_p, hf_p, x1_p, mod_p, *shared)
    ysmp = _combine(pos[:, m_p:], ys, g_s, hf_s, x1_s, mod_s, *shared)
    return yp, ysmp, st_p, st_s


def kernel(x_prompt, x_sample, c_prompt, c_sample, cache_k, cache_v, cache_logf, state_conv, state_lru, w_mod, b_mod, g_pre_mix, g_post_mix, g_pre_ffn, g_post_ffn, w_in, conv_w, conv_b, w_r, b_r, w_i, b_i, lru_lambda, b_f, g_lru_out, g_att_out, w_out, router_w, router_bias, w_gate, w_up, w_down, ws_gate, ws_up, ws_down):
    names = ("w_mod", "b_mod", "g_pre_mix", "g_post_mix", "g_pre_ffn", "g_post_ffn", "w_in", "conv_w", "conv_b",
             "w_r", "b_r", "w_i", "b_i", "lru_lambda", "b_f", "g_lru_out", "g_att_out", "w_out", "router_w",
             "router_bias", "w_gate", "w_up", "w_down", "ws_gate", "ws_up", "ws_down")
    stacked = (w_mod, b_mod, g_pre_mix, g_post_mix, g_pre_ffn, g_post_ffn, w_in, conv_w, conv_b, w_r, b_r, w_i, b_i,
               lru_lambda, b_f, g_lru_out, g_att_out, w_out, router_w, router_bias, w_gate, w_up, w_down,
               ws_gate, ws_up, ws_down)
    depth = w_mod.shape[0]
    n_p, n_s = x_prompt.shape[0], x_sample.shape[0]
    yp, ys = x_prompt, x_sample
    st_p, st_s = [], []
    for l in range(depth):
        w = _prep_weights({k: v[l] for k, v in zip(names, stacked)})
        mod = _modulation(jnp.concatenate([c_prompt, c_sample], axis=0), w["w_mod"], w["b_mod"])
        mod = mod.reshape(n_p + n_s, 6, D_MODEL)
        yp, ys, sp, ss = _layer(yp, ys, mod[:n_p], mod[n_p:], state_conv[l], state_lru[l],
                                (cache_k[l], cache_v[l], cache_logf[l]), w)
        st_p.append(sp)
        st_s.append(ss)
    stack = lambda sts, i: jnp.stack([s[i] for s in sts])
    return (yp, ys) + tuple(stack(st_p, i) for i in range(5)) + tuple(stack(st_s, i) for i in range(5))
```

```python
import functools

import jax
import jax.numpy as jnp
import numpy as np
from jax import lax
from jax.experimental import pallas as pl
from jax.experimental.pallas import tpu as pltpu

F32 = jnp.float32
BF16 = jnp.bfloat16

D_MODEL = 1024
WL = 512
WA = 512
N_HEADS = 8
HEAD_DIM = 64
N_PAIRS = N_HEADS // 2
PAIR_W = 2 * HEAD_DIM
LANES = 128
CONV_W = 4
LRU_BLOCKS = 8
LRU_C = 8.0
N_EXPERTS = 64
N_GROUPS = 8
GROUP_SIZE = N_EXPERTS // N_GROUPS
TOPK_GROUPS = 4
TOP_K = 8
D_EXPERT = 256
ROUTED_SCALE = 2.5
EPS = 1e-6
NEG_INF = float("-inf")

ROW_TILE = 512
ATT_TQ = 512
ATT_TK = 512
ATT_CHUNK = 64
SAMPLE_TK = 1024
EXPERT_TILE = 512
SLOT_COLS = 2048
VMEM_LIMIT = 56 * 1024 * 1024


def _params(*sem):
    return pltpu.CompilerParams(dimension_semantics=sem, vmem_limit_bytes=VMEM_LIMIT)


def _dot(a, b):
    return jnp.dot(a, b, preferred_element_type=F32)


def _dot_nt(a, b):
    return lax.dot_general(a, b, (((1,), (1,)), ((), ())), preferred_element_type=F32)


def _split3(x):
    hi = x.astype(BF16)
    r1 = x - hi.astype(F32)
    mid = r1.astype(BF16)
    lo = (r1 - mid.astype(F32)).astype(BF16)
    return hi, mid, lo


def _rms(x, g):
    return x * lax.rsqrt(jnp.mean(x * x, axis=-1, keepdims=True) + EPS) * g


def _sigmoid(x):
    return 1.0 / (1.0 + jnp.exp(-x))


def _silu(x):
    return x * _sigmoid(x)


def _gelu_tanh(x):
    return 0.5 * x * (1.0 + jnp.tanh(0.7978845608028654 * (x + 0.044715 * (x * x * x))))


def _log_sigmoid(x):
    return jnp.minimum(x, 0.0) - jnp.log1p(jnp.exp(-jnp.abs(x)))


def _seq_blocks(n, t):
    if t >= ROW_TILE:
        assert t % ROW_TILE == 0
        return 1, ROW_TILE
    nb = ROW_TILE // t
    assert nb * t == ROW_TILE and n % nb == 0
    return nb, t


def _mod_kernel(c_ref, w_ref, b_ref, o_ref):
    c = _silu(c_ref[...])
    c_hi = c.astype(BF16)
    c_lo = (c - c_hi.astype(F32)).astype(BF16)
    w = w_ref[...]
    w_hi = w.astype(BF16)
    w_lo = (w - w_hi.astype(F32)).astype(BF16)
    o_ref[...] = _dot(c_hi, w_hi) + _dot(c_lo, w_hi) + _dot(c_hi, w_lo) + b_ref[...]


def _modulation(c, w_mod, b_mod):
    rows = c.shape[0]
    n = -(-rows // 8) * 8
    c = jnp.pad(c, ((0, n - rows), (0, 0)))
    d6 = w_mod.shape[1]
    return pl.pallas_call(
        _mod_kernel,
        grid=(d6 // D_MODEL,),
        in_specs=[pl.BlockSpec((n, D_MODEL), lambda j: (0, 0)),
                  pl.BlockSpec((D_MODEL, D_MODEL), lambda j: (0, j)),
                  pl.BlockSpec((1, D_MODEL), lambda j: (0, j))],
        out_specs=pl.BlockSpec((n, D_MODEL), lambda j: (0, j)),
        out_shape=jax.ShapeDtypeStruct((n, d6), F32),
        compiler_params=_params("arbitrary"),
        name="modulation",
    )(c, w_mod, b_mod.reshape(1, d6))[:rows]


def _inproj_kernel(x_ref, mod_ref, g_ref, w_ref, wf_ref, bf_ref,
                   xl_ref, gy_ref, qb_ref, kb_ref, vb_ref, k_ref, v_ref, lf_ref, lfp_ref):
    nb, tt, d = x_ref.shape
    x = x_ref[...]
    mod = mod_ref[...]
    hn = _rms(x, g_ref[...]) * (1.0 + mod[:, 1:2, :]) + mod[:, 0:1, :]
    hb = hn.reshape(nb * tt, d).astype(BF16)

    def proj(col):
        return _dot(hb, w_ref[:, col * WL:(col + 1) * WL]).reshape(nb, tt, WL)

    xl_ref[...] = proj(0)
    gy_ref[...] = _gelu_tanh(proj(1))
    qb_ref[...] = (proj(2) * (HEAD_DIM ** -0.5)).astype(BF16)
    k = proj(3)
    k_ref[...] = k
    kb_ref[...] = k.astype(BF16)
    v = proj(4)
    v_ref[...] = v
    vb_ref[...] = v.astype(BF16)
    fl = _dot(hb, wf_ref[...]) + bf_ref[...]
    lane = lax.broadcasted_iota(jnp.int32, fl.shape, 1)
    lf = jnp.where(lane < N_HEADS, _log_sigmoid(fl), 0.0).reshape(nb, tt, LANES)
    lfp_ref[...] = lf
    lf_ref[...] = lf[:, :, :N_HEADS]


def _inproj(x, mod, g_pre, w_main, w_f, b_f):
    n, t, d = x.shape
    nb, tt = _seq_blocks(n, t)
    blk = lambda w: pl.BlockSpec((nb, tt, w), lambda i, j: (i, j, 0))
    const = lambda shape: pl.BlockSpec(shape, lambda i, j: (0,) * len(shape))
    f32 = lambda w: jax.ShapeDtypeStruct((n, t, w), F32)
    b16 = lambda w: jax.ShapeDtypeStruct((n, t, w), BF16)
    return pl.pallas_call(
        _inproj_kernel,
        grid=(n // nb, t // tt),
        in_specs=[blk(d),
                  pl.BlockSpec((nb, 6, d), lambda i, j: (i, 0, 0)),
                  const((1, d)), const(w_main.shape), const(w_f.shape), const((1, LANES))],
        out_specs=[blk(WL), blk(WL), blk(WA), blk(WA), blk(WA), blk(WA), blk(WA), blk(N_HEADS), blk(LANES)],
        out_shape=[f32(WL), f32(WL), b16(WA), b16(WA), b16(WA), f32(WA), f32(WA), f32(N_HEADS), f32(LANES)],
        compiler_params=_params("parallel", "arbitrary"),
        name="inproj",
    )(x, mod, g_pre, w_main, w_f, b_f)


def _aug_lane(h):
    return HEAD_DIM if h % 2 == 0 else 0


def _inproj_prompt_kernel(x_ref, mod_ref, g_ref, w_ref, wf_ref, bf_ref, place_ref,
                          xl_ref, gy_ref, k_ref, v_ref, lf_ref, qt_ref, ka_ref, vt_ref, bpre_ref, carry_ref):
    _, tt, d = x_ref.shape

    @pl.when(pl.program_id(1) == 0)
    def _():
        carry_ref[...] = jnp.zeros_like(carry_ref)

    mod = mod_ref[0]
    hb = (_rms(x_ref[0], g_ref[...]) * (1.0 + mod[1:2, :]) + mod[0:1, :]).astype(BF16)

    def proj(col):
        return _dot(hb, w_ref[:, col * WL:(col + 1) * WL])

    xl_ref[0] = proj(0)
    gy_ref[0] = _gelu_tanh(proj(1))
    q = proj(2) * (HEAD_DIM ** -0.5)
    k = proj(3)
    k_ref[0] = k
    v = proj(4)
    v_ref[0] = v
    fl = _dot(hb, wf_ref[...]) + bf_ref[...]
    lane = lax.broadcasted_iota(jnp.int32, (tt, LANES), 1)
    lf = jnp.where(lane < N_HEADS, _log_sigmoid(fl), 0.0)
    lf_ref[0] = lf[:, :N_HEADS]

    row = lax.broadcasted_iota(jnp.int32, (tt, tt), 0)
    col = lax.broadcasted_iota(jnp.int32, (tt, tt), 1)
    tril = (col <= row).astype(BF16)
    hi, mid, lo = _split3(lf)
    e = _dot(tril, hi) + _dot(tril, mid) + _dot(tril, lo)
    bpre_ref[0, 0] = carry_ref[...]
    carry_ref[...] += e[tt - 1:tt, :]
    e_hi, e_mid, e_lo = _split3(-e)
    aug_k = _dot(e_hi, place_ref[0]) + _dot(e_mid, place_ref[1]) + _dot(e_lo, place_ref[2])

    for h in range(N_HEADS):
        pair = slice((h // 2) * PAIR_W, (h // 2 + 1) * PAIR_W)
        dims = (lane < HEAD_DIM) if h % 2 == 0 else (lane >= HEAD_DIM)
        a0 = _aug_lane(h)
        ones3 = ((lane >= a0) & (lane < a0 + 3)).astype(F32)
        one1 = (lane == a0).astype(F32)
        qt_ref[0, h] = (jnp.where(dims, q[:, pair], 0.0) + ones3).T.astype(BF16)
        ka_ref[0, h] = (jnp.where(dims, k[:, pair], 0.0) + aug_k[:, h * LANES:(h + 1) * LANES]).astype(BF16)
        vt_ref[0, h] = (jnp.where(dims, v[:, pair], 0.0) + one1).T.astype(BF16)


def _placement():
    pl_mat = np.zeros((3, LANES, N_HEADS * LANES), np.float32)
    for p in range(3):
        for h in range(N_HEADS):
            pl_mat[p, h, h * LANES + _aug_lane(h) + p] = 1.0
    return jnp.asarray(pl_mat, BF16)


def _inproj_prompt(x, mod, g_pre, w_main, w_f, b_f):
    n, t, d = x.shape
    tt = ROW_TILE
    assert t % tt == 0
    nblk = t // tt
    blk = lambda w: pl.BlockSpec((1, tt, w), lambda i, j: (i, j, 0))
    const = lambda shape: pl.BlockSpec(shape, lambda i, j: (0,) * len(shape))
    f32 = lambda w: jax.ShapeDtypeStruct((n, t, w), F32)
    place = _placement()
    return pl.pallas_call(
        _inproj_prompt_kernel,
        grid=(n, nblk),
        in_specs=[blk(d), pl.BlockSpec((1, 6, d), lambda i, j: (i, 0, 0)),
                  const((1, d)), const(w_main.shape), const(w_f.shape), const((1, LANES)), const(place.shape)],
        out_specs=[blk(WL), blk(WL), blk(WA), blk(WA), blk(N_HEADS),
                   pl.BlockSpec((1, N_HEADS, LANES, tt), lambda i, j: (i, 0, 0, j)),
                   pl.BlockSpec((1, N_HEADS, tt, LANES), lambda i, j: (i, 0, j, 0)),
                   pl.BlockSpec((1, N_HEADS, LANES, tt), lambda i, j: (i, 0, 0, j)),
                   pl.BlockSpec((1, 1, 1, LANES), lambda i, j: (i, j, 0, 0))],
        out_shape=[f32(WL), f32(WL), f32(WA), f32(WA), f32(N_HEADS),
                   jax.ShapeDtypeStruct((n, N_HEADS, LANES, t), BF16),
                   jax.ShapeDtypeStruct((n, N_HEADS, t, LANES), BF16),
                   jax.ShapeDtypeStruct((n, N_HEADS, LANES, t), BF16),
                   jax.ShapeDtypeStruct((n, nblk, 1, LANES), F32)],
        scratch_shapes=[pltpu.VMEM((1, LANES), F32)],
        compiler_params=_params("parallel", "arbitrary"),
        name="inproj_prompt",
    )(x, mod, g_pre, w_main, w_f, b_f, place)


def _expm1_neg(x):
    poly = x * (1.0 + x * (0.5 + x * (1.0 / 6.0 + x * (1.0 / 24.0 + x * (1.0 / 120.0)))))
    return jnp.where(x > -0.1, poly, jnp.exp(x) - 1.0)


def _lru_kernel(xl_ref, gy_ref, conv0_ref, h0_ref, cw_ref, cb_ref, wr_ref, br_ref, wi_ref, bi_ref,
                lam_ref, g_ref, out_ref, conv_ref, hlast_ref, tail_ref, carry_ref):
    nb, tt, w = xl_ref.shape
    j = pl.program_id(1)

    @pl.when(j == 0)
    def _():
        tail_ref[:, 8 - (CONV_W - 1):, :] = conv0_ref[...]
        carry_ref[...] = h0_ref[...]

    xl = xl_ref[...]
    xpad = jnp.concatenate([tail_ref[...], xl], axis=1)
    cw = cw_ref[...]
    xc = jnp.zeros_like(xl) + cb_ref[...]
    for k in range(CONV_W):
        off = 8 - (CONV_W - 1) + k
        xc = xc + xpad[:, off:off + tt, :] * cw[k:k + 1, :]
    conv_ref[...] = xpad[:, tt + 8 - (CONV_W - 1):, :]
    tail_ref[...] = xpad[:, tt:, :]

    m = nb * tt
    xf = xc.reshape(m, w)
    xb = xf.astype(BF16)
    r = _sigmoid(_dot(xb, wr_ref[...]) + br_ref[...])
    gi = _sigmoid(_dot(xb, wi_ref[...]) + bi_ref[...])
    lam = lam_ref[...]
    softplus = jnp.maximum(-lam, 0.0) + jnp.log1p(jnp.exp(-jnp.abs(lam)))
    log_a = (-LRU_C) * r * softplus
    a = jnp.exp(log_a)
    b = jnp.sqrt(-_expm1_neg(2.0 * log_a)) * (gi * xf)

    pos = lax.broadcasted_iota(jnp.int32, (m, w), 0) % tt
    d = 1
    while d < tt:
        keep = pos >= d
        a_prev = jnp.where(keep, pltpu.roll(a, d, 0), 1.0)
        b_prev = jnp.where(keep, pltpu.roll(b, d, 0), 0.0)
        b = a * b_prev + b
        a = a * a_prev
        d *= 2
    h = a.reshape(nb, tt, w) * carry_ref[...] + b.reshape(nb, tt, w)
    h_last = h[:, tt - 1:tt, :]
    carry_ref[...] = h_last
    hlast_ref[...] = h_last
    out_ref[...] = _rms(h * gy_ref[...], g_ref[...]).astype(BF16)


def _lru(xl, gy, conv0, h0, conv_w, conv_b, wr_bd, b_r, wi_bd, b_i, lam, g_lru):
    n, t, w = xl.shape
    nb, tt = _seq_blocks(n, t)
    blk = pl.BlockSpec((nb, tt, w), lambda i, j: (i, j, 0))
    per_seq = lambda rows: pl.BlockSpec((nb, rows, w), lambda i, j: (i, 0, 0))
    const = lambda shape: pl.BlockSpec(shape, lambda i, j: (0,) * len(shape))
    row = const((1, w))
    return pl.pallas_call(
        _lru_kernel,
        grid=(n // nb, t // tt),
        in_specs=[blk, blk, per_seq(CONV_W - 1), per_seq(1),
                  const((CONV_W, w)), row, const((w, w)), row, const((w, w)), row, row, row],
        out_specs=[blk, per_seq(CONV_W - 1), per_seq(1)],
        out_shape=[jax.ShapeDtypeStruct((n, t, w), BF16),
                   jax.ShapeDtypeStruct((n, CONV_W - 1, w), F32),
                   jax.ShapeDtypeStruct((n, 1, w), F32)],
        scratch_shapes=[pltpu.VMEM((nb, 8, w), F32), pltpu.VMEM((nb, 1, w), F32)],
        compiler_params=_params("parallel", "arbitrary"),
        name="rglru",
    )(xl, gy, conv0, h0, conv_w, conv_b, wr_bd, b_r, wi_bd, b_i, lam, g_lru)


def _cumsum_kernel(lf_ref, o_ref, carry_ref):
    tb = lf_ref.shape[1]

    @pl.when(pl.program_id(1) == 0)
    def _():
        carry_ref[...] = jnp.zeros_like(carry_ref)

    row = lax.broadcasted_iota(jnp.int32, (tb, tb), 0)
    col = lax.broadcasted_iota(jnp.int32, (tb, tb), 1)
    tril = (col <= row).astype(BF16)
    hi, mid, lo = _split3(lf_ref[0])
    d = _dot(tril, hi) + _dot(tril, mid) + _dot(tril, lo) + carry_ref[...]
    carry_ref[...] = d[tb - 1:tb, :]
    o_ref[0] = d.T[:N_HEADS, :]


def _cumsum_heads(lfp, tb):
    n, t, _ = lfp.shape
    return pl.pallas_call(
        _cumsum_kernel,
        grid=(n, t // tb),
        in_specs=[pl.BlockSpec((1, tb, LANES), lambda i, j: (i, j, 0))],
        out_specs=pl.BlockSpec((1, N_HEADS, tb), lambda i, j: (i, 0, j)),
        out_shape=jax.ShapeDtypeStruct((n, N_HEADS, t), F32),
        scratch_shapes=[pltpu.VMEM((1, LANES), F32)],
        compiler_params=_params("parallel", "arbitrary"),
        name="logf_cumsum",
    )(lfp)


def _pair_masks(x):
    lane = lax.broadcasted_iota(jnp.int32, x.shape, 1)
    zero = jnp.zeros_like(x)
    return jnp.where(lane < HEAD_DIM, x, zero), jnp.where(lane >= HEAD_DIM, x, zero)


def _online_update(s, m_prev, l_prev):
    m_new = jnp.maximum(m_prev, jnp.max(s, axis=1, keepdims=True))
    alpha = jnp.exp(m_prev - m_new)
    p = jnp.exp(s - m_new)
    l_new = alpha * l_prev + jnp.sum(p, axis=1, keepdims=True)
    return p, alpha, m_new, l_new


def _pair_step(q_a, q_b, k, v_a, v_b, bias_a, bias_b, mask, state):
    m_a, l_a, m_b, l_b, acc = state
    s_a = _dot_nt(q_a, k) + bias_a
    s_b = _dot_nt(q_b, k) + bias_b
    if mask is not None:
        s_a = jnp.where(mask, s_a, NEG_INF)
        s_b = jnp.where(mask, s_b, NEG_INF)
    p_a, al_a, m_a, l_a = _online_update(s_a, m_a, l_a)
    p_b, al_b, m_b, l_b = _online_update(s_b, m_b, l_b)
    lane = lax.broadcasted_iota(jnp.int32, acc.shape, 1)
    alpha = jnp.where(lane < HEAD_DIM, al_a, al_b)
    acc = alpha * acc + (_dot(p_a.astype(BF16), v_a) + _dot(p_b.astype(BF16), v_b))
    return m_a, l_a, m_b, l_b, acc


def _pair_finish(state):
    m_a, l_a, m_b, l_b, acc = state
    lane = lax.broadcasted_iota(jnp.int32, acc.shape, 1)
    return acc / jnp.where(lane < HEAD_DIM, l_a, l_b)


def _att_prompt_kernel(bpre_ref, qt_ref, ka_ref, vt_ref, o_ref, s_ref, p_ref, acc_ref):
    tq = qt_ref.shape[2]
    tk = ATT_TK
    nblk = ka_ref.shape[1] // tk
    b, hp, i = pl.program_id(0), pl.program_id(1), pl.program_id(2)
    q0 = i * tq
    jd = q0 // tk
    kpos = lax.broadcasted_iota(jnp.int32, (tk, tq), 0)
    qpos = lax.broadcasted_iota(jnp.int32, (tk, tq), 1)
    rows = lax.broadcasted_iota(jnp.int32, (LANES, tq), 0)
    base = [((b * N_PAIRS + hp) * 2 + hh) * nblk for hh in range(2)]

    def scores(j, masked):
        start = pl.multiple_of(j * tk, tk)
        col_max = []
        for hh in range(2):
            s = _dot(ka_ref[hh, pl.ds(start, tk), :], qt_ref[hh])
            if masked:
                s = jnp.where(kpos + start <= qpos + q0, s, NEG_INF)
            s_ref[hh] = s
            col_max.append(jnp.max(s, axis=0, keepdims=True))
        return tuple(col_max)

    def softmax_pv(j, col_max, m):
        start = pl.multiple_of(j * tk, tk)
        m_out = []
        for hh in range(2):
            c = bpre_ref[base[hh] + jd] - bpre_ref[base[hh] + j]
            m_new = jnp.maximum(m[hh], col_max[hh] + c)
            alpha = jnp.exp(m[hh] - m_new)
            shift = m_new - c
            for ch in range(tk // ATT_CHUNK):
                sl = slice(ch * ATT_CHUNK, (ch + 1) * ATT_CHUNK)
                p_ref[hh, sl, :] = jnp.exp(s_ref[hh, sl, :] - shift).astype(BF16)
            m_out.append((m_new, alpha))
        return tuple(m_out), start

    def accumulate(m_alpha, start):
        for hh in range(2):
            pv = _dot(vt_ref[hh, :, pl.ds(start, tk)], p_ref[hh])
            acc_ref[hh] = m_alpha[hh][1] * acc_ref[hh] + pv
        return tuple(ma[0] for ma in m_alpha)

    def step(j, next_masked, carry):
        col_max, m = carry
        m_alpha, start = softmax_pv(j, col_max, m)
        col_max_next = scores(j + 1, next_masked)
        return col_max_next, accumulate(m_alpha, start)

    acc_ref[...] = jnp.zeros_like(acc_ref)
    neg = jnp.full((1, tq), NEG_INF, F32)
    carry = (scores(0, True), (neg, neg))
    carry = lax.fori_loop(0, jd - 1, lambda j, cr: step(j, False, cr), carry)
    carry = lax.cond(jd > 0, lambda cr: step(jd - 1, True, cr), lambda cr: cr, carry)
    col_max, m = carry
    m_alpha, start = softmax_pv(jd, col_max, m)
    accumulate(m_alpha, start)

    acc_a, acc_b = acc_ref[0], acc_ref[1]
    out_a = acc_a / acc_a[_aug_lane(0):_aug_lane(0) + 1, :]
    out_b = acc_b / acc_b[_aug_lane(1):_aug_lane(1) + 1, :]
    o_ref[0] = jnp.where(rows < HEAD_DIM, out_a, out_b).T


def _att_prompt(qt, ka, vt, bpre):
    n, _, _, t = qt.shape
    assert t % ATT_TQ == 0 and ATT_TK % ATT_TQ == 0 and ATT_TK == ROW_TILE
    pair = lambda q: (N_PAIRS, 2) + q.shape[2:]
    qt, ka, vt = (a.reshape((n,) + pair(a)) for a in (qt, ka, vt))
    bflat = jnp.transpose(bpre[:, :, 0, :N_HEADS], (0, 2, 1)).reshape(-1)
    return pl.pallas_call(
        _att_prompt_kernel,
        grid=(n, N_PAIRS, t // ATT_TQ),
        in_specs=[pl.BlockSpec(memory_space=pltpu.SMEM),
                  pl.BlockSpec((None, None, 2, LANES, ATT_TQ), lambda b, h, i: (b, h, 0, 0, i)),
                  pl.BlockSpec((None, None, 2, t, LANES), lambda b, h, i: (b, h, 0, 0, 0),
                               pipeline_mode=pl.Buffered(1)),
                  pl.BlockSpec((None, None, 2, LANES, t), lambda b, h, i: (b, h, 0, 0, 0),
                               pipeline_mode=pl.Buffered(1))],
        out_specs=pl.BlockSpec((1, ATT_TQ, PAIR_W), lambda b, h, i: (b, i, h)),
        out_shape=jax.ShapeDtypeStruct((n, t, WA), F32),
        scratch_shapes=[pltpu.VMEM((2, ATT_TK, ATT_TQ), F32), pltpu.VMEM((2, ATT_TK, ATT_TQ), BF16),
                        pltpu.VMEM((2, LANES, ATT_TQ), F32)],
        compiler_params=_params("parallel", "parallel", "arbitrary"),
        name="att_prompt",
    )(bflat, qt, ka, vt)


def _att_sample_kernel(q_ref, kp_ref, vp_ref, dp_ref, dend_ref, kn_ref, vn_ref, dn_ref, o_ref,
                       m_ref, l_ref, acc_ref):
    t = q_ref.shape[1]
    j = pl.program_id(1)

    @pl.when(j == 0)
    def _():
        m_ref[...] = jnp.full_like(m_ref, NEG_INF)
        l_ref[...] = jnp.zeros_like(l_ref)
        acc_ref[...] = jnp.zeros_like(acc_ref)

    def pair(h, k, v, bias, mask):
        sl = slice(h * PAIR_W, (h + 1) * PAIR_W)
        q_a, q_b = _pair_masks(q_ref[0, :, sl])
        v_a, v_b = _pair_masks(v[:, sl].astype(BF16))
        state = (m_ref[2 * h], l_ref[2 * h], m_ref[2 * h + 1], l_ref[2 * h + 1], acc_ref[:, sl])
        state = _pair_step(q_a, q_b, k[:, sl].astype(BF16), v_a, v_b,
                           bias[2 * h:2 * h + 1], bias[2 * h + 1:2 * h + 2], mask, state)
        m_ref[2 * h], l_ref[2 * h], m_ref[2 * h + 1], l_ref[2 * h + 1], acc_ref[:, sl] = state
        return state

    bias_past = dend_ref[0] - dp_ref[0]
    for h in range(N_PAIRS):
        pair(h, kp_ref[0], vp_ref[0], bias_past, None)

    @pl.when(j == pl.num_programs(1) - 1)
    def _():
        bias_new = -dn_ref[0]
        qpos = lax.broadcasted_iota(jnp.int32, (t, t), 0)
        kpos = lax.broadcasted_iota(jnp.int32, (t, t), 1)
        for h in range(N_PAIRS):
            state = pair(h, kn_ref[0], vn_ref[0], bias_new, kpos <= qpos)
            o_ref[0, :, h * PAIR_W:(h + 1) * PAIR_W] = _pair_finish(state)


def _att_sample(qb, k_past, v_past, d_past, kb_new, vb_new, d_new):
    n, t, _ = qb.shape
    past = k_past.shape[1]
    assert past % SAMPLE_TK == 0
    d_end = d_past[:, :, past - 1:]
    new = lambda dt: pl.BlockSpec((1, t, WA), lambda b, j: (b, 0, 0))
    return pl.pallas_call(
        _att_sample_kernel,
        grid=(n, past // SAMPLE_TK),
        in_specs=[new(BF16),
                  pl.BlockSpec((1, SAMPLE_TK, WA), lambda b, j: (b, j, 0)),
                  pl.BlockSpec((1, SAMPLE_TK, WA), lambda b, j: (b, j, 0)),
                  pl.BlockSpec((1, N_HEADS, SAMPLE_TK), lambda b, j: (b, 0, j)),
                  pl.BlockSpec((1, N_HEADS, 1), lambda b, j: (b, 0, 0)),
                  new(BF16), new(BF16),
                  pl.BlockSpec((1, N_HEADS, t), lambda b, j: (b, 0, 0))],
        out_specs=pl.BlockSpec((1, t, WA), lambda b, j: (b, 0, 0)),
        out_shape=jax.ShapeDtypeStruct((n, t, WA), F32),
        scratch_shapes=[pltpu.VMEM((N_HEADS, t, 1), F32), pltpu.VMEM((N_HEADS, t, 1), F32),
                        pltpu.VMEM((t, WA), F32)],
        compiler_params=_params("parallel", "arbitrary"),
        name="att_sample",
    )(qb, k_past, v_past, d_past, d_end, kb_new, vb_new, d_new)


def _first_index_of_max(x, axis):
    mx = jnp.max(x, axis=axis, keepdims=True)
    idx = lax.broadcasted_iota(jnp.int32, x.shape, axis)
    first = jnp.min(jnp.where(x == mx, idx, x.shape[axis]), axis=axis, keepdims=True)
    return mx, idx == first


def _route(s, bias):
    m = s.shape[1]
    sb = (s + bias).reshape(N_GROUPS, GROUP_SIZE, m)
    top1, is_top1 = _first_index_of_max(sb, 1)
    top2 = jnp.max(jnp.where(is_top1, NEG_INF, sb), axis=1, keepdims=True)
    grp = (top1 + top2).reshape(N_GROUPS, m)
    gi = lax.broadcasted_iota(jnp.int32, (N_GROUPS, N_GROUPS, m), 0)
    gj = lax.broadcasted_iota(jnp.int32, (N_GROUPS, N_GROUPS, m), 1)
    other, mine = grp[None, :, :], grp[:, None, :]
    beats = (other > mine) | ((other == mine) & (gj < gi))
    g_rank = jnp.sum(beats.astype(jnp.int32), axis=1)
    g_keep = (g_rank < TOPK_GROUPS)[:, None, :]
    cand = jnp.where(g_keep, sb, NEG_INF).reshape(N_EXPERTS, m)
    picks = []
    for _ in range(TOP_K):
        _, pick = _first_index_of_max(cand, 0)
        picks.append(pick)
        cand = jnp.where(pick, NEG_INF, cand)
    w = jnp.concatenate([jnp.sum(jnp.where(pk, s, 0.0), axis=0, keepdims=True) for pk in picks], axis=0)
    return picks, w / jnp.sum(w, axis=0, keepdims=True) * ROUTED_SCALE


HALF_MASK = 0xFFFF0000


def _pack_halves(x):
    c = x.shape[1] // 2
    lo = lax.bitcast_convert_type(x[:, :c].astype(BF16).astype(F32), jnp.uint32) >> jnp.uint32(16)
    hi = lax.bitcast_convert_type(x[:, c:].astype(BF16).astype(F32), jnp.uint32) & jnp.uint32(HALF_MASK)
    return lo | hi


def _unpack_halves(w):
    lo = lax.bitcast_convert_type(w << jnp.uint32(16), F32)
    hi = lax.bitcast_convert_type(w & jnp.uint32(HALF_MASK), F32)
    return lo, hi


def _outproj_kernel(x_ref, lru_ref, att_ref, mod_ref, gatt_ref, wtop_ref, wbot_ref, gpost_ref, gpre_ref,
                    rwh_ref, rwl_ref, rb_ref, cnt_in_ref,
                    x1_ref, hf_ref, xw_ref, ids_ref, ranks_ref, gates_ref, cnt_out_ref, carry_ref):
    nb, tt, d = x_ref.shape
    m = nb * tt

    @pl.when((pl.program_id(0) == 0) & (pl.program_id(1) == 0))
    def _():
        carry_ref[...] = cnt_in_ref[...]

    mod = mod_ref[...]
    att_n = _rms(att_ref[...], gatt_ref[...]).reshape(m, WA).astype(BF16)
    mix = _dot(lru_ref[...].reshape(m, WL), wtop_ref[...]) + _dot(att_n, wbot_ref[...])
    x1 = x_ref[...] + mod[:, 2:3, :] * _rms(mix, gpost_ref[...]).reshape(nb, tt, d)
    x1_ref[...] = x1
    hf = (_rms(x1, gpre_ref[...]) * (1.0 + mod[:, 4:5, :]) + mod[:, 3:4, :]).reshape(m, d)
    hf_hi = hf.astype(BF16)
    hf_ref[...] = hf_hi.reshape(nb, tt, d)
    hf_lo = (hf - hf_hi.astype(F32)).astype(BF16)
    rwh = rwh_ref[...]
    logits = _dot_nt(rwh, hf_hi) + _dot_nt(rwh, hf_lo) + _dot_nt(rwl_ref[...], hf_hi)
    picks, gates = _route(_sigmoid(logits), rb_ref[...])
    xw_ref[...] = _pack_halves(hf).reshape(nb, tt, d // 2)

    sel = jnp.zeros((N_EXPERTS, m), F32)
    for pk in picks:
        sel = sel + pk.astype(F32)
    before = (lax.broadcasted_iota(jnp.int32, (m, m), 0) < lax.broadcasted_iota(jnp.int32, (m, m), 1)).astype(BF16)
    prior = _dot(sel.astype(BF16), before) + carry_ref[...]
    expert = lax.broadcasted_iota(jnp.int32, (N_EXPERTS, m), 0).astype(F32)
    take = lambda pk, v: jnp.sum(jnp.where(pk, v, 0.0), axis=0, keepdims=True)
    ids_ref[...] = jnp.concatenate([take(pk, expert) for pk in picks], axis=0).astype(jnp.int32)
    ranks_ref[...] = jnp.concatenate([take(pk, prior) for pk in picks], axis=0).astype(jnp.int32)
    carry_ref[...] += jnp.sum(sel, axis=1, keepdims=True)
    cnt_out_ref[...] = carry_ref[...]
    gates = jnp.concatenate([gates, jnp.zeros((LANES - TOP_K, m), F32)], axis=0)
    gates_ref[...] = gates.T.reshape(nb, tt, LANES)


def _outproj(x, lru_n, att, mod, g_att, w_top, w_bot, g_post, g_pre, rw_hi, rw_lo, r_bias, cnt_in):
    n, t, d = x.shape
    nb, tt = _seq_blocks(n, t)
    m = nb * tt
    steps_t = t // tt
    blk = lambda w: pl.BlockSpec((nb, tt, w), lambda i, j: (i, j, 0))
    const = lambda shape: pl.BlockSpec(shape, lambda i, j: (0,) * len(shape))
    per_tok = pl.BlockSpec((TOP_K, m), lambda i, j: (0, i * steps_t + j))
    return pl.pallas_call(
        _outproj_kernel,
        grid=(n // nb, steps_t),
        in_specs=[blk(d), blk(WL), blk(WA), pl.BlockSpec((nb, 6, d), lambda i, j: (i, 0, 0)),
                  const((1, WA)), const((WL, d)), const((WA, d)), const((1, d)), const((1, d)),
                  const((N_EXPERTS, d)), const((N_EXPERTS, d)), const((N_EXPERTS, 1)), const((N_EXPERTS, 1))],
        out_specs=[blk(d), blk(d), blk(d // 2), per_tok, per_tok, blk(LANES), const((N_EXPERTS, 1))],
        out_shape=[jax.ShapeDtypeStruct((n, t, d), F32), jax.ShapeDtypeStruct((n, t, d), BF16),
                   jax.ShapeDtypeStruct((n, t, d // 2), jnp.uint32),
                   jax.ShapeDtypeStruct((TOP_K, n * t), jnp.int32), jax.ShapeDtypeStruct((TOP_K, n * t), jnp.int32),
                   jax.ShapeDtypeStruct((n, t, LANES), F32), jax.ShapeDtypeStruct((N_EXPERTS, 1), F32)],
        scratch_shapes=[pltpu.VMEM((N_EXPERTS, 1), F32)],
        compiler_params=_params("arbitrary", "arbitrary"),
        name="outproj_router",
    )(x, lru_n, att, mod, g_att, w_top, w_bot, g_post, g_pre, rw_hi, rw_lo, r_bias, cnt_in)


def _row_copies(pos_ref, n_rows, make):
    def each(method):
        def body(t8, carry):
            for k in range(8):
                t = t8 * 8 + k
                for r in range(TOP_K):
                    getattr(make(r, t, pos_ref[r, t]), method)()
            return carry
        lax.fori_loop(0, n_rows // 8, body, 0)
    each("start")
    each("wait")


def _dispatch_kernel(pos_ref, x_ref, xs_ref, sem):
    m = x_ref.shape[0]
    _row_copies(pos_ref, m, lambda r, t, slot: pltpu.make_async_copy(
        x_ref.at[pl.ds(t, 1), :], xs_ref.at[pl.ds(slot, 1), :], sem))


def _dispatch(xw, pos, n_slots):
    m_tot, c = xw.shape
    return pl.pallas_call(
        _dispatch_kernel,
        grid=(m_tot // ROW_TILE,),
        in_specs=[pl.BlockSpec((TOP_K, ROW_TILE), lambda i: (0, i), memory_space=pltpu.SMEM),
                  pl.BlockSpec((ROW_TILE, c), lambda i: (i, 0))],
        out_specs=pl.BlockSpec(memory_space=pl.ANY),
        out_shape=jax.ShapeDtypeStruct((n_slots, c), jnp.uint32),
        scratch_shapes=[pltpu.SemaphoreType.DMA],
        compiler_params=_params("arbitrary"),
        name="moe_dispatch",
    )(pos, xw)


def _swiglu_halves(lo, hi, wg, wu, wd):
    c = lo.shape[1]
    hg = _dot(lo, wg[:c]) + _dot(hi, wg[c:])
    hu = _dot(lo, wu[:c]) + _dot(hi, wu[c:])
    return _dot((_silu(hg) * hu).astype(BF16), wd)


def _expert_kernel(te_ref, valid_ref, x_ref, wg_ref, wu_ref, wd_ref, y_ref):
    i = pl.program_id(0)
    valid = valid_ref[i]

    @pl.when(valid > 0)
    def _():
        w = x_ref[...]
        row = lax.broadcasted_iota(jnp.int32, w.shape, 0)
        lo, hi = _unpack_halves(jnp.where(row < valid, w, jnp.uint32(0)))
        y_ref[...] = _pack_halves(_swiglu_halves(lo.astype(BF16), hi.astype(BF16), wg_ref[0], wu_ref[0], wd_ref[0]))


def _experts(xs, tile_expert, tile_valid, wg, wu, wd):
    n_slots, c = xs.shape
    n_tiles = n_slots // EXPERT_TILE
    d = 2 * c
    rows = pl.BlockSpec((EXPERT_TILE, c), lambda i, te, tv: (i, 0))
    weight = lambda shape: pl.BlockSpec((1,) + shape, lambda i, te, tv: (te[i], 0, 0))
    return pl.pallas_call(
        _expert_kernel,
        grid_spec=pltpu.PrefetchScalarGridSpec(
            num_scalar_prefetch=2, grid=(n_tiles,),
            in_specs=[rows, weight((d, D_EXPERT)), weight((d, D_EXPERT)), weight((D_EXPERT, d))],
            out_specs=rows),
        out_shape=jax.ShapeDtypeStruct((n_slots, c), jnp.uint32),
        compiler_params=_params("arbitrary"),
        name="moe_experts",
    )(tile_expert, tile_valid, xs, wg, wu, wd)


def _combine_kernel(pos_ref, ys_ref, gates_ref, hf_ref, x1_ref, mod_ref, sg_ref, su_ref, sd_ref, gpost_ref,
                    y_ref, buf_ref, sem):
    nb, tt, d = hf_ref.shape
    m = nb * tt
    c = d // 2
    _row_copies(pos_ref, m, lambda r, t, slot: pltpu.make_async_copy(
        ys_ref.at[pl.ds(slot, 1), :], buf_ref.at[r, pl.ds(t, 1), :], sem))
    x = hf_ref[...].reshape(m, d)
    shared = _swiglu_halves(x[:, :c], x[:, c:], sg_ref[...], su_ref[...], sd_ref[...])
    gates = gates_ref[...].reshape(m, LANES)
    acc_lo = shared[:, :c]
    acc_hi = shared[:, c:]
    for r in range(TOP_K):
        lo, hi = _unpack_halves(buf_ref[r])
        g = gates[:, r:r + 1]
        acc_lo = acc_lo + g * lo
        acc_hi = acc_hi + g * hi
    z = _rms(jnp.concatenate([acc_lo, acc_hi], axis=1), gpost_ref[...]).reshape(nb, tt, d)
    y_ref[...] = x1_ref[...] + mod_ref[...][:, 5:6, :] * z


def _combine(pos, ys, gates_t, hf, x1, mod, sg, su, sd, g_post):
    n, t, d = hf.shape
    nb, tt = _seq_blocks(n, t)
    m = nb * tt
    steps_t = t // tt
    blk = lambda w: pl.BlockSpec((nb, tt, w), lambda i, j: (i, j, 0))
    const = lambda shape: pl.BlockSpec(shape, lambda i, j: (0,) * len(shape))
    return pl.pallas_call(
        _combine_kernel,
        grid=(n // nb, steps_t),
        in_specs=[pl.BlockSpec((TOP_K, m), lambda i, j: (0, i * steps_t + j), memory_space=pltpu.SMEM),
                  pl.BlockSpec(memory_space=pl.ANY),
                  blk(LANES), blk(d), blk(d), pl.BlockSpec((nb, 6, d), lambda i, j: (i, 0, 0)),
                  const((d, D_EXPERT)), const((d, D_EXPERT)), const((D_EXPERT, d)), const((1, d))],
        out_specs=blk(d),
        out_shape=jax.ShapeDtypeStruct((n, t, d), F32),
        scratch_shapes=[pltpu.VMEM((TOP_K, m, d // 2), jnp.uint32), pltpu.SemaphoreType.DMA],
        compiler_params=_params("arbitrary", "arbitrary"),
        name="moe_combine",
    )(pos, ys, gates_t, hf, x1, mod, sg, su, sd, g_post)


def _slots_kernel(starts_ref, ids_ref, ranks_ref, pos_ref):
    ids = ids_ref[...]

    def add_start(e, pos):
        return pos + jnp.where(ids == e, starts_ref[e], 0)

    pos_ref[...] = lax.fori_loop(0, N_EXPERTS, add_start, ranks_ref[...])


def _slots(starts, ids, ranks):
    k, m_tot = ids.shape
    blk = pl.BlockSpec((k, SLOT_COLS), lambda i: (0, i))
    return pl.pallas_call(
        _slots_kernel,
        grid=(m_tot // SLOT_COLS,),
        in_specs=[pl.BlockSpec(memory_space=pltpu.SMEM), blk, blk],
        out_specs=blk,
        out_shape=jax.ShapeDtypeStruct((k, m_tot), jnp.int32),
        compiler_params=_params("arbitrary"),
        name="moe_slots",
    )(starts, ids, ranks)


def _slot_plan(ids, ranks, counts):
    n_pairs = ids.shape[0] * ids.shape[1]
    n_tiles = -(-(n_pairs + N_EXPERTS * (EXPERT_TILE - 1)) // EXPERT_TILE)
    cnt = counts.reshape(N_EXPERTS).astype(jnp.int32)
    padded = (cnt + EXPERT_TILE - 1) // EXPERT_TILE * EXPERT_TILE
    ends = jnp.cumsum(padded)
    starts = ends - padded
    pos = _slots(starts, ids, ranks)
    tile_start = jnp.arange(n_tiles, dtype=jnp.int32) * EXPERT_TILE
    in_expert = (tile_start[:, None] >= starts[None, :]) & (tile_start[:, None] < ends[None, :])
    tile_expert = jnp.sum(jnp.where(in_expert, jnp.arange(N_EXPERTS, dtype=jnp.int32)[None, :], 0), axis=1)
    tile_fill = jnp.sum(jnp.where(in_expert, (starts + cnt)[None, :] - tile_start[:, None], 0), axis=1)
    tile_valid = jnp.clip(tile_fill, 0, EXPERT_TILE).astype(jnp.int32)
    return pos, tile_expert.astype(jnp.int32), tile_valid, n_tiles * EXPERT_TILE


def _block_diag(w):
    g, bw, _ = w.shape
    eye = jnp.eye(g, dtype=w.dtype)
    return (eye[:, None, :, None] * w[:, :, None, :]).reshape(g * bw, g * bw)


def _prep_weights(p):
    d_main = 2 * WL + 3 * WA
    w_in = p["w_in"]
    rw_t = p["router_w"].T
    rw_hi = rw_t.astype(BF16)
    row = lambda v: v.reshape(1, -1)
    return dict(
        w_mod=p["w_mod"], b_mod=p["b_mod"],
        g_pre_mix=row(p["g_pre_mix"]), g_post_mix=row(p["g_post_mix"]),
        g_pre_ffn=row(p["g_pre_ffn"]), g_post_ffn=row(p["g_post_ffn"]),
        w_main=w_in[:, :d_main].astype(BF16),
        w_f=jnp.pad(w_in[:, d_main:], ((0, 0), (0, LANES - N_HEADS))).astype(BF16),
        b_f=jnp.pad(p["b_f"], (0, LANES - N_HEADS)).reshape(1, LANES),
        conv_w=p["conv_w"], conv_b=row(p["conv_b"]),
        wr_bd=_block_diag(p["w_r"]).astype(BF16), b_r=row(p["b_r"]),
        wi_bd=_block_diag(p["w_i"]).astype(BF16), b_i=row(p["b_i"]),
        lam=row(p["lru_lambda"]), g_lru=row(p["g_lru_out"]), g_att=row(p["g_att_out"]),
        w_top=p["w_out"][:WL].astype(BF16), w_bot=p["w_out"][WL:].astype(BF16),
        rw_hi=rw_hi, rw_lo=(rw_t - rw_hi.astype(F32)).astype(BF16),
        r_bias=p["router_bias"].reshape(N_EXPERTS, 1),
        wg=p["w_gate"].astype(BF16), wu=p["w_up"].astype(BF16), wd=p["w_down"].astype(BF16),
        sg=p["ws_gate"].astype(BF16), su=p["ws_up"].astype(BF16), sd=p["ws_down"].astype(BF16),
    )


def _mixers(x, mod, conv0, h0, past, w, cnt_in):
    n, t, _ = x.shape
    proj_args = (x, mod, w["g_pre_mix"], w["w_main"], w["w_f"], w["b_f"])
    if past is None:
        xl, gy, k, v, lf, qt, ka, vt, bpre = _inproj_prompt(*proj_args)
        att = _att_prompt(qt, ka, vt, bpre)
    else:
        xl, gy, qb, kb, vb, k, v, lf, lfp = _inproj(*proj_args)
        d_new = _cumsum_heads(lfp, t)
        k_past, v_past, lf_past = past
        plen = k_past.shape[1]
        lf_past_p = jnp.pad(lf_past, ((0, 0), (0, 0), (0, LANES - N_HEADS)))
        d_past = _cumsum_heads(lf_past_p, ROW_TILE)
        att = _att_sample(qb, k_past.reshape(n, plen, WA), v_past.reshape(n, plen, WA), d_past, kb, vb, d_new)
    lru_n, conv_new, h_new = _lru(xl, gy, conv0, h0.reshape(n, 1, WL), w["conv_w"], w["conv_b"],
                                  w["wr_bd"], w["b_r"], w["wi_bd"], w["b_i"], w["lam"], w["g_lru"])
    routed = _outproj(x, lru_n, att, mod, w["g_att"], w["w_top"], w["w_bot"], w["g_post_mix"],
                      w["g_pre_ffn"], w["rw_hi"], w["rw_lo"], w["r_bias"], cnt_in)
    state = (k.reshape(n, t, N_HEADS, HEAD_DIM), v.reshape(n, t, N_HEADS, HEAD_DIM), lf,
             conv_new, h_new.reshape(n, WL))
    return routed, state


def _layer(xp, xs, mod_p, mod_s, conv_s, h_s, past_s, w):
    n_p = xp.shape[0]
    conv0 = jnp.zeros((n_p, CONV_W - 1, WL), F32)
    h0 = jnp.zeros((n_p, WL), F32)
    zero_cnt = jnp.zeros((N_EXPERTS, 1), F32)
    (x1_p, hf_p, xw_p, ids_p, rk_p, g_p, cnt_p), st_p = _mixers(xp, mod_p, conv0, h0, None, w, zero_cnt)
    (x1_s, hf_s, xw_s, ids_s, rk_s, g_s, cnt), st_s = _mixers(xs, mod_s, conv_s, h_s, past_s, w, cnt_p)

    half = xw_p.shape[-1]
    xw = jnp.concatenate([xw_p.reshape(-1, half), xw_s.reshape(-1, half)], axis=0)
    ids = jnp.concatenate([ids_p, ids_s], axis=1)
    ranks = jnp.concatenate([rk_p, rk_s], axis=1)
    pos, tile_expert, tile_valid, n_slots = _slot_plan(ids, ranks, cnt)
    ys = _experts(_dispatch(xw, pos, n_slots), tile_expert, tile_valid, w["wg"], w["wu"], w["wd"])
    m_p = ids_p.shape[1]
    shared = (w["sg"], w["su"], w["sd"], w["g_post_ffn"])
    yp = _combine(pos[:, :m_p], ys, g_p, hf_p, x1_p, mod_p, *shared)
    ysmp = _combine(pos[:, m_p:], ys, g_s, hf_s, x1_s, mod_s, *shared)
    return yp, ysmp, st_p, st_s


def kernel(x_prompt, x_sample, c_prompt, c_sample, cache_k, cache_v, cache_logf, state_conv, state_lru, w_mod, b_mod, g_pre_mix, g_post_mix, g_pre_ffn, g_post_ffn, w_in, conv_w, conv_b, w_r, b_r, w_i, b_i, lru_lambda, b_f, g_lru_out, g_att_out, w_out, router_w, router_bias, w_gate, w_up, w_down, ws_gate, ws_up, ws_down):
    names = ("w_mod", "b_mod", "g_pre_mix", "g_post_mix", "g_pre_ffn", "g_post_ffn", "w_in", "conv_w", "conv_b",
             "w_r", "b_r", "w_i", "b_i", "lru_lambda", "b_f", "g_lru_out", "g_att_out", "w_out", "router_w",
             "router_bias", "w_gate", "w_up", "w_down", "ws_gate", "ws_up", "ws_down")
    stacked = (w_mod, b_mod, g_pre_mix, g_post_mix, g_pre_ffn, g_post_ffn, w_in, conv_w, conv_b, w_r, b_r, w_i, b_i,
               lru_lambda, b_f, g_lru_out, g_att_out, w_out, router_w, router_bias, w_gate, w_up, w_down,
               ws_gate, ws_up, ws_down)
    depth = w_mod.shape[0]
    n_p, n_s = x_prompt.shape[0], x_sample.shape[0]
    yp, ys = x_prompt, x_sample
    st_p, st_s = [], []
    for l in range(depth):
        w = _prep_weights({k: v[l] for k, v in zip(names, stacked)})
        mod = _modulation(jnp.concatenate([c_prompt, c_sample], axis=0), w["w_mod"], w["b_mod"])
        mod = mod.reshape(n_p + n_s, 6, D_MODEL)
        yp, ys, sp, ss = _layer(yp, ys, mod[:n_p], mod[n_p:], state_conv[l], state_lru[l],
                                (cache_k[l], cache_v[l], cache_logf[l]), w)
        st_p.append(sp)
        st_s.append(ss)
    stack = lambda sts, i: jnp.stack([s[i] for s in sts])
    return (yp, ys) + tuple(stack(st_p, i) for i in range(5)) + tuple(stack(st_s, i) for i in range(5))
```

```python
import functools

import jax
import jax.numpy as jnp
import numpy as np
from jax import lax
from jax.experimental import pallas as pl
from jax.experimental.pallas import tpu as pltpu
from jax.experimental.pallas import tpu_sc as plsc

F32 = jnp.float32
BF16 = jnp.bfloat16

D_MODEL = 1024
WL = 512
WA = 512
N_HEADS = 8
HEAD_DIM = 64
N_PAIRS = N_HEADS // 2
PAIR_W = 2 * HEAD_DIM
LANES = 128
CONV_W = 4
LRU_BLOCKS = 8
LRU_C = 8.0
N_EXPERTS = 64
N_GROUPS = 8
GROUP_SIZE = N_EXPERTS // N_GROUPS
TOPK_GROUPS = 4
TOP_K = 8
D_EXPERT = 256
ROUTED_SCALE = 2.5
EPS = 1e-6
NEG_INF = float("-inf")

ROW_TILE = 512
ATT_TQ = 512
ATT_TK = 512
ATT_CHUNK = 64
SAMPLE_TK = 1024
EXPERT_TILE = 512
SLOT_COLS = 2048
GATHER_ROWS = 64
VMEM_LIMIT = 56 * 1024 * 1024


def _params(*sem):
    return pltpu.CompilerParams(dimension_semantics=sem, vmem_limit_bytes=VMEM_LIMIT)


def _dot(a, b):
    return jnp.dot(a, b, preferred_element_type=F32)


def _dot_nt(a, b):
    return lax.dot_general(a, b, (((1,), (1,)), ((), ())), preferred_element_type=F32)


def _split3(x):
    hi = x.astype(BF16)
    r1 = x - hi.astype(F32)
    mid = r1.astype(BF16)
    lo = (r1 - mid.astype(F32)).astype(BF16)
    return hi, mid, lo


def _rms(x, g):
    return x * lax.rsqrt(jnp.mean(x * x, axis=-1, keepdims=True) + EPS) * g


def _sigmoid(x):
    return 1.0 / (1.0 + jnp.exp(-x))


def _silu(x):
    return x * _sigmoid(x)


def _gelu_tanh(x):
    return 0.5 * x * (1.0 + jnp.tanh(0.7978845608028654 * (x + 0.044715 * (x * x * x))))


def _log_sigmoid(x):
    return jnp.minimum(x, 0.0) - jnp.log1p(jnp.exp(-jnp.abs(x)))


def _seq_blocks(n, t):
    if t >= ROW_TILE:
        assert t % ROW_TILE == 0
        return 1, ROW_TILE
    nb = ROW_TILE // t
    assert nb * t == ROW_TILE and n % nb == 0
    return nb, t


def _mod_kernel(c_ref, w_ref, b_ref, o_ref):
    c = _silu(c_ref[...])
    c_hi = c.astype(BF16)
    c_lo = (c - c_hi.astype(F32)).astype(BF16)
    w = w_ref[...]
    w_hi = w.astype(BF16)
    w_lo = (w - w_hi.astype(F32)).astype(BF16)
    o_ref[...] = _dot(c_hi, w_hi) + _dot(c_lo, w_hi) + _dot(c_hi, w_lo) + b_ref[...]


def _modulation(c, w_mod, b_mod):
    rows = c.shape[0]
    n = -(-rows // 8) * 8
    c = jnp.pad(c, ((0, n - rows), (0, 0)))
    d6 = w_mod.shape[1]
    return pl.pallas_call(
        _mod_kernel,
        grid=(d6 // D_MODEL,),
        in_specs=[pl.BlockSpec((n, D_MODEL), lambda j: (0, 0)),
                  pl.BlockSpec((D_MODEL, D_MODEL), lambda j: (0, j)),
                  pl.BlockSpec((1, D_MODEL), lambda j: (0, j))],
        out_specs=pl.BlockSpec((n, D_MODEL), lambda j: (0, j)),
        out_shape=jax.ShapeDtypeStruct((n, d6), F32),
        compiler_params=_params("arbitrary"),
        name="modulation",
    )(c, w_mod, b_mod.reshape(1, d6))[:rows]


def _inproj_kernel(x_ref, mod_ref, g_ref, w_ref, wf_ref, bf_ref,
                   xl_ref, gy_ref, qb_ref, kb_ref, vb_ref, k_ref, v_ref, lf_ref, lfp_ref):
    nb, tt, d = x_ref.shape
    x = x_ref[...]
    mod = mod_ref[...]
    hn = _rms(x, g_ref[...]) * (1.0 + mod[:, 1:2, :]) + mod[:, 0:1, :]
    hb = hn.reshape(nb * tt, d).astype(BF16)

    def proj(col):
        return _dot(hb, w_ref[:, col * WL:(col + 1) * WL]).reshape(nb, tt, WL)

    xl_ref[...] = proj(0)
    gy_ref[...] = _gelu_tanh(proj(1))
    qb_ref[...] = (proj(2) * (HEAD_DIM ** -0.5)).astype(BF16)
    k = proj(3)
    k_ref[...] = k
    kb_ref[...] = k.astype(BF16)
    v = proj(4)
    v_ref[...] = v
    vb_ref[...] = v.astype(BF16)
    fl = _dot(hb, wf_ref[...]) + bf_ref[...]
    lane = lax.broadcasted_iota(jnp.int32, fl.shape, 1)
    lf = jnp.where(lane < N_HEADS, _log_sigmoid(fl), 0.0).reshape(nb, tt, LANES)
    lfp_ref[...] = lf
    lf_ref[...] = lf[:, :, :N_HEADS]


def _inproj(x, mod, g_pre, w_main, w_f, b_f):
    n, t, d = x.shape
    nb, tt = _seq_blocks(n, t)
    blk = lambda w: pl.BlockSpec((nb, tt, w), lambda i, j: (i, j, 0))
    const = lambda shape: pl.BlockSpec(shape, lambda i, j: (0,) * len(shape))
    f32 = lambda w: jax.ShapeDtypeStruct((n, t, w), F32)
    b16 = lambda w: jax.ShapeDtypeStruct((n, t, w), BF16)
    return pl.pallas_call(
        _inproj_kernel,
        grid=(n // nb, t // tt),
        in_specs=[blk(d),
                  pl.BlockSpec((nb, 6, d), lambda i, j: (i, 0, 0)),
                  const((1, d)), const(w_main.shape), const(w_f.shape), const((1, LANES))],
        out_specs=[blk(WL), blk(WL), blk(WA), blk(WA), blk(WA), blk(WA), blk(WA), blk(N_HEADS), blk(LANES)],
        out_shape=[f32(WL), f32(WL), b16(WA), b16(WA), b16(WA), f32(WA), f32(WA), f32(N_HEADS), f32(LANES)],
        compiler_params=_params("parallel", "arbitrary"),
        name="inproj",
    )(x, mod, g_pre, w_main, w_f, b_f)


def _aug_lane(h):
    return HEAD_DIM if h % 2 == 0 else 0


def _inproj_prompt_kernel(x_ref, mod_ref, g_ref, w_ref, wf_ref, bf_ref, place_ref,
                          xl_ref, gy_ref, k_ref, v_ref, lf_ref, qt_ref, ka_ref, vt_ref, bpre_ref, carry_ref):
    _, tt, d = x_ref.shape

    @pl.when(pl.program_id(1) == 0)
    def _():
        carry_ref[...] = jnp.zeros_like(carry_ref)

    mod = mod_ref[0]
    hb = (_rms(x_ref[0], g_ref[...]) * (1.0 + mod[1:2, :]) + mod[0:1, :]).astype(BF16)

    def proj(col):
        return _dot(hb, w_ref[:, col * WL:(col + 1) * WL])

    xl_ref[0] = proj(0)
    gy_ref[0] = _gelu_tanh(proj(1))
    q = proj(2) * (HEAD_DIM ** -0.5)
    k = proj(3)
    k_ref[0] = k
    v = proj(4)
    v_ref[0] = v
    fl = _dot(hb, wf_ref[...]) + bf_ref[...]
    lane = lax.broadcasted_iota(jnp.int32, (tt, LANES), 1)
    lf = jnp.where(lane < N_HEADS, _log_sigmoid(fl), 0.0)
    lf_ref[0] = lf[:, :N_HEADS]

    row = lax.broadcasted_iota(jnp.int32, (tt, tt), 0)
    col = lax.broadcasted_iota(jnp.int32, (tt, tt), 1)
    tril = (col <= row).astype(BF16)
    hi, mid, lo = _split3(lf)
    e = _dot(tril, hi) + _dot(tril, mid) + _dot(tril, lo)
    bpre_ref[0, 0] = carry_ref[...]
    carry_ref[...] += e[tt - 1:tt, :]
    e_hi, e_mid, e_lo = _split3(-e)
    aug_k = _dot(e_hi, place_ref[0]) + _dot(e_mid, place_ref[1]) + _dot(e_lo, place_ref[2])

    for h in range(N_HEADS):
        pair = slice((h // 2) * PAIR_W, (h // 2 + 1) * PAIR_W)
        dims = (lane < HEAD_DIM) if h % 2 == 0 else (lane >= HEAD_DIM)
        a0 = _aug_lane(h)
        ones3 = ((lane >= a0) & (lane < a0 + 3)).astype(F32)
        one1 = (lane == a0).astype(F32)
        qt_ref[0, h] = (jnp.where(dims, q[:, pair], 0.0) + ones3).T.astype(BF16)
        ka_ref[0, h] = (jnp.where(dims, k[:, pair], 0.0) + aug_k[:, h * LANES:(h + 1) * LANES]).astype(BF16)
        vt_ref[0, h] = (jnp.where(dims, v[:, pair], 0.0) + one1).T.astype(BF16)


def _placement():
    pl_mat = np.zeros((3, LANES, N_HEADS * LANES), np.float32)
    for p in range(3):
        for h in range(N_HEADS):
            pl_mat[p, h, h * LANES + _aug_lane(h) + p] = 1.0
    return jnp.asarray(pl_mat, BF16)


def _inproj_prompt(x, mod, g_pre, w_main, w_f, b_f):
    n, t, d = x.shape
    tt = ROW_TILE
    assert t % tt == 0
    nblk = t // tt
    blk = lambda w: pl.BlockSpec((1, tt, w), lambda i, j: (i, j, 0))
    const = lambda shape: pl.BlockSpec(shape, lambda i, j: (0,) * len(shape))
    f32 = lambda w: jax.ShapeDtypeStruct((n, t, w), F32)
    place = _placement()
    return pl.pallas_call(
        _inproj_prompt_kernel,
        grid=(n, nblk),
        in_specs=[blk(d), pl.BlockSpec((1, 6, d), lambda i, j: (i, 0, 0)),
                  const((1, d)), const(w_main.shape), const(w_f.shape), const((1, LANES)), const(place.shape)],
        out_specs=[blk(WL), blk(WL), blk(WA), blk(WA), blk(N_HEADS),
                   pl.BlockSpec((1, N_HEADS, LANES, tt), lambda i, j: (i, 0, 0, j)),
                   pl.BlockSpec((1, N_HEADS, tt, LANES), lambda i, j: (i, 0, j, 0)),
                   pl.BlockSpec((1, N_HEADS, LANES, tt), lambda i, j: (i, 0, 0, j)),
                   pl.BlockSpec((1, 1, 1, LANES), lambda i, j: (i, j, 0, 0))],
        out_shape=[f32(WL), f32(WL), f32(WA), f32(WA), f32(N_HEADS),
                   jax.ShapeDtypeStruct((n, N_HEADS, LANES, t), BF16),
                   jax.ShapeDtypeStruct((n, N_HEADS, t, LANES), BF16),
                   jax.ShapeDtypeStruct((n, N_HEADS, LANES, t), BF16),
                   jax.ShapeDtypeStruct((n, nblk, 1, LANES), F32)],
        scratch_shapes=[pltpu.VMEM((1, LANES), F32)],
        compiler_params=_params("parallel", "arbitrary"),
        name="inproj_prompt",
    )(x, mod, g_pre, w_main, w_f, b_f, place)


def _expm1_neg(x):
    poly = x * (1.0 + x * (0.5 + x * (1.0 / 6.0 + x * (1.0 / 24.0 + x * (1.0 / 120.0)))))
    return jnp.where(x > -0.1, poly, jnp.exp(x) - 1.0)


def _lru_kernel(xl_ref, gy_ref, conv0_ref, h0_ref, cw_ref, cb_ref, wr_ref, br_ref, wi_ref, bi_ref,
                lam_ref, g_ref, out_ref, conv_ref, hlast_ref, tail_ref, carry_ref):
    nb, tt, w = xl_ref.shape
    j = pl.program_id(1)

    @pl.when(j == 0)
    def _():
        tail_ref[:, 8 - (CONV_W - 1):, :] = conv0_ref[...]
        carry_ref[...] = h0_ref[...]

    xl = xl_ref[...]
    xpad = jnp.concatenate([tail_ref[...], xl], axis=1)
    cw = cw_ref[...]
    xc = jnp.zeros_like(xl) + cb_ref[...]
    for k in range(CONV_W):
        off = 8 - (CONV_W - 1) + k
        xc = xc + xpad[:, off:off + tt, :] * cw[k:k + 1, :]
    conv_ref[...] = xpad[:, tt + 8 - (CONV_W - 1):, :]
    tail_ref[...] = xpad[:, tt:, :]

    m = nb * tt
    xf = xc.reshape(m, w)
    xb = xf.astype(BF16)
    r = _sigmoid(_dot(xb, wr_ref[...]) + br_ref[...])
    gi = _sigmoid(_dot(xb, wi_ref[...]) + bi_ref[...])
    lam = lam_ref[...]
    softplus = jnp.maximum(-lam, 0.0) + jnp.log1p(jnp.exp(-jnp.abs(lam)))
    log_a = (-LRU_C) * r * softplus
    a = jnp.exp(log_a)
    b = jnp.sqrt(-_expm1_neg(2.0 * log_a)) * (gi * xf)

    pos = lax.broadcasted_iota(jnp.int32, (m, w), 0) % tt
    d = 1
    while d < tt:
        keep = pos >= d
        a_prev = jnp.where(keep, pltpu.roll(a, d, 0), 1.0)
        b_prev = jnp.where(keep, pltpu.roll(b, d, 0), 0.0)
        b = a * b_prev + b
        a = a * a_prev
        d *= 2
    h = a.reshape(nb, tt, w) * carry_ref[...] + b.reshape(nb, tt, w)
    h_last = h[:, tt - 1:tt, :]
    carry_ref[...] = h_last
    hlast_ref[...] = h_last
    out_ref[...] = _rms(h * gy_ref[...], g_ref[...]).astype(BF16)


def _lru(xl, gy, conv0, h0, conv_w, conv_b, wr_bd, b_r, wi_bd, b_i, lam, g_lru):
    n, t, w = xl.shape
    nb, tt = _seq_blocks(n, t)
    blk = pl.BlockSpec((nb, tt, w), lambda i, j: (i, j, 0))
    per_seq = lambda rows: pl.BlockSpec((nb, rows, w), lambda i, j: (i, 0, 0))
    const = lambda shape: pl.BlockSpec(shape, lambda i, j: (0,) * len(shape))
    row = const((1, w))
    return pl.pallas_call(
        _lru_kernel,
        grid=(n // nb, t // tt),
        in_specs=[blk, blk, per_seq(CONV_W - 1), per_seq(1),
                  const((CONV_W, w)), row, const((w, w)), row, const((w, w)), row, row, row],
        out_specs=[blk, per_seq(CONV_W - 1), per_seq(1)],
        out_shape=[jax.ShapeDtypeStruct((n, t, w), BF16),
                   jax.ShapeDtypeStruct((n, CONV_W - 1, w), F32),
                   jax.ShapeDtypeStruct((n, 1, w), F32)],
        scratch_shapes=[pltpu.VMEM((nb, 8, w), F32), pltpu.VMEM((nb, 1, w), F32)],
        compiler_params=_params("parallel", "arbitrary"),
        name="rglru",
    )(xl, gy, conv0, h0, conv_w, conv_b, wr_bd, b_r, wi_bd, b_i, lam, g_lru)


def _cumsum_kernel(lf_ref, o_ref, carry_ref):
    tb = lf_ref.shape[1]

    @pl.when(pl.program_id(1) == 0)
    def _():
        carry_ref[...] = jnp.zeros_like(carry_ref)

    row = lax.broadcasted_iota(jnp.int32, (tb, tb), 0)
    col = lax.broadcasted_iota(jnp.int32, (tb, tb), 1)
    tril = (col <= row).astype(BF16)
    hi, mid, lo = _split3(lf_ref[0])
    d = _dot(tril, hi) + _dot(tril, mid) + _dot(tril, lo) + carry_ref[...]
    carry_ref[...] = d[tb - 1:tb, :]
    o_ref[0] = d.T[:N_HEADS, :]


def _cumsum_heads(lfp, tb):
    n, t, _ = lfp.shape
    return pl.pallas_call(
        _cumsum_kernel,
        grid=(n, t // tb),
        in_specs=[pl.BlockSpec((1, tb, LANES), lambda i, j: (i, j, 0))],
        out_specs=pl.BlockSpec((1, N_HEADS, tb), lambda i, j: (i, 0, j)),
        out_shape=jax.ShapeDtypeStruct((n, N_HEADS, t), F32),
        scratch_shapes=[pltpu.VMEM((1, LANES), F32)],
        compiler_params=_params("parallel", "arbitrary"),
        name="logf_cumsum",
    )(lfp)


def _pair_masks(x):
    lane = lax.broadcasted_iota(jnp.int32, x.shape, 1)
    zero = jnp.zeros_like(x)
    return jnp.where(lane < HEAD_DIM, x, zero), jnp.where(lane >= HEAD_DIM, x, zero)


def _online_update(s, m_prev, l_prev):
    m_new = jnp.maximum(m_prev, jnp.max(s, axis=1, keepdims=True))
    alpha = jnp.exp(m_prev - m_new)
    p = jnp.exp(s - m_new)
    l_new = alpha * l_prev + jnp.sum(p, axis=1, keepdims=True)
    return p, alpha, m_new, l_new


def _pair_step(q_a, q_b, k, v_a, v_b, bias_a, bias_b, mask, state):
    m_a, l_a, m_b, l_b, acc = state
    s_a = _dot_nt(q_a, k) + bias_a
    s_b = _dot_nt(q_b, k) + bias_b
    if mask is not None:
        s_a = jnp.where(mask, s_a, NEG_INF)
        s_b = jnp.where(mask, s_b, NEG_INF)
    p_a, al_a, m_a, l_a = _online_update(s_a, m_a, l_a)
    p_b, al_b, m_b, l_b = _online_update(s_b, m_b, l_b)
    lane = lax.broadcasted_iota(jnp.int32, acc.shape, 1)
    alpha = jnp.where(lane < HEAD_DIM, al_a, al_b)
    acc = alpha * acc + (_dot(p_a.astype(BF16), v_a) + _dot(p_b.astype(BF16), v_b))
    return m_a, l_a, m_b, l_b, acc


def _pair_finish(state):
    m_a, l_a, m_b, l_b, acc = state
    lane = lax.broadcasted_iota(jnp.int32, acc.shape, 1)
    return acc / jnp.where(lane < HEAD_DIM, l_a, l_b)


def _att_prompt_kernel(bpre_ref, qt_ref, ka_ref, vt_ref, o_ref, s_ref, p_ref, acc_ref):
    tq = qt_ref.shape[2]
    tk = ATT_TK
    nblk = ka_ref.shape[1] // tk
    b, hp, i = pl.program_id(0), pl.program_id(1), pl.program_id(2)
    q0 = i * tq
    jd = q0 // tk
    kpos = lax.broadcasted_iota(jnp.int32, (tk, tq), 0)
    qpos = lax.broadcasted_iota(jnp.int32, (tk, tq), 1)
    rows = lax.broadcasted_iota(jnp.int32, (LANES, tq), 0)
    base = [((b * N_PAIRS + hp) * 2 + hh) * nblk for hh in range(2)]

    def scores(j, masked):
        start = pl.multiple_of(j * tk, tk)
        col_max = []
        for hh in range(2):
            s = _dot(ka_ref[hh, pl.ds(start, tk), :], qt_ref[hh])
            if masked:
                s = jnp.where(kpos + start <= qpos + q0, s, NEG_INF)
            s_ref[hh] = s
            col_max.append(jnp.max(s, axis=0, keepdims=True))
        return tuple(col_max)

    def softmax_pv(j, col_max, m):
        start = pl.multiple_of(j * tk, tk)
        m_out = []
        for hh in range(2):
            c = bpre_ref[base[hh] + jd] - bpre_ref[base[hh] + j]
            m_new = jnp.maximum(m[hh], col_max[hh] + c)
            alpha = jnp.exp(m[hh] - m_new)
            shift = m_new - c
            for ch in range(tk // ATT_CHUNK):
                sl = slice(ch * ATT_CHUNK, (ch + 1) * ATT_CHUNK)
                p_ref[hh, sl, :] = jnp.exp(s_ref[hh, sl, :] - shift).astype(BF16)
            m_out.append((m_new, alpha))
        return tuple(m_out), start

    def accumulate(m_alpha, start):
        for hh in range(2):
            pv = _dot(vt_ref[hh, :, pl.ds(start, tk)], p_ref[hh])
            acc_ref[hh] = m_alpha[hh][1] * acc_ref[hh] + pv
        return tuple(ma[0] for ma in m_alpha)

    def step(j, next_masked, carry):
        col_max, m = carry
        m_alpha, start = softmax_pv(j, col_max, m)
        col_max_next = scores(j + 1, next_masked)
        return col_max_next, accumulate(m_alpha, start)

    acc_ref[...] = jnp.zeros_like(acc_ref)
    neg = jnp.full((1, tq), NEG_INF, F32)
    carry = (scores(0, True), (neg, neg))
    carry = lax.fori_loop(0, jd - 1, lambda j, cr: step(j, False, cr), carry)
    carry = lax.cond(jd > 0, lambda cr: step(jd - 1, True, cr), lambda cr: cr, carry)
    col_max, m = carry
    m_alpha, start = softmax_pv(jd, col_max, m)
    accumulate(m_alpha, start)

    acc_a, acc_b = acc_ref[0], acc_ref[1]
    out_a = acc_a / acc_a[_aug_lane(0):_aug_lane(0) + 1, :]
    out_b = acc_b / acc_b[_aug_lane(1):_aug_lane(1) + 1, :]
    o_ref[0] = jnp.where(rows < HEAD_DIM, out_a, out_b).T


def _att_prompt(qt, ka, vt, bpre):
    n, _, _, t = qt.shape
    assert t % ATT_TQ == 0 and ATT_TK % ATT_TQ == 0 and ATT_TK == ROW_TILE
    pair = lambda q: (N_PAIRS, 2) + q.shape[2:]
    qt, ka, vt = (a.reshape((n,) + pair(a)) for a in (qt, ka, vt))
    bflat = jnp.transpose(bpre[:, :, 0, :N_HEADS], (0, 2, 1)).reshape(-1)
    return pl.pallas_call(
        _att_prompt_kernel,
        grid=(n, N_PAIRS, t // ATT_TQ),
        in_specs=[pl.BlockSpec(memory_space=pltpu.SMEM),
                  pl.BlockSpec((None, None, 2, LANES, ATT_TQ), lambda b, h, i: (b, h, 0, 0, i)),
                  pl.BlockSpec((None, None, 2, t, LANES), lambda b, h, i: (b, h, 0, 0, 0),
                               pipeline_mode=pl.Buffered(1)),
                  pl.BlockSpec((None, None, 2, LANES, t), lambda b, h, i: (b, h, 0, 0, 0),
                               pipeline_mode=pl.Buffered(1))],
        out_specs=pl.BlockSpec((1, ATT_TQ, PAIR_W), lambda b, h, i: (b, i, h)),
        out_shape=jax.ShapeDtypeStruct((n, t, WA), F32),
        scratch_shapes=[pltpu.VMEM((2, ATT_TK, ATT_TQ), F32), pltpu.VMEM((2, ATT_TK, ATT_TQ), BF16),
                        pltpu.VMEM((2, LANES, ATT_TQ), F32)],
        compiler_params=_params("parallel", "parallel", "arbitrary"),
        name="att_prompt",
    )(bflat, qt, ka, vt)


def _att_sample_kernel(q_ref, kp_ref, vp_ref, dp_ref, dend_ref, kn_ref, vn_ref, dn_ref, o_ref,
                       m_ref, l_ref, acc_ref):
    t = q_ref.shape[1]
    j = pl.program_id(1)

    @pl.when(j == 0)
    def _():
        m_ref[...] = jnp.full_like(m_ref, NEG_INF)
        l_ref[...] = jnp.zeros_like(l_ref)
        acc_ref[...] = jnp.zeros_like(acc_ref)

    def pair(h, k, v, bias, mask):
        sl = slice(h * PAIR_W, (h + 1) * PAIR_W)
        q_a, q_b = _pair_masks(q_ref[0, :, sl])
        v_a, v_b = _pair_masks(v[:, sl].astype(BF16))
        state = (m_ref[2 * h], l_ref[2 * h], m_ref[2 * h + 1], l_ref[2 * h + 1], acc_ref[:, sl])
        state = _pair_step(q_a, q_b, k[:, sl].astype(BF16), v_a, v_b,
                           bias[2 * h:2 * h + 1], bias[2 * h + 1:2 * h + 2], mask, state)
        m_ref[2 * h], l_ref[2 * h], m_ref[2 * h + 1], l_ref[2 * h + 1], acc_ref[:, sl] = state
        return state

    bias_past = dend_ref[0] - dp_ref[0]
    for h in range(N_PAIRS):
        pair(h, kp_ref[0], vp_ref[0], bias_past, None)

    @pl.when(j == pl.num_programs(1) - 1)
    def _():
        bias_new = -dn_ref[0]
        qpos = lax.broadcasted_iota(jnp.int32, (t, t), 0)
        kpos = lax.broadcasted_iota(jnp.int32, (t, t), 1)
        for h in range(N_PAIRS):
            state = pair(h, kn_ref[0], vn_ref[0], bias_new, kpos <= qpos)
            o_ref[0, :, h * PAIR_W:(h + 1) * PAIR_W] = _pair_finish(state)


def _att_sample(qb, k_past, v_past, d_past, kb_new, vb_new, d_new):
    n, t, _ = qb.shape
    past = k_past.shape[1]
    assert past % SAMPLE_TK == 0
    d_end = d_past[:, :, past - 1:]
    new = lambda dt: pl.BlockSpec((1, t, WA), lambda b, j: (b, 0, 0))
    return pl.pallas_call(
        _att_sample_kernel,
        grid=(n, past // SAMPLE_TK),
        in_specs=[new(BF16),
                  pl.BlockSpec((1, SAMPLE_TK, WA), lambda b, j: (b, j, 0)),
                  pl.BlockSpec((1, SAMPLE_TK, WA), lambda b, j: (b, j, 0)),
                  pl.BlockSpec((1, N_HEADS, SAMPLE_TK), lambda b, j: (b, 0, j)),
                  pl.BlockSpec((1, N_HEADS, 1), lambda b, j: (b, 0, 0)),
                  new(BF16), new(BF16),
                  pl.BlockSpec((1, N_HEADS, t), lambda b, j: (b, 0, 0))],
        out_specs=pl.BlockSpec((1, t, WA), lambda b, j: (b, 0, 0)),
        out_shape=jax.ShapeDtypeStruct((n, t, WA), F32),
        scratch_shapes=[pltpu.VMEM((N_HEADS, t, 1), F32), pltpu.VMEM((N_HEADS, t, 1), F32),
                        pltpu.VMEM((t, WA), F32)],
        compiler_params=_params("parallel", "arbitrary"),
        name="att_sample",
    )(qb, k_past, v_past, d_past, d_end, kb_new, vb_new, d_new)


def _first_index_of_max(x, axis):
    mx = jnp.max(x, axis=axis, keepdims=True)
    idx = lax.broadcasted_iota(jnp.int32, x.shape, axis)
    first = jnp.min(jnp.where(x == mx, idx, x.shape[axis]), axis=axis, keepdims=True)
    return mx, idx == first


def _route(s, bias):
    m = s.shape[1]
    sb = (s + bias).reshape(N_GROUPS, GROUP_SIZE, m)
    top1, is_top1 = _first_index_of_max(sb, 1)
    top2 = jnp.max(jnp.where(is_top1, NEG_INF, sb), axis=1, keepdims=True)
    grp = (top1 + top2).reshape(N_GROUPS, m)
    gi = lax.broadcasted_iota(jnp.int32, (N_GROUPS, N_GROUPS, m), 0)
    gj = lax.broadcasted_iota(jnp.int32, (N_GROUPS, N_GROUPS, m), 1)
    other, mine = grp[None, :, :], grp[:, None, :]
    beats = (other > mine) | ((other == mine) & (gj < gi))
    g_rank = jnp.sum(beats.astype(jnp.int32), axis=1)
    g_keep = (g_rank < TOPK_GROUPS)[:, None, :]
    cand = jnp.where(g_keep, sb, NEG_INF).reshape(N_EXPERTS, m)
    picks = []
    for _ in range(TOP_K):
        _, pick = _first_index_of_max(cand, 0)
        picks.append(pick)
        cand = jnp.where(pick, NEG_INF, cand)
    w = jnp.concatenate([jnp.sum(jnp.where(pk, s, 0.0), axis=0, keepdims=True) for pk in picks], axis=0)
    return picks, w / jnp.sum(w, axis=0, keepdims=True) * ROUTED_SCALE


HALF_MASK = 0xFFFF0000


def _pack_halves(x):
    c = x.shape[1] // 2
    lo = lax.bitcast_convert_type(x[:, :c].astype(BF16).astype(F32), jnp.uint32) >> jnp.uint32(16)
    hi = lax.bitcast_convert_type(x[:, c:].astype(BF16).astype(F32), jnp.uint32) & jnp.uint32(HALF_MASK)
    return lo | hi


def _unpack_halves(w):
    lo = lax.bitcast_convert_type(w << jnp.uint32(16), F32)
    hi = lax.bitcast_convert_type(w & jnp.uint32(HALF_MASK), F32)
    return lo, hi


def _outproj_kernel(x_ref, lru_ref, att_ref, mod_ref, gatt_ref, wtop_ref, wbot_ref, gpost_ref, gpre_ref,
                    rwh_ref, rwl_ref, rb_ref, cnt_in_ref,
                    x1_ref, hf_ref, xw_ref, ids_ref, ranks_ref, gates_ref, cnt_out_ref, carry_ref):
    nb, tt, d = x_ref.shape
    m = nb * tt

    @pl.when((pl.program_id(0) == 0) & (pl.program_id(1) == 0))
    def _():
        carry_ref[...] = cnt_in_ref[...]

    mod = mod_ref[...]
    att_n = _rms(att_ref[...], gatt_ref[...]).reshape(m, WA).astype(BF16)
    mix = _dot(lru_ref[...].reshape(m, WL), wtop_ref[...]) + _dot(att_n, wbot_ref[...])
    x1 = x_ref[...] + mod[:, 2:3, :] * _rms(mix, gpost_ref[...]).reshape(nb, tt, d)
    x1_ref[...] = x1
    hf = (_rms(x1, gpre_ref[...]) * (1.0 + mod[:, 4:5, :]) + mod[:, 3:4, :]).reshape(m, d)
    hf_hi = hf.astype(BF16)
    hf_ref[...] = hf_hi.reshape(nb, tt, d)
    hf_lo = (hf - hf_hi.astype(F32)).astype(BF16)
    rwh = rwh_ref[...]
    logits = _dot_nt(rwh, hf_hi) + _dot_nt(rwh, hf_lo) + _dot_nt(rwl_ref[...], hf_hi)
    picks, gates = _route(_sigmoid(logits), rb_ref[...])
    xw_ref[...] = _pack_halves(hf).reshape(nb, tt, d // 2)

    sel = jnp.zeros((N_EXPERTS, m), F32)
    for pk in picks:
        sel = sel + pk.astype(F32)
    before = (lax.broadcasted_iota(jnp.int32, (m, m), 0) < lax.broadcasted_iota(jnp.int32, (m, m), 1)).astype(BF16)
    prior = _dot(sel.astype(BF16), before) + carry_ref[...]
    expert = lax.broadcasted_iota(jnp.int32, (N_EXPERTS, m), 0).astype(F32)
    take = lambda pk, v: jnp.sum(jnp.where(pk, v, 0.0), axis=0, keepdims=True)
    ids_ref[...] = jnp.concatenate([take(pk, expert) for pk in picks], axis=0).astype(jnp.int32)
    ranks_ref[...] = jnp.concatenate([take(pk, prior) for pk in picks], axis=0).astype(jnp.int32)
    carry_ref[...] += jnp.sum(sel, axis=1, keepdims=True)
    cnt_out_ref[...] = carry_ref[...]
    gates = jnp.concatenate([gates, jnp.zeros((LANES - TOP_K, m), F32)], axis=0)
    gates_ref[...] = gates.T.reshape(nb, tt, LANES)


def _outproj(x, lru_n, att, mod, g_att, w_top, w_bot, g_post, g_pre, rw_hi, rw_lo, r_bias, cnt_in):
    n, t, d = x.shape
    nb, tt = _seq_blocks(n, t)
    m = nb * tt
    steps_t = t // tt
    blk = lambda w: pl.BlockSpec((nb, tt, w), lambda i, j: (i, j, 0))
    const = lambda shape: pl.BlockSpec(shape, lambda i, j: (0,) * len(shape))
    per_tok = pl.BlockSpec((TOP_K, m), lambda i, j: (0, i * steps_t + j))
    return pl.pallas_call(
        _outproj_kernel,
        grid=(n // nb, steps_t),
        in_specs=[blk(d), blk(WL), blk(WA), pl.BlockSpec((nb, 6, d), lambda i, j: (i, 0, 0)),
                  const((1, WA)), const((WL, d)), const((WA, d)), const((1, d)), const((1, d)),
                  const((N_EXPERTS, d)), const((N_EXPERTS, d)), const((N_EXPERTS, 1)), const((N_EXPERTS, 1))],
        out_specs=[blk(d), blk(d), blk(d // 2), per_tok, per_tok, blk(LANES), const((N_EXPERTS, 1))],
        out_shape=[jax.ShapeDtypeStruct((n, t, d), F32), jax.ShapeDtypeStruct((n, t, d), BF16),
                   jax.ShapeDtypeStruct((n, t, d // 2), jnp.uint32),
                   jax.ShapeDtypeStruct((TOP_K, n * t), jnp.int32), jax.ShapeDtypeStruct((TOP_K, n * t), jnp.int32),
                   jax.ShapeDtypeStruct((n, t, LANES), F32), jax.ShapeDtypeStruct((N_EXPERTS, 1), F32)],
        scratch_shapes=[pltpu.VMEM((N_EXPERTS, 1), F32)],
        compiler_params=_params("arbitrary", "arbitrary"),
        name="outproj_router",
    )(x, lru_n, att, mod, g_att, w_top, w_bot, g_post, g_pre, rw_hi, rw_lo, r_bias, cnt_in)


def _row_copies(pos_ref, n_rows, make):
    def each(method):
        def body(t8, carry):
            for k in range(8):
                t = t8 * 8 + k
                for r in range(TOP_K):
                    getattr(make(r, t, pos_ref[r, t]), method)()
            return carry
        lax.fori_loop(0, n_rows // 8, body, 0)
    each("start")
    each("wait")


def _dispatch_kernel(pos_ref, x_ref, xs_ref, sem):
    m = x_ref.shape[0]
    _row_copies(pos_ref, m, lambda r, t, slot: pltpu.make_async_copy(
        x_ref.at[pl.ds(t, 1), :], xs_ref.at[pl.ds(slot, 1), :], sem))


def _dispatch(xw, pos, n_slots):
    m_tot, c = xw.shape
    return pl.pallas_call(
        _dispatch_kernel,
        grid=(m_tot // ROW_TILE,),
        in_specs=[pl.BlockSpec((TOP_K, ROW_TILE), lambda i: (0, i), memory_space=pltpu.SMEM),
                  pl.BlockSpec((ROW_TILE, c), lambda i: (i, 0))],
        out_specs=pl.BlockSpec(memory_space=pl.ANY),
        out_shape=jax.ShapeDtypeStruct((n_slots, c), jnp.uint32),
        scratch_shapes=[pltpu.SemaphoreType.DMA],
        compiler_params=_params("arbitrary"),
        name="moe_dispatch",
    )(pos, xw)


def _swiglu_halves(lo, hi, wg, wu, wd):
    c = lo.shape[1]
    hg = _dot(lo, wg[:c]) + _dot(hi, wg[c:])
    hu = _dot(lo, wu[:c]) + _dot(hi, wu[c:])
    return _dot((_silu(hg) * hu).astype(BF16), wd)


def _expert_kernel(te_ref, valid_ref, x_ref, wg_ref, wu_ref, wd_ref, y_ref):
    i = pl.program_id(0)
    valid = valid_ref[i]

    @pl.when(valid > 0)
    def _():
        w = x_ref[...]
        row = lax.broadcasted_iota(jnp.int32, w.shape, 0)
        lo, hi = _unpack_halves(jnp.where(row < valid, w, jnp.uint32(0)))
        y_ref[...] = _pack_halves(_swiglu_halves(lo.astype(BF16), hi.astype(BF16), wg_ref[0], wu_ref[0], wd_ref[0]))


def _experts(xs, tile_expert, tile_valid, wg, wu, wd):
    n_slots, c = xs.shape
    n_tiles = n_slots // EXPERT_TILE
    d = 2 * c
    rows = pl.BlockSpec((EXPERT_TILE, c), lambda i, te, tv: (i, 0))
    weight = lambda shape: pl.BlockSpec((1,) + shape, lambda i, te, tv: (te[i], 0, 0))
    return pl.pallas_call(
        _expert_kernel,
        grid_spec=pltpu.PrefetchScalarGridSpec(
            num_scalar_prefetch=2, grid=(n_tiles,),
            in_specs=[rows, weight((d, D_EXPERT)), weight((d, D_EXPERT)), weight((D_EXPERT, d))],
            out_specs=rows),
        out_shape=jax.ShapeDtypeStruct((n_slots, c), jnp.uint32),
        compiler_params=_params("arbitrary"),
        name="moe_experts",
    )(tile_expert, tile_valid, xs, wg, wu, wd)


def _gather_rows(table, idx):
    info = plsc.get_sparse_core_info()
    n_workers = info.num_cores * info.num_subcores
    b, c = idx.shape[0], table.shape[1]
    per_worker = b // n_workers
    assert per_worker * n_workers == b and per_worker % GATHER_ROWS == 0
    mesh = plsc.VectorSubcoreMesh(core_axis_name="c", subcore_axis_name="s")

    @functools.partial(
        pl.kernel, mesh=mesh, out_type=jax.ShapeDtypeStruct((b, c), jnp.int32),
        scratch_types=[pltpu.VMEM((GATHER_ROWS,), jnp.int32), pltpu.VMEM((GATHER_ROWS, c), jnp.int32),
                       pltpu.SemaphoreType.DMA])
    def gather(table_hbm, idx_hbm, out_hbm, idx_v, rows_v, sem):
        worker = lax.axis_index("s") * info.num_cores + lax.axis_index("c")
        base = worker * per_worker

        @pl.loop(0, per_worker // GATHER_ROWS)
        def _(step):
            off = pl.multiple_of(base + step * GATHER_ROWS, GATHER_ROWS)
            pltpu.sync_copy(idx_hbm.at[pl.ds(off, GATHER_ROWS)], idx_v)
            pltpu.async_copy(table_hbm.at[idx_v], rows_v, sem).wait()
            pltpu.sync_copy(rows_v, out_hbm.at[pl.ds(off, GATHER_ROWS)])

    return gather(table, idx)


def _combine_kernel(rows_ref, gates_ref, hf_ref, x1_ref, mod_ref, sg_ref, su_ref, sd_ref, gpost_ref, y_ref):
    nb, tt, d = hf_ref.shape
    m = nb * tt
    c = d // 2
    x = hf_ref[...].reshape(m, d)
    shared = _swiglu_halves(x[:, :c], x[:, c:], sg_ref[...], su_ref[...], sd_ref[...])
    gates = gates_ref[...].reshape(m, LANES)
    acc_lo = shared[:, :c]
    acc_hi = shared[:, c:]
    for r in range(TOP_K):
        lo, hi = _unpack_halves(lax.bitcast_convert_type(rows_ref[r], jnp.uint32))
        g = gates[:, r:r + 1]
        acc_lo = acc_lo + g * lo
        acc_hi = acc_hi + g * hi
    z = _rms(jnp.concatenate([acc_lo, acc_hi], axis=1), gpost_ref[...]).reshape(nb, tt, d)
    y_ref[...] = x1_ref[...] + mod_ref[...][:, 5:6, :] * z


def _combine(rows, first_tile, gates_t, hf, x1, mod, sg, su, sd, g_post):
    n, t, d = hf.shape
    nb, tt = _seq_blocks(n, t)
    m = nb * tt
    steps_t = t // tt
    blk = lambda w: pl.BlockSpec((nb, tt, w), lambda i, j: (i, j, 0))
    const = lambda shape: pl.BlockSpec(shape, lambda i, j: (0,) * len(shape))
    return pl.pallas_call(
        _combine_kernel,
        grid=(n // nb, steps_t),
        in_specs=[pl.BlockSpec((TOP_K, m, d // 2), lambda i, j: (0, first_tile + i * steps_t + j, 0)),
                  blk(LANES), blk(d), blk(d), pl.BlockSpec((nb, 6, d), lambda i, j: (i, 0, 0)),
                  const((d, D_EXPERT)), const((d, D_EXPERT)), const((D_EXPERT, d)), const((1, d))],
        out_specs=blk(d),
        out_shape=jax.ShapeDtypeStruct((n, t, d), F32),
        compiler_params=_params("parallel", "parallel"),
        name="moe_combine",
    )(rows, gates_t, hf, x1, mod, sg, su, sd, g_post)


def _slots_kernel(starts_ref, ids_ref, ranks_ref, pos_ref):
    ids = ids_ref[...]

    def add_start(e, pos):
        return pos + jnp.where(ids == e, starts_ref[e], 0)

    pos_ref[...] = lax.fori_loop(0, N_EXPERTS, add_start, ranks_ref[...])


def _slots(starts, ids, ranks):
    k, m_tot = ids.shape
    blk = pl.BlockSpec((k, SLOT_COLS), lambda i: (0, i))
    return pl.pallas_call(
        _slots_kernel,
        grid=(m_tot // SLOT_COLS,),
        in_specs=[pl.BlockSpec(memory_space=pltpu.SMEM), blk, blk],
        out_specs=blk,
        out_shape=jax.ShapeDtypeStruct((k, m_tot), jnp.int32),
        compiler_params=_params("arbitrary"),
        name="moe_slots",
    )(starts, ids, ranks)


def _slot_plan(ids, ranks, counts):
    n_pairs = ids.shape[0] * ids.shape[1]
    n_tiles = -(-(n_pairs + N_EXPERTS * (EXPERT_TILE - 1)) // EXPERT_TILE)
    cnt = counts.reshape(N_EXPERTS).astype(jnp.int32)
    padded = (cnt + EXPERT_TILE - 1) // EXPERT_TILE * EXPERT_TILE
    ends = jnp.cumsum(padded)
    starts = ends - padded
    pos = _slots(starts, ids, ranks)
    tile_start = jnp.arange(n_tiles, dtype=jnp.int32) * EXPERT_TILE
    in_expert = (tile_start[:, None] >= starts[None, :]) & (tile_start[:, None] < ends[None, :])
    tile_expert = jnp.sum(jnp.where(in_expert, jnp.arange(N_EXPERTS, dtype=jnp.int32)[None, :], 0), axis=1)
    tile_fill = jnp.sum(jnp.where(in_expert, (starts + cnt)[None, :] - tile_start[:, None], 0), axis=1)
    tile_valid = jnp.clip(tile_fill, 0, EXPERT_TILE).astype(jnp.int32)
    return pos, tile_expert.astype(jnp.int32), tile_valid, n_tiles * EXPERT_TILE


def _block_diag(w):
    g, bw, _ = w.shape
    eye = jnp.eye(g, dtype=w.dtype)
    return (eye[:, None, :, None] * w[:, :, None, :]).reshape(g * bw, g * bw)


def _prep_weights(p):
    d_main = 2 * WL + 3 * WA
    w_in = p["w_in"]
    rw_t = p["router_w"].T
    rw_hi = rw_t.astype(BF16)
    row = lambda v: v.reshape(1, -1)
    return dict(
        w_mod=p["w_mod"], b_mod=p["b_mod"],
        g_pre_mix=row(p["g_pre_mix"]), g_post_mix=row(p["g_post_mix"]),
        g_pre_ffn=row(p["g_pre_ffn"]), g_post_ffn=row(p["g_post_ffn"]),
        w_main=w_in[:, :d_main].astype(BF16),
        w_f=jnp.pad(w_in[:, d_main:], ((0, 0), (0, LANES - N_HEADS))).astype(BF16),
        b_f=jnp.pad(p["b_f"], (0, LANES - N_HEADS)).reshape(1, LANES),
        conv_w=p["conv_w"], conv_b=row(p["conv_b"]),
        wr_bd=_block_diag(p["w_r"]).astype(BF16), b_r=row(p["b_r"]),
        wi_bd=_block_diag(p["w_i"]).astype(BF16), b_i=row(p["b_i"]),
        lam=row(p["lru_lambda"]), g_lru=row(p["g_lru_out"]), g_att=row(p["g_att_out"]),
        w_top=p["w_out"][:WL].astype(BF16), w_bot=p["w_out"][WL:].astype(BF16),
        rw_hi=rw_hi, rw_lo=(rw_t - rw_hi.astype(F32)).astype(BF16),
        r_bias=p["router_bias"].reshape(N_EXPERTS, 1),
        wg=p["w_gate"].astype(BF16), wu=p["w_up"].astype(BF16), wd=p["w_down"].astype(BF16),
        sg=p["ws_gate"].astype(BF16), su=p["ws_up"].astype(BF16), sd=p["ws_down"].astype(BF16),
    )


def _mixers(x, mod, conv0, h0, past, w, cnt_in):
    n, t, _ = x.shape
    proj_args = (x, mod, w["g_pre_mix"], w["w_main"], w["w_f"], w["b_f"])
    if past is None:
        xl, gy, k, v, lf, qt, ka, vt, bpre = _inproj_prompt(*proj_args)
        att = _att_prompt(qt, ka, vt, bpre)
    else:
        xl, gy, qb, kb, vb, k, v, lf, lfp = _inproj(*proj_args)
        d_new = _cumsum_heads(lfp, t)
        k_past, v_past, lf_past = past
        plen = k_past.shape[1]
        lf_past_p = jnp.pad(lf_past, ((0, 0), (0, 0), (0, LANES - N_HEADS)))
        d_past = _cumsum_heads(lf_past_p, ROW_TILE)
        att = _att_sample(qb, k_past.reshape(n, plen, WA), v_past.reshape(n, plen, WA), d_past, kb, vb, d_new)
    lru_n, conv_new, h_new = _lru(xl, gy, conv0, h0.reshape(n, 1, WL), w["conv_w"], w["conv_b"],
                                  w["wr_bd"], w["b_r"], w["wi_bd"], w["b_i"], w["lam"], w["g_lru"])
    routed = _outproj(x, lru_n, att, mod, w["g_att"], w["w_top"], w["w_bot"], w["g_post_mix"],
                      w["g_pre_ffn"], w["rw_hi"], w["rw_lo"], w["r_bias"], cnt_in)
    state = (k.reshape(n, t, N_HEADS, HEAD_DIM), v.reshape(n, t, N_HEADS, HEAD_DIM), lf,
             conv_new, h_new.reshape(n, WL))
    return routed, state


def _layer(xp, xs, mod_p, mod_s, conv_s, h_s, past_s, w):
    n_p = xp.shape[0]
    conv0 = jnp.zeros((n_p, CONV_W - 1, WL), F32)
    h0 = jnp.zeros((n_p, WL), F32)
    zero_cnt = jnp.zeros((N_EXPERTS, 1), F32)
    (x1_p, hf_p, xw_p, ids_p, rk_p, g_p, cnt_p), st_p = _mixers(xp, mod_p, conv0, h0, None, w, zero_cnt)
    (x1_s, hf_s, xw_s, ids_s, rk_s, g_s, cnt), st_s = _mixers(xs, mod_s, conv_s, h_s, past_s, w, cnt_p)

    half = xw_p.shape[-1]
    xw = jnp.concatenate([xw_p.reshape(-1, half), xw_s.reshape(-1, half)], axis=0)
    ids = jnp.concatenate([ids_p, ids_s], axis=1)
    ranks = jnp.concatenate([rk_p, rk_s], axis=1)
    pos, tile_expert, tile_valid, n_slots = _slot_plan(ids, ranks, cnt)
    ys = _experts(_dispatch(xw, pos, n_slots), tile_expert, tile_valid, w["wg"], w["wu"], w["wd"])
    m_p, m_tot = ids_p.shape[1], ids.shape[1]
    rows = _gather_rows(lax.bitcast_convert_type(ys, jnp.int32), pos.reshape(-1)).reshape(TOP_K, m_tot, half)
    shared = (w["sg"], w["su"], w["sd"], w["g_post_ffn"])
    yp = _combine(rows, 0, g_p, hf_p, x1_p, mod_p, *shared)
    ysmp = _combine(rows, m_p // ROW_TILE, g_s, hf_s, x1_s, mod_s, *shared)
    return yp, ysmp, st_p, st_s


def kernel(x_prompt, x_sample, c_prompt, c_sample, cache_k, cache_v, cache_logf, state_conv, state_lru, w_mod, b_mod, g_pre_mix, g_post_mix, g_pre_ffn, g_post_ffn, w_in, conv_w, conv_b, w_r, b_r, w_i, b_i, lru_lambda, b_f, g_lru_out, g_att_out, w_out, router_w, router_bias, w_gate, w_up, w_down, ws_gate, ws_up, ws_down):
    names = ("w_mod", "b_mod", "g_pre_mix", "g_post_mix", "g_pre_ffn", "g_post_ffn", "w_in", "conv_w", "conv_b",
             "w_r", "b_r", "w_i", "b_i", "lru_lambda", "b_f", "g_lru_out", "g_att_out", "w_out", "router_w",
             "router_bias", "w_gate", "w_up", "w_down", "ws_gate", "ws_up", "ws_down")
    stacked = (w_mod, b_mod, g_pre_mix, g_post_mix, g_pre_ffn, g_post_ffn, w_in, conv_w, conv_b, w_r, b_r, w_i, b_i,
               lru_lambda, b_f, g_lru_out, g_att_out, w_out, router_w, router_bias, w_gate, w_up, w_down,
               ws_gate, ws_up, ws_down)
    depth = w_mod.shape[0]
    n_p, n_s = x_prompt.shape[0], x_sample.shape[0]
    yp, ys = x_prompt, x_sample
    st_p, st_s = [], []
    for l in range(depth):
        w = _prep_weights({k: v[l] for k, v in zip(names, stacked)})
        mod = _modulation(jnp.concatenate([c_prompt, c_sample], axis=0), w["w_mod"], w["b_mod"])
        mod = mod.reshape(n_p + n_s, 6, D_MODEL)
        yp, ys, sp, ss = _layer(yp, ys, mod[:n_p], mod[n_p:], state_conv[l], state_lru[l],
                                (cache_k[l], cache_v[l], cache_logf[l]), w)
        st_p.append(sp)
        st_s.append(ss)
    stack = lambda sts, i: jnp.stack([s[i] for s in sts])
    return (yp, ys) + tuple(stack(st_p, i) for i in range(5)) + tuple(stack(st_s, i) for i in range(5))
```

```python
import functools

import jax
import jax.numpy as jnp
import numpy as np
from jax import lax
from jax.experimental import pallas as pl
from jax.experimental.pallas import tpu as pltpu
from jax.experimental.pallas import tpu_sc as plsc

F32 = jnp.float32
BF16 = jnp.bfloat16

D_MODEL = 1024
WL = 512
WA = 512
N_HEADS = 8
HEAD_DIM = 64
N_PAIRS = N_HEADS // 2
PAIR_W = 2 * HEAD_DIM
LANES = 128
CONV_W = 4
LRU_BLOCKS = 8
LRU_C = 8.0
N_EXPERTS = 64
N_GROUPS = 8
GROUP_SIZE = N_EXPERTS // N_GROUPS
TOPK_GROUPS = 4
TOP_K = 8
D_EXPERT = 256
ROUTED_SCALE = 2.5
EPS = 1e-6
NEG_INF = float("-inf")

ROW_TILE = 512
ATT_TQ = 512
ATT_TK = 512
ATT_CHUNK = 64
SAMPLE_TK = 1024
EXPERT_TILE = 512
SLOT_COLS = 2048
GATHER_ROWS = 64
VMEM_LIMIT = 56 * 1024 * 1024


def _params(*sem):
    return pltpu.CompilerParams(dimension_semantics=sem, vmem_limit_bytes=VMEM_LIMIT)


def _dot(a, b):
    return jnp.dot(a, b, preferred_element_type=F32)


def _dot_nt(a, b):
    return lax.dot_general(a, b, (((1,), (1,)), ((), ())), preferred_element_type=F32)


def _split3(x):
    hi = x.astype(BF16)
    r1 = x - hi.astype(F32)
    mid = r1.astype(BF16)
    lo = (r1 - mid.astype(F32)).astype(BF16)
    return hi, mid, lo


def _rms(x, g):
    return x * lax.rsqrt(jnp.mean(x * x, axis=-1, keepdims=True) + EPS) * g


def _sigmoid(x):
    return 1.0 / (1.0 + jnp.exp(-x))


def _silu(x):
    return x * _sigmoid(x)


def _gelu_tanh(x):
    return 0.5 * x * (1.0 + jnp.tanh(0.7978845608028654 * (x + 0.044715 * (x * x * x))))


def _log_sigmoid(x):
    return jnp.minimum(x, 0.0) - jnp.log1p(jnp.exp(-jnp.abs(x)))


def _seq_blocks(n, t):
    if t >= ROW_TILE:
        assert t % ROW_TILE == 0
        return 1, ROW_TILE
    nb = ROW_TILE // t
    assert nb * t == ROW_TILE and n % nb == 0
    return nb, t


def _mod_kernel(c_ref, w_ref, b_ref, o_ref):
    c = _silu(c_ref[...])
    c_hi = c.astype(BF16)
    c_lo = (c - c_hi.astype(F32)).astype(BF16)
    w = w_ref[...]
    w_hi = w.astype(BF16)
    w_lo = (w - w_hi.astype(F32)).astype(BF16)
    o_ref[...] = _dot(c_hi, w_hi) + _dot(c_lo, w_hi) + _dot(c_hi, w_lo) + b_ref[...]


def _modulation(c, w_mod, b_mod):
    rows = c.shape[0]
    n = -(-rows // 8) * 8
    c = jnp.pad(c, ((0, n - rows), (0, 0)))
    d6 = w_mod.shape[1]
    return pl.pallas_call(
        _mod_kernel,
        grid=(d6 // D_MODEL,),
        in_specs=[pl.BlockSpec((n, D_MODEL), lambda j: (0, 0)),
                  pl.BlockSpec((D_MODEL, D_MODEL), lambda j: (0, j)),
                  pl.BlockSpec((1, D_MODEL), lambda j: (0, j))],
        out_specs=pl.BlockSpec((n, D_MODEL), lambda j: (0, j)),
        out_shape=jax.ShapeDtypeStruct((n, d6), F32),
        compiler_params=_params("arbitrary"),
        name="modulation",
    )(c, w_mod, b_mod.reshape(1, d6))[:rows]


def _inproj_kernel(x_ref, mod_ref, g_ref, w_ref, wf_ref, bf_ref,
                   xl_ref, gy_ref, qb_ref, kb_ref, vb_ref, k_ref, v_ref, lf_ref, lfp_ref):
    nb, tt, d = x_ref.shape
    x = x_ref[...]
    mod = mod_ref[...]
    hn = _rms(x, g_ref[...]) * (1.0 + mod[:, 1:2, :]) + mod[:, 0:1, :]
    hb = hn.reshape(nb * tt, d).astype(BF16)

    def proj(col):
        return _dot(hb, w_ref[:, col * WL:(col + 1) * WL]).reshape(nb, tt, WL)

    xl_ref[...] = proj(0)
    gy_ref[...] = _gelu_tanh(proj(1))
    qb_ref[...] = (proj(2) * (HEAD_DIM ** -0.5)).astype(BF16)
    k = proj(3)
    k_ref[...] = k
    kb_ref[...] = k.astype(BF16)
    v = proj(4)
    v_ref[...] = v
    vb_ref[...] = v.astype(BF16)
    fl = _dot(hb, wf_ref[...]) + bf_ref[...]
    lane = lax.broadcasted_iota(jnp.int32, fl.shape, 1)
    lf = jnp.where(lane < N_HEADS, _log_sigmoid(fl), 0.0).reshape(nb, tt, LANES)
    lfp_ref[...] = lf
    lf_ref[...] = lf[:, :, :N_HEADS]


def _inproj(x, mod, g_pre, w_main, w_f, b_f):
    n, t, d = x.shape
    nb, tt = _seq_blocks(n, t)
    blk = lambda w: pl.BlockSpec((nb, tt, w), lambda i, j: (i, j, 0))
    const = lambda shape: pl.BlockSpec(shape, lambda i, j: (0,) * len(shape))
    f32 = lambda w: jax.ShapeDtypeStruct((n, t, w), F32)
    b16 = lambda w: jax.ShapeDtypeStruct((n, t, w), BF16)
    return pl.pallas_call(
        _inproj_kernel,
        grid=(n // nb, t // tt),
        in_specs=[blk(d),
                  pl.BlockSpec((nb, 6, d), lambda i, j: (i, 0, 0)),
                  const((1, d)), const(w_main.shape), const(w_f.shape), const((1, LANES))],
        out_specs=[blk(WL), blk(WL), blk(WA), blk(WA), blk(WA), blk(WA), blk(WA), blk(N_HEADS), blk(LANES)],
        out_shape=[f32(WL), f32(WL), b16(WA), b16(WA), b16(WA), f32(WA), f32(WA), f32(N_HEADS), f32(LANES)],
        compiler_params=_params("parallel", "arbitrary"),
        name="inproj",
    )(x, mod, g_pre, w_main, w_f, b_f)


def _aug_lane(h):
    return HEAD_DIM if h % 2 == 0 else 0


def _inproj_prompt_kernel(x_ref, mod_ref, g_ref, w_ref, wf_ref, bf_ref, place_ref,
                          xl_ref, gy_ref, k_ref, v_ref, lf_ref, qt_ref, ka_ref, vt_ref, bpre_ref, carry_ref):
    _, tt, d = x_ref.shape

    @pl.when(pl.program_id(1) == 0)
    def _():
        carry_ref[...] = jnp.zeros_like(carry_ref)

    mod = mod_ref[0]
    hb = (_rms(x_ref[0], g_ref[...]) * (1.0 + mod[1:2, :]) + mod[0:1, :]).astype(BF16)

    def proj(col):
        return _dot(hb, w_ref[:, col * WL:(col + 1) * WL])

    xl_ref[0] = proj(0)
    gy_ref[0] = _gelu_tanh(proj(1))
    q = proj(2) * (HEAD_DIM ** -0.5)
    k = proj(3)
    k_ref[0] = k
    v = proj(4)
    v_ref[0] = v
    fl = _dot(hb, wf_ref[...]) + bf_ref[...]
    lane = lax.broadcasted_iota(jnp.int32, (tt, LANES), 1)
    lf = jnp.where(lane < N_HEADS, _log_sigmoid(fl), 0.0)
    lf_ref[0] = lf[:, :N_HEADS]

    row = lax.broadcasted_iota(jnp.int32, (tt, tt), 0)
    col = lax.broadcasted_iota(jnp.int32, (tt, tt), 1)
    tril = (col <= row).astype(BF16)
    hi, mid, lo = _split3(lf)
    e = _dot(tril, hi) + _dot(tril, mid) + _dot(tril, lo)
    bpre_ref[0, 0] = carry_ref[...]
    carry_ref[...] += e[tt - 1:tt, :]
    e_hi, e_mid, e_lo = _split3(-e)
    aug_k = _dot(e_hi, place_ref[0]) + _dot(e_mid, place_ref[1]) + _dot(e_lo, place_ref[2])

    for h in range(N_HEADS):
        pair = slice((h // 2) * PAIR_W, (h // 2 + 1) * PAIR_W)
        dims = (lane < HEAD_DIM) if h % 2 == 0 else (lane >= HEAD_DIM)
        a0 = _aug_lane(h)
        ones3 = ((lane >= a0) & (lane < a0 + 3)).astype(F32)
        one1 = (lane == a0).astype(F32)
        qt_ref[0, h] = (jnp.where(dims, q[:, pair], 0.0) + ones3).T.astype(BF16)
        ka_ref[0, h] = (jnp.where(dims, k[:, pair], 0.0) + aug_k[:, h * LANES:(h + 1) * LANES]).astype(BF16)
        vt_ref[0, h] = (jnp.where(dims, v[:, pair], 0.0) + one1).T.astype(BF16)


def _placement():
    pl_mat = np.zeros((3, LANES, N_HEADS * LANES), np.float32)
    for p in range(3):
        for h in range(N_HEADS):
            pl_mat[p, h, h * LANES + _aug_lane(h) + p] = 1.0
    return jnp.asarray(pl_mat, BF16)


def _inproj_prompt(x, mod, g_pre, w_main, w_f, b_f):
    n, t, d = x.shape
    tt = ROW_TILE
    assert t % tt == 0
    nblk = t // tt
    blk = lambda w: pl.BlockSpec((1, tt, w), lambda i, j: (i, j, 0))
    const = lambda shape: pl.BlockSpec(shape, lambda i, j: (0,) * len(shape))
    f32 = lambda w: jax.ShapeDtypeStruct((n, t, w), F32)
    place = _placement()
    return pl.pallas_call(
        _inproj_prompt_kernel,
        grid=(n, nblk),
        in_specs=[blk(d), pl.BlockSpec((1, 6, d), lambda i, j: (i, 0, 0)),
                  const((1, d)), const(w_main.shape), const(w_f.shape), const((1, LANES)), const(place.shape)],
        out_specs=[blk(WL), blk(WL), blk(WA), blk(WA), blk(N_HEADS),
                   pl.BlockSpec((1, N_HEADS, LANES, tt), lambda i, j: (i, 0, 0, j)),
                   pl.BlockSpec((1, N_HEADS, tt, LANES), lambda i, j: (i, 0, j, 0)),
                   pl.BlockSpec((1, N_HEADS, LANES, tt), lambda i, j: (i, 0, 0, j)),
                   pl.BlockSpec((1, 1, 1, LANES), lambda i, j: (i, j, 0, 0))],
        out_shape=[f32(WL), f32(WL), f32(WA), f32(WA), f32(N_HEADS),
                   jax.ShapeDtypeStruct((n, N_HEADS, LANES, t), BF16),
                   jax.ShapeDtypeStruct((n, N_HEADS, t, LANES), BF16),
                   jax.ShapeDtypeStruct((n, N_HEADS, LANES, t), BF16),
                   jax.ShapeDtypeStruct((n, nblk, 1, LANES), F32)],
        scratch_shapes=[pltpu.VMEM((1, LANES), F32)],
        compiler_params=_params("parallel", "arbitrary"),
        name="inproj_prompt",
    )(x, mod, g_pre, w_main, w_f, b_f, place)


def _expm1_neg(x):
    poly = x * (1.0 + x * (0.5 + x * (1.0 / 6.0 + x * (1.0 / 24.0 + x * (1.0 / 120.0)))))
    return jnp.where(x > -0.1, poly, jnp.exp(x) - 1.0)


def _lru_kernel(xl_ref, gy_ref, conv0_ref, h0_ref, cw_ref, cb_ref, wr_ref, br_ref, wi_ref, bi_ref,
                lam_ref, g_ref, out_ref, conv_ref, hlast_ref, tail_ref, carry_ref):
    nb, tt, w = xl_ref.shape
    j = pl.program_id(1)

    @pl.when(j == 0)
    def _():
        tail_ref[:, 8 - (CONV_W - 1):, :] = conv0_ref[...]
        carry_ref[...] = h0_ref[...]

    xl = xl_ref[...]
    xpad = jnp.concatenate([tail_ref[...], xl], axis=1)
    cw = cw_ref[...]
    xc = jnp.zeros_like(xl) + cb_ref[...]
    for k in range(CONV_W):
        off = 8 - (CONV_W - 1) + k
        xc = xc + xpad[:, off:off + tt, :] * cw[k:k + 1, :]
    conv_ref[...] = xpad[:, tt + 8 - (CONV_W - 1):, :]
    tail_ref[...] = xpad[:, tt:, :]

    m = nb * tt
    xf = xc.reshape(m, w)
    xb = xf.astype(BF16)
    r = _sigmoid(_dot(xb, wr_ref[...]) + br_ref[...])
    gi = _sigmoid(_dot(xb, wi_ref[...]) + bi_ref[...])
    lam = lam_ref[...]
    softplus = jnp.maximum(-lam, 0.0) + jnp.log1p(jnp.exp(-jnp.abs(lam)))
    log_a = (-LRU_C) * r * softplus
    a = jnp.exp(log_a)
    b = jnp.sqrt(-_expm1_neg(2.0 * log_a)) * (gi * xf)

    pos = lax.broadcasted_iota(jnp.int32, (m, w), 0) % tt
    d = 1
    while d < tt:
        keep = pos >= d
        a_prev = jnp.where(keep, pltpu.roll(a, d, 0), 1.0)
        b_prev = jnp.where(keep, pltpu.roll(b, d, 0), 0.0)
        b = a * b_prev + b
        a = a * a_prev
        d *= 2
    h = a.reshape(nb, tt, w) * carry_ref[...] + b.reshape(nb, tt, w)
    h_last = h[:, tt - 1:tt, :]
    carry_ref[...] = h_last
    hlast_ref[...] = h_last
    out_ref[...] = _rms(h * gy_ref[...], g_ref[...]).astype(BF16)


def _lru(xl, gy, conv0, h0, conv_w, conv_b, wr_bd, b_r, wi_bd, b_i, lam, g_lru):
    n, t, w = xl.shape
    nb, tt = _seq_blocks(n, t)
    blk = pl.BlockSpec((nb, tt, w), lambda i, j: (i, j, 0))
    per_seq = lambda rows: pl.BlockSpec((nb, rows, w), lambda i, j: (i, 0, 0))
    const = lambda shape: pl.BlockSpec(shape, lambda i, j: (0,) * len(shape))
    row = const((1, w))
    return pl.pallas_call(
        _lru_kernel,
        grid=(n // nb, t // tt),
        in_specs=[blk, blk, per_seq(CONV_W - 1), per_seq(1),
                  const((CONV_W, w)), row, const((w, w)), row, const((w, w)), row, row, row],
        out_specs=[blk, per_seq(CONV_W - 1), per_seq(1)],
        out_shape=[jax.ShapeDtypeStruct((n, t, w), BF16),
                   jax.ShapeDtypeStruct((n, CONV_W - 1, w), F32),
                   jax.ShapeDtypeStruct((n, 1, w), F32)],
        scratch_shapes=[pltpu.VMEM((nb, 8, w), F32), pltpu.VMEM((nb, 1, w), F32)],
        compiler_params=_params("parallel", "arbitrary"),
        name="rglru",
    )(xl, gy, conv0, h0, conv_w, conv_b, wr_bd, b_r, wi_bd, b_i, lam, g_lru)


def _cumsum_kernel(lf_ref, o_ref, carry_ref):
    tb = lf_ref.shape[1]

    @pl.when(pl.program_id(1) == 0)
    def _():
        carry_ref[...] = jnp.zeros_like(carry_ref)

    row = lax.broadcasted_iota(jnp.int32, (tb, tb), 0)
    col = lax.broadcasted_iota(jnp.int32, (tb, tb), 1)
    tril = (col <= row).astype(BF16)
    hi, mid, lo = _split3(lf_ref[0])
    d = _dot(tril, hi) + _dot(tril, mid) + _dot(tril, lo) + carry_ref[...]
    carry_ref[...] = d[tb - 1:tb, :]
    o_ref[0] = d.T[:N_HEADS, :]


def _cumsum_heads(lfp, tb):
    n, t, _ = lfp.shape
    return pl.pallas_call(
        _cumsum_kernel,
        grid=(n, t // tb),
        in_specs=[pl.BlockSpec((1, tb, LANES), lambda i, j: (i, j, 0))],
        out_specs=pl.BlockSpec((1, N_HEADS, tb), lambda i, j: (i, 0, j)),
        out_shape=jax.ShapeDtypeStruct((n, N_HEADS, t), F32),
        scratch_shapes=[pltpu.VMEM((1, LANES), F32)],
        compiler_params=_params("parallel", "arbitrary"),
        name="logf_cumsum",
    )(lfp)


def _pair_masks(x):
    lane = lax.broadcasted_iota(jnp.int32, x.shape, 1)
    zero = jnp.zeros_like(x)
    return jnp.where(lane < HEAD_DIM, x, zero), jnp.where(lane >= HEAD_DIM, x, zero)


def _online_update(s, m_prev, l_prev):
    m_new = jnp.maximum(m_prev, jnp.max(s, axis=1, keepdims=True))
    alpha = jnp.exp(m_prev - m_new)
    p = jnp.exp(s - m_new)
    l_new = alpha * l_prev + jnp.sum(p, axis=1, keepdims=True)
    return p, alpha, m_new, l_new


def _pair_step(q_a, q_b, k, v_a, v_b, bias_a, bias_b, mask, state):
    m_a, l_a, m_b, l_b, acc = state
    s_a = _dot_nt(q_a, k) + bias_a
    s_b = _dot_nt(q_b, k) + bias_b
    if mask is not None:
        s_a = jnp.where(mask, s_a, NEG_INF)
        s_b = jnp.where(mask, s_b, NEG_INF)
    p_a, al_a, m_a, l_a = _online_update(s_a, m_a, l_a)
    p_b, al_b, m_b, l_b = _online_update(s_b, m_b, l_b)
    lane = lax.broadcasted_iota(jnp.int32, acc.shape, 1)
    alpha = jnp.where(lane < HEAD_DIM, al_a, al_b)
    acc = alpha * acc + (_dot(p_a.astype(BF16), v_a) + _dot(p_b.astype(BF16), v_b))
    return m_a, l_a, m_b, l_b, acc


def _pair_finish(state):
    m_a, l_a, m_b, l_b, acc = state
    lane = lax.broadcasted_iota(jnp.int32, acc.shape, 1)
    return acc / jnp.where(lane < HEAD_DIM, l_a, l_b)


def _att_prompt_kernel(bpre_ref, qt_ref, ka_ref, vt_ref, o_ref, s_ref, p_ref, acc_ref):
    tq = qt_ref.shape[2]
    tk = ATT_TK
    nblk = ka_ref.shape[1] // tk
    b, hp, i = pl.program_id(0), pl.program_id(1), pl.program_id(2)
    q0 = i * tq
    jd = q0 // tk
    kpos = lax.broadcasted_iota(jnp.int32, (tk, tq), 0)
    qpos = lax.broadcasted_iota(jnp.int32, (tk, tq), 1)
    rows = lax.broadcasted_iota(jnp.int32, (LANES, tq), 0)
    base = [((b * N_PAIRS + hp) * 2 + hh) * nblk for hh in range(2)]

    def scores(j, masked):
        start = pl.multiple_of(j * tk, tk)
        col_max = []
        for hh in range(2):
            s = _dot(ka_ref[hh, pl.ds(start, tk), :], qt_ref[hh])
            if masked:
                s = jnp.where(kpos + start <= qpos + q0, s, NEG_INF)
            s_ref[hh] = s
            col_max.append(jnp.max(s, axis=0, keepdims=True))
        return tuple(col_max)

    def softmax_pv(j, col_max, m):
        start = pl.multiple_of(j * tk, tk)
        m_out = []
        for hh in range(2):
            c = bpre_ref[base[hh] + jd] - bpre_ref[base[hh] + j]
            m_new = jnp.maximum(m[hh], col_max[hh] + c)
            alpha = jnp.exp(m[hh] - m_new)
            shift = m_new - c
            for ch in range(tk // ATT_CHUNK):
                sl = slice(ch * ATT_CHUNK, (ch + 1) * ATT_CHUNK)
                p_ref[hh, sl, :] = jnp.exp(s_ref[hh, sl, :] - shift).astype(BF16)
            m_out.append((m_new, alpha))
        return tuple(m_out), start

    def accumulate(m_alpha, start):
        for hh in range(2):
            pv = _dot(vt_ref[hh, :, pl.ds(start, tk)], p_ref[hh])
            acc_ref[hh] = m_alpha[hh][1] * acc_ref[hh] + pv
        return tuple(ma[0] for ma in m_alpha)

    def step(j, next_masked, carry):
        col_max, m = carry
        m_alpha, start = softmax_pv(j, col_max, m)
        col_max_next = scores(j + 1, next_masked)
        return col_max_next, accumulate(m_alpha, start)

    acc_ref[...] = jnp.zeros_like(acc_ref)
    neg = jnp.full((1, tq), NEG_INF, F32)
    carry = (scores(0, True), (neg, neg))
    carry = lax.fori_loop(0, jd - 1, lambda j, cr: step(j, False, cr), carry)
    carry = lax.cond(jd > 0, lambda cr: step(jd - 1, True, cr), lambda cr: cr, carry)
    col_max, m = carry
    m_alpha, start = softmax_pv(jd, col_max, m)
    accumulate(m_alpha, start)

    acc_a, acc_b = acc_ref[0], acc_ref[1]
    out_a = acc_a / acc_a[_aug_lane(0):_aug_lane(0) + 1, :]
    out_b = acc_b / acc_b[_aug_lane(1):_aug_lane(1) + 1, :]
    o_ref[0] = jnp.where(rows < HEAD_DIM, out_a, out_b).T


def _att_prompt(qt, ka, vt, bpre):
    n, _, _, t = qt.shape
    assert t % ATT_TQ == 0 and ATT_TK % ATT_TQ == 0 and ATT_TK == ROW_TILE
    pair = lambda q: (N_PAIRS, 2) + q.shape[2:]
    qt, ka, vt = (a.reshape((n,) + pair(a)) for a in (qt, ka, vt))
    bflat = jnp.transpose(bpre[:, :, 0, :N_HEADS], (0, 2, 1)).reshape(-1)
    return pl.pallas_call(
        _att_prompt_kernel,
        grid=(n, N_PAIRS, t // ATT_TQ),
        in_specs=[pl.BlockSpec(memory_space=pltpu.SMEM),
                  pl.BlockSpec((None, None, 2, LANES, ATT_TQ), lambda b, h, i: (b, h, 0, 0, i)),
                  pl.BlockSpec((None, None, 2, t, LANES), lambda b, h, i: (b, h, 0, 0, 0),
                               pipeline_mode=pl.Buffered(1)),
                  pl.BlockSpec((None, None, 2, LANES, t), lambda b, h, i: (b, h, 0, 0, 0),
                               pipeline_mode=pl.Buffered(1))],
        out_specs=pl.BlockSpec((1, ATT_TQ, PAIR_W), lambda b, h, i: (b, i, h)),
        out_shape=jax.ShapeDtypeStruct((n, t, WA), F32),
        scratch_shapes=[pltpu.VMEM((2, ATT_TK, ATT_TQ), F32), pltpu.VMEM((2, ATT_TK, ATT_TQ), BF16),
                        pltpu.VMEM((2, LANES, ATT_TQ), F32)],
        compiler_params=_params("parallel", "parallel", "arbitrary"),
        name="att_prompt",
    )(bflat, qt, ka, vt)


def _att_sample_kernel(q_ref, kp_ref, vp_ref, dp_ref, dend_ref, kn_ref, vn_ref, dn_ref, o_ref,
                       m_ref, l_ref, acc_ref):
    t = q_ref.shape[1]
    j = pl.program_id(1)

    @pl.when(j == 0)
    def _():
        m_ref[...] = jnp.full_like(m_ref, NEG_INF)
        l_ref[...] = jnp.zeros_like(l_ref)
        acc_ref[...] = jnp.zeros_like(acc_ref)

    def pair(h, k, v, bias, mask):
        sl = slice(h * PAIR_W, (h + 1) * PAIR_W)
        q_a, q_b = _pair_masks(q_ref[0, :, sl])
        v_a, v_b = _pair_masks(v[:, sl].astype(BF16))
        state = (m_ref[2 * h], l_ref[2 * h], m_ref[2 * h + 1], l_ref[2 * h + 1], acc_ref[:, sl])
        state = _pair_step(q_a, q_b, k[:, sl].astype(BF16), v_a, v_b,
                           bias[2 * h:2 * h + 1], bias[2 * h + 1:2 * h + 2], mask, state)
        m_ref[2 * h], l_ref[2 * h], m_ref[2 * h + 1], l_ref[2 * h + 1], acc_ref[:, sl] = state
        return state

    bias_past = dend_ref[0] - dp_ref[0]
    for h in range(N_PAIRS):
        pair(h, kp_ref[0], vp_ref[0], bias_past, None)

    @pl.when(j == pl.num_programs(1) - 1)
    def _():
        bias_new = -dn_ref[0]
        qpos = lax.broadcasted_iota(jnp.int32, (t, t), 0)
        kpos = lax.broadcasted_iota(jnp.int32, (t, t), 1)
        for h in range(N_PAIRS):
            state = pair(h, kn_ref[0], vn_ref[0], bias_new, kpos <= qpos)
            o_ref[0, :, h * PAIR_W:(h + 1) * PAIR_W] = _pair_finish(state)


def _att_sample(qb, k_past, v_past, d_past, kb_new, vb_new, d_new):
    n, t, _ = qb.shape
    past = k_past.shape[1]
    assert past % SAMPLE_TK == 0
    d_end = d_past[:, :, past - 1:]
    new = lambda dt: pl.BlockSpec((1, t, WA), lambda b, j: (b, 0, 0))
    return pl.pallas_call(
        _att_sample_kernel,
        grid=(n, past // SAMPLE_TK),
        in_specs=[new(BF16),
                  pl.BlockSpec((1, SAMPLE_TK, WA), lambda b, j: (b, j, 0)),
                  pl.BlockSpec((1, SAMPLE_TK, WA), lambda b, j: (b, j, 0)),
                  pl.BlockSpec((1, N_HEADS, SAMPLE_TK), lambda b, j: (b, 0, j)),
                  pl.BlockSpec((1, N_HEADS, 1), lambda b, j: (b, 0, 0)),
                  new(BF16), new(BF16),
                  pl.BlockSpec((1, N_HEADS, t), lambda b, j: (b, 0, 0))],
        out_specs=pl.BlockSpec((1, t, WA), lambda b, j: (b, 0, 0)),
        out_shape=jax.ShapeDtypeStruct((n, t, WA), F32),
        scratch_shapes=[pltpu.VMEM((N_HEADS, t, 1), F32), pltpu.VMEM((N_HEADS, t, 1), F32),
                        pltpu.VMEM((t, WA), F32)],
        compiler_params=_params("parallel", "arbitrary"),
        name="att_sample",
    )(qb, k_past, v_past, d_past, d_end, kb_new, vb_new, d_new)


def _first_index_of_max(x, axis):
    mx = jnp.max(x, axis=axis, keepdims=True)
    idx = lax.broadcasted_iota(jnp.int32, x.shape, axis)
    first = jnp.min(jnp.where(x == mx, idx, x.shape[axis]), axis=axis, keepdims=True)
    return mx, idx == first


def _route(s, bias):
    m = s.shape[1]
    sb = (s + bias).reshape(N_GROUPS, GROUP_SIZE, m)
    top1, is_top1 = _first_index_of_max(sb, 1)
    top2 = jnp.max(jnp.where(is_top1, NEG_INF, sb), axis=1, keepdims=True)
    grp = (top1 + top2).reshape(N_GROUPS, m)
    gi = lax.broadcasted_iota(jnp.int32, (N_GROUPS, N_GROUPS, m), 0)
    gj = lax.broadcasted_iota(jnp.int32, (N_GROUPS, N_GROUPS, m), 1)
    other, mine = grp[None, :, :], grp[:, None, :]
    beats = (other > mine) | ((other == mine) & (gj < gi))
    g_rank = jnp.sum(beats.astype(jnp.int32), axis=1)
    g_keep = (g_rank < TOPK_GROUPS)[:, None, :]
    cand = jnp.where(g_keep, sb, NEG_INF).reshape(N_EXPERTS, m)
    picks = []
    for _ in range(TOP_K):
        _, pick = _first_index_of_max(cand, 0)
        picks.append(pick)
        cand = jnp.where(pick, NEG_INF, cand)
    w = jnp.concatenate([jnp.sum(jnp.where(pk, s, 0.0), axis=0, keepdims=True) for pk in picks], axis=0)
    return picks, w / jnp.sum(w, axis=0, keepdims=True) * ROUTED_SCALE


HALF_MASK = 0xFFFF0000


def _pack_halves(x):
    c = x.shape[1] // 2
    lo = lax.bitcast_convert_type(x[:, :c].astype(BF16).astype(F32), jnp.uint32) >> jnp.uint32(16)
    hi = lax.bitcast_convert_type(x[:, c:].astype(BF16).astype(F32), jnp.uint32) & jnp.uint32(HALF_MASK)
    return lax.bitcast_convert_type(lo | hi, jnp.int32)


def _unpack_halves(words):
    w = lax.bitcast_convert_type(words, jnp.uint32)
    lo = lax.bitcast_convert_type(w << jnp.uint32(16), F32)
    hi = lax.bitcast_convert_type(w & jnp.uint32(HALF_MASK), F32)
    return lo, hi


def _outproj_kernel(x_ref, lru_ref, att_ref, mod_ref, gatt_ref, wtop_ref, wbot_ref, gpost_ref, gpre_ref,
                    rwh_ref, rwl_ref, rb_ref, cnt_in_ref,
                    x1_ref, hf_ref, xw_ref, ids_ref, ranks_ref, gates_ref, cnt_out_ref, carry_ref):
    nb, tt, d = x_ref.shape
    m = nb * tt

    @pl.when((pl.program_id(0) == 0) & (pl.program_id(1) == 0))
    def _():
        carry_ref[...] = cnt_in_ref[...]

    mod = mod_ref[...]
    att_n = _rms(att_ref[...], gatt_ref[...]).reshape(m, WA).astype(BF16)
    mix = _dot(lru_ref[...].reshape(m, WL), wtop_ref[...]) + _dot(att_n, wbot_ref[...])
    x1 = x_ref[...] + mod[:, 2:3, :] * _rms(mix, gpost_ref[...]).reshape(nb, tt, d)
    x1_ref[...] = x1
    hf = (_rms(x1, gpre_ref[...]) * (1.0 + mod[:, 4:5, :]) + mod[:, 3:4, :]).reshape(m, d)
    hf_hi = hf.astype(BF16)
    hf_ref[...] = hf_hi.reshape(nb, tt, d)
    hf_lo = (hf - hf_hi.astype(F32)).astype(BF16)
    rwh = rwh_ref[...]
    logits = _dot_nt(rwh, hf_hi) + _dot_nt(rwh, hf_lo) + _dot_nt(rwl_ref[...], hf_hi)
    picks, gates = _route(_sigmoid(logits), rb_ref[...])
    xw_ref[...] = _pack_halves(hf).reshape(nb, tt, d // 2)

    sel = jnp.zeros((N_EXPERTS, m), F32)
    for pk in picks:
        sel = sel + pk.astype(F32)
    before = (lax.broadcasted_iota(jnp.int32, (m, m), 0) < lax.broadcasted_iota(jnp.int32, (m, m), 1)).astype(BF16)
    prior = _dot(sel.astype(BF16), before) + carry_ref[...]
    expert = lax.broadcasted_iota(jnp.int32, (N_EXPERTS, m), 0).astype(F32)
    take = lambda pk, v: jnp.sum(jnp.where(pk, v, 0.0), axis=0, keepdims=True)
    ids_ref[...] = jnp.concatenate([take(pk, expert) for pk in picks], axis=0).astype(jnp.int32)
    ranks_ref[...] = jnp.concatenate([take(pk, prior) for pk in picks], axis=0).astype(jnp.int32)
    carry_ref[...] += jnp.sum(sel, axis=1, keepdims=True)
    cnt_out_ref[...] = carry_ref[...]
    gates = jnp.concatenate([gates, jnp.zeros((LANES - TOP_K, m), F32)], axis=0)
    gates_ref[...] = gates.T.reshape(nb, tt, LANES)


def _outproj(x, lru_n, att, mod, g_att, w_top, w_bot, g_post, g_pre, rw_hi, rw_lo, r_bias, cnt_in):
    n, t, d = x.shape
    nb, tt = _seq_blocks(n, t)
    m = nb * tt
    steps_t = t // tt
    blk = lambda w: pl.BlockSpec((nb, tt, w), lambda i, j: (i, j, 0))
    const = lambda shape: pl.BlockSpec(shape, lambda i, j: (0,) * len(shape))
    per_tok = pl.BlockSpec((TOP_K, m), lambda i, j: (0, i * steps_t + j))
    return pl.pallas_call(
        _outproj_kernel,
        grid=(n // nb, steps_t),
        in_specs=[blk(d), blk(WL), blk(WA), pl.BlockSpec((nb, 6, d), lambda i, j: (i, 0, 0)),
                  const((1, WA)), const((WL, d)), const((WA, d)), const((1, d)), const((1, d)),
                  const((N_EXPERTS, d)), const((N_EXPERTS, d)), const((N_EXPERTS, 1)), const((N_EXPERTS, 1))],
        out_specs=[blk(d), blk(d), blk(d // 2), per_tok, per_tok, blk(LANES), const((N_EXPERTS, 1))],
        out_shape=[jax.ShapeDtypeStruct((n, t, d), F32), jax.ShapeDtypeStruct((n, t, d), BF16),
                   jax.ShapeDtypeStruct((n, t, d // 2), jnp.int32),
                   jax.ShapeDtypeStruct((TOP_K, n * t), jnp.int32), jax.ShapeDtypeStruct((TOP_K, n * t), jnp.int32),
                   jax.ShapeDtypeStruct((n, t, LANES), F32), jax.ShapeDtypeStruct((N_EXPERTS, 1), F32)],
        scratch_shapes=[pltpu.VMEM((N_EXPERTS, 1), F32)],
        compiler_params=_params("arbitrary", "arbitrary"),
        name="outproj_router",
    )(x, lru_n, att, mod, g_att, w_top, w_bot, g_post, g_pre, rw_hi, rw_lo, r_bias, cnt_in)


def _subcore_ranges(n_items):
    info = plsc.get_sparse_core_info()
    n_workers = info.num_cores * info.num_subcores
    per_worker = n_items // n_workers
    assert per_worker * n_workers == n_items and per_worker % GATHER_ROWS == 0
    return info, plsc.VectorSubcoreMesh(core_axis_name="c", subcore_axis_name="s"), per_worker


def _scatter_rows(xw, pos, n_slots):
    m_tot, c = xw.shape
    info, mesh, per_worker = _subcore_ranges(m_tot)
    pos_flat = pos.reshape(-1)

    @functools.partial(
        pl.kernel, mesh=mesh, out_type=jax.ShapeDtypeStruct((n_slots, c), jnp.int32),
        scratch_types=[pltpu.VMEM((GATHER_ROWS,), jnp.int32), pltpu.VMEM((GATHER_ROWS, c), jnp.int32),
                       pltpu.SemaphoreType.DMA])
    def scatter(xw_hbm, pos_hbm, out_hbm, idx_v, rows_v, sem):
        worker = lax.axis_index("s") * info.num_cores + lax.axis_index("c")
        base = worker * per_worker

        @pl.loop(0, per_worker // GATHER_ROWS)
        def _(step):
            off = pl.multiple_of(base + step * GATHER_ROWS, GATHER_ROWS)
            pltpu.sync_copy(xw_hbm.at[pl.ds(off, GATHER_ROWS)], rows_v)
            for r in range(TOP_K):
                pltpu.sync_copy(pos_hbm.at[pl.ds(pl.multiple_of(r * m_tot + off, GATHER_ROWS), GATHER_ROWS)], idx_v)
                pltpu.async_copy(rows_v, out_hbm.at[idx_v], sem).wait()

    return scatter(xw, pos_flat)


def _swiglu_halves(lo, hi, wg, wu, wd):
    c = lo.shape[1]
    hg = _dot(lo, wg[:c]) + _dot(hi, wg[c:])
    hu = _dot(lo, wu[:c]) + _dot(hi, wu[c:])
    return _dot((_silu(hg) * hu).astype(BF16), wd)


def _expert_kernel(te_ref, valid_ref, x_ref, wg_ref, wu_ref, wd_ref, y_ref):
    i = pl.program_id(0)
    valid = valid_ref[i]

    @pl.when(valid > 0)
    def _():
        w = x_ref[...]
        row = lax.broadcasted_iota(jnp.int32, w.shape, 0)
        lo, hi = _unpack_halves(jnp.where(row < valid, w, 0))
        y_ref[...] = _pack_halves(_swiglu_halves(lo.astype(BF16), hi.astype(BF16), wg_ref[0], wu_ref[0], wd_ref[0]))


def _experts(xs, tile_expert, tile_valid, wg, wu, wd):
    n_slots, c = xs.shape
    n_tiles = n_slots // EXPERT_TILE
    d = 2 * c
    rows = pl.BlockSpec((EXPERT_TILE, c), lambda i, te, tv: (i, 0))
    weight = lambda shape: pl.BlockSpec((1,) + shape, lambda i, te, tv: (te[i], 0, 0))
    return pl.pallas_call(
        _expert_kernel,
        grid_spec=pltpu.PrefetchScalarGridSpec(
            num_scalar_prefetch=2, grid=(n_tiles,),
            in_specs=[rows, weight((d, D_EXPERT)), weight((d, D_EXPERT)), weight((D_EXPERT, d))],
            out_specs=rows),
        out_shape=jax.ShapeDtypeStruct((n_slots, c), jnp.int32),
        compiler_params=_params("arbitrary"),
        name="moe_experts",
    )(tile_expert, tile_valid, xs, wg, wu, wd)


def _gather_rows(table, idx):
    b, c = idx.shape[0], table.shape[1]
    info, mesh, per_worker = _subcore_ranges(b)

    @functools.partial(
        pl.kernel, mesh=mesh, out_type=jax.ShapeDtypeStruct((b, c), jnp.int32),
        scratch_types=[pltpu.VMEM((GATHER_ROWS,), jnp.int32), pltpu.VMEM((GATHER_ROWS, c), jnp.int32),
                       pltpu.SemaphoreType.DMA])
    def gather(table_hbm, idx_hbm, out_hbm, idx_v, rows_v, sem):
        worker = lax.axis_index("s") * info.num_cores + lax.axis_index("c")
        base = worker * per_worker

        @pl.loop(0, per_worker // GATHER_ROWS)
        def _(step):
            off = pl.multiple_of(base + step * GATHER_ROWS, GATHER_ROWS)
            pltpu.sync_copy(idx_hbm.at[pl.ds(off, GATHER_ROWS)], idx_v)
            pltpu.async_copy(table_hbm.at[idx_v], rows_v, sem).wait()
            pltpu.sync_copy(rows_v, out_hbm.at[pl.ds(off, GATHER_ROWS)])

    return gather(table, idx)


def _combine_kernel(rows_ref, gates_ref, hf_ref, x1_ref, mod_ref, sg_ref, su_ref, sd_ref, gpost_ref, y_ref):
    nb, tt, d = hf_ref.shape
    m = nb * tt
    c = d // 2
    x = hf_ref[...].reshape(m, d)
    shared = _swiglu_halves(x[:, :c], x[:, c:], sg_ref[...], su_ref[...], sd_ref[...])
    gates = gates_ref[...].reshape(m, LANES)
    acc_lo = shared[:, :c]
    acc_hi = shared[:, c:]
    for r in range(TOP_K):
        lo, hi = _unpack_halves(rows_ref[r])
        g = gates[:, r:r + 1]
        acc_lo = acc_lo + g * lo
        acc_hi = acc_hi + g * hi
    z = _rms(jnp.concatenate([acc_lo, acc_hi], axis=1), gpost_ref[...]).reshape(nb, tt, d)
    y_ref[...] = x1_ref[...] + mod_ref[...][:, 5:6, :] * z


def _combine(rows, first_tile, gates_t, hf, x1, mod, sg, su, sd, g_post):
    n, t, d = hf.shape
    nb, tt = _seq_blocks(n, t)
    m = nb * tt
    steps_t = t // tt
    blk = lambda w: pl.BlockSpec((nb, tt, w), lambda i, j: (i, j, 0))
    const = lambda shape: pl.BlockSpec(shape, lambda i, j: (0,) * len(shape))
    return pl.pallas_call(
        _combine_kernel,
        grid=(n // nb, steps_t),
        in_specs=[pl.BlockSpec((TOP_K, m, d // 2), lambda i, j: (0, first_tile + i * steps_t + j, 0)),
                  blk(LANES), blk(d), blk(d), pl.BlockSpec((nb, 6, d), lambda i, j: (i, 0, 0)),
                  const((d, D_EXPERT)), const((d, D_EXPERT)), const((D_EXPERT, d)), const((1, d))],
        out_specs=blk(d),
        out_shape=jax.ShapeDtypeStruct((n, t, d), F32),
        compiler_params=_params("parallel", "parallel"),
        name="moe_combine",
    )(rows, gates_t, hf, x1, mod, sg, su, sd, g_post)


def _slots_kernel(starts_ref, ids_ref, ranks_ref, pos_ref):
    ids = ids_ref[...]

    def add_start(e, pos):
        return pos + jnp.where(ids == e, starts_ref[e], 0)

    pos_ref[...] = lax.fori_loop(0, N_EXPERTS, add_start, ranks_ref[...])


def _slots(starts, ids, ranks):
    k, m_tot = ids.shape
    blk = pl.BlockSpec((k, SLOT_COLS), lambda i: (0, i))
    return pl.pallas_call(
        _slots_kernel,
        grid=(m_tot // SLOT_COLS,),
        in_specs=[pl.BlockSpec(memory_space=pltpu.SMEM), blk, blk],
        out_specs=blk,
        out_shape=jax.ShapeDtypeStruct((k, m_tot), jnp.int32),
        compiler_params=_params("arbitrary"),
        name="moe_slots",
    )(starts, ids, ranks)


def _slot_plan(ids, ranks, counts):
    n_pairs = ids.shape[0] * ids.shape[1]
    n_tiles = -(-(n_pairs + N_EXPERTS * (EXPERT_TILE - 1)) // EXPERT_TILE)
    cnt = counts.reshape(N_EXPERTS).astype(jnp.int32)
    padded = (cnt + EXPERT_TILE - 1) // EXPERT_TILE * EXPERT_TILE
    ends = jnp.cumsum(padded)
    starts = ends - padded
    pos = _slots(starts, ids, ranks)
    tile_start = jnp.arange(n_tiles, dtype=jnp.int32) * EXPERT_TILE
    in_expert = (tile_start[:, None] >= starts[None, :]) & (tile_start[:, None] < ends[None, :])
    tile_expert = jnp.sum(jnp.where(in_expert, jnp.arange(N_EXPERTS, dtype=jnp.int32)[None, :], 0), axis=1)
    tile_fill = jnp.sum(jnp.where(in_expert, (starts + cnt)[None, :] - tile_start[:, None], 0), axis=1)
    tile_valid = jnp.clip(tile_fill, 0, EXPERT_TILE).astype(jnp.int32)
    return pos, tile_expert.astype(jnp.int32), tile_valid, n_tiles * EXPERT_TILE


def _block_diag(w):
    g, bw, _ = w.shape
    eye = jnp.eye(g, dtype=w.dtype)
    return (eye[:, None, :, None] * w[:, :, None, :]).reshape(g * bw, g * bw)


def _prep_weights(p):
    d_main = 2 * WL + 3 * WA
    w_in = p["w_in"]
    rw_t = p["router_w"].T
    rw_hi = rw_t.astype(BF16)
    row = lambda v: v.reshape(1, -1)
    return dict(
        w_mod=p["w_mod"], b_mod=p["b_mod"],
        g_pre_mix=row(p["g_pre_mix"]), g_post_mix=row(p["g_post_mix"]),
        g_pre_ffn=row(p["g_pre_ffn"]), g_post_ffn=row(p["g_post_ffn"]),
        w_main=w_in[:, :d_main].astype(BF16),
        w_f=jnp.pad(w_in[:, d_main:], ((0, 0), (0, LANES - N_HEADS))).astype(BF16),
        b_f=jnp.pad(p["b_f"], (0, LANES - N_HEADS)).reshape(1, LANES),
        conv_w=p["conv_w"], conv_b=row(p["conv_b"]),
        wr_bd=_block_diag(p["w_r"]).astype(BF16), b_r=row(p["b_r"]),
        wi_bd=_block_diag(p["w_i"]).astype(BF16), b_i=row(p["b_i"]),
        lam=row(p["lru_lambda"]), g_lru=row(p["g_lru_out"]), g_att=row(p["g_att_out"]),
        w_top=p["w_out"][:WL].astype(BF16), w_bot=p["w_out"][WL:].astype(BF16),
        rw_hi=rw_hi, rw_lo=(rw_t - rw_hi.astype(F32)).astype(BF16),
        r_bias=p["router_bias"].reshape(N_EXPERTS, 1),
        wg=p["w_gate"].astype(BF16), wu=p["w_up"].astype(BF16), wd=p["w_down"].astype(BF16),
        sg=p["ws_gate"].astype(BF16), su=p["ws_up"].astype(BF16), sd=p["ws_down"].astype(BF16),
    )


def _mixers(x, mod, conv0, h0, past, w, cnt_in):
    n, t, _ = x.shape
    proj_args = (x, mod, w["g_pre_mix"], w["w_main"], w["w_f"], w["b_f"])
    if past is None:
        xl, gy, k, v, lf, qt, ka, vt, bpre = _inproj_prompt(*proj_args)
        att = _att_prompt(qt, ka, vt, bpre)
    else:
        xl, gy, qb, kb, vb, k, v, lf, lfp = _inproj(*proj_args)
        d_new = _cumsum_heads(lfp, t)
        k_past, v_past, lf_past = past
        plen = k_past.shape[1]
        lf_past_p = jnp.pad(lf_past, ((0, 0), (0, 0), (0, LANES - N_HEADS)))
        d_past = _cumsum_heads(lf_past_p, ROW_TILE)
        att = _att_sample(qb, k_past.reshape(n, plen, WA), v_past.reshape(n, plen, WA), d_past, kb, vb, d_new)
    lru_n, conv_new, h_new = _lru(xl, gy, conv0, h0.reshape(n, 1, WL), w["conv_w"], w["conv_b"],
                                  w["wr_bd"], w["b_r"], w["wi_bd"], w["b_i"], w["lam"], w["g_lru"])
    routed = _outproj(x, lru_n, att, mod, w["g_att"], w["w_top"], w["w_bot"], w["g_post_mix"],
                      w["g_pre_ffn"], w["rw_hi"], w["rw_lo"], w["r_bias"], cnt_in)
    state = (k.reshape(n, t, N_HEADS, HEAD_DIM), v.reshape(n, t, N_HEADS, HEAD_DIM), lf,
             conv_new, h_new.reshape(n, WL))
    return routed, state


def _layer(xp, xs, mod_p, mod_s, conv_s, h_s, past_s, w):
    n_p = xp.shape[0]
    conv0 = jnp.zeros((n_p, CONV_W - 1, WL), F32)
    h0 = jnp.zeros((n_p, WL), F32)
    zero_cnt = jnp.zeros((N_EXPERTS, 1), F32)
    (x1_p, hf_p, xw_p, ids_p, rk_p, g_p, cnt_p), st_p = _mixers(xp, mod_p, conv0, h0, None, w, zero_cnt)
    (x1_s, hf_s, xw_s, ids_s, rk_s, g_s, cnt), st_s = _mixers(xs, mod_s, conv_s, h_s, past_s, w, cnt_p)

    half = xw_p.shape[-1]
    xw = jnp.concatenate([xw_p.reshape(-1, half), xw_s.reshape(-1, half)], axis=0)
    ids = jnp.concatenate([ids_p, ids_s], axis=1)
    ranks = jnp.concatenate([rk_p, rk_s], axis=1)
    pos, tile_expert, tile_valid, n_slots = _slot_plan(ids, ranks, cnt)
    ys = _experts(_scatter_rows(xw, pos, n_slots), tile_expert, tile_valid, w["wg"], w["wu"], w["wd"])
    m_p, m_tot = ids_p.shape[1], ids.shape[1]
    rows = _gather_rows(ys, pos.reshape(-1)).reshape(TOP_K, m_tot, half)
    shared = (w["sg"], w["su"], w["sd"], w["g_post_ffn"])
    yp = _combine(rows, 0, g_p, hf_p, x1_p, mod_p, *shared)
    ysmp = _combine(rows, m_p // ROW_TILE, g_s, hf_s, x1_s, mod_s, *shared)
    return yp, ysmp, st_p, st_s


def kernel(x_prompt, x_sample, c_prompt, c_sample, cache_k, cache_v, cache_logf, state_conv, state_lru, w_mod, b_mod, g_pre_mix, g_post_mix, g_pre_ffn, g_post_ffn, w_in, conv_w, conv_b, w_r, b_r, w_i, b_i, lru_lambda, b_f, g_lru_out, g_att_out, w_out, router_w, router_bias, w_gate, w_up, w_down, ws_gate, ws_up, ws_down):
    names = ("w_mod", "b_mod", "g_pre_mix", "g_post_mix", "g_pre_ffn", "g_post_ffn", "w_in", "conv_w", "conv_b",
             "w_r", "b_r", "w_i", "b_i", "lru_lambda", "b_f", "g_lru_out", "g_att_out", "w_out", "router_w",
             "router_bias", "w_gate", "w_up", "w_down", "ws_gate", "ws_up", "ws_down")
    stacked = (w_mod, b_mod, g_pre_mix, g_post_mix, g_pre_ffn, g_post_ffn, w_in, conv_w, conv_b, w_r, b_r, w_i, b_i,
               lru_lambda, b_f, g_lru_out, g_att_out, w_out, router_w, router_bias, w_gate, w_up, w_down,
               ws_gate, ws_up, ws_down)
    depth = w_mod.shape[0]
    n_p, n_s = x_prompt.shape[0], x_sample.shape[0]
    yp, ys = x_prompt, x_sample
    st_p, st_s = [], []
    for l in range(depth):
        w = _prep_weights({k: v[l] for k, v in zip(names, stacked)})
        mod = _modulation(jnp.concatenate([c_prompt, c_sample], axis=0), w["w_mod"], w["b_mod"])
        mod = mod.reshape(n_p + n_s, 6, D_MODEL)
        yp, ys, sp, ss = _layer(yp, ys, mod[:n_p], mod[n_p:], state_conv[l], state_lru[l],
                                (cache_k[l], cache_v[l], cache_logf[l]), w)
        st_p.append(sp)
        st_s.append(ss)
    stack = lambda sts, i: jnp.stack([s[i] for s in sts])
    return (yp, ys) + tuple(stack(st_p, i) for i in range(5)) + tuple(stack(st_s, i) for i in range(5))
```

```python
import functools

import jax
import jax.numpy as jnp
import numpy as np
from jax import lax
from jax.experimental import pallas as pl
from jax.experimental.pallas import tpu as pltpu
from jax.experimental.pallas import tpu_sc as plsc

F32 = jnp.float32
BF16 = jnp.bfloat16

D_MODEL = 1024
WL = 512
WA = 512
N_HEADS = 8
HEAD_DIM = 64
N_PAIRS = N_HEADS // 2
PAIR_W = 2 * HEAD_DIM
LANES = 128
CONV_W = 4
LRU_BLOCKS = 8
LRU_C = 8.0
N_EXPERTS = 64
N_GROUPS = 8
GROUP_SIZE = N_EXPERTS // N_GROUPS
TOPK_GROUPS = 4
TOP_K = 8
D_EXPERT = 256
ROUTED_SCALE = 2.5
EPS = 1e-6
NEG_INF = float("-inf")

ROW_TILE = 512
ATT_TQ = 512
ATT_TK = 512
ATT_CHUNK = 64
SAMPLE_TK = 1024
EXPERT_TILE = 512
SLOT_COLS = 2048
GATHER_ROWS = 64
VMEM_LIMIT = 56 * 1024 * 1024


def _params(*sem):
    return pltpu.CompilerParams(dimension_semantics=sem, vmem_limit_bytes=VMEM_LIMIT)


def _dot(a, b):
    return jnp.dot(a, b, preferred_element_type=F32)


def _dot_nt(a, b):
    return lax.dot_general(a, b, (((1,), (1,)), ((), ())), preferred_element_type=F32)


def _split3(x):
    hi = x.astype(BF16)
    r1 = x - hi.astype(F32)
    mid = r1.astype(BF16)
    lo = (r1 - mid.astype(F32)).astype(BF16)
    return hi, mid, lo


def _rms(x, g):
    return x * lax.rsqrt(jnp.mean(x * x, axis=-1, keepdims=True) + EPS) * g


def _sigmoid(x):
    return 1.0 / (1.0 + jnp.exp(-x))


def _silu(x):
    return x * _sigmoid(x)


def _gelu_tanh(x):
    return 0.5 * x * (1.0 + jnp.tanh(0.7978845608028654 * (x + 0.044715 * (x * x * x))))


def _log_sigmoid(x):
    return jnp.minimum(x, 0.0) - jnp.log1p(jnp.exp(-jnp.abs(x)))


def _seq_blocks(n, t):
    if t >= ROW_TILE:
        assert t % ROW_TILE == 0
        return 1, ROW_TILE
    nb = ROW_TILE // t
    assert nb * t == ROW_TILE and n % nb == 0
    return nb, t


def _mod_kernel(c_ref, w_ref, b_ref, o_ref):
    c = _silu(c_ref[...])
    c_hi = c.astype(BF16)
    c_lo = (c - c_hi.astype(F32)).astype(BF16)
    w = w_ref[...]
    w_hi = w.astype(BF16)
    w_lo = (w - w_hi.astype(F32)).astype(BF16)
    o_ref[...] = _dot(c_hi, w_hi) + _dot(c_lo, w_hi) + _dot(c_hi, w_lo) + b_ref[...]


def _modulation(c, w_mod, b_mod):
    rows = c.shape[0]
    n = -(-rows // 8) * 8
    c = jnp.pad(c, ((0, n - rows), (0, 0)))
    d6 = w_mod.shape[1]
    return pl.pallas_call(
        _mod_kernel,
        grid=(d6 // D_MODEL,),
        in_specs=[pl.BlockSpec((n, D_MODEL), lambda j: (0, 0)),
                  pl.BlockSpec((D_MODEL, D_MODEL), lambda j: (0, j)),
                  pl.BlockSpec((1, D_MODEL), lambda j: (0, j))],
        out_specs=pl.BlockSpec((n, D_MODEL), lambda j: (0, j)),
        out_shape=jax.ShapeDtypeStruct((n, d6), F32),
        compiler_params=_params("arbitrary"),
        name="modulation",
    )(c, w_mod, b_mod.reshape(1, d6))[:rows]


def _inproj_kernel(x_ref, mod_ref, g_ref, w_ref, wf_ref, bf_ref,
                   xl_ref, gy_ref, qb_ref, kb_ref, vb_ref, k_ref, v_ref, lf_ref):
    nb, tt, d = x_ref.shape
    x = x_ref[...]
    mod = mod_ref[...]
    hn = _rms(x, g_ref[...]) * (1.0 + mod[:, 1:2, :]) + mod[:, 0:1, :]
    hb = hn.reshape(nb * tt, d).astype(BF16)

    def proj(col):
        return _dot(hb, w_ref[:, col * WL:(col + 1) * WL]).reshape(nb, tt, WL)

    xl_ref[...] = proj(0)
    gy_ref[...] = _gelu_tanh(proj(1))
    qb_ref[...] = (proj(2) * (HEAD_DIM ** -0.5)).astype(BF16)
    k = proj(3)
    k_ref[...] = k
    kb_ref[...] = k.astype(BF16)
    v = proj(4)
    v_ref[...] = v
    vb_ref[...] = v.astype(BF16)
    fl = _dot(hb, wf_ref[...]) + bf_ref[...]
    lf_ref[...] = _log_sigmoid(fl).reshape(nb, tt, LANES)[:, :, :N_HEADS]


def _inproj(x, mod, g_pre, w_main, w_f, b_f):
    n, t, d = x.shape
    nb, tt = _seq_blocks(n, t)
    blk = lambda w: pl.BlockSpec((nb, tt, w), lambda i, j: (i, j, 0))
    const = lambda shape: pl.BlockSpec(shape, lambda i, j: (0,) * len(shape))
    f32 = lambda w: jax.ShapeDtypeStruct((n, t, w), F32)
    b16 = lambda w: jax.ShapeDtypeStruct((n, t, w), BF16)
    return pl.pallas_call(
        _inproj_kernel,
        grid=(n // nb, t // tt),
        in_specs=[blk(d),
                  pl.BlockSpec((nb, 6, d), lambda i, j: (i, 0, 0)),
                  const((1, d)), const(w_main.shape), const(w_f.shape), const((1, LANES))],
        out_specs=[blk(WL), blk(WL), blk(WA), blk(WA), blk(WA), blk(WA), blk(WA), blk(N_HEADS)],
        out_shape=[f32(WL), f32(WL), b16(WA), b16(WA), b16(WA), f32(WA), f32(WA), f32(N_HEADS)],
        compiler_params=_params("parallel", "arbitrary"),
        name="inproj",
    )(x, mod, g_pre, w_main, w_f, b_f)


def _aug_lane(h):
    return HEAD_DIM if h % 2 == 0 else 0


def _inproj_prompt_kernel(x_ref, mod_ref, g_ref, w_ref, wf_ref, bf_ref, place_ref,
                          xl_ref, gy_ref, k_ref, v_ref, lf_ref, qt_ref, ka_ref, vt_ref, bpre_ref, carry_ref):
    _, tt, d = x_ref.shape

    @pl.when(pl.program_id(1) == 0)
    def _():
        carry_ref[...] = jnp.zeros_like(carry_ref)

    mod = mod_ref[0]
    hb = (_rms(x_ref[0], g_ref[...]) * (1.0 + mod[1:2, :]) + mod[0:1, :]).astype(BF16)

    def proj(col):
        return _dot(hb, w_ref[:, col * WL:(col + 1) * WL])

    xl_ref[0] = proj(0)
    gy_ref[0] = _gelu_tanh(proj(1))
    q = proj(2) * (HEAD_DIM ** -0.5)
    k = proj(3)
    k_ref[0] = k
    v = proj(4)
    v_ref[0] = v
    fl = _dot(hb, wf_ref[...]) + bf_ref[...]
    lane = lax.broadcasted_iota(jnp.int32, (tt, LANES), 1)
    lf = jnp.where(lane < N_HEADS, _log_sigmoid(fl), 0.0)
    lf_ref[0] = lf[:, :N_HEADS]

    row = lax.broadcasted_iota(jnp.int32, (tt, tt), 0)
    col = lax.broadcasted_iota(jnp.int32, (tt, tt), 1)
    tril = (col <= row).astype(BF16)
    hi, mid, lo = _split3(lf)
    e = _dot(tril, hi) + _dot(tril, mid) + _dot(tril, lo)
    bpre_ref[0, 0] = carry_ref[...]
    carry_ref[...] += e[tt - 1:tt, :]
    e_hi, e_mid, e_lo = _split3(-e)
    aug_k = _dot(e_hi, place_ref[0]) + _dot(e_mid, place_ref[1]) + _dot(e_lo, place_ref[2])

    for h in range(N_HEADS):
        pair = slice((h // 2) * PAIR_W, (h // 2 + 1) * PAIR_W)
        dims = (lane < HEAD_DIM) if h % 2 == 0 else (lane >= HEAD_DIM)
        a0 = _aug_lane(h)
        ones3 = ((lane >= a0) & (lane < a0 + 3)).astype(F32)
        one1 = (lane == a0).astype(F32)
        qt_ref[0, h] = (jnp.where(dims, q[:, pair], 0.0) + ones3).T.astype(BF16)
        ka_ref[0, h] = (jnp.where(dims, k[:, pair], 0.0) + aug_k[:, h * LANES:(h + 1) * LANES]).astype(BF16)
        vt_ref[0, h] = (jnp.where(dims, v[:, pair], 0.0) + one1).T.astype(BF16)


def _placement():
    pl_mat = np.zeros((3, LANES, N_HEADS * LANES), np.float32)
    for p in range(3):
        for h in range(N_HEADS):
            pl_mat[p, h, h * LANES + _aug_lane(h) + p] = 1.0
    return jnp.asarray(pl_mat, BF16)


def _inproj_prompt(x, mod, g_pre, w_main, w_f, b_f):
    n, t, d = x.shape
    tt = ROW_TILE
    assert t % tt == 0
    nblk = t // tt
    blk = lambda w: pl.BlockSpec((1, tt, w), lambda i, j: (i, j, 0))
    const = lambda shape: pl.BlockSpec(shape, lambda i, j: (0,) * len(shape))
    f32 = lambda w: jax.ShapeDtypeStruct((n, t, w), F32)
    place = _placement()
    return pl.pallas_call(
        _inproj_prompt_kernel,
        grid=(n, nblk),
        in_specs=[blk(d), pl.BlockSpec((1, 6, d), lambda i, j: (i, 0, 0)),
                  const((1, d)), const(w_main.shape), const(w_f.shape), const((1, LANES)), const(place.shape)],
        out_specs=[blk(WL), blk(WL), blk(WA), blk(WA), blk(N_HEADS),
                   pl.BlockSpec((1, N_HEADS, LANES, tt), lambda i, j: (i, 0, 0, j)),
                   pl.BlockSpec((1, N_HEADS, tt, LANES), lambda i, j: (i, 0, j, 0)),
                   pl.BlockSpec((1, N_HEADS, LANES, tt), lambda i, j: (i, 0, 0, j)),
                   pl.BlockSpec((1, 1, 1, LANES), lambda i, j: (i, j, 0, 0))],
        out_shape=[f32(WL), f32(WL), f32(WA), f32(WA), f32(N_HEADS),
                   jax.ShapeDtypeStruct((n, N_HEADS, LANES, t), BF16),
                   jax.ShapeDtypeStruct((n, N_HEADS, t, LANES), BF16),
                   jax.ShapeDtypeStruct((n, N_HEADS, LANES, t), BF16),
                   jax.ShapeDtypeStruct((n, nblk, 1, LANES), F32)],
        scratch_shapes=[pltpu.VMEM((1, LANES), F32)],
        compiler_params=_params("parallel", "arbitrary"),
        name="inproj_prompt",
    )(x, mod, g_pre, w_main, w_f, b_f, place)


def _expm1_neg(x):
    poly = x * (1.0 + x * (0.5 + x * (1.0 / 6.0 + x * (1.0 / 24.0 + x * (1.0 / 120.0)))))
    return jnp.where(x > -0.1, poly, jnp.exp(x) - 1.0)


def _lru_kernel(xl_ref, gy_ref, conv0_ref, h0_ref, cw_ref, cb_ref, wr_ref, br_ref, wi_ref, bi_ref,
                lam_ref, g_ref, out_ref, conv_ref, hlast_ref, tail_ref, carry_ref):
    nb, tt, w = xl_ref.shape
    j = pl.program_id(1)

    @pl.when(j == 0)
    def _():
        tail_ref[:, 8 - (CONV_W - 1):, :] = conv0_ref[...]
        carry_ref[...] = h0_ref[...]

    xl = xl_ref[...]
    xpad = jnp.concatenate([tail_ref[...], xl], axis=1)
    cw = cw_ref[...]
    xc = jnp.zeros_like(xl) + cb_ref[...]
    for k in range(CONV_W):
        off = 8 - (CONV_W - 1) + k
        xc = xc + xpad[:, off:off + tt, :] * cw[k:k + 1, :]
    conv_ref[...] = xpad[:, tt + 8 - (CONV_W - 1):, :]
    tail_ref[...] = xpad[:, tt:, :]

    m = nb * tt
    xf = xc.reshape(m, w)
    xb = xf.astype(BF16)
    r = _sigmoid(_dot(xb, wr_ref[...]) + br_ref[...])
    gi = _sigmoid(_dot(xb, wi_ref[...]) + bi_ref[...])
    lam = lam_ref[...]
    softplus = jnp.maximum(-lam, 0.0) + jnp.log1p(jnp.exp(-jnp.abs(lam)))
    log_a = (-LRU_C) * r * softplus
    a = jnp.exp(log_a)
    b = jnp.sqrt(-_expm1_neg(2.0 * log_a)) * (gi * xf)

    pos = lax.broadcasted_iota(jnp.int32, (m, w), 0) % tt
    d = 1
    while d < tt:
        keep = pos >= d
        a_prev = jnp.where(keep, pltpu.roll(a, d, 0), 1.0)
        b_prev = jnp.where(keep, pltpu.roll(b, d, 0), 0.0)
        b = a * b_prev + b
        a = a * a_prev
        d *= 2
    h = a.reshape(nb, tt, w) * carry_ref[...] + b.reshape(nb, tt, w)
    h_last = h[:, tt - 1:tt, :]
    carry_ref[...] = h_last
    hlast_ref[...] = h_last
    out_ref[...] = _rms(h * gy_ref[...], g_ref[...]).astype(BF16)


def _lru(xl, gy, conv0, h0, conv_w, conv_b, wr_bd, b_r, wi_bd, b_i, lam, g_lru):
    n, t, w = xl.shape
    nb, tt = _seq_blocks(n, t)
    blk = pl.BlockSpec((nb, tt, w), lambda i, j: (i, j, 0))
    per_seq = lambda rows: pl.BlockSpec((nb, rows, w), lambda i, j: (i, 0, 0))
    const = lambda shape: pl.BlockSpec(shape, lambda i, j: (0,) * len(shape))
    row = const((1, w))
    return pl.pallas_call(
        _lru_kernel,
        grid=(n // nb, t // tt),
        in_specs=[blk, blk, per_seq(CONV_W - 1), per_seq(1),
                  const((CONV_W, w)), row, const((w, w)), row, const((w, w)), row, row, row],
        out_specs=[blk, per_seq(CONV_W - 1), per_seq(1)],
        out_shape=[jax.ShapeDtypeStruct((n, t, w), BF16),
                   jax.ShapeDtypeStruct((n, CONV_W - 1, w), F32),
                   jax.ShapeDtypeStruct((n, 1, w), F32)],
        scratch_shapes=[pltpu.VMEM((nb, 8, w), F32), pltpu.VMEM((nb, 1, w), F32)],
        compiler_params=_params("parallel", "arbitrary"),
        name="rglru",
    )(xl, gy, conv0, h0, conv_w, conv_b, wr_bd, b_r, wi_bd, b_i, lam, g_lru)


def _cumsum_rows_kernel(x_ref, upper_ref, o_ref, carry_ref):
    tb = x_ref.shape[1]

    @pl.when(pl.program_id(0) == 0)
    def _():
        carry_ref[...] = jnp.zeros_like(carry_ref)

    upper = upper_ref[...]
    hi, mid, lo = _split3(x_ref[...])
    d = _dot(hi, upper) + _dot(mid, upper) + _dot(lo, upper) + carry_ref[...]
    carry_ref[...] = d[:, tb - 1:tb]
    o_ref[...] = d


def _cumsum_rows(x, tb):
    rows, t = x.shape
    upper = jnp.asarray(np.triu(np.ones((tb, tb), np.float32)), BF16)
    return pl.pallas_call(
        _cumsum_rows_kernel,
        grid=(t // tb,),
        in_specs=[pl.BlockSpec((rows, tb), lambda j: (0, j)), pl.BlockSpec((tb, tb), lambda j: (0, 0))],
        out_specs=pl.BlockSpec((rows, tb), lambda j: (0, j)),
        out_shape=jax.ShapeDtypeStruct((rows, t), F32),
        scratch_shapes=[pltpu.VMEM((rows, 1), F32)],
        compiler_params=_params("arbitrary"),
        name="logf_cumsum",
    )(x, upper)


def _online_update(s, m_prev, l_prev):
    m_new = jnp.maximum(m_prev, jnp.max(s, axis=1, keepdims=True))
    alpha = jnp.exp(m_prev - m_new)
    p = jnp.exp(s - m_new)
    l_new = alpha * l_prev + jnp.sum(p, axis=1, keepdims=True)
    return p, alpha, m_new, l_new


def _att_prompt_kernel(bpre_ref, qt_ref, ka_ref, vt_ref, o_ref, s_ref, p_ref, acc_ref):
    tq = qt_ref.shape[2]
    tk = ATT_TK
    nblk = ka_ref.shape[1] // tk
    b, hp, i = pl.program_id(0), pl.program_id(1), pl.program_id(2)
    q0 = i * tq
    jd = q0 // tk
    kpos = lax.broadcasted_iota(jnp.int32, (tk, tq), 0)
    qpos = lax.broadcasted_iota(jnp.int32, (tk, tq), 1)
    rows = lax.broadcasted_iota(jnp.int32, (LANES, tq), 0)
    base = [((b * N_PAIRS + hp) * 2 + hh) * nblk for hh in range(2)]

    def scores(j, masked):
        start = pl.multiple_of(j * tk, tk)
        col_max = []
        for hh in range(2):
            s = _dot(ka_ref[hh, pl.ds(start, tk), :], qt_ref[hh])
            if masked:
                s = jnp.where(kpos + start <= qpos + q0, s, NEG_INF)
            s_ref[hh] = s
            col_max.append(jnp.max(s, axis=0, keepdims=True))
        return tuple(col_max)

    def softmax_pv(j, col_max, m):
        start = pl.multiple_of(j * tk, tk)
        m_out = []
        for hh in range(2):
            c = bpre_ref[base[hh] + jd] - bpre_ref[base[hh] + j]
            m_new = jnp.maximum(m[hh], col_max[hh] + c)
            alpha = jnp.exp(m[hh] - m_new)
            shift = m_new - c
            for ch in range(tk // ATT_CHUNK):
                sl = slice(ch * ATT_CHUNK, (ch + 1) * ATT_CHUNK)
                p_ref[hh, sl, :] = jnp.exp(s_ref[hh, sl, :] - shift).astype(BF16)
            m_out.append((m_new, alpha))
        return tuple(m_out), start

    def accumulate(m_alpha, start):
        for hh in range(2):
            pv = _dot(vt_ref[hh, :, pl.ds(start, tk)], p_ref[hh])
            acc_ref[hh] = m_alpha[hh][1] * acc_ref[hh] + pv
        return tuple(ma[0] for ma in m_alpha)

    def step(j, next_masked, carry):
        col_max, m = carry
        m_alpha, start = softmax_pv(j, col_max, m)
        col_max_next = scores(j + 1, next_masked)
        return col_max_next, accumulate(m_alpha, start)

    acc_ref[...] = jnp.zeros_like(acc_ref)
    neg = jnp.full((1, tq), NEG_INF, F32)
    n_masked = max(1, tq // tk)
    carry = (scores(0, True), (neg, neg))
    carry = lax.fori_loop(0, jd - 1, lambda j, cr: step(j, False, cr), carry)
    carry = lax.cond(jd > 0, lambda cr: step(jd - 1, True, cr), lambda cr: cr, carry)
    for extra in range(n_masked - 1):
        carry = step(jd + extra, True, carry)
    col_max, m = carry
    m_alpha, start = softmax_pv(jd + n_masked - 1, col_max, m)
    accumulate(m_alpha, start)

    acc_a, acc_b = acc_ref[0], acc_ref[1]
    out_a = acc_a / acc_a[_aug_lane(0):_aug_lane(0) + 1, :]
    out_b = acc_b / acc_b[_aug_lane(1):_aug_lane(1) + 1, :]
    o_ref[0] = jnp.where(rows < HEAD_DIM, out_a, out_b).T


def _att_prompt(qt, ka, vt, bpre):
    n, _, _, t = qt.shape
    assert t % ATT_TQ == 0 and (ATT_TK % ATT_TQ == 0 or ATT_TQ % ATT_TK == 0) and ATT_TK == ROW_TILE
    pair = lambda q: (N_PAIRS, 2) + q.shape[2:]
    qt, ka, vt = (a.reshape((n,) + pair(a)) for a in (qt, ka, vt))
    bflat = jnp.transpose(bpre[:, :, 0, :N_HEADS], (0, 2, 1)).reshape(-1)
    return pl.pallas_call(
        _att_prompt_kernel,
        grid=(n, N_PAIRS, t // ATT_TQ),
        in_specs=[pl.BlockSpec(memory_space=pltpu.SMEM),
                  pl.BlockSpec((None, None, 2, LANES, ATT_TQ), lambda b, h, i: (b, h, 0, 0, i)),
                  pl.BlockSpec((None, None, 2, t, LANES), lambda b, h, i: (b, h, 0, 0, 0),
                               pipeline_mode=pl.Buffered(1)),
                  pl.BlockSpec((None, None, 2, LANES, t), lambda b, h, i: (b, h, 0, 0, 0),
                               pipeline_mode=pl.Buffered(1))],
        out_specs=pl.BlockSpec((1, ATT_TQ, PAIR_W), lambda b, h, i: (b, i, h)),
        out_shape=jax.ShapeDtypeStruct((n, t, WA), F32),
        scratch_shapes=[pltpu.VMEM((2, ATT_TK, ATT_TQ), F32), pltpu.VMEM((2, ATT_TK, ATT_TQ), BF16),
                        pltpu.VMEM((2, LANES, ATT_TQ), F32)],
        compiler_params=_params("parallel", "parallel", "arbitrary"),
        name="att_prompt",
    )(bflat, qt, ka, vt)


def _att_sample_kernel(q_ref, kp_ref, vp_ref, dp_ref, dend_ref, kn_ref, vn_ref, dn_ref, o_ref,
                       m_ref, l_ref, acc_ref):
    t = q_ref.shape[1]
    j = pl.program_id(1)

    @pl.when(j == 0)
    def _():
        m_ref[...] = jnp.full_like(m_ref, NEG_INF)
        l_ref[...] = jnp.zeros_like(l_ref)
        acc_ref[...] = jnp.zeros_like(acc_ref)

    def head(h, k, v, bias, mask):
        sl = slice(h * HEAD_DIM, (h + 1) * HEAD_DIM)
        s = _dot_nt(q_ref[0, :, sl], k) + bias[h:h + 1]
        if mask is not None:
            s = jnp.where(mask, s, NEG_INF)
        p, alpha, m_new, l_new = _online_update(s, m_ref[h], l_ref[h])
        acc = alpha * acc_ref[:, sl] + _dot(p.astype(BF16), v)
        m_ref[h], l_ref[h], acc_ref[:, sl] = m_new, l_new, acc
        return acc / l_new

    n_rows = kp_ref.shape[1] // N_HEADS
    bias_past = dend_ref[0] - dp_ref[0]
    for h in range(N_HEADS):
        rows_h = pl.ds(h, n_rows, stride=N_HEADS)
        head(h, kp_ref[0, rows_h, :].astype(BF16), vp_ref[0, rows_h, :].astype(BF16), bias_past, None)

    @pl.when(j == pl.num_programs(1) - 1)
    def _():
        bias_new = -dn_ref[0]
        qpos = lax.broadcasted_iota(jnp.int32, (t, t), 0)
        kpos = lax.broadcasted_iota(jnp.int32, (t, t), 1)
        for h in range(N_HEADS):
            sl = slice(h * HEAD_DIM, (h + 1) * HEAD_DIM)
            o_ref[0, :, sl] = head(h, kn_ref[0, :, sl], vn_ref[0, :, sl], bias_new, kpos <= qpos)


def _att_sample(qb, k_past, v_past, d_past, kb_new, vb_new, d_new):
    n, t, _ = qb.shape
    past = k_past.shape[1]
    assert past % SAMPLE_TK == 0
    d_end = d_past[:, :, past - 1:]
    k_past = k_past.reshape(n, past * N_HEADS, HEAD_DIM)
    v_past = v_past.reshape(n, past * N_HEADS, HEAD_DIM)
    new = lambda dt: pl.BlockSpec((1, t, WA), lambda b, j: (b, 0, 0))
    cache = pl.BlockSpec((1, SAMPLE_TK * N_HEADS, HEAD_DIM), lambda b, j: (b, j, 0))
    return pl.pallas_call(
        _att_sample_kernel,
        grid=(n, past // SAMPLE_TK),
        in_specs=[new(BF16), cache, cache,
                  pl.BlockSpec((1, N_HEADS, SAMPLE_TK), lambda b, j: (b, 0, j)),
                  pl.BlockSpec((1, N_HEADS, 1), lambda b, j: (b, 0, 0)),
                  new(BF16), new(BF16),
                  pl.BlockSpec((1, N_HEADS, t), lambda b, j: (b, 0, 0))],
        out_specs=pl.BlockSpec((1, t, WA), lambda b, j: (b, 0, 0)),
        out_shape=jax.ShapeDtypeStruct((n, t, WA), F32),
        scratch_shapes=[pltpu.VMEM((N_HEADS, t, 1), F32), pltpu.VMEM((N_HEADS, t, 1), F32),
                        pltpu.VMEM((t, WA), F32)],
        compiler_params=_params("parallel", "arbitrary"),
        name="att_sample",
    )(qb, k_past, v_past, d_past, d_end, kb_new, vb_new, d_new)


def _first_index_of_max(x, axis):
    mx = jnp.max(x, axis=axis, keepdims=True)
    idx = lax.broadcasted_iota(jnp.int32, x.shape, axis)
    first = jnp.min(jnp.where(x == mx, idx, x.shape[axis]), axis=axis, keepdims=True)
    return mx, idx == first


def _route(s, bias):
    m = s.shape[1]
    sb = (s + bias).reshape(N_GROUPS, GROUP_SIZE, m)
    top1, is_top1 = _first_index_of_max(sb, 1)
    top2 = jnp.max(jnp.where(is_top1, NEG_INF, sb), axis=1, keepdims=True)
    grp = (top1 + top2).reshape(N_GROUPS, m)
    gi = lax.broadcasted_iota(jnp.int32, (N_GROUPS, N_GROUPS, m), 0)
    gj = lax.broadcasted_iota(jnp.int32, (N_GROUPS, N_GROUPS, m), 1)
    other, mine = grp[None, :, :], grp[:, None, :]
    beats = (other > mine) | ((other == mine) & (gj < gi))
    g_rank = jnp.sum(beats.astype(jnp.int32), axis=1)
    g_keep = (g_rank < TOPK_GROUPS)[:, None, :]
    cand = jnp.where(g_keep, sb, NEG_INF).reshape(N_EXPERTS, m)
    picks = []
    for _ in range(TOP_K):
        _, pick = _first_index_of_max(cand, 0)
        picks.append(pick)
        cand = jnp.where(pick, NEG_INF, cand)
    w = jnp.concatenate([jnp.sum(jnp.where(pk, s, 0.0), axis=0, keepdims=True) for pk in picks], axis=0)
    return picks, w / jnp.sum(w, axis=0, keepdims=True) * ROUTED_SCALE


HALF_MASK = 0xFFFF0000


def _pack_halves(x):
    c = x.shape[1] // 2
    lo = lax.bitcast_convert_type(x[:, :c].astype(BF16).astype(F32), jnp.uint32) >> jnp.uint32(16)
    hi = lax.bitcast_convert_type(x[:, c:].astype(BF16).astype(F32), jnp.uint32) & jnp.uint32(HALF_MASK)
    return lax.bitcast_convert_type(lo | hi, jnp.int32)


def _unpack_halves(words):
    w = lax.bitcast_convert_type(words, jnp.uint32)
    lo = lax.bitcast_convert_type(w << jnp.uint32(16), F32)
    hi = lax.bitcast_convert_type(w & jnp.uint32(HALF_MASK), F32)
    return lo, hi


def _outproj_kernel(x_ref, lru_ref, att_ref, mod_ref, gatt_ref, wtop_ref, wbot_ref, gpost_ref, gpre_ref,
                    rwh_ref, rwl_ref, rb_ref, cnt_in_ref,
                    x1_ref, hf_ref, xw_ref, ids_ref, ranks_ref, gates_ref, cnt_out_ref, carry_ref):
    nb, tt, d = x_ref.shape
    m = nb * tt

    @pl.when((pl.program_id(0) == 0) & (pl.program_id(1) == 0))
    def _():
        carry_ref[...] = cnt_in_ref[...]

    mod = mod_ref[...]
    att_n = _rms(att_ref[...], gatt_ref[...]).reshape(m, WA).astype(BF16)
    mix = _dot(lru_ref[...].reshape(m, WL), wtop_ref[...]) + _dot(att_n, wbot_ref[...])
    x1 = x_ref[...] + mod[:, 2:3, :] * _rms(mix, gpost_ref[...]).reshape(nb, tt, d)
    x1_ref[...] = x1
    hf = (_rms(x1, gpre_ref[...]) * (1.0 + mod[:, 4:5, :]) + mod[:, 3:4, :]).reshape(m, d)
    hf_hi = hf.astype(BF16)
    hf_ref[...] = hf_hi.reshape(nb, tt, d)
    hf_lo = (hf - hf_hi.astype(F32)).astype(BF16)
    rwh = rwh_ref[...]
    logits = _dot_nt(rwh, hf_hi) + _dot_nt(rwh, hf_lo) + _dot_nt(rwl_ref[...], hf_hi)
    picks, gates = _route(_sigmoid(logits), rb_ref[...])
    xw_ref[...] = _pack_halves(hf).reshape(nb, tt, d // 2)

    sel = jnp.zeros((N_EXPERTS, m), F32)
    for pk in picks:
        sel = sel + pk.astype(F32)
    before = (lax.broadcasted_iota(jnp.int32, (m, m), 0) < lax.broadcasted_iota(jnp.int32, (m, m), 1)).astype(BF16)
    prior = _dot(sel.astype(BF16), before) + carry_ref[...]
    expert = lax.broadcasted_iota(jnp.int32, (N_EXPERTS, m), 0).astype(F32)
    take = lambda pk, v: jnp.sum(jnp.where(pk, v, 0.0), axis=0, keepdims=True)
    ids_ref[...] = jnp.concatenate([take(pk, expert) for pk in picks], axis=0).astype(jnp.int32)
    ranks_ref[...] = jnp.concatenate([take(pk, prior) for pk in picks], axis=0).astype(jnp.int32)
    carry_ref[...] += jnp.sum(sel, axis=1, keepdims=True)
    cnt_out_ref[...] = carry_ref[...]
    gates = jnp.concatenate([gates, jnp.zeros((LANES - TOP_K, m), F32)], axis=0)
    gates_ref[...] = gates.T.reshape(nb, tt, LANES)


def _outproj(x, lru_n, att, mod, g_att, w_top, w_bot, g_post, g_pre, rw_hi, rw_lo, r_bias, cnt_in):
    n, t, d = x.shape
    nb, tt = _seq_blocks(n, t)
    m = nb * tt
    steps_t = t // tt
    blk = lambda w: pl.BlockSpec((nb, tt, w), lambda i, j: (i, j, 0))
    const = lambda shape: pl.BlockSpec(shape, lambda i, j: (0,) * len(shape))
    per_tok = pl.BlockSpec((TOP_K, m), lambda i, j: (0, i * steps_t + j))
    return pl.pallas_call(
        _outproj_kernel,
        grid=(n // nb, steps_t),
        in_specs=[blk(d), blk(WL), blk(WA), pl.BlockSpec((nb, 6, d), lambda i, j: (i, 0, 0)),
                  const((1, WA)), const((WL, d)), const((WA, d)), const((1, d)), const((1, d)),
                  const((N_EXPERTS, d)), const((N_EXPERTS, d)), const((N_EXPERTS, 1)), const((N_EXPERTS, 1))],
        out_specs=[blk(d), blk(d), blk(d // 2), per_tok, per_tok, blk(LANES), const((N_EXPERTS, 1))],
        out_shape=[jax.ShapeDtypeStruct((n, t, d), F32), jax.ShapeDtypeStruct((n, t, d), BF16),
                   jax.ShapeDtypeStruct((n, t, d // 2), jnp.int32),
                   jax.ShapeDtypeStruct((TOP_K, n * t), jnp.int32), jax.ShapeDtypeStruct((TOP_K, n * t), jnp.int32),
                   jax.ShapeDtypeStruct((n, t, LANES), F32), jax.ShapeDtypeStruct((N_EXPERTS, 1), F32)],
        scratch_shapes=[pltpu.VMEM((N_EXPERTS, 1), F32)],
        compiler_params=_params("arbitrary", "arbitrary"),
        name="outproj_router",
    )(x, lru_n, att, mod, g_att, w_top, w_bot, g_post, g_pre, rw_hi, rw_lo, r_bias, cnt_in)


def _subcore_ranges(n_items):
    info = plsc.get_sparse_core_info()
    n_workers = info.num_cores * info.num_subcores
    per_worker = n_items // n_workers
    assert per_worker * n_workers == n_items and per_worker % GATHER_ROWS == 0
    return info, plsc.VectorSubcoreMesh(core_axis_name="c", subcore_axis_name="s"), per_worker


def _scatter_rows(xw, pos, n_slots):
    m_tot, c = xw.shape
    info, mesh, per_worker = _subcore_ranges(m_tot)
    pos_flat = pos.reshape(-1)

    @functools.partial(
        pl.kernel, mesh=mesh, out_type=jax.ShapeDtypeStruct((n_slots, c), jnp.int32),
        scratch_types=[pltpu.VMEM((GATHER_ROWS,), jnp.int32), pltpu.VMEM((GATHER_ROWS, c), jnp.int32),
                       pltpu.SemaphoreType.DMA])
    def scatter(xw_hbm, pos_hbm, out_hbm, idx_v, rows_v, sem):
        worker = lax.axis_index("s") * info.num_cores + lax.axis_index("c")
        base = worker * per_worker

        @pl.loop(0, per_worker // GATHER_ROWS)
        def _(step):
            off = pl.multiple_of(base + step * GATHER_ROWS, GATHER_ROWS)
            pltpu.sync_copy(xw_hbm.at[pl.ds(off, GATHER_ROWS)], rows_v)
            for r in range(TOP_K):
                pltpu.sync_copy(pos_hbm.at[pl.ds(pl.multiple_of(r * m_tot + off, GATHER_ROWS), GATHER_ROWS)], idx_v)
                pltpu.async_copy(rows_v, out_hbm.at[idx_v], sem).wait()

    return scatter(xw, pos_flat)


def _swiglu_halves(lo, hi, wg, wu, wd):
    c = lo.shape[1]
    hg = _dot(lo, wg[:c]) + _dot(hi, wg[c:])
    hu = _dot(lo, wu[:c]) + _dot(hi, wu[c:])
    return _dot((_silu(hg) * hu).astype(BF16), wd)


def _expert_kernel(te_ref, valid_ref, x_ref, wg_ref, wu_ref, wd_ref, y_ref):
    i = pl.program_id(0)
    valid = valid_ref[i]

    @pl.when(valid > 0)
    def _():
        w = x_ref[...]
        row = lax.broadcasted_iota(jnp.int32, w.shape, 0)
        lo, hi = _unpack_halves(jnp.where(row < valid, w, 0))
        y_ref[...] = _pack_halves(_swiglu_halves(lo.astype(BF16), hi.astype(BF16), wg_ref[0], wu_ref[0], wd_ref[0]))


def _experts(xs, tile_expert, tile_valid, wg, wu, wd):
    n_slots, c = xs.shape
    n_tiles = n_slots // EXPERT_TILE
    d = 2 * c
    rows = pl.BlockSpec((EXPERT_TILE, c), lambda i, te, tv: (i, 0))
    weight = lambda shape: pl.BlockSpec((1,) + shape, lambda i, te, tv: (te[i], 0, 0))
    return pl.pallas_call(
        _expert_kernel,
        grid_spec=pltpu.PrefetchScalarGridSpec(
            num_scalar_prefetch=2, grid=(n_tiles,),
            in_specs=[rows, weight((d, D_EXPERT)), weight((d, D_EXPERT)), weight((D_EXPERT, d))],
            out_specs=rows),
        out_shape=jax.ShapeDtypeStruct((n_slots, c), jnp.int32),
        compiler_params=_params("arbitrary"),
        name="moe_experts",
    )(tile_expert, tile_valid, xs, wg, wu, wd)


def _gather_rows(table, idx):
    b, c = idx.shape[0], table.shape[1]
    info, mesh, per_worker = _subcore_ranges(b)

    n_steps = per_worker // GATHER_ROWS
    assert n_steps % 2 == 0

    @functools.partial(
        pl.kernel, mesh=mesh, out_type=jax.ShapeDtypeStruct((b, c), jnp.int32),
        scratch_types=[pltpu.VMEM((2, GATHER_ROWS), jnp.int32), pltpu.VMEM((2, GATHER_ROWS, c), jnp.int32),
                       pltpu.SemaphoreType.DMA((2,))])
    def gather(table_hbm, idx_hbm, out_hbm, idx_v, rows_v, sems):
        worker = lax.axis_index("s") * info.num_cores + lax.axis_index("c")
        base = worker * per_worker

        def chunk(step):
            return pl.ds(pl.multiple_of(base + step * GATHER_ROWS, GATHER_ROWS), GATHER_ROWS)

        def stream(buf):
            return pltpu.make_async_copy(table_hbm.at[idx_v.at[buf]], rows_v.at[buf], sems.at[buf])

        def start(step, buf):
            pltpu.sync_copy(idx_hbm.at[chunk(step)], idx_v.at[buf])
            stream(buf).start()

        def finish(step, buf):
            stream(buf).wait()
            pltpu.sync_copy(rows_v.at[buf], out_hbm.at[chunk(step)])

        start(0, 0)

        @pl.loop(0, n_steps, step=2)
        def _(step):
            start(step + 1, 1)
            finish(step, 0)

            @pl.when(step + 2 < n_steps)
            def _():
                start(step + 2, 0)

            finish(step + 1, 1)

    return gather(table, idx)


def _combine_kernel(rows_ref, gates_ref, hf_ref, x1_ref, mod_ref, sg_ref, su_ref, sd_ref, gpost_ref, y_ref):
    nb, tt, d = hf_ref.shape
    m = nb * tt
    c = d // 2
    x = hf_ref[...].reshape(m, d)
    shared = _swiglu_halves(x[:, :c], x[:, c:], sg_ref[...], su_ref[...], sd_ref[...])
    gates = gates_ref[...].reshape(m, LANES)
    acc_lo = shared[:, :c]
    acc_hi = shared[:, c:]
    for r in range(TOP_K):
        lo, hi = _unpack_halves(rows_ref[r])
        g = gates[:, r:r + 1]
        acc_lo = acc_lo + g * lo
        acc_hi = acc_hi + g * hi
    z = _rms(jnp.concatenate([acc_lo, acc_hi], axis=1), gpost_ref[...]).reshape(nb, tt, d)
    y_ref[...] = x1_ref[...] + mod_ref[...][:, 5:6, :] * z


def _combine(rows, first_tile, gates_t, hf, x1, mod, sg, su, sd, g_post):
    n, t, d = hf.shape
    nb, tt = _seq_blocks(n, t)
    m = nb * tt
    steps_t = t // tt
    blk = lambda w: pl.BlockSpec((nb, tt, w), lambda i, j: (i, j, 0))
    const = lambda shape: pl.BlockSpec(shape, lambda i, j: (0,) * len(shape))
    return pl.pallas_call(
        _combine_kernel,
        grid=(n // nb, steps_t),
        in_specs=[pl.BlockSpec((TOP_K, m, d // 2), lambda i, j: (0, first_tile + i * steps_t + j, 0)),
                  blk(LANES), blk(d), blk(d), pl.BlockSpec((nb, 6, d), lambda i, j: (i, 0, 0)),
                  const((d, D_EXPERT)), const((d, D_EXPERT)), const((D_EXPERT, d)), const((1, d))],
        out_specs=blk(d),
        out_shape=jax.ShapeDtypeStruct((n, t, d), F32),
        compiler_params=_params("parallel", "parallel"),
        name="moe_combine",
    )(rows, gates_t, hf, x1, mod, sg, su, sd, g_post)


def _slots_kernel(starts_ref, ids_ref, ranks_ref, pos_ref):
    ids = ids_ref[...]

    def add_start(e, pos):
        return pos + jnp.where(ids == e, starts_ref[e], 0)

    pos_ref[...] = lax.fori_loop(0, N_EXPERTS, add_start, ranks_ref[...])


def _slots(starts, ids, ranks):
    k, m_tot = ids.shape
    assert m_tot % SLOT_COLS == 0
    blk = pl.BlockSpec((k, SLOT_COLS), lambda i: (0, i))
    return pl.pallas_call(
        _slots_kernel,
        grid=(m_tot // SLOT_COLS,),
        in_specs=[pl.BlockSpec(memory_space=pltpu.SMEM), blk, blk],
        out_specs=blk,
        out_shape=jax.ShapeDtypeStruct((k, m_tot), jnp.int32),
        compiler_params=_params("arbitrary"),
        name="moe_slots",
    )(starts, ids, ranks)


def _slot_plan(ids, ranks, counts):
    n_pairs = ids.shape[0] * ids.shape[1]
    n_tiles = -(-(n_pairs + N_EXPERTS * (EXPERT_TILE - 1)) // EXPERT_TILE)
    cnt = counts.reshape(N_EXPERTS).astype(jnp.int32)
    padded = (cnt + EXPERT_TILE - 1) // EXPERT_TILE * EXPERT_TILE
    ends = jnp.cumsum(padded)
    starts = ends - padded
    pos = _slots(starts, ids, ranks)
    tile_start = jnp.arange(n_tiles, dtype=jnp.int32) * EXPERT_TILE
    in_expert = (tile_start[:, None] >= starts[None, :]) & (tile_start[:, None] < ends[None, :])
    tile_expert = jnp.sum(jnp.where(in_expert, jnp.arange(N_EXPERTS, dtype=jnp.int32)[None, :], 0), axis=1)
    tile_fill = jnp.sum(jnp.where(in_expert, (starts + cnt)[None, :] - tile_start[:, None], 0), axis=1)
    tile_valid = jnp.clip(tile_fill, 0, EXPERT_TILE).astype(jnp.int32)
    return pos, tile_expert.astype(jnp.int32), tile_valid, n_tiles * EXPERT_TILE


def _block_diag(w):
    g, bw, _ = w.shape
    eye = jnp.eye(g, dtype=w.dtype)
    return (eye[:, None, :, None] * w[:, :, None, :]).reshape(g * bw, g * bw)


def _prep_weights(p):
    d_main = 2 * WL + 3 * WA
    w_in = p["w_in"]
    rw_t = p["router_w"].T
    rw_hi = rw_t.astype(BF16)
    row = lambda v: v.reshape(1, -1)
    return dict(
        w_mod=p["w_mod"], b_mod=p["b_mod"],
        g_pre_mix=row(p["g_pre_mix"]), g_post_mix=row(p["g_post_mix"]),
        g_pre_ffn=row(p["g_pre_ffn"]), g_post_ffn=row(p["g_post_ffn"]),
        w_main=w_in[:, :d_main].astype(BF16),
        w_f=jnp.pad(w_in[:, d_main:], ((0, 0), (0, LANES - N_HEADS))).astype(BF16),
        b_f=jnp.pad(p["b_f"], (0, LANES - N_HEADS)).reshape(1, LANES),
        conv_w=p["conv_w"], conv_b=row(p["conv_b"]),
        wr_bd=_block_diag(p["w_r"]).astype(BF16), b_r=row(p["b_r"]),
        wi_bd=_block_diag(p["w_i"]).astype(BF16), b_i=row(p["b_i"]),
        lam=row(p["lru_lambda"]), g_lru=row(p["g_lru_out"]), g_att=row(p["g_att_out"]),
        w_top=p["w_out"][:WL].astype(BF16), w_bot=p["w_out"][WL:].astype(BF16),
        rw_hi=rw_hi, rw_lo=(rw_t - rw_hi.astype(F32)).astype(BF16),
        r_bias=p["router_bias"].reshape(N_EXPERTS, 1),
        wg=p["w_gate"].astype(BF16), wu=p["w_up"].astype(BF16), wd=p["w_down"].astype(BF16),
        sg=p["ws_gate"].astype(BF16), su=p["ws_up"].astype(BF16), sd=p["ws_down"].astype(BF16),
    )


def _mixers(x, mod, conv0, h0, past, w, cnt_in):
    n, t, _ = x.shape
    proj_args = (x, mod, w["g_pre_mix"], w["w_main"], w["w_f"], w["b_f"])
    if past is None:
        xl, gy, k, v, lf, qt, ka, vt, bpre = _inproj_prompt(*proj_args)
        att = _att_prompt(qt, ka, vt, bpre)
    else:
        xl, gy, qb, kb, vb, k, v, lf = _inproj(*proj_args)
        k_past, v_past, lf_past = past
        plen = k_past.shape[1]
        by_head = lambda a: jnp.transpose(a, (0, 2, 1)).reshape(n * N_HEADS, a.shape[1])
        d_new = _cumsum_rows(by_head(lf), t).reshape(n, N_HEADS, t)
        d_past = _cumsum_rows(by_head(lf_past), SAMPLE_TK).reshape(n, N_HEADS, plen)
        att = _att_sample(qb, k_past, v_past, d_past, kb, vb, d_new)
    lru_n, conv_new, h_new = _lru(xl, gy, conv0, h0.reshape(n, 1, WL), w["conv_w"], w["conv_b"],
                                  w["wr_bd"], w["b_r"], w["wi_bd"], w["b_i"], w["lam"], w["g_lru"])
    routed = _outproj(x, lru_n, att, mod, w["g_att"], w["w_top"], w["w_bot"], w["g_post_mix"],
                      w["g_pre_ffn"], w["rw_hi"], w["rw_lo"], w["r_bias"], cnt_in)
    state = (k.reshape(n, t, N_HEADS, HEAD_DIM), v.reshape(n, t, N_HEADS, HEAD_DIM), lf,
             conv_new, h_new.reshape(n, WL))
    return routed, state


def _layer(xp, xs, mod_p, mod_s, conv_s, h_s, past_s, w):
    n_p = xp.shape[0]
    conv0 = jnp.zeros((n_p, CONV_W - 1, WL), F32)
    h0 = jnp.zeros((n_p, WL), F32)
    zero_cnt = jnp.zeros((N_EXPERTS, 1), F32)
    (x1_p, hf_p, xw_p, ids_p, rk_p, g_p, cnt_p), st_p = _mixers(xp, mod_p, conv0, h0, None, w, zero_cnt)
    (x1_s, hf_s, xw_s, ids_s, rk_s, g_s, cnt), st_s = _mixers(xs, mod_s, conv_s, h_s, past_s, w, cnt_p)

    half = xw_p.shape[-1]
    xw = jnp.concatenate([xw_p.reshape(-1, half), xw_s.reshape(-1, half)], axis=0)
    ids = jnp.concatenate([ids_p, ids_s], axis=1)
    ranks = jnp.concatenate([rk_p, rk_s], axis=1)
    pos, tile_expert, tile_valid, n_slots = _slot_plan(ids, ranks, cnt)
    ys = _experts(_scatter_rows(xw, pos, n_slots), tile_expert, tile_valid, w["wg"], w["wu"], w["wd"])
    m_p, m_tot = ids_p.shape[1], ids.shape[1]
    rows = _gather_rows(ys, pos.reshape(-1)).reshape(TOP_K, m_tot, half)
    shared = (w["sg"], w["su"], w["sd"], w["g_post_ffn"])
    yp = _combine(rows, 0, g_p, hf_p, x1_p, mod_p, *shared)
    ysmp = _combine(rows, m_p // ROW_TILE, g_s, hf_s, x1_s, mod_s, *shared)
    return yp, ysmp, st_p, st_s


def kernel(x_prompt, x_sample, c_prompt, c_sample, cache_k, cache_v, cache_logf, state_conv, state_lru, w_mod, b_mod, g_pre_mix, g_post_mix, g_pre_ffn, g_post_ffn, w_in, conv_w, conv_b, w_r, b_r, w_i, b_i, lru_lambda, b_f, g_lru_out, g_att_out, w_out, router_w, router_bias, w_gate, w_up, w_down, ws_gate, ws_up, ws_down):
    names = ("w_mod", "b_mod", "g_pre_mix", "g_post_mix", "g_pre_ffn", "g_post_ffn", "w_in", "conv_w", "conv_b",
             "w_r", "b_r", "w_i", "b_i", "lru_lambda", "b_f", "g_lru_out", "g_att_out", "w_out", "router_w",
             "router_bias", "w_gate", "w_up", "w_down", "ws_gate", "ws_up", "ws_down")
    stacked = (w_mod, b_mod, g_pre_mix, g_post_mix, g_pre_ffn, g_post_ffn, w_in, conv_w, conv_b, w_r, b_r, w_i, b_i,
               lru_lambda, b_f, g_lru_out, g_att_out, w_out, router_w, router_bias, w_gate, w_up, w_down,
               ws_gate, ws_up, ws_down)
    depth = w_mod.shape[0]
    n_p, n_s = x_prompt.shape[0], x_sample.shape[0]
    yp, ys = x_prompt, x_sample
    st_p, st_s = [], []
    for l in range(depth):
        w = _prep_weights({k: v[l] for k, v in zip(names, stacked)})
        mod = _modulation(jnp.concatenate([c_prompt, c_sample], axis=0), w["w_mod"], w["b_mod"])
        mod = mod.reshape(n_p + n_s, 6, D_MODEL)
        yp, ys, sp, ss = _layer(yp, ys, mod[:n_p], mod[n_p:], state_conv[l], state_lru[l],
                                (cache_k[l], cache_v[l], cache_logf[l]), w)
        st_p.append(sp)
        st_s.append(ss)
    stack = lambda sts, i: jnp.stack([s[i] for s in sts])
    return (yp, ys) + tuple(stack(st_p, i) for i in range(5)) + tuple(stack(st_s, i) for i in range(5))
```

```python
import functools

import jax
import jax.numpy as jnp
import numpy as np
from jax import lax
from jax.experimental import pallas as pl
from jax.experimental.pallas import tpu as pltpu
from jax.experimental.pallas import tpu_sc as plsc

F32 = jnp.float32
BF16 = jnp.bfloat16

D_MODEL = 1024
WL = 512
WA = 512
N_HEADS = 8
HEAD_DIM = 64
N_PAIRS = N_HEADS // 2
PAIR_W = 2 * HEAD_DIM
LANES = 128
CONV_W = 4
LRU_BLOCKS = 8
LRU_C = 8.0
N_EXPERTS = 64
N_GROUPS = 8
GROUP_SIZE = N_EXPERTS // N_GROUPS
TOPK_GROUPS = 4
TOP_K = 8
D_EXPERT = 256
ROUTED_SCALE = 2.5
EPS = 1e-6
NEG_INF = float("-inf")

ROW_TILE = 512
ATT_TQ = 512
ATT_TK = 512
ATT_CHUNK = 64
SAMPLE_TK = 1024
EXPERT_TILE = 512
SLOT_COLS = 2048
GATHER_ROWS = 64
VMEM_LIMIT = 56 * 1024 * 1024


def _params(*sem):
    return pltpu.CompilerParams(dimension_semantics=sem, vmem_limit_bytes=VMEM_LIMIT)


def _dot(a, b):
    return jnp.dot(a, b, preferred_element_type=F32)


def _dot_nt(a, b):
    return lax.dot_general(a, b, (((1,), (1,)), ((), ())), preferred_element_type=F32)


def _split3(x):
    hi = x.astype(BF16)
    r1 = x - hi.astype(F32)
    mid = r1.astype(BF16)
    lo = (r1 - mid.astype(F32)).astype(BF16)
    return hi, mid, lo


def _rms(x, g):
    return x * lax.rsqrt(jnp.mean(x * x, axis=-1, keepdims=True) + EPS) * g


def _sigmoid(x):
    return 1.0 / (1.0 + jnp.exp(-x))


def _silu(x):
    return x * _sigmoid(x)


def _gelu_tanh(x):
    return 0.5 * x * (1.0 + jnp.tanh(0.7978845608028654 * (x + 0.044715 * (x * x * x))))


def _log_sigmoid(x):
    return jnp.minimum(x, 0.0) - jnp.log1p(jnp.exp(-jnp.abs(x)))


def _seq_blocks(n, t):
    if t >= ROW_TILE:
        assert t % ROW_TILE == 0
        return 1, ROW_TILE
    nb = ROW_TILE // t
    assert nb * t == ROW_TILE and n % nb == 0
    return nb, t


def _mod_kernel(c_ref, w_ref, b_ref, o_ref):
    c = _silu(c_ref[...])
    c_hi = c.astype(BF16)
    c_lo = (c - c_hi.astype(F32)).astype(BF16)
    w = w_ref[...]
    w_hi = w.astype(BF16)
    w_lo = (w - w_hi.astype(F32)).astype(BF16)
    o_ref[...] = _dot(c_hi, w_hi) + _dot(c_lo, w_hi) + _dot(c_hi, w_lo) + b_ref[...]


def _modulation(c, w_mod, b_mod):
    rows = c.shape[0]
    n = -(-rows // 8) * 8
    c = jnp.pad(c, ((0, n - rows), (0, 0)))
    d6 = w_mod.shape[1]
    return pl.pallas_call(
        _mod_kernel,
        grid=(d6 // D_MODEL,),
        in_specs=[pl.BlockSpec((n, D_MODEL), lambda j: (0, 0)),
                  pl.BlockSpec((D_MODEL, D_MODEL), lambda j: (0, j)),
                  pl.BlockSpec((1, D_MODEL), lambda j: (0, j))],
        out_specs=pl.BlockSpec((n, D_MODEL), lambda j: (0, j)),
        out_shape=jax.ShapeDtypeStruct((n, d6), F32),
        compiler_params=_params("arbitrary"),
        name="modulation",
    )(c, w_mod, b_mod.reshape(1, d6))[:rows]


def _inproj_kernel(x_ref, mod_ref, g_ref, w_ref, wf_ref, bf_ref,
                   xl_ref, gy_ref, qb_ref, kb_ref, vb_ref, k_ref, v_ref, lf_ref):
    nb, tt, d = x_ref.shape
    x = x_ref[...]
    mod = mod_ref[...]
    hn = _rms(x, g_ref[...]) * (1.0 + mod[:, 1:2, :]) + mod[:, 0:1, :]
    hb = hn.reshape(nb * tt, d).astype(BF16)

    def proj(col):
        return _dot(hb, w_ref[:, col * WL:(col + 1) * WL]).reshape(nb, tt, WL)

    xl_ref[...] = proj(0)
    gy_ref[...] = _gelu_tanh(proj(1))
    qb_ref[...] = (proj(2) * (HEAD_DIM ** -0.5)).astype(BF16)
    k = proj(3)
    k_ref[...] = k
    kb_ref[...] = k.astype(BF16)
    v = proj(4)
    v_ref[...] = v
    vb_ref[...] = v.astype(BF16)
    fl = _dot(hb, wf_ref[...]) + bf_ref[...]
    lf_ref[...] = _log_sigmoid(fl).reshape(nb, tt, LANES)[:, :, :N_HEADS]


def _inproj(x, mod, g_pre, w_main, w_f, b_f):
    n, t, d = x.shape
    nb, tt = _seq_blocks(n, t)
    blk = lambda w: pl.BlockSpec((nb, tt, w), lambda i, j: (i, j, 0))
    const = lambda shape: pl.BlockSpec(shape, lambda i, j: (0,) * len(shape))
    f32 = lambda w: jax.ShapeDtypeStruct((n, t, w), F32)
    b16 = lambda w: jax.ShapeDtypeStruct((n, t, w), BF16)
    return pl.pallas_call(
        _inproj_kernel,
        grid=(n // nb, t // tt),
        in_specs=[blk(d),
                  pl.BlockSpec((nb, 6, d), lambda i, j: (i, 0, 0)),
                  const((1, d)), const(w_main.shape), const(w_f.shape), const((1, LANES))],
        out_specs=[blk(WL), blk(WL), blk(WA), blk(WA), blk(WA), blk(WA), blk(WA), blk(N_HEADS)],
        out_shape=[f32(WL), f32(WL), b16(WA), b16(WA), b16(WA), f32(WA), f32(WA), f32(N_HEADS)],
        compiler_params=_params("parallel", "arbitrary"),
        name="inproj",
    )(x, mod, g_pre, w_main, w_f, b_f)


def _aug_lane(h):
    return HEAD_DIM if h % 2 == 0 else 0


def _inproj_prompt_kernel(x_ref, mod_ref, g_ref, w_ref, wf_ref, bf_ref, place_ref,
                          xl_ref, gy_ref, k_ref, v_ref, lf_ref, qt_ref, ka_ref, vt_ref, bpre_ref, carry_ref):
    _, tt, d = x_ref.shape

    @pl.when(pl.program_id(1) == 0)
    def _():
        carry_ref[...] = jnp.zeros_like(carry_ref)

    mod = mod_ref[0]
    hb = (_rms(x_ref[0], g_ref[...]) * (1.0 + mod[1:2, :]) + mod[0:1, :]).astype(BF16)

    def proj(col):
        return _dot(hb, w_ref[:, col * WL:(col + 1) * WL])

    xl_ref[0] = proj(0)
    gy_ref[0] = _gelu_tanh(proj(1))
    q = proj(2) * (HEAD_DIM ** -0.5)
    k = proj(3)
    k_ref[0] = k
    v = proj(4)
    v_ref[0] = v
    fl = _dot(hb, wf_ref[...]) + bf_ref[...]
    lane = lax.broadcasted_iota(jnp.int32, (tt, LANES), 1)
    lf = jnp.where(lane < N_HEADS, _log_sigmoid(fl), 0.0)
    lf_ref[0] = lf[:, :N_HEADS]

    row = lax.broadcasted_iota(jnp.int32, (tt, tt), 0)
    col = lax.broadcasted_iota(jnp.int32, (tt, tt), 1)
    tril = (col <= row).astype(BF16)
    hi, mid, lo = _split3(lf)
    e = _dot(tril, hi) + _dot(tril, mid) + _dot(tril, lo)
    bpre_ref[0, 0] = carry_ref[...]
    carry_ref[...] += e[tt - 1:tt, :]
    e_hi, e_mid, e_lo = _split3(-e)
    aug_k = _dot(e_hi, place_ref[0]) + _dot(e_mid, place_ref[1]) + _dot(e_lo, place_ref[2])

    for h in range(N_HEADS):
        pair = slice((h // 2) * PAIR_W, (h // 2 + 1) * PAIR_W)
        dims = (lane < HEAD_DIM) if h % 2 == 0 else (lane >= HEAD_DIM)
        a0 = _aug_lane(h)
        ones3 = ((lane >= a0) & (lane < a0 + 3)).astype(F32)
        one1 = (lane == a0).astype(F32)
        qt_ref[0, h] = (jnp.where(dims, q[:, pair], 0.0) + ones3).T.astype(BF16)
        ka_ref[0, h] = (jnp.where(dims, k[:, pair], 0.0) + aug_k[:, h * LANES:(h + 1) * LANES]).astype(BF16)
        vt_ref[0, h] = (jnp.where(dims, v[:, pair], 0.0) + one1).T.astype(BF16)


def _placement():
    pl_mat = np.zeros((3, LANES, N_HEADS * LANES), np.float32)
    for p in range(3):
        for h in range(N_HEADS):
            pl_mat[p, h, h * LANES + _aug_lane(h) + p] = 1.0
    return jnp.asarray(pl_mat, BF16)


def _inproj_prompt(x, mod, g_pre, w_main, w_f, b_f):
    n, t, d = x.shape
    tt = ROW_TILE
    assert t % tt == 0
    nblk = t // tt
    blk = lambda w: pl.BlockSpec((1, tt, w), lambda i, j: (i, j, 0))
    const = lambda shape: pl.BlockSpec(shape, lambda i, j: (0,) * len(shape))
    f32 = lambda w: jax.ShapeDtypeStruct((n, t, w), F32)
    place = _placement()
    return pl.pallas_call(
        _inproj_prompt_kernel,
        grid=(n, nblk),
        in_specs=[blk(d), pl.BlockSpec((1, 6, d), lambda i, j: (i, 0, 0)),
                  const((1, d)), const(w_main.shape), const(w_f.shape), const((1, LANES)), const(place.shape)],
        out_specs=[blk(WL), blk(WL), blk(WA), blk(WA), blk(N_HEADS),
                   pl.BlockSpec((1, N_HEADS, LANES, tt), lambda i, j: (i, 0, 0, j)),
                   pl.BlockSpec((1, N_HEADS, tt, LANES), lambda i, j: (i, 0, j, 0)),
                   pl.BlockSpec((1, N_HEADS, LANES, tt), lambda i, j: (i, 0, 0, j)),
                   pl.BlockSpec((1, 1, 1, LANES), lambda i, j: (i, j, 0, 0))],
        out_shape=[f32(WL), f32(WL), f32(WA), f32(WA), f32(N_HEADS),
                   jax.ShapeDtypeStruct((n, N_HEADS, LANES, t), BF16),
                   jax.ShapeDtypeStruct((n, N_HEADS, t, LANES), BF16),
                   jax.ShapeDtypeStruct((n, N_HEADS, LANES, t), BF16),
                   jax.ShapeDtypeStruct((n, nblk, 1, LANES), F32)],
        scratch_shapes=[pltpu.VMEM((1, LANES), F32)],
        compiler_params=_params("parallel", "arbitrary"),
        name="inproj_prompt",
    )(x, mod, g_pre, w_main, w_f, b_f, place)


def _expm1_neg(x):
    poly = x * (1.0 + x * (0.5 + x * (1.0 / 6.0 + x * (1.0 / 24.0 + x * (1.0 / 120.0)))))
    return jnp.where(x > -0.1, poly, jnp.exp(x) - 1.0)


def _lru_kernel(xl_ref, gy_ref, conv0_ref, h0_ref, cw_ref, cb_ref, wr_ref, br_ref, wi_ref, bi_ref,
                lam_ref, g_ref, out_ref, conv_ref, hlast_ref, tail_ref, carry_ref):
    nb, tt, w = xl_ref.shape
    j = pl.program_id(1)

    @pl.when(j == 0)
    def _():
        tail_ref[:, 8 - (CONV_W - 1):, :] = conv0_ref[...]
        carry_ref[...] = h0_ref[...]

    xl = xl_ref[...]
    xpad = jnp.concatenate([tail_ref[...], xl], axis=1)
    cw = cw_ref[...]
    xc = jnp.zeros_like(xl) + cb_ref[...]
    for k in range(CONV_W):
        off = 8 - (CONV_W - 1) + k
        xc = xc + xpad[:, off:off + tt, :] * cw[k:k + 1, :]
    conv_ref[...] = xpad[:, tt + 8 - (CONV_W - 1):, :]
    tail_ref[...] = xpad[:, tt:, :]

    m = nb * tt
    xf = xc.reshape(m, w)
    xb = xf.astype(BF16)
    r = _sigmoid(_dot(xb, wr_ref[...]) + br_ref[...])
    gi = _sigmoid(_dot(xb, wi_ref[...]) + bi_ref[...])
    lam = lam_ref[...]
    softplus = jnp.maximum(-lam, 0.0) + jnp.log1p(jnp.exp(-jnp.abs(lam)))
    log_a = (-LRU_C) * r * softplus
    a = jnp.exp(log_a)
    b = jnp.sqrt(-_expm1_neg(2.0 * log_a)) * (gi * xf)

    pos = lax.broadcasted_iota(jnp.int32, (m, w), 0) % tt
    d = 1
    while d < tt:
        keep = pos >= d
        a_prev = jnp.where(keep, pltpu.roll(a, d, 0), 1.0)
        b_prev = jnp.where(keep, pltpu.roll(b, d, 0), 0.0)
        b = a * b_prev + b
        a = a * a_prev
        d *= 2
    h = a.reshape(nb, tt, w) * carry_ref[...] + b.reshape(nb, tt, w)
    h_last = h[:, tt - 1:tt, :]
    carry_ref[...] = h_last
    hlast_ref[...] = h_last
    out_ref[...] = _rms(h * gy_ref[...], g_ref[...]).astype(BF16)


def _lru(xl, gy, conv0, h0, conv_w, conv_b, wr_bd, b_r, wi_bd, b_i, lam, g_lru):
    n, t, w = xl.shape
    nb, tt = _seq_blocks(n, t)
    blk = pl.BlockSpec((nb, tt, w), lambda i, j: (i, j, 0))
    per_seq = lambda rows: pl.BlockSpec((nb, rows, w), lambda i, j: (i, 0, 0))
    const = lambda shape: pl.BlockSpec(shape, lambda i, j: (0,) * len(shape))
    row = const((1, w))
    return pl.pallas_call(
        _lru_kernel,
        grid=(n // nb, t // tt),
        in_specs=[blk, blk, per_seq(CONV_W - 1), per_seq(1),
                  const((CONV_W, w)), row, const((w, w)), row, const((w, w)), row, row, row],
        out_specs=[blk, per_seq(CONV_W - 1), per_seq(1)],
        out_shape=[jax.ShapeDtypeStruct((n, t, w), BF16),
                   jax.ShapeDtypeStruct((n, CONV_W - 1, w), F32),
                   jax.ShapeDtypeStruct((n, 1, w), F32)],
        scratch_shapes=[pltpu.VMEM((nb, 8, w), F32), pltpu.VMEM((nb, 1, w), F32)],
        compiler_params=_params("parallel", "arbitrary"),
        name="rglru",
    )(xl, gy, conv0, h0, conv_w, conv_b, wr_bd, b_r, wi_bd, b_i, lam, g_lru)


def _cumsum_rows_kernel(x_ref, upper_ref, o_ref, carry_ref):
    tb = x_ref.shape[1]

    @pl.when(pl.program_id(0) == 0)
    def _():
        carry_ref[...] = jnp.zeros_like(carry_ref)

    upper = upper_ref[...]
    hi, mid, lo = _split3(x_ref[...])
    d = _dot(hi, upper) + _dot(mid, upper) + _dot(lo, upper) + carry_ref[...]
    carry_ref[...] = d[:, tb - 1:tb]
    o_ref[...] = d


def _cumsum_rows(x, tb):
    rows, t = x.shape
    upper = jnp.asarray(np.triu(np.ones((tb, tb), np.float32)), BF16)
    return pl.pallas_call(
        _cumsum_rows_kernel,
        grid=(t // tb,),
        in_specs=[pl.BlockSpec((rows, tb), lambda j: (0, j)), pl.BlockSpec((tb, tb), lambda j: (0, 0))],
        out_specs=pl.BlockSpec((rows, tb), lambda j: (0, j)),
        out_shape=jax.ShapeDtypeStruct((rows, t), F32),
        scratch_shapes=[pltpu.VMEM((rows, 1), F32)],
        compiler_params=_params("arbitrary"),
        name="logf_cumsum",
    )(x, upper)


def _online_update(s, m_prev, l_prev):
    m_new = jnp.maximum(m_prev, jnp.max(s, axis=1, keepdims=True))
    alpha = jnp.exp(m_prev - m_new)
    p = jnp.exp(s - m_new)
    l_new = alpha * l_prev + jnp.sum(p, axis=1, keepdims=True)
    return p, alpha, m_new, l_new


def _att_prompt_kernel(bpre_ref, qt_ref, ka_ref, vt_ref, o_ref, s_ref, p_ref, acc_ref):
    tq = qt_ref.shape[2]
    tk = ATT_TK
    nblk = ka_ref.shape[1] // tk
    b, hp, i = pl.program_id(0), pl.program_id(1), pl.program_id(2)
    q0 = i * tq
    jd = q0 // tk
    kpos = lax.broadcasted_iota(jnp.int32, (tk, tq), 0)
    qpos = lax.broadcasted_iota(jnp.int32, (tk, tq), 1)
    rows = lax.broadcasted_iota(jnp.int32, (LANES, tq), 0)
    base = [((b * N_PAIRS + hp) * 2 + hh) * nblk for hh in range(2)]

    def scores(j, masked):
        start = pl.multiple_of(j * tk, tk)
        col_max = []
        for hh in range(2):
            s = _dot(ka_ref[hh, pl.ds(start, tk), :], qt_ref[hh])
            if masked:
                s = jnp.where(kpos + start <= qpos + q0, s, NEG_INF)
            s_ref[hh] = s
            col_max.append(jnp.max(s, axis=0, keepdims=True))
        return tuple(col_max)

    def softmax_pv(j, col_max, m):
        start = pl.multiple_of(j * tk, tk)
        m_out = []
        for hh in range(2):
            c = bpre_ref[base[hh] + jd] - bpre_ref[base[hh] + j]
            m_new = jnp.maximum(m[hh], col_max[hh] + c)
            alpha = jnp.exp(m[hh] - m_new)
            shift = m_new - c
            for ch in range(tk // ATT_CHUNK):
                sl = slice(ch * ATT_CHUNK, (ch + 1) * ATT_CHUNK)
                p_ref[hh, sl, :] = jnp.exp(s_ref[hh, sl, :] - shift).astype(BF16)
            m_out.append((m_new, alpha))
        return tuple(m_out), start

    def accumulate(m_alpha, start):
        for hh in range(2):
            pv = _dot(vt_ref[hh, :, pl.ds(start, tk)], p_ref[hh])
            acc_ref[hh] = m_alpha[hh][1] * acc_ref[hh] + pv
        return tuple(ma[0] for ma in m_alpha)

    def step(j, next_masked, carry):
        col_max, m = carry
        m_alpha, start = softmax_pv(j, col_max, m)
        col_max_next = scores(j + 1, next_masked)
        return col_max_next, accumulate(m_alpha, start)

    acc_ref[...] = jnp.zeros_like(acc_ref)
    neg = jnp.full((1, tq), NEG_INF, F32)
    n_masked = max(1, tq // tk)
    carry = (scores(0, True), (neg, neg))
    carry = lax.fori_loop(0, jd - 1, lambda j, cr: step(j, False, cr), carry)
    carry = lax.cond(jd > 0, lambda cr: step(jd - 1, True, cr), lambda cr: cr, carry)
    for extra in range(n_masked - 1):
        carry = step(jd + extra, True, carry)
    col_max, m = carry
    m_alpha, start = softmax_pv(jd + n_masked - 1, col_max, m)
    accumulate(m_alpha, start)

    acc_a, acc_b = acc_ref[0], acc_ref[1]
    out_a = acc_a / acc_a[_aug_lane(0):_aug_lane(0) + 1, :]
    out_b = acc_b / acc_b[_aug_lane(1):_aug_lane(1) + 1, :]
    o_ref[0] = jnp.where(rows < HEAD_DIM, out_a, out_b).T


def _att_prompt(qt, ka, vt, bpre):
    n, _, _, t = qt.shape
    assert t % ATT_TQ == 0 and (ATT_TK % ATT_TQ == 0 or ATT_TQ % ATT_TK == 0) and ATT_TK == ROW_TILE
    pair = lambda q: (N_PAIRS, 2) + q.shape[2:]
    qt, ka, vt = (a.reshape((n,) + pair(a)) for a in (qt, ka, vt))
    bflat = jnp.transpose(bpre[:, :, 0, :N_HEADS], (0, 2, 1)).reshape(-1)
    return pl.pallas_call(
        _att_prompt_kernel,
        grid=(n, N_PAIRS, t // ATT_TQ),
        in_specs=[pl.BlockSpec(memory_space=pltpu.SMEM),
                  pl.BlockSpec((None, None, 2, LANES, ATT_TQ), lambda b, h, i: (b, h, 0, 0, i)),
                  pl.BlockSpec((None, None, 2, t, LANES), lambda b, h, i: (b, h, 0, 0, 0),
                               pipeline_mode=pl.Buffered(1)),
                  pl.BlockSpec((None, None, 2, LANES, t), lambda b, h, i: (b, h, 0, 0, 0),
                               pipeline_mode=pl.Buffered(1))],
        out_specs=pl.BlockSpec((1, ATT_TQ, PAIR_W), lambda b, h, i: (b, i, h)),
        out_shape=jax.ShapeDtypeStruct((n, t, WA), F32),
        scratch_shapes=[pltpu.VMEM((2, ATT_TK, ATT_TQ), F32), pltpu.VMEM((2, ATT_TK, ATT_TQ), BF16),
                        pltpu.VMEM((2, LANES, ATT_TQ), F32)],
        compiler_params=_params("parallel", "parallel", "arbitrary"),
        name="att_prompt",
    )(bflat, qt, ka, vt)


def _att_sample_kernel(q_ref, kp_ref, vp_ref, dp_ref, dend_ref, kn_ref, vn_ref, dn_ref, o_ref,
                       m_ref, l_ref, acc_ref):
    t = q_ref.shape[1]
    j = pl.program_id(1)

    @pl.when(j == 0)
    def _():
        m_ref[...] = jnp.full_like(m_ref, NEG_INF)
        l_ref[...] = jnp.zeros_like(l_ref)
        acc_ref[...] = jnp.zeros_like(acc_ref)

    def head(h, k, v, time_minor, bias, mask):
        sl = slice(h * HEAD_DIM, (h + 1) * HEAD_DIM)
        q_h = q_ref[0, :, sl]
        s = (_dot(q_h, k) if time_minor else _dot_nt(q_h, k)) + bias[h:h + 1]
        if mask is not None:
            s = jnp.where(mask, s, NEG_INF)
        p, alpha, m_new, l_new = _online_update(s, m_ref[h], l_ref[h])
        p = p.astype(BF16)
        acc = alpha * acc_ref[:, sl] + (_dot_nt(p, v) if time_minor else _dot(p, v))
        m_ref[h], l_ref[h], acc_ref[:, sl] = m_new, l_new, acc
        return acc / l_new

    bias_past = dend_ref[0] - dp_ref[0]
    for h in range(N_HEADS):
        head(h, kp_ref[0, h].astype(BF16), vp_ref[0, h].astype(BF16), True, bias_past, None)

    @pl.when(j == pl.num_programs(1) - 1)
    def _():
        bias_new = -dn_ref[0]
        qpos = lax.broadcasted_iota(jnp.int32, (t, t), 0)
        kpos = lax.broadcasted_iota(jnp.int32, (t, t), 1)
        for h in range(N_HEADS):
            sl = slice(h * HEAD_DIM, (h + 1) * HEAD_DIM)
            o_ref[0, :, sl] = head(h, kn_ref[0, :, sl], vn_ref[0, :, sl], False, bias_new, kpos <= qpos)


def _att_sample(qb, k_past, v_past, d_past, kb_new, vb_new, d_new):
    n, t, _ = qb.shape
    past = k_past.shape[1]
    assert past % SAMPLE_TK == 0
    d_end = d_past[:, :, past - 1:]
    k_past = jnp.transpose(k_past, (0, 2, 3, 1))
    v_past = jnp.transpose(v_past, (0, 2, 3, 1))
    new = lambda dt: pl.BlockSpec((1, t, WA), lambda b, j: (b, 0, 0))
    cache = pl.BlockSpec((1, N_HEADS, HEAD_DIM, SAMPLE_TK), lambda b, j: (b, 0, 0, j))
    return pl.pallas_call(
        _att_sample_kernel,
        grid=(n, past // SAMPLE_TK),
        in_specs=[new(BF16), cache, cache,
                  pl.BlockSpec((1, N_HEADS, SAMPLE_TK), lambda b, j: (b, 0, j)),
                  pl.BlockSpec((1, N_HEADS, 1), lambda b, j: (b, 0, 0)),
                  new(BF16), new(BF16),
                  pl.BlockSpec((1, N_HEADS, t), lambda b, j: (b, 0, 0))],
        out_specs=pl.BlockSpec((1, t, WA), lambda b, j: (b, 0, 0)),
        out_shape=jax.ShapeDtypeStruct((n, t, WA), F32),
        scratch_shapes=[pltpu.VMEM((N_HEADS, t, 1), F32), pltpu.VMEM((N_HEADS, t, 1), F32),
                        pltpu.VMEM((t, WA), F32)],
        compiler_params=_params("parallel", "arbitrary"),
        name="att_sample",
    )(qb, k_past, v_past, d_past, d_end, kb_new, vb_new, d_new)


def _first_index_of_max(x, axis):
    mx = jnp.max(x, axis=axis, keepdims=True)
    idx = lax.broadcasted_iota(jnp.int32, x.shape, axis)
    first = jnp.min(jnp.where(x == mx, idx, x.shape[axis]), axis=axis, keepdims=True)
    return mx, idx == first


def _route(s, bias):
    m = s.shape[1]
    sb = (s + bias).reshape(N_GROUPS, GROUP_SIZE, m)
    top1, is_top1 = _first_index_of_max(sb, 1)
    top2 = jnp.max(jnp.where(is_top1, NEG_INF, sb), axis=1, keepdims=True)
    grp = (top1 + top2).reshape(N_GROUPS, m)
    gi = lax.broadcasted_iota(jnp.int32, (N_GROUPS, N_GROUPS, m), 0)
    gj = lax.broadcasted_iota(jnp.int32, (N_GROUPS, N_GROUPS, m), 1)
    other, mine = grp[None, :, :], grp[:, None, :]
    beats = (other > mine) | ((other == mine) & (gj < gi))
    g_rank = jnp.sum(beats.astype(jnp.int32), axis=1)
    g_keep = (g_rank < TOPK_GROUPS)[:, None, :]
    cand = jnp.where(g_keep, sb, NEG_INF).reshape(N_EXPERTS, m)
    picks = []
    for _ in range(TOP_K):
        _, pick = _first_index_of_max(cand, 0)
        picks.append(pick)
        cand = jnp.where(pick, NEG_INF, cand)
    w = jnp.concatenate([jnp.sum(jnp.where(pk, s, 0.0), axis=0, keepdims=True) for pk in picks], axis=0)
    return picks, w / jnp.sum(w, axis=0, keepdims=True) * ROUTED_SCALE


HALF_MASK = 0xFFFF0000


def _pack_halves(x):
    c = x.shape[1] // 2
    lo = lax.bitcast_convert_type(x[:, :c].astype(BF16).astype(F32), jnp.uint32) >> jnp.uint32(16)
    hi = lax.bitcast_convert_type(x[:, c:].astype(BF16).astype(F32), jnp.uint32) & jnp.uint32(HALF_MASK)
    return lax.bitcast_convert_type(lo | hi, jnp.int32)


def _unpack_halves(words):
    w = lax.bitcast_convert_type(words, jnp.uint32)
    lo = lax.bitcast_convert_type(w << jnp.uint32(16), F32)
    hi = lax.bitcast_convert_type(w & jnp.uint32(HALF_MASK), F32)
    return lo, hi


def _outproj_kernel(x_ref, lru_ref, att_ref, mod_ref, gatt_ref, wtop_ref, wbot_ref, gpost_ref, gpre_ref,
                    rwh_ref, rwl_ref, rb_ref, cnt_in_ref,
                    x1_ref, hf_ref, xw_ref, ids_ref, ranks_ref, gates_ref, cnt_out_ref, carry_ref):
    nb, tt, d = x_ref.shape
    m = nb * tt

    @pl.when((pl.program_id(0) == 0) & (pl.program_id(1) == 0))
    def _():
        carry_ref[...] = cnt_in_ref[...]

    mod = mod_ref[...]
    att_n = _rms(att_ref[...], gatt_ref[...]).reshape(m, WA).astype(BF16)
    mix = _dot(lru_ref[...].reshape(m, WL), wtop_ref[...]) + _dot(att_n, wbot_ref[...])
    x1 = x_ref[...] + mod[:, 2:3, :] * _rms(mix, gpost_ref[...]).reshape(nb, tt, d)
    x1_ref[...] = x1
    hf = (_rms(x1, gpre_ref[...]) * (1.0 + mod[:, 4:5, :]) + mod[:, 3:4, :]).reshape(m, d)
    hf_hi = hf.astype(BF16)
    hf_ref[...] = hf_hi.reshape(nb, tt, d)
    hf_lo = (hf - hf_hi.astype(F32)).astype(BF16)
    rwh = rwh_ref[...]
    logits = _dot_nt(rwh, hf_hi) + _dot_nt(rwh, hf_lo) + _dot_nt(rwl_ref[...], hf_hi)
    picks, gates = _route(_sigmoid(logits), rb_ref[...])
    xw_ref[...] = _pack_halves(hf).reshape(nb, tt, d // 2)

    sel = jnp.zeros((N_EXPERTS, m), F32)
    for pk in picks:
        sel = sel + pk.astype(F32)
    before = (lax.broadcasted_iota(jnp.int32, (m, m), 0) < lax.broadcasted_iota(jnp.int32, (m, m), 1)).astype(BF16)
    prior = _dot(sel.astype(BF16), before) + carry_ref[...]
    expert = lax.broadcasted_iota(jnp.int32, (N_EXPERTS, m), 0).astype(F32)
    take = lambda pk, v: jnp.sum(jnp.where(pk, v, 0.0), axis=0, keepdims=True)
    ids_ref[...] = jnp.concatenate([take(pk, expert) for pk in picks], axis=0).astype(jnp.int32)
    ranks_ref[...] = jnp.concatenate([take(pk, prior) for pk in picks], axis=0).astype(jnp.int32)
    carry_ref[...] += jnp.sum(sel, axis=1, keepdims=True)
    cnt_out_ref[...] = carry_ref[...]
    gates = jnp.concatenate([gates, jnp.zeros((LANES - TOP_K, m), F32)], axis=0)
    gates_ref[...] = gates.T.reshape(nb, tt, LANES)


def _outproj(x, lru_n, att, mod, g_att, w_top, w_bot, g_post, g_pre, rw_hi, rw_lo, r_bias, cnt_in):
    n, t, d = x.shape
    nb, tt = _seq_blocks(n, t)
    m = nb * tt
    steps_t = t // tt
    blk = lambda w: pl.BlockSpec((nb, tt, w), lambda i, j: (i, j, 0))
    const = lambda shape: pl.BlockSpec(shape, lambda i, j: (0,) * len(shape))
    per_tok = pl.BlockSpec((TOP_K, m), lambda i, j: (0, i * steps_t + j))
    return pl.pallas_call(
        _outproj_kernel,
        grid=(n // nb, steps_t),
        in_specs=[blk(d), blk(WL), blk(WA), pl.BlockSpec((nb, 6, d), lambda i, j: (i, 0, 0)),
                  const((1, WA)), const((WL, d)), const((WA, d)), const((1, d)), const((1, d)),
                  const((N_EXPERTS, d)), const((N_EXPERTS, d)), const((N_EXPERTS, 1)), const((N_EXPERTS, 1))],
        out_specs=[blk(d), blk(d), blk(d // 2), per_tok, per_tok, blk(LANES), const((N_EXPERTS, 1))],
        out_shape=[jax.ShapeDtypeStruct((n, t, d), F32), jax.ShapeDtypeStruct((n, t, d), BF16),
                   jax.ShapeDtypeStruct((n, t, d // 2), jnp.int32),
                   jax.ShapeDtypeStruct((TOP_K, n * t), jnp.int32), jax.ShapeDtypeStruct((TOP_K, n * t), jnp.int32),
                   jax.ShapeDtypeStruct((n, t, LANES), F32), jax.ShapeDtypeStruct((N_EXPERTS, 1), F32)],
        scratch_shapes=[pltpu.VMEM((N_EXPERTS, 1), F32)],
        compiler_params=_params("arbitrary", "arbitrary"),
        name="outproj_router",
    )(x, lru_n, att, mod, g_att, w_top, w_bot, g_post, g_pre, rw_hi, rw_lo, r_bias, cnt_in)


def _subcore_ranges(n_items):
    info = plsc.get_sparse_core_info()
    n_workers = info.num_cores * info.num_subcores
    per_worker = n_items // n_workers
    assert per_worker * n_workers == n_items and per_worker % GATHER_ROWS == 0
    return info, plsc.VectorSubcoreMesh(core_axis_name="c", subcore_axis_name="s"), per_worker


def _scatter_rows(xw, pos, n_slots):
    m_tot, c = xw.shape
    info, mesh, per_worker = _subcore_ranges(m_tot)
    pos_flat = pos.reshape(-1)

    @functools.partial(
        pl.kernel, mesh=mesh, out_type=jax.ShapeDtypeStruct((n_slots, c), jnp.int32),
        scratch_types=[pltpu.VMEM((GATHER_ROWS,), jnp.int32), pltpu.VMEM((GATHER_ROWS, c), jnp.int32),
                       pltpu.SemaphoreType.DMA])
    def scatter(xw_hbm, pos_hbm, out_hbm, idx_v, rows_v, sem):
        worker = lax.axis_index("s") * info.num_cores + lax.axis_index("c")
        base = worker * per_worker

        @pl.loop(0, per_worker // GATHER_ROWS)
        def _(step):
            off = pl.multiple_of(base + step * GATHER_ROWS, GATHER_ROWS)
            pltpu.sync_copy(xw_hbm.at[pl.ds(off, GATHER_ROWS)], rows_v)
            for r in range(TOP_K):
                pltpu.sync_copy(pos_hbm.at[pl.ds(pl.multiple_of(r * m_tot + off, GATHER_ROWS), GATHER_ROWS)], idx_v)
                pltpu.async_copy(rows_v, out_hbm.at[idx_v], sem).wait()

    return scatter(xw, pos_flat)


def _swiglu_halves(lo, hi, wg, wu, wd):
    c = lo.shape[1]
    hg = _dot(lo, wg[:c]) + _dot(hi, wg[c:])
    hu = _dot(lo, wu[:c]) + _dot(hi, wu[c:])
    return _dot((_silu(hg) * hu).astype(BF16), wd)


def _expert_kernel(te_ref, valid_ref, x_ref, wg_ref, wu_ref, wd_ref, y_ref):
    i = pl.program_id(0)
    valid = valid_ref[i]

    @pl.when(valid > 0)
    def _():
        w = x_ref[...]
        row = lax.broadcasted_iota(jnp.int32, w.shape, 0)
        lo, hi = _unpack_halves(jnp.where(row < valid, w, 0))
        y_ref[...] = _pack_halves(_swiglu_halves(lo.astype(BF16), hi.astype(BF16), wg_ref[0], wu_ref[0], wd_ref[0]))


def _experts(xs, tile_expert, tile_valid, wg, wu, wd):
    n_slots, c = xs.shape
    n_tiles = n_slots // EXPERT_TILE
    d = 2 * c
    rows = pl.BlockSpec((EXPERT_TILE, c), lambda i, te, tv: (i, 0))
    weight = lambda shape: pl.BlockSpec((1,) + shape, lambda i, te, tv: (te[i], 0, 0))
    return pl.pallas_call(
        _expert_kernel,
        grid_spec=pltpu.PrefetchScalarGridSpec(
            num_scalar_prefetch=2, grid=(n_tiles,),
            in_specs=[rows, weight((d, D_EXPERT)), weight((d, D_EXPERT)), weight((D_EXPERT, d))],
            out_specs=rows),
        out_shape=jax.ShapeDtypeStruct((n_slots, c), jnp.int32),
        compiler_params=_params("arbitrary"),
        name="moe_experts",
    )(tile_expert, tile_valid, xs, wg, wu, wd)


def _gather_rows(table, idx):
    b, c = idx.shape[0], table.shape[1]
    info, mesh, per_worker = _subcore_ranges(b)

    n_steps = per_worker // GATHER_ROWS
    assert n_steps % 2 == 0

    @functools.partial(
        pl.kernel, mesh=mesh, out_type=jax.ShapeDtypeStruct((b, c), jnp.int32),
        scratch_types=[pltpu.VMEM((2, GATHER_ROWS), jnp.int32), pltpu.VMEM((2, GATHER_ROWS, c), jnp.int32),
                       pltpu.SemaphoreType.DMA((2,))])
    def gather(table_hbm, idx_hbm, out_hbm, idx_v, rows_v, sems):
        worker = lax.axis_index("s") * info.num_cores + lax.axis_index("c")
        base = worker * per_worker

        def chunk(step):
            return pl.ds(pl.multiple_of(base + step * GATHER_ROWS, GATHER_ROWS), GATHER_ROWS)

        def stream(buf):
            return pltpu.make_async_copy(table_hbm.at[idx_v.at[buf]], rows_v.at[buf], sems.at[buf])

        def start(step, buf):
            pltpu.sync_copy(idx_hbm.at[chunk(step)], idx_v.at[buf])
            stream(buf).start()

        def finish(step, buf):
            stream(buf).wait()
            pltpu.sync_copy(rows_v.at[buf], out_hbm.at[chunk(step)])

        start(0, 0)

        @pl.loop(0, n_steps, step=2)
        def _(step):
            start(step + 1, 1)
            finish(step, 0)

            @pl.when(step + 2 < n_steps)
            def _():
                start(step + 2, 0)

            finish(step + 1, 1)

    return gather(table, idx)


def _combine_kernel(rows_ref, gates_ref, hf_ref, x1_ref, mod_ref, sg_ref, su_ref, sd_ref, gpost_ref, y_ref):
    nb, tt, d = hf_ref.shape
    m = nb * tt
    c = d // 2
    x = hf_ref[...].reshape(m, d)
    shared = _swiglu_halves(x[:, :c], x[:, c:], sg_ref[...], su_ref[...], sd_ref[...])
    gates = gates_ref[...].reshape(m, LANES)
    acc_lo = shared[:, :c]
    acc_hi = shared[:, c:]
    for r in range(TOP_K):
        lo, hi = _unpack_halves(rows_ref[r])
        g = gates[:, r:r + 1]
        acc_lo = acc_lo + g * lo
        acc_hi = acc_hi + g * hi
    z = _rms(jnp.concatenate([acc_lo, acc_hi], axis=1), gpost_ref[...]).reshape(nb, tt, d)
    y_ref[...] = x1_ref[...] + mod_ref[...][:, 5:6, :] * z


def _combine(rows, first_tile, gates_t, hf, x1, mod, sg, su, sd, g_post):
    n, t, d = hf.shape
    nb, tt = _seq_blocks(n, t)
    m = nb * tt
    steps_t = t // tt
    blk = lambda w: pl.BlockSpec((nb, tt, w), lambda i, j: (i, j, 0))
    const = lambda shape: pl.BlockSpec(shape, lambda i, j: (0,) * len(shape))
    return pl.pallas_call(
        _combine_kernel,
        grid=(n // nb, steps_t),
        in_specs=[pl.BlockSpec((TOP_K, m, d // 2), lambda i, j: (0, first_tile + i * steps_t + j, 0)),
                  blk(LANES), blk(d), blk(d), pl.BlockSpec((nb, 6, d), lambda i, j: (i, 0, 0)),
                  const((d, D_EXPERT)), const((d, D_EXPERT)), const((D_EXPERT, d)), const((1, d))],
        out_specs=blk(d),
        out_shape=jax.ShapeDtypeStruct((n, t, d), F32),
        compiler_params=_params("parallel", "parallel"),
        name="moe_combine",
    )(rows, gates_t, hf, x1, mod, sg, su, sd, g_post)


def _slots_kernel(starts_ref, ids_ref, ranks_ref, pos_ref):
    ids = ids_ref[...]

    def add_start(e, pos):
        return pos + jnp.where(ids == e, starts_ref[e], 0)

    pos_ref[...] = lax.fori_loop(0, N_EXPERTS, add_start, ranks_ref[...])


def _slots(starts, ids, ranks):
    k, m_tot = ids.shape
    assert m_tot % SLOT_COLS == 0
    blk = pl.BlockSpec((k, SLOT_COLS), lambda i: (0, i))
    return pl.pallas_call(
        _slots_kernel,
        grid=(m_tot // SLOT_COLS,),
        in_specs=[pl.BlockSpec(memory_space=pltpu.SMEM), blk, blk],
        out_specs=blk,
        out_shape=jax.ShapeDtypeStruct((k, m_tot), jnp.int32),
        compiler_params=_params("arbitrary"),
        name="moe_slots",
    )(starts, ids, ranks)


def _slot_plan(ids, ranks, counts):
    n_pairs = ids.shape[0] * ids.shape[1]
    n_tiles = -(-(n_pairs + N_EXPERTS * (EXPERT_TILE - 1)) // EXPERT_TILE)
    cnt = counts.reshape(N_EXPERTS).astype(jnp.int32)
    padded = (cnt + EXPERT_TILE - 1) // EXPERT_TILE * EXPERT_TILE
    ends = jnp.cumsum(padded)
    starts = ends - padded
    pos = _slots(starts, ids, ranks)
    tile_start = jnp.arange(n_tiles, dtype=jnp.int32) * EXPERT_TILE
    in_expert = (tile_start[:, None] >= starts[None, :]) & (tile_start[:, None] < ends[None, :])
    tile_expert = jnp.sum(jnp.where(in_expert, jnp.arange(N_EXPERTS, dtype=jnp.int32)[None, :], 0), axis=1)
    tile_fill = jnp.sum(jnp.where(in_expert, (starts + cnt)[None, :] - tile_start[:, None], 0), axis=1)
    tile_valid = jnp.clip(tile_fill, 0, EXPERT_TILE).astype(jnp.int32)
    return pos, tile_expert.astype(jnp.int32), tile_valid, n_tiles * EXPERT_TILE


def _block_diag(w):
    g, bw, _ = w.shape
    eye = jnp.eye(g, dtype=w.dtype)
    return (eye[:, None, :, None] * w[:, :, None, :]).reshape(g * bw, g * bw)


def _prep_weights(p):
    d_main = 2 * WL + 3 * WA
    w_in = p["w_in"]
    rw_t = p["router_w"].T
    rw_hi = rw_t.astype(BF16)
    row = lambda v: v.reshape(1, -1)
    return dict(
        w_mod=p["w_mod"], b_mod=p["b_mod"],
        g_pre_mix=row(p["g_pre_mix"]), g_post_mix=row(p["g_post_mix"]),
        g_pre_ffn=row(p["g_pre_ffn"]), g_post_ffn=row(p["g_post_ffn"]),
        w_main=w_in[:, :d_main].astype(BF16),
        w_f=jnp.pad(w_in[:, d_main:], ((0, 0), (0, LANES - N_HEADS))).astype(BF16),
        b_f=jnp.pad(p["b_f"], (0, LANES - N_HEADS)).reshape(1, LANES),
        conv_w=p["conv_w"], conv_b=row(p["conv_b"]),
        wr_bd=_block_diag(p["w_r"]).astype(BF16), b_r=row(p["b_r"]),
        wi_bd=_block_diag(p["w_i"]).astype(BF16), b_i=row(p["b_i"]),
        lam=row(p["lru_lambda"]), g_lru=row(p["g_lru_out"]), g_att=row(p["g_att_out"]),
        w_top=p["w_out"][:WL].astype(BF16), w_bot=p["w_out"][WL:].astype(BF16),
        rw_hi=rw_hi, rw_lo=(rw_t - rw_hi.astype(F32)).astype(BF16),
        r_bias=p["router_bias"].reshape(N_EXPERTS, 1),
        wg=p["w_gate"].astype(BF16), wu=p["w_up"].astype(BF16), wd=p["w_down"].astype(BF16),
        sg=p["ws_gate"].astype(BF16), su=p["ws_up"].astype(BF16), sd=p["ws_down"].astype(BF16),
    )


def _mixers(x, mod, conv0, h0, past, w, cnt_in):
    n, t, _ = x.shape
    proj_args = (x, mod, w["g_pre_mix"], w["w_main"], w["w_f"], w["b_f"])
    if past is None:
        xl, gy, k, v, lf, qt, ka, vt, bpre = _inproj_prompt(*proj_args)
        att = _att_prompt(qt, ka, vt, bpre)
    else:
        xl, gy, qb, kb, vb, k, v, lf = _inproj(*proj_args)
        k_past, v_past, lf_past = past
        plen = k_past.shape[1]
        by_head = lambda a: jnp.transpose(a, (0, 2, 1)).reshape(n * N_HEADS, a.shape[1])
        d_new = _cumsum_rows(by_head(lf), t).reshape(n, N_HEADS, t)
        d_past = _cumsum_rows(by_head(lf_past), SAMPLE_TK).reshape(n, N_HEADS, plen)
        att = _att_sample(qb, k_past, v_past, d_past, kb, vb, d_new)
    lru_n, conv_new, h_new = _lru(xl, gy, conv0, h0.reshape(n, 1, WL), w["conv_w"], w["conv_b"],
                                  w["wr_bd"], w["b_r"], w["wi_bd"], w["b_i"], w["lam"], w["g_lru"])
    routed = _outproj(x, lru_n, att, mod, w["g_att"], w["w_top"], w["w_bot"], w["g_post_mix"],
                      w["g_pre_ffn"], w["rw_hi"], w["rw_lo"], w["r_bias"], cnt_in)
    state = (k.reshape(n, t, N_HEADS, HEAD_DIM), v.reshape(n, t, N_HEADS, HEAD_DIM), lf,
             conv_new, h_new.reshape(n, WL))
    return routed, state


def _layer(xp, xs, mod_p, mod_s, conv_s, h_s, past_s, w):
    n_p = xp.shape[0]
    conv0 = jnp.zeros((n_p, CONV_W - 1, WL), F32)
    h0 = jnp.zeros((n_p, WL), F32)
    zero_cnt = jnp.zeros((N_EXPERTS, 1), F32)
    (x1_p, hf_p, xw_p, ids_p, rk_p, g_p, cnt_p), st_p = _mixers(xp, mod_p, conv0, h0, None, w, zero_cnt)
    (x1_s, hf_s, xw_s, ids_s, rk_s, g_s, cnt), st_s = _mixers(xs, mod_s, conv_s, h_s, past_s, w, cnt_p)

    half = xw_p.shape[-1]
    xw = jnp.concatenate([xw_p.reshape(-1, half), xw_s.reshape(-1, half)], axis=0)
    ids = jnp.concatenate([ids_p, ids_s], axis=1)
    ranks = jnp.concatenate([rk_p, rk_s], axis=1)
    pos, tile_expert, tile_valid, n_slots = _slot_plan(ids, ranks, cnt)
    ys = _experts(_scatter_rows(xw, pos, n_slots), tile_expert, tile_valid, w["wg"], w["wu"], w["wd"])
    m_p, m_tot = ids_p.shape[1], ids.shape[1]
    rows = _gather_rows(ys, pos.reshape(-1)).reshape(TOP_K, m_tot, half)
    shared = (w["sg"], w["su"], w["sd"], w["g_post_ffn"])
    yp = _combine(rows, 0, g_p, hf_p, x1_p, mod_p, *shared)
    ysmp = _combine(rows, m_p // ROW_TILE, g_s, hf_s, x1_s, mod_s, *shared)
    return yp, ysmp, st_p, st_s


def kernel(x_prompt, x_sample, c_prompt, c_sample, cache_k, cache_v, cache_logf, state_conv, state_lru, w_mod, b_mod, g_pre_mix, g_post_mix, g_pre_ffn, g_post_ffn, w_in, conv_w, conv_b, w_r, b_r, w_i, b_i, lru_lambda, b_f, g_lru_out, g_att_out, w_out, router_w, router_bias, w_gate, w_up, w_down, ws_gate, ws_up, ws_down):
    names = ("w_mod", "b_mod", "g_pre_mix", "g_post_mix", "g_pre_ffn", "g_post_ffn", "w_in", "conv_w", "conv_b",
             "w_r", "b_r", "w_i", "b_i", "lru_lambda", "b_f", "g_lru_out", "g_att_out", "w_out", "router_w",
             "router_bias", "w_gate", "w_up", "w_down", "ws_gate", "ws_up", "ws_down")
    stacked = (w_mod, b_mod, g_pre_mix, g_post_mix, g_pre_ffn, g_post_ffn, w_in, conv_w, conv_b, w_r, b_r, w_i, b_i,
               lru_lambda, b_f, g_lru_out, g_att_out, w_out, router_w, router_bias, w_gate, w_up, w_down,
               ws_gate, ws_up, ws_down)
    depth = w_mod.shape[0]
    n_p, n_s = x_prompt.shape[0], x_sample.shape[0]
    yp, ys = x_prompt, x_sample
    st_p, st_s = [], []
    for l in range(depth):
        w = _prep_weights({k: v[l] for k, v in zip(names, stacked)})
        mod = _modulation(jnp.concatenate([c_prompt, c_sample], axis=0), w["w_mod"], w["b_mod"])
        mod = mod.reshape(n_p + n_s, 6, D_MODEL)
        yp, ys, sp, ss = _layer(yp, ys, mod[:n_p], mod[n_p:], state_conv[l], state_lru[l],
                                (cache_k[l], cache_v[l], cache_logf[l]), w)
        st_p.append(sp)
        st_s.append(ss)
    stack = lambda sts, i: jnp.stack([s[i] for s in sts])
    return (yp, ys) + tuple(stack(st_p, i) for i in range(5)) + tuple(stack(st_s, i) for i in range(5))
```

```python
import functools

import jax
import jax.numpy as jnp
import numpy as np
from jax import lax
from jax.experimental import pallas as pl
from jax.experimental.pallas import tpu as pltpu
from jax.experimental.pallas import tpu_sc as plsc

F32 = jnp.float32
BF16 = jnp.bfloat16

D_MODEL = 1024
WL = 512
WA = 512
N_HEADS = 8
HEAD_DIM = 64
N_PAIRS = N_HEADS // 2
PAIR_W = 2 * HEAD_DIM
LANES = 128
CONV_W = 4
LRU_BLOCKS = 8
LRU_C = 8.0
N_EXPERTS = 64
N_GROUPS = 8
GROUP_SIZE = N_EXPERTS // N_GROUPS
TOPK_GROUPS = 4
TOP_K = 8
D_EXPERT = 256
ROUTED_SCALE = 2.5
EPS = 1e-6
NEG_INF = float("-inf")
LOG2E = 1.4426950408889634

ROW_TILE = 512
ATT_TQ = 512
ATT_TK = 512
ATT_CHUNK = 64
SAMPLE_TK = 4096
CUMSUM_COLS = 1024
EXPERT_TILE = 512
SLOT_COLS = 2048
GATHER_ROWS = 64
VMEM_LIMIT = 56 * 1024 * 1024


def _params(*sem):
    return pltpu.CompilerParams(dimension_semantics=sem, vmem_limit_bytes=VMEM_LIMIT)


def _dot(a, b):
    return jnp.dot(a, b, preferred_element_type=F32)


def _dot_nt(a, b):
    return lax.dot_general(a, b, (((1,), (1,)), ((), ())), preferred_element_type=F32)


def _split3(x):
    hi = x.astype(BF16)
    r1 = x - hi.astype(F32)
    mid = r1.astype(BF16)
    lo = (r1 - mid.astype(F32)).astype(BF16)
    return hi, mid, lo


def _rms(x, g):
    return x * lax.rsqrt(jnp.mean(x * x, axis=-1, keepdims=True) + EPS) * g


def _sigmoid(x):
    return 1.0 / (1.0 + jnp.exp(-x))


def _silu(x):
    return x * _sigmoid(x)


def _gelu_tanh(x):
    return 0.5 * x * (1.0 + jnp.tanh(0.7978845608028654 * (x + 0.044715 * (x * x * x))))


def _log_sigmoid(x):
    return jnp.minimum(x, 0.0) - jnp.log1p(jnp.exp(-jnp.abs(x)))


def _seq_blocks(n, t):
    if t >= ROW_TILE:
        assert t % ROW_TILE == 0
        return 1, ROW_TILE
    nb = ROW_TILE // t
    assert nb * t == ROW_TILE and n % nb == 0
    return nb, t


def _mod_kernel(c_ref, w_ref, b_ref, o_ref):
    c = _silu(c_ref[...])
    c_hi = c.astype(BF16)
    c_lo = (c - c_hi.astype(F32)).astype(BF16)
    w = w_ref[...]
    w_hi = w.astype(BF16)
    w_lo = (w - w_hi.astype(F32)).astype(BF16)
    o_ref[...] = _dot(c_hi, w_hi) + _dot(c_lo, w_hi) + _dot(c_hi, w_lo) + b_ref[...]


def _modulation(c, w_mod, b_mod):
    rows = c.shape[0]
    n = -(-rows // 8) * 8
    c = jnp.pad(c, ((0, n - rows), (0, 0)))
    d6 = w_mod.shape[1]
    return pl.pallas_call(
        _mod_kernel,
        grid=(d6 // D_MODEL,),
        in_specs=[pl.BlockSpec((n, D_MODEL), lambda j: (0, 0)),
                  pl.BlockSpec((D_MODEL, D_MODEL), lambda j: (0, j)),
                  pl.BlockSpec((1, D_MODEL), lambda j: (0, j))],
        out_specs=pl.BlockSpec((n, D_MODEL), lambda j: (0, j)),
        out_shape=jax.ShapeDtypeStruct((n, d6), F32),
        compiler_params=_params("arbitrary"),
        name="modulation",
    )(c, w_mod, b_mod.reshape(1, d6))[:rows]


def _inproj_kernel(x_ref, mod_ref, g_ref, w_ref, wf_ref, bf_ref,
                   xl_ref, gy_ref, qb_ref, kb_ref, vb_ref, k_ref, v_ref, lf_ref):
    nb, tt, d = x_ref.shape
    x = x_ref[...]
    mod = mod_ref[...]
    hn = _rms(x, g_ref[...]) * (1.0 + mod[:, 1:2, :]) + mod[:, 0:1, :]
    hb = hn.reshape(nb * tt, d).astype(BF16)

    def proj(col):
        return _dot(hb, w_ref[:, col * WL:(col + 1) * WL]).reshape(nb, tt, WL)

    xl_ref[...] = proj(0)
    gy_ref[...] = _gelu_tanh(proj(1))
    qb_ref[...] = (proj(2) * (HEAD_DIM ** -0.5)).astype(BF16)
    k = proj(3)
    k_ref[...] = k
    kb_ref[...] = k.astype(BF16)
    v = proj(4)
    v_ref[...] = v
    vb_ref[...] = v.astype(BF16)
    fl = _dot(hb, wf_ref[...]) + bf_ref[...]
    lf_ref[...] = _log_sigmoid(fl).reshape(nb, tt, LANES)[:, :, :N_HEADS]


def _inproj(x, mod, g_pre, w_main, w_f, b_f):
    n, t, d = x.shape
    nb, tt = _seq_blocks(n, t)
    blk = lambda w: pl.BlockSpec((nb, tt, w), lambda i, j: (i, j, 0))
    const = lambda shape: pl.BlockSpec(shape, lambda i, j: (0,) * len(shape))
    f32 = lambda w: jax.ShapeDtypeStruct((n, t, w), F32)
    b16 = lambda w: jax.ShapeDtypeStruct((n, t, w), BF16)
    return pl.pallas_call(
        _inproj_kernel,
        grid=(n // nb, t // tt),
        in_specs=[blk(d),
                  pl.BlockSpec((nb, 6, d), lambda i, j: (i, 0, 0)),
                  const((1, d)), const(w_main.shape), const(w_f.shape), const((1, LANES))],
        out_specs=[blk(WL), blk(WL), blk(WA), blk(WA), blk(WA), blk(WA), blk(WA), blk(N_HEADS)],
        out_shape=[f32(WL), f32(WL), b16(WA), b16(WA), b16(WA), f32(WA), f32(WA), f32(N_HEADS)],
        compiler_params=_params("parallel", "arbitrary"),
        name="inproj",
    )(x, mod, g_pre, w_main, w_f, b_f)


def _aug_lane(h):
    return HEAD_DIM if h % 2 == 0 else 0


def _inproj_prompt_kernel(x_ref, mod_ref, g_ref, w_ref, wf_ref, bf_ref, place_ref,
                          xl_ref, gy_ref, k_ref, v_ref, lf_ref, qt_ref, ka_ref, vt_ref, bpre_ref, carry_ref):
    _, tt, d = x_ref.shape

    @pl.when(pl.program_id(1) == 0)
    def _():
        carry_ref[...] = jnp.zeros_like(carry_ref)

    mod = mod_ref[0]
    hb = (_rms(x_ref[0], g_ref[...]) * (1.0 + mod[1:2, :]) + mod[0:1, :]).astype(BF16)

    def proj(col):
        return _dot(hb, w_ref[:, col * WL:(col + 1) * WL])

    xl_ref[0] = proj(0)
    gy_ref[0] = _gelu_tanh(proj(1))
    q = proj(2) * (HEAD_DIM ** -0.5 * LOG2E)
    k = proj(3)
    k_ref[0] = k
    v = proj(4)
    v_ref[0] = v
    fl = _dot(hb, wf_ref[...]) + bf_ref[...]
    lane = lax.broadcasted_iota(jnp.int32, (tt, LANES), 1)
    lf = jnp.where(lane < N_HEADS, _log_sigmoid(fl), 0.0)
    lf_ref[0] = lf[:, :N_HEADS]

    row = lax.broadcasted_iota(jnp.int32, (tt, tt), 0)
    col = lax.broadcasted_iota(jnp.int32, (tt, tt), 1)
    tril = (col <= row).astype(BF16)
    hi, mid, lo = _split3(lf)
    e = _dot(tril, hi) + _dot(tril, mid) + _dot(tril, lo)
    bpre_ref[0, 0] = carry_ref[...]
    carry_ref[...] += e[tt - 1:tt, :]
    e_hi, e_mid, e_lo = _split3(e * (-LOG2E))
    aug_k = _dot(e_hi, place_ref[0]) + _dot(e_mid, place_ref[1]) + _dot(e_lo, place_ref[2])

    for h in range(N_HEADS):
        pair = slice((h // 2) * PAIR_W, (h // 2 + 1) * PAIR_W)
        dims = (lane < HEAD_DIM) if h % 2 == 0 else (lane >= HEAD_DIM)
        a0 = _aug_lane(h)
        ones3 = ((lane >= a0) & (lane < a0 + 3)).astype(F32)
        one1 = (lane == a0).astype(F32)
        qt_ref[0, h] = (jnp.where(dims, q[:, pair], 0.0) + ones3).T.astype(BF16)
        ka_ref[0, h] = (jnp.where(dims, k[:, pair], 0.0) + aug_k[:, h * LANES:(h + 1) * LANES]).astype(BF16)
        vt_ref[0, h] = (jnp.where(dims, v[:, pair], 0.0) + one1).T.astype(BF16)


def _placement():
    pl_mat = np.zeros((3, LANES, N_HEADS * LANES), np.float32)
    for p in range(3):
        for h in range(N_HEADS):
            pl_mat[p, h, h * LANES + _aug_lane(h) + p] = 1.0
    return jnp.asarray(pl_mat, BF16)


def _inproj_prompt(x, mod, g_pre, w_main, w_f, b_f):
    n, t, d = x.shape
    tt = ROW_TILE
    assert t % tt == 0
    nblk = t // tt
    blk = lambda w: pl.BlockSpec((1, tt, w), lambda i, j: (i, j, 0))
    const = lambda shape: pl.BlockSpec(shape, lambda i, j: (0,) * len(shape))
    f32 = lambda w: jax.ShapeDtypeStruct((n, t, w), F32)
    place = _placement()
    return pl.pallas_call(
        _inproj_prompt_kernel,
        grid=(n, nblk),
        in_specs=[blk(d), pl.BlockSpec((1, 6, d), lambda i, j: (i, 0, 0)),
                  const((1, d)), const(w_main.shape), const(w_f.shape), const((1, LANES)), const(place.shape)],
        out_specs=[blk(WL), blk(WL), blk(WA), blk(WA), blk(N_HEADS),
                   pl.BlockSpec((1, N_HEADS, LANES, tt), lambda i, j: (i, 0, 0, j)),
                   pl.BlockSpec((1, N_HEADS, tt, LANES), lambda i, j: (i, 0, j, 0)),
                   pl.BlockSpec((1, N_HEADS, LANES, tt), lambda i, j: (i, 0, 0, j)),
                   pl.BlockSpec((1, 1, 1, LANES), lambda i, j: (i, j, 0, 0))],
        out_shape=[f32(WL), f32(WL), f32(WA), f32(WA), f32(N_HEADS),
                   jax.ShapeDtypeStruct((n, N_HEADS, LANES, t), BF16),
                   jax.ShapeDtypeStruct((n, N_HEADS, t, LANES), BF16),
                   jax.ShapeDtypeStruct((n, N_HEADS, LANES, t), BF16),
                   jax.ShapeDtypeStruct((n, nblk, 1, LANES), F32)],
        scratch_shapes=[pltpu.VMEM((1, LANES), F32)],
        compiler_params=_params("parallel", "arbitrary"),
        name="inproj_prompt",
    )(x, mod, g_pre, w_main, w_f, b_f, place)


def _expm1_neg(x):
    poly = x * (1.0 + x * (0.5 + x * (1.0 / 6.0 + x * (1.0 / 24.0 + x * (1.0 / 120.0)))))
    return jnp.where(x > -0.1, poly, jnp.exp(x) - 1.0)


def _lru_kernel(xl_ref, gy_ref, conv0_ref, h0_ref, cw_ref, cb_ref, wr_ref, br_ref, wi_ref, bi_ref,
                lam_ref, g_ref, out_ref, conv_ref, hlast_ref, tail_ref, carry_ref):
    nb, tt, w = xl_ref.shape
    j = pl.program_id(1)

    @pl.when(j == 0)
    def _():
        tail_ref[:, 8 - (CONV_W - 1):, :] = conv0_ref[...]
        carry_ref[...] = h0_ref[...]

    xl = xl_ref[...]
    xpad = jnp.concatenate([tail_ref[...], xl], axis=1)
    cw = cw_ref[...]
    xc = jnp.zeros_like(xl) + cb_ref[...]
    for k in range(CONV_W):
        off = 8 - (CONV_W - 1) + k
        xc = xc + xpad[:, off:off + tt, :] * cw[k:k + 1, :]
    conv_ref[...] = xpad[:, tt + 8 - (CONV_W - 1):, :]
    tail_ref[...] = xpad[:, tt:, :]

    m = nb * tt
    xf = xc.reshape(m, w)
    xb = xf.astype(BF16)
    r = _sigmoid(_dot(xb, wr_ref[...]) + br_ref[...])
    gi = _sigmoid(_dot(xb, wi_ref[...]) + bi_ref[...])
    lam = lam_ref[...]
    softplus = jnp.maximum(-lam, 0.0) + jnp.log1p(jnp.exp(-jnp.abs(lam)))
    log_a = (-LRU_C) * r * softplus
    a = jnp.exp(log_a)
    b = jnp.sqrt(-_expm1_neg(2.0 * log_a)) * (gi * xf)

    pos = lax.broadcasted_iota(jnp.int32, (m, w), 0) % tt
    d = 1
    while d < tt:
        keep = pos >= d
        a_prev = jnp.where(keep, pltpu.roll(a, d, 0), 1.0)
        b_prev = jnp.where(keep, pltpu.roll(b, d, 0), 0.0)
        b = a * b_prev + b
        a = a * a_prev
        d *= 2
    h = a.reshape(nb, tt, w) * carry_ref[...] + b.reshape(nb, tt, w)
    h_last = h[:, tt - 1:tt, :]
    carry_ref[...] = h_last
    hlast_ref[...] = h_last
    out_ref[...] = _rms(h * gy_ref[...], g_ref[...]).astype(BF16)


def _lru(xl, gy, conv0, h0, conv_w, conv_b, wr_bd, b_r, wi_bd, b_i, lam, g_lru):
    n, t, w = xl.shape
    nb, tt = _seq_blocks(n, t)
    blk = pl.BlockSpec((nb, tt, w), lambda i, j: (i, j, 0))
    per_seq = lambda rows: pl.BlockSpec((nb, rows, w), lambda i, j: (i, 0, 0))
    const = lambda shape: pl.BlockSpec(shape, lambda i, j: (0,) * len(shape))
    row = const((1, w))
    return pl.pallas_call(
        _lru_kernel,
        grid=(n // nb, t // tt),
        in_specs=[blk, blk, per_seq(CONV_W - 1), per_seq(1),
                  const((CONV_W, w)), row, const((w, w)), row, const((w, w)), row, row, row],
        out_specs=[blk, per_seq(CONV_W - 1), per_seq(1)],
        out_shape=[jax.ShapeDtypeStruct((n, t, w), BF16),
                   jax.ShapeDtypeStruct((n, CONV_W - 1, w), F32),
                   jax.ShapeDtypeStruct((n, 1, w), F32)],
        scratch_shapes=[pltpu.VMEM((nb, 8, w), F32), pltpu.VMEM((nb, 1, w), F32)],
        compiler_params=_params("parallel", "arbitrary"),
        name="rglru",
    )(xl, gy, conv0, h0, conv_w, conv_b, wr_bd, b_r, wi_bd, b_i, lam, g_lru)


def _cumsum_rows_kernel(x_ref, upper_ref, o_ref, carry_ref):
    tb = x_ref.shape[1]

    @pl.when(pl.program_id(0) == 0)
    def _():
        carry_ref[...] = jnp.zeros_like(carry_ref)

    upper = upper_ref[...]
    hi, mid, lo = _split3(x_ref[...])
    d = _dot(hi, upper) + _dot(mid, upper) + _dot(lo, upper) + carry_ref[...]
    carry_ref[...] = d[:, tb - 1:tb]
    o_ref[...] = d


def _cumsum_rows(x, tb):
    rows, t = x.shape
    upper = jnp.asarray(np.triu(np.ones((tb, tb), np.float32)), BF16)
    return pl.pallas_call(
        _cumsum_rows_kernel,
        grid=(t // tb,),
        in_specs=[pl.BlockSpec((rows, tb), lambda j: (0, j)), pl.BlockSpec((tb, tb), lambda j: (0, 0))],
        out_specs=pl.BlockSpec((rows, tb), lambda j: (0, j)),
        out_shape=jax.ShapeDtypeStruct((rows, t), F32),
        scratch_shapes=[pltpu.VMEM((rows, 1), F32)],
        compiler_params=_params("arbitrary"),
        name="logf_cumsum",
    )(x, upper)


def _online_update(s, m_prev, l_prev):
    m_new = jnp.maximum(m_prev, jnp.max(s, axis=1, keepdims=True))
    alpha = jnp.exp(m_prev - m_new)
    p = jnp.exp(s - m_new)
    l_new = alpha * l_prev + jnp.sum(p, axis=1, keepdims=True)
    return p, alpha, m_new, l_new


def _att_prompt_kernel(bpre_ref, qt_ref, ka_ref, vt_ref, o_ref, s_ref, p_ref, acc_ref):
    tq = qt_ref.shape[2]
    tk = ATT_TK
    nblk = ka_ref.shape[1] // tk
    b, hp, i = pl.program_id(0), pl.program_id(1), pl.program_id(2)
    q0 = i * tq
    jd = q0 // tk
    kpos = lax.broadcasted_iota(jnp.int32, (tk, tq), 0)
    qpos = lax.broadcasted_iota(jnp.int32, (tk, tq), 1)
    rows = lax.broadcasted_iota(jnp.int32, (LANES, tq), 0)
    base = [((b * N_PAIRS + hp) * 2 + hh) * nblk for hh in range(2)]

    def scores(j, masked):
        start = pl.multiple_of(j * tk, tk)
        col_max = []
        for hh in range(2):
            s = _dot(ka_ref[hh, pl.ds(start, tk), :], qt_ref[hh])
            if masked:
                s = jnp.where(kpos + start <= qpos + q0, s, NEG_INF)
            s_ref[hh] = s
            col_max.append(jnp.max(s, axis=0, keepdims=True))
        return tuple(col_max)

    def softmax_pv(j, col_max, m):
        start = pl.multiple_of(j * tk, tk)
        m_out = []
        for hh in range(2):
            c = (bpre_ref[base[hh] + jd] - bpre_ref[base[hh] + j]) * LOG2E
            m_new = jnp.maximum(m[hh], col_max[hh] + c)
            alpha = jnp.exp2(m[hh] - m_new)
            shift = m_new - c
            for ch in range(tk // ATT_CHUNK):
                sl = slice(ch * ATT_CHUNK, (ch + 1) * ATT_CHUNK)
                p_ref[hh, sl, :] = jnp.exp2(s_ref[hh, sl, :] - shift).astype(BF16)
            m_out.append((m_new, alpha))
        return tuple(m_out), start

    def accumulate(m_alpha, start):
        for hh in range(2):
            pv = _dot(vt_ref[hh, :, pl.ds(start, tk)], p_ref[hh])
            acc_ref[hh] = m_alpha[hh][1] * acc_ref[hh] + pv
        return tuple(ma[0] for ma in m_alpha)

    def step(j, next_masked, carry):
        col_max, m = carry
        m_alpha, start = softmax_pv(j, col_max, m)
        col_max_next = scores(j + 1, next_masked)
        return col_max_next, accumulate(m_alpha, start)

    acc_ref[...] = jnp.zeros_like(acc_ref)
    neg = jnp.full((1, tq), NEG_INF, F32)
    n_masked = max(1, tq // tk)
    carry = (scores(0, True), (neg, neg))
    carry = lax.fori_loop(0, jd - 1, lambda j, cr: step(j, False, cr), carry)
    carry = lax.cond(jd > 0, lambda cr: step(jd - 1, True, cr), lambda cr: cr, carry)
    for extra in range(n_masked - 1):
        carry = step(jd + extra, True, carry)
    col_max, m = carry
    m_alpha, start = softmax_pv(jd + n_masked - 1, col_max, m)
    accumulate(m_alpha, start)

    acc_a, acc_b = acc_ref[0], acc_ref[1]
    out_a = acc_a / acc_a[_aug_lane(0):_aug_lane(0) + 1, :]
    out_b = acc_b / acc_b[_aug_lane(1):_aug_lane(1) + 1, :]
    o_ref[0] = jnp.where(rows < HEAD_DIM, out_a, out_b).T


def _att_prompt(qt, ka, vt, bpre):
    n, _, _, t = qt.shape
    assert t % ATT_TQ == 0 and (ATT_TK % ATT_TQ == 0 or ATT_TQ % ATT_TK == 0) and ATT_TK == ROW_TILE
    pair = lambda q: (N_PAIRS, 2) + q.shape[2:]
    qt, ka, vt = (a.reshape((n,) + pair(a)) for a in (qt, ka, vt))
    bflat = jnp.transpose(bpre[:, :, 0, :N_HEADS], (0, 2, 1)).reshape(-1)
    return pl.pallas_call(
        _att_prompt_kernel,
        grid=(n, N_PAIRS, t // ATT_TQ),
        in_specs=[pl.BlockSpec(memory_space=pltpu.SMEM),
                  pl.BlockSpec((None, None, 2, LANES, ATT_TQ), lambda b, h, i: (b, h, 0, 0, i)),
                  pl.BlockSpec((None, None, 2, t, LANES), lambda b, h, i: (b, h, 0, 0, 0),
                               pipeline_mode=pl.Buffered(1)),
                  pl.BlockSpec((None, None, 2, LANES, t), lambda b, h, i: (b, h, 0, 0, 0),
                               pipeline_mode=pl.Buffered(1))],
        out_specs=pl.BlockSpec((1, ATT_TQ, PAIR_W), lambda b, h, i: (b, i, h)),
        out_shape=jax.ShapeDtypeStruct((n, t, WA), F32),
        scratch_shapes=[pltpu.VMEM((2, ATT_TK, ATT_TQ), F32), pltpu.VMEM((2, ATT_TK, ATT_TQ), BF16),
                        pltpu.VMEM((2, LANES, ATT_TQ), F32)],
        compiler_params=_params("parallel", "parallel", "arbitrary"),
        name="att_prompt",
    )(bflat, qt, ka, vt)


def _att_sample_kernel(q_ref, kp_ref, vp_ref, dp_ref, dend_ref, kn_ref, vn_ref, dn_ref, o_ref,
                       m_ref, l_ref, acc_ref):
    t = q_ref.shape[1]
    j = pl.program_id(1)

    @pl.when(j == 0)
    def _():
        m_ref[...] = jnp.full_like(m_ref, NEG_INF)
        l_ref[...] = jnp.zeros_like(l_ref)
        acc_ref[...] = jnp.zeros_like(acc_ref)

    def head(h, k, v, time_minor, bias, mask):
        sl = slice(h * HEAD_DIM, (h + 1) * HEAD_DIM)
        q_h = q_ref[0, :, sl]
        s = (_dot(q_h, k) if time_minor else _dot_nt(q_h, k)) + bias[h:h + 1]
        if mask is not None:
            s = jnp.where(mask, s, NEG_INF)
        p, alpha, m_new, l_new = _online_update(s, m_ref[h], l_ref[h])
        p = p.astype(BF16)
        acc = alpha * acc_ref[:, sl] + (_dot_nt(p, v) if time_minor else _dot(p, v))
        m_ref[h], l_ref[h], acc_ref[:, sl] = m_new, l_new, acc
        return acc / l_new

    bias_past = dend_ref[0] - dp_ref[0]
    for h in range(N_HEADS):
        head(h, kp_ref[0, h].astype(BF16), vp_ref[0, h].astype(BF16), True, bias_past, None)

    @pl.when(j == pl.num_programs(1) - 1)
    def _():
        bias_new = -dn_ref[0]
        qpos = lax.broadcasted_iota(jnp.int32, (t, t), 0)
        kpos = lax.broadcasted_iota(jnp.int32, (t, t), 1)
        for h in range(N_HEADS):
            sl = slice(h * HEAD_DIM, (h + 1) * HEAD_DIM)
            o_ref[0, :, sl] = head(h, kn_ref[0, :, sl], vn_ref[0, :, sl], False, bias_new, kpos <= qpos)


def _att_sample(qb, k_past, v_past, d_past, kb_new, vb_new, d_new):
    n, t, _ = qb.shape
    past = k_past.shape[1]
    tk = min(SAMPLE_TK, past)
    assert past % tk == 0
    d_end = d_past[:, :, past - 1:]
    k_past = jnp.transpose(k_past, (0, 2, 3, 1))
    v_past = jnp.transpose(v_past, (0, 2, 3, 1))
    new = lambda dt: pl.BlockSpec((1, t, WA), lambda b, j: (b, 0, 0))
    cache = pl.BlockSpec((1, N_HEADS, HEAD_DIM, tk), lambda b, j: (b, 0, 0, j))
    return pl.pallas_call(
        _att_sample_kernel,
        grid=(n, past // tk),
        in_specs=[new(BF16), cache, cache,
                  pl.BlockSpec((1, N_HEADS, tk), lambda b, j: (b, 0, j)),
                  pl.BlockSpec((1, N_HEADS, 1), lambda b, j: (b, 0, 0)),
                  new(BF16), new(BF16),
                  pl.BlockSpec((1, N_HEADS, t), lambda b, j: (b, 0, 0))],
        out_specs=pl.BlockSpec((1, t, WA), lambda b, j: (b, 0, 0)),
        out_shape=jax.ShapeDtypeStruct((n, t, WA), F32),
        scratch_shapes=[pltpu.VMEM((N_HEADS, t, 1), F32), pltpu.VMEM((N_HEADS, t, 1), F32),
                        pltpu.VMEM((t, WA), F32)],
        compiler_params=_params("parallel", "arbitrary"),
        name="att_sample",
    )(qb, k_past, v_past, d_past, d_end, kb_new, vb_new, d_new)


def _first_index_of_max(x, axis):
    mx = jnp.max(x, axis=axis, keepdims=True)
    idx = lax.broadcasted_iota(jnp.int32, x.shape, axis)
    first = jnp.min(jnp.where(x == mx, idx, x.shape[axis]), axis=axis, keepdims=True)
    return mx, idx == first


def _route(s, bias):
    m = s.shape[1]
    sb = (s + bias).reshape(N_GROUPS, GROUP_SIZE, m)
    top1, is_top1 = _first_index_of_max(sb, 1)
    top2 = jnp.max(jnp.where(is_top1, NEG_INF, sb), axis=1, keepdims=True)
    grp = (top1 + top2).reshape(N_GROUPS, m)
    gi = lax.broadcasted_iota(jnp.int32, (N_GROUPS, N_GROUPS, m), 0)
    gj = lax.broadcasted_iota(jnp.int32, (N_GROUPS, N_GROUPS, m), 1)
    other, mine = grp[None, :, :], grp[:, None, :]
    beats = (other > mine) | ((other == mine) & (gj < gi))
    g_rank = jnp.sum(beats.astype(jnp.int32), axis=1)
    g_keep = (g_rank < TOPK_GROUPS)[:, None, :]
    cand = jnp.where(g_keep, sb, NEG_INF).reshape(N_EXPERTS, m)
    picks = []
    for _ in range(TOP_K):
        _, pick = _first_index_of_max(cand, 0)
        picks.append(pick)
        cand = jnp.where(pick, NEG_INF, cand)
    w = jnp.concatenate([jnp.sum(jnp.where(pk, s, 0.0), axis=0, keepdims=True) for pk in picks], axis=0)
    return picks, w / jnp.sum(w, axis=0, keepdims=True) * ROUTED_SCALE


HALF_MASK = 0xFFFF0000


def _pack_halves(x):
    c = x.shape[1] // 2
    lo = lax.bitcast_convert_type(x[:, :c].astype(BF16).astype(F32), jnp.uint32) >> jnp.uint32(16)
    hi = lax.bitcast_convert_type(x[:, c:].astype(BF16).astype(F32), jnp.uint32) & jnp.uint32(HALF_MASK)
    return lax.bitcast_convert_type(lo | hi, jnp.int32)


def _unpack_halves(words):
    w = lax.bitcast_convert_type(words, jnp.uint32)
    lo = lax.bitcast_convert_type(w << jnp.uint32(16), F32)
    hi = lax.bitcast_convert_type(w & jnp.uint32(HALF_MASK), F32)
    return lo, hi


def _outproj_kernel(x_ref, lru_ref, att_ref, mod_ref, gatt_ref, wtop_ref, wbot_ref, gpost_ref, gpre_ref,
                    rwh_ref, rwl_ref, rb_ref, cnt_in_ref,
                    x1_ref, hf_ref, xw_ref, ids_ref, ranks_ref, gates_ref, cnt_out_ref, carry_ref):
    nb, tt, d = x_ref.shape
    m = nb * tt

    @pl.when((pl.program_id(0) == 0) & (pl.program_id(1) == 0))
    def _():
        carry_ref[...] = cnt_in_ref[...]

    mod = mod_ref[...]
    att_n = _rms(att_ref[...], gatt_ref[...]).reshape(m, WA).astype(BF16)
    mix = _dot(lru_ref[...].reshape(m, WL), wtop_ref[...]) + _dot(att_n, wbot_ref[...])
    x1 = x_ref[...] + mod[:, 2:3, :] * _rms(mix, gpost_ref[...]).reshape(nb, tt, d)
    x1_ref[...] = x1
    hf = (_rms(x1, gpre_ref[...]) * (1.0 + mod[:, 4:5, :]) + mod[:, 3:4, :]).reshape(m, d)
    hf_hi = hf.astype(BF16)
    hf_ref[...] = hf_hi.reshape(nb, tt, d)
    hf_lo = (hf - hf_hi.astype(F32)).astype(BF16)
    rwh = rwh_ref[...]
    logits = _dot_nt(rwh, hf_hi) + _dot_nt(rwh, hf_lo) + _dot_nt(rwl_ref[...], hf_hi)
    picks, gates = _route(_sigmoid(logits), rb_ref[...])
    xw_ref[...] = _pack_halves(hf).reshape(nb, tt, d // 2)

    sel = jnp.zeros((N_EXPERTS, m), F32)
    for pk in picks:
        sel = sel + pk.astype(F32)
    before = (lax.broadcasted_iota(jnp.int32, (m, m), 0) < lax.broadcasted_iota(jnp.int32, (m, m), 1)).astype(BF16)
    prior = _dot(sel.astype(BF16), before) + carry_ref[...]
    expert = lax.broadcasted_iota(jnp.int32, (N_EXPERTS, m), 0).astype(F32)
    take = lambda pk, v: jnp.sum(jnp.where(pk, v, 0.0), axis=0, keepdims=True)
    ids_ref[...] = jnp.concatenate([take(pk, expert) for pk in picks], axis=0).astype(jnp.int32)
    ranks_ref[...] = jnp.concatenate([take(pk, prior) for pk in picks], axis=0).astype(jnp.int32)
    carry_ref[...] += jnp.sum(sel, axis=1, keepdims=True)
    cnt_out_ref[...] = carry_ref[...]
    gates = jnp.concatenate([gates, jnp.zeros((LANES - TOP_K, m), F32)], axis=0)
    gates_ref[...] = gates.T.reshape(nb, tt, LANES)


def _outproj(x, lru_n, att, mod, g_att, w_top, w_bot, g_post, g_pre, rw_hi, rw_lo, r_bias, cnt_in):
    n, t, d = x.shape
    nb, tt = _seq_blocks(n, t)
    m = nb * tt
    steps_t = t // tt
    blk = lambda w: pl.BlockSpec((nb, tt, w), lambda i, j: (i, j, 0))
    const = lambda shape: pl.BlockSpec(shape, lambda i, j: (0,) * len(shape))
    per_tok = pl.BlockSpec((TOP_K, m), lambda i, j: (0, i * steps_t + j))
    return pl.pallas_call(
        _outproj_kernel,
        grid=(n // nb, steps_t),
        in_specs=[blk(d), blk(WL), blk(WA), pl.BlockSpec((nb, 6, d), lambda i, j: (i, 0, 0)),
                  const((1, WA)), const((WL, d)), const((WA, d)), const((1, d)), const((1, d)),
                  const((N_EXPERTS, d)), const((N_EXPERTS, d)), const((N_EXPERTS, 1)), const((N_EXPERTS, 1))],
        out_specs=[blk(d), blk(d), blk(d // 2), per_tok, per_tok, blk(LANES), const((N_EXPERTS, 1))],
        out_shape=[jax.ShapeDtypeStruct((n, t, d), F32), jax.ShapeDtypeStruct((n, t, d), BF16),
                   jax.ShapeDtypeStruct((n, t, d // 2), jnp.int32),
                   jax.ShapeDtypeStruct((TOP_K, n * t), jnp.int32), jax.ShapeDtypeStruct((TOP_K, n * t), jnp.int32),
                   jax.ShapeDtypeStruct((n, t, LANES), F32), jax.ShapeDtypeStruct((N_EXPERTS, 1), F32)],
        scratch_shapes=[pltpu.VMEM((N_EXPERTS, 1), F32)],
        compiler_params=_params("arbitrary", "arbitrary"),
        name="outproj_router",
    )(x, lru_n, att, mod, g_att, w_top, w_bot, g_post, g_pre, rw_hi, rw_lo, r_bias, cnt_in)


def _subcore_ranges(n_items):
    info = plsc.get_sparse_core_info()
    n_workers = info.num_cores * info.num_subcores
    per_worker = n_items // n_workers
    assert per_worker * n_workers == n_items and per_worker % GATHER_ROWS == 0
    return info, plsc.VectorSubcoreMesh(core_axis_name="c", subcore_axis_name="s"), per_worker


def _scatter_rows(xw, pos, n_slots):
    m_tot, c = xw.shape
    info, mesh, per_worker = _subcore_ranges(m_tot)
    pos_flat = pos.reshape(-1)

    @functools.partial(
        pl.kernel, mesh=mesh, out_type=jax.ShapeDtypeStruct((n_slots, c), jnp.int32),
        scratch_types=[pltpu.VMEM((GATHER_ROWS,), jnp.int32), pltpu.VMEM((GATHER_ROWS, c), jnp.int32),
                       pltpu.SemaphoreType.DMA])
    def scatter(xw_hbm, pos_hbm, out_hbm, idx_v, rows_v, sem):
        worker = lax.axis_index("s") * info.num_cores + lax.axis_index("c")
        base = worker * per_worker

        @pl.loop(0, per_worker // GATHER_ROWS)
        def _(step):
            off = pl.multiple_of(base + step * GATHER_ROWS, GATHER_ROWS)
            pltpu.sync_copy(xw_hbm.at[pl.ds(off, GATHER_ROWS)], rows_v)
            for r in range(TOP_K):
                pltpu.sync_copy(pos_hbm.at[pl.ds(pl.multiple_of(r * m_tot + off, GATHER_ROWS), GATHER_ROWS)], idx_v)
                pltpu.async_copy(rows_v, out_hbm.at[idx_v], sem).wait()

    return scatter(xw, pos_flat)


def _swiglu_halves(lo, hi, wg, wu, wd):
    c = lo.shape[1]
    hg = _dot(lo, wg[:c]) + _dot(hi, wg[c:])
    hu = _dot(lo, wu[:c]) + _dot(hi, wu[c:])
    return _dot((_silu(hg) * hu).astype(BF16), wd)


def _expert_kernel(te_ref, valid_ref, x_ref, wg_ref, wu_ref, wd_ref, y_ref):
    i = pl.program_id(0)
    valid = valid_ref[i]

    @pl.when(valid > 0)
    def _():
        w = x_ref[...]
        row = lax.broadcasted_iota(jnp.int32, w.shape, 0)
        lo, hi = _unpack_halves(jnp.where(row < valid, w, 0))
        y_ref[...] = _pack_halves(_swiglu_halves(lo.astype(BF16), hi.astype(BF16), wg_ref[0], wu_ref[0], wd_ref[0]))


def _experts(xs, tile_expert, tile_valid, wg, wu, wd):
    n_slots, c = xs.shape
    n_tiles = n_slots // EXPERT_TILE
    d = 2 * c
    rows = pl.BlockSpec((EXPERT_TILE, c), lambda i, te, tv: (i, 0))
    weight = lambda shape: pl.BlockSpec((1,) + shape, lambda i, te, tv: (te[i], 0, 0))
    return pl.pallas_call(
        _expert_kernel,
        grid_spec=pltpu.PrefetchScalarGridSpec(
            num_scalar_prefetch=2, grid=(n_tiles,),
            in_specs=[rows, weight((d, D_EXPERT)), weight((d, D_EXPERT)), weight((D_EXPERT, d))],
            out_specs=rows),
        out_shape=jax.ShapeDtypeStruct((n_slots, c), jnp.int32),
        compiler_params=_params("arbitrary"),
        name="moe_experts",
    )(tile_expert, tile_valid, xs, wg, wu, wd)


def _gather_rows(table, idx):
    b, c = idx.shape[0], table.shape[1]
    info, mesh, per_worker = _subcore_ranges(b)

    n_steps = per_worker // GATHER_ROWS
    assert n_steps % 2 == 0

    @functools.partial(
        pl.kernel, mesh=mesh, out_type=jax.ShapeDtypeStruct((b, c), jnp.int32),
        scratch_types=[pltpu.VMEM((2, GATHER_ROWS), jnp.int32), pltpu.VMEM((2, GATHER_ROWS, c), jnp.int32),
                       pltpu.SemaphoreType.DMA((2,))])
    def gather(table_hbm, idx_hbm, out_hbm, idx_v, rows_v, sems):
        worker = lax.axis_index("s") * info.num_cores + lax.axis_index("c")
        base = worker * per_worker

        def chunk(step):
            return pl.ds(pl.multiple_of(base + step * GATHER_ROWS, GATHER_ROWS), GATHER_ROWS)

        def stream(buf):
            return pltpu.make_async_copy(table_hbm.at[idx_v.at[buf]], rows_v.at[buf], sems.at[buf])

        def start(step, buf):
            pltpu.sync_copy(idx_hbm.at[chunk(step)], idx_v.at[buf])
            stream(buf).start()

        def finish(step, buf):
            stream(buf).wait()
            pltpu.sync_copy(rows_v.at[buf], out_hbm.at[chunk(step)])

        start(0, 0)

        @pl.loop(0, n_steps, step=2)
        def _(step):
            start(step + 1, 1)
            finish(step, 0)

            @pl.when(step + 2 < n_steps)
            def _():
                start(step + 2, 0)

            finish(step + 1, 1)

    return gather(table, idx)


def _combine_kernel(rows_ref, gates_ref, hf_ref, x1_ref, mod_ref, sg_ref, su_ref, sd_ref, gpost_ref, y_ref):
    nb, tt, d = hf_ref.shape
    m = nb * tt
    c = d // 2
    x = hf_ref[...].reshape(m, d)
    shared = _swiglu_halves(x[:, :c], x[:, c:], sg_ref[...], su_ref[...], sd_ref[...])
    gates = gates_ref[...].reshape(m, LANES)
    acc_lo = shared[:, :c]
    acc_hi = shared[:, c:]
    for r in range(TOP_K):
        lo, hi = _unpack_halves(rows_ref[r])
        g = gates[:, r:r + 1]
        acc_lo = acc_lo + g * lo
        acc_hi = acc_hi + g * hi
    z = _rms(jnp.concatenate([acc_lo, acc_hi], axis=1), gpost_ref[...]).reshape(nb, tt, d)
    y_ref[...] = x1_ref[...] + mod_ref[...][:, 5:6, :] * z


def _combine(rows, first_tile, gates_t, hf, x1, mod, sg, su, sd, g_post):
    n, t, d = hf.shape
    nb, tt = _seq_blocks(n, t)
    m = nb * tt
    steps_t = t // tt
    blk = lambda w: pl.BlockSpec((nb, tt, w), lambda i, j: (i, j, 0))
    const = lambda shape: pl.BlockSpec(shape, lambda i, j: (0,) * len(shape))
    return pl.pallas_call(
        _combine_kernel,
        grid=(n // nb, steps_t),
        in_specs=[pl.BlockSpec((TOP_K, m, d // 2), lambda i, j: (0, first_tile + i * steps_t + j, 0)),
                  blk(LANES), blk(d), blk(d), pl.BlockSpec((nb, 6, d), lambda i, j: (i, 0, 0)),
                  const((d, D_EXPERT)), const((d, D_EXPERT)), const((D_EXPERT, d)), const((1, d))],
        out_specs=blk(d),
        out_shape=jax.ShapeDtypeStruct((n, t, d), F32),
        compiler_params=_params("parallel", "parallel"),
        name="moe_combine",
    )(rows, gates_t, hf, x1, mod, sg, su, sd, g_post)


def _slots_kernel(starts_ref, ids_ref, ranks_ref, pos_ref):
    ids = ids_ref[...]

    def add_start(e, pos):
        return pos + jnp.where(ids == e, starts_ref[e], 0)

    pos_ref[...] = lax.fori_loop(0, N_EXPERTS, add_start, ranks_ref[...])


def _slots(starts, ids, ranks):
    k, m_tot = ids.shape
    assert m_tot % SLOT_COLS == 0
    blk = pl.BlockSpec((k, SLOT_COLS), lambda i: (0, i))
    return pl.pallas_call(
        _slots_kernel,
        grid=(m_tot // SLOT_COLS,),
        in_specs=[pl.BlockSpec(memory_space=pltpu.SMEM), blk, blk],
        out_specs=blk,
        out_shape=jax.ShapeDtypeStruct((k, m_tot), jnp.int32),
        compiler_params=_params("arbitrary"),
        name="moe_slots",
    )(starts, ids, ranks)


def _slot_plan(ids, ranks, counts):
    n_pairs = ids.shape[0] * ids.shape[1]
    n_tiles = -(-(n_pairs + N_EXPERTS * (EXPERT_TILE - 1)) // EXPERT_TILE)
    cnt = counts.reshape(N_EXPERTS).astype(jnp.int32)
    padded = (cnt + EXPERT_TILE - 1) // EXPERT_TILE * EXPERT_TILE
    ends = jnp.cumsum(padded)
    starts = ends - padded
    pos = _slots(starts, ids, ranks)
    tile_start = jnp.arange(n_tiles, dtype=jnp.int32) * EXPERT_TILE
    in_expert = (tile_start[:, None] >= starts[None, :]) & (tile_start[:, None] < ends[None, :])
    tile_expert = jnp.sum(jnp.where(in_expert, jnp.arange(N_EXPERTS, dtype=jnp.int32)[None, :], 0), axis=1)
    tile_fill = jnp.sum(jnp.where(in_expert, (starts + cnt)[None, :] - tile_start[:, None], 0), axis=1)
    tile_valid = jnp.clip(tile_fill, 0, EXPERT_TILE).astype(jnp.int32)
    return pos, tile_expert.astype(jnp.int32), tile_valid, n_tiles * EXPERT_TILE


def _block_diag(w):
    g, bw, _ = w.shape
    eye = jnp.eye(g, dtype=w.dtype)
    return (eye[:, None, :, None] * w[:, :, None, :]).reshape(g * bw, g * bw)


def _prep_weights(p):
    d_main = 2 * WL + 3 * WA
    w_in = p["w_in"]
    rw_t = p["router_w"].T
    rw_hi = rw_t.astype(BF16)
    row = lambda v: v.reshape(1, -1)
    return dict(
        w_mod=p["w_mod"], b_mod=p["b_mod"],
        g_pre_mix=row(p["g_pre_mix"]), g_post_mix=row(p["g_post_mix"]),
        g_pre_ffn=row(p["g_pre_ffn"]), g_post_ffn=row(p["g_post_ffn"]),
        w_main=w_in[:, :d_main].astype(BF16),
        w_f=jnp.pad(w_in[:, d_main:], ((0, 0), (0, LANES - N_HEADS))).astype(BF16),
        b_f=jnp.pad(p["b_f"], (0, LANES - N_HEADS)).reshape(1, LANES),
        conv_w=p["conv_w"], conv_b=row(p["conv_b"]),
        wr_bd=_block_diag(p["w_r"]).astype(BF16), b_r=row(p["b_r"]),
        wi_bd=_block_diag(p["w_i"]).astype(BF16), b_i=row(p["b_i"]),
        lam=row(p["lru_lambda"]), g_lru=row(p["g_lru_out"]), g_att=row(p["g_att_out"]),
        w_top=p["w_out"][:WL].astype(BF16), w_bot=p["w_out"][WL:].astype(BF16),
        rw_hi=rw_hi, rw_lo=(rw_t - rw_hi.astype(F32)).astype(BF16),
        r_bias=p["router_bias"].reshape(N_EXPERTS, 1),
        wg=p["w_gate"].astype(BF16), wu=p["w_up"].astype(BF16), wd=p["w_down"].astype(BF16),
        sg=p["ws_gate"].astype(BF16), su=p["ws_up"].astype(BF16), sd=p["ws_down"].astype(BF16),
    )


def _mixers(x, mod, conv0, h0, past, w, cnt_in):
    n, t, _ = x.shape
    proj_args = (x, mod, w["g_pre_mix"], w["w_main"], w["w_f"], w["b_f"])
    if past is None:
        xl, gy, k, v, lf, qt, ka, vt, bpre = _inproj_prompt(*proj_args)
        att = _att_prompt(qt, ka, vt, bpre)
    else:
        xl, gy, qb, kb, vb, k, v, lf = _inproj(*proj_args)
        k_past, v_past, lf_past = past
        plen = k_past.shape[1]
        by_head = lambda a: jnp.transpose(a, (0, 2, 1)).reshape(n * N_HEADS, a.shape[1])
        d_new = _cumsum_rows(by_head(lf), t).reshape(n, N_HEADS, t)
        d_past = _cumsum_rows(by_head(lf_past), min(CUMSUM_COLS, plen)).reshape(n, N_HEADS, plen)
        att = _att_sample(qb, k_past, v_past, d_past, kb, vb, d_new)
    lru_n, conv_new, h_new = _lru(xl, gy, conv0, h0.reshape(n, 1, WL), w["conv_w"], w["conv_b"],
                                  w["wr_bd"], w["b_r"], w["wi_bd"], w["b_i"], w["lam"], w["g_lru"])
    routed = _outproj(x, lru_n, att, mod, w["g_att"], w["w_top"], w["w_bot"], w["g_post_mix"],
                      w["g_pre_ffn"], w["rw_hi"], w["rw_lo"], w["r_bias"], cnt_in)
    state = (k.reshape(n, t, N_HEADS, HEAD_DIM), v.reshape(n, t, N_HEADS, HEAD_DIM), lf,
             conv_new, h_new.reshape(n, WL))
    return routed, state


def _layer(xp, xs, mod_p, mod_s, conv_s, h_s, past_s, w):
    n_p = xp.shape[0]
    conv0 = jnp.zeros((n_p, CONV_W - 1, WL), F32)
    h0 = jnp.zeros((n_p, WL), F32)
    zero_cnt = jnp.zeros((N_EXPERTS, 1), F32)
    (x1_p, hf_p, xw_p, ids_p, rk_p, g_p, cnt_p), st_p = _mixers(xp, mod_p, conv0, h0, None, w, zero_cnt)
    (x1_s, hf_s, xw_s, ids_s, rk_s, g_s, cnt), st_s = _mixers(xs, mod_s, conv_s, h_s, past_s, w, cnt_p)

    half = xw_p.shape[-1]
    xw = jnp.concatenate([xw_p.reshape(-1, half), xw_s.reshape(-1, half)], axis=0)
    ids = jnp.concatenate([ids_p, ids_s], axis=1)
    ranks = jnp.concatenate([rk_p, rk_s], axis=1)
    pos, tile_expert, tile_valid, n_slots = _slot_plan(ids, ranks, cnt)
    ys = _experts(_scatter_rows(xw, pos, n_slots), tile_expert, tile_valid, w["wg"], w["wu"], w["wd"])
    m_p, m_tot = ids_p.shape[1], ids.shape[1]
    rows = _gather_rows(ys, pos.reshape(-1)).reshape(TOP_K, m_tot, half)
    shared = (w["sg"], w["su"], w["sd"], w["g_post_ffn"])
    yp = _combine(rows, 0, g_p, hf_p, x1_p, mod_p, *shared)
    ysmp = _combine(rows, m_p // ROW_TILE, g_s, hf_s, x1_s, mod_s, *shared)
    return yp, ysmp, st_p, st_s


def kernel(x_prompt, x_sample, c_prompt, c_sample, cache_k, cache_v, cache_logf, state_conv, state_lru, w_mod, b_mod, g_pre_mix, g_post_mix, g_pre_ffn, g_post_ffn, w_in, conv_w, conv_b, w_r, b_r, w_i, b_i, lru_lambda, b_f, g_lru_out, g_att_out, w_out, router_w, router_bias, w_gate, w_up, w_down, ws_gate, ws_up, ws_down):
    names = ("w_mod", "b_mod", "g_pre_mix", "g_post_mix", "g_pre_ffn", "g_post_ffn", "w_in", "conv_w", "conv_b",
             "w_r", "b_r", "w_i", "b_i", "lru_lambda", "b_f", "g_lru_out", "g_att_out", "w_out", "router_w",
             "router_bias", "w_gate", "w_up", "w_down", "ws_gate", "ws_up", "ws_down")
    stacked = (w_mod, b_mod, g_pre_mix, g_post_mix, g_pre_ffn, g_post_ffn, w_in, conv_w, conv_b, w_r, b_r, w_i, b_i,
               lru_lambda, b_f, g_lru_out, g_att_out, w_out, router_w, router_bias, w_gate, w_up, w_down,
               ws_gate, ws_up, ws_down)
    depth = w_mod.shape[0]
    n_p, n_s = x_prompt.shape[0], x_sample.shape[0]
    yp, ys = x_prompt, x_sample
    st_p, st_s = [], []
    for l in range(depth):
        w = _prep_weights({k: v[l] for k, v in zip(names, stacked)})
        mod = _modulation(jnp.concatenate([c_prompt, c_sample], axis=0), w["w_mod"], w["b_mod"])
        mod = mod.reshape(n_p + n_s, 6, D_MODEL)
        yp, ys, sp, ss = _layer(yp, ys, mod[:n_p], mod[n_p:], state_conv[l], state_lru[l],
                                (cache_k[l], cache_v[l], cache_logf[l]), w)
        st_p.append(sp)
        st_s.append(ss)
    stack = lambda sts, i: jnp.stack([s[i] for s in sts])
    return (yp, ys) + tuple(stack(st_p, i) for i in range(5)) + tuple(stack(st_s, i) for i in range(5))
```

```python
import functools

import jax
import jax.numpy as jnp
import numpy as np
from jax import lax
from jax.experimental import pallas as pl
from jax.experimental.pallas import tpu as pltpu
from jax.experimental.pallas import tpu_sc as plsc

F32 = jnp.float32
BF16 = jnp.bfloat16

D_MODEL = 1024
WL = 512
WA = 512
N_HEADS = 8
HEAD_DIM = 64
N_PAIRS = N_HEADS // 2
PAIR_W = 2 * HEAD_DIM
LANES = 128
CONV_W = 4
LRU_BLOCKS = 8
LRU_C = 8.0
N_EXPERTS = 64
N_GROUPS = 8
GROUP_SIZE = N_EXPERTS // N_GROUPS
TOPK_GROUPS = 4
TOP_K = 8
D_EXPERT = 256
ROUTED_SCALE = 2.5
EPS = 1e-6
NEG_INF = float("-inf")
LOG2E = 1.4426950408889634

ROW_TILE = 512
ATT_TQ = 512
ATT_TK = 512
ATT_CHUNK = 64
SAMPLE_TK = 4096
CUMSUM_COLS = 1024
EXPERT_TILE = 512
SLOT_COLS = 2048
GATHER_ROWS = 64
VMEM_LIMIT = 56 * 1024 * 1024


def _params(*sem):
    return pltpu.CompilerParams(dimension_semantics=sem, vmem_limit_bytes=VMEM_LIMIT)


def _dot(a, b):
    return jnp.dot(a, b, preferred_element_type=F32)


def _dot_nt(a, b):
    return lax.dot_general(a, b, (((1,), (1,)), ((), ())), preferred_element_type=F32)


def _split3(x):
    hi = x.astype(BF16)
    r1 = x - hi.astype(F32)
    mid = r1.astype(BF16)
    lo = (r1 - mid.astype(F32)).astype(BF16)
    return hi, mid, lo


def _rms(x, g):
    return x * lax.rsqrt(jnp.mean(x * x, axis=-1, keepdims=True) + EPS) * g


def _sigmoid(x):
    return 1.0 / (1.0 + jnp.exp(-x))


def _silu(x):
    return x * _sigmoid(x)


def _gelu_tanh(x):
    return 0.5 * x * (1.0 + jnp.tanh(0.7978845608028654 * (x + 0.044715 * (x * x * x))))


def _log_sigmoid(x):
    return jnp.minimum(x, 0.0) - jnp.log1p(jnp.exp(-jnp.abs(x)))


def _seq_blocks(n, t):
    if t >= ROW_TILE:
        assert t % ROW_TILE == 0
        return 1, ROW_TILE
    nb = ROW_TILE // t
    assert nb * t == ROW_TILE and n % nb == 0
    return nb, t


def _mod_kernel(c_ref, w_ref, b_ref, o_ref):
    c = _silu(c_ref[...])
    c_hi = c.astype(BF16)
    c_lo = (c - c_hi.astype(F32)).astype(BF16)
    w = w_ref[...]
    w_hi = w.astype(BF16)
    w_lo = (w - w_hi.astype(F32)).astype(BF16)
    o_ref[...] = _dot(c_hi, w_hi) + _dot(c_lo, w_hi) + _dot(c_hi, w_lo) + b_ref[...]


def _modulation(c, w_mod, b_mod):
    rows = c.shape[0]
    n = -(-rows // 8) * 8
    c = jnp.pad(c, ((0, n - rows), (0, 0)))
    d6 = w_mod.shape[1]
    return pl.pallas_call(
        _mod_kernel,
        grid=(d6 // D_MODEL,),
        in_specs=[pl.BlockSpec((n, D_MODEL), lambda j: (0, 0)),
                  pl.BlockSpec((D_MODEL, D_MODEL), lambda j: (0, j)),
                  pl.BlockSpec((1, D_MODEL), lambda j: (0, j))],
        out_specs=pl.BlockSpec((n, D_MODEL), lambda j: (0, j)),
        out_shape=jax.ShapeDtypeStruct((n, d6), F32),
        compiler_params=_params("arbitrary"),
        name="modulation",
    )(c, w_mod, b_mod.reshape(1, d6))[:rows]


def _inproj_kernel(x_ref, mod_ref, g_ref, w_ref, wf_ref, bf_ref,
                   xl_ref, gy_ref, qb_ref, kb_ref, vb_ref, k_ref, v_ref, lf_ref):
    nb, tt, d = x_ref.shape
    x = x_ref[...]
    mod = mod_ref[...]
    hn = _rms(x, g_ref[...]) * (1.0 + mod[:, 1:2, :]) + mod[:, 0:1, :]
    hb = hn.reshape(nb * tt, d).astype(BF16)

    def proj(col):
        return _dot(hb, w_ref[:, col * WL:(col + 1) * WL]).reshape(nb, tt, WL)

    xl_ref[...] = proj(0)
    gy_ref[...] = _gelu_tanh(proj(1))
    qb_ref[...] = (proj(2) * (HEAD_DIM ** -0.5)).astype(BF16)
    k = proj(3)
    k_ref[...] = k
    kb_ref[...] = k.astype(BF16)
    v = proj(4)
    v_ref[...] = v
    vb_ref[...] = v.astype(BF16)
    fl = _dot(hb, wf_ref[...]) + bf_ref[...]
    lf_ref[...] = _log_sigmoid(fl).reshape(nb, tt, LANES)[:, :, :N_HEADS]


def _inproj(x, mod, g_pre, w_main, w_f, b_f):
    n, t, d = x.shape
    nb, tt = _seq_blocks(n, t)
    blk = lambda w: pl.BlockSpec((nb, tt, w), lambda i, j: (i, j, 0))
    const = lambda shape: pl.BlockSpec(shape, lambda i, j: (0,) * len(shape))
    f32 = lambda w: jax.ShapeDtypeStruct((n, t, w), F32)
    b16 = lambda w: jax.ShapeDtypeStruct((n, t, w), BF16)
    return pl.pallas_call(
        _inproj_kernel,
        grid=(n // nb, t // tt),
        in_specs=[blk(d),
                  pl.BlockSpec((nb, 6, d), lambda i, j: (i, 0, 0)),
                  const((1, d)), const(w_main.shape), const(w_f.shape), const((1, LANES))],
        out_specs=[blk(WL), blk(WL), blk(WA), blk(WA), blk(WA), blk(WA), blk(WA), blk(N_HEADS)],
        out_shape=[f32(WL), f32(WL), b16(WA), b16(WA), b16(WA), f32(WA), f32(WA), f32(N_HEADS)],
        compiler_params=_params("parallel", "arbitrary"),
        name="inproj",
    )(x, mod, g_pre, w_main, w_f, b_f)


def _aug_lane(h):
    return HEAD_DIM if h % 2 == 0 else 0


def _inproj_prompt_kernel(x_ref, mod_ref, g_ref, w_ref, wf_ref, bf_ref, place_ref,
                          xl_ref, gy_ref, k_ref, v_ref, lf_ref, qt_ref, ka_ref, vt_ref, bpre_ref, carry_ref):
    _, tt, d = x_ref.shape

    @pl.when(pl.program_id(1) == 0)
    def _():
        carry_ref[...] = jnp.zeros_like(carry_ref)

    mod = mod_ref[0]
    hb = (_rms(x_ref[0], g_ref[...]) * (1.0 + mod[1:2, :]) + mod[0:1, :]).astype(BF16)

    def proj(col):
        return _dot(hb, w_ref[:, col * WL:(col + 1) * WL])

    xl_ref[0] = proj(0)
    gy_ref[0] = _gelu_tanh(proj(1))
    q = proj(2) * (HEAD_DIM ** -0.5 * LOG2E)
    k = proj(3)
    k_ref[0] = k
    v = proj(4)
    v_ref[0] = v
    fl = _dot(hb, wf_ref[...]) + bf_ref[...]
    lane = lax.broadcasted_iota(jnp.int32, (tt, LANES), 1)
    lf = jnp.where(lane < N_HEADS, _log_sigmoid(fl), 0.0)
    lf_ref[0] = lf[:, :N_HEADS]

    row = lax.broadcasted_iota(jnp.int32, (tt, tt), 0)
    col = lax.broadcasted_iota(jnp.int32, (tt, tt), 1)
    tril = (col <= row).astype(BF16)
    hi, mid, lo = _split3(lf)
    e = _dot(tril, hi) + _dot(tril, mid) + _dot(tril, lo)
    bpre_ref[0, 0] = carry_ref[...]
    carry_ref[...] += e[tt - 1:tt, :]
    e_hi, e_mid, e_lo = _split3(e * (-LOG2E))
    aug_k = _dot(e_hi, place_ref[0]) + _dot(e_mid, place_ref[1]) + _dot(e_lo, place_ref[2])

    for h in range(N_HEADS):
        pair = slice((h // 2) * PAIR_W, (h // 2 + 1) * PAIR_W)
        dims = (lane < HEAD_DIM) if h % 2 == 0 else (lane >= HEAD_DIM)
        a0 = _aug_lane(h)
        ones3 = ((lane >= a0) & (lane < a0 + 3)).astype(F32)
        one1 = (lane == a0).astype(F32)
        qt_ref[0, h] = (jnp.where(dims, q[:, pair], 0.0) + ones3).T.astype(BF16)
        ka_ref[0, h] = (jnp.where(dims, k[:, pair], 0.0) + aug_k[:, h * LANES:(h + 1) * LANES]).astype(BF16)
        vt_ref[0, h] = (jnp.where(dims, v[:, pair], 0.0) + one1).T.astype(BF16)


def _placement():
    pl_mat = np.zeros((3, LANES, N_HEADS * LANES), np.float32)
    for p in range(3):
        for h in range(N_HEADS):
            pl_mat[p, h, h * LANES + _aug_lane(h) + p] = 1.0
    return jnp.asarray(pl_mat, BF16)


def _inproj_prompt(x, mod, g_pre, w_main, w_f, b_f):
    n, t, d = x.shape
    tt = ROW_TILE
    assert t % tt == 0
    nblk = t // tt
    blk = lambda w: pl.BlockSpec((1, tt, w), lambda i, j: (i, j, 0))
    const = lambda shape: pl.BlockSpec(shape, lambda i, j: (0,) * len(shape))
    f32 = lambda w: jax.ShapeDtypeStruct((n, t, w), F32)
    place = _placement()
    return pl.pallas_call(
        _inproj_prompt_kernel,
        grid=(n, nblk),
        in_specs=[blk(d), pl.BlockSpec((1, 6, d), lambda i, j: (i, 0, 0)),
                  const((1, d)), const(w_main.shape), const(w_f.shape), const((1, LANES)), const(place.shape)],
        out_specs=[blk(WL), blk(WL), blk(WA), blk(WA), blk(N_HEADS),
                   pl.BlockSpec((1, N_HEADS, LANES, tt), lambda i, j: (i, 0, 0, j)),
                   pl.BlockSpec((1, N_HEADS, tt, LANES), lambda i, j: (i, 0, j, 0)),
                   pl.BlockSpec((1, N_HEADS, LANES, tt), lambda i, j: (i, 0, 0, j)),
                   pl.BlockSpec((1, 1, 1, LANES), lambda i, j: (i, j, 0, 0))],
        out_shape=[f32(WL), f32(WL), f32(WA), f32(WA), f32(N_HEADS),
                   jax.ShapeDtypeStruct((n, N_HEADS, LANES, t), BF16),
                   jax.ShapeDtypeStruct((n, N_HEADS, t, LANES), BF16),
                   jax.ShapeDtypeStruct((n, N_HEADS, LANES, t), BF16),
                   jax.ShapeDtypeStruct((n, nblk, 1, LANES), F32)],
        scratch_shapes=[pltpu.VMEM((1, LANES), F32)],
        compiler_params=_params("parallel", "arbitrary"),
        name="inproj_prompt",
    )(x, mod, g_pre, w_main, w_f, b_f, place)


def _expm1_neg(x):
    poly = x * (1.0 + x * (0.5 + x * (1.0 / 6.0 + x * (1.0 / 24.0 + x * (1.0 / 120.0)))))
    return jnp.where(x > -0.1, poly, jnp.exp(x) - 1.0)


def _lru_kernel(xl_ref, gy_ref, conv0_ref, h0_ref, cw_ref, cb_ref, wr_ref, br_ref, wi_ref, bi_ref,
                lam_ref, g_ref, out_ref, conv_ref, hlast_ref, tail_ref, carry_ref):
    nb, tt, w = xl_ref.shape
    j = pl.program_id(1)

    @pl.when(j == 0)
    def _():
        tail_ref[:, 8 - (CONV_W - 1):, :] = conv0_ref[...]
        carry_ref[...] = h0_ref[...]

    xl = xl_ref[...]
    xpad = jnp.concatenate([tail_ref[...], xl], axis=1)
    cw = cw_ref[...]
    xc = jnp.zeros_like(xl) + cb_ref[...]
    for k in range(CONV_W):
        off = 8 - (CONV_W - 1) + k
        xc = xc + xpad[:, off:off + tt, :] * cw[k:k + 1, :]
    conv_ref[...] = xpad[:, tt + 8 - (CONV_W - 1):, :]
    tail_ref[...] = xpad[:, tt:, :]

    m = nb * tt
    xf = xc.reshape(m, w)
    xb = xf.astype(BF16)
    r = _sigmoid(_dot(xb, wr_ref[...]) + br_ref[...])
    gi = _sigmoid(_dot(xb, wi_ref[...]) + bi_ref[...])
    lam = lam_ref[...]
    softplus = jnp.maximum(-lam, 0.0) + jnp.log1p(jnp.exp(-jnp.abs(lam)))
    log_a = (-LRU_C) * r * softplus
    a = jnp.exp(log_a)
    b = jnp.sqrt(-_expm1_neg(2.0 * log_a)) * (gi * xf)

    groups = m // 8
    a = a.reshape(groups, 8, w)
    b = b.reshape(groups, 8, w)
    sub = lax.broadcasted_iota(jnp.int32, (groups, 8, w), 1)
    for d in (1, 2, 4):
        keep = sub >= d
        a_prev = jnp.where(keep, pltpu.roll(a, d, 1), 1.0)
        b_prev = jnp.where(keep, pltpu.roll(b, d, 1), 0.0)
        b = a * b_prev + b
        a = a * a_prev
    carry = carry_ref[...]
    groups_per_seq = tt // 8
    rows = []
    for g in range(groups):
        if g % groups_per_seq == 0:
            prev = carry[g // groups_per_seq]
        h_g = a[g] * prev + b[g]
        prev = h_g[7:8]
        rows.append(h_g)
    h = jnp.concatenate(rows, axis=0).reshape(nb, tt, w)
    h_last = h[:, tt - 1:tt, :]
    carry_ref[...] = h_last
    hlast_ref[...] = h_last
    out_ref[...] = _rms(h * gy_ref[...], g_ref[...]).astype(BF16)


def _lru(xl, gy, conv0, h0, conv_w, conv_b, wr_bd, b_r, wi_bd, b_i, lam, g_lru):
    n, t, w = xl.shape
    nb, tt = _seq_blocks(n, t)
    blk = pl.BlockSpec((nb, tt, w), lambda i, j: (i, j, 0))
    per_seq = lambda rows: pl.BlockSpec((nb, rows, w), lambda i, j: (i, 0, 0))
    const = lambda shape: pl.BlockSpec(shape, lambda i, j: (0,) * len(shape))
    row = const((1, w))
    return pl.pallas_call(
        _lru_kernel,
        grid=(n // nb, t // tt),
        in_specs=[blk, blk, per_seq(CONV_W - 1), per_seq(1),
                  const((CONV_W, w)), row, const((w, w)), row, const((w, w)), row, row, row],
        out_specs=[blk, per_seq(CONV_W - 1), per_seq(1)],
        out_shape=[jax.ShapeDtypeStruct((n, t, w), BF16),
                   jax.ShapeDtypeStruct((n, CONV_W - 1, w), F32),
                   jax.ShapeDtypeStruct((n, 1, w), F32)],
        scratch_shapes=[pltpu.VMEM((nb, 8, w), F32), pltpu.VMEM((nb, 1, w), F32)],
        compiler_params=_params("parallel", "arbitrary"),
        name="rglru",
    )(xl, gy, conv0, h0, conv_w, conv_b, wr_bd, b_r, wi_bd, b_i, lam, g_lru)


def _cumsum_rows_kernel(x_ref, upper_ref, o_ref, carry_ref):
    tb = x_ref.shape[1]

    @pl.when(pl.program_id(0) == 0)
    def _():
        carry_ref[...] = jnp.zeros_like(carry_ref)

    upper = upper_ref[...]
    hi, mid, lo = _split3(x_ref[...])
    d = _dot(hi, upper) + _dot(mid, upper) + _dot(lo, upper) + carry_ref[...]
    carry_ref[...] = d[:, tb - 1:tb]
    o_ref[...] = d


def _cumsum_rows(x, tb):
    rows, t = x.shape
    upper = jnp.asarray(np.triu(np.ones((tb, tb), np.float32)), BF16)
    return pl.pallas_call(
        _cumsum_rows_kernel,
        grid=(t // tb,),
        in_specs=[pl.BlockSpec((rows, tb), lambda j: (0, j)), pl.BlockSpec((tb, tb), lambda j: (0, 0))],
        out_specs=pl.BlockSpec((rows, tb), lambda j: (0, j)),
        out_shape=jax.ShapeDtypeStruct((rows, t), F32),
        scratch_shapes=[pltpu.VMEM((rows, 1), F32)],
        compiler_params=_params("arbitrary"),
        name="logf_cumsum",
    )(x, upper)


def _online_update(s, m_prev, l_prev):
    m_new = jnp.maximum(m_prev, jnp.max(s, axis=1, keepdims=True))
    alpha = jnp.exp(m_prev - m_new)
    p = jnp.exp(s - m_new)
    l_new = alpha * l_prev + jnp.sum(p, axis=1, keepdims=True)
    return p, alpha, m_new, l_new


def _att_prompt_kernel(bpre_ref, qt_ref, ka_ref, vt_ref, o_ref, s_ref, p_ref, acc_ref):
    tq = qt_ref.shape[2]
    tk = ATT_TK
    nblk = ka_ref.shape[1] // tk
    b, hp, i = pl.program_id(0), pl.program_id(1), pl.program_id(2)
    q0 = i * tq
    jd = q0 // tk
    kpos = lax.broadcasted_iota(jnp.int32, (tk, tq), 0)
    qpos = lax.broadcasted_iota(jnp.int32, (tk, tq), 1)
    rows = lax.broadcasted_iota(jnp.int32, (LANES, tq), 0)
    base = [((b * N_PAIRS + hp) * 2 + hh) * nblk for hh in range(2)]

    def scores(j, masked):
        start = pl.multiple_of(j * tk, tk)
        col_max = []
        for hh in range(2):
            s = _dot(ka_ref[hh, pl.ds(start, tk), :], qt_ref[hh])
            if masked:
                s = jnp.where(kpos + start <= qpos + q0, s, NEG_INF)
            s_ref[hh] = s
            col_max.append(jnp.max(s, axis=0, keepdims=True))
        return tuple(col_max)

    def softmax_pv(j, col_max, m):
        start = pl.multiple_of(j * tk, tk)
        m_out = []
        for hh in range(2):
            c = (bpre_ref[base[hh] + jd] - bpre_ref[base[hh] + j]) * LOG2E
            m_new = jnp.maximum(m[hh], col_max[hh] + c)
            alpha = jnp.exp2(m[hh] - m_new)
            shift = m_new - c
            for ch in range(tk // ATT_CHUNK):
                sl = slice(ch * ATT_CHUNK, (ch + 1) * ATT_CHUNK)
                p_ref[hh, sl, :] = jnp.exp2(s_ref[hh, sl, :] - shift).astype(BF16)
            m_out.append((m_new, alpha))
        return tuple(m_out), start

    def accumulate(m_alpha, start):
        for hh in range(2):
            pv = _dot(vt_ref[hh, :, pl.ds(start, tk)], p_ref[hh])
            acc_ref[hh] = m_alpha[hh][1] * acc_ref[hh] + pv
        return tuple(ma[0] for ma in m_alpha)

    def step(j, next_masked, carry):
        col_max, m = carry
        m_alpha, start = softmax_pv(j, col_max, m)
        col_max_next = scores(j + 1, next_masked)
        return col_max_next, accumulate(m_alpha, start)

    acc_ref[...] = jnp.zeros_like(acc_ref)
    neg = jnp.full((1, tq), NEG_INF, F32)
    n_masked = max(1, tq // tk)
    carry = (scores(0, True), (neg, neg))
    carry = lax.fori_loop(0, jd - 1, lambda j, cr: step(j, False, cr), carry)
    carry = lax.cond(jd > 0, lambda cr: step(jd - 1, True, cr), lambda cr: cr, carry)
    for extra in range(n_masked - 1):
        carry = step(jd + extra, True, carry)
    col_max, m = carry
    m_alpha, start = softmax_pv(jd + n_masked - 1, col_max, m)
    accumulate(m_alpha, start)

    acc_a, acc_b = acc_ref[0], acc_ref[1]
    out_a = acc_a / acc_a[_aug_lane(0):_aug_lane(0) + 1, :]
    out_b = acc_b / acc_b[_aug_lane(1):_aug_lane(1) + 1, :]
    o_ref[0] = jnp.where(rows < HEAD_DIM, out_a, out_b).T


def _att_prompt(qt, ka, vt, bpre):
    n, _, _, t = qt.shape
    assert t % ATT_TQ == 0 and (ATT_TK % ATT_TQ == 0 or ATT_TQ % ATT_TK == 0) and ATT_TK == ROW_TILE
    pair = lambda q: (N_PAIRS, 2) + q.shape[2:]
    qt, ka, vt = (a.reshape((n,) + pair(a)) for a in (qt, ka, vt))
    bflat = jnp.transpose(bpre[:, :, 0, :N_HEADS], (0, 2, 1)).reshape(-1)
    return pl.pallas_call(
        _att_prompt_kernel,
        grid=(n, N_PAIRS, t // ATT_TQ),
        in_specs=[pl.BlockSpec(memory_space=pltpu.SMEM),
                  pl.BlockSpec((None, None, 2, LANES, ATT_TQ), lambda b, h, i: (b, h, 0, 0, i)),
                  pl.BlockSpec((None, None, 2, t, LANES), lambda b, h, i: (b, h, 0, 0, 0),
                               pipeline_mode=pl.Buffered(1)),
                  pl.BlockSpec((None, None, 2, LANES, t), lambda b, h, i: (b, h, 0, 0, 0),
                               pipeline_mode=pl.Buffered(1))],
        out_specs=pl.BlockSpec((1, ATT_TQ, PAIR_W), lambda b, h, i: (b, i, h)),
        out_shape=jax.ShapeDtypeStruct((n, t, WA), F32),
        scratch_shapes=[pltpu.VMEM((2, ATT_TK, ATT_TQ), F32), pltpu.VMEM((2, ATT_TK, ATT_TQ), BF16),
                        pltpu.VMEM((2, LANES, ATT_TQ), F32)],
        compiler_params=_params("parallel", "parallel", "arbitrary"),
        name="att_prompt",
    )(bflat, qt, ka, vt)


def _att_sample_kernel(q_ref, kp_ref, vp_ref, dp_ref, dend_ref, kn_ref, vn_ref, dn_ref, o_ref,
                       m_ref, l_ref, acc_ref):
    t = q_ref.shape[1]
    j = pl.program_id(1)

    @pl.when(j == 0)
    def _():
        m_ref[...] = jnp.full_like(m_ref, NEG_INF)
        l_ref[...] = jnp.zeros_like(l_ref)
        acc_ref[...] = jnp.zeros_like(acc_ref)

    def head(h, k, v, time_minor, bias, mask):
        sl = slice(h * HEAD_DIM, (h + 1) * HEAD_DIM)
        q_h = q_ref[0, :, sl]
        s = (_dot(q_h, k) if time_minor else _dot_nt(q_h, k)) + bias[h:h + 1]
        if mask is not None:
            s = jnp.where(mask, s, NEG_INF)
        p, alpha, m_new, l_new = _online_update(s, m_ref[h], l_ref[h])
        p = p.astype(BF16)
        acc = alpha * acc_ref[:, sl] + (_dot_nt(p, v) if time_minor else _dot(p, v))
        m_ref[h], l_ref[h], acc_ref[:, sl] = m_new, l_new, acc
        return acc / l_new

    bias_past = dend_ref[0] - dp_ref[0]
    for h in range(N_HEADS):
        head(h, kp_ref[0, h].astype(BF16), vp_ref[0, h].astype(BF16), True, bias_past, None)

    @pl.when(j == pl.num_programs(1) - 1)
    def _():
        bias_new = -dn_ref[0]
        qpos = lax.broadcasted_iota(jnp.int32, (t, t), 0)
        kpos = lax.broadcasted_iota(jnp.int32, (t, t), 1)
        for h in range(N_HEADS):
            sl = slice(h * HEAD_DIM, (h + 1) * HEAD_DIM)
            o_ref[0, :, sl] = head(h, kn_ref[0, :, sl], vn_ref[0, :, sl], False, bias_new, kpos <= qpos)


def _att_sample(qb, k_past, v_past, d_past, kb_new, vb_new, d_new):
    n, t, _ = qb.shape
    past = k_past.shape[1]
    tk = min(SAMPLE_TK, past)
    assert past % tk == 0
    d_end = d_past[:, :, past - 1:]
    k_past = jnp.transpose(k_past, (0, 2, 3, 1))
    v_past = jnp.transpose(v_past, (0, 2, 3, 1))
    new = lambda dt: pl.BlockSpec((1, t, WA), lambda b, j: (b, 0, 0))
    cache = pl.BlockSpec((1, N_HEADS, HEAD_DIM, tk), lambda b, j: (b, 0, 0, j))
    return pl.pallas_call(
        _att_sample_kernel,
        grid=(n, past // tk),
        in_specs=[new(BF16), cache, cache,
                  pl.BlockSpec((1, N_HEADS, tk), lambda b, j: (b, 0, j)),
                  pl.BlockSpec((1, N_HEADS, 1), lambda b, j: (b, 0, 0)),
                  new(BF16), new(BF16),
                  pl.BlockSpec((1, N_HEADS, t), lambda b, j: (b, 0, 0))],
        out_specs=pl.BlockSpec((1, t, WA), lambda b, j: (b, 0, 0)),
        out_shape=jax.ShapeDtypeStruct((n, t, WA), F32),
        scratch_shapes=[pltpu.VMEM((N_HEADS, t, 1), F32), pltpu.VMEM((N_HEADS, t, 1), F32),
                        pltpu.VMEM((t, WA), F32)],
        compiler_params=_params("parallel", "arbitrary"),
        name="att_sample",
    )(qb, k_past, v_past, d_past, d_end, kb_new, vb_new, d_new)


def _first_index_of_max(x, axis):
    mx = jnp.max(x, axis=axis, keepdims=True)
    idx = lax.broadcasted_iota(jnp.int32, x.shape, axis)
    first = jnp.min(jnp.where(x == mx, idx, x.shape[axis]), axis=axis, keepdims=True)
    return mx, idx == first


def _route(s, bias):
    m = s.shape[1]
    sb = (s + bias).reshape(N_GROUPS, GROUP_SIZE, m)
    top1, is_top1 = _first_index_of_max(sb, 1)
    top2 = jnp.max(jnp.where(is_top1, NEG_INF, sb), axis=1, keepdims=True)
    grp = (top1 + top2).reshape(N_GROUPS, m)
    gi = lax.broadcasted_iota(jnp.int32, (N_GROUPS, N_GROUPS, m), 0)
    gj = lax.broadcasted_iota(jnp.int32, (N_GROUPS, N_GROUPS, m), 1)
    other, mine = grp[None, :, :], grp[:, None, :]
    beats = (other > mine) | ((other == mine) & (gj < gi))
    g_rank = jnp.sum(beats.astype(jnp.int32), axis=1)
    g_keep = (g_rank < TOPK_GROUPS)[:, None, :]
    cand = jnp.where(g_keep, sb, NEG_INF).reshape(N_EXPERTS, m)
    picks = []
    for _ in range(TOP_K):
        _, pick = _first_index_of_max(cand, 0)
        picks.append(pick)
        cand = jnp.where(pick, NEG_INF, cand)
    w = jnp.concatenate([jnp.sum(jnp.where(pk, s, 0.0), axis=0, keepdims=True) for pk in picks], axis=0)
    return picks, w / jnp.sum(w, axis=0, keepdims=True) * ROUTED_SCALE


HALF_MASK = 0xFFFF0000


def _pack_halves(x):
    c = x.shape[1] // 2
    lo = lax.bitcast_convert_type(x[:, :c].astype(BF16).astype(F32), jnp.uint32) >> jnp.uint32(16)
    hi = lax.bitcast_convert_type(x[:, c:].astype(BF16).astype(F32), jnp.uint32) & jnp.uint32(HALF_MASK)
    return lax.bitcast_convert_type(lo | hi, jnp.int32)


def _unpack_halves(words):
    w = lax.bitcast_convert_type(words, jnp.uint32)
    lo = lax.bitcast_convert_type(w << jnp.uint32(16), F32)
    hi = lax.bitcast_convert_type(w & jnp.uint32(HALF_MASK), F32)
    return lo, hi


def _outproj_kernel(x_ref, lru_ref, att_ref, mod_ref, gatt_ref, wtop_ref, wbot_ref, gpost_ref, gpre_ref,
                    rwh_ref, rwl_ref, rb_ref, cnt_in_ref,
                    x1_ref, hf_ref, xw_ref, ids_ref, ranks_ref, gates_ref, cnt_out_ref, carry_ref):
    nb, tt, d = x_ref.shape
    m = nb * tt

    @pl.when((pl.program_id(0) == 0) & (pl.program_id(1) == 0))
    def _():
        carry_ref[...] = cnt_in_ref[...]

    mod = mod_ref[...]
    att_n = _rms(att_ref[...], gatt_ref[...]).reshape(m, WA).astype(BF16)
    mix = _dot(lru_ref[...].reshape(m, WL), wtop_ref[...]) + _dot(att_n, wbot_ref[...])
    x1 = x_ref[...] + mod[:, 2:3, :] * _rms(mix, gpost_ref[...]).reshape(nb, tt, d)
    x1_ref[...] = x1
    hf = (_rms(x1, gpre_ref[...]) * (1.0 + mod[:, 4:5, :]) + mod[:, 3:4, :]).reshape(m, d)
    hf_hi = hf.astype(BF16)
    hf_ref[...] = hf_hi.reshape(nb, tt, d)
    hf_lo = (hf - hf_hi.astype(F32)).astype(BF16)
    rwh = rwh_ref[...]
    logits = _dot_nt(rwh, hf_hi) + _dot_nt(rwh, hf_lo) + _dot_nt(rwl_ref[...], hf_hi)
    picks, gates = _route(_sigmoid(logits), rb_ref[...])
    xw_ref[...] = _pack_halves(hf).reshape(nb, tt, d // 2)

    sel = jnp.zeros((N_EXPERTS, m), F32)
    for pk in picks:
        sel = sel + pk.astype(F32)
    before = (lax.broadcasted_iota(jnp.int32, (m, m), 0) < lax.broadcasted_iota(jnp.int32, (m, m), 1)).astype(BF16)
    prior = _dot(sel.astype(BF16), before) + carry_ref[...]
    expert = lax.broadcasted_iota(jnp.int32, (N_EXPERTS, m), 0).astype(F32)
    take = lambda pk, v: jnp.sum(jnp.where(pk, v, 0.0), axis=0, keepdims=True)
    ids_ref[...] = jnp.concatenate([take(pk, expert) for pk in picks], axis=0).astype(jnp.int32)
    ranks_ref[...] = jnp.concatenate([take(pk, prior) for pk in picks], axis=0).astype(jnp.int32)
    carry_ref[...] += jnp.sum(sel, axis=1, keepdims=True)
    cnt_out_ref[...] = carry_ref[...]
    gates = jnp.concatenate([gates, jnp.zeros((LANES - TOP_K, m), F32)], axis=0)
    gates_ref[...] = gates.T.reshape(nb, tt, LANES)


def _outproj(x, lru_n, att, mod, g_att, w_top, w_bot, g_post, g_pre, rw_hi, rw_lo, r_bias, cnt_in):
    n, t, d = x.shape
    nb, tt = _seq_blocks(n, t)
    m = nb * tt
    steps_t = t // tt
    blk = lambda w: pl.BlockSpec((nb, tt, w), lambda i, j: (i, j, 0))
    const = lambda shape: pl.BlockSpec(shape, lambda i, j: (0,) * len(shape))
    per_tok = pl.BlockSpec((TOP_K, m), lambda i, j: (0, i * steps_t + j))
    return pl.pallas_call(
        _outproj_kernel,
        grid=(n // nb, steps_t),
        in_specs=[blk(d), blk(WL), blk(WA), pl.BlockSpec((nb, 6, d), lambda i, j: (i, 0, 0)),
                  const((1, WA)), const((WL, d)), const((WA, d)), const((1, d)), const((1, d)),
                  const((N_EXPERTS, d)), const((N_EXPERTS, d)), const((N_EXPERTS, 1)), const((N_EXPERTS, 1))],
        out_specs=[blk(d), blk(d), blk(d // 2), per_tok, per_tok, blk(LANES), const((N_EXPERTS, 1))],
        out_shape=[jax.ShapeDtypeStruct((n, t, d), F32), jax.ShapeDtypeStruct((n, t, d), BF16),
                   jax.ShapeDtypeStruct((n, t, d // 2), jnp.int32),
                   jax.ShapeDtypeStruct((TOP_K, n * t), jnp.int32), jax.ShapeDtypeStruct((TOP_K, n * t), jnp.int32),
                   jax.ShapeDtypeStruct((n, t, LANES), F32), jax.ShapeDtypeStruct((N_EXPERTS, 1), F32)],
        scratch_shapes=[pltpu.VMEM((N_EXPERTS, 1), F32)],
        compiler_params=_params("arbitrary", "arbitrary"),
        name="outproj_router",
    )(x, lru_n, att, mod, g_att, w_top, w_bot, g_post, g_pre, rw_hi, rw_lo, r_bias, cnt_in)


def _subcore_ranges(n_items):
    info = plsc.get_sparse_core_info()
    n_workers = info.num_cores * info.num_subcores
    per_worker = n_items // n_workers
    assert per_worker * n_workers == n_items and per_worker % GATHER_ROWS == 0
    return info, plsc.VectorSubcoreMesh(core_axis_name="c", subcore_axis_name="s"), per_worker


def _scatter_rows(xw, pos, n_slots):
    m_tot, c = xw.shape
    info, mesh, per_worker = _subcore_ranges(m_tot)
    pos_flat = pos.reshape(-1)

    @functools.partial(
        pl.kernel, mesh=mesh, out_type=jax.ShapeDtypeStruct((n_slots, c), jnp.int32),
        scratch_types=[pltpu.VMEM((GATHER_ROWS,), jnp.int32), pltpu.VMEM((GATHER_ROWS, c), jnp.int32),
                       pltpu.SemaphoreType.DMA])
    def scatter(xw_hbm, pos_hbm, out_hbm, idx_v, rows_v, sem):
        worker = lax.axis_index("s") * info.num_cores + lax.axis_index("c")
        base = worker * per_worker

        @pl.loop(0, per_worker // GATHER_ROWS)
        def _(step):
            off = pl.multiple_of(base + step * GATHER_ROWS, GATHER_ROWS)
            pltpu.sync_copy(xw_hbm.at[pl.ds(off, GATHER_ROWS)], rows_v)
            for r in range(TOP_K):
                pltpu.sync_copy(pos_hbm.at[pl.ds(pl.multiple_of(r * m_tot + off, GATHER_ROWS), GATHER_ROWS)], idx_v)
                pltpu.async_copy(rows_v, out_hbm.at[idx_v], sem).wait()

    return scatter(xw, pos_flat)


def _swiglu_halves(lo, hi, wg, wu, wd):
    c = lo.shape[1]
    hg = _dot(lo, wg[:c]) + _dot(hi, wg[c:])
    hu = _dot(lo, wu[:c]) + _dot(hi, wu[c:])
    return _dot((_silu(hg) * hu).astype(BF16), wd)


def _expert_kernel(te_ref, valid_ref, x_ref, wg_ref, wu_ref, wd_ref, y_ref, wg_bf, wu_bf, wd_bf):
    i = pl.program_id(0)
    valid = valid_ref[i]

    @pl.when((i == 0) | (te_ref[i] != te_ref[jnp.maximum(i - 1, 0)]))
    def _():
        wg_bf[...] = wg_ref[0].astype(BF16)
        wu_bf[...] = wu_ref[0].astype(BF16)
        wd_bf[...] = wd_ref[0].astype(BF16)

    @pl.when(valid > 0)
    def _():
        w = x_ref[...]
        row = lax.broadcasted_iota(jnp.int32, w.shape, 0)
        lo, hi = _unpack_halves(jnp.where(row < valid, w, 0))
        y_ref[...] = _pack_halves(_swiglu_halves(lo.astype(BF16), hi.astype(BF16), wg_bf[...], wu_bf[...], wd_bf[...]))


def _experts(xs, tile_expert, tile_valid, wg, wu, wd):
    n_slots, c = xs.shape
    n_tiles = n_slots // EXPERT_TILE
    d = 2 * c
    rows = pl.BlockSpec((EXPERT_TILE, c), lambda i, te, tv: (i, 0))
    weight = lambda shape: pl.BlockSpec((1,) + shape, lambda i, te, tv: (te[i], 0, 0))
    return pl.pallas_call(
        _expert_kernel,
        grid_spec=pltpu.PrefetchScalarGridSpec(
            num_scalar_prefetch=2, grid=(n_tiles,),
            in_specs=[rows, weight((d, D_EXPERT)), weight((d, D_EXPERT)), weight((D_EXPERT, d))],
            out_specs=rows,
            scratch_shapes=[pltpu.VMEM((d, D_EXPERT), BF16), pltpu.VMEM((d, D_EXPERT), BF16),
                            pltpu.VMEM((D_EXPERT, d), BF16)]),
        out_shape=jax.ShapeDtypeStruct((n_slots, c), jnp.int32),
        compiler_params=_params("arbitrary"),
        name="moe_experts",
    )(tile_expert, tile_valid, xs, wg, wu, wd)


def _gather_rows(table, idx):
    b, c = idx.shape[0], table.shape[1]
    info, mesh, per_worker = _subcore_ranges(b)

    n_steps = per_worker // GATHER_ROWS
    assert n_steps % 2 == 0

    @functools.partial(
        pl.kernel, mesh=mesh, out_type=jax.ShapeDtypeStruct((b, c), jnp.int32),
        scratch_types=[pltpu.VMEM((2, GATHER_ROWS), jnp.int32), pltpu.VMEM((2, GATHER_ROWS, c), jnp.int32),
                       pltpu.SemaphoreType.DMA((2,))])
    def gather(table_hbm, idx_hbm, out_hbm, idx_v, rows_v, sems):
        worker = lax.axis_index("s") * info.num_cores + lax.axis_index("c")
        base = worker * per_worker

        def chunk(step):
            return pl.ds(pl.multiple_of(base + step * GATHER_ROWS, GATHER_ROWS), GATHER_ROWS)

        def stream(buf):
            return pltpu.make_async_copy(table_hbm.at[idx_v.at[buf]], rows_v.at[buf], sems.at[buf])

        def start(step, buf):
            pltpu.sync_copy(idx_hbm.at[chunk(step)], idx_v.at[buf])
            stream(buf).start()

        def finish(step, buf):
            stream(buf).wait()
            pltpu.sync_copy(rows_v.at[buf], out_hbm.at[chunk(step)])

        start(0, 0)

        @pl.loop(0, n_steps, step=2)
        def _(step):
            start(step + 1, 1)
            finish(step, 0)

            @pl.when(step + 2 < n_steps)
            def _():
                start(step + 2, 0)

            finish(step + 1, 1)

    return gather(table, idx)


def _combine_kernel(rows_ref, gates_ref, hf_ref, x1_ref, mod_ref, sg_ref, su_ref, sd_ref, gpost_ref, y_ref):
    nb, tt, d = hf_ref.shape
    m = nb * tt
    c = d // 2
    x = hf_ref[...].reshape(m, d)
    shared = _swiglu_halves(x[:, :c], x[:, c:], sg_ref[...], su_ref[...], sd_ref[...])
    gates = gates_ref[...].reshape(m, LANES)
    acc_lo = shared[:, :c]
    acc_hi = shared[:, c:]
    for r in range(TOP_K):
        lo, hi = _unpack_halves(rows_ref[r])
        g = gates[:, r:r + 1]
        acc_lo = acc_lo + g * lo
        acc_hi = acc_hi + g * hi
    z = _rms(jnp.concatenate([acc_lo, acc_hi], axis=1), gpost_ref[...]).reshape(nb, tt, d)
    y_ref[...] = x1_ref[...] + mod_ref[...][:, 5:6, :] * z


def _combine(rows, first_tile, gates_t, hf, x1, mod, sg, su, sd, g_post):
    n, t, d = hf.shape
    nb, tt = _seq_blocks(n, t)
    m = nb * tt
    steps_t = t // tt
    blk = lambda w: pl.BlockSpec((nb, tt, w), lambda i, j: (i, j, 0))
    const = lambda shape: pl.BlockSpec(shape, lambda i, j: (0,) * len(shape))
    return pl.pallas_call(
        _combine_kernel,
        grid=(n // nb, steps_t),
        in_specs=[pl.BlockSpec((TOP_K, m, d // 2), lambda i, j: (0, first_tile + i * steps_t + j, 0)),
                  blk(LANES), blk(d), blk(d), pl.BlockSpec((nb, 6, d), lambda i, j: (i, 0, 0)),
                  const((d, D_EXPERT)), const((d, D_EXPERT)), const((D_EXPERT, d)), const((1, d))],
        out_specs=blk(d),
        out_shape=jax.ShapeDtypeStruct((n, t, d), F32),
        compiler_params=_params("parallel", "parallel"),
        name="moe_combine",
    )(rows, gates_t, hf, x1, mod, sg, su, sd, g_post)


def _slots_kernel(starts_ref, ids_ref, ranks_ref, pos_ref):
    ids = ids_ref[...]

    def add_start(e, pos):
        return pos + jnp.where(ids == e, starts_ref[e], 0)

    pos_ref[...] = lax.fori_loop(0, N_EXPERTS, add_start, ranks_ref[...])


def _slots(starts, ids, ranks):
    k, m_tot = ids.shape
    assert m_tot % SLOT_COLS == 0
    blk = pl.BlockSpec((k, SLOT_COLS), lambda i: (0, i))
    return pl.pallas_call(
        _slots_kernel,
        grid=(m_tot // SLOT_COLS,),
        in_specs=[pl.BlockSpec(memory_space=pltpu.SMEM), blk, blk],
        out_specs=blk,
        out_shape=jax.ShapeDtypeStruct((k, m_tot), jnp.int32),
        compiler_params=_params("arbitrary"),
        name="moe_slots",
    )(starts, ids, ranks)


def _slot_plan(ids, ranks, counts):
    n_pairs = ids.shape[0] * ids.shape[1]
    n_tiles = -(-(n_pairs + N_EXPERTS * (EXPERT_TILE - 1)) // EXPERT_TILE)
    cnt = counts.reshape(N_EXPERTS).astype(jnp.int32)
    padded = (cnt + EXPERT_TILE - 1) // EXPERT_TILE * EXPERT_TILE
    ends = jnp.cumsum(padded)
    starts = ends - padded
    pos = _slots(starts, ids, ranks)
    tile_start = jnp.arange(n_tiles, dtype=jnp.int32) * EXPERT_TILE
    in_expert = (tile_start[:, None] >= starts[None, :]) & (tile_start[:, None] < ends[None, :])
    tile_expert = jnp.sum(jnp.where(in_expert, jnp.arange(N_EXPERTS, dtype=jnp.int32)[None, :], 0), axis=1)
    tile_fill = jnp.sum(jnp.where(in_expert, (starts + cnt)[None, :] - tile_start[:, None], 0), axis=1)
    tile_valid = jnp.clip(tile_fill, 0, EXPERT_TILE).astype(jnp.int32)
    return pos, tile_expert.astype(jnp.int32), tile_valid, n_tiles * EXPERT_TILE


def _block_diag(w):
    g, bw, _ = w.shape
    eye = jnp.eye(g, dtype=w.dtype)
    return (eye[:, None, :, None] * w[:, :, None, :]).reshape(g * bw, g * bw)


def _prep_weights(p):
    d_main = 2 * WL + 3 * WA
    w_in = p["w_in"]
    rw_t = p["router_w"].T
    rw_hi = rw_t.astype(BF16)
    row = lambda v: v.reshape(1, -1)
    return dict(
        w_mod=p["w_mod"], b_mod=p["b_mod"],
        g_pre_mix=row(p["g_pre_mix"]), g_post_mix=row(p["g_post_mix"]),
        g_pre_ffn=row(p["g_pre_ffn"]), g_post_ffn=row(p["g_post_ffn"]),
        w_main=w_in[:, :d_main].astype(BF16),
        w_f=jnp.pad(w_in[:, d_main:], ((0, 0), (0, LANES - N_HEADS))).astype(BF16),
        b_f=jnp.pad(p["b_f"], (0, LANES - N_HEADS)).reshape(1, LANES),
        conv_w=p["conv_w"], conv_b=row(p["conv_b"]),
        wr_bd=_block_diag(p["w_r"]).astype(BF16), b_r=row(p["b_r"]),
        wi_bd=_block_diag(p["w_i"]).astype(BF16), b_i=row(p["b_i"]),
        lam=row(p["lru_lambda"]), g_lru=row(p["g_lru_out"]), g_att=row(p["g_att_out"]),
        w_top=p["w_out"][:WL].astype(BF16), w_bot=p["w_out"][WL:].astype(BF16),
        rw_hi=rw_hi, rw_lo=(rw_t - rw_hi.astype(F32)).astype(BF16),
        r_bias=p["router_bias"].reshape(N_EXPERTS, 1),
        wg=p["w_gate"], wu=p["w_up"], wd=p["w_down"],
        sg=p["ws_gate"].astype(BF16), su=p["ws_up"].astype(BF16), sd=p["ws_down"].astype(BF16),
    )


def _mixers(x, mod, conv0, h0, past, w, cnt_in):
    n, t, _ = x.shape
    proj_args = (x, mod, w["g_pre_mix"], w["w_main"], w["w_f"], w["b_f"])
    if past is None:
        xl, gy, k, v, lf, qt, ka, vt, bpre = _inproj_prompt(*proj_args)
        att = _att_prompt(qt, ka, vt, bpre)
    else:
        xl, gy, qb, kb, vb, k, v, lf = _inproj(*proj_args)
        k_past, v_past, lf_past = past
        plen = k_past.shape[1]
        by_head = lambda a: jnp.transpose(a, (0, 2, 1)).reshape(n * N_HEADS, a.shape[1])
        d_new = _cumsum_rows(by_head(lf), t).reshape(n, N_HEADS, t)
        d_past = _cumsum_rows(by_head(lf_past), min(CUMSUM_COLS, plen)).reshape(n, N_HEADS, plen)
        att = _att_sample(qb, k_past, v_past, d_past, kb, vb, d_new)
    lru_n, conv_new, h_new = _lru(xl, gy, conv0, h0.reshape(n, 1, WL), w["conv_w"], w["conv_b"],
                                  w["wr_bd"], w["b_r"], w["wi_bd"], w["b_i"], w["lam"], w["g_lru"])
    routed = _outproj(x, lru_n, att, mod, w["g_att"], w["w_top"], w["w_bot"], w["g_post_mix"],
                      w["g_pre_ffn"], w["rw_hi"], w["rw_lo"], w["r_bias"], cnt_in)
    state = (k.reshape(n, t, N_HEADS, HEAD_DIM), v.reshape(n, t, N_HEADS, HEAD_DIM), lf,
             conv_new, h_new.reshape(n, WL))
    return routed, state


def _layer(xp, xs, mod_p, mod_s, conv_s, h_s, past_s, w):
    n_p = xp.shape[0]
    conv0 = jnp.zeros((n_p, CONV_W - 1, WL), F32)
    h0 = jnp.zeros((n_p, WL), F32)
    zero_cnt = jnp.zeros((N_EXPERTS, 1), F32)
    (x1_p, hf_p, xw_p, ids_p, rk_p, g_p, cnt_p), st_p = _mixers(xp, mod_p, conv0, h0, None, w, zero_cnt)
    (x1_s, hf_s, xw_s, ids_s, rk_s, g_s, cnt), st_s = _mixers(xs, mod_s, conv_s, h_s, past_s, w, cnt_p)

    half = xw_p.shape[-1]
    xw = jnp.concatenate([xw_p.reshape(-1, half), xw_s.reshape(-1, half)], axis=0)
    ids = jnp.concatenate([ids_p, ids_s], axis=1)
    ranks = jnp.concatenate([rk_p, rk_s], axis=1)
    pos, tile_expert, tile_valid, n_slots = _slot_plan(ids, ranks, cnt)
    ys = _experts(_scatter_rows(xw, pos, n_slots), tile_expert, tile_valid, w["wg"], w["wu"], w["wd"])
    m_p, m_tot = ids_p.shape[1], ids.shape[1]
    rows = _gather_rows(ys, pos.reshape(-1)).reshape(TOP_K, m_tot, half)
    shared = (w["sg"], w["su"], w["sd"], w["g_post_ffn"])
    yp = _combine(rows, 0, g_p, hf_p, x1_p, mod_p, *shared)
    ysmp = _combine(rows, m_p // ROW_TILE, g_s, hf_s, x1_s, mod_s, *shared)
    return yp, ysmp, st_p, st_s


def kernel(x_prompt, x_sample, c_prompt, c_sample, cache_k, cache_v, cache_logf, state_conv, state_lru, w_mod, b_mod, g_pre_mix, g_post_mix, g_pre_ffn, g_post_ffn, w_in, conv_w, conv_b, w_r, b_r, w_i, b_i, lru_lambda, b_f, g_lru_out, g_att_out, w_out, router_w, router_bias, w_gate, w_up, w_down, ws_gate, ws_up, ws_down):
    names = ("w_mod", "b_mod", "g_pre_mix", "g_post_mix", "g_pre_ffn", "g_post_ffn", "w_in", "conv_w", "conv_b",
             "w_r", "b_r", "w_i", "b_i", "lru_lambda", "b_f", "g_lru_out", "g_att_out", "w_out", "router_w",
             "router_bias", "w_gate", "w_up", "w_down", "ws_gate", "ws_up", "ws_down")
    stacked = (w_mod, b_mod, g_pre_mix, g_post_mix, g_pre_ffn, g_post_ffn, w_in, conv_w, conv_b, w_r, b_r, w_i, b_i,
               lru_lambda, b_f, g_lru_out, g_att_out, w_out, router_w, router_bias, w_gate, w_up, w_down,
               ws_gate, ws_up, ws_down)
    depth = w_mod.shape[0]
    n_p, n_s = x_prompt.shape[0], x_sample.shape[0]
    yp, ys = x_prompt, x_sample
    st_p, st_s = [], []
    for l in range(depth):
        w = _prep_weights({k: v[l] for k, v in zip(names, stacked)})
        mod = _modulation(jnp.concatenate([c_prompt, c_sample], axis=0), w["w_mod"], w["b_mod"])
        mod = mod.reshape(n_p + n_s, 6, D_MODEL)
        yp, ys, sp, ss = _layer(yp, ys, mod[:n_p], mod[n_p:], state_conv[l], state_lru[l],
                                (cache_k[l], cache_v[l], cache_logf[l]), w)
        st_p.append(sp)
        st_s.append(ss)
    stack = lambda sts, i: jnp.stack([s[i] for s in sts])
    return (yp, ys) + tuple(stack(st_p, i) for i in range(5)) + tuple(stack(st_s, i) for i in range(5))
```

```python
import functools
import math

import jax
import jax.numpy as jnp
import numpy as np
from jax import lax
from jax.experimental import pallas as pl
from jax.experimental.pallas import tpu as pltpu
from jax.experimental.pallas import tpu_sc as plsc

F32 = jnp.float32
BF16 = jnp.bfloat16

D_MODEL = 1024
WL = 512
WA = 512
N_HEADS = 8
HEAD_DIM = 64
N_PAIRS = N_HEADS // 2
PAIR_W = 2 * HEAD_DIM
LANES = 128
CONV_W = 4
LRU_BLOCKS = 8
LRU_C = 8.0
N_EXPERTS = 64
N_GROUPS = 8
GROUP_SIZE = N_EXPERTS // N_GROUPS
TOPK_GROUPS = 4
TOP_K = 8
D_EXPERT = 256
ROUTED_SCALE = 2.5
EPS = 1e-6
NEG_INF = float("-inf")
LOG2E = 1.4426950408889634

ROW_TILE = 512
ATT_TQ = 512
ATT_TK = 512
ATT_CHUNK = 64
SAMPLE_TK = 4096
CUMSUM_COLS = 1024
EXPERT_TILE = 512
SLOT_COLS = 2048
GATHER_ROWS = 64
VMEM_LIMIT = 56 * 1024 * 1024


def _params(*sem):
    return pltpu.CompilerParams(dimension_semantics=sem, vmem_limit_bytes=VMEM_LIMIT)


def _dot(a, b):
    return jnp.dot(a, b, preferred_element_type=F32)


def _dot_nt(a, b):
    return lax.dot_general(a, b, (((1,), (1,)), ((), ())), preferred_element_type=F32)


def _split3(x):
    hi = x.astype(BF16)
    r1 = x - hi.astype(F32)
    mid = r1.astype(BF16)
    lo = (r1 - mid.astype(F32)).astype(BF16)
    return hi, mid, lo


def _rms(x, g):
    return x * lax.rsqrt(jnp.mean(x * x, axis=-1, keepdims=True) + EPS) * g


def _sigmoid(x):
    return 1.0 / (1.0 + jnp.exp(-x))


def _silu(x):
    return x * _sigmoid(x)


def _gelu_tanh(x):
    return 0.5 * x * (1.0 + jnp.tanh(0.7978845608028654 * (x + 0.044715 * (x * x * x))))


def _log_sigmoid(x):
    return jnp.minimum(x, 0.0) - jnp.log1p(jnp.exp(-jnp.abs(x)))


def _seq_blocks(n, t):
    if t >= ROW_TILE:
        assert t % ROW_TILE == 0
        return 1, ROW_TILE
    nb = ROW_TILE // t
    assert nb * t == ROW_TILE and n % nb == 0
    return nb, t


def _mod_kernel(c_ref, w_ref, b_ref, o_ref):
    c = _silu(c_ref[...])
    c_hi = c.astype(BF16)
    c_lo = (c - c_hi.astype(F32)).astype(BF16)
    w = w_ref[...]
    w_hi = w.astype(BF16)
    w_lo = (w - w_hi.astype(F32)).astype(BF16)
    o_ref[...] = _dot(c_hi, w_hi) + _dot(c_lo, w_hi) + _dot(c_hi, w_lo) + b_ref[...]


def _modulation(c, w_mod, b_mod):
    rows = c.shape[0]
    n = -(-rows // 8) * 8
    c = jnp.pad(c, ((0, n - rows), (0, 0)))
    d6 = w_mod.shape[1]
    return pl.pallas_call(
        _mod_kernel,
        grid=(d6 // D_MODEL,),
        in_specs=[pl.BlockSpec((n, D_MODEL), lambda j: (0, 0)),
                  pl.BlockSpec((D_MODEL, D_MODEL), lambda j: (0, j)),
                  pl.BlockSpec((1, D_MODEL), lambda j: (0, j))],
        out_specs=pl.BlockSpec((n, D_MODEL), lambda j: (0, j)),
        out_shape=jax.ShapeDtypeStruct((n, d6), F32),
        compiler_params=_params("arbitrary"),
        name="modulation",
    )(c, w_mod, b_mod.reshape(1, d6))[:rows]


def _inproj_kernel(x_ref, mod_ref, g_ref, w_ref, wf_ref, bf_ref,
                   xl_ref, gy_ref, qb_ref, kb_ref, vb_ref, k_ref, v_ref, lf_ref):
    nb, tt, d = x_ref.shape
    x = x_ref[...]
    mod = mod_ref[...]
    hn = _rms(x, g_ref[...]) * (1.0 + mod[:, 1:2, :]) + mod[:, 0:1, :]
    hb = hn.reshape(nb * tt, d).astype(BF16)

    def proj(col):
        return _dot(hb, w_ref[:, col * WL:(col + 1) * WL]).reshape(nb, tt, WL)

    xl_ref[...] = proj(0)
    gy_ref[...] = _gelu_tanh(proj(1))
    qb_ref[...] = (proj(2) * (HEAD_DIM ** -0.5)).astype(BF16)
    k = proj(3)
    k_ref[...] = k
    kb_ref[...] = k.astype(BF16)
    v = proj(4)
    v_ref[...] = v
    vb_ref[...] = v.astype(BF16)
    fl = _dot(hb, wf_ref[...]) + bf_ref[...]
    lf_ref[...] = _log_sigmoid(fl).reshape(nb, tt, LANES)[:, :, :N_HEADS]


def _inproj(x, mod, g_pre, w_main, w_f, b_f):
    n, t, d = x.shape
    nb, tt = _seq_blocks(n, t)
    blk = lambda w: pl.BlockSpec((nb, tt, w), lambda i, j: (i, j, 0))
    const = lambda shape: pl.BlockSpec(shape, lambda i, j: (0,) * len(shape))
    f32 = lambda w: jax.ShapeDtypeStruct((n, t, w), F32)
    b16 = lambda w: jax.ShapeDtypeStruct((n, t, w), BF16)
    return pl.pallas_call(
        _inproj_kernel,
        grid=(n // nb, t // tt),
        in_specs=[blk(d),
                  pl.BlockSpec((nb, 6, d), lambda i, j: (i, 0, 0)),
                  const((1, d)), const(w_main.shape), const(w_f.shape), const((1, LANES))],
        out_specs=[blk(WL), blk(WL), blk(WA), blk(WA), blk(WA), blk(WA), blk(WA), blk(N_HEADS)],
        out_shape=[f32(WL), f32(WL), b16(WA), b16(WA), b16(WA), f32(WA), f32(WA), f32(N_HEADS)],
        compiler_params=_params("parallel", "arbitrary"),
        name="inproj",
    )(x, mod, g_pre, w_main, w_f, b_f)


def _aug_lane(h):
    return HEAD_DIM if h % 2 == 0 else 0


def _inproj_prompt_kernel(x_ref, mod_ref, g_ref, w_ref, wf_ref, bf_ref, place_ref,
                          xl_ref, gy_ref, k_ref, v_ref, lf_ref, qt_ref, ka_ref, vt_ref, bpre_ref, carry_ref):
    _, tt, d = x_ref.shape

    @pl.when(pl.program_id(1) == 0)
    def _():
        carry_ref[...] = jnp.zeros_like(carry_ref)

    mod = mod_ref[0]
    hb = (_rms(x_ref[0], g_ref[...]) * (1.0 + mod[1:2, :]) + mod[0:1, :]).astype(BF16)

    def proj(col):
        return _dot(hb, w_ref[:, col * WL:(col + 1) * WL])

    xl_ref[0] = proj(0)
    gy_ref[0] = _gelu_tanh(proj(1))
    q = proj(2) * (HEAD_DIM ** -0.5 * LOG2E)
    k = proj(3)
    k_ref[0] = k
    v = proj(4)
    v_ref[0] = v
    fl = _dot(hb, wf_ref[...]) + bf_ref[...]
    lane = lax.broadcasted_iota(jnp.int32, (tt, LANES), 1)
    lf = jnp.where(lane < N_HEADS, _log_sigmoid(fl), 0.0)
    lf_ref[0] = lf[:, :N_HEADS]

    row = lax.broadcasted_iota(jnp.int32, (tt, tt), 0)
    col = lax.broadcasted_iota(jnp.int32, (tt, tt), 1)
    tril = (col <= row).astype(BF16)
    hi, mid, lo = _split3(lf)
    e = _dot(tril, hi) + _dot(tril, mid) + _dot(tril, lo)
    bpre_ref[0, 0] = carry_ref[...]
    carry_ref[...] += e[tt - 1:tt, :]
    e_hi, e_mid, e_lo = _split3(e * (-LOG2E))
    aug_k = _dot(e_hi, place_ref[0]) + _dot(e_mid, place_ref[1]) + _dot(e_lo, place_ref[2])

    for h in range(N_HEADS):
        pair = slice((h // 2) * PAIR_W, (h // 2 + 1) * PAIR_W)
        dims = (lane < HEAD_DIM) if h % 2 == 0 else (lane >= HEAD_DIM)
        a0 = _aug_lane(h)
        ones3 = ((lane >= a0) & (lane < a0 + 3)).astype(F32)
        one1 = (lane == a0).astype(F32)
        qt_ref[0, h] = (jnp.where(dims, q[:, pair], 0.0) + ones3).T.astype(BF16)
        ka_ref[0, h] = (jnp.where(dims, k[:, pair], 0.0) + aug_k[:, h * LANES:(h + 1) * LANES]).astype(BF16)
        vt_ref[0, h] = (jnp.where(dims, v[:, pair], 0.0) + one1).T.astype(BF16)


def _placement():
    pl_mat = np.zeros((3, LANES, N_HEADS * LANES), np.float32)
    for p in range(3):
        for h in range(N_HEADS):
            pl_mat[p, h, h * LANES + _aug_lane(h) + p] = 1.0
    return jnp.asarray(pl_mat, BF16)


def _inproj_prompt(x, mod, g_pre, w_main, w_f, b_f):
    n, t, d = x.shape
    tt = ROW_TILE
    assert t % tt == 0
    nblk = t // tt
    blk = lambda w: pl.BlockSpec((1, tt, w), lambda i, j: (i, j, 0))
    const = lambda shape: pl.BlockSpec(shape, lambda i, j: (0,) * len(shape))
    f32 = lambda w: jax.ShapeDtypeStruct((n, t, w), F32)
    place = _placement()
    return pl.pallas_call(
        _inproj_prompt_kernel,
        grid=(n, nblk),
        in_specs=[blk(d), pl.BlockSpec((1, 6, d), lambda i, j: (i, 0, 0)),
                  const((1, d)), const(w_main.shape), const(w_f.shape), const((1, LANES)), const(place.shape)],
        out_specs=[blk(WL), blk(WL), blk(WA), blk(WA), blk(N_HEADS),
                   pl.BlockSpec((1, N_HEADS, LANES, tt), lambda i, j: (i, 0, 0, j)),
                   pl.BlockSpec((1, N_HEADS, tt, LANES), lambda i, j: (i, 0, j, 0)),
                   pl.BlockSpec((1, N_HEADS, LANES, tt), lambda i, j: (i, 0, 0, j)),
                   pl.BlockSpec((1, 1, 1, LANES), lambda i, j: (i, j, 0, 0))],
        out_shape=[f32(WL), f32(WL), f32(WA), f32(WA), f32(N_HEADS),
                   jax.ShapeDtypeStruct((n, N_HEADS, LANES, t), BF16),
                   jax.ShapeDtypeStruct((n, N_HEADS, t, LANES), BF16),
                   jax.ShapeDtypeStruct((n, N_HEADS, LANES, t), BF16),
                   jax.ShapeDtypeStruct((n, nblk, 1, LANES), F32)],
        scratch_shapes=[pltpu.VMEM((1, LANES), F32)],
        compiler_params=_params("parallel", "arbitrary"),
        name="inproj_prompt",
    )(x, mod, g_pre, w_main, w_f, b_f, place)


def _expm1_neg(x):
    poly = x * (1.0 + x * (0.5 + x * (1.0 / 6.0 + x * (1.0 / 24.0 + x * (1.0 / 120.0)))))
    return jnp.where(x > -0.1, poly, jnp.exp(x) - 1.0)


def _lru_kernel(xl_ref, gy_ref, conv0_ref, h0_ref, cw_ref, cb_ref, wr_ref, br_ref, wi_ref, bi_ref,
                lam_ref, g_ref, out_ref, conv_ref, hlast_ref, tail_ref, carry_ref):
    nb, tt, w = xl_ref.shape
    j = pl.program_id(1)

    @pl.when(j == 0)
    def _():
        tail_ref[:, 8 - (CONV_W - 1):, :] = conv0_ref[...]
        carry_ref[...] = h0_ref[...]

    xl = xl_ref[...]
    xpad = jnp.concatenate([tail_ref[...], xl], axis=1)
    cw = cw_ref[...]
    xc = jnp.zeros_like(xl) + cb_ref[...]
    for k in range(CONV_W):
        off = 8 - (CONV_W - 1) + k
        xc = xc + xpad[:, off:off + tt, :] * cw[k:k + 1, :]
    conv_ref[...] = xpad[:, tt + 8 - (CONV_W - 1):, :]
    tail_ref[...] = xpad[:, tt:, :]

    m = nb * tt
    xf = xc.reshape(m, w)
    xb = xf.astype(BF16)
    r = _sigmoid(_dot(xb, wr_ref[...]) + br_ref[...])
    gi = _sigmoid(_dot(xb, wi_ref[...]) + bi_ref[...])
    lam = lam_ref[...]
    softplus = jnp.maximum(-lam, 0.0) + jnp.log1p(jnp.exp(-jnp.abs(lam)))
    log_a = (-LRU_C) * r * softplus
    a = jnp.exp(log_a)
    b = jnp.sqrt(-_expm1_neg(2.0 * log_a)) * (gi * xf)

    groups = m // 8
    a = a.reshape(groups, 8, w)
    b = b.reshape(groups, 8, w)
    sub = lax.broadcasted_iota(jnp.int32, (groups, 8, w), 1)
    for d in (1, 2, 4):
        keep = sub >= d
        a_prev = jnp.where(keep, pltpu.roll(a, d, 1), 1.0)
        b_prev = jnp.where(keep, pltpu.roll(b, d, 1), 0.0)
        b = a * b_prev + b
        a = a * a_prev
    carry = carry_ref[...]
    groups_per_seq = tt // 8
    rows = []
    for g in range(groups):
        if g % groups_per_seq == 0:
            prev = carry[g // groups_per_seq]
        h_g = a[g] * prev + b[g]
        prev = h_g[7:8]
        rows.append(h_g)
    h = jnp.concatenate(rows, axis=0).reshape(nb, tt, w)
    h_last = h[:, tt - 1:tt, :]
    carry_ref[...] = h_last
    hlast_ref[...] = h_last
    out_ref[...] = _rms(h * gy_ref[...], g_ref[...]).astype(BF16)


def _lru(xl, gy, conv0, h0, conv_w, conv_b, wr_bd, b_r, wi_bd, b_i, lam, g_lru):
    n, t, w = xl.shape
    nb, tt = _seq_blocks(n, t)
    blk = pl.BlockSpec((nb, tt, w), lambda i, j: (i, j, 0))
    per_seq = lambda rows: pl.BlockSpec((nb, rows, w), lambda i, j: (i, 0, 0))
    const = lambda shape: pl.BlockSpec(shape, lambda i, j: (0,) * len(shape))
    row = const((1, w))
    return pl.pallas_call(
        _lru_kernel,
        grid=(n // nb, t // tt),
        in_specs=[blk, blk, per_seq(CONV_W - 1), per_seq(1),
                  const((CONV_W, w)), row, const((w, w)), row, const((w, w)), row, row, row],
        out_specs=[blk, per_seq(CONV_W - 1), per_seq(1)],
        out_shape=[jax.ShapeDtypeStruct((n, t, w), BF16),
                   jax.ShapeDtypeStruct((n, CONV_W - 1, w), F32),
                   jax.ShapeDtypeStruct((n, 1, w), F32)],
        scratch_shapes=[pltpu.VMEM((nb, 8, w), F32), pltpu.VMEM((nb, 1, w), F32)],
        compiler_params=_params("parallel", "arbitrary"),
        name="rglru",
    )(xl, gy, conv0, h0, conv_w, conv_b, wr_bd, b_r, wi_bd, b_i, lam, g_lru)


def _cumsum_rows_kernel(x_ref, upper_ref, o_ref, carry_ref):
    tb = x_ref.shape[1]

    @pl.when(pl.program_id(0) == 0)
    def _():
        carry_ref[...] = jnp.zeros_like(carry_ref)

    upper = upper_ref[...]
    hi, mid, lo = _split3(x_ref[...])
    d = _dot(hi, upper) + _dot(mid, upper) + _dot(lo, upper) + carry_ref[...]
    carry_ref[...] = d[:, tb - 1:tb]
    o_ref[...] = d


def _cumsum_rows(x, tb):
    rows, t = x.shape
    upper = jnp.asarray(np.triu(np.ones((tb, tb), np.float32)), BF16)
    return pl.pallas_call(
        _cumsum_rows_kernel,
        grid=(t // tb,),
        in_specs=[pl.BlockSpec((rows, tb), lambda j: (0, j)), pl.BlockSpec((tb, tb), lambda j: (0, 0))],
        out_specs=pl.BlockSpec((rows, tb), lambda j: (0, j)),
        out_shape=jax.ShapeDtypeStruct((rows, t), F32),
        scratch_shapes=[pltpu.VMEM((rows, 1), F32)],
        compiler_params=_params("arbitrary"),
        name="logf_cumsum",
    )(x, upper)


def _online_update(s, m_prev, l_prev):
    m_new = jnp.maximum(m_prev, jnp.max(s, axis=1, keepdims=True))
    alpha = jnp.exp(m_prev - m_new)
    p = jnp.exp(s - m_new)
    l_new = alpha * l_prev + jnp.sum(p, axis=1, keepdims=True)
    return p, alpha, m_new, l_new


def _att_prompt_kernel(bpre_ref, qt_ref, ka_ref, vt_ref, o_ref, s_ref, p_ref, acc_ref):
    tq = qt_ref.shape[2]
    tk = ATT_TK
    nblk = ka_ref.shape[1] // tk
    b, hp, i = pl.program_id(0), pl.program_id(1), pl.program_id(2)
    q0 = i * tq
    jd = q0 // tk
    kpos = lax.broadcasted_iota(jnp.int32, (tk, tq), 0)
    qpos = lax.broadcasted_iota(jnp.int32, (tk, tq), 1)
    rows = lax.broadcasted_iota(jnp.int32, (LANES, tq), 0)
    base = [((b * N_PAIRS + hp) * 2 + hh) * nblk for hh in range(2)]

    def scores(j, masked):
        start = pl.multiple_of(j * tk, tk)
        col_max = []
        for hh in range(2):
            s = _dot(ka_ref[hh, pl.ds(start, tk), :], qt_ref[hh])
            if masked:
                s = jnp.where(kpos + start <= qpos + q0, s, NEG_INF)
            s_ref[hh] = s
            col_max.append(jnp.max(s, axis=0, keepdims=True))
        return tuple(col_max)

    def softmax_pv(j, col_max, m):
        start = pl.multiple_of(j * tk, tk)
        m_out = []
        for hh in range(2):
            c = (bpre_ref[base[hh] + jd] - bpre_ref[base[hh] + j]) * LOG2E
            m_new = jnp.maximum(m[hh], col_max[hh] + c)
            alpha = jnp.exp2(m[hh] - m_new)
            shift = m_new - c
            for ch in range(tk // ATT_CHUNK):
                sl = slice(ch * ATT_CHUNK, (ch + 1) * ATT_CHUNK)
                p_ref[hh, sl, :] = jnp.exp2(s_ref[hh, sl, :] - shift).astype(BF16)
            m_out.append((m_new, alpha))
        return tuple(m_out), start

    def accumulate(m_alpha, start):
        for hh in range(2):
            pv = _dot(vt_ref[hh, :, pl.ds(start, tk)], p_ref[hh])
            acc_ref[hh] = m_alpha[hh][1] * acc_ref[hh] + pv
        return tuple(ma[0] for ma in m_alpha)

    def step(j, next_masked, carry):
        col_max, m = carry
        m_alpha, start = softmax_pv(j, col_max, m)
        col_max_next = scores(j + 1, next_masked)
        return col_max_next, accumulate(m_alpha, start)

    acc_ref[...] = jnp.zeros_like(acc_ref)
    neg = jnp.full((1, tq), NEG_INF, F32)
    n_masked = max(1, tq // tk)
    carry = (scores(0, True), (neg, neg))
    carry = lax.fori_loop(0, jd - 1, lambda j, cr: step(j, False, cr), carry)
    carry = lax.cond(jd > 0, lambda cr: step(jd - 1, True, cr), lambda cr: cr, carry)
    for extra in range(n_masked - 1):
        carry = step(jd + extra, True, carry)
    col_max, m = carry
    m_alpha, start = softmax_pv(jd + n_masked - 1, col_max, m)
    accumulate(m_alpha, start)

    acc_a, acc_b = acc_ref[0], acc_ref[1]
    out_a = acc_a / acc_a[_aug_lane(0):_aug_lane(0) + 1, :]
    out_b = acc_b / acc_b[_aug_lane(1):_aug_lane(1) + 1, :]
    o_ref[0] = jnp.where(rows < HEAD_DIM, out_a, out_b).T


def _att_prompt(qt, ka, vt, bpre):
    n, _, _, t = qt.shape
    assert t % ATT_TQ == 0 and (ATT_TK % ATT_TQ == 0 or ATT_TQ % ATT_TK == 0) and ATT_TK == ROW_TILE
    pair = lambda q: (N_PAIRS, 2) + q.shape[2:]
    qt, ka, vt = (a.reshape((n,) + pair(a)) for a in (qt, ka, vt))
    bflat = jnp.transpose(bpre[:, :, 0, :N_HEADS], (0, 2, 1)).reshape(-1)
    return pl.pallas_call(
        _att_prompt_kernel,
        grid=(n, N_PAIRS, t // ATT_TQ),
        in_specs=[pl.BlockSpec(memory_space=pltpu.SMEM),
                  pl.BlockSpec((None, None, 2, LANES, ATT_TQ), lambda b, h, i: (b, h, 0, 0, i)),
                  pl.BlockSpec((None, None, 2, t, LANES), lambda b, h, i: (b, h, 0, 0, 0),
                               pipeline_mode=pl.Buffered(1)),
                  pl.BlockSpec((None, None, 2, LANES, t), lambda b, h, i: (b, h, 0, 0, 0),
                               pipeline_mode=pl.Buffered(1))],
        out_specs=pl.BlockSpec((1, ATT_TQ, PAIR_W), lambda b, h, i: (b, i, h)),
        out_shape=jax.ShapeDtypeStruct((n, t, WA), F32),
        scratch_shapes=[pltpu.VMEM((2, ATT_TK, ATT_TQ), F32), pltpu.VMEM((2, ATT_TK, ATT_TQ), BF16),
                        pltpu.VMEM((2, LANES, ATT_TQ), F32)],
        compiler_params=_params("parallel", "parallel", "arbitrary"),
        name="att_prompt",
    )(bflat, qt, ka, vt)


def _att_sample_kernel(q_ref, kp_ref, vp_ref, dp_ref, dend_ref, kn_ref, vn_ref, dn_ref, o_ref,
                       m_ref, l_ref, acc_ref):
    t = q_ref.shape[1]
    j = pl.program_id(1)

    @pl.when(j == 0)
    def _():
        m_ref[...] = jnp.full_like(m_ref, NEG_INF)
        l_ref[...] = jnp.zeros_like(l_ref)
        acc_ref[...] = jnp.zeros_like(acc_ref)

    def head(h, k, v, time_minor, bias, mask):
        sl = slice(h * HEAD_DIM, (h + 1) * HEAD_DIM)
        q_h = q_ref[0, :, sl]
        s = (_dot(q_h, k) if time_minor else _dot_nt(q_h, k)) + bias[h:h + 1]
        if mask is not None:
            s = jnp.where(mask, s, NEG_INF)
        p, alpha, m_new, l_new = _online_update(s, m_ref[h], l_ref[h])
        p = p.astype(BF16)
        acc = alpha * acc_ref[:, sl] + (_dot_nt(p, v) if time_minor else _dot(p, v))
        m_ref[h], l_ref[h], acc_ref[:, sl] = m_new, l_new, acc
        return acc / l_new

    bias_past = dend_ref[0] - dp_ref[0]
    for h in range(N_HEADS):
        head(h, kp_ref[0, h].astype(BF16), vp_ref[0, h].astype(BF16), True, bias_past, None)

    @pl.when(j == pl.num_programs(1) - 1)
    def _():
        bias_new = -dn_ref[0]
        qpos = lax.broadcasted_iota(jnp.int32, (t, t), 0)
        kpos = lax.broadcasted_iota(jnp.int32, (t, t), 1)
        for h in range(N_HEADS):
            sl = slice(h * HEAD_DIM, (h + 1) * HEAD_DIM)
            o_ref[0, :, sl] = head(h, kn_ref[0, :, sl], vn_ref[0, :, sl], False, bias_new, kpos <= qpos)


def _att_sample(qb, k_past, v_past, d_past, kb_new, vb_new, d_new):
    n, t, _ = qb.shape
    past = k_past.shape[1]
    tk = min(SAMPLE_TK, past)
    assert past % tk == 0
    d_end = d_past[:, :, past - 1:]
    k_past = jnp.transpose(k_past, (0, 2, 3, 1))
    v_past = jnp.transpose(v_past, (0, 2, 3, 1))
    new = lambda dt: pl.BlockSpec((1, t, WA), lambda b, j: (b, 0, 0))
    cache = pl.BlockSpec((1, N_HEADS, HEAD_DIM, tk), lambda b, j: (b, 0, 0, j))
    return pl.pallas_call(
        _att_sample_kernel,
        grid=(n, past // tk),
        in_specs=[new(BF16), cache, cache,
                  pl.BlockSpec((1, N_HEADS, tk), lambda b, j: (b, 0, j)),
                  pl.BlockSpec((1, N_HEADS, 1), lambda b, j: (b, 0, 0)),
                  new(BF16), new(BF16),
                  pl.BlockSpec((1, N_HEADS, t), lambda b, j: (b, 0, 0))],
        out_specs=pl.BlockSpec((1, t, WA), lambda b, j: (b, 0, 0)),
        out_shape=jax.ShapeDtypeStruct((n, t, WA), F32),
        scratch_shapes=[pltpu.VMEM((N_HEADS, t, 1), F32), pltpu.VMEM((N_HEADS, t, 1), F32),
                        pltpu.VMEM((t, WA), F32)],
        compiler_params=_params("parallel", "arbitrary"),
        name="att_sample",
    )(qb, k_past, v_past, d_past, d_end, kb_new, vb_new, d_new)


def _first_index_of_max(x, axis):
    mx = jnp.max(x, axis=axis, keepdims=True)
    idx = lax.broadcasted_iota(jnp.int32, x.shape, axis)
    first = jnp.min(jnp.where(x == mx, idx, x.shape[axis]), axis=axis, keepdims=True)
    return mx, idx == first


def _route(s, bias):
    m = s.shape[1]
    sb = (s + bias).reshape(N_GROUPS, GROUP_SIZE, m)
    top1, is_top1 = _first_index_of_max(sb, 1)
    top2 = jnp.max(jnp.where(is_top1, NEG_INF, sb), axis=1, keepdims=True)
    grp = (top1 + top2).reshape(N_GROUPS, m)
    gi = lax.broadcasted_iota(jnp.int32, (N_GROUPS, N_GROUPS, m), 0)
    gj = lax.broadcasted_iota(jnp.int32, (N_GROUPS, N_GROUPS, m), 1)
    other, mine = grp[None, :, :], grp[:, None, :]
    beats = (other > mine) | ((other == mine) & (gj < gi))
    g_rank = jnp.sum(beats.astype(jnp.int32), axis=1)
    g_keep = (g_rank < TOPK_GROUPS)[:, None, :]
    cand = jnp.where(g_keep, sb, NEG_INF).reshape(N_EXPERTS, m)
    picks = []
    for _ in range(TOP_K):
        _, pick = _first_index_of_max(cand, 0)
        picks.append(pick)
        cand = jnp.where(pick, NEG_INF, cand)
    w = jnp.concatenate([jnp.sum(jnp.where(pk, s, 0.0), axis=0, keepdims=True) for pk in picks], axis=0)
    return picks, w / jnp.sum(w, axis=0, keepdims=True) * ROUTED_SCALE


HALF_MASK = 0xFFFF0000


def _pack_halves(x):
    c = x.shape[1] // 2
    lo = lax.bitcast_convert_type(x[:, :c].astype(BF16).astype(F32), jnp.uint32) >> jnp.uint32(16)
    hi = lax.bitcast_convert_type(x[:, c:].astype(BF16).astype(F32), jnp.uint32) & jnp.uint32(HALF_MASK)
    return lax.bitcast_convert_type(lo | hi, jnp.int32)


def _unpack_halves(words):
    w = lax.bitcast_convert_type(words, jnp.uint32)
    lo = lax.bitcast_convert_type(w << jnp.uint32(16), F32)
    hi = lax.bitcast_convert_type(w & jnp.uint32(HALF_MASK), F32)
    return lo, hi


def _outproj_kernel(x_ref, lru_ref, att_ref, mod_ref, gatt_ref, wtop_ref, wbot_ref, gpost_ref, gpre_ref,
                    rwh_ref, rwl_ref, rb_ref, cnt_in_ref,
                    x1_ref, hf_ref, xw_ref, ids_ref, ranks_ref, gates_ref, cnt_out_ref, carry_ref, *, split):
    nb, tt, d = x_ref.shape
    m = nb * tt
    first = (pl.program_id(0) == 0) & (pl.program_id(1) == 0)

    @pl.when(first)
    def _():
        carry_ref[...] = cnt_in_ref[...]

    if split is not None:
        @pl.when((pl.program_id(0) == split) & (pl.program_id(1) == 0))
        def _():
            carry_ref[...] = jnp.zeros_like(carry_ref)

    mod = mod_ref[...]
    att_n = _rms(att_ref[...], gatt_ref[...]).reshape(m, WA).astype(BF16)
    mix = _dot(lru_ref[...].reshape(m, WL), wtop_ref[...]) + _dot(att_n, wbot_ref[...])
    x1 = x_ref[...] + mod[:, 2:3, :] * _rms(mix, gpost_ref[...]).reshape(nb, tt, d)
    x1_ref[...] = x1
    hf = (_rms(x1, gpre_ref[...]) * (1.0 + mod[:, 4:5, :]) + mod[:, 3:4, :]).reshape(m, d)
    hf_hi = hf.astype(BF16)
    hf_ref[...] = hf_hi.reshape(nb, tt, d)
    hf_lo = (hf - hf_hi.astype(F32)).astype(BF16)
    rwh = rwh_ref[...]
    logits = _dot_nt(rwh, hf_hi) + _dot_nt(rwh, hf_lo) + _dot_nt(rwl_ref[...], hf_hi)
    picks, gates = _route(_sigmoid(logits), rb_ref[...])
    xw_ref[...] = _pack_halves(hf).reshape(nb, tt, d // 2)

    sel = jnp.zeros((N_EXPERTS, m), F32)
    for pk in picks:
        sel = sel + pk.astype(F32)
    before = (lax.broadcasted_iota(jnp.int32, (m, m), 0) < lax.broadcasted_iota(jnp.int32, (m, m), 1)).astype(BF16)
    prior = _dot(sel.astype(BF16), before) + carry_ref[...]
    expert = lax.broadcasted_iota(jnp.int32, (N_EXPERTS, m), 0).astype(F32)
    take = lambda pk, v: jnp.sum(jnp.where(pk, v, 0.0), axis=0, keepdims=True)
    ids_ref[...] = jnp.concatenate([take(pk, expert) for pk in picks], axis=0).astype(jnp.int32)
    ranks_ref[...] = jnp.concatenate([take(pk, prior) for pk in picks], axis=0).astype(jnp.int32)
    carry_ref[...] += jnp.sum(sel, axis=1, keepdims=True)
    cnt_out_ref[0] = carry_ref[...]
    gates = jnp.concatenate([gates, jnp.zeros((LANES - TOP_K, m), F32)], axis=0)
    gates_ref[...] = gates.T.reshape(nb, tt, LANES)


def _outproj(x, lru_n, att, mod, g_att, w_top, w_bot, g_post, g_pre, rw_hi, rw_lo, r_bias, cnt_in, split=None):
    n, t, d = x.shape
    nb, tt = _seq_blocks(n, t)
    m = nb * tt
    steps_t = t // tt
    n_groups = 1 if split is None else 2
    blk = lambda w: pl.BlockSpec((nb, tt, w), lambda i, j: (i, j, 0))
    const = lambda shape: pl.BlockSpec(shape, lambda i, j: (0,) * len(shape))
    per_tok = pl.BlockSpec((TOP_K, m), lambda i, j: (0, i * steps_t + j))
    group_of = (lambda i: 0) if split is None else (lambda i: jnp.where(i < split, 0, 1))
    return pl.pallas_call(
        functools.partial(_outproj_kernel, split=split),
        grid=(n // nb, steps_t),
        in_specs=[blk(d), blk(WL), blk(WA), pl.BlockSpec((nb, 6, d), lambda i, j: (i, 0, 0)),
                  const((1, WA)), const((WL, d)), const((WA, d)), const((1, d)), const((1, d)),
                  const((N_EXPERTS, d)), const((N_EXPERTS, d)), const((N_EXPERTS, 1)), const((N_EXPERTS, 1))],
        out_specs=[blk(d), blk(d), blk(d // 2), per_tok, per_tok, blk(LANES),
                   pl.BlockSpec((1, N_EXPERTS, 1), lambda i, j: (group_of(i), 0, 0))],
        out_shape=[jax.ShapeDtypeStruct((n, t, d), F32), jax.ShapeDtypeStruct((n, t, d), BF16),
                   jax.ShapeDtypeStruct((n, t, d // 2), jnp.int32),
                   jax.ShapeDtypeStruct((TOP_K, n * t), jnp.int32), jax.ShapeDtypeStruct((TOP_K, n * t), jnp.int32),
                   jax.ShapeDtypeStruct((n, t, LANES), F32), jax.ShapeDtypeStruct((n_groups, N_EXPERTS, 1), F32)],
        scratch_shapes=[pltpu.VMEM((N_EXPERTS, 1), F32)],
        compiler_params=_params("arbitrary", "arbitrary"),
        name="outproj_router",
    )(x, lru_n, att, mod, g_att, w_top, w_bot, g_post, g_pre, rw_hi, rw_lo, r_bias, cnt_in)


def _subcore_ranges(n_items):
    info = plsc.get_sparse_core_info()
    n_workers = info.num_cores * info.num_subcores
    per_worker = n_items // n_workers
    assert per_worker * n_workers == n_items and per_worker % GATHER_ROWS == 0
    return info, plsc.VectorSubcoreMesh(core_axis_name="c", subcore_axis_name="s"), per_worker


def _scatter_rows(xw, pos, n_slots, row_offset=0):
    m_tot, c = pos.shape[1], xw.shape[1]
    assert row_offset % GATHER_ROWS == 0
    info, mesh, per_worker = _subcore_ranges(m_tot)
    pos_flat = pos.reshape(-1)

    @functools.partial(
        pl.kernel, mesh=mesh, out_type=jax.ShapeDtypeStruct((n_slots, c), jnp.int32),
        scratch_types=[pltpu.VMEM((GATHER_ROWS,), jnp.int32), pltpu.VMEM((GATHER_ROWS, c), jnp.int32),
                       pltpu.SemaphoreType.DMA])
    def scatter(xw_hbm, pos_hbm, out_hbm, idx_v, rows_v, sem):
        worker = lax.axis_index("s") * info.num_cores + lax.axis_index("c")
        base = worker * per_worker

        @pl.loop(0, per_worker // GATHER_ROWS)
        def _(step):
            off = pl.multiple_of(base + step * GATHER_ROWS, GATHER_ROWS)
            pltpu.sync_copy(xw_hbm.at[pl.ds(row_offset + off, GATHER_ROWS)], rows_v)
            for r in range(TOP_K):
                pltpu.sync_copy(pos_hbm.at[pl.ds(pl.multiple_of(r * m_tot + off, GATHER_ROWS), GATHER_ROWS)], idx_v)
                pltpu.async_copy(rows_v, out_hbm.at[idx_v], sem).wait()

    return scatter(xw, pos_flat)


def _swiglu_halves(lo, hi, wg, wu, wd):
    c = lo.shape[1]
    hg = _dot(lo, wg[:c]) + _dot(hi, wg[c:])
    hu = _dot(lo, wu[:c]) + _dot(hi, wu[c:])
    return _dot((_silu(hg) * hu).astype(BF16), wd)


def _expert_kernel(te_ref, valid_ref, x_ref, wg_ref, wu_ref, wd_ref, y_ref, wg_bf, wu_bf, wd_bf):
    i = pl.program_id(0)
    valid = valid_ref[i]

    @pl.when((i == 0) | (te_ref[i] != te_ref[jnp.maximum(i - 1, 0)]))
    def _():
        wg_bf[...] = wg_ref[0].astype(BF16)
        wu_bf[...] = wu_ref[0].astype(BF16)
        wd_bf[...] = wd_ref[0].astype(BF16)

    @pl.when(valid > 0)
    def _():
        w = x_ref[...]
        row = lax.broadcasted_iota(jnp.int32, w.shape, 0)
        lo, hi = _unpack_halves(jnp.where(row < valid, w, 0))
        y_ref[...] = _pack_halves(_swiglu_halves(lo.astype(BF16), hi.astype(BF16), wg_bf[...], wu_bf[...], wd_bf[...]))


def _experts(xs, tile_expert, tile_valid, wg, wu, wd):
    n_slots, c = xs.shape
    n_tiles = n_slots // EXPERT_TILE
    d = 2 * c
    rows = pl.BlockSpec((EXPERT_TILE, c), lambda i, te, tv: (i, 0))
    weight = lambda shape: pl.BlockSpec((1,) + shape, lambda i, te, tv: (te[i], 0, 0))
    return pl.pallas_call(
        _expert_kernel,
        grid_spec=pltpu.PrefetchScalarGridSpec(
            num_scalar_prefetch=2, grid=(n_tiles,),
            in_specs=[rows, weight((d, D_EXPERT)), weight((d, D_EXPERT)), weight((D_EXPERT, d))],
            out_specs=rows,
            scratch_shapes=[pltpu.VMEM((d, D_EXPERT), BF16), pltpu.VMEM((d, D_EXPERT), BF16),
                            pltpu.VMEM((D_EXPERT, d), BF16)]),
        out_shape=jax.ShapeDtypeStruct((n_slots, c), jnp.int32),
        compiler_params=_params("arbitrary"),
        name="moe_experts",
    )(tile_expert, tile_valid, xs, wg, wu, wd)


def _gather_rows(table, idx):
    b, c = idx.shape[0], table.shape[1]
    info, mesh, per_worker = _subcore_ranges(b)

    n_steps = per_worker // GATHER_ROWS
    assert n_steps % 2 == 0

    @functools.partial(
        pl.kernel, mesh=mesh, out_type=jax.ShapeDtypeStruct((b, c), jnp.int32),
        scratch_types=[pltpu.VMEM((2, GATHER_ROWS), jnp.int32), pltpu.VMEM((2, GATHER_ROWS, c), jnp.int32),
                       pltpu.SemaphoreType.DMA((2,))])
    def gather(table_hbm, idx_hbm, out_hbm, idx_v, rows_v, sems):
        worker = lax.axis_index("s") * info.num_cores + lax.axis_index("c")
        base = worker * per_worker

        def chunk(step):
            return pl.ds(pl.multiple_of(base + step * GATHER_ROWS, GATHER_ROWS), GATHER_ROWS)

        def stream(buf):
            return pltpu.make_async_copy(table_hbm.at[idx_v.at[buf]], rows_v.at[buf], sems.at[buf])

        def start(step, buf):
            pltpu.sync_copy(idx_hbm.at[chunk(step)], idx_v.at[buf])
            stream(buf).start()

        def finish(step, buf):
            stream(buf).wait()
            pltpu.sync_copy(rows_v.at[buf], out_hbm.at[chunk(step)])

        start(0, 0)

        @pl.loop(0, n_steps, step=2)
        def _(step):
            start(step + 1, 1)
            finish(step, 0)

            @pl.when(step + 2 < n_steps)
            def _():
                start(step + 2, 0)

            finish(step + 1, 1)

    return gather(table, idx)


def _combine_kernel(rows_ref, gates_ref, hf_ref, x1_ref, mod_ref, sg_ref, su_ref, sd_ref, gpost_ref, y_ref):
    nb, tt, d = hf_ref.shape
    m = nb * tt
    c = d // 2
    x = hf_ref[...].reshape(m, d)
    shared = _swiglu_halves(x[:, :c], x[:, c:], sg_ref[...], su_ref[...], sd_ref[...])
    gates = gates_ref[...].reshape(m, LANES)
    acc_lo = shared[:, :c]
    acc_hi = shared[:, c:]
    for r in range(TOP_K):
        lo, hi = _unpack_halves(rows_ref[r])
        g = gates[:, r:r + 1]
        acc_lo = acc_lo + g * lo
        acc_hi = acc_hi + g * hi
    z = _rms(jnp.concatenate([acc_lo, acc_hi], axis=1), gpost_ref[...]).reshape(nb, tt, d)
    y_ref[...] = x1_ref[...] + mod_ref[...][:, 5:6, :] * z


def _combine(rows, first_tile, gates_t, hf, x1, mod, sg, su, sd, g_post, blocks=None, y_prev=None):
    n, t, d = hf.shape
    nb, tt = _seq_blocks(n, t)
    m = nb * tt
    steps_t = t // tt
    b0, nblocks = (0, n // nb) if blocks is None else blocks
    blk = lambda w: pl.BlockSpec((nb, tt, w), lambda i, j: (b0 + i, j, 0))
    const = lambda shape: pl.BlockSpec(shape, lambda i, j: (0,) * len(shape))
    in_specs = [pl.BlockSpec((TOP_K, m, d // 2), lambda i, j: (0, first_tile + i * steps_t + j, 0)),
                blk(LANES), blk(d), blk(d), pl.BlockSpec((nb, 6, d), lambda i, j: (b0 + i, 0, 0)),
                const((d, D_EXPERT)), const((d, D_EXPERT)), const((D_EXPERT, d)), const((1, d))]
    args = (rows, gates_t, hf, x1, mod, sg, su, sd, g_post)
    kernel_fn, aliases = _combine_kernel, {}
    if y_prev is not None:
        in_specs.append(pl.BlockSpec(memory_space=pl.ANY))
        args += (y_prev,)
        aliases = {len(args) - 1: 0}
        kernel_fn = lambda *refs: _combine_kernel(*refs[:len(args) - 1], refs[-1])
    return pl.pallas_call(
        kernel_fn,
        grid=(nblocks, steps_t),
        in_specs=in_specs,
        out_specs=blk(d),
        out_shape=jax.ShapeDtypeStruct((n, t, d), F32),
        input_output_aliases=aliases,
        compiler_params=_params("parallel", "parallel"),
        name="moe_combine",
    )(*args)


def _slots_kernel(starts_ref, ids_ref, ranks_ref, pos_ref):
    ids = ids_ref[...]

    def add_start(e, pos):
        return pos + jnp.where(ids == e, starts_ref[e], 0)

    pos_ref[...] = lax.fori_loop(0, N_EXPERTS, add_start, ranks_ref[...])


def _slots(starts, ids, ranks):
    k, m_tot = ids.shape
    cols = math.gcd(m_tot, SLOT_COLS)
    assert cols % LANES == 0
    blk = pl.BlockSpec((k, cols), lambda i: (0, i))
    return pl.pallas_call(
        _slots_kernel,
        grid=(m_tot // cols,),
        in_specs=[pl.BlockSpec(memory_space=pltpu.SMEM), blk, blk],
        out_specs=blk,
        out_shape=jax.ShapeDtypeStruct((k, m_tot), jnp.int32),
        compiler_params=_params("arbitrary"),
        name="moe_slots",
    )(starts, ids, ranks)


def _slot_plan(ids, ranks, counts):
    n_pairs = ids.shape[0] * ids.shape[1]
    n_tiles = -(-(n_pairs + N_EXPERTS * (EXPERT_TILE - 1)) // EXPERT_TILE)
    cnt = counts.reshape(N_EXPERTS).astype(jnp.int32)
    padded = (cnt + EXPERT_TILE - 1) // EXPERT_TILE * EXPERT_TILE
    ends = jnp.cumsum(padded)
    starts = ends - padded
    pos = _slots(starts, ids, ranks)
    tile_start = jnp.arange(n_tiles, dtype=jnp.int32) * EXPERT_TILE
    in_expert = (tile_start[:, None] >= starts[None, :]) & (tile_start[:, None] < ends[None, :])
    tile_expert = jnp.sum(jnp.where(in_expert, jnp.arange(N_EXPERTS, dtype=jnp.int32)[None, :], 0), axis=1)
    tile_fill = jnp.sum(jnp.where(in_expert, (starts + cnt)[None, :] - tile_start[:, None], 0), axis=1)
    tile_valid = jnp.clip(tile_fill, 0, EXPERT_TILE).astype(jnp.int32)
    return pos, tile_expert.astype(jnp.int32), tile_valid, n_tiles * EXPERT_TILE


def _block_diag(w):
    g, bw, _ = w.shape
    eye = jnp.eye(g, dtype=w.dtype)
    return (eye[:, None, :, None] * w[:, :, None, :]).reshape(g * bw, g * bw)


def _prep_weights(p):
    d_main = 2 * WL + 3 * WA
    w_in = p["w_in"]
    rw_t = p["router_w"].T
    rw_hi = rw_t.astype(BF16)
    row = lambda v: v.reshape(1, -1)
    return dict(
        w_mod=p["w_mod"], b_mod=p["b_mod"],
        g_pre_mix=row(p["g_pre_mix"]), g_post_mix=row(p["g_post_mix"]),
        g_pre_ffn=row(p["g_pre_ffn"]), g_post_ffn=row(p["g_post_ffn"]),
        w_main=w_in[:, :d_main].astype(BF16),
        w_f=jnp.pad(w_in[:, d_main:], ((0, 0), (0, LANES - N_HEADS))).astype(BF16),
        b_f=jnp.pad(p["b_f"], (0, LANES - N_HEADS)).reshape(1, LANES),
        conv_w=p["conv_w"], conv_b=row(p["conv_b"]),
        wr_bd=_block_diag(p["w_r"]).astype(BF16), b_r=row(p["b_r"]),
        wi_bd=_block_diag(p["w_i"]).astype(BF16), b_i=row(p["b_i"]),
        lam=row(p["lru_lambda"]), g_lru=row(p["g_lru_out"]), g_att=row(p["g_att_out"]),
        w_top=p["w_out"][:WL].astype(BF16), w_bot=p["w_out"][WL:].astype(BF16),
        rw_hi=rw_hi, rw_lo=(rw_t - rw_hi.astype(F32)).astype(BF16),
        r_bias=p["router_bias"].reshape(N_EXPERTS, 1),
        wg=p["w_gate"], wu=p["w_up"], wd=p["w_down"],
        sg=p["ws_gate"].astype(BF16), su=p["ws_up"].astype(BF16), sd=p["ws_down"].astype(BF16),
    )


def _mixers(x, mod, conv0, h0, past, w, cnt_in, split=None):
    n, t, _ = x.shape
    proj_args = (x, mod, w["g_pre_mix"], w["w_main"], w["w_f"], w["b_f"])
    if past is None:
        xl, gy, k, v, lf, qt, ka, vt, bpre = _inproj_prompt(*proj_args)
        att = _att_prompt(qt, ka, vt, bpre)
    else:
        xl, gy, qb, kb, vb, k, v, lf = _inproj(*proj_args)
        k_past, v_past, lf_past = past
        plen = k_past.shape[1]
        by_head = lambda a: jnp.transpose(a, (0, 2, 1)).reshape(n * N_HEADS, a.shape[1])
        d_new = _cumsum_rows(by_head(lf), t).reshape(n, N_HEADS, t)
        d_past = _cumsum_rows(by_head(lf_past), min(CUMSUM_COLS, plen)).reshape(n, N_HEADS, plen)
        att = _att_sample(qb, k_past, v_past, d_past, kb, vb, d_new)
    lru_n, conv_new, h_new = _lru(xl, gy, conv0, h0.reshape(n, 1, WL), w["conv_w"], w["conv_b"],
                                  w["wr_bd"], w["b_r"], w["wi_bd"], w["b_i"], w["lam"], w["g_lru"])
    routed = _outproj(x, lru_n, att, mod, w["g_att"], w["w_top"], w["w_bot"], w["g_post_mix"],
                      w["g_pre_ffn"], w["rw_hi"], w["rw_lo"], w["r_bias"], cnt_in, split)
    state = (k.reshape(n, t, N_HEADS, HEAD_DIM), v.reshape(n, t, N_HEADS, HEAD_DIM), lf,
             conv_new, h_new.reshape(n, WL))
    return routed, state


def _layer(xp, xs, mod_p, mod_s, conv_s, h_s, past_s, w):
    n_p = xp.shape[0]
    conv0 = jnp.zeros((n_p, CONV_W - 1, WL), F32)
    h0 = jnp.zeros((n_p, WL), F32)
    zero_cnt = jnp.zeros((N_EXPERTS, 1), F32)
    t_p = xp.shape[1]
    split = 1 if n_p > 1 else None
    (x1_p, hf_p, xw_p, ids_p, rk_p, g_p, cnt_p), st_p = _mixers(xp, mod_p, conv0, h0, None, w, zero_cnt, split)
    cnt_tail = cnt_p[1] if n_p > 1 else zero_cnt
    (x1_s, hf_s, xw_s, ids_s, rk_s, g_s, cnt_s), st_s = _mixers(xs, mod_s, conv_s, h_s, past_s, w, cnt_tail)

    half = xw_p.shape[-1]
    xw_p = xw_p.reshape(-1, half)

    def routed(xw, row_offset, ids, ranks, counts):
        pos, tile_expert, tile_valid, n_slots = _slot_plan(ids, ranks, counts)
        ys = _experts(_scatter_rows(xw, pos, n_slots, row_offset), tile_expert, tile_valid,
                      w["wg"], w["wu"], w["wd"])
        return _gather_rows(ys, pos.reshape(-1)).reshape(TOP_K, ids.shape[1], half)

    rows_a = routed(xw_p, 0, ids_p[:, :t_p], rk_p[:, :t_p], cnt_p[0])
    xw_b = jnp.concatenate([xw_p[t_p:], xw_s.reshape(-1, half)], axis=0)
    rows_b = routed(xw_b, 0, jnp.concatenate([ids_p[:, t_p:], ids_s], axis=1),
                    jnp.concatenate([rk_p[:, t_p:], rk_s], axis=1), cnt_s[0])
    shared = (w["sg"], w["su"], w["sd"], w["g_post_ffn"])
    yp = _combine(rows_a, 0, g_p, hf_p, x1_p, mod_p, *shared, blocks=(0, 1))
    if n_p > 1:
        yp = _combine(rows_b, 0, g_p, hf_p, x1_p, mod_p, *shared, blocks=(1, n_p - 1), y_prev=yp)
    ysmp = _combine(rows_b, (n_p - 1) * t_p // ROW_TILE, g_s, hf_s, x1_s, mod_s, *shared)
    return yp, ysmp, st_p, st_s


def kernel(x_prompt, x_sample, c_prompt, c_sample, cache_k, cache_v, cache_logf, state_conv, state_lru, w_mod, b_mod, g_pre_mix, g_post_mix, g_pre_ffn, g_post_ffn, w_in, conv_w, conv_b, w_r, b_r, w_i, b_i, lru_lambda, b_f, g_lru_out, g_att_out, w_out, router_w, router_bias, w_gate, w_up, w_down, ws_gate, ws_up, ws_down):
    names = ("w_mod", "b_mod", "g_pre_mix", "g_post_mix", "g_pre_ffn", "g_post_ffn", "w_in", "conv_w", "conv_b",
             "w_r", "b_r", "w_i", "b_i", "lru_lambda", "b_f", "g_lru_out", "g_att_out", "w_out", "router_w",
             "router_bias", "w_gate", "w_up", "w_down", "ws_gate", "ws_up", "ws_down")
    stacked = (w_mod, b_mod, g_pre_mix, g_post_mix, g_pre_ffn, g_post_ffn, w_in, conv_w, conv_b, w_r, b_r, w_i, b_i,
               lru_lambda, b_f, g_lru_out, g_att_out, w_out, router_w, router_bias, w_gate, w_up, w_down,
               ws_gate, ws_up, ws_down)
    depth = w_mod.shape[0]
    n_p, n_s = x_prompt.shape[0], x_sample.shape[0]
    yp, ys = x_prompt, x_sample
    st_p, st_s = [], []
    for l in range(depth):
        w = _prep_weights({k: v[l] for k, v in zip(names, stacked)})
        mod = _modulation(jnp.concatenate([c_prompt, c_sample], axis=0), w["w_mod"], w["b_mod"])
        mod = mod.reshape(n_p + n_s, 6, D_MODEL)
        yp, ys, sp, ss = _layer(yp, ys, mod[:n_p], mod[n_p:], state_conv[l], state_lru[l],
                                (cache_k[l], cache_v[l], cache_logf[l]), w)
        st_p.append(sp)
        st_s.append(ss)
    stack = lambda sts, i: jnp.stack([s[i] for s in sts])
    return (yp, ys) + tuple(stack(st_p, i) for i in range(5)) + tuple(stack(st_s, i) for i in range(5))
```

```python
import functools
import math

import jax
import jax.numpy as jnp
import numpy as np
from jax import lax
from jax.experimental import pallas as pl
from jax.experimental.pallas import tpu as pltpu
from jax.experimental.pallas import tpu_sc as plsc

F32 = jnp.float32
BF16 = jnp.bfloat16

D_MODEL = 1024
WL = 512
WA = 512
N_HEADS = 8
HEAD_DIM = 64
N_PAIRS = N_HEADS // 2
PAIR_W = 2 * HEAD_DIM
LANES = 128
CONV_W = 4
LRU_BLOCKS = 8
LRU_C = 8.0
N_EXPERTS = 64
N_GROUPS = 8
GROUP_SIZE = N_EXPERTS // N_GROUPS
TOPK_GROUPS = 4
TOP_K = 8
D_EXPERT = 256
ROUTED_SCALE = 2.5
EPS = 1e-6
NEG_INF = float("-inf")
LOG2E = 1.4426950408889634

ROW_TILE = 512
ATT_TQ = 512
ATT_TK = 512
ATT_CHUNK = 64
SAMPLE_TK = 4096
CUMSUM_COLS = 1024
EXPERT_TILE = 512
SLOT_COLS = 2048
GATHER_ROWS = 64
VMEM_LIMIT = 56 * 1024 * 1024


def _params(*sem):
    return pltpu.CompilerParams(dimension_semantics=sem, vmem_limit_bytes=VMEM_LIMIT)


def _dot(a, b):
    return jnp.dot(a, b, preferred_element_type=F32)


def _dot_nt(a, b):
    return lax.dot_general(a, b, (((1,), (1,)), ((), ())), preferred_element_type=F32)


def _split3(x):
    hi = x.astype(BF16)
    r1 = x - hi.astype(F32)
    mid = r1.astype(BF16)
    lo = (r1 - mid.astype(F32)).astype(BF16)
    return hi, mid, lo


def _rms(x, g):
    return x * lax.rsqrt(jnp.mean(x * x, axis=-1, keepdims=True) + EPS) * g


def _sigmoid(x):
    return 1.0 / (1.0 + jnp.exp(-x))


def _silu(x):
    return x * _sigmoid(x)


def _gelu_tanh(x):
    return 0.5 * x * (1.0 + jnp.tanh(0.7978845608028654 * (x + 0.044715 * (x * x * x))))


def _log_sigmoid(x):
    return jnp.minimum(x, 0.0) - jnp.log1p(jnp.exp(-jnp.abs(x)))


def _seq_blocks(n, t):
    if t >= ROW_TILE:
        assert t % ROW_TILE == 0
        return 1, ROW_TILE
    nb = ROW_TILE // t
    assert nb * t == ROW_TILE and n % nb == 0
    return nb, t


def _mod_kernel(c_ref, w_ref, b_ref, o_ref):
    c = _silu(c_ref[...])
    c_hi = c.astype(BF16)
    c_lo = (c - c_hi.astype(F32)).astype(BF16)
    w = w_ref[...]
    w_hi = w.astype(BF16)
    w_lo = (w - w_hi.astype(F32)).astype(BF16)
    o_ref[...] = _dot(c_hi, w_hi) + _dot(c_lo, w_hi) + _dot(c_hi, w_lo) + b_ref[...]


def _modulation(c, w_mod, b_mod):
    rows = c.shape[0]
    n = -(-rows // 8) * 8
    c = jnp.pad(c, ((0, n - rows), (0, 0)))
    d6 = w_mod.shape[1]
    return pl.pallas_call(
        _mod_kernel,
        grid=(d6 // D_MODEL,),
        in_specs=[pl.BlockSpec((n, D_MODEL), lambda j: (0, 0)),
                  pl.BlockSpec((D_MODEL, D_MODEL), lambda j: (0, j)),
                  pl.BlockSpec((1, D_MODEL), lambda j: (0, j))],
        out_specs=pl.BlockSpec((n, D_MODEL), lambda j: (0, j)),
        out_shape=jax.ShapeDtypeStruct((n, d6), F32),
        compiler_params=_params("arbitrary"),
        name="modulation",
    )(c, w_mod, b_mod.reshape(1, d6))[:rows]


def _inproj_kernel(x_ref, mod_ref, g_ref, w_ref, wf_ref, bf_ref,
                   xl_ref, gy_ref, qb_ref, kb_ref, vb_ref, k_ref, v_ref, lf_ref):
    nb, tt, d = x_ref.shape
    x = x_ref[...]
    mod = mod_ref[...]
    hn = _rms(x, g_ref[...]) * (1.0 + mod[:, 1:2, :]) + mod[:, 0:1, :]
    hb = hn.reshape(nb * tt, d).astype(BF16)

    def proj(col):
        return _dot(hb, w_ref[:, col * WL:(col + 1) * WL]).reshape(nb, tt, WL)

    xl_ref[...] = proj(0)
    gy_ref[...] = _gelu_tanh(proj(1))
    qb_ref[...] = (proj(2) * (HEAD_DIM ** -0.5)).astype(BF16)
    k = proj(3)
    k_ref[...] = k
    kb_ref[...] = k.astype(BF16)
    v = proj(4)
    v_ref[...] = v
    vb_ref[...] = v.astype(BF16)
    fl = _dot(hb, wf_ref[...]) + bf_ref[...]
    lf_ref[...] = _log_sigmoid(fl).reshape(nb, tt, LANES)[:, :, :N_HEADS]


def _inproj(x, mod, g_pre, w_main, w_f, b_f):
    n, t, d = x.shape
    nb, tt = _seq_blocks(n, t)
    blk = lambda w: pl.BlockSpec((nb, tt, w), lambda i, j: (i, j, 0))
    const = lambda shape: pl.BlockSpec(shape, lambda i, j: (0,) * len(shape))
    f32 = lambda w: jax.ShapeDtypeStruct((n, t, w), F32)
    b16 = lambda w: jax.ShapeDtypeStruct((n, t, w), BF16)
    return pl.pallas_call(
        _inproj_kernel,
        grid=(n // nb, t // tt),
        in_specs=[blk(d),
                  pl.BlockSpec((nb, 6, d), lambda i, j: (i, 0, 0)),
                  const((1, d)), const(w_main.shape), const(w_f.shape), const((1, LANES))],
        out_specs=[blk(WL), blk(WL), blk(WA), blk(WA), blk(WA), blk(WA), blk(WA), blk(N_HEADS)],
        out_shape=[f32(WL), f32(WL), b16(WA), b16(WA), b16(WA), f32(WA), f32(WA), f32(N_HEADS)],
        compiler_params=_params("parallel", "arbitrary"),
        name="inproj",
    )(x, mod, g_pre, w_main, w_f, b_f)


def _aug_lane(h):
    return HEAD_DIM if h % 2 == 0 else 0


def _inproj_prompt_kernel(x_ref, mod_ref, g_ref, w_ref, wf_ref, bf_ref, place_ref,
                          xl_ref, gy_ref, k_ref, v_ref, lf_ref, qt_ref, ka_ref, vt_ref, bpre_ref, carry_ref):
    _, tt, d = x_ref.shape

    @pl.when(pl.program_id(1) == 0)
    def _():
        carry_ref[...] = jnp.zeros_like(carry_ref)

    mod = mod_ref[0]
    hb = (_rms(x_ref[0], g_ref[...]) * (1.0 + mod[1:2, :]) + mod[0:1, :]).astype(BF16)

    def proj(col):
        return _dot(hb, w_ref[:, col * WL:(col + 1) * WL])

    xl_ref[0] = proj(0)
    gy_ref[0] = _gelu_tanh(proj(1))
    q = proj(2) * (HEAD_DIM ** -0.5 * LOG2E)
    k = proj(3)
    k_ref[0] = k
    v = proj(4)
    v_ref[0] = v
    fl = _dot(hb, wf_ref[...]) + bf_ref[...]
    lane = lax.broadcasted_iota(jnp.int32, (tt, LANES), 1)
    lf = jnp.where(lane < N_HEADS, _log_sigmoid(fl), 0.0)
    lf_ref[0] = lf[:, :N_HEADS]

    row = lax.broadcasted_iota(jnp.int32, (tt, tt), 0)
    col = lax.broadcasted_iota(jnp.int32, (tt, tt), 1)
    tril = (col <= row).astype(BF16)
    hi, mid, lo = _split3(lf)
    e = _dot(tril, hi) + _dot(tril, mid) + _dot(tril, lo)
    bpre_ref[0, 0] = carry_ref[...]
    carry_ref[...] += e[tt - 1:tt, :]
    e_hi, e_mid, e_lo = _split3(e * (-LOG2E))
    aug_k = _dot(e_hi, place_ref[0]) + _dot(e_mid, place_ref[1]) + _dot(e_lo, place_ref[2])

    for h in range(N_HEADS):
        pair = slice((h // 2) * PAIR_W, (h // 2 + 1) * PAIR_W)
        dims = (lane < HEAD_DIM) if h % 2 == 0 else (lane >= HEAD_DIM)
        a0 = _aug_lane(h)
        ones3 = ((lane >= a0) & (lane < a0 + 3)).astype(F32)
        qt_ref[0, h] = (jnp.where(dims, q[:, pair], 0.0) + ones3).T.astype(BF16)
        ka_ref[0, h] = (jnp.where(dims, k[:, pair], 0.0) + aug_k[:, h * LANES:(h + 1) * LANES]).astype(BF16)
        vt_ref[0, h] = jnp.where(dims, v[:, pair], 0.0).T.astype(BF16)


def _placement():
    pl_mat = np.zeros((3, LANES, N_HEADS * LANES), np.float32)
    for p in range(3):
        for h in range(N_HEADS):
            pl_mat[p, h, h * LANES + _aug_lane(h) + p] = 1.0
    return jnp.asarray(pl_mat, BF16)


def _inproj_prompt(x, mod, g_pre, w_main, w_f, b_f):
    n, t, d = x.shape
    tt = ROW_TILE
    assert t % tt == 0
    nblk = t // tt
    blk = lambda w: pl.BlockSpec((1, tt, w), lambda i, j: (i, j, 0))
    const = lambda shape: pl.BlockSpec(shape, lambda i, j: (0,) * len(shape))
    f32 = lambda w: jax.ShapeDtypeStruct((n, t, w), F32)
    place = _placement()
    return pl.pallas_call(
        _inproj_prompt_kernel,
        grid=(n, nblk),
        in_specs=[blk(d), pl.BlockSpec((1, 6, d), lambda i, j: (i, 0, 0)),
                  const((1, d)), const(w_main.shape), const(w_f.shape), const((1, LANES)), const(place.shape)],
        out_specs=[blk(WL), blk(WL), blk(WA), blk(WA), blk(N_HEADS),
                   pl.BlockSpec((1, N_HEADS, LANES, tt), lambda i, j: (i, 0, 0, j)),
                   pl.BlockSpec((1, N_HEADS, tt, LANES), lambda i, j: (i, 0, j, 0)),
                   pl.BlockSpec((1, N_HEADS, LANES, tt), lambda i, j: (i, 0, 0, j)),
                   pl.BlockSpec((1, 1, 1, LANES), lambda i, j: (i, j, 0, 0))],
        out_shape=[f32(WL), f32(WL), f32(WA), f32(WA), f32(N_HEADS),
                   jax.ShapeDtypeStruct((n, N_HEADS, LANES, t), BF16),
                   jax.ShapeDtypeStruct((n, N_HEADS, t, LANES), BF16),
                   jax.ShapeDtypeStruct((n, N_HEADS, LANES, t), BF16),
                   jax.ShapeDtypeStruct((n, nblk, 1, LANES), F32)],
        scratch_shapes=[pltpu.VMEM((1, LANES), F32)],
        compiler_params=_params("parallel", "arbitrary"),
        name="inproj_prompt",
    )(x, mod, g_pre, w_main, w_f, b_f, place)


def _expm1_neg(x):
    poly = x * (1.0 + x * (0.5 + x * (1.0 / 6.0 + x * (1.0 / 24.0 + x * (1.0 / 120.0)))))
    return jnp.where(x > -0.1, poly, jnp.exp(x) - 1.0)


def _lru_kernel(xl_ref, gy_ref, conv0_ref, h0_ref, cw_ref, cb_ref, wr_ref, br_ref, wi_ref, bi_ref,
                lam_ref, g_ref, out_ref, conv_ref, hlast_ref, tail_ref, carry_ref):
    nb, tt, w = xl_ref.shape
    j = pl.program_id(1)

    @pl.when(j == 0)
    def _():
        tail_ref[:, 8 - (CONV_W - 1):, :] = conv0_ref[...]
        carry_ref[...] = h0_ref[...]

    xl = xl_ref[...]
    xpad = jnp.concatenate([tail_ref[...], xl], axis=1)
    cw = cw_ref[...]
    xc = jnp.zeros_like(xl) + cb_ref[...]
    for k in range(CONV_W):
        off = 8 - (CONV_W - 1) + k
        xc = xc + xpad[:, off:off + tt, :] * cw[k:k + 1, :]
    conv_ref[...] = xpad[:, tt + 8 - (CONV_W - 1):, :]
    tail_ref[...] = xpad[:, tt:, :]

    m = nb * tt
    xf = xc.reshape(m, w)
    xb = xf.astype(BF16)
    r = _sigmoid(_dot(xb, wr_ref[...]) + br_ref[...])
    gi = _sigmoid(_dot(xb, wi_ref[...]) + bi_ref[...])
    lam = lam_ref[...]
    softplus = jnp.maximum(-lam, 0.0) + jnp.log1p(jnp.exp(-jnp.abs(lam)))
    log_a = (-LRU_C) * r * softplus
    a = jnp.exp(log_a)
    b = jnp.sqrt(-_expm1_neg(2.0 * log_a)) * (gi * xf)

    groups = m // 8
    a = a.reshape(groups, 8, w)
    b = b.reshape(groups, 8, w)
    sub = lax.broadcasted_iota(jnp.int32, (groups, 8, w), 1)
    for d in (1, 2, 4):
        keep = sub >= d
        a_prev = jnp.where(keep, pltpu.roll(a, d, 1), 1.0)
        b_prev = jnp.where(keep, pltpu.roll(b, d, 1), 0.0)
        b = a * b_prev + b
        a = a * a_prev
    carry = carry_ref[...]
    groups_per_seq = tt // 8
    rows = []
    for g in range(groups):
        if g % groups_per_seq == 0:
            prev = carry[g // groups_per_seq]
        h_g = a[g] * prev + b[g]
        prev = h_g[7:8]
        rows.append(h_g)
    h = jnp.concatenate(rows, axis=0).reshape(nb, tt, w)
    h_last = h[:, tt - 1:tt, :]
    carry_ref[...] = h_last
    hlast_ref[...] = h_last
    out_ref[...] = _rms(h * gy_ref[...], g_ref[...]).astype(BF16)


def _lru(xl, gy, conv0, h0, conv_w, conv_b, wr_bd, b_r, wi_bd, b_i, lam, g_lru):
    n, t, w = xl.shape
    nb, tt = _seq_blocks(n, t)
    blk = pl.BlockSpec((nb, tt, w), lambda i, j: (i, j, 0))
    per_seq = lambda rows: pl.BlockSpec((nb, rows, w), lambda i, j: (i, 0, 0))
    const = lambda shape: pl.BlockSpec(shape, lambda i, j: (0,) * len(shape))
    row = const((1, w))
    return pl.pallas_call(
        _lru_kernel,
        grid=(n // nb, t // tt),
        in_specs=[blk, blk, per_seq(CONV_W - 1), per_seq(1),
                  const((CONV_W, w)), row, const((w, w)), row, const((w, w)), row, row, row],
        out_specs=[blk, per_seq(CONV_W - 1), per_seq(1)],
        out_shape=[jax.ShapeDtypeStruct((n, t, w), BF16),
                   jax.ShapeDtypeStruct((n, CONV_W - 1, w), F32),
                   jax.ShapeDtypeStruct((n, 1, w), F32)],
        scratch_shapes=[pltpu.VMEM((nb, 8, w), F32), pltpu.VMEM((nb, 1, w), F32)],
        compiler_params=_params("parallel", "arbitrary"),
        name="rglru",
    )(xl, gy, conv0, h0, conv_w, conv_b, wr_bd, b_r, wi_bd, b_i, lam, g_lru)


def _cumsum_rows_kernel(x_ref, upper_ref, o_ref, carry_ref):
    tb = x_ref.shape[1]

    @pl.when(pl.program_id(0) == 0)
    def _():
        carry_ref[...] = jnp.zeros_like(carry_ref)

    upper = upper_ref[...]
    hi, mid, lo = _split3(x_ref[...])
    d = _dot(hi, upper) + _dot(mid, upper) + _dot(lo, upper) + carry_ref[...]
    carry_ref[...] = d[:, tb - 1:tb]
    o_ref[...] = d


def _cumsum_rows(x, tb):
    rows, t = x.shape
    upper = jnp.asarray(np.triu(np.ones((tb, tb), np.float32)), BF16)
    return pl.pallas_call(
        _cumsum_rows_kernel,
        grid=(t // tb,),
        in_specs=[pl.BlockSpec((rows, tb), lambda j: (0, j)), pl.BlockSpec((tb, tb), lambda j: (0, 0))],
        out_specs=pl.BlockSpec((rows, tb), lambda j: (0, j)),
        out_shape=jax.ShapeDtypeStruct((rows, t), F32),
        scratch_shapes=[pltpu.VMEM((rows, 1), F32)],
        compiler_params=_params("arbitrary"),
        name="logf_cumsum",
    )(x, upper)


def _online_update(s, m_prev, l_prev):
    m_new = jnp.maximum(m_prev, jnp.max(s, axis=1, keepdims=True))
    alpha = jnp.exp(m_prev - m_new)
    p = jnp.exp(s - m_new)
    l_new = alpha * l_prev + jnp.sum(p, axis=1, keepdims=True)
    return p, alpha, m_new, l_new


def _att_prompt_kernel(bpre_ref, qt_ref, ka_ref, vt_ref, o_ref, s_ref, p_ref, acc_ref):
    tq = qt_ref.shape[2]
    tk = ATT_TK
    nblk = ka_ref.shape[1] // tk
    b, hp, i = pl.program_id(0), pl.program_id(1), pl.program_id(2)
    q0 = i * tq
    jd = q0 // tk
    kpos = lax.broadcasted_iota(jnp.int32, (tk, tq), 0)
    qpos = lax.broadcasted_iota(jnp.int32, (tk, tq), 1)
    base = [((b * N_PAIRS + hp) * 2 + hh) * nblk for hh in range(2)]

    def scores(j, masked):
        start = pl.multiple_of(j * tk, tk)
        col_max = []
        for hh in range(2):
            s = _dot(ka_ref[hh, pl.ds(start, tk), :], qt_ref[hh])
            if masked:
                s = jnp.where(kpos + start <= qpos + q0, s, NEG_INF)
            s_ref[hh] = s
            col_max.append(jnp.max(s, axis=0, keepdims=True))
        return tuple(col_max)

    def softmax_pv(j, col_max, ml):
        start = pl.multiple_of(j * tk, tk)
        out = []
        for hh in range(2):
            m, l = ml[hh]
            c = (bpre_ref[base[hh] + jd] - bpre_ref[base[hh] + j]) * LOG2E
            m_new = jnp.maximum(m, col_max[hh] + c)
            alpha = jnp.exp2(m - m_new)
            shift = m_new - c
            part = jnp.zeros((8, tq), F32)
            for ch in range(tk // ATT_CHUNK):
                sl = slice(ch * ATT_CHUNK, (ch + 1) * ATT_CHUNK)
                p = jnp.exp2(s_ref[hh, sl, :] - shift)
                part = part + jnp.sum(p.reshape(ATT_CHUNK // 8, 8, tq), axis=0)
                p_ref[hh, sl, :] = p.astype(BF16)
            l_new = alpha * l + jnp.sum(part, axis=0, keepdims=True)
            out.append((m_new, l_new, alpha))
        return tuple(out), start

    def accumulate(mla, start):
        for hh in range(2):
            v_t = vt_ref[hh, hh * HEAD_DIM:(hh + 1) * HEAD_DIM, pl.ds(start, tk)]
            acc_ref[hh] = mla[hh][2] * acc_ref[hh] + _dot(v_t, p_ref[hh])
        return tuple((m, l) for m, l, _ in mla)

    def step(j, next_masked, carry):
        col_max, ml = carry
        mla, start = softmax_pv(j, col_max, ml)
        col_max_next = scores(j + 1, next_masked)
        return col_max_next, accumulate(mla, start)

    acc_ref[...] = jnp.zeros_like(acc_ref)
    init = (jnp.full((1, tq), NEG_INF, F32), jnp.zeros((1, tq), F32))
    n_masked = max(1, tq // tk)
    carry = (scores(0, True), (init, init))
    carry = lax.fori_loop(0, jd - 1, lambda j, cr: step(j, False, cr), carry)
    carry = lax.cond(jd > 0, lambda cr: step(jd - 1, True, cr), lambda cr: cr, carry)
    for extra in range(n_masked - 1):
        carry = step(jd + extra, True, carry)
    col_max, ml = carry
    mla, start = softmax_pv(jd + n_masked - 1, col_max, ml)
    (_, l_a), (_, l_b) = accumulate(mla, start)
    o_ref[0] = jnp.concatenate([acc_ref[0] / l_a, acc_ref[1] / l_b], axis=0).T


def _att_prompt(qt, ka, vt, bpre):
    n, _, _, t = qt.shape
    assert t % ATT_TQ == 0 and (ATT_TK % ATT_TQ == 0 or ATT_TQ % ATT_TK == 0) and ATT_TK == ROW_TILE
    pair = lambda q: (N_PAIRS, 2) + q.shape[2:]
    qt, ka, vt = (a.reshape((n,) + pair(a)) for a in (qt, ka, vt))
    bflat = jnp.transpose(bpre[:, :, 0, :N_HEADS], (0, 2, 1)).reshape(-1)
    return pl.pallas_call(
        _att_prompt_kernel,
        grid=(n, N_PAIRS, t // ATT_TQ),
        in_specs=[pl.BlockSpec(memory_space=pltpu.SMEM),
                  pl.BlockSpec((None, None, 2, LANES, ATT_TQ), lambda b, h, i: (b, h, 0, 0, i)),
                  pl.BlockSpec((None, None, 2, t, LANES), lambda b, h, i: (b, h, 0, 0, 0),
                               pipeline_mode=pl.Buffered(1)),
                  pl.BlockSpec((None, None, 2, LANES, t), lambda b, h, i: (b, h, 0, 0, 0),
                               pipeline_mode=pl.Buffered(1))],
        out_specs=pl.BlockSpec((1, ATT_TQ, PAIR_W), lambda b, h, i: (b, i, h)),
        out_shape=jax.ShapeDtypeStruct((n, t, WA), F32),
        scratch_shapes=[pltpu.VMEM((2, ATT_TK, ATT_TQ), F32), pltpu.VMEM((2, ATT_TK, ATT_TQ), BF16),
                        pltpu.VMEM((2, HEAD_DIM, ATT_TQ), F32)],
        compiler_params=_params("parallel", "parallel", "arbitrary"),
        name="att_prompt",
    )(bflat, qt, ka, vt)


def _att_sample_kernel(q_ref, kp_ref, vp_ref, dp_ref, dend_ref, kn_ref, vn_ref, dn_ref, o_ref,
                       m_ref, l_ref, acc_ref):
    t = q_ref.shape[1]
    j = pl.program_id(1)

    @pl.when(j == 0)
    def _():
        m_ref[...] = jnp.full_like(m_ref, NEG_INF)
        l_ref[...] = jnp.zeros_like(l_ref)
        acc_ref[...] = jnp.zeros_like(acc_ref)

    def head(h, k, v, time_minor, bias, mask):
        sl = slice(h * HEAD_DIM, (h + 1) * HEAD_DIM)
        q_h = q_ref[0, :, sl]
        s = (_dot(q_h, k) if time_minor else _dot_nt(q_h, k)) + bias[h:h + 1]
        if mask is not None:
            s = jnp.where(mask, s, NEG_INF)
        p, alpha, m_new, l_new = _online_update(s, m_ref[h], l_ref[h])
        p = p.astype(BF16)
        acc = alpha * acc_ref[:, sl] + (_dot_nt(p, v) if time_minor else _dot(p, v))
        m_ref[h], l_ref[h], acc_ref[:, sl] = m_new, l_new, acc
        return acc / l_new

    bias_past = dend_ref[0] - dp_ref[0]
    for h in range(N_HEADS):
        head(h, kp_ref[0, h].astype(BF16), vp_ref[0, h].astype(BF16), True, bias_past, None)

    @pl.when(j == pl.num_programs(1) - 1)
    def _():
        bias_new = -dn_ref[0]
        qpos = lax.broadcasted_iota(jnp.int32, (t, t), 0)
        kpos = lax.broadcasted_iota(jnp.int32, (t, t), 1)
        for h in range(N_HEADS):
            sl = slice(h * HEAD_DIM, (h + 1) * HEAD_DIM)
            o_ref[0, :, sl] = head(h, kn_ref[0, :, sl], vn_ref[0, :, sl], False, bias_new, kpos <= qpos)


def _att_sample(qb, k_past, v_past, d_past, kb_new, vb_new, d_new):
    n, t, _ = qb.shape
    past = k_past.shape[1]
    tk = min(SAMPLE_TK, past)
    assert past % tk == 0
    d_end = d_past[:, :, past - 1:]
    k_past = jnp.transpose(k_past, (0, 2, 3, 1))
    v_past = jnp.transpose(v_past, (0, 2, 3, 1))
    new = lambda dt: pl.BlockSpec((1, t, WA), lambda b, j: (b, 0, 0))
    cache = pl.BlockSpec((1, N_HEADS, HEAD_DIM, tk), lambda b, j: (b, 0, 0, j))
    return pl.pallas_call(
        _att_sample_kernel,
        grid=(n, past // tk),
        in_specs=[new(BF16), cache, cache,
                  pl.BlockSpec((1, N_HEADS, tk), lambda b, j: (b, 0, j)),
                  pl.BlockSpec((1, N_HEADS, 1), lambda b, j: (b, 0, 0)),
                  new(BF16), new(BF16),
                  pl.BlockSpec((1, N_HEADS, t), lambda b, j: (b, 0, 0))],
        out_specs=pl.BlockSpec((1, t, WA), lambda b, j: (b, 0, 0)),
        out_shape=jax.ShapeDtypeStruct((n, t, WA), F32),
        scratch_shapes=[pltpu.VMEM((N_HEADS, t, 1), F32), pltpu.VMEM((N_HEADS, t, 1), F32),
                        pltpu.VMEM((t, WA), F32)],
        compiler_params=_params("parallel", "arbitrary"),
        name="att_sample",
    )(qb, k_past, v_past, d_past, d_end, kb_new, vb_new, d_new)


def _first_index_of_max(x, axis):
    mx = jnp.max(x, axis=axis, keepdims=True)
    idx = lax.broadcasted_iota(jnp.int32, x.shape, axis)
    first = jnp.min(jnp.where(x == mx, idx, x.shape[axis]), axis=axis, keepdims=True)
    return mx, idx == first


def _route(s, bias):
    m = s.shape[1]
    sb = (s + bias).reshape(N_GROUPS, GROUP_SIZE, m)
    top1, is_top1 = _first_index_of_max(sb, 1)
    top2 = jnp.max(jnp.where(is_top1, NEG_INF, sb), axis=1, keepdims=True)
    grp = (top1 + top2).reshape(N_GROUPS, m)
    gi = lax.broadcasted_iota(jnp.int32, (N_GROUPS, N_GROUPS, m), 0)
    gj = lax.broadcasted_iota(jnp.int32, (N_GROUPS, N_GROUPS, m), 1)
    other, mine = grp[None, :, :], grp[:, None, :]
    beats = (other > mine) | ((other == mine) & (gj < gi))
    g_rank = jnp.sum(beats.astype(jnp.int32), axis=1)
    g_keep = (g_rank < TOPK_GROUPS)[:, None, :]
    cand = jnp.where(g_keep, sb, NEG_INF).reshape(N_EXPERTS, m)
    picks = []
    for _ in range(TOP_K):
        _, pick = _first_index_of_max(cand, 0)
        picks.append(pick)
        cand = jnp.where(pick, NEG_INF, cand)
    w = jnp.concatenate([jnp.sum(jnp.where(pk, s, 0.0), axis=0, keepdims=True) for pk in picks], axis=0)
    return picks, w / jnp.sum(w, axis=0, keepdims=True) * ROUTED_SCALE


HALF_MASK = 0xFFFF0000


def _pack_halves(x):
    c = x.shape[1] // 2
    lo = lax.bitcast_convert_type(x[:, :c].astype(BF16).astype(F32), jnp.uint32) >> jnp.uint32(16)
    hi = lax.bitcast_convert_type(x[:, c:].astype(BF16).astype(F32), jnp.uint32) & jnp.uint32(HALF_MASK)
    return lax.bitcast_convert_type(lo | hi, jnp.int32)


def _unpack_halves(words):
    w = lax.bitcast_convert_type(words, jnp.uint32)
    lo = lax.bitcast_convert_type(w << jnp.uint32(16), F32)
    hi = lax.bitcast_convert_type(w & jnp.uint32(HALF_MASK), F32)
    return lo, hi


def _outproj_kernel(x_ref, lru_ref, att_ref, mod_ref, gatt_ref, wtop_ref, wbot_ref, gpost_ref, gpre_ref,
                    rwh_ref, rwl_ref, rb_ref, cnt_in_ref,
                    x1_ref, hf_ref, xw_ref, ids_ref, ranks_ref, gates_ref, cnt_out_ref, carry_ref, *, split):
    nb, tt, d = x_ref.shape
    m = nb * tt
    first = (pl.program_id(0) == 0) & (pl.program_id(1) == 0)

    @pl.when(first)
    def _():
        carry_ref[...] = cnt_in_ref[...]

    if split is not None:
        @pl.when((pl.program_id(0) == split) & (pl.program_id(1) == 0))
        def _():
            carry_ref[...] = jnp.zeros_like(carry_ref)

    mod = mod_ref[...]
    att_n = _rms(att_ref[...], gatt_ref[...]).reshape(m, WA).astype(BF16)
    mix = _dot(lru_ref[...].reshape(m, WL), wtop_ref[...]) + _dot(att_n, wbot_ref[...])
    x1 = x_ref[...] + mod[:, 2:3, :] * _rms(mix, gpost_ref[...]).reshape(nb, tt, d)
    x1_ref[...] = x1
    hf = (_rms(x1, gpre_ref[...]) * (1.0 + mod[:, 4:5, :]) + mod[:, 3:4, :]).reshape(m, d)
    hf_hi = hf.astype(BF16)
    hf_ref[...] = hf_hi.reshape(nb, tt, d)
    hf_lo = (hf - hf_hi.astype(F32)).astype(BF16)
    rwh = rwh_ref[...]
    logits = _dot_nt(rwh, hf_hi) + _dot_nt(rwh, hf_lo) + _dot_nt(rwl_ref[...], hf_hi)
    picks, gates = _route(_sigmoid(logits), rb_ref[...])
    xw_ref[...] = _pack_halves(hf).reshape(nb, tt, d // 2)

    sel = jnp.zeros((N_EXPERTS, m), F32)
    for pk in picks:
        sel = sel + pk.astype(F32)
    before = (lax.broadcasted_iota(jnp.int32, (m, m), 0) < lax.broadcasted_iota(jnp.int32, (m, m), 1)).astype(BF16)
    prior = _dot(sel.astype(BF16), before) + carry_ref[...]
    expert = lax.broadcasted_iota(jnp.int32, (N_EXPERTS, m), 0).astype(F32)
    take = lambda pk, v: jnp.sum(jnp.where(pk, v, 0.0), axis=0, keepdims=True)
    ids_ref[...] = jnp.concatenate([take(pk, expert) for pk in picks], axis=0).astype(jnp.int32)
    ranks_ref[...] = jnp.concatenate([take(pk, prior) for pk in picks], axis=0).astype(jnp.int32)
    carry_ref[...] += jnp.sum(sel, axis=1, keepdims=True)
    cnt_out_ref[0] = carry_ref[...]
    gates = jnp.concatenate([gates, jnp.zeros((LANES - TOP_K, m), F32)], axis=0)
    gates_ref[...] = gates.T.reshape(nb, tt, LANES)


def _outproj(x, lru_n, att, mod, g_att, w_top, w_bot, g_post, g_pre, rw_hi, rw_lo, r_bias, cnt_in, split=None):
    n, t, d = x.shape
    nb, tt = _seq_blocks(n, t)
    m = nb * tt
    steps_t = t // tt
    n_groups = 1 if split is None else 2
    blk = lambda w: pl.BlockSpec((nb, tt, w), lambda i, j: (i, j, 0))
    const = lambda shape: pl.BlockSpec(shape, lambda i, j: (0,) * len(shape))
    per_tok = pl.BlockSpec((TOP_K, m), lambda i, j: (0, i * steps_t + j))
    group_of = (lambda i: 0) if split is None else (lambda i: jnp.where(i < split, 0, 1))
    return pl.pallas_call(
        functools.partial(_outproj_kernel, split=split),
        grid=(n // nb, steps_t),
        in_specs=[blk(d), blk(WL), blk(WA), pl.BlockSpec((nb, 6, d), lambda i, j: (i, 0, 0)),
                  const((1, WA)), const((WL, d)), const((WA, d)), const((1, d)), const((1, d)),
                  const((N_EXPERTS, d)), const((N_EXPERTS, d)), const((N_EXPERTS, 1)), const((N_EXPERTS, 1))],
        out_specs=[blk(d), blk(d), blk(d // 2), per_tok, per_tok, blk(LANES),
                   pl.BlockSpec((1, N_EXPERTS, 1), lambda i, j: (group_of(i), 0, 0))],
        out_shape=[jax.ShapeDtypeStruct((n, t, d), F32), jax.ShapeDtypeStruct((n, t, d), BF16),
                   jax.ShapeDtypeStruct((n, t, d // 2), jnp.int32),
                   jax.ShapeDtypeStruct((TOP_K, n * t), jnp.int32), jax.ShapeDtypeStruct((TOP_K, n * t), jnp.int32),
                   jax.ShapeDtypeStruct((n, t, LANES), F32), jax.ShapeDtypeStruct((n_groups, N_EXPERTS, 1), F32)],
        scratch_shapes=[pltpu.VMEM((N_EXPERTS, 1), F32)],
        compiler_params=_params("arbitrary", "arbitrary"),
        name="outproj_router",
    )(x, lru_n, att, mod, g_att, w_top, w_bot, g_post, g_pre, rw_hi, rw_lo, r_bias, cnt_in)


def _subcore_ranges(n_items):
    info = plsc.get_sparse_core_info()
    n_workers = info.num_cores * info.num_subcores
    per_worker = n_items // n_workers
    assert per_worker * n_workers == n_items and per_worker % GATHER_ROWS == 0
    return info, plsc.VectorSubcoreMesh(core_axis_name="c", subcore_axis_name="s"), per_worker


def _scatter_rows(xw, pos, n_slots, row_offset=0):
    m_tot, c = pos.shape[1], xw.shape[1]
    assert row_offset % GATHER_ROWS == 0
    info, mesh, per_worker = _subcore_ranges(m_tot)
    pos_flat = pos.reshape(-1)

    @functools.partial(
        pl.kernel, mesh=mesh, out_type=jax.ShapeDtypeStruct((n_slots, c), jnp.int32),
        scratch_types=[pltpu.VMEM((GATHER_ROWS,), jnp.int32), pltpu.VMEM((GATHER_ROWS, c), jnp.int32),
                       pltpu.SemaphoreType.DMA])
    def scatter(xw_hbm, pos_hbm, out_hbm, idx_v, rows_v, sem):
        worker = lax.axis_index("s") * info.num_cores + lax.axis_index("c")
        base = worker * per_worker

        @pl.loop(0, per_worker // GATHER_ROWS)
        def _(step):
            off = pl.multiple_of(base + step * GATHER_ROWS, GATHER_ROWS)
            pltpu.sync_copy(xw_hbm.at[pl.ds(row_offset + off, GATHER_ROWS)], rows_v)
            for r in range(TOP_K):
                pltpu.sync_copy(pos_hbm.at[pl.ds(pl.multiple_of(r * m_tot + off, GATHER_ROWS), GATHER_ROWS)], idx_v)
                pltpu.async_copy(rows_v, out_hbm.at[idx_v], sem).wait()

    return scatter(xw, pos_flat)


def _swiglu_halves(lo, hi, wg, wu, wd):
    c = lo.shape[1]
    hg = _dot(lo, wg[:c]) + _dot(hi, wg[c:])
    hu = _dot(lo, wu[:c]) + _dot(hi, wu[c:])
    return _dot((_silu(hg) * hu).astype(BF16), wd)


def _expert_kernel(te_ref, valid_ref, x_ref, wg_ref, wu_ref, wd_ref, y_ref, wg_bf, wu_bf, wd_bf):
    i = pl.program_id(0)
    valid = valid_ref[i]

    @pl.when((i == 0) | (te_ref[i] != te_ref[jnp.maximum(i - 1, 0)]))
    def _():
        wg_bf[...] = wg_ref[0].astype(BF16)
        wu_bf[...] = wu_ref[0].astype(BF16)
        wd_bf[...] = wd_ref[0].astype(BF16)

    @pl.when(valid > 0)
    def _():
        w = x_ref[...]
        row = lax.broadcasted_iota(jnp.int32, w.shape, 0)
        lo, hi = _unpack_halves(jnp.where(row < valid, w, 0))
        y_ref[...] = _pack_halves(_swiglu_halves(lo.astype(BF16), hi.astype(BF16), wg_bf[...], wu_bf[...], wd_bf[...]))


def _experts(xs, tile_expert, tile_valid, wg, wu, wd):
    n_slots, c = xs.shape
    n_tiles = n_slots // EXPERT_TILE
    d = 2 * c
    rows = pl.BlockSpec((EXPERT_TILE, c), lambda i, te, tv: (i, 0))
    weight = lambda shape: pl.BlockSpec((1,) + shape, lambda i, te, tv: (te[i], 0, 0))
    return pl.pallas_call(
        _expert_kernel,
        grid_spec=pltpu.PrefetchScalarGridSpec(
            num_scalar_prefetch=2, grid=(n_tiles,),
            in_specs=[rows, weight((d, D_EXPERT)), weight((d, D_EXPERT)), weight((D_EXPERT, d))],
            out_specs=rows,
            scratch_shapes=[pltpu.VMEM((d, D_EXPERT), BF16), pltpu.VMEM((d, D_EXPERT), BF16),
                            pltpu.VMEM((D_EXPERT, d), BF16)]),
        out_shape=jax.ShapeDtypeStruct((n_slots, c), jnp.int32),
        compiler_params=_params("arbitrary"),
        name="moe_experts",
    )(tile_expert, tile_valid, xs, wg, wu, wd)


def _gather_rows(table, idx):
    b, c = idx.shape[0], table.shape[1]
    info, mesh, per_worker = _subcore_ranges(b)

    n_steps = per_worker // GATHER_ROWS
    assert n_steps % 2 == 0

    @functools.partial(
        pl.kernel, mesh=mesh, out_type=jax.ShapeDtypeStruct((b, c), jnp.int32),
        scratch_types=[pltpu.VMEM((2, GATHER_ROWS), jnp.int32), pltpu.VMEM((2, GATHER_ROWS, c), jnp.int32),
                       pltpu.SemaphoreType.DMA((2,))])
    def gather(table_hbm, idx_hbm, out_hbm, idx_v, rows_v, sems):
        worker = lax.axis_index("s") * info.num_cores + lax.axis_index("c")
        base = worker * per_worker

        def chunk(step):
            return pl.ds(pl.multiple_of(base + step * GATHER_ROWS, GATHER_ROWS), GATHER_ROWS)

        def stream(buf):
            return pltpu.make_async_copy(table_hbm.at[idx_v.at[buf]], rows_v.at[buf], sems.at[buf])

        def start(step, buf):
            pltpu.sync_copy(idx_hbm.at[chunk(step)], idx_v.at[buf])
            stream(buf).start()

        def finish(step, buf):
            stream(buf).wait()
            pltpu.sync_copy(rows_v.at[buf], out_hbm.at[chunk(step)])

        start(0, 0)

        @pl.loop(0, n_steps, step=2)
        def _(step):
            start(step + 1, 1)
            finish(step, 0)

            @pl.when(step + 2 < n_steps)
            def _():
                start(step + 2, 0)

            finish(step + 1, 1)

    return gather(table, idx)


def _combine_kernel(rows_ref, gates_ref, hf_ref, x1_ref, mod_ref, sg_ref, su_ref, sd_ref, gpost_ref, y_ref):
    nb, tt, d = hf_ref.shape
    m = nb * tt
    c = d // 2
    x = hf_ref[...].reshape(m, d)
    shared = _swiglu_halves(x[:, :c], x[:, c:], sg_ref[...], su_ref[...], sd_ref[...])
    gates = gates_ref[...].reshape(m, LANES)
    acc_lo = shared[:, :c]
    acc_hi = shared[:, c:]
    for r in range(TOP_K):
        lo, hi = _unpack_halves(rows_ref[r])
        g = gates[:, r:r + 1]
        acc_lo = acc_lo + g * lo
        acc_hi = acc_hi + g * hi
    z = _rms(jnp.concatenate([acc_lo, acc_hi], axis=1), gpost_ref[...]).reshape(nb, tt, d)
    y_ref[...] = x1_ref[...] + mod_ref[...][:, 5:6, :] * z


def _combine(rows, first_tile, gates_t, hf, x1, mod, sg, su, sd, g_post, blocks=None, y_prev=None):
    n, t, d = hf.shape
    nb, tt = _seq_blocks(n, t)
    m = nb * tt
    steps_t = t // tt
    b0, nblocks = (0, n // nb) if blocks is None else blocks
    blk = lambda w: pl.BlockSpec((nb, tt, w), lambda i, j: (b0 + i, j, 0))
    const = lambda shape: pl.BlockSpec(shape, lambda i, j: (0,) * len(shape))
    in_specs = [pl.BlockSpec((TOP_K, m, d // 2), lambda i, j: (0, first_tile + i * steps_t + j, 0)),
                blk(LANES), blk(d), blk(d), pl.BlockSpec((nb, 6, d), lambda i, j: (b0 + i, 0, 0)),
                const((d, D_EXPERT)), const((d, D_EXPERT)), const((D_EXPERT, d)), const((1, d))]
    args = (rows, gates_t, hf, x1, mod, sg, su, sd, g_post)
    kernel_fn, aliases = _combine_kernel, {}
    if y_prev is not None:
        in_specs.append(pl.BlockSpec(memory_space=pl.ANY))
        args += (y_prev,)
        aliases = {len(args) - 1: 0}
        kernel_fn = lambda *refs: _combine_kernel(*refs[:len(args) - 1], refs[-1])
    return pl.pallas_call(
        kernel_fn,
        grid=(nblocks, steps_t),
        in_specs=in_specs,
        out_specs=blk(d),
        out_shape=jax.ShapeDtypeStruct((n, t, d), F32),
        input_output_aliases=aliases,
        compiler_params=_params("parallel", "parallel"),
        name="moe_combine",
    )(*args)


def _slots_kernel(starts_ref, ids_ref, ranks_ref, pos_ref):
    ids = ids_ref[...]

    def add_start(e, pos):
        return pos + jnp.where(ids == e, starts_ref[e], 0)

    pos_ref[...] = lax.fori_loop(0, N_EXPERTS, add_start, ranks_ref[...])


def _slots(starts, ids, ranks):
    k, m_tot = ids.shape
    cols = math.gcd(m_tot, SLOT_COLS)
    assert cols % LANES == 0
    blk = pl.BlockSpec((k, cols), lambda i: (0, i))
    return pl.pallas_call(
        _slots_kernel,
        grid=(m_tot // cols,),
        in_specs=[pl.BlockSpec(memory_space=pltpu.SMEM), blk, blk],
        out_specs=blk,
        out_shape=jax.ShapeDtypeStruct((k, m_tot), jnp.int32),
        compiler_params=_params("arbitrary"),
        name="moe_slots",
    )(starts, ids, ranks)


def _slot_plan(ids, ranks, counts):
    n_pairs = ids.shape[0] * ids.shape[1]
    n_tiles = -(-(n_pairs + N_EXPERTS * (EXPERT_TILE - 1)) // EXPERT_TILE)
    cnt = counts.reshape(N_EXPERTS).astype(jnp.int32)
    padded = (cnt + EXPERT_TILE - 1) // EXPERT_TILE * EXPERT_TILE
    ends = jnp.cumsum(padded)
    starts = ends - padded
    pos = _slots(starts, ids, ranks)
    tile_start = jnp.arange(n_tiles, dtype=jnp.int32) * EXPERT_TILE
    in_expert = (tile_start[:, None] >= starts[None, :]) & (tile_start[:, None] < ends[None, :])
    tile_expert = jnp.sum(jnp.where(in_expert, jnp.arange(N_EXPERTS, dtype=jnp.int32)[None, :], 0), axis=1)
    tile_fill = jnp.sum(jnp.where(in_expert, (starts + cnt)[None, :] - tile_start[:, None], 0), axis=1)
    tile_valid = jnp.clip(tile_fill, 0, EXPERT_TILE).astype(jnp.int32)
    return pos, tile_expert.astype(jnp.int32), tile_valid, n_tiles * EXPERT_TILE


def _block_diag(w):
    g, bw, _ = w.shape
    eye = jnp.eye(g, dtype=w.dtype)
    return (eye[:, None, :, None] * w[:, :, None, :]).reshape(g * bw, g * bw)


def _prep_weights(p):
    d_main = 2 * WL + 3 * WA
    w_in = p["w_in"]
    rw_t = p["router_w"].T
    rw_hi = rw_t.astype(BF16)
    row = lambda v: v.reshape(1, -1)
    return dict(
        w_mod=p["w_mod"], b_mod=p["b_mod"],
        g_pre_mix=row(p["g_pre_mix"]), g_post_mix=row(p["g_post_mix"]),
        g_pre_ffn=row(p["g_pre_ffn"]), g_post_ffn=row(p["g_post_ffn"]),
        w_main=w_in[:, :d_main].astype(BF16),
        w_f=jnp.pad(w_in[:, d_main:], ((0, 0), (0, LANES - N_HEADS))).astype(BF16),
        b_f=jnp.pad(p["b_f"], (0, LANES - N_HEADS)).reshape(1, LANES),
        conv_w=p["conv_w"], conv_b=row(p["conv_b"]),
        wr_bd=_block_diag(p["w_r"]).astype(BF16), b_r=row(p["b_r"]),
        wi_bd=_block_diag(p["w_i"]).astype(BF16), b_i=row(p["b_i"]),
        lam=row(p["lru_lambda"]), g_lru=row(p["g_lru_out"]), g_att=row(p["g_att_out"]),
        w_top=p["w_out"][:WL].astype(BF16), w_bot=p["w_out"][WL:].astype(BF16),
        rw_hi=rw_hi, rw_lo=(rw_t - rw_hi.astype(F32)).astype(BF16),
        r_bias=p["router_bias"].reshape(N_EXPERTS, 1),
        wg=p["w_gate"], wu=p["w_up"], wd=p["w_down"],
        sg=p["ws_gate"].astype(BF16), su=p["ws_up"].astype(BF16), sd=p["ws_down"].astype(BF16),
    )


def _mixers(x, mod, conv0, h0, past, w, cnt_in, split=None):
    n, t, _ = x.shape
    proj_args = (x, mod, w["g_pre_mix"], w["w_main"], w["w_f"], w["b_f"])
    if past is None:
        xl, gy, k, v, lf, qt, ka, vt, bpre = _inproj_prompt(*proj_args)
        att = _att_prompt(qt, ka, vt, bpre)
    else:
        xl, gy, qb, kb, vb, k, v, lf = _inproj(*proj_args)
        k_past, v_past, lf_past = past
        plen = k_past.shape[1]
        by_head = lambda a: jnp.transpose(a, (0, 2, 1)).reshape(n * N_HEADS, a.shape[1])
        d_new = _cumsum_rows(by_head(lf), t).reshape(n, N_HEADS, t)
        d_past = _cumsum_rows(by_head(lf_past), min(CUMSUM_COLS, plen)).reshape(n, N_HEADS, plen)
        att = _att_sample(qb, k_past, v_past, d_past, kb, vb, d_new)
    lru_n, conv_new, h_new = _lru(xl, gy, conv0, h0.reshape(n, 1, WL), w["conv_w"], w["conv_b"],
                                  w["wr_bd"], w["b_r"], w["wi_bd"], w["b_i"], w["lam"], w["g_lru"])
    routed = _outproj(x, lru_n, att, mod, w["g_att"], w["w_top"], w["w_bot"], w["g_post_mix"],
                      w["g_pre_ffn"], w["rw_hi"], w["rw_lo"], w["r_bias"], cnt_in, split)
    state = (k.reshape(n, t, N_HEADS, HEAD_DIM), v.reshape(n, t, N_HEADS, HEAD_DIM), lf,
             conv_new, h_new.reshape(n, WL))
    return routed, state


def _layer(xp, xs, mod_p, mod_s, conv_s, h_s, past_s, w):
    n_p = xp.shape[0]
    conv0 = jnp.zeros((n_p, CONV_W - 1, WL), F32)
    h0 = jnp.zeros((n_p, WL), F32)
    zero_cnt = jnp.zeros((N_EXPERTS, 1), F32)
    t_p = xp.shape[1]
    split = 1 if n_p > 1 else None
    (x1_p, hf_p, xw_p, ids_p, rk_p, g_p, cnt_p), st_p = _mixers(xp, mod_p, conv0, h0, None, w, zero_cnt, split)
    cnt_tail = cnt_p[1] if n_p > 1 else zero_cnt
    (x1_s, hf_s, xw_s, ids_s, rk_s, g_s, cnt_s), st_s = _mixers(xs, mod_s, conv_s, h_s, past_s, w, cnt_tail)

    half = xw_p.shape[-1]
    xw_p = xw_p.reshape(-1, half)

    def routed(xw, row_offset, ids, ranks, counts):
        pos, tile_expert, tile_valid, n_slots = _slot_plan(ids, ranks, counts)
        ys = _experts(_scatter_rows(xw, pos, n_slots, row_offset), tile_expert, tile_valid,
                      w["wg"], w["wu"], w["wd"])
        return _gather_rows(ys, pos.reshape(-1)).reshape(TOP_K, ids.shape[1], half)

    rows_a = routed(xw_p, 0, ids_p[:, :t_p], rk_p[:, :t_p], cnt_p[0])
    xw_b = jnp.concatenate([xw_p[t_p:], xw_s.reshape(-1, half)], axis=0)
    rows_b = routed(xw_b, 0, jnp.concatenate([ids_p[:, t_p:], ids_s], axis=1),
                    jnp.concatenate([rk_p[:, t_p:], rk_s], axis=1), cnt_s[0])
    shared = (w["sg"], w["su"], w["sd"], w["g_post_ffn"])
    yp = _combine(rows_a, 0, g_p, hf_p, x1_p, mod_p, *shared, blocks=(0, 1))
    if n_p > 1:
        yp = _combine(rows_b, 0, g_p, hf_p, x1_p, mod_p, *shared, blocks=(1, n_p - 1), y_prev=yp)
    ysmp = _combine(rows_b, (n_p - 1) * t_p // ROW_TILE, g_s, hf_s, x1_s, mod_s, *shared)
    return yp, ysmp, st_p, st_s


def kernel(x_prompt, x_sample, c_prompt, c_sample, cache_k, cache_v, cache_logf, state_conv, state_lru, w_mod, b_mod, g_pre_mix, g_post_mix, g_pre_ffn, g_post_ffn, w_in, conv_w, conv_b, w_r, b_r, w_i, b_i, lru_lambda, b_f, g_lru_out, g_att_out, w_out, router_w, router_bias, w_gate, w_up, w_down, ws_gate, ws_up, ws_down):
    names = ("w_mod", "b_mod", "g_pre_mix", "g_post_mix", "g_pre_ffn", "g_post_ffn", "w_in", "conv_w", "conv_b",
             "w_r", "b_r", "w_i", "b_i", "lru_lambda", "b_f", "g_lru_out", "g_att_out", "w_out", "router_w",
             "router_bias", "w_gate", "w_up", "w_down", "ws_gate", "ws_up", "ws_down")
    stacked = (w_mod, b_mod, g_pre_mix, g_post_mix, g_pre_ffn, g_post_ffn, w_in, conv_w, conv_b, w_r, b_r, w_i, b_i,
               lru_lambda, b_f, g_lru_out, g_att_out, w_out, router_w, router_bias, w_gate, w_up, w_down,
               ws_gate, ws_up, ws_down)
    depth = w_mod.shape[0]
    n_p, n_s = x_prompt.shape[0], x_sample.shape[0]
    yp, ys = x_prompt, x_sample
    st_p, st_s = [], []
    for l in range(depth):
        w = _prep_weights({k: v[l] for k, v in zip(names, stacked)})
        mod = _modulation(jnp.concatenate([c_prompt, c_sample], axis=0), w["w_mod"], w["b_mod"])
        mod = mod.reshape(n_p + n_s, 6, D_MODEL)
        yp, ys, sp, ss = _layer(yp, ys, mod[:n_p], mod[n_p:], state_conv[l], state_lru[l],
                                (cache_k[l], cache_v[l], cache_logf[l]), w)
        st_p.append(sp)
        st_s.append(ss)
    stack = lambda sts, i: jnp.stack([s[i] for s in sts])
    return (yp, ys) + tuple(stack(st_p, i) for i in range(5)) + tuple(stack(st_s, i) for i in range(5))
```

```python
import functools
import math

import jax
import jax.numpy as jnp
import numpy as np
from jax import lax
from jax.experimental import pallas as pl
from jax.experimental.pallas import tpu as pltpu
from jax.experimental.pallas import tpu_sc as plsc

F32 = jnp.float32
BF16 = jnp.bfloat16

D_MODEL = 1024
WL = 512
WA = 512
N_HEADS = 8
HEAD_DIM = 64
N_PAIRS = N_HEADS // 2
PAIR_W = 2 * HEAD_DIM
LANES = 128
CONV_W = 4
LRU_BLOCKS = 8
LRU_C = 8.0
N_EXPERTS = 64
N_GROUPS = 8
GROUP_SIZE = N_EXPERTS // N_GROUPS
TOPK_GROUPS = 4
TOP_K = 8
D_EXPERT = 256
ROUTED_SCALE = 2.5
EPS = 1e-6
NEG_INF = float("-inf")
LOG2E = 1.4426950408889634

ROW_TILE = 512
ATT_TQ = 512
ATT_TK = 512
ATT_CHUNK = 64
SAMPLE_TK = 4096
CUMSUM_COLS = 1024
EXPERT_TILE = 512
SLOT_COLS = 2048
GATHER_ROWS = 64
VMEM_LIMIT = 56 * 1024 * 1024


def _params(*sem):
    return pltpu.CompilerParams(dimension_semantics=sem, vmem_limit_bytes=VMEM_LIMIT)


def _dot(a, b):
    return jnp.dot(a, b, preferred_element_type=F32)


def _dot_nt(a, b):
    return lax.dot_general(a, b, (((1,), (1,)), ((), ())), preferred_element_type=F32)


def _split3(x):
    hi = x.astype(BF16)
    r1 = x - hi.astype(F32)
    mid = r1.astype(BF16)
    lo = (r1 - mid.astype(F32)).astype(BF16)
    return hi, mid, lo


def _rms(x, g):
    return x * lax.rsqrt(jnp.mean(x * x, axis=-1, keepdims=True) + EPS) * g


def _sigmoid(x):
    return 1.0 / (1.0 + jnp.exp(-x))


def _silu(x):
    return x * _sigmoid(x)


def _gelu_tanh(x):
    return 0.5 * x * (1.0 + jnp.tanh(0.7978845608028654 * (x + 0.044715 * (x * x * x))))


def _log_sigmoid(x):
    return jnp.minimum(x, 0.0) - jnp.log1p(jnp.exp(-jnp.abs(x)))


def _seq_blocks(n, t):
    if t >= ROW_TILE:
        assert t % ROW_TILE == 0
        return 1, ROW_TILE
    nb = ROW_TILE // t
    assert nb * t == ROW_TILE and n % nb == 0
    return nb, t


def _mod_kernel(c_ref, w_ref, b_ref, o_ref):
    c = _silu(c_ref[...])
    c_hi = c.astype(BF16)
    c_lo = (c - c_hi.astype(F32)).astype(BF16)
    w = w_ref[...]
    w_hi = w.astype(BF16)
    w_lo = (w - w_hi.astype(F32)).astype(BF16)
    o_ref[...] = _dot(c_hi, w_hi) + _dot(c_lo, w_hi) + _dot(c_hi, w_lo) + b_ref[...]


def _modulation(c, w_mod, b_mod):
    rows = c.shape[0]
    n = -(-rows // 8) * 8
    c = jnp.pad(c, ((0, n - rows), (0, 0)))
    d6 = w_mod.shape[1]
    return pl.pallas_call(
        _mod_kernel,
        grid=(d6 // D_MODEL,),
        in_specs=[pl.BlockSpec((n, D_MODEL), lambda j: (0, 0)),
                  pl.BlockSpec((D_MODEL, D_MODEL), lambda j: (0, j)),
                  pl.BlockSpec((1, D_MODEL), lambda j: (0, j))],
        out_specs=pl.BlockSpec((n, D_MODEL), lambda j: (0, j)),
        out_shape=jax.ShapeDtypeStruct((n, d6), F32),
        compiler_params=_params("arbitrary"),
        name="modulation",
    )(c, w_mod, b_mod.reshape(1, d6))[:rows]


def _inproj_kernel(x_ref, mod_ref, g_ref, w_ref, wf_ref, bf_ref,
                   xl_ref, gy_ref, qb_ref, kb_ref, vb_ref, k_ref, v_ref, lf_ref):
    nb, tt, d = x_ref.shape
    x = x_ref[...]
    mod = mod_ref[...]
    hn = _rms(x, g_ref[...]) * (1.0 + mod[:, 1:2, :]) + mod[:, 0:1, :]
    hb = hn.reshape(nb * tt, d).astype(BF16)

    def proj(col):
        return _dot(hb, w_ref[:, col * WL:(col + 1) * WL]).reshape(nb, tt, WL)

    xl_ref[...] = proj(0)
    gy_ref[...] = _gelu_tanh(proj(1)).astype(BF16)
    qb_ref[...] = (proj(2) * (HEAD_DIM ** -0.5)).astype(BF16)
    k = proj(3)
    k_ref[...] = k
    kb_ref[...] = k.astype(BF16)
    v = proj(4)
    v_ref[...] = v
    vb_ref[...] = v.astype(BF16)
    fl = _dot(hb, wf_ref[...]) + bf_ref[...]
    lf_ref[...] = _log_sigmoid(fl).reshape(nb, tt, LANES)[:, :, :N_HEADS]


def _inproj(x, mod, g_pre, w_main, w_f, b_f):
    n, t, d = x.shape
    nb, tt = _seq_blocks(n, t)
    blk = lambda w: pl.BlockSpec((nb, tt, w), lambda i, j: (i, j, 0))
    const = lambda shape: pl.BlockSpec(shape, lambda i, j: (0,) * len(shape))
    f32 = lambda w: jax.ShapeDtypeStruct((n, t, w), F32)
    b16 = lambda w: jax.ShapeDtypeStruct((n, t, w), BF16)
    return pl.pallas_call(
        _inproj_kernel,
        grid=(n // nb, t // tt),
        in_specs=[blk(d),
                  pl.BlockSpec((nb, 6, d), lambda i, j: (i, 0, 0)),
                  const((1, d)), const(w_main.shape), const(w_f.shape), const((1, LANES))],
        out_specs=[blk(WL), blk(WL), blk(WA), blk(WA), blk(WA), blk(WA), blk(WA), blk(N_HEADS)],
        out_shape=[f32(WL), b16(WL), b16(WA), b16(WA), b16(WA), f32(WA), f32(WA), f32(N_HEADS)],
        compiler_params=_params("parallel", "arbitrary"),
        name="inproj",
    )(x, mod, g_pre, w_main, w_f, b_f)


def _aug_lane(h):
    return HEAD_DIM if h % 2 == 0 else 0


def _inproj_prompt_kernel(x_ref, mod_ref, g_ref, w_ref, wf_ref, bf_ref, place_ref,
                          xl_ref, gy_ref, k_ref, v_ref, lf_ref, qt_ref, ka_ref, vt_ref, bpre_ref, carry_ref):
    _, tt, d = x_ref.shape

    @pl.when(pl.program_id(1) == 0)
    def _():
        carry_ref[...] = jnp.zeros_like(carry_ref)

    mod = mod_ref[0]
    hb = (_rms(x_ref[0], g_ref[...]) * (1.0 + mod[1:2, :]) + mod[0:1, :]).astype(BF16)

    def proj(col):
        return _dot(hb, w_ref[:, col * WL:(col + 1) * WL])

    xl_ref[0] = proj(0)
    gy_ref[0] = _gelu_tanh(proj(1)).astype(BF16)
    q = proj(2) * (HEAD_DIM ** -0.5 * LOG2E)
    k = proj(3)
    k_ref[0] = k
    v = proj(4)
    v_ref[0] = v
    fl = _dot(hb, wf_ref[...]) + bf_ref[...]
    lane = lax.broadcasted_iota(jnp.int32, (tt, LANES), 1)
    lf = jnp.where(lane < N_HEADS, _log_sigmoid(fl), 0.0)
    lf_ref[0] = lf[:, :N_HEADS]

    row = lax.broadcasted_iota(jnp.int32, (tt, tt), 0)
    col = lax.broadcasted_iota(jnp.int32, (tt, tt), 1)
    tril = (col <= row).astype(BF16)
    hi, mid, lo = _split3(lf)
    e = _dot(tril, hi) + _dot(tril, mid) + _dot(tril, lo)
    bpre_ref[0, 0] = carry_ref[...]
    carry_ref[...] += e[tt - 1:tt, :]
    e_hi, e_mid, e_lo = _split3(e * (-LOG2E))
    aug_k = _dot(jnp.concatenate([e_hi, e_mid, e_lo], axis=1), place_ref[...])

    for h in range(N_HEADS):
        pair = slice((h // 2) * PAIR_W, (h // 2 + 1) * PAIR_W)
        dims = (lane < HEAD_DIM) if h % 2 == 0 else (lane >= HEAD_DIM)
        a0 = _aug_lane(h)
        ones3 = ((lane >= a0) & (lane < a0 + 3)).astype(F32)
        qt_ref[0, h] = (jnp.where(dims, q[:, pair], 0.0) + ones3).T.astype(BF16)
        ka_ref[0, h] = (jnp.where(dims, k[:, pair], 0.0) + aug_k[:, h * LANES:(h + 1) * LANES]).astype(BF16)
        one1 = (lane == a0).astype(F32)
        vt_ref[0, h] = (jnp.where(dims, v[:, pair], 0.0) + one1).T.astype(BF16)


def _placement():
    pl_mat = np.zeros((3 * LANES, N_HEADS * LANES), np.float32)
    for p in range(3):
        for h in range(N_HEADS):
            pl_mat[p * LANES + h, h * LANES + _aug_lane(h) + p] = 1.0
    return jnp.asarray(pl_mat, BF16)


def _inproj_prompt(x, mod, g_pre, w_main, w_f, b_f):
    n, t, d = x.shape
    tt = ROW_TILE
    assert t % tt == 0
    nblk = t // tt
    blk = lambda w: pl.BlockSpec((1, tt, w), lambda i, j: (i, j, 0))
    const = lambda shape: pl.BlockSpec(shape, lambda i, j: (0,) * len(shape))
    f32 = lambda w: jax.ShapeDtypeStruct((n, t, w), F32)
    place = _placement()
    return pl.pallas_call(
        _inproj_prompt_kernel,
        grid=(n, nblk),
        in_specs=[blk(d), pl.BlockSpec((1, 6, d), lambda i, j: (i, 0, 0)),
                  const((1, d)), const(w_main.shape), const(w_f.shape), const((1, LANES)), const(place.shape)],
        out_specs=[blk(WL), blk(WL), blk(WA), blk(WA), blk(N_HEADS),
                   pl.BlockSpec((1, N_HEADS, LANES, tt), lambda i, j: (i, 0, 0, j)),
                   pl.BlockSpec((1, N_HEADS, tt, LANES), lambda i, j: (i, 0, j, 0)),
                   pl.BlockSpec((1, N_HEADS, LANES, tt), lambda i, j: (i, 0, 0, j)),
                   pl.BlockSpec((1, 1, 1, LANES), lambda i, j: (i, j, 0, 0))],
        out_shape=[f32(WL), jax.ShapeDtypeStruct((n, t, WL), BF16), f32(WA), f32(WA), f32(N_HEADS),
                   jax.ShapeDtypeStruct((n, N_HEADS, LANES, t), BF16),
                   jax.ShapeDtypeStruct((n, N_HEADS, t, LANES), BF16),
                   jax.ShapeDtypeStruct((n, N_HEADS, LANES, t), BF16),
                   jax.ShapeDtypeStruct((n, nblk, 1, LANES), F32)],
        scratch_shapes=[pltpu.VMEM((1, LANES), F32)],
        compiler_params=_params("parallel", "arbitrary"),
        name="inproj_prompt",
    )(x, mod, g_pre, w_main, w_f, b_f, place)


def _expm1_neg(x):
    poly = x * (1.0 + x * (0.5 + x * (1.0 / 6.0 + x * (1.0 / 24.0 + x * (1.0 / 120.0)))))
    return jnp.where(x > -0.1, poly, jnp.exp(x) - 1.0)


def _lru_kernel(xl_ref, gy_ref, conv0_ref, h0_ref, cw_ref, cb_ref, wr_ref, br_ref, wi_ref, bi_ref,
                lam_ref, g_ref, out_ref, conv_ref, hlast_ref, tail_ref, carry_ref):
    nb, tt, w = xl_ref.shape
    j = pl.program_id(1)

    @pl.when(j == 0)
    def _():
        tail_ref[:, 8 - (CONV_W - 1):, :] = conv0_ref[...]
        carry_ref[...] = h0_ref[...]

    xl = xl_ref[...]
    xpad = jnp.concatenate([tail_ref[...], xl], axis=1)
    cw = cw_ref[...]
    xc = jnp.zeros_like(xl) + cb_ref[...]
    for k in range(CONV_W):
        off = 8 - (CONV_W - 1) + k
        xc = xc + xpad[:, off:off + tt, :] * cw[k:k + 1, :]
    conv_ref[...] = xpad[:, tt + 8 - (CONV_W - 1):, :]
    tail_ref[...] = xpad[:, tt:, :]

    m = nb * tt
    xf = xc.reshape(m, w)
    xb = xf.astype(BF16)
    r = _sigmoid(_dot(xb, wr_ref[...]) + br_ref[...])
    gi = _sigmoid(_dot(xb, wi_ref[...]) + bi_ref[...])
    lam = lam_ref[...]
    softplus = jnp.maximum(-lam, 0.0) + jnp.log1p(jnp.exp(-jnp.abs(lam)))
    log_a = (-LRU_C) * r * softplus
    a = jnp.exp(log_a)
    b = jnp.sqrt(-_expm1_neg(2.0 * log_a)) * (gi * xf)

    groups = m // 8
    a = a.reshape(groups, 8, w)
    b = b.reshape(groups, 8, w)
    sub = lax.broadcasted_iota(jnp.int32, (groups, 8, w), 1)
    for d in (1, 2, 4):
        keep = sub >= d
        a_prev = jnp.where(keep, pltpu.roll(a, d, 1), 1.0)
        b_prev = jnp.where(keep, pltpu.roll(b, d, 1), 0.0)
        b = a * b_prev + b
        a = a * a_prev
    carry = carry_ref[...]
    groups_per_seq = tt // 8
    rows = []
    for g in range(groups):
        if g % groups_per_seq == 0:
            prev = carry[g // groups_per_seq]
        h_g = a[g] * prev + b[g]
        prev = h_g[7:8]
        rows.append(h_g)
    h = jnp.concatenate(rows, axis=0).reshape(nb, tt, w)
    h_last = h[:, tt - 1:tt, :]
    carry_ref[...] = h_last
    hlast_ref[...] = h_last
    out_ref[...] = _rms(h * gy_ref[...].astype(F32), g_ref[...]).astype(BF16)


def _lru(xl, gy, conv0, h0, conv_w, conv_b, wr_bd, b_r, wi_bd, b_i, lam, g_lru):
    n, t, w = xl.shape
    nb, tt = _seq_blocks(n, t)
    blk = pl.BlockSpec((nb, tt, w), lambda i, j: (i, j, 0))
    per_seq = lambda rows: pl.BlockSpec((nb, rows, w), lambda i, j: (i, 0, 0))
    const = lambda shape: pl.BlockSpec(shape, lambda i, j: (0,) * len(shape))
    row = const((1, w))
    return pl.pallas_call(
        _lru_kernel,
        grid=(n // nb, t // tt),
        in_specs=[blk, blk, per_seq(CONV_W - 1), per_seq(1),
                  const((CONV_W, w)), row, const((w, w)), row, const((w, w)), row, row, row],
        out_specs=[blk, per_seq(CONV_W - 1), per_seq(1)],
        out_shape=[jax.ShapeDtypeStruct((n, t, w), BF16),
                   jax.ShapeDtypeStruct((n, CONV_W - 1, w), F32),
                   jax.ShapeDtypeStruct((n, 1, w), F32)],
        scratch_shapes=[pltpu.VMEM((nb, 8, w), F32), pltpu.VMEM((nb, 1, w), F32)],
        compiler_params=_params("parallel", "arbitrary"),
        name="rglru",
    )(xl, gy, conv0, h0, conv_w, conv_b, wr_bd, b_r, wi_bd, b_i, lam, g_lru)


def _cumsum_rows_kernel(x_ref, upper_ref, o_ref, carry_ref):
    tb = x_ref.shape[1]

    @pl.when(pl.program_id(0) == 0)
    def _():
        carry_ref[...] = jnp.zeros_like(carry_ref)

    upper = upper_ref[...]
    hi, mid, lo = _split3(x_ref[...])
    d = _dot(hi, upper) + _dot(mid, upper) + _dot(lo, upper) + carry_ref[...]
    carry_ref[...] = d[:, tb - 1:tb]
    o_ref[...] = d


def _cumsum_rows(x, tb):
    rows, t = x.shape
    upper = jnp.asarray(np.triu(np.ones((tb, tb), np.float32)), BF16)
    return pl.pallas_call(
        _cumsum_rows_kernel,
        grid=(t // tb,),
        in_specs=[pl.BlockSpec((rows, tb), lambda j: (0, j)), pl.BlockSpec((tb, tb), lambda j: (0, 0))],
        out_specs=pl.BlockSpec((rows, tb), lambda j: (0, j)),
        out_shape=jax.ShapeDtypeStruct((rows, t), F32),
        scratch_shapes=[pltpu.VMEM((rows, 1), F32)],
        compiler_params=_params("arbitrary"),
        name="logf_cumsum",
    )(x, upper)


def _online_update(s, m_prev, l_prev):
    m_new = jnp.maximum(m_prev, jnp.max(s, axis=1, keepdims=True))
    alpha = jnp.exp(m_prev - m_new)
    p = jnp.exp(s - m_new)
    l_new = alpha * l_prev + jnp.sum(p, axis=1, keepdims=True)
    return p, alpha, m_new, l_new


def _att_prompt_kernel(bpre_ref, qt_ref, ka_ref, vt_ref, o_ref, s_ref, p_ref, acc_ref):
    tq = qt_ref.shape[2]
    tk = ATT_TK
    nblk = ka_ref.shape[1] // tk
    b, hp, i = pl.program_id(0), pl.program_id(1), pl.program_id(2)
    q0 = i * tq
    jd = q0 // tk
    kpos = lax.broadcasted_iota(jnp.int32, (tk, tq), 0)
    qpos = lax.broadcasted_iota(jnp.int32, (tk, tq), 1)
    rows = lax.broadcasted_iota(jnp.int32, (LANES, tq), 0)
    base = [((b * N_PAIRS + hp) * 2 + hh) * nblk for hh in range(2)]

    def scores(j, masked):
        start = pl.multiple_of(j * tk, tk)
        col_max = []
        for hh in range(2):
            s = _dot(ka_ref[hh, pl.ds(start, tk), :], qt_ref[hh])
            if masked:
                s = jnp.where(kpos + start <= qpos + q0, s, NEG_INF)
            s_ref[hh] = s
            col_max.append(jnp.max(s, axis=0, keepdims=True))
        return tuple(col_max)

    def softmax_pv(j, col_max, m):
        start = pl.multiple_of(j * tk, tk)
        m_out = []
        for hh in range(2):
            c = (bpre_ref[base[hh] + jd] - bpre_ref[base[hh] + j]) * LOG2E
            m_new = jnp.maximum(m[hh], col_max[hh] + c)
            alpha = jnp.exp2(m[hh] - m_new)
            shift = m_new - c
            for ch in range(tk // ATT_CHUNK):
                sl = slice(ch * ATT_CHUNK, (ch + 1) * ATT_CHUNK)
                p_ref[hh, sl, :] = jnp.exp2(s_ref[hh, sl, :] - shift).astype(BF16)
            m_out.append((m_new, alpha))
        return tuple(m_out), start

    def accumulate(m_alpha, start):
        for hh in range(2):
            pv = _dot(vt_ref[hh, :, pl.ds(start, tk)], p_ref[hh])
            acc_ref[hh] = m_alpha[hh][1] * acc_ref[hh] + pv
        return tuple(ma[0] for ma in m_alpha)

    def step(j, next_masked, carry):
        col_max, m = carry
        m_alpha, start = softmax_pv(j, col_max, m)
        col_max_next = scores(j + 1, next_masked)
        return col_max_next, accumulate(m_alpha, start)

    acc_ref[...] = jnp.zeros_like(acc_ref)
    neg = jnp.full((1, tq), NEG_INF, F32)
    n_masked = max(1, tq // tk)
    carry = (scores(0, True), (neg, neg))
    carry = lax.fori_loop(0, jd - 1, lambda j, cr: step(j, False, cr), carry)
    carry = lax.cond(jd > 0, lambda cr: step(jd - 1, True, cr), lambda cr: cr, carry)
    for extra in range(n_masked - 1):
        carry = step(jd + extra, True, carry)
    col_max, m = carry
    m_alpha, start = softmax_pv(jd + n_masked - 1, col_max, m)
    accumulate(m_alpha, start)

    acc_a, acc_b = acc_ref[0], acc_ref[1]
    out_a = acc_a / acc_a[_aug_lane(0):_aug_lane(0) + 1, :]
    out_b = acc_b / acc_b[_aug_lane(1):_aug_lane(1) + 1, :]
    o_ref[0] = jnp.where(rows < HEAD_DIM, out_a, out_b).T


def _att_prompt(qt, ka, vt, bpre):
    n, _, _, t = qt.shape
    assert t % ATT_TQ == 0 and (ATT_TK % ATT_TQ == 0 or ATT_TQ % ATT_TK == 0) and ATT_TK == ROW_TILE
    pair = lambda q: (N_PAIRS, 2) + q.shape[2:]
    qt, ka, vt = (a.reshape((n,) + pair(a)) for a in (qt, ka, vt))
    bflat = jnp.transpose(bpre[:, :, 0, :N_HEADS], (0, 2, 1)).reshape(-1)
    return pl.pallas_call(
        _att_prompt_kernel,
        grid=(n, N_PAIRS, t // ATT_TQ),
        in_specs=[pl.BlockSpec(memory_space=pltpu.SMEM),
                  pl.BlockSpec((None, None, 2, LANES, ATT_TQ), lambda b, h, i: (b, h, 0, 0, i)),
                  pl.BlockSpec((None, None, 2, t, LANES), lambda b, h, i: (b, h, 0, 0, 0),
                               pipeline_mode=pl.Buffered(1)),
                  pl.BlockSpec((None, None, 2, LANES, t), lambda b, h, i: (b, h, 0, 0, 0),
                               pipeline_mode=pl.Buffered(1))],
        out_specs=pl.BlockSpec((1, ATT_TQ, PAIR_W), lambda b, h, i: (b, i, h)),
        out_shape=jax.ShapeDtypeStruct((n, t, WA), F32),
        scratch_shapes=[pltpu.VMEM((2, ATT_TK, ATT_TQ), F32), pltpu.VMEM((2, ATT_TK, ATT_TQ), BF16),
                        pltpu.VMEM((2, LANES, ATT_TQ), F32)],
        compiler_params=_params("parallel", "parallel", "arbitrary"),
        name="att_prompt",
    )(bflat, qt, ka, vt)


def _att_sample_kernel(q_ref, kp_ref, vp_ref, dp_ref, dend_ref, kn_ref, vn_ref, dn_ref, o_ref,
                       m_ref, l_ref, acc_ref):
    t = q_ref.shape[1]
    j = pl.program_id(1)

    @pl.when(j == 0)
    def _():
        m_ref[...] = jnp.full_like(m_ref, NEG_INF)
        l_ref[...] = jnp.zeros_like(l_ref)
        acc_ref[...] = jnp.zeros_like(acc_ref)

    def head(h, k, v, time_minor, bias, mask):
        sl = slice(h * HEAD_DIM, (h + 1) * HEAD_DIM)
        q_h = q_ref[0, :, sl]
        s = (_dot(q_h, k) if time_minor else _dot_nt(q_h, k)) + bias[h:h + 1]
        if mask is not None:
            s = jnp.where(mask, s, NEG_INF)
        p, alpha, m_new, l_new = _online_update(s, m_ref[h], l_ref[h])
        p = p.astype(BF16)
        acc = alpha * acc_ref[:, sl] + (_dot_nt(p, v) if time_minor else _dot(p, v))
        m_ref[h], l_ref[h], acc_ref[:, sl] = m_new, l_new, acc
        return acc / l_new

    bias_past = dend_ref[0] - dp_ref[0]
    for h in range(N_HEADS):
        head(h, kp_ref[0, h].astype(BF16), vp_ref[0, h].astype(BF16), True, bias_past, None)

    @pl.when(j == pl.num_programs(1) - 1)
    def _():
        bias_new = -dn_ref[0]
        qpos = lax.broadcasted_iota(jnp.int32, (t, t), 0)
        kpos = lax.broadcasted_iota(jnp.int32, (t, t), 1)
        for h in range(N_HEADS):
            sl = slice(h * HEAD_DIM, (h + 1) * HEAD_DIM)
            o_ref[0, :, sl] = head(h, kn_ref[0, :, sl], vn_ref[0, :, sl], False, bias_new, kpos <= qpos)


def _att_sample(qb, k_past, v_past, d_past, kb_new, vb_new, d_new):
    n, t, _ = qb.shape
    past = k_past.shape[1]
    tk = min(SAMPLE_TK, past)
    assert past % tk == 0
    d_end = d_past[:, :, past - 1:]
    k_past = jnp.transpose(k_past, (0, 2, 3, 1))
    v_past = jnp.transpose(v_past, (0, 2, 3, 1))
    new = lambda dt: pl.BlockSpec((1, t, WA), lambda b, j: (b, 0, 0))
    cache = pl.BlockSpec((1, N_HEADS, HEAD_DIM, tk), lambda b, j: (b, 0, 0, j))
    return pl.pallas_call(
        _att_sample_kernel,
        grid=(n, past // tk),
        in_specs=[new(BF16), cache, cache,
                  pl.BlockSpec((1, N_HEADS, tk), lambda b, j: (b, 0, j)),
                  pl.BlockSpec((1, N_HEADS, 1), lambda b, j: (b, 0, 0)),
                  new(BF16), new(BF16),
                  pl.BlockSpec((1, N_HEADS, t), lambda b, j: (b, 0, 0))],
        out_specs=pl.BlockSpec((1, t, WA), lambda b, j: (b, 0, 0)),
        out_shape=jax.ShapeDtypeStruct((n, t, WA), F32),
        scratch_shapes=[pltpu.VMEM((N_HEADS, t, 1), F32), pltpu.VMEM((N_HEADS, t, 1), F32),
                        pltpu.VMEM((t, WA), F32)],
        compiler_params=_params("parallel", "arbitrary"),
        name="att_sample",
    )(qb, k_past, v_past, d_past, d_end, kb_new, vb_new, d_new)


def _first_index_of_max(x, axis):
    mx = jnp.max(x, axis=axis, keepdims=True)
    idx = lax.broadcasted_iota(jnp.int32, x.shape, axis)
    first = jnp.min(jnp.where(x == mx, idx, x.shape[axis]), axis=axis, keepdims=True)
    return mx, idx == first


def _route(s, bias):
    m = s.shape[1]
    sb = (s + bias).reshape(N_GROUPS, GROUP_SIZE, m)
    top1, is_top1 = _first_index_of_max(sb, 1)
    top2 = jnp.max(jnp.where(is_top1, NEG_INF, sb), axis=1, keepdims=True)
    grp = (top1 + top2).reshape(N_GROUPS, m)
    gi = lax.broadcasted_iota(jnp.int32, (N_GROUPS, N_GROUPS, m), 0)
    gj = lax.broadcasted_iota(jnp.int32, (N_GROUPS, N_GROUPS, m), 1)
    other, mine = grp[None, :, :], grp[:, None, :]
    beats = (other > mine) | ((other == mine) & (gj < gi))
    g_rank = jnp.sum(beats.astype(jnp.int32), axis=1)
    g_keep = (g_rank < TOPK_GROUPS)[:, None, :]
    cand = jnp.where(g_keep, sb, NEG_INF).reshape(N_EXPERTS, m)
    picks = []
    for _ in range(TOP_K):
        _, pick = _first_index_of_max(cand, 0)
        picks.append(pick)
        cand = jnp.where(pick, NEG_INF, cand)
    w = jnp.concatenate([jnp.sum(jnp.where(pk, s, 0.0), axis=0, keepdims=True) for pk in picks], axis=0)
    return picks, w / jnp.sum(w, axis=0, keepdims=True) * ROUTED_SCALE


HALF_MASK = 0xFFFF0000


def _pack_halves(x):
    c = x.shape[1] // 2
    lo = lax.bitcast_convert_type(x[:, :c].astype(BF16).astype(F32), jnp.uint32) >> jnp.uint32(16)
    hi = lax.bitcast_convert_type(x[:, c:].astype(BF16).astype(F32), jnp.uint32) & jnp.uint32(HALF_MASK)
    return lax.bitcast_convert_type(lo | hi, jnp.int32)


def _unpack_halves(words):
    w = lax.bitcast_convert_type(words, jnp.uint32)
    lo = lax.bitcast_convert_type(w << jnp.uint32(16), F32)
    hi = lax.bitcast_convert_type(w & jnp.uint32(HALF_MASK), F32)
    return lo, hi


def _outproj_kernel(x_ref, lru_ref, att_ref, mod_ref, gatt_ref, wtop_ref, wbot_ref, gpost_ref, gpre_ref,
                    rwh_ref, rwl_ref, rb_ref, cnt_in_ref, before_ref,
                    x1_ref, hf_ref, xw_ref, ids_ref, ranks_ref, gates_ref, cnt_out_ref, carry_ref, *, split):
    nb, tt, d = x_ref.shape
    m = nb * tt
    first = (pl.program_id(0) == 0) & (pl.program_id(1) == 0)

    @pl.when(first)
    def _():
        carry_ref[...] = cnt_in_ref[...]

    if split is not None:
        @pl.when((pl.program_id(0) == split) & (pl.program_id(1) == 0))
        def _():
            carry_ref[...] = jnp.zeros_like(carry_ref)

    mod = mod_ref[...]
    att_n = _rms(att_ref[...], gatt_ref[...]).reshape(m, WA).astype(BF16)
    mix = _dot(lru_ref[...].reshape(m, WL), wtop_ref[...]) + _dot(att_n, wbot_ref[...])
    x1 = x_ref[...] + mod[:, 2:3, :] * _rms(mix, gpost_ref[...]).reshape(nb, tt, d)
    x1_ref[...] = x1
    hf = (_rms(x1, gpre_ref[...]) * (1.0 + mod[:, 4:5, :]) + mod[:, 3:4, :]).reshape(m, d)
    hf_hi = hf.astype(BF16)
    hf_ref[...] = hf_hi.reshape(nb, tt, d)
    hf_lo = (hf - hf_hi.astype(F32)).astype(BF16)
    rwh = rwh_ref[...]
    logits = _dot_nt(rwh, hf_hi) + _dot_nt(rwh, hf_lo) + _dot_nt(rwl_ref[...], hf_hi)
    picks, gates = _route(_sigmoid(logits), rb_ref[...])
    xw_ref[...] = _pack_halves(hf).reshape(nb, tt, d // 2)

    sel = jnp.zeros((N_EXPERTS, m), F32)
    for pk in picks:
        sel = sel + pk.astype(F32)
    prior = _dot(sel.astype(BF16), before_ref[...]) + carry_ref[...]
    expert = lax.broadcasted_iota(jnp.int32, (N_EXPERTS, m), 0).astype(F32)
    take = lambda pk, v: jnp.sum(jnp.where(pk, v, 0.0), axis=0, keepdims=True)
    ids_ref[...] = jnp.concatenate([take(pk, expert) for pk in picks], axis=0).astype(jnp.int32)
    ranks_ref[...] = jnp.concatenate([take(pk, prior) for pk in picks], axis=0).astype(jnp.int32)
    carry_ref[...] += jnp.sum(sel, axis=1, keepdims=True)
    cnt_out_ref[0] = carry_ref[...]
    gates = jnp.concatenate([gates, jnp.zeros((LANES - TOP_K, m), F32)], axis=0)
    gates_ref[...] = gates.T.reshape(nb, tt, LANES)


def _outproj(x, lru_n, att, mod, g_att, w_top, w_bot, g_post, g_pre, rw_hi, rw_lo, r_bias, cnt_in, split=None):
    n, t, d = x.shape
    nb, tt = _seq_blocks(n, t)
    m = nb * tt
    steps_t = t // tt
    n_groups = 1 if split is None else 2
    blk = lambda w: pl.BlockSpec((nb, tt, w), lambda i, j: (i, j, 0))
    const = lambda shape: pl.BlockSpec(shape, lambda i, j: (0,) * len(shape))
    per_tok = pl.BlockSpec((TOP_K, m), lambda i, j: (0, i * steps_t + j))
    group_of = (lambda i: 0) if split is None else (lambda i: jnp.where(i < split, 0, 1))
    before = jnp.asarray(np.triu(np.ones((m, m), np.float32), k=1), BF16)
    return pl.pallas_call(
        functools.partial(_outproj_kernel, split=split),
        grid=(n // nb, steps_t),
        in_specs=[blk(d), blk(WL), blk(WA), pl.BlockSpec((nb, 6, d), lambda i, j: (i, 0, 0)),
                  const((1, WA)), const((WL, d)), const((WA, d)), const((1, d)), const((1, d)),
                  const((N_EXPERTS, d)), const((N_EXPERTS, d)), const((N_EXPERTS, 1)), const((N_EXPERTS, 1)),
                  const((m, m))],
        out_specs=[blk(d), blk(d), blk(d // 2), per_tok, per_tok, blk(LANES),
                   pl.BlockSpec((1, N_EXPERTS, 1), lambda i, j: (group_of(i), 0, 0))],
        out_shape=[jax.ShapeDtypeStruct((n, t, d), F32), jax.ShapeDtypeStruct((n, t, d), BF16),
                   jax.ShapeDtypeStruct((n, t, d // 2), jnp.int32),
                   jax.ShapeDtypeStruct((TOP_K, n * t), jnp.int32), jax.ShapeDtypeStruct((TOP_K, n * t), jnp.int32),
                   jax.ShapeDtypeStruct((n, t, LANES), F32), jax.ShapeDtypeStruct((n_groups, N_EXPERTS, 1), F32)],
        scratch_shapes=[pltpu.VMEM((N_EXPERTS, 1), F32)],
        compiler_params=_params("arbitrary", "arbitrary"),
        name="outproj_router",
    )(x, lru_n, att, mod, g_att, w_top, w_bot, g_post, g_pre, rw_hi, rw_lo, r_bias, cnt_in, before)


def _subcore_ranges(n_items):
    info = plsc.get_sparse_core_info()
    n_workers = info.num_cores * info.num_subcores
    per_worker = n_items // n_workers
    assert per_worker * n_workers == n_items and per_worker % GATHER_ROWS == 0
    return info, plsc.VectorSubcoreMesh(core_axis_name="c", subcore_axis_name="s"), per_worker


def _scatter_rows(xw, pos, n_slots, row_offset=0):
    m_tot, c = pos.shape[1], xw.shape[1]
    assert row_offset % GATHER_ROWS == 0
    info, mesh, per_worker = _subcore_ranges(m_tot)
    pos_flat = pos.reshape(-1)

    @functools.partial(
        pl.kernel, mesh=mesh, out_type=jax.ShapeDtypeStruct((n_slots, c), jnp.int32),
        scratch_types=[pltpu.VMEM((GATHER_ROWS,), jnp.int32), pltpu.VMEM((GATHER_ROWS, c), jnp.int32),
                       pltpu.SemaphoreType.DMA])
    def scatter(xw_hbm, pos_hbm, out_hbm, idx_v, rows_v, sem):
        worker = lax.axis_index("s") * info.num_cores + lax.axis_index("c")
        base = worker * per_worker

        @pl.loop(0, per_worker // GATHER_ROWS)
        def _(step):
            off = pl.multiple_of(base + step * GATHER_ROWS, GATHER_ROWS)
            pltpu.sync_copy(xw_hbm.at[pl.ds(row_offset + off, GATHER_ROWS)], rows_v)
            for r in range(TOP_K):
                pltpu.sync_copy(pos_hbm.at[pl.ds(pl.multiple_of(r * m_tot + off, GATHER_ROWS), GATHER_ROWS)], idx_v)
                pltpu.async_copy(rows_v, out_hbm.at[idx_v], sem).wait()

    return scatter(xw, pos_flat)


def _swiglu_halves(lo, hi, wg, wu, wd):
    c = lo.shape[1]
    hg = _dot(lo, wg[:c]) + _dot(hi, wg[c:])
    hu = _dot(lo, wu[:c]) + _dot(hi, wu[c:])
    return _dot((_silu(hg) * hu).astype(BF16), wd)


def _expert_kernel(te_ref, valid_ref, x_ref, wg_ref, wu_ref, wd_ref, y_ref, wg_bf, wu_bf, wd_bf):
    i = pl.program_id(0)
    valid = valid_ref[i]

    @pl.when((i == 0) | (te_ref[i] != te_ref[jnp.maximum(i - 1, 0)]))
    def _():
        wg_bf[...] = wg_ref[0].astype(BF16)
        wu_bf[...] = wu_ref[0].astype(BF16)
        wd_bf[...] = wd_ref[0].astype(BF16)

    @pl.when(valid > 0)
    def _():
        w = x_ref[...]
        row = lax.broadcasted_iota(jnp.int32, w.shape, 0)
        lo, hi = _unpack_halves(jnp.where(row < valid, w, 0))
        y_ref[...] = _pack_halves(_swiglu_halves(lo.astype(BF16), hi.astype(BF16), wg_bf[...], wu_bf[...], wd_bf[...]))


def _experts(xs, tile_expert, tile_valid, wg, wu, wd):
    n_slots, c = xs.shape
    n_tiles = n_slots // EXPERT_TILE
    d = 2 * c
    rows = pl.BlockSpec((EXPERT_TILE, c), lambda i, te, tv: (i, 0))
    weight = lambda shape: pl.BlockSpec((1,) + shape, lambda i, te, tv: (te[i], 0, 0))
    return pl.pallas_call(
        _expert_kernel,
        grid_spec=pltpu.PrefetchScalarGridSpec(
            num_scalar_prefetch=2, grid=(n_tiles,),
            in_specs=[rows, weight((d, D_EXPERT)), weight((d, D_EXPERT)), weight((D_EXPERT, d))],
            out_specs=rows,
            scratch_shapes=[pltpu.VMEM((d, D_EXPERT), BF16), pltpu.VMEM((d, D_EXPERT), BF16),
                            pltpu.VMEM((D_EXPERT, d), BF16)]),
        out_shape=jax.ShapeDtypeStruct((n_slots, c), jnp.int32),
        compiler_params=_params("arbitrary"),
        name="moe_experts",
    )(tile_expert, tile_valid, xs, wg, wu, wd)


def _gather_rows(table, idx):
    b, c = idx.shape[0], table.shape[1]
    info, mesh, per_worker = _subcore_ranges(b)

    n_steps = per_worker // GATHER_ROWS
    assert n_steps % 2 == 0

    @functools.partial(
        pl.kernel, mesh=mesh, out_type=jax.ShapeDtypeStruct((b, c), jnp.int32),
        scratch_types=[pltpu.VMEM((2, GATHER_ROWS), jnp.int32), pltpu.VMEM((2, GATHER_ROWS, c), jnp.int32),
                       pltpu.SemaphoreType.DMA((2,))])
    def gather(table_hbm, idx_hbm, out_hbm, idx_v, rows_v, sems):
        worker = lax.axis_index("s") * info.num_cores + lax.axis_index("c")
        base = worker * per_worker

        def chunk(step):
            return pl.ds(pl.multiple_of(base + step * GATHER_ROWS, GATHER_ROWS), GATHER_ROWS)

        def stream(buf):
            return pltpu.make_async_copy(table_hbm.at[idx_v.at[buf]], rows_v.at[buf], sems.at[buf])

        def start(step, buf):
            pltpu.sync_copy(idx_hbm.at[chunk(step)], idx_v.at[buf])
            stream(buf).start()

        def finish(step, buf):
            stream(buf).wait()
            pltpu.sync_copy(rows_v.at[buf], out_hbm.at[chunk(step)])

        start(0, 0)

        @pl.loop(0, n_steps, step=2)
        def _(step):
            start(step + 1, 1)
            finish(step, 0)

            @pl.when(step + 2 < n_steps)
            def _():
                start(step + 2, 0)

            finish(step + 1, 1)

    return gather(table, idx)


def _combine_kernel(rows_ref, gates_ref, hf_ref, x1_ref, mod_ref, sg_ref, su_ref, sd_ref, gpost_ref, y_ref):
    nb, tt, d = hf_ref.shape
    m = nb * tt
    c = d // 2
    x = hf_ref[...].reshape(m, d)
    shared = _swiglu_halves(x[:, :c], x[:, c:], sg_ref[...], su_ref[...], sd_ref[...])
    gates = gates_ref[...].reshape(m, LANES)
    acc_lo = shared[:, :c]
    acc_hi = shared[:, c:]
    for r in range(TOP_K):
        lo, hi = _unpack_halves(rows_ref[r])
        g = gates[:, r:r + 1]
        acc_lo = acc_lo + g * lo
        acc_hi = acc_hi + g * hi
    z = _rms(jnp.concatenate([acc_lo, acc_hi], axis=1), gpost_ref[...]).reshape(nb, tt, d)
    y_ref[...] = x1_ref[...] + mod_ref[...][:, 5:6, :] * z


def _combine(rows, first_tile, gates_t, hf, x1, mod, sg, su, sd, g_post, blocks=None, y_prev=None):
    n, t, d = hf.shape
    nb, tt = _seq_blocks(n, t)
    m = nb * tt
    steps_t = t // tt
    b0, nblocks = (0, n // nb) if blocks is None else blocks
    blk = lambda w: pl.BlockSpec((nb, tt, w), lambda i, j: (b0 + i, j, 0))
    const = lambda shape: pl.BlockSpec(shape, lambda i, j: (0,) * len(shape))
    in_specs = [pl.BlockSpec((TOP_K, m, d // 2), lambda i, j: (0, first_tile + i * steps_t + j, 0)),
                blk(LANES), blk(d), blk(d), pl.BlockSpec((nb, 6, d), lambda i, j: (b0 + i, 0, 0)),
                const((d, D_EXPERT)), const((d, D_EXPERT)), const((D_EXPERT, d)), const((1, d))]
    args = (rows, gates_t, hf, x1, mod, sg, su, sd, g_post)
    kernel_fn, aliases = _combine_kernel, {}
    if y_prev is not None:
        in_specs.append(pl.BlockSpec(memory_space=pl.ANY))
        args += (y_prev,)
        aliases = {len(args) - 1: 0}
        kernel_fn = lambda *refs: _combine_kernel(*refs[:len(args) - 1], refs[-1])
    return pl.pallas_call(
        kernel_fn,
        grid=(nblocks, steps_t),
        in_specs=in_specs,
        out_specs=blk(d),
        out_shape=jax.ShapeDtypeStruct((n, t, d), F32),
        input_output_aliases=aliases,
        compiler_params=_params("parallel", "parallel"),
        name="moe_combine",
    )(*args)


def _slots_kernel(starts_ref, ids_ref, ranks_ref, pos_ref):
    ids = ids_ref[...]

    def add_start(e, pos):
        return pos + jnp.where(ids == e, starts_ref[e], 0)

    pos_ref[...] = lax.fori_loop(0, N_EXPERTS, add_start, ranks_ref[...])


def _slots(starts, ids, ranks):
    k, m_tot = ids.shape
    cols = math.gcd(m_tot, SLOT_COLS)
    assert cols % LANES == 0
    blk = pl.BlockSpec((k, cols), lambda i: (0, i))
    return pl.pallas_call(
        _slots_kernel,
        grid=(m_tot // cols,),
        in_specs=[pl.BlockSpec(memory_space=pltpu.SMEM), blk, blk],
        out_specs=blk,
        out_shape=jax.ShapeDtypeStruct((k, m_tot), jnp.int32),
        compiler_params=_params("arbitrary"),
        name="moe_slots",
    )(starts, ids, ranks)


def _slot_plan(ids, ranks, counts):
    n_pairs = ids.shape[0] * ids.shape[1]
    n_tiles = -(-(n_pairs + N_EXPERTS * (EXPERT_TILE - 1)) // EXPERT_TILE)
    cnt = counts.reshape(N_EXPERTS).astype(jnp.int32)
    padded = (cnt + EXPERT_TILE - 1) // EXPERT_TILE * EXPERT_TILE
    ends = jnp.cumsum(padded)
    starts = ends - padded
    pos = _slots(starts, ids, ranks)
    tile_start = jnp.arange(n_tiles, dtype=jnp.int32) * EXPERT_TILE
    in_expert = (tile_start[:, None] >= starts[None, :]) & (tile_start[:, None] < ends[None, :])
    tile_expert = jnp.sum(jnp.where(in_expert, jnp.arange(N_EXPERTS, dtype=jnp.int32)[None, :], 0), axis=1)
    tile_fill = jnp.sum(jnp.where(in_expert, (starts + cnt)[None, :] - tile_start[:, None], 0), axis=1)
    tile_valid = jnp.clip(tile_fill, 0, EXPERT_TILE).astype(jnp.int32)
    return pos, tile_expert.astype(jnp.int32), tile_valid, n_tiles * EXPERT_TILE


def _block_diag(w):
    g, bw, _ = w.shape
    eye = jnp.eye(g, dtype=w.dtype)
    return (eye[:, None, :, None] * w[:, :, None, :]).reshape(g * bw, g * bw)


def _prep_weights(p):
    d_main = 2 * WL + 3 * WA
    w_in = p["w_in"]
    rw_t = p["router_w"].T
    rw_hi = rw_t.astype(BF16)
    row = lambda v: v.reshape(1, -1)
    return dict(
        w_mod=p["w_mod"], b_mod=p["b_mod"],
        g_pre_mix=row(p["g_pre_mix"]), g_post_mix=row(p["g_post_mix"]),
        g_pre_ffn=row(p["g_pre_ffn"]), g_post_ffn=row(p["g_post_ffn"]),
        w_main=w_in[:, :d_main].astype(BF16),
        w_f=jnp.pad(w_in[:, d_main:], ((0, 0), (0, LANES - N_HEADS))).astype(BF16),
        b_f=jnp.pad(p["b_f"], (0, LANES - N_HEADS)).reshape(1, LANES),
        conv_w=p["conv_w"], conv_b=row(p["conv_b"]),
        wr_bd=_block_diag(p["w_r"]).astype(BF16), b_r=row(p["b_r"]),
        wi_bd=_block_diag(p["w_i"]).astype(BF16), b_i=row(p["b_i"]),
        lam=row(p["lru_lambda"]), g_lru=row(p["g_lru_out"]), g_att=row(p["g_att_out"]),
        w_top=p["w_out"][:WL].astype(BF16), w_bot=p["w_out"][WL:].astype(BF16),
        rw_hi=rw_hi, rw_lo=(rw_t - rw_hi.astype(F32)).astype(BF16),
        r_bias=p["router_bias"].reshape(N_EXPERTS, 1),
        wg=p["w_gate"], wu=p["w_up"], wd=p["w_down"],
        sg=p["ws_gate"].astype(BF16), su=p["ws_up"].astype(BF16), sd=p["ws_down"].astype(BF16),
    )


def _mixers(x, mod, conv0, h0, past, w, cnt_in, split=None):
    n, t, _ = x.shape
    proj_args = (x, mod, w["g_pre_mix"], w["w_main"], w["w_f"], w["b_f"])
    if past is None:
        xl, gy, k, v, lf, qt, ka, vt, bpre = _inproj_prompt(*proj_args)
        att = _att_prompt(qt, ka, vt, bpre)
    else:
        xl, gy, qb, kb, vb, k, v, lf = _inproj(*proj_args)
        k_past, v_past, lf_past = past
        plen = k_past.shape[1]
        by_head = lambda a: jnp.transpose(a, (0, 2, 1)).reshape(n * N_HEADS, a.shape[1])
        d_new = _cumsum_rows(by_head(lf), t).reshape(n, N_HEADS, t)
        d_past = _cumsum_rows(by_head(lf_past), min(CUMSUM_COLS, plen)).reshape(n, N_HEADS, plen)
        att = _att_sample(qb, k_past, v_past, d_past, kb, vb, d_new)
    lru_n, conv_new, h_new = _lru(xl, gy, conv0, h0.reshape(n, 1, WL), w["conv_w"], w["conv_b"],
                                  w["wr_bd"], w["b_r"], w["wi_bd"], w["b_i"], w["lam"], w["g_lru"])
    routed = _outproj(x, lru_n, att, mod, w["g_att"], w["w_top"], w["w_bot"], w["g_post_mix"],
                      w["g_pre_ffn"], w["rw_hi"], w["rw_lo"], w["r_bias"], cnt_in, split)
    state = (k.reshape(n, t, N_HEADS, HEAD_DIM), v.reshape(n, t, N_HEADS, HEAD_DIM), lf,
             conv_new, h_new.reshape(n, WL))
    return routed, state


def _layer(xp, xs, mod_p, mod_s, conv_s, h_s, past_s, w):
    n_p = xp.shape[0]
    conv0 = jnp.zeros((n_p, CONV_W - 1, WL), F32)
    h0 = jnp.zeros((n_p, WL), F32)
    zero_cnt = jnp.zeros((N_EXPERTS, 1), F32)
    t_p = xp.shape[1]
    split = 1 if n_p > 1 else None
    (x1_p, hf_p, xw_p, ids_p, rk_p, g_p, cnt_p), st_p = _mixers(xp, mod_p, conv0, h0, None, w, zero_cnt, split)
    cnt_tail = cnt_p[1] if n_p > 1 else zero_cnt
    (x1_s, hf_s, xw_s, ids_s, rk_s, g_s, cnt_s), st_s = _mixers(xs, mod_s, conv_s, h_s, past_s, w, cnt_tail)

    half = xw_p.shape[-1]
    xw_p = xw_p.reshape(-1, half)

    def routed(xw, row_offset, ids, ranks, counts):
        pos, tile_expert, tile_valid, n_slots = _slot_plan(ids, ranks, counts)
        ys = _experts(_scatter_rows(xw, pos, n_slots, row_offset), tile_expert, tile_valid,
                      w["wg"], w["wu"], w["wd"])
        return _gather_rows(ys, pos.reshape(-1)).reshape(TOP_K, ids.shape[1], half)

    rows_a = routed(xw_p, 0, ids_p[:, :t_p], rk_p[:, :t_p], cnt_p[0])
    xw_b = jnp.concatenate([xw_p[t_p:], xw_s.reshape(-1, half)], axis=0)
    rows_b = routed(xw_b, 0, jnp.concatenate([ids_p[:, t_p:], ids_s], axis=1),
                    jnp.concatenate([rk_p[:, t_p:], rk_s], axis=1), cnt_s[0])
    shared = (w["sg"], w["su"], w["sd"], w["g_post_ffn"])
    yp = _combine(rows_a, 0, g_p, hf_p, x1_p, mod_p, *shared, blocks=(0, 1))
    if n_p > 1:
        yp = _combine(rows_b, 0, g_p, hf_p, x1_p, mod_p, *shared, blocks=(1, n_p - 1), y_prev=yp)
    ysmp = _combine(rows_b, (n_p - 1) * t_p // ROW_TILE, g_s, hf_s, x1_s, mod_s, *shared)
    return yp, ysmp, st_p, st_s


def kernel(x_prompt, x_sample, c_prompt, c_sample, cache_k, cache_v, cache_logf, state_conv, state_lru, w_mod, b_mod, g_pre_mix, g_post_mix, g_pre_ffn, g_post_ffn, w_in, conv_w, conv_b, w_r, b_r, w_i, b_i, lru_lambda, b_f, g_lru_out, g_att_out, w_out, router_w, router_bias, w_gate, w_up, w_down, ws_gate, ws_up, ws_down):
    names = ("w_mod", "b_mod", "g_pre_mix", "g_post_mix", "g_pre_ffn", "g_post_ffn", "w_in", "conv_w", "conv_b",
             "w_r", "b_r", "w_i", "b_i", "lru_lambda", "b_f", "g_lru_out", "g_att_out", "w_out", "router_w",
             "router_bias", "w_gate", "w_up", "w_down", "ws_gate", "ws_up", "ws_down")
    stacked = (w_mod, b_mod, g_pre_mix, g_post_mix, g_pre_ffn, g_post_ffn, w_in, conv_w, conv_b, w_r, b_r, w_i, b_i,
               lru_lambda, b_f, g_lru_out, g_att_out, w_out, router_w, router_bias, w_gate, w_up, w_down,
               ws_gate, ws_up, ws_down)
    depth = w_mod.shape[0]
    n_p, n_s = x_prompt.shape[0], x_sample.shape[0]
    yp, ys = x_prompt, x_sample
    st_p, st_s = [], []
    for l in range(depth):
        w = _prep_weights({k: v[l] for k, v in zip(names, stacked)})
        mod = _modulation(jnp.concatenate([c_prompt, c_sample], axis=0), w["w_mod"], w["b_mod"])
        mod = mod.reshape(n_p + n_s, 6, D_MODEL)
        yp, ys, sp, ss = _layer(yp, ys, mod[:n_p], mod[n_p:], state_conv[l], state_lru[l],
                                (cache_k[l], cache_v[l], cache_logf[l]), w)
        st_p.append(sp)
        st_s.append(ss)
    stack = lambda sts, i: jnp.stack([s[i] for s in sts])
    return (yp, ys) + tuple(stack(st_p, i) for i in range(5)) + tuple(stack(st_s, i) for i in range(5))
```

```python
import functools
import math

import jax
import jax.numpy as jnp
import numpy as np
from jax import lax
from jax.experimental import pallas as pl
from jax.experimental.pallas import tpu as pltpu
from jax.experimental.pallas import tpu_sc as plsc

F32 = jnp.float32
BF16 = jnp.bfloat16

D_MODEL = 1024
WL = 512
WA = 512
N_HEADS = 8
HEAD_DIM = 64
N_PAIRS = N_HEADS // 2
PAIR_W = 2 * HEAD_DIM
LANES = 128
CONV_W = 4
LRU_BLOCKS = 8
LRU_C = 8.0
N_EXPERTS = 64
N_GROUPS = 8
GROUP_SIZE = N_EXPERTS // N_GROUPS
TOPK_GROUPS = 4
TOP_K = 8
D_EXPERT = 256
ROUTED_SCALE = 2.5
EPS = 1e-6
NEG_INF = float("-inf")
LOG2E = 1.4426950408889634

ROW_TILE = 512
ATT_TQ = 512
ATT_TK = 512
ATT_CHUNK = 64
SAMPLE_TK = 4096
CUMSUM_COLS = 1024
EXPERT_TILE = 512
SLOT_COLS = 2048
GATHER_ROWS = 64
VMEM_LIMIT = 56 * 1024 * 1024


def _params(*sem):
    return pltpu.CompilerParams(dimension_semantics=sem, vmem_limit_bytes=VMEM_LIMIT)


def _dot(a, b):
    return jnp.dot(a, b, preferred_element_type=F32)


def _dot_nt(a, b):
    return lax.dot_general(a, b, (((1,), (1,)), ((), ())), preferred_element_type=F32)


def _split3(x):
    hi = x.astype(BF16)
    r1 = x - hi.astype(F32)
    mid = r1.astype(BF16)
    lo = (r1 - mid.astype(F32)).astype(BF16)
    return hi, mid, lo


def _rms(x, g):
    return x * lax.rsqrt(jnp.mean(x * x, axis=-1, keepdims=True) + EPS) * g


def _sigmoid(x):
    return 1.0 / (1.0 + jnp.exp(-x))


def _silu(x):
    return x * _sigmoid(x)


def _gelu_tanh(x):
    return 0.5 * x * (1.0 + jnp.tanh(0.7978845608028654 * (x + 0.044715 * (x * x * x))))


def _log_sigmoid(x):
    return jnp.minimum(x, 0.0) - jnp.log1p(jnp.exp(-jnp.abs(x)))


def _seq_blocks(n, t):
    if t >= ROW_TILE:
        assert t % ROW_TILE == 0
        return 1, ROW_TILE
    nb = ROW_TILE // t
    assert nb * t == ROW_TILE and n % nb == 0
    return nb, t


def _mod_kernel(c_ref, w_ref, b_ref, o_ref):
    c = _silu(c_ref[...])
    c_hi = c.astype(BF16)
    c_lo = (c - c_hi.astype(F32)).astype(BF16)
    w = w_ref[...]
    w_hi = w.astype(BF16)
    w_lo = (w - w_hi.astype(F32)).astype(BF16)
    o_ref[...] = _dot(c_hi, w_hi) + _dot(c_lo, w_hi) + _dot(c_hi, w_lo) + b_ref[...]


def _modulation(c, w_mod, b_mod):
    rows = c.shape[0]
    n = -(-rows // 8) * 8
    c = jnp.pad(c, ((0, n - rows), (0, 0)))
    d6 = w_mod.shape[1]
    return pl.pallas_call(
        _mod_kernel,
        grid=(d6 // D_MODEL,),
        in_specs=[pl.BlockSpec((n, D_MODEL), lambda j: (0, 0)),
                  pl.BlockSpec((D_MODEL, D_MODEL), lambda j: (0, j)),
                  pl.BlockSpec((1, D_MODEL), lambda j: (0, j))],
        out_specs=pl.BlockSpec((n, D_MODEL), lambda j: (0, j)),
        out_shape=jax.ShapeDtypeStruct((n, d6), F32),
        compiler_params=_params("arbitrary"),
        name="modulation",
    )(c, w_mod, b_mod.reshape(1, d6))[:rows]


def _inproj_kernel(x_ref, mod_ref, g_ref, w_ref, wf_ref, bf_ref,
                   xl_ref, gy_ref, qb_ref, kb_ref, vb_ref, k_ref, v_ref, lf_ref):
    nb, tt, d = x_ref.shape
    x = x_ref[...]
    mod = mod_ref[...]
    hn = _rms(x, g_ref[...]) * (1.0 + mod[:, 1:2, :]) + mod[:, 0:1, :]
    hb = hn.reshape(nb * tt, d).astype(BF16)

    def proj(col):
        return _dot(hb, w_ref[:, col * WL:(col + 1) * WL]).reshape(nb, tt, WL)

    xl_ref[...] = proj(0)
    gy_ref[...] = _gelu_tanh(proj(1)).astype(BF16)
    qb_ref[...] = (proj(2) * (HEAD_DIM ** -0.5)).astype(BF16)
    k = proj(3)
    k_ref[...] = k
    kb_ref[...] = k.astype(BF16)
    v = proj(4)
    v_ref[...] = v
    vb_ref[...] = v.astype(BF16)
    fl = _dot(hb, wf_ref[...]) + bf_ref[...]
    lf_ref[...] = _log_sigmoid(fl).reshape(nb, tt, LANES)[:, :, :N_HEADS]


def _inproj(x, mod, g_pre, w_main, w_f, b_f):
    n, t, d = x.shape
    nb, tt = _seq_blocks(n, t)
    blk = lambda w: pl.BlockSpec((nb, tt, w), lambda i, j: (i, j, 0))
    const = lambda shape: pl.BlockSpec(shape, lambda i, j: (0,) * len(shape))
    f32 = lambda w: jax.ShapeDtypeStruct((n, t, w), F32)
    b16 = lambda w: jax.ShapeDtypeStruct((n, t, w), BF16)
    return pl.pallas_call(
        _inproj_kernel,
        grid=(n // nb, t // tt),
        in_specs=[blk(d),
                  pl.BlockSpec((nb, 6, d), lambda i, j: (i, 0, 0)),
                  const((1, d)), const(w_main.shape), const(w_f.shape), const((1, LANES))],
        out_specs=[blk(WL), blk(WL), blk(WA), blk(WA), blk(WA), blk(WA), blk(WA), blk(N_HEADS)],
        out_shape=[f32(WL), b16(WL), b16(WA), b16(WA), b16(WA), f32(WA), f32(WA), f32(N_HEADS)],
        compiler_params=_params("parallel", "arbitrary"),
        name="inproj",
    )(x, mod, g_pre, w_main, w_f, b_f)


def _aug_lane(h):
    return HEAD_DIM if h % 2 == 0 else 0


def _inproj_prompt_kernel(x_ref, mod_ref, g_ref, w_ref, wf_ref, bf_ref, place_ref,
                          xl_ref, gy_ref, k_ref, v_ref, lf_ref, qt_ref, ka_ref, vt_ref, bpre_ref, carry_ref):
    _, tt, d = x_ref.shape

    @pl.when(pl.program_id(1) == 0)
    def _():
        carry_ref[...] = jnp.zeros_like(carry_ref)

    mod = mod_ref[0]
    hb = (_rms(x_ref[0], g_ref[...]) * (1.0 + mod[1:2, :]) + mod[0:1, :]).astype(BF16)

    def proj(col):
        return _dot(hb, w_ref[:, col * WL:(col + 1) * WL])

    xl_ref[0] = proj(0)
    gy_ref[0] = _gelu_tanh(proj(1)).astype(BF16)
    q = proj(2) * (HEAD_DIM ** -0.5 * LOG2E)
    k = proj(3)
    k_ref[0] = k
    v = proj(4)
    v_ref[0] = v
    fl = _dot(hb, wf_ref[...]) + bf_ref[...]
    lane = lax.broadcasted_iota(jnp.int32, (tt, LANES), 1)
    lf = jnp.where(lane < N_HEADS, _log_sigmoid(fl), 0.0)
    lf_ref[0] = lf[:, :N_HEADS]

    row = lax.broadcasted_iota(jnp.int32, (tt, tt), 0)
    col = lax.broadcasted_iota(jnp.int32, (tt, tt), 1)
    tril = (col <= row).astype(BF16)
    hi, mid, lo = _split3(lf)
    e = _dot(tril, hi) + _dot(tril, mid) + _dot(tril, lo)
    bpre_ref[0, 0] = carry_ref[...]
    carry_ref[...] += e[tt - 1:tt, :]
    e_hi, e_mid, e_lo = _split3(e * (-LOG2E))
    aug_k = _dot(jnp.concatenate([e_hi, e_mid, e_lo], axis=1), place_ref[...])

    for h in range(N_HEADS):
        pair = slice((h // 2) * PAIR_W, (h // 2 + 1) * PAIR_W)
        dims = (lane < HEAD_DIM) if h % 2 == 0 else (lane >= HEAD_DIM)
        a0 = _aug_lane(h)
        ones3 = ((lane >= a0) & (lane < a0 + 3)).astype(F32)
        qt_ref[0, h] = (jnp.where(dims, q[:, pair], 0.0) + ones3).T.astype(BF16)
        ka_ref[0, h] = (jnp.where(dims, k[:, pair], 0.0) + aug_k[:, h * LANES:(h + 1) * LANES]).astype(BF16)
        one1 = (lane == a0).astype(F32)
        vt_ref[0, h] = (jnp.where(dims, v[:, pair], 0.0) + one1).T.astype(BF16)


def _placement():
    pl_mat = np.zeros((3 * LANES, N_HEADS * LANES), np.float32)
    for p in range(3):
        for h in range(N_HEADS):
            pl_mat[p * LANES + h, h * LANES + _aug_lane(h) + p] = 1.0
    return jnp.asarray(pl_mat, BF16)


def _inproj_prompt(x, mod, g_pre, w_main, w_f, b_f):
    n, t, d = x.shape
    tt = ROW_TILE
    assert t % tt == 0
    nblk = t // tt
    blk = lambda w: pl.BlockSpec((1, tt, w), lambda i, j: (i, j, 0))
    const = lambda shape: pl.BlockSpec(shape, lambda i, j: (0,) * len(shape))
    f32 = lambda w: jax.ShapeDtypeStruct((n, t, w), F32)
    place = _placement()
    return pl.pallas_call(
        _inproj_prompt_kernel,
        grid=(n, nblk),
        in_specs=[blk(d), pl.BlockSpec((1, 6, d), lambda i, j: (i, 0, 0)),
                  const((1, d)), const(w_main.shape), const(w_f.shape), const((1, LANES)), const(place.shape)],
        out_specs=[blk(WL), blk(WL), blk(WA), blk(WA), blk(N_HEADS),
                   pl.BlockSpec((1, N_HEADS, LANES, tt), lambda i, j: (i, 0, 0, j)),
                   pl.BlockSpec((1, N_HEADS, tt, LANES), lambda i, j: (i, 0, j, 0)),
                   pl.BlockSpec((1, N_HEADS, LANES, tt), lambda i, j: (i, 0, 0, j)),
                   pl.BlockSpec((1, 1, 1, LANES), lambda i, j: (i, j, 0, 0))],
        out_shape=[f32(WL), jax.ShapeDtypeStruct((n, t, WL), BF16), f32(WA), f32(WA), f32(N_HEADS),
                   jax.ShapeDtypeStruct((n, N_HEADS, LANES, t), BF16),
                   jax.ShapeDtypeStruct((n, N_HEADS, t, LANES), BF16),
                   jax.ShapeDtypeStruct((n, N_HEADS, LANES, t), BF16),
                   jax.ShapeDtypeStruct((n, nblk, 1, LANES), F32)],
        scratch_shapes=[pltpu.VMEM((1, LANES), F32)],
        compiler_params=_params("parallel", "arbitrary"),
        name="inproj_prompt",
    )(x, mod, g_pre, w_main, w_f, b_f, place)


def _expm1_neg(x):
    poly = x * (1.0 + x * (0.5 + x * (1.0 / 6.0 + x * (1.0 / 24.0 + x * (1.0 / 120.0)))))
    return jnp.where(x > -0.1, poly, jnp.exp(x) - 1.0)


def _lru_kernel(xl_ref, gy_ref, conv0_ref, h0_ref, cw_ref, cb_ref, wr_ref, br_ref, wi_ref, bi_ref,
                lam_ref, g_ref, out_ref, conv_ref, hlast_ref, tail_ref, carry_ref):
    nb, tt, w = xl_ref.shape
    j = pl.program_id(1)

    @pl.when(j == 0)
    def _():
        tail_ref[:, 8 - (CONV_W - 1):, :] = conv0_ref[...]
        carry_ref[...] = h0_ref[...]

    xl = xl_ref[...]
    xpad = jnp.concatenate([tail_ref[...], xl], axis=1)
    cw = cw_ref[...]
    xc = jnp.zeros_like(xl) + cb_ref[...]
    for k in range(CONV_W):
        off = 8 - (CONV_W - 1) + k
        xc = xc + xpad[:, off:off + tt, :] * cw[k:k + 1, :]
    conv_ref[...] = xpad[:, tt + 8 - (CONV_W - 1):, :]
    tail_ref[...] = xpad[:, tt:, :]

    m = nb * tt
    xf = xc.reshape(m, w)
    xb = xf.astype(BF16)
    r = _sigmoid(_dot(xb, wr_ref[...]) + br_ref[...])
    gi = _sigmoid(_dot(xb, wi_ref[...]) + bi_ref[...])
    lam = lam_ref[...]
    softplus = jnp.maximum(-lam, 0.0) + jnp.log1p(jnp.exp(-jnp.abs(lam)))
    log_a = (-LRU_C) * r * softplus
    a = jnp.exp(log_a)
    b = jnp.sqrt(-_expm1_neg(2.0 * log_a)) * (gi * xf)

    groups = m // 8
    a = a.reshape(groups, 8, w)
    b = b.reshape(groups, 8, w)
    sub = lax.broadcasted_iota(jnp.int32, (groups, 8, w), 1)
    for d in (1, 2, 4):
        keep = sub >= d
        a_prev = jnp.where(keep, pltpu.roll(a, d, 1), 1.0)
        b_prev = jnp.where(keep, pltpu.roll(b, d, 1), 0.0)
        b = a * b_prev + b
        a = a * a_prev
    carry = carry_ref[...]
    groups_per_seq = tt // 8
    rows = []
    for g in range(groups):
        if g % groups_per_seq == 0:
            prev = carry[g // groups_per_seq]
        h_g = a[g] * prev + b[g]
        prev = h_g[7:8]
        rows.append(h_g)
    h = jnp.concatenate(rows, axis=0).reshape(nb, tt, w)
    h_last = h[:, tt - 1:tt, :]
    carry_ref[...] = h_last
    hlast_ref[...] = h_last
    out_ref[...] = _rms(h * gy_ref[...].astype(F32), g_ref[...]).astype(BF16)


def _lru(xl, gy, conv0, h0, conv_w, conv_b, wr_bd, b_r, wi_bd, b_i, lam, g_lru):
    n, t, w = xl.shape
    nb, tt = _seq_blocks(n, t)
    blk = pl.BlockSpec((nb, tt, w), lambda i, j: (i, j, 0))
    per_seq = lambda rows: pl.BlockSpec((nb, rows, w), lambda i, j: (i, 0, 0))
    const = lambda shape: pl.BlockSpec(shape, lambda i, j: (0,) * len(shape))
    row = const((1, w))
    return pl.pallas_call(
        _lru_kernel,
        grid=(n // nb, t // tt),
        in_specs=[blk, blk, per_seq(CONV_W - 1), per_seq(1),
                  const((CONV_W, w)), row, const((w, w)), row, const((w, w)), row, row, row],
        out_specs=[blk, per_seq(CONV_W - 1), per_seq(1)],
        out_shape=[jax.ShapeDtypeStruct((n, t, w), BF16),
                   jax.ShapeDtypeStruct((n, CONV_W - 1, w), F32),
                   jax.ShapeDtypeStruct((n, 1, w), F32)],
        scratch_shapes=[pltpu.VMEM((nb, 8, w), F32), pltpu.VMEM((nb, 1, w), F32)],
        compiler_params=_params("parallel", "arbitrary"),
        name="rglru",
    )(xl, gy, conv0, h0, conv_w, conv_b, wr_bd, b_r, wi_bd, b_i, lam, g_lru)


def _cumsum_rows_kernel(x_ref, upper_ref, o_ref, carry_ref):
    tb = x_ref.shape[1]

    @pl.when(pl.program_id(0) == 0)
    def _():
        carry_ref[...] = jnp.zeros_like(carry_ref)

    upper = upper_ref[...]
    hi, mid, lo = _split3(x_ref[...])
    d = _dot(hi, upper) + _dot(mid, upper) + _dot(lo, upper) + carry_ref[...]
    carry_ref[...] = d[:, tb - 1:tb]
    o_ref[...] = d


def _cumsum_rows(x, tb):
    rows, t = x.shape
    upper = jnp.asarray(np.triu(np.ones((tb, tb), np.float32)), BF16)
    return pl.pallas_call(
        _cumsum_rows_kernel,
        grid=(t // tb,),
        in_specs=[pl.BlockSpec((rows, tb), lambda j: (0, j)), pl.BlockSpec((tb, tb), lambda j: (0, 0))],
        out_specs=pl.BlockSpec((rows, tb), lambda j: (0, j)),
        out_shape=jax.ShapeDtypeStruct((rows, t), F32),
        scratch_shapes=[pltpu.VMEM((rows, 1), F32)],
        compiler_params=_params("arbitrary"),
        name="logf_cumsum",
    )(x, upper)


def _online_update(s, m_prev, l_prev):
    m_new = jnp.maximum(m_prev, jnp.max(s, axis=1, keepdims=True))
    alpha = jnp.exp(m_prev - m_new)
    p = jnp.exp(s - m_new)
    l_new = alpha * l_prev + jnp.sum(p, axis=1, keepdims=True)
    return p, alpha, m_new, l_new


def _att_prompt_kernel(bpre_ref, qt_ref, ka_ref, vt_ref, o_ref, s_ref, p_ref, acc_ref):
    tq = qt_ref.shape[2]
    tk = ATT_TK
    nblk = ka_ref.shape[1] // tk
    b, hp, i = pl.program_id(0), pl.program_id(1), pl.program_id(2)
    q0 = i * tq
    jd = q0 // tk
    kpos = lax.broadcasted_iota(jnp.int32, (tk, tq), 0)
    qpos = lax.broadcasted_iota(jnp.int32, (tk, tq), 1)
    rows = lax.broadcasted_iota(jnp.int32, (LANES, tq), 0)
    base = [((b * N_PAIRS + hp) * 2 + hh) * nblk for hh in range(2)]

    def scores(j, masked):
        start = pl.multiple_of(j * tk, tk)
        col_max = []
        for hh in range(2):
            s = _dot(ka_ref[hh, pl.ds(start, tk), :], qt_ref[hh])
            if masked:
                s = jnp.where(kpos + start <= qpos + q0, s, NEG_INF)
            s_ref[hh] = s
            col_max.append(jnp.max(s, axis=0, keepdims=True))
        return tuple(col_max)

    def softmax_pv(j, col_max, m):
        start = pl.multiple_of(j * tk, tk)
        m_out = []
        for hh in range(2):
            c = (bpre_ref[base[hh] + jd] - bpre_ref[base[hh] + j]) * LOG2E
            m_new = jnp.maximum(m[hh], col_max[hh] + c)
            alpha = jnp.exp2(m[hh] - m_new)
            shift = m_new - c
            for ch in range(tk // ATT_CHUNK):
                sl = slice(ch * ATT_CHUNK, (ch + 1) * ATT_CHUNK)
                p_ref[hh, sl, :] = jnp.exp2((s_ref[hh, sl, :] - shift).astype(BF16))
            m_out.append((m_new, alpha))
        return tuple(m_out), start

    def accumulate(m_alpha, start):
        for hh in range(2):
            pv = _dot(vt_ref[hh, :, pl.ds(start, tk)], p_ref[hh])
            acc_ref[hh] = m_alpha[hh][1] * acc_ref[hh] + pv
        return tuple(ma[0] for ma in m_alpha)

    def step(j, next_masked, carry):
        col_max, m = carry
        m_alpha, start = softmax_pv(j, col_max, m)
        col_max_next = scores(j + 1, next_masked)
        return col_max_next, accumulate(m_alpha, start)

    acc_ref[...] = jnp.zeros_like(acc_ref)
    neg = jnp.full((1, tq), NEG_INF, F32)
    n_masked = max(1, tq // tk)
    carry = (scores(0, True), (neg, neg))
    carry = lax.fori_loop(0, jd - 1, lambda j, cr: step(j, False, cr), carry)
    carry = lax.cond(jd > 0, lambda cr: step(jd - 1, True, cr), lambda cr: cr, carry)
    for extra in range(n_masked - 1):
        carry = step(jd + extra, True, carry)
    col_max, m = carry
    m_alpha, start = softmax_pv(jd + n_masked - 1, col_max, m)
    accumulate(m_alpha, start)

    acc_a, acc_b = acc_ref[0], acc_ref[1]
    out_a = acc_a / acc_a[_aug_lane(0):_aug_lane(0) + 1, :]
    out_b = acc_b / acc_b[_aug_lane(1):_aug_lane(1) + 1, :]
    o_ref[0] = jnp.where(rows < HEAD_DIM, out_a, out_b).T


def _att_prompt(qt, ka, vt, bpre):
    n, _, _, t = qt.shape
    assert t % ATT_TQ == 0 and (ATT_TK % ATT_TQ == 0 or ATT_TQ % ATT_TK == 0) and ATT_TK == ROW_TILE
    pair = lambda q: (N_PAIRS, 2) + q.shape[2:]
    qt, ka, vt = (a.reshape((n,) + pair(a)) for a in (qt, ka, vt))
    bflat = jnp.transpose(bpre[:, :, 0, :N_HEADS], (0, 2, 1)).reshape(-1)
    return pl.pallas_call(
        _att_prompt_kernel,
        grid=(n, N_PAIRS, t // ATT_TQ),
        in_specs=[pl.BlockSpec(memory_space=pltpu.SMEM),
                  pl.BlockSpec((None, None, 2, LANES, ATT_TQ), lambda b, h, i: (b, h, 0, 0, i)),
                  pl.BlockSpec((None, None, 2, t, LANES), lambda b, h, i: (b, h, 0, 0, 0),
                               pipeline_mode=pl.Buffered(1)),
                  pl.BlockSpec((None, None, 2, LANES, t), lambda b, h, i: (b, h, 0, 0, 0),
                               pipeline_mode=pl.Buffered(1))],
        out_specs=pl.BlockSpec((1, ATT_TQ, PAIR_W), lambda b, h, i: (b, i, h)),
        out_shape=jax.ShapeDtypeStruct((n, t, WA), F32),
        scratch_shapes=[pltpu.VMEM((2, ATT_TK, ATT_TQ), F32), pltpu.VMEM((2, ATT_TK, ATT_TQ), BF16),
                        pltpu.VMEM((2, LANES, ATT_TQ), F32)],
        compiler_params=_params("parallel", "parallel", "arbitrary"),
        name="att_prompt",
    )(bflat, qt, ka, vt)


def _att_sample_kernel(q_ref, kp_ref, vp_ref, dp_ref, dend_ref, kn_ref, vn_ref, dn_ref, o_ref,
                       m_ref, l_ref, acc_ref):
    t = q_ref.shape[1]
    j = pl.program_id(1)

    @pl.when(j == 0)
    def _():
        m_ref[...] = jnp.full_like(m_ref, NEG_INF)
        l_ref[...] = jnp.zeros_like(l_ref)
        acc_ref[...] = jnp.zeros_like(acc_ref)

    def head(h, k, v, time_minor, bias, mask):
        sl = slice(h * HEAD_DIM, (h + 1) * HEAD_DIM)
        q_h = q_ref[0, :, sl]
        s = (_dot(q_h, k) if time_minor else _dot_nt(q_h, k)) + bias[h:h + 1]
        if mask is not None:
            s = jnp.where(mask, s, NEG_INF)
        p, alpha, m_new, l_new = _online_update(s, m_ref[h], l_ref[h])
        p = p.astype(BF16)
        acc = alpha * acc_ref[:, sl] + (_dot_nt(p, v) if time_minor else _dot(p, v))
        m_ref[h], l_ref[h], acc_ref[:, sl] = m_new, l_new, acc
        return acc / l_new

    bias_past = dend_ref[0] - dp_ref[0]
    for h in range(N_HEADS):
        head(h, kp_ref[0, h].astype(BF16), vp_ref[0, h].astype(BF16), True, bias_past, None)

    @pl.when(j == pl.num_programs(1) - 1)
    def _():
        bias_new = -dn_ref[0]
        qpos = lax.broadcasted_iota(jnp.int32, (t, t), 0)
        kpos = lax.broadcasted_iota(jnp.int32, (t, t), 1)
        for h in range(N_HEADS):
            sl = slice(h * HEAD_DIM, (h + 1) * HEAD_DIM)
            o_ref[0, :, sl] = head(h, kn_ref[0, :, sl], vn_ref[0, :, sl], False, bias_new, kpos <= qpos)


def _att_sample(qb, k_past, v_past, d_past, kb_new, vb_new, d_new):
    n, t, _ = qb.shape
    past = k_past.shape[1]
    tk = min(SAMPLE_TK, past)
    assert past % tk == 0
    d_end = d_past[:, :, past - 1:]
    k_past = jnp.transpose(k_past, (0, 2, 3, 1))
    v_past = jnp.transpose(v_past, (0, 2, 3, 1))
    new = lambda dt: pl.BlockSpec((1, t, WA), lambda b, j: (b, 0, 0))
    cache = pl.BlockSpec((1, N_HEADS, HEAD_DIM, tk), lambda b, j: (b, 0, 0, j))
    return pl.pallas_call(
        _att_sample_kernel,
        grid=(n, past // tk),
        in_specs=[new(BF16), cache, cache,
                  pl.BlockSpec((1, N_HEADS, tk), lambda b, j: (b, 0, j)),
                  pl.BlockSpec((1, N_HEADS, 1), lambda b, j: (b, 0, 0)),
                  new(BF16), new(BF16),
                  pl.BlockSpec((1, N_HEADS, t), lambda b, j: (b, 0, 0))],
        out_specs=pl.BlockSpec((1, t, WA), lambda b, j: (b, 0, 0)),
        out_shape=jax.ShapeDtypeStruct((n, t, WA), F32),
        scratch_shapes=[pltpu.VMEM((N_HEADS, t, 1), F32), pltpu.VMEM((N_HEADS, t, 1), F32),
                        pltpu.VMEM((t, WA), F32)],
        compiler_params=_params("parallel", "arbitrary"),
        name="att_sample",
    )(qb, k_past, v_past, d_past, d_end, kb_new, vb_new, d_new)


def _first_index_of_max(x, axis):
    mx = jnp.max(x, axis=axis, keepdims=True)
    idx = lax.broadcasted_iota(jnp.int32, x.shape, axis)
    first = jnp.min(jnp.where(x == mx, idx, x.shape[axis]), axis=axis, keepdims=True)
    return mx, idx == first


def _route(s, bias):
    m = s.shape[1]
    sb = (s + bias).reshape(N_GROUPS, GROUP_SIZE, m)
    top1, is_top1 = _first_index_of_max(sb, 1)
    top2 = jnp.max(jnp.where(is_top1, NEG_INF, sb), axis=1, keepdims=True)
    grp = (top1 + top2).reshape(N_GROUPS, m)
    gi = lax.broadcasted_iota(jnp.int32, (N_GROUPS, N_GROUPS, m), 0)
    gj = lax.broadcasted_iota(jnp.int32, (N_GROUPS, N_GROUPS, m), 1)
    other, mine = grp[None, :, :], grp[:, None, :]
    beats = (other > mine) | ((other == mine) & (gj < gi))
    g_rank = jnp.sum(beats.astype(jnp.int32), axis=1)
    g_keep = (g_rank < TOPK_GROUPS)[:, None, :]
    cand = jnp.where(g_keep, sb, NEG_INF).reshape(N_EXPERTS, m)
    picks = []
    for _ in range(TOP_K):
        _, pick = _first_index_of_max(cand, 0)
        picks.append(pick)
        cand = jnp.where(pick, NEG_INF, cand)
    w = jnp.concatenate([jnp.sum(jnp.where(pk, s, 0.0), axis=0, keepdims=True) for pk in picks], axis=0)
    return picks, w / jnp.sum(w, axis=0, keepdims=True) * ROUTED_SCALE


HALF_MASK = 0xFFFF0000


def _pack_halves(x):
    c = x.shape[1] // 2
    lo = lax.bitcast_convert_type(x[:, :c].astype(BF16).astype(F32), jnp.uint32) >> jnp.uint32(16)
    hi = lax.bitcast_convert_type(x[:, c:].astype(BF16).astype(F32), jnp.uint32) & jnp.uint32(HALF_MASK)
    return lax.bitcast_convert_type(lo | hi, jnp.int32)


def _unpack_halves(words):
    w = lax.bitcast_convert_type(words, jnp.uint32)
    lo = lax.bitcast_convert_type(w << jnp.uint32(16), F32)
    hi = lax.bitcast_convert_type(w & jnp.uint32(HALF_MASK), F32)
    return lo, hi


def _outproj_kernel(x_ref, lru_ref, att_ref, mod_ref, gatt_ref, wtop_ref, wbot_ref, gpost_ref, gpre_ref,
                    rwh_ref, rwl_ref, rb_ref, cnt_in_ref, before_ref,
                    x1_ref, hf_ref, xw_ref, ids_ref, ranks_ref, gates_ref, cnt_out_ref, carry_ref, *, split):
    nb, tt, d = x_ref.shape
    m = nb * tt
    first = (pl.program_id(0) == 0) & (pl.program_id(1) == 0)

    @pl.when(first)
    def _():
        carry_ref[...] = cnt_in_ref[...]

    if split is not None:
        @pl.when((pl.program_id(0) == split) & (pl.program_id(1) == 0))
        def _():
            carry_ref[...] = jnp.zeros_like(carry_ref)

    mod = mod_ref[...]
    att_n = _rms(att_ref[...], gatt_ref[...]).reshape(m, WA).astype(BF16)
    mix = _dot(lru_ref[...].reshape(m, WL), wtop_ref[...]) + _dot(att_n, wbot_ref[...])
    x1 = x_ref[...] + mod[:, 2:3, :] * _rms(mix, gpost_ref[...]).reshape(nb, tt, d)
    x1_ref[...] = x1
    hf = (_rms(x1, gpre_ref[...]) * (1.0 + mod[:, 4:5, :]) + mod[:, 3:4, :]).reshape(m, d)
    hf_hi = hf.astype(BF16)
    hf_ref[...] = hf_hi.reshape(nb, tt, d)
    hf_lo = (hf - hf_hi.astype(F32)).astype(BF16)
    rwh = rwh_ref[...]
    logits = _dot_nt(rwh, hf_hi) + _dot_nt(rwh, hf_lo) + _dot_nt(rwl_ref[...], hf_hi)
    picks, gates = _route(_sigmoid(logits), rb_ref[...])
    xw_ref[...] = _pack_halves(hf).reshape(nb, tt, d // 2)

    sel = jnp.zeros((N_EXPERTS, m), F32)
    for pk in picks:
        sel = sel + pk.astype(F32)
    prior = _dot(sel.astype(BF16), before_ref[...]) + carry_ref[...]
    expert = lax.broadcasted_iota(jnp.int32, (N_EXPERTS, m), 0).astype(F32)
    take = lambda pk, v: jnp.sum(jnp.where(pk, v, 0.0), axis=0, keepdims=True)
    ids_ref[...] = jnp.concatenate([take(pk, expert) for pk in picks], axis=0).astype(jnp.int32)
    ranks_ref[...] = jnp.concatenate([take(pk, prior) for pk in picks], axis=0).astype(jnp.int32)
    carry_ref[...] += jnp.sum(sel, axis=1, keepdims=True)
    cnt_out_ref[0] = carry_ref[...]
    gates = jnp.concatenate([gates, jnp.zeros((LANES - TOP_K, m), F32)], axis=0)
    gates_ref[...] = gates.T.reshape(nb, tt, LANES)


def _outproj(x, lru_n, att, mod, g_att, w_top, w_bot, g_post, g_pre, rw_hi, rw_lo, r_bias, cnt_in, split=None):
    n, t, d = x.shape
    nb, tt = _seq_blocks(n, t)
    m = nb * tt
    steps_t = t // tt
    n_groups = 1 if split is None else 2
    blk = lambda w: pl.BlockSpec((nb, tt, w), lambda i, j: (i, j, 0))
    const = lambda shape: pl.BlockSpec(shape, lambda i, j: (0,) * len(shape))
    per_tok = pl.BlockSpec((TOP_K, m), lambda i, j: (0, i * steps_t + j))
    group_of = (lambda i: 0) if split is None else (lambda i: jnp.where(i < split, 0, 1))
    before = jnp.asarray(np.triu(np.ones((m, m), np.float32), k=1), BF16)
    return pl.pallas_call(
        functools.partial(_outproj_kernel, split=split),
        grid=(n // nb, steps_t),
        in_specs=[blk(d), blk(WL), blk(WA), pl.BlockSpec((nb, 6, d), lambda i, j: (i, 0, 0)),
                  const((1, WA)), const((WL, d)), const((WA, d)), const((1, d)), const((1, d)),
                  const((N_EXPERTS, d)), const((N_EXPERTS, d)), const((N_EXPERTS, 1)), const((N_EXPERTS, 1)),
                  const((m, m))],
        out_specs=[blk(d), blk(d), blk(d // 2), per_tok, per_tok, blk(LANES),
                   pl.BlockSpec((1, N_EXPERTS, 1), lambda i, j: (group_of(i), 0, 0))],
        out_shape=[jax.ShapeDtypeStruct((n, t, d), F32), jax.ShapeDtypeStruct((n, t, d), BF16),
                   jax.ShapeDtypeStruct((n, t, d // 2), jnp.int32),
                   jax.ShapeDtypeStruct((TOP_K, n * t), jnp.int32), jax.ShapeDtypeStruct((TOP_K, n * t), jnp.int32),
                   jax.ShapeDtypeStruct((n, t, LANES), F32), jax.ShapeDtypeStruct((n_groups, N_EXPERTS, 1), F32)],
        scratch_shapes=[pltpu.VMEM((N_EXPERTS, 1), F32)],
        compiler_params=_params("arbitrary", "arbitrary"),
        name="outproj_router",
    )(x, lru_n, att, mod, g_att, w_top, w_bot, g_post, g_pre, rw_hi, rw_lo, r_bias, cnt_in, before)


def _subcore_ranges(n_items):
    info = plsc.get_sparse_core_info()
    n_workers = info.num_cores * info.num_subcores
    per_worker = n_items // n_workers
    assert per_worker * n_workers == n_items and per_worker % GATHER_ROWS == 0
    return info, plsc.VectorSubcoreMesh(core_axis_name="c", subcore_axis_name="s"), per_worker


def _scatter_rows(xw, pos, n_slots, row_offset=0):
    m_tot, c = pos.shape[1], xw.shape[1]
    assert row_offset % GATHER_ROWS == 0
    info, mesh, per_worker = _subcore_ranges(m_tot)
    pos_flat = pos.reshape(-1)

    @functools.partial(
        pl.kernel, mesh=mesh, out_type=jax.ShapeDtypeStruct((n_slots, c), jnp.int32),
        scratch_types=[pltpu.VMEM((GATHER_ROWS,), jnp.int32), pltpu.VMEM((GATHER_ROWS, c), jnp.int32),
                       pltpu.SemaphoreType.DMA])
    def scatter(xw_hbm, pos_hbm, out_hbm, idx_v, rows_v, sem):
        worker = lax.axis_index("s") * info.num_cores + lax.axis_index("c")
        base = worker * per_worker

        @pl.loop(0, per_worker // GATHER_ROWS)
        def _(step):
            off = pl.multiple_of(base + step * GATHER_ROWS, GATHER_ROWS)
            pltpu.sync_copy(xw_hbm.at[pl.ds(row_offset + off, GATHER_ROWS)], rows_v)
            for r in range(TOP_K):
                pltpu.sync_copy(pos_hbm.at[pl.ds(pl.multiple_of(r * m_tot + off, GATHER_ROWS), GATHER_ROWS)], idx_v)
                pltpu.async_copy(rows_v, out_hbm.at[idx_v], sem).wait()

    return scatter(xw, pos_flat)


def _swiglu_halves(lo, hi, wg, wu, wd):
    c = lo.shape[1]
    hg = _dot(lo, wg[:c]) + _dot(hi, wg[c:])
    hu = _dot(lo, wu[:c]) + _dot(hi, wu[c:])
    return _dot((_silu(hg) * hu).astype(BF16), wd)


def _expert_kernel(te_ref, valid_ref, x_ref, wg_ref, wu_ref, wd_ref, y_ref, wg_bf, wu_bf, wd_bf):
    i = pl.program_id(0)
    valid = valid_ref[i]

    @pl.when((i == 0) | (te_ref[i] != te_ref[jnp.maximum(i - 1, 0)]))
    def _():
        wg_bf[...] = wg_ref[0].astype(BF16)
        wu_bf[...] = wu_ref[0].astype(BF16)
        wd_bf[...] = wd_ref[0].astype(BF16)

    @pl.when(valid > 0)
    def _():
        w = x_ref[...]
        row = lax.broadcasted_iota(jnp.int32, w.shape, 0)
        lo, hi = _unpack_halves(jnp.where(row < valid, w, 0))
        y_ref[...] = _pack_halves(_swiglu_halves(lo.astype(BF16), hi.astype(BF16), wg_bf[...], wu_bf[...], wd_bf[...]))


def _experts(xs, tile_expert, tile_valid, wg, wu, wd):
    n_slots, c = xs.shape
    n_tiles = n_slots // EXPERT_TILE
    d = 2 * c
    rows = pl.BlockSpec((EXPERT_TILE, c), lambda i, te, tv: (i, 0))
    weight = lambda shape: pl.BlockSpec((1,) + shape, lambda i, te, tv: (te[i], 0, 0))
    return pl.pallas_call(
        _expert_kernel,
        grid_spec=pltpu.PrefetchScalarGridSpec(
            num_scalar_prefetch=2, grid=(n_tiles,),
            in_specs=[rows, weight((d, D_EXPERT)), weight((d, D_EXPERT)), weight((D_EXPERT, d))],
            out_specs=rows,
            scratch_shapes=[pltpu.VMEM((d, D_EXPERT), BF16), pltpu.VMEM((d, D_EXPERT), BF16),
                            pltpu.VMEM((D_EXPERT, d), BF16)]),
        out_shape=jax.ShapeDtypeStruct((n_slots, c), jnp.int32),
        compiler_params=_params("arbitrary"),
        name="moe_experts",
    )(tile_expert, tile_valid, xs, wg, wu, wd)


def _gather_rows(table, idx):
    b, c = idx.shape[0], table.shape[1]
    info, mesh, per_worker = _subcore_ranges(b)

    n_steps = per_worker // GATHER_ROWS
    assert n_steps % 2 == 0

    @functools.partial(
        pl.kernel, mesh=mesh, out_type=jax.ShapeDtypeStruct((b, c), jnp.int32),
        scratch_types=[pltpu.VMEM((2, GATHER_ROWS), jnp.int32), pltpu.VMEM((2, GATHER_ROWS, c), jnp.int32),
                       pltpu.SemaphoreType.DMA((2,))])
    def gather(table_hbm, idx_hbm, out_hbm, idx_v, rows_v, sems):
        worker = lax.axis_index("s") * info.num_cores + lax.axis_index("c")
        base = worker * per_worker

        def chunk(step):
            return pl.ds(pl.multiple_of(base + step * GATHER_ROWS, GATHER_ROWS), GATHER_ROWS)

        def stream(buf):
            return pltpu.make_async_copy(table_hbm.at[idx_v.at[buf]], rows_v.at[buf], sems.at[buf])

        def start(step, buf):
            pltpu.sync_copy(idx_hbm.at[chunk(step)], idx_v.at[buf])
            stream(buf).start()

        def finish(step, buf):
            stream(buf).wait()
            pltpu.sync_copy(rows_v.at[buf], out_hbm.at[chunk(step)])

        start(0, 0)

        @pl.loop(0, n_steps, step=2)
        def _(step):
            start(step + 1, 1)
            finish(step, 0)

            @pl.when(step + 2 < n_steps)
            def _():
                start(step + 2, 0)

            finish(step + 1, 1)

    return gather(table, idx)


def _combine_kernel(rows_ref, gates_ref, hf_ref, x1_ref, mod_ref, sg_ref, su_ref, sd_ref, gpost_ref, y_ref):
    nb, tt, d = hf_ref.shape
    m = nb * tt
    c = d // 2
    x = hf_ref[...].reshape(m, d)
    shared = _swiglu_halves(x[:, :c], x[:, c:], sg_ref[...], su_ref[...], sd_ref[...])
    gates = gates_ref[...].reshape(m, LANES)
    acc_lo = shared[:, :c]
    acc_hi = shared[:, c:]
    for r in range(TOP_K):
        lo, hi = _unpack_halves(rows_ref[r])
        g = gates[:, r:r + 1]
        acc_lo = acc_lo + g * lo
        acc_hi = acc_hi + g * hi
    z = _rms(jnp.concatenate([acc_lo, acc_hi], axis=1), gpost_ref[...]).reshape(nb, tt, d)
    y_ref[...] = x1_ref[...] + mod_ref[...][:, 5:6, :] * z


def _combine(rows, first_tile, gates_t, hf, x1, mod, sg, su, sd, g_post, blocks=None, y_prev=None):
    n, t, d = hf.shape
    nb, tt = _seq_blocks(n, t)
    m = nb * tt
    steps_t = t // tt
    b0, nblocks = (0, n // nb) if blocks is None else blocks
    blk = lambda w: pl.BlockSpec((nb, tt, w), lambda i, j: (b0 + i, j, 0))
    const = lambda shape: pl.BlockSpec(shape, lambda i, j: (0,) * len(shape))
    in_specs = [pl.BlockSpec((TOP_K, m, d // 2), lambda i, j: (0, first_tile + i * steps_t + j, 0)),
                blk(LANES), blk(d), blk(d), pl.BlockSpec((nb, 6, d), lambda i, j: (b0 + i, 0, 0)),
                const((d, D_EXPERT)), const((d, D_EXPERT)), const((D_EXPERT, d)), const((1, d))]
    args = (rows, gates_t, hf, x1, mod, sg, su, sd, g_post)
    kernel_fn, aliases = _combine_kernel, {}
    if y_prev is not None:
        in_specs.append(pl.BlockSpec(memory_space=pl.ANY))
        args += (y_prev,)
        aliases = {len(args) - 1: 0}
        kernel_fn = lambda *refs: _combine_kernel(*refs[:len(args) - 1], refs[-1])
    return pl.pallas_call(
        kernel_fn,
        grid=(nblocks, steps_t),
        in_specs=in_specs,
        out_specs=blk(d),
        out_shape=jax.ShapeDtypeStruct((n, t, d), F32),
        input_output_aliases=aliases,
        compiler_params=_params("parallel", "parallel"),
        name="moe_combine",
    )(*args)


def _slots_kernel(starts_ref, ids_ref, ranks_ref, pos_ref):
    ids = ids_ref[...]

    def add_start(e, pos):
        return pos + jnp.where(ids == e, starts_ref[e], 0)

    pos_ref[...] = lax.fori_loop(0, N_EXPERTS, add_start, ranks_ref[...])


def _slots(starts, ids, ranks):
    k, m_tot = ids.shape
    cols = math.gcd(m_tot, SLOT_COLS)
    assert cols % LANES == 0
    blk = pl.BlockSpec((k, cols), lambda i: (0, i))
    return pl.pallas_call(
        _slots_kernel,
        grid=(m_tot // cols,),
        in_specs=[pl.BlockSpec(memory_space=pltpu.SMEM), blk, blk],
        out_specs=blk,
        out_shape=jax.ShapeDtypeStruct((k, m_tot), jnp.int32),
        compiler_params=_params("arbitrary"),
        name="moe_slots",
    )(starts, ids, ranks)


def _slot_plan(ids, ranks, counts):
    n_pairs = ids.shape[0] * ids.shape[1]
    n_tiles = -(-(n_pairs + N_EXPERTS * (EXPERT_TILE - 1)) // EXPERT_TILE)
    cnt = counts.reshape(N_EXPERTS).astype(jnp.int32)
    padded = (cnt + EXPERT_TILE - 1) // EXPERT_TILE * EXPERT_TILE
    ends = jnp.cumsum(padded)
    starts = ends - padded
    pos = _slots(starts, ids, ranks)
    tile_start = jnp.arange(n_tiles, dtype=jnp.int32) * EXPERT_TILE
    in_expert = (tile_start[:, None] >= starts[None, :]) & (tile_start[:, None] < ends[None, :])
    tile_expert = jnp.sum(jnp.where(in_expert, jnp.arange(N_EXPERTS, dtype=jnp.int32)[None, :], 0), axis=1)
    tile_fill = jnp.sum(jnp.where(in_expert, (starts + cnt)[None, :] - tile_start[:, None], 0), axis=1)
    tile_valid = jnp.clip(tile_fill, 0, EXPERT_TILE).astype(jnp.int32)
    return pos, tile_expert.astype(jnp.int32), tile_valid, n_tiles * EXPERT_TILE


def _block_diag(w):
    g, bw, _ = w.shape
    eye = jnp.eye(g, dtype=w.dtype)
    return (eye[:, None, :, None] * w[:, :, None, :]).reshape(g * bw, g * bw)


def _prep_weights(p):
    d_main = 2 * WL + 3 * WA
    w_in = p["w_in"]
    rw_t = p["router_w"].T
    rw_hi = rw_t.astype(BF16)
    row = lambda v: v.reshape(1, -1)
    return dict(
        w_mod=p["w_mod"], b_mod=p["b_mod"],
        g_pre_mix=row(p["g_pre_mix"]), g_post_mix=row(p["g_post_mix"]),
        g_pre_ffn=row(p["g_pre_ffn"]), g_post_ffn=row(p["g_post_ffn"]),
        w_main=w_in[:, :d_main].astype(BF16),
        w_f=jnp.pad(w_in[:, d_main:], ((0, 0), (0, LANES - N_HEADS))).astype(BF16),
        b_f=jnp.pad(p["b_f"], (0, LANES - N_HEADS)).reshape(1, LANES),
        conv_w=p["conv_w"], conv_b=row(p["conv_b"]),
        wr_bd=_block_diag(p["w_r"]).astype(BF16), b_r=row(p["b_r"]),
        wi_bd=_block_diag(p["w_i"]).astype(BF16), b_i=row(p["b_i"]),
        lam=row(p["lru_lambda"]), g_lru=row(p["g_lru_out"]), g_att=row(p["g_att_out"]),
        w_top=p["w_out"][:WL].astype(BF16), w_bot=p["w_out"][WL:].astype(BF16),
        rw_hi=rw_hi, rw_lo=(rw_t - rw_hi.astype(F32)).astype(BF16),
        r_bias=p["router_bias"].reshape(N_EXPERTS, 1),
        wg=p["w_gate"], wu=p["w_up"], wd=p["w_down"],
        sg=p["ws_gate"].astype(BF16), su=p["ws_up"].astype(BF16), sd=p["ws_down"].astype(BF16),
    )


def _mixers(x, mod, conv0, h0, past, w, cnt_in, split=None):
    n, t, _ = x.shape
    proj_args = (x, mod, w["g_pre_mix"], w["w_main"], w["w_f"], w["b_f"])
    if past is None:
        xl, gy, k, v, lf, qt, ka, vt, bpre = _inproj_prompt(*proj_args)
        att = _att_prompt(qt, ka, vt, bpre)
    else:
        xl, gy, qb, kb, vb, k, v, lf = _inproj(*proj_args)
        k_past, v_past, lf_past = past
        plen = k_past.shape[1]
        by_head = lambda a: jnp.transpose(a, (0, 2, 1)).reshape(n * N_HEADS, a.shape[1])
        d_new = _cumsum_rows(by_head(lf), t).reshape(n, N_HEADS, t)
        d_past = _cumsum_rows(by_head(lf_past), min(CUMSUM_COLS, plen)).reshape(n, N_HEADS, plen)
        att = _att_sample(qb, k_past, v_past, d_past, kb, vb, d_new)
    lru_n, conv_new, h_new = _lru(xl, gy, conv0, h0.reshape(n, 1, WL), w["conv_w"], w["conv_b"],
                                  w["wr_bd"], w["b_r"], w["wi_bd"], w["b_i"], w["lam"], w["g_lru"])
    routed = _outproj(x, lru_n, att, mod, w["g_att"], w["w_top"], w["w_bot"], w["g_post_mix"],
                      w["g_pre_ffn"], w["rw_hi"], w["rw_lo"], w["r_bias"], cnt_in, split)
    state = (k.reshape(n, t, N_HEADS, HEAD_DIM), v.reshape(n, t, N_HEADS, HEAD_DIM), lf,
             conv_new, h_new.reshape(n, WL))
    return routed, state


def _layer(xp, xs, mod_p, mod_s, conv_s, h_s, past_s, w):
    n_p = xp.shape[0]
    conv0 = jnp.zeros((n_p, CONV_W - 1, WL), F32)
    h0 = jnp.zeros((n_p, WL), F32)
    zero_cnt = jnp.zeros((N_EXPERTS, 1), F32)
    t_p = xp.shape[1]
    split = 1 if n_p > 1 else None
    (x1_p, hf_p, xw_p, ids_p, rk_p, g_p, cnt_p), st_p = _mixers(xp, mod_p, conv0, h0, None, w, zero_cnt, split)
    cnt_tail = cnt_p[1] if n_p > 1 else zero_cnt
    (x1_s, hf_s, xw_s, ids_s, rk_s, g_s, cnt_s), st_s = _mixers(xs, mod_s, conv_s, h_s, past_s, w, cnt_tail)

    half = xw_p.shape[-1]
    xw_p = xw_p.reshape(-1, half)

    def routed(xw, row_offset, ids, ranks, counts):
        pos, tile_expert, tile_valid, n_slots = _slot_plan(ids, ranks, counts)
        ys = _experts(_scatter_rows(xw, pos, n_slots, row_offset), tile_expert, tile_valid,
                      w["wg"], w["wu"], w["wd"])
        return _gather_rows(ys, pos.reshape(-1)).reshape(TOP_K, ids.shape[1], half)

    rows_a = routed(xw_p, 0, ids_p[:, :t_p], rk_p[:, :t_p], cnt_p[0])
    xw_b = jnp.concatenate([xw_p[t_p:], xw_s.reshape(-1, half)], axis=0)
    rows_b = routed(xw_b, 0, jnp.concatenate([ids_p[:, t_p:], ids_s], axis=1),
                    jnp.concatenate([rk_p[:, t_p:], rk_s], axis=1), cnt_s[0])
    shared = (w["sg"], w["su"], w["sd"], w["g_post_ffn"])
    yp = _combine(rows_a, 0, g_p, hf_p, x1_p, mod_p, *shared, blocks=(0, 1))
    if n_p > 1:
        yp = _combine(rows_b, 0, g_p, hf_p, x1_p, mod_p, *shared, blocks=(1, n_p - 1), y_prev=yp)
    ysmp = _combine(rows_b, (n_p - 1) * t_p // ROW_TILE, g_s, hf_s, x1_s, mod_s, *shared)
    return yp, ysmp, st_p, st_s


def kernel(x_prompt, x_sample, c_prompt, c_sample, cache_k, cache_v, cache_logf, state_conv, state_lru, w_mod, b_mod, g_pre_mix, g_post_mix, g_pre_ffn, g_post_ffn, w_in, conv_w, conv_b, w_r, b_r, w_i, b_i, lru_lambda, b_f, g_lru_out, g_att_out, w_out, router_w, router_bias, w_gate, w_up, w_down, ws_gate, ws_up, ws_down):
    names = ("w_mod", "b_mod", "g_pre_mix", "g_post_mix", "g_pre_ffn", "g_post_ffn", "w_in", "conv_w", "conv_b",
             "w_r", "b_r", "w_i", "b_i", "lru_lambda", "b_f", "g_lru_out", "g_att_out", "w_out", "router_w",
             "router_bias", "w_gate", "w_up", "w_down", "ws_gate", "ws_up", "ws_down")
    stacked = (w_mod, b_mod, g_pre_mix, g_post_mix, g_pre_ffn, g_post_ffn, w_in, conv_w, conv_b, w_r, b_r, w_i, b_i,
               lru_lambda, b_f, g_lru_out, g_att_out, w_out, router_w, router_bias, w_gate, w_up, w_down,
               ws_gate, ws_up, ws_down)
    depth = w_mod.shape[0]
    n_p, n_s = x_prompt.shape[0], x_sample.shape[0]
    yp, ys = x_prompt, x_sample
    st_p, st_s = [], []
    for l in range(depth):
        w = _prep_weights({k: v[l] for k, v in zip(names, stacked)})
        mod = _modulation(jnp.concatenate([c_prompt, c_sample], axis=0), w["w_mod"], w["b_mod"])
        mod = mod.reshape(n_p + n_s, 6, D_MODEL)
        yp, ys, sp, ss = _layer(yp, ys, mod[:n_p], mod[n_p:], state_conv[l], state_lru[l],
                                (cache_k[l], cache_v[l], cache_logf[l]), w)
        st_p.append(sp)
        st_s.append(ss)
    stack = lambda sts, i: jnp.stack([s[i] for s in sts])
    return (yp, ys) + tuple(stack(st_p, i) for i in range(5)) + tuple(stack(st_s, i) for i in range(5))
```

```python
import functools
import math

import jax
import jax.numpy as jnp
import numpy as np
from jax import lax
from jax.experimental import pallas as pl
from jax.experimental.pallas import tpu as pltpu
from jax.experimental.pallas import tpu_sc as plsc

F32 = jnp.float32
BF16 = jnp.bfloat16

D_MODEL = 1024
WL = 512
WA = 512
N_HEADS = 8
HEAD_DIM = 64
N_PAIRS = N_HEADS // 2
PAIR_W = 2 * HEAD_DIM
LANES = 128
CONV_W = 4
LRU_BLOCKS = 8
LRU_C = 8.0
N_EXPERTS = 64
N_GROUPS = 8
GROUP_SIZE = N_EXPERTS // N_GROUPS
TOPK_GROUPS = 4
TOP_K = 8
D_EXPERT = 256
ROUTED_SCALE = 2.5
EPS = 1e-6
NEG_INF = float("-inf")
LOG2E = 1.4426950408889634

ROW_TILE = 512
ATT_TQ = 512
ATT_TK = 512
ATT_CHUNK = 64
ATT_HEADS = 4
SAMPLE_TK = 4096
CUMSUM_COLS = 1024
EXPERT_TILE = 512
SLOT_COLS = 2048
GATHER_ROWS = 64
VMEM_LIMIT = 56 * 1024 * 1024


def _params(*sem):
    return pltpu.CompilerParams(dimension_semantics=sem, vmem_limit_bytes=VMEM_LIMIT)


def _dot(a, b):
    return jnp.dot(a, b, preferred_element_type=F32)


def _dot_nt(a, b):
    return lax.dot_general(a, b, (((1,), (1,)), ((), ())), preferred_element_type=F32)


def _split3(x):
    hi = x.astype(BF16)
    r1 = x - hi.astype(F32)
    mid = r1.astype(BF16)
    lo = (r1 - mid.astype(F32)).astype(BF16)
    return hi, mid, lo


def _rms(x, g):
    return x * lax.rsqrt(jnp.mean(x * x, axis=-1, keepdims=True) + EPS) * g


def _sigmoid(x):
    return 1.0 / (1.0 + jnp.exp(-x))


def _silu(x):
    return x * _sigmoid(x)


def _gelu_tanh(x):
    return 0.5 * x * (1.0 + jnp.tanh(0.7978845608028654 * (x + 0.044715 * (x * x * x))))


def _log_sigmoid(x):
    return jnp.minimum(x, 0.0) - jnp.log1p(jnp.exp(-jnp.abs(x)))


def _seq_blocks(n, t):
    if t >= ROW_TILE:
        assert t % ROW_TILE == 0
        return 1, ROW_TILE
    nb = ROW_TILE // t
    assert nb * t == ROW_TILE and n % nb == 0
    return nb, t


def _mod_kernel(c_ref, w_ref, b_ref, o_ref):
    c = _silu(c_ref[...])
    c_hi = c.astype(BF16)
    c_lo = (c - c_hi.astype(F32)).astype(BF16)
    w = w_ref[...]
    w_hi = w.astype(BF16)
    w_lo = (w - w_hi.astype(F32)).astype(BF16)
    o_ref[...] = _dot(c_hi, w_hi) + _dot(c_lo, w_hi) + _dot(c_hi, w_lo) + b_ref[...]


def _modulation(c, w_mod, b_mod):
    rows = c.shape[0]
    n = -(-rows // 8) * 8
    c = jnp.pad(c, ((0, n - rows), (0, 0)))
    d6 = w_mod.shape[1]
    return pl.pallas_call(
        _mod_kernel,
        grid=(d6 // D_MODEL,),
        in_specs=[pl.BlockSpec((n, D_MODEL), lambda j: (0, 0)),
                  pl.BlockSpec((D_MODEL, D_MODEL), lambda j: (0, j)),
                  pl.BlockSpec((1, D_MODEL), lambda j: (0, j))],
        out_specs=pl.BlockSpec((n, D_MODEL), lambda j: (0, j)),
        out_shape=jax.ShapeDtypeStruct((n, d6), F32),
        compiler_params=_params("arbitrary"),
        name="modulation",
    )(c, w_mod, b_mod.reshape(1, d6))[:rows]


def _inproj_kernel(x_ref, mod_ref, g_ref, w_ref, wf_ref, bf_ref,
                   xl_ref, gy_ref, qb_ref, kb_ref, vb_ref, k_ref, v_ref, lf_ref):
    nb, tt, d = x_ref.shape
    x = x_ref[...]
    mod = mod_ref[...]
    hn = _rms(x, g_ref[...]) * (1.0 + mod[:, 1:2, :]) + mod[:, 0:1, :]
    hb = hn.reshape(nb * tt, d).astype(BF16)

    def proj(col):
        return _dot(hb, w_ref[:, col * WL:(col + 1) * WL]).reshape(nb, tt, WL)

    xl_ref[...] = proj(0)
    gy_ref[...] = _gelu_tanh(proj(1)).astype(BF16)
    qb_ref[...] = (proj(2) * (HEAD_DIM ** -0.5)).astype(BF16)
    k = proj(3)
    k_ref[...] = k
    kb_ref[...] = k.astype(BF16)
    v = proj(4)
    v_ref[...] = v
    vb_ref[...] = v.astype(BF16)
    fl = _dot(hb, wf_ref[...]) + bf_ref[...]
    lf_ref[...] = _log_sigmoid(fl).reshape(nb, tt, LANES)[:, :, :N_HEADS]


def _inproj(x, mod, g_pre, w_main, w_f, b_f):
    n, t, d = x.shape
    nb, tt = _seq_blocks(n, t)
    blk = lambda w: pl.BlockSpec((nb, tt, w), lambda i, j: (i, j, 0))
    const = lambda shape: pl.BlockSpec(shape, lambda i, j: (0,) * len(shape))
    f32 = lambda w: jax.ShapeDtypeStruct((n, t, w), F32)
    b16 = lambda w: jax.ShapeDtypeStruct((n, t, w), BF16)
    return pl.pallas_call(
        _inproj_kernel,
        grid=(n // nb, t // tt),
        in_specs=[blk(d),
                  pl.BlockSpec((nb, 6, d), lambda i, j: (i, 0, 0)),
                  const((1, d)), const(w_main.shape), const(w_f.shape), const((1, LANES))],
        out_specs=[blk(WL), blk(WL), blk(WA), blk(WA), blk(WA), blk(WA), blk(WA), blk(N_HEADS)],
        out_shape=[f32(WL), b16(WL), b16(WA), b16(WA), b16(WA), f32(WA), f32(WA), f32(N_HEADS)],
        compiler_params=_params("parallel", "arbitrary"),
        name="inproj",
    )(x, mod, g_pre, w_main, w_f, b_f)


def _aug_lane(h):
    return HEAD_DIM if h % 2 == 0 else 0


def _inproj_prompt_kernel(x_ref, mod_ref, g_ref, w_ref, wf_ref, bf_ref, place_ref,
                          xl_ref, gy_ref, k_ref, v_ref, lf_ref, qt_ref, ka_ref, vt_ref, bpre_ref, carry_ref):
    _, tt, d = x_ref.shape

    @pl.when(pl.program_id(1) == 0)
    def _():
        carry_ref[...] = jnp.zeros_like(carry_ref)

    mod = mod_ref[0]
    hb = (_rms(x_ref[0], g_ref[...]) * (1.0 + mod[1:2, :]) + mod[0:1, :]).astype(BF16)

    def proj(col):
        return _dot(hb, w_ref[:, col * WL:(col + 1) * WL])

    xl_ref[0] = proj(0)
    gy_ref[0] = _gelu_tanh(proj(1)).astype(BF16)
    q = proj(2) * (HEAD_DIM ** -0.5 * LOG2E)
    k = proj(3)
    k_ref[0] = k
    v = proj(4)
    v_ref[0] = v
    fl = _dot(hb, wf_ref[...]) + bf_ref[...]
    lane = lax.broadcasted_iota(jnp.int32, (tt, LANES), 1)
    lf = jnp.where(lane < N_HEADS, _log_sigmoid(fl), 0.0)
    lf_ref[0] = lf[:, :N_HEADS]

    row = lax.broadcasted_iota(jnp.int32, (tt, tt), 0)
    col = lax.broadcasted_iota(jnp.int32, (tt, tt), 1)
    tril = (col <= row).astype(BF16)
    hi, mid, lo = _split3(lf)
    e = _dot(tril, hi) + _dot(tril, mid) + _dot(tril, lo)
    bpre_ref[0, 0] = carry_ref[...]
    carry_ref[...] += e[tt - 1:tt, :]
    e_hi, e_mid, e_lo = _split3(e * (-LOG2E))
    aug_k = _dot(jnp.concatenate([e_hi, e_mid, e_lo], axis=1), place_ref[...])

    for h in range(N_HEADS):
        pair = slice((h // 2) * PAIR_W, (h // 2 + 1) * PAIR_W)
        dims = (lane < HEAD_DIM) if h % 2 == 0 else (lane >= HEAD_DIM)
        a0 = _aug_lane(h)
        ones3 = ((lane >= a0) & (lane < a0 + 3)).astype(F32)
        qt_ref[0, h] = (jnp.where(dims, q[:, pair], 0.0) + ones3).T.astype(BF16)
        ka_ref[0, h] = (jnp.where(dims, k[:, pair], 0.0) + aug_k[:, h * LANES:(h + 1) * LANES]).astype(BF16)
        one1 = (lane == a0).astype(F32)
        vt_ref[0, h] = (jnp.where(dims, v[:, pair], 0.0) + one1).T.astype(BF16)


def _placement():
    pl_mat = np.zeros((3 * LANES, N_HEADS * LANES), np.float32)
    for p in range(3):
        for h in range(N_HEADS):
            pl_mat[p * LANES + h, h * LANES + _aug_lane(h) + p] = 1.0
    return jnp.asarray(pl_mat, BF16)


def _inproj_prompt(x, mod, g_pre, w_main, w_f, b_f):
    n, t, d = x.shape
    tt = ROW_TILE
    assert t % tt == 0
    nblk = t // tt
    blk = lambda w: pl.BlockSpec((1, tt, w), lambda i, j: (i, j, 0))
    const = lambda shape: pl.BlockSpec(shape, lambda i, j: (0,) * len(shape))
    f32 = lambda w: jax.ShapeDtypeStruct((n, t, w), F32)
    place = _placement()
    return pl.pallas_call(
        _inproj_prompt_kernel,
        grid=(n, nblk),
        in_specs=[blk(d), pl.BlockSpec((1, 6, d), lambda i, j: (i, 0, 0)),
                  const((1, d)), const(w_main.shape), const(w_f.shape), const((1, LANES)), const(place.shape)],
        out_specs=[blk(WL), blk(WL), blk(WA), blk(WA), blk(N_HEADS),
                   pl.BlockSpec((1, N_HEADS, LANES, tt), lambda i, j: (i, 0, 0, j)),
                   pl.BlockSpec((1, N_HEADS, tt, LANES), lambda i, j: (i, 0, j, 0)),
                   pl.BlockSpec((1, N_HEADS, LANES, tt), lambda i, j: (i, 0, 0, j)),
                   pl.BlockSpec((1, 1, 1, LANES), lambda i, j: (i, j, 0, 0))],
        out_shape=[f32(WL), jax.ShapeDtypeStruct((n, t, WL), BF16), f32(WA), f32(WA), f32(N_HEADS),
                   jax.ShapeDtypeStruct((n, N_HEADS, LANES, t), BF16),
                   jax.ShapeDtypeStruct((n, N_HEADS, t, LANES), BF16),
                   jax.ShapeDtypeStruct((n, N_HEADS, LANES, t), BF16),
                   jax.ShapeDtypeStruct((n, nblk, 1, LANES), F32)],
        scratch_shapes=[pltpu.VMEM((1, LANES), F32)],
        compiler_params=_params("parallel", "arbitrary"),
        name="inproj_prompt",
    )(x, mod, g_pre, w_main, w_f, b_f, place)


def _expm1_neg(x):
    poly = x * (1.0 + x * (0.5 + x * (1.0 / 6.0 + x * (1.0 / 24.0 + x * (1.0 / 120.0)))))
    return jnp.where(x > -0.1, poly, jnp.exp(x) - 1.0)


def _lru_kernel(xl_ref, gy_ref, conv0_ref, h0_ref, cw_ref, cb_ref, wr_ref, br_ref, wi_ref, bi_ref,
                lam_ref, g_ref, out_ref, conv_ref, hlast_ref, tail_ref, carry_ref):
    nb, tt, w = xl_ref.shape
    j = pl.program_id(1)

    @pl.when(j == 0)
    def _():
        tail_ref[:, 8 - (CONV_W - 1):, :] = conv0_ref[...]
        carry_ref[...] = h0_ref[...]

    xl = xl_ref[...]
    xpad = jnp.concatenate([tail_ref[...], xl], axis=1)
    cw = cw_ref[...]
    xc = jnp.zeros_like(xl) + cb_ref[...]
    for k in range(CONV_W):
        off = 8 - (CONV_W - 1) + k
        xc = xc + xpad[:, off:off + tt, :] * cw[k:k + 1, :]
    conv_ref[...] = xpad[:, tt + 8 - (CONV_W - 1):, :]
    tail_ref[...] = xpad[:, tt:, :]

    m = nb * tt
    xf = xc.reshape(m, w)
    xb = xf.astype(BF16)
    r = _sigmoid(_dot(xb, wr_ref[...]) + br_ref[...])
    gi = _sigmoid(_dot(xb, wi_ref[...]) + bi_ref[...])
    lam = lam_ref[...]
    softplus = jnp.maximum(-lam, 0.0) + jnp.log1p(jnp.exp(-jnp.abs(lam)))
    log_a = (-LRU_C) * r * softplus
    a = jnp.exp(log_a)
    b = jnp.sqrt(-_expm1_neg(2.0 * log_a)) * (gi * xf)

    groups = m // 8
    a = a.reshape(groups, 8, w)
    b = b.reshape(groups, 8, w)
    sub = lax.broadcasted_iota(jnp.int32, (groups, 8, w), 1)
    for d in (1, 2, 4):
        keep = sub >= d
        a_prev = jnp.where(keep, pltpu.roll(a, d, 1), 1.0)
        b_prev = jnp.where(keep, pltpu.roll(b, d, 1), 0.0)
        b = a * b_prev + b
        a = a * a_prev
    carry = carry_ref[...]
    groups_per_seq = tt // 8
    rows = []
    for g in range(groups):
        if g % groups_per_seq == 0:
            prev = carry[g // groups_per_seq]
        h_g = a[g] * prev + b[g]
        prev = h_g[7:8]
        rows.append(h_g)
    h = jnp.concatenate(rows, axis=0).reshape(nb, tt, w)
    h_last = h[:, tt - 1:tt, :]
    carry_ref[...] = h_last
    hlast_ref[...] = h_last
    out_ref[...] = _rms(h * gy_ref[...].astype(F32), g_ref[...]).astype(BF16)


def _lru(xl, gy, conv0, h0, conv_w, conv_b, wr_bd, b_r, wi_bd, b_i, lam, g_lru):
    n, t, w = xl.shape
    nb, tt = _seq_blocks(n, t)
    blk = pl.BlockSpec((nb, tt, w), lambda i, j: (i, j, 0))
    per_seq = lambda rows: pl.BlockSpec((nb, rows, w), lambda i, j: (i, 0, 0))
    const = lambda shape: pl.BlockSpec(shape, lambda i, j: (0,) * len(shape))
    row = const((1, w))
    return pl.pallas_call(
        _lru_kernel,
        grid=(n // nb, t // tt),
        in_specs=[blk, blk, per_seq(CONV_W - 1), per_seq(1),
                  const((CONV_W, w)), row, const((w, w)), row, const((w, w)), row, row, row],
        out_specs=[blk, per_seq(CONV_W - 1), per_seq(1)],
        out_shape=[jax.ShapeDtypeStruct((n, t, w), BF16),
                   jax.ShapeDtypeStruct((n, CONV_W - 1, w), F32),
                   jax.ShapeDtypeStruct((n, 1, w), F32)],
        scratch_shapes=[pltpu.VMEM((nb, 8, w), F32), pltpu.VMEM((nb, 1, w), F32)],
        compiler_params=_params("parallel", "arbitrary"),
        name="rglru",
    )(xl, gy, conv0, h0, conv_w, conv_b, wr_bd, b_r, wi_bd, b_i, lam, g_lru)


def _cumsum_rows_kernel(x_ref, upper_ref, o_ref, carry_ref):
    tb = x_ref.shape[1]

    @pl.when(pl.program_id(0) == 0)
    def _():
        carry_ref[...] = jnp.zeros_like(carry_ref)

    upper = upper_ref[...]
    hi, mid, lo = _split3(x_ref[...])
    d = _dot(hi, upper) + _dot(mid, upper) + _dot(lo, upper) + carry_ref[...]
    carry_ref[...] = d[:, tb - 1:tb]
    o_ref[...] = d


def _cumsum_rows(x, tb):
    rows, t = x.shape
    upper = jnp.asarray(np.triu(np.ones((tb, tb), np.float32)), BF16)
    return pl.pallas_call(
        _cumsum_rows_kernel,
        grid=(t // tb,),
        in_specs=[pl.BlockSpec((rows, tb), lambda j: (0, j)), pl.BlockSpec((tb, tb), lambda j: (0, 0))],
        out_specs=pl.BlockSpec((rows, tb), lambda j: (0, j)),
        out_shape=jax.ShapeDtypeStruct((rows, t), F32),
        scratch_shapes=[pltpu.VMEM((rows, 1), F32)],
        compiler_params=_params("arbitrary"),
        name="logf_cumsum",
    )(x, upper)


def _online_update(s, m_prev, l_prev):
    m_new = jnp.maximum(m_prev, jnp.max(s, axis=1, keepdims=True))
    alpha = jnp.exp(m_prev - m_new)
    p = jnp.exp(s - m_new)
    l_new = alpha * l_prev + jnp.sum(p, axis=1, keepdims=True)
    return p, alpha, m_new, l_new


def _att_prompt_kernel(bpre_ref, qt_ref, ka_ref, vt_ref, o_ref, s_ref, p_ref, acc_ref):
    tq = qt_ref.shape[2]
    tk = ATT_TK
    nblk = ka_ref.shape[1] // tk
    b, hp, i = pl.program_id(0), pl.program_id(1), pl.program_id(2)
    q0 = i * tq
    jd = q0 // tk
    kpos = lax.broadcasted_iota(jnp.int32, (tk, tq), 0)
    qpos = lax.broadcasted_iota(jnp.int32, (tk, tq), 1)
    rows = lax.broadcasted_iota(jnp.int32, (LANES, tq), 0)
    nh = qt_ref.shape[0]
    heads = range(nh)
    base = [((b * (N_HEADS // nh) + hp) * nh + hh) * nblk for hh in heads]

    def scores(j, masked):
        start = pl.multiple_of(j * tk, tk)
        col_max = []
        for hh in heads:
            s = _dot(ka_ref[hh, pl.ds(start, tk), :], qt_ref[hh])
            if masked:
                s = jnp.where(kpos + start <= qpos + q0, s, NEG_INF)
            s_ref[hh] = s
            col_max.append(jnp.max(s, axis=0, keepdims=True))
        return tuple(col_max)

    def softmax_pv(j, col_max, m):
        start = pl.multiple_of(j * tk, tk)
        m_out = []
        for hh in heads:
            c = (bpre_ref[base[hh] + jd] - bpre_ref[base[hh] + j]) * LOG2E
            m_new = jnp.maximum(m[hh], col_max[hh] + c)
            alpha = jnp.exp2(m[hh] - m_new)
            shift = m_new - c
            for ch in range(tk // ATT_CHUNK):
                sl = slice(ch * ATT_CHUNK, (ch + 1) * ATT_CHUNK)
                p_ref[hh, sl, :] = jnp.exp2(s_ref[hh, sl, :] - shift).astype(BF16)
            m_out.append((m_new, alpha))
        return tuple(m_out), start

    def accumulate(m_alpha, start):
        for hh in heads:
            pv = _dot(vt_ref[hh, :, pl.ds(start, tk)], p_ref[hh])
            acc_ref[hh] = m_alpha[hh][1] * acc_ref[hh] + pv
        return tuple(ma[0] for ma in m_alpha)

    def step(j, next_masked, carry):
        col_max, m = carry
        m_alpha, start = softmax_pv(j, col_max, m)
        col_max_next = scores(j + 1, next_masked)
        return col_max_next, accumulate(m_alpha, start)

    acc_ref[...] = jnp.zeros_like(acc_ref)
    neg = jnp.full((1, tq), NEG_INF, F32)
    n_masked = max(1, tq // tk)
    carry = (scores(0, True), (neg,) * nh)
    carry = lax.fori_loop(0, jd - 1, lambda j, cr: step(j, False, cr), carry)
    carry = lax.cond(jd > 0, lambda cr: step(jd - 1, True, cr), lambda cr: cr, carry)
    for extra in range(n_masked - 1):
        carry = step(jd + extra, True, carry)
    col_max, m = carry
    m_alpha, start = softmax_pv(jd + n_masked - 1, col_max, m)
    accumulate(m_alpha, start)

    for pair in range(nh // 2):
        acc_a, acc_b = acc_ref[2 * pair], acc_ref[2 * pair + 1]
        out_a = acc_a / acc_a[_aug_lane(0):_aug_lane(0) + 1, :]
        out_b = acc_b / acc_b[_aug_lane(1):_aug_lane(1) + 1, :]
        o_ref[0, :, pair * PAIR_W:(pair + 1) * PAIR_W] = jnp.where(rows < HEAD_DIM, out_a, out_b).T


def _att_prompt(qt, ka, vt, bpre):
    n, _, _, t = qt.shape
    assert t % ATT_TQ == 0 and (ATT_TK % ATT_TQ == 0 or ATT_TQ % ATT_TK == 0) and ATT_TK == ROW_TILE
    nh = ATT_HEADS
    grouped = lambda q: (N_HEADS // nh, nh) + q.shape[2:]
    qt, ka, vt = (a.reshape((n,) + grouped(a)) for a in (qt, ka, vt))
    bflat = jnp.transpose(bpre[:, :, 0, :N_HEADS], (0, 2, 1)).reshape(-1)
    return pl.pallas_call(
        _att_prompt_kernel,
        grid=(n, N_HEADS // nh, t // ATT_TQ),
        in_specs=[pl.BlockSpec(memory_space=pltpu.SMEM),
                  pl.BlockSpec((None, None, nh, LANES, ATT_TQ), lambda b, h, i: (b, h, 0, 0, i)),
                  pl.BlockSpec((None, None, nh, t, LANES), lambda b, h, i: (b, h, 0, 0, 0),
                               pipeline_mode=pl.Buffered(1)),
                  pl.BlockSpec((None, None, nh, LANES, t), lambda b, h, i: (b, h, 0, 0, 0),
                               pipeline_mode=pl.Buffered(1))],
        out_specs=pl.BlockSpec((1, ATT_TQ, nh * HEAD_DIM), lambda b, h, i: (b, i, h)),
        out_shape=jax.ShapeDtypeStruct((n, t, WA), F32),
        scratch_shapes=[pltpu.VMEM((nh, ATT_TK, ATT_TQ), F32), pltpu.VMEM((nh, ATT_TK, ATT_TQ), BF16),
                        pltpu.VMEM((nh, LANES, ATT_TQ), F32)],
        compiler_params=_params("parallel", "parallel", "arbitrary"),
        name="att_prompt",
    )(bflat, qt, ka, vt)


def _att_sample_kernel(q_ref, kp_ref, vp_ref, dp_ref, dend_ref, kn_ref, vn_ref, dn_ref, o_ref,
                       m_ref, l_ref, acc_ref):
    t = q_ref.shape[1]
    j = pl.program_id(1)

    @pl.when(j == 0)
    def _():
        m_ref[...] = jnp.full_like(m_ref, NEG_INF)
        l_ref[...] = jnp.zeros_like(l_ref)
        acc_ref[...] = jnp.zeros_like(acc_ref)

    def head(h, k, v, time_minor, bias, mask):
        sl = slice(h * HEAD_DIM, (h + 1) * HEAD_DIM)
        q_h = q_ref[0, :, sl]
        s = (_dot(q_h, k) if time_minor else _dot_nt(q_h, k)) + bias[h:h + 1]
        if mask is not None:
            s = jnp.where(mask, s, NEG_INF)
        p, alpha, m_new, l_new = _online_update(s, m_ref[h], l_ref[h])
        p = p.astype(BF16)
        acc = alpha * acc_ref[:, sl] + (_dot_nt(p, v) if time_minor else _dot(p, v))
        m_ref[h], l_ref[h], acc_ref[:, sl] = m_new, l_new, acc
        return acc / l_new

    bias_past = dend_ref[0] - dp_ref[0]
    for h in range(N_HEADS):
        head(h, kp_ref[0, h].astype(BF16), vp_ref[0, h].astype(BF16), True, bias_past, None)

    @pl.when(j == pl.num_programs(1) - 1)
    def _():
        bias_new = -dn_ref[0]
        qpos = lax.broadcasted_iota(jnp.int32, (t, t), 0)
        kpos = lax.broadcasted_iota(jnp.int32, (t, t), 1)
        for h in range(N_HEADS):
            sl = slice(h * HEAD_DIM, (h + 1) * HEAD_DIM)
            o_ref[0, :, sl] = head(h, kn_ref[0, :, sl], vn_ref[0, :, sl], False, bias_new, kpos <= qpos)


def _att_sample(qb, k_past, v_past, d_past, kb_new, vb_new, d_new):
    n, t, _ = qb.shape
    past = k_past.shape[1]
    tk = min(SAMPLE_TK, past)
    assert past % tk == 0
    d_end = d_past[:, :, past - 1:]
    k_past = jnp.transpose(k_past, (0, 2, 3, 1))
    v_past = jnp.transpose(v_past, (0, 2, 3, 1))
    new = lambda dt: pl.BlockSpec((1, t, WA), lambda b, j: (b, 0, 0))
    cache = pl.BlockSpec((1, N_HEADS, HEAD_DIM, tk), lambda b, j: (b, 0, 0, j))
    return pl.pallas_call(
        _att_sample_kernel,
        grid=(n, past // tk),
        in_specs=[new(BF16), cache, cache,
                  pl.BlockSpec((1, N_HEADS, tk), lambda b, j: (b, 0, j)),
                  pl.BlockSpec((1, N_HEADS, 1), lambda b, j: (b, 0, 0)),
                  new(BF16), new(BF16),
                  pl.BlockSpec((1, N_HEADS, t), lambda b, j: (b, 0, 0))],
        out_specs=pl.BlockSpec((1, t, WA), lambda b, j: (b, 0, 0)),
        out_shape=jax.ShapeDtypeStruct((n, t, WA), F32),
        scratch_shapes=[pltpu.VMEM((N_HEADS, t, 1), F32), pltpu.VMEM((N_HEADS, t, 1), F32),
                        pltpu.VMEM((t, WA), F32)],
        compiler_params=_params("parallel", "arbitrary"),
        name="att_sample",
    )(qb, k_past, v_past, d_past, d_end, kb_new, vb_new, d_new)


def _first_index_of_max(x, axis):
    mx = jnp.max(x, axis=axis, keepdims=True)
    idx = lax.broadcasted_iota(jnp.int32, x.shape, axis)
    first = jnp.min(jnp.where(x == mx, idx, x.shape[axis]), axis=axis, keepdims=True)
    return mx, idx == first


def _route(s, bias):
    m = s.shape[1]
    sb = (s + bias).reshape(N_GROUPS, GROUP_SIZE, m)
    top1, is_top1 = _first_index_of_max(sb, 1)
    top2 = jnp.max(jnp.where(is_top1, NEG_INF, sb), axis=1, keepdims=True)
    grp = (top1 + top2).reshape(N_GROUPS, m)
    gi = lax.broadcasted_iota(jnp.int32, (N_GROUPS, N_GROUPS, m), 0)
    gj = lax.broadcasted_iota(jnp.int32, (N_GROUPS, N_GROUPS, m), 1)
    other, mine = grp[None, :, :], grp[:, None, :]
    beats = (other > mine) | ((other == mine) & (gj < gi))
    g_rank = jnp.sum(beats.astype(jnp.int32), axis=1)
    g_keep = (g_rank < TOPK_GROUPS)[:, None, :]
    cand = jnp.where(g_keep, sb, NEG_INF).reshape(N_EXPERTS, m)
    picks = []
    for _ in range(TOP_K):
        _, pick = _first_index_of_max(cand, 0)
        picks.append(pick)
        cand = jnp.where(pick, NEG_INF, cand)
    w = jnp.concatenate([jnp.sum(jnp.where(pk, s, 0.0), axis=0, keepdims=True) for pk in picks], axis=0)
    return picks, w / jnp.sum(w, axis=0, keepdims=True) * ROUTED_SCALE


HALF_MASK = 0xFFFF0000


def _pack_halves(x):
    c = x.shape[1] // 2
    lo = lax.bitcast_convert_type(x[:, :c].astype(BF16).astype(F32), jnp.uint32) >> jnp.uint32(16)
    hi = lax.bitcast_convert_type(x[:, c:].astype(BF16).astype(F32), jnp.uint32) & jnp.uint32(HALF_MASK)
    return lax.bitcast_convert_type(lo | hi, jnp.int32)


def _unpack_halves(words):
    w = lax.bitcast_convert_type(words, jnp.uint32)
    lo = lax.bitcast_convert_type(w << jnp.uint32(16), F32)
    hi = lax.bitcast_convert_type(w & jnp.uint32(HALF_MASK), F32)
    return lo, hi


def _outproj_kernel(x_ref, lru_ref, att_ref, mod_ref, gatt_ref, wtop_ref, wbot_ref, gpost_ref, gpre_ref,
                    rwh_ref, rwl_ref, rb_ref, cnt_in_ref, before_ref,
                    x1_ref, hf_ref, xw_ref, ids_ref, ranks_ref, gates_ref, cnt_out_ref, carry_ref, *, split):
    nb, tt, d = x_ref.shape
    m = nb * tt
    first = (pl.program_id(0) == 0) & (pl.program_id(1) == 0)

    @pl.when(first)
    def _():
        carry_ref[...] = cnt_in_ref[...]

    if split is not None:
        @pl.when((pl.program_id(0) == split) & (pl.program_id(1) == 0))
        def _():
            carry_ref[...] = jnp.zeros_like(carry_ref)

    mod = mod_ref[...]
    att_n = _rms(att_ref[...], gatt_ref[...]).reshape(m, WA).astype(BF16)
    mix = _dot(lru_ref[...].reshape(m, WL), wtop_ref[...]) + _dot(att_n, wbot_ref[...])
    x1 = x_ref[...] + mod[:, 2:3, :] * _rms(mix, gpost_ref[...]).reshape(nb, tt, d)
    x1_ref[...] = x1
    hf = (_rms(x1, gpre_ref[...]) * (1.0 + mod[:, 4:5, :]) + mod[:, 3:4, :]).reshape(m, d)
    hf_hi = hf.astype(BF16)
    hf_ref[...] = hf_hi.reshape(nb, tt, d)
    hf_lo = (hf - hf_hi.astype(F32)).astype(BF16)
    rwh = rwh_ref[...]
    logits = _dot_nt(rwh, hf_hi) + _dot_nt(rwh, hf_lo) + _dot_nt(rwl_ref[...], hf_hi)
    picks, gates = _route(_sigmoid(logits), rb_ref[...])
    xw_ref[...] = _pack_halves(hf).reshape(nb, tt, d // 2)

    sel = jnp.zeros((N_EXPERTS, m), F32)
    for pk in picks:
        sel = sel + pk.astype(F32)
    prior = _dot(sel.astype(BF16), before_ref[...]) + carry_ref[...]
    expert = lax.broadcasted_iota(jnp.int32, (N_EXPERTS, m), 0).astype(F32)
    take = lambda pk, v: jnp.sum(jnp.where(pk, v, 0.0), axis=0, keepdims=True)
    ids_ref[...] = jnp.concatenate([take(pk, expert) for pk in picks], axis=0).astype(jnp.int32)
    ranks_ref[...] = jnp.concatenate([take(pk, prior) for pk in picks], axis=0).astype(jnp.int32)
    carry_ref[...] += jnp.sum(sel, axis=1, keepdims=True)
    cnt_out_ref[0] = carry_ref[...]
    gates = jnp.concatenate([gates, jnp.zeros((LANES - TOP_K, m), F32)], axis=0)
    gates_ref[...] = gates.T.reshape(nb, tt, LANES)


def _outproj(x, lru_n, att, mod, g_att, w_top, w_bot, g_post, g_pre, rw_hi, rw_lo, r_bias, cnt_in, split=None):
    n, t, d = x.shape
    nb, tt = _seq_blocks(n, t)
    m = nb * tt
    steps_t = t // tt
    n_groups = 1 if split is None else 2
    blk = lambda w: pl.BlockSpec((nb, tt, w), lambda i, j: (i, j, 0))
    const = lambda shape: pl.BlockSpec(shape, lambda i, j: (0,) * len(shape))
    per_tok = pl.BlockSpec((TOP_K, m), lambda i, j: (0, i * steps_t + j))
    group_of = (lambda i: 0) if split is None else (lambda i: jnp.where(i < split, 0, 1))
    before = jnp.asarray(np.triu(np.ones((m, m), np.float32), k=1), BF16)
    return pl.pallas_call(
        functools.partial(_outproj_kernel, split=split),
        grid=(n // nb, steps_t),
        in_specs=[blk(d), blk(WL), blk(WA), pl.BlockSpec((nb, 6, d), lambda i, j: (i, 0, 0)),
                  const((1, WA)), const((WL, d)), const((WA, d)), const((1, d)), const((1, d)),
                  const((N_EXPERTS, d)), const((N_EXPERTS, d)), const((N_EXPERTS, 1)), const((N_EXPERTS, 1)),
                  const((m, m))],
        out_specs=[blk(d), blk(d), blk(d // 2), per_tok, per_tok, blk(LANES),
                   pl.BlockSpec((1, N_EXPERTS, 1), lambda i, j: (group_of(i), 0, 0))],
        out_shape=[jax.ShapeDtypeStruct((n, t, d), F32), jax.ShapeDtypeStruct((n, t, d), BF16),
                   jax.ShapeDtypeStruct((n, t, d // 2), jnp.int32),
                   jax.ShapeDtypeStruct((TOP_K, n * t), jnp.int32), jax.ShapeDtypeStruct((TOP_K, n * t), jnp.int32),
                   jax.ShapeDtypeStruct((n, t, LANES), F32), jax.ShapeDtypeStruct((n_groups, N_EXPERTS, 1), F32)],
        scratch_shapes=[pltpu.VMEM((N_EXPERTS, 1), F32)],
        compiler_params=_params("arbitrary", "arbitrary"),
        name="outproj_router",
    )(x, lru_n, att, mod, g_att, w_top, w_bot, g_post, g_pre, rw_hi, rw_lo, r_bias, cnt_in, before)


def _subcore_ranges(n_items):
    info = plsc.get_sparse_core_info()
    n_workers = info.num_cores * info.num_subcores
    per_worker = n_items // n_workers
    assert per_worker * n_workers == n_items and per_worker % GATHER_ROWS == 0
    return info, plsc.VectorSubcoreMesh(core_axis_name="c", subcore_axis_name="s"), per_worker


def _scatter_rows(xw, pos, n_slots, row_offset=0):
    m_tot, c = pos.shape[1], xw.shape[1]
    assert row_offset % GATHER_ROWS == 0
    info, mesh, per_worker = _subcore_ranges(m_tot)
    pos_flat = pos.reshape(-1)

    @functools.partial(
        pl.kernel, mesh=mesh, out_type=jax.ShapeDtypeStruct((n_slots, c), jnp.int32),
        scratch_types=[pltpu.VMEM((GATHER_ROWS,), jnp.int32), pltpu.VMEM((GATHER_ROWS, c), jnp.int32),
                       pltpu.SemaphoreType.DMA])
    def scatter(xw_hbm, pos_hbm, out_hbm, idx_v, rows_v, sem):
        worker = lax.axis_index("s") * info.num_cores + lax.axis_index("c")
        base = worker * per_worker

        @pl.loop(0, per_worker // GATHER_ROWS)
        def _(step):
            off = pl.multiple_of(base + step * GATHER_ROWS, GATHER_ROWS)
            pltpu.sync_copy(xw_hbm.at[pl.ds(row_offset + off, GATHER_ROWS)], rows_v)
            for r in range(TOP_K):
                pltpu.sync_copy(pos_hbm.at[pl.ds(pl.multiple_of(r * m_tot + off, GATHER_ROWS), GATHER_ROWS)], idx_v)
                pltpu.async_copy(rows_v, out_hbm.at[idx_v], sem).wait()

    return scatter(xw, pos_flat)


def _swiglu_halves(lo, hi, wg, wu, wd):
    c = lo.shape[1]
    hg = _dot(lo, wg[:c]) + _dot(hi, wg[c:])
    hu = _dot(lo, wu[:c]) + _dot(hi, wu[c:])
    return _dot((_silu(hg) * hu).astype(BF16), wd)


def _expert_kernel(te_ref, valid_ref, x_ref, wg_ref, wu_ref, wd_ref, y_ref, wg_bf, wu_bf, wd_bf):
    i = pl.program_id(0)
    valid = valid_ref[i]

    @pl.when((i == 0) | (te_ref[i] != te_ref[jnp.maximum(i - 1, 0)]))
    def _():
        wg_bf[...] = wg_ref[0].astype(BF16)
        wu_bf[...] = wu_ref[0].astype(BF16)
        wd_bf[...] = wd_ref[0].astype(BF16)

    @pl.when(valid > 0)
    def _():
        w = x_ref[...]
        row = lax.broadcasted_iota(jnp.int32, w.shape, 0)
        lo, hi = _unpack_halves(jnp.where(row < valid, w, 0))
        y_ref[...] = _pack_halves(_swiglu_halves(lo.astype(BF16), hi.astype(BF16), wg_bf[...], wu_bf[...], wd_bf[...]))


def _experts(xs, tile_expert, tile_valid, wg, wu, wd):
    n_slots, c = xs.shape
    n_tiles = n_slots // EXPERT_TILE
    d = 2 * c
    rows = pl.BlockSpec((EXPERT_TILE, c), lambda i, te, tv: (i, 0))
    weight = lambda shape: pl.BlockSpec((1,) + shape, lambda i, te, tv: (te[i], 0, 0))
    return pl.pallas_call(
        _expert_kernel,
        grid_spec=pltpu.PrefetchScalarGridSpec(
            num_scalar_prefetch=2, grid=(n_tiles,),
            in_specs=[rows, weight((d, D_EXPERT)), weight((d, D_EXPERT)), weight((D_EXPERT, d))],
            out_specs=rows,
            scratch_shapes=[pltpu.VMEM((d, D_EXPERT), BF16), pltpu.VMEM((d, D_EXPERT), BF16),
                            pltpu.VMEM((D_EXPERT, d), BF16)]),
        out_shape=jax.ShapeDtypeStruct((n_slots, c), jnp.int32),
        compiler_params=_params("arbitrary"),
        name="moe_experts",
    )(tile_expert, tile_valid, xs, wg, wu, wd)


def _gather_rows(table, idx):
    b, c = idx.shape[0], table.shape[1]
    info, mesh, per_worker = _subcore_ranges(b)

    n_steps = per_worker // GATHER_ROWS
    assert n_steps % 2 == 0

    @functools.partial(
        pl.kernel, mesh=mesh, out_type=jax.ShapeDtypeStruct((b, c), jnp.int32),
        scratch_types=[pltpu.VMEM((2, GATHER_ROWS), jnp.int32), pltpu.VMEM((2, GATHER_ROWS, c), jnp.int32),
                       pltpu.SemaphoreType.DMA((2,))])
    def gather(table_hbm, idx_hbm, out_hbm, idx_v, rows_v, sems):
        worker = lax.axis_index("s") * info.num_cores + lax.axis_index("c")
        base = worker * per_worker

        def chunk(step):
            return pl.ds(pl.multiple_of(base + step * GATHER_ROWS, GATHER_ROWS), GATHER_ROWS)

        def stream(buf):
            return pltpu.make_async_copy(table_hbm.at[idx_v.at[buf]], rows_v.at[buf], sems.at[buf])

        def start(step, buf):
            pltpu.sync_copy(idx_hbm.at[chunk(step)], idx_v.at[buf])
            stream(buf).start()

        def finish(step, buf):
            stream(buf).wait()
            pltpu.sync_copy(rows_v.at[buf], out_hbm.at[chunk(step)])

        start(0, 0)

        @pl.loop(0, n_steps, step=2)
        def _(step):
            start(step + 1, 1)
            finish(step, 0)

            @pl.when(step + 2 < n_steps)
            def _():
                start(step + 2, 0)

            finish(step + 1, 1)

    return gather(table, idx)


def _combine_kernel(rows_ref, gates_ref, hf_ref, x1_ref, mod_ref, sg_ref, su_ref, sd_ref, gpost_ref, y_ref):
    nb, tt, d = hf_ref.shape
    m = nb * tt
    c = d // 2
    x = hf_ref[...].reshape(m, d)
    shared = _swiglu_halves(x[:, :c], x[:, c:], sg_ref[...], su_ref[...], sd_ref[...])
    gates = gates_ref[...].reshape(m, LANES)
    acc_lo = shared[:, :c]
    acc_hi = shared[:, c:]
    for r in range(TOP_K):
        lo, hi = _unpack_halves(rows_ref[r])
        g = gates[:, r:r + 1]
        acc_lo = acc_lo + g * lo
        acc_hi = acc_hi + g * hi
    z = _rms(jnp.concatenate([acc_lo, acc_hi], axis=1), gpost_ref[...]).reshape(nb, tt, d)
    y_ref[...] = x1_ref[...] + mod_ref[...][:, 5:6, :] * z


def _combine(rows, first_tile, gates_t, hf, x1, mod, sg, su, sd, g_post, blocks=None, y_prev=None):
    n, t, d = hf.shape
    nb, tt = _seq_blocks(n, t)
    m = nb * tt
    steps_t = t // tt
    b0, nblocks = (0, n // nb) if blocks is None else blocks
    blk = lambda w: pl.BlockSpec((nb, tt, w), lambda i, j: (b0 + i, j, 0))
    const = lambda shape: pl.BlockSpec(shape, lambda i, j: (0,) * len(shape))
    in_specs = [pl.BlockSpec((TOP_K, m, d // 2), lambda i, j: (0, first_tile + i * steps_t + j, 0)),
                blk(LANES), blk(d), blk(d), pl.BlockSpec((nb, 6, d), lambda i, j: (b0 + i, 0, 0)),
                const((d, D_EXPERT)), const((d, D_EXPERT)), const((D_EXPERT, d)), const((1, d))]
    args = (rows, gates_t, hf, x1, mod, sg, su, sd, g_post)
    kernel_fn, aliases = _combine_kernel, {}
    if y_prev is not None:
        in_specs.append(pl.BlockSpec(memory_space=pl.ANY))
        args += (y_prev,)
        aliases = {len(args) - 1: 0}
        kernel_fn = lambda *refs: _combine_kernel(*refs[:len(args) - 1], refs[-1])
    return pl.pallas_call(
        kernel_fn,
        grid=(nblocks, steps_t),
        in_specs=in_specs,
        out_specs=blk(d),
        out_shape=jax.ShapeDtypeStruct((n, t, d), F32),
        input_output_aliases=aliases,
        compiler_params=_params("parallel", "parallel"),
        name="moe_combine",
    )(*args)


def _slots_kernel(starts_ref, ids_ref, ranks_ref, pos_ref):
    ids = ids_ref[...]

    def add_start(e, pos):
        return pos + jnp.where(ids == e, starts_ref[e], 0)

    pos_ref[...] = lax.fori_loop(0, N_EXPERTS, add_start, ranks_ref[...])


def _slots(starts, ids, ranks):
    k, m_tot = ids.shape
    cols = math.gcd(m_tot, SLOT_COLS)
    assert cols % LANES == 0
    blk = pl.BlockSpec((k, cols), lambda i: (0, i))
    return pl.pallas_call(
        _slots_kernel,
        grid=(m_tot // cols,),
        in_specs=[pl.BlockSpec(memory_space=pltpu.SMEM), blk, blk],
        out_specs=blk,
        out_shape=jax.ShapeDtypeStruct((k, m_tot), jnp.int32),
        compiler_params=_params("arbitrary"),
        name="moe_slots",
    )(starts, ids, ranks)


def _slot_plan(ids, ranks, counts):
    n_pairs = ids.shape[0] * ids.shape[1]
    n_tiles = -(-(n_pairs + N_EXPERTS * (EXPERT_TILE - 1)) // EXPERT_TILE)
    cnt = counts.reshape(N_EXPERTS).astype(jnp.int32)
    padded = (cnt + EXPERT_TILE - 1) // EXPERT_TILE * EXPERT_TILE
    ends = jnp.cumsum(padded)
    starts = ends - padded
    pos = _slots(starts, ids, ranks)
    tile_start = jnp.arange(n_tiles, dtype=jnp.int32) * EXPERT_TILE
    in_expert = (tile_start[:, None] >= starts[None, :]) & (tile_start[:, None] < ends[None, :])
    tile_expert = jnp.sum(jnp.where(in_expert, jnp.arange(N_EXPERTS, dtype=jnp.int32)[None, :], 0), axis=1)
    tile_fill = jnp.sum(jnp.where(in_expert, (starts + cnt)[None, :] - tile_start[:, None], 0), axis=1)
    tile_valid = jnp.clip(tile_fill, 0, EXPERT_TILE).astype(jnp.int32)
    return pos, tile_expert.astype(jnp.int32), tile_valid, n_tiles * EXPERT_TILE


def _block_diag(w):
    g, bw, _ = w.shape
    eye = jnp.eye(g, dtype=w.dtype)
    return (eye[:, None, :, None] * w[:, :, None, :]).reshape(g * bw, g * bw)


def _prep_weights(p):
    d_main = 2 * WL + 3 * WA
    w_in = p["w_in"]
    rw_t = p["router_w"].T
    rw_hi = rw_t.astype(BF16)
    row = lambda v: v.reshape(1, -1)
    return dict(
        w_mod=p["w_mod"], b_mod=p["b_mod"],
        g_pre_mix=row(p["g_pre_mix"]), g_post_mix=row(p["g_post_mix"]),
        g_pre_ffn=row(p["g_pre_ffn"]), g_post_ffn=row(p["g_post_ffn"]),
        w_main=w_in[:, :d_main].astype(BF16),
        w_f=jnp.pad(w_in[:, d_main:], ((0, 0), (0, LANES - N_HEADS))).astype(BF16),
        b_f=jnp.pad(p["b_f"], (0, LANES - N_HEADS)).reshape(1, LANES),
        conv_w=p["conv_w"], conv_b=row(p["conv_b"]),
        wr_bd=_block_diag(p["w_r"]).astype(BF16), b_r=row(p["b_r"]),
        wi_bd=_block_diag(p["w_i"]).astype(BF16), b_i=row(p["b_i"]),
        lam=row(p["lru_lambda"]), g_lru=row(p["g_lru_out"]), g_att=row(p["g_att_out"]),
        w_top=p["w_out"][:WL].astype(BF16), w_bot=p["w_out"][WL:].astype(BF16),
        rw_hi=rw_hi, rw_lo=(rw_t - rw_hi.astype(F32)).astype(BF16),
        r_bias=p["router_bias"].reshape(N_EXPERTS, 1),
        wg=p["w_gate"], wu=p["w_up"], wd=p["w_down"],
        sg=p["ws_gate"].astype(BF16), su=p["ws_up"].astype(BF16), sd=p["ws_down"].astype(BF16),
    )


def _mixers(x, mod, conv0, h0, past, w, cnt_in, split=None):
    n, t, _ = x.shape
    proj_args = (x, mod, w["g_pre_mix"], w["w_main"], w["w_f"], w["b_f"])
    if past is None:
        xl, gy, k, v, lf, qt, ka, vt, bpre = _inproj_prompt(*proj_args)
        att = _att_prompt(qt, ka, vt, bpre)
    else:
        xl, gy, qb, kb, vb, k, v, lf = _inproj(*proj_args)
        k_past, v_past, lf_past = past
        plen = k_past.shape[1]
        by_head = lambda a: jnp.transpose(a, (0, 2, 1)).reshape(n * N_HEADS, a.shape[1])
        d_new = _cumsum_rows(by_head(lf), t).reshape(n, N_HEADS, t)
        d_past = _cumsum_rows(by_head(lf_past), min(CUMSUM_COLS, plen)).reshape(n, N_HEADS, plen)
        att = _att_sample(qb, k_past, v_past, d_past, kb, vb, d_new)
    lru_n, conv_new, h_new = _lru(xl, gy, conv0, h0.reshape(n, 1, WL), w["conv_w"], w["conv_b"],
                                  w["wr_bd"], w["b_r"], w["wi_bd"], w["b_i"], w["lam"], w["g_lru"])
    routed = _outproj(x, lru_n, att, mod, w["g_att"], w["w_top"], w["w_bot"], w["g_post_mix"],
                      w["g_pre_ffn"], w["rw_hi"], w["rw_lo"], w["r_bias"], cnt_in, split)
    state = (k.reshape(n, t, N_HEADS, HEAD_DIM), v.reshape(n, t, N_HEADS, HEAD_DIM), lf,
             conv_new, h_new.reshape(n, WL))
    return routed, state


def _layer(xp, xs, mod_p, mod_s, conv_s, h_s, past_s, w):
    n_p = xp.shape[0]
    conv0 = jnp.zeros((n_p, CONV_W - 1, WL), F32)
    h0 = jnp.zeros((n_p, WL), F32)
    zero_cnt = jnp.zeros((N_EXPERTS, 1), F32)
    t_p = xp.shape[1]
    split = 1 if n_p > 1 else None
    (x1_p, hf_p, xw_p, ids_p, rk_p, g_p, cnt_p), st_p = _mixers(xp, mod_p, conv0, h0, None, w, zero_cnt, split)
    cnt_tail = cnt_p[1] if n_p > 1 else zero_cnt
    (x1_s, hf_s, xw_s, ids_s, rk_s, g_s, cnt_s), st_s = _mixers(xs, mod_s, conv_s, h_s, past_s, w, cnt_tail)

    half = xw_p.shape[-1]
    xw_p = xw_p.reshape(-1, half)

    def routed(xw, row_offset, ids, ranks, counts):
        pos, tile_expert, tile_valid, n_slots = _slot_plan(ids, ranks, counts)
        ys = _experts(_scatter_rows(xw, pos, n_slots, row_offset), tile_expert, tile_valid,
                      w["wg"], w["wu"], w["wd"])
        return _gather_rows(ys, pos.reshape(-1)).reshape(TOP_K, ids.shape[1], half)

    rows_a = routed(xw_p, 0, ids_p[:, :t_p], rk_p[:, :t_p], cnt_p[0])
    xw_b = jnp.concatenate([xw_p[t_p:], xw_s.reshape(-1, half)], axis=0)
    rows_b = routed(xw_b, 0, jnp.concatenate([ids_p[:, t_p:], ids_s], axis=1),
                    jnp.concatenate([rk_p[:, t_p:], rk_s], axis=1), cnt_s[0])
    shared = (w["sg"], w["su"], w["sd"], w["g_post_ffn"])
    yp = _combine(rows_a, 0, g_p, hf_p, x1_p, mod_p, *shared, blocks=(0, 1))
    if n_p > 1:
        yp = _combine(rows_b, 0, g_p, hf_p, x1_p, mod_p, *shared, blocks=(1, n_p - 1), y_prev=yp)
    ysmp = _combine(rows_b, (n_p - 1) * t_p // ROW_TILE, g_s, hf_s, x1_s, mod_s, *shared)
    return yp, ysmp, st_p, st_s


def kernel(x_prompt, x_sample, c_prompt, c_sample, cache_k, cache_v, cache_logf, state_conv, state_lru, w_mod, b_mod, g_pre_mix, g_post_mix, g_pre_ffn, g_post_ffn, w_in, conv_w, conv_b, w_r, b_r, w_i, b_i, lru_lambda, b_f, g_lru_out, g_att_out, w_out, router_w, router_bias, w_gate, w_up, w_down, ws_gate, ws_up, ws_down):
    names = ("w_mod", "b_mod", "g_pre_mix", "g_post_mix", "g_pre_ffn", "g_post_ffn", "w_in", "conv_w", "conv_b",
             "w_r", "b_r", "w_i", "b_i", "lru_lambda", "b_f", "g_lru_out", "g_att_out", "w_out", "router_w",
             "router_bias", "w_gate", "w_up", "w_down", "ws_gate", "ws_up", "ws_down")
    stacked = (w_mod, b_mod, g_pre_mix, g_post_mix, g_pre_ffn, g_post_ffn, w_in, conv_w, conv_b, w_r, b_r, w_i, b_i,
               lru_lambda, b_f, g_lru_out, g_att_out, w_out, router_w, router_bias, w_gate, w_up, w_down,
               ws_gate, ws_up, ws_down)
    depth = w_mod.shape[0]
    n_p, n_s = x_prompt.shape[0], x_sample.shape[0]
    yp, ys = x_prompt, x_sample
    st_p, st_s = [], []
    for l in range(depth):
        w = _prep_weights({k: v[l] for k, v in zip(names, stacked)})
        mod = _modulation(jnp.concatenate([c_prompt, c_sample], axis=0), w["w_mod"], w["b_mod"])
        mod = mod.reshape(n_p + n_s, 6, D_MODEL)
        yp, ys, sp, ss = _layer(yp, ys, mod[:n_p], mod[n_p:], state_conv[l], state_lru[l],
                                (cache_k[l], cache_v[l], cache_logf[l]), w)
        st_p.append(sp)
        st_s.append(ss)
    stack = lambda sts, i: jnp.stack([s[i] for s in sts])
    return (yp, ys) + tuple(stack(st_p, i) for i in range(5)) + tuple(stack(st_s, i) for i in range(5))
```

```python
import functools
import math

import jax
import jax.numpy as jnp
import numpy as np
from jax import lax
from jax.experimental import pallas as pl
from jax.experimental.pallas import tpu as pltpu
from jax.experimental.pallas import tpu_sc as plsc

F32 = jnp.float32
BF16 = jnp.bfloat16

D_MODEL = 1024
WL = 512
WA = 512
N_HEADS = 8
HEAD_DIM = 64
N_PAIRS = N_HEADS // 2
PAIR_W = 2 * HEAD_DIM
LANES = 128
CONV_W = 4
LRU_BLOCKS = 8
LRU_C = 8.0
N_EXPERTS = 64
N_GROUPS = 8
GROUP_SIZE = N_EXPERTS // N_GROUPS
TOPK_GROUPS = 4
TOP_K = 8
D_EXPERT = 256
ROUTED_SCALE = 2.5
EPS = 1e-6
NEG_INF = float("-inf")
LOG2E = 1.4426950408889634

ROW_TILE = 512
ATT_TQ = 512
ATT_TK = 512
ATT_CHUNK = 64
ATT_HEADS = 4
SAMPLE_TK = 4096
CUMSUM_COLS = 1024
EXPERT_TILE = 512
SLOT_COLS = 2048
GATHER_ROWS = 64
VMEM_LIMIT = 56 * 1024 * 1024


def _params(*sem):
    return pltpu.CompilerParams(dimension_semantics=sem, vmem_limit_bytes=VMEM_LIMIT)


def _dot(a, b):
    return jnp.dot(a, b, preferred_element_type=F32)


def _dot_nt(a, b):
    return lax.dot_general(a, b, (((1,), (1,)), ((), ())), preferred_element_type=F32)


def _split3(x):
    hi = x.astype(BF16)
    r1 = x - hi.astype(F32)
    mid = r1.astype(BF16)
    lo = (r1 - mid.astype(F32)).astype(BF16)
    return hi, mid, lo


def _rms(x, g):
    return x * lax.rsqrt(jnp.mean(x * x, axis=-1, keepdims=True) + EPS) * g


def _sigmoid(x):
    return 1.0 / (1.0 + jnp.exp(-x))


def _silu(x):
    return x * _sigmoid(x)


def _gelu_tanh(x):
    return 0.5 * x * (1.0 + jnp.tanh(0.7978845608028654 * (x + 0.044715 * (x * x * x))))


def _log_sigmoid(x):
    return jnp.minimum(x, 0.0) - jnp.log1p(jnp.exp(-jnp.abs(x)))


def _seq_blocks(n, t):
    if t >= ROW_TILE:
        assert t % ROW_TILE == 0
        return 1, ROW_TILE
    nb = ROW_TILE // t
    assert nb * t == ROW_TILE and n % nb == 0
    return nb, t


def _mod_kernel(c_ref, w_ref, b_ref, o_ref):
    c = _silu(c_ref[...])
    c_hi = c.astype(BF16)
    c_lo = (c - c_hi.astype(F32)).astype(BF16)
    w = w_ref[...]
    w_hi = w.astype(BF16)
    w_lo = (w - w_hi.astype(F32)).astype(BF16)
    o_ref[...] = _dot(c_hi, w_hi) + _dot(c_lo, w_hi) + _dot(c_hi, w_lo) + b_ref[...]


def _modulation(c, w_mod, b_mod):
    rows = c.shape[0]
    n = -(-rows // 8) * 8
    c = jnp.pad(c, ((0, n - rows), (0, 0)))
    d6 = w_mod.shape[1]
    return pl.pallas_call(
        _mod_kernel,
        grid=(d6 // D_MODEL,),
        in_specs=[pl.BlockSpec((n, D_MODEL), lambda j: (0, 0)),
                  pl.BlockSpec((D_MODEL, D_MODEL), lambda j: (0, j)),
                  pl.BlockSpec((1, D_MODEL), lambda j: (0, j))],
        out_specs=pl.BlockSpec((n, D_MODEL), lambda j: (0, j)),
        out_shape=jax.ShapeDtypeStruct((n, d6), F32),
        compiler_params=_params("arbitrary"),
        name="modulation",
    )(c, w_mod, b_mod.reshape(1, d6))[:rows]


def _inproj_kernel(x_ref, mod_ref, g_ref, w_ref, wf_ref, bf_ref,
                   xl_ref, gy_ref, qb_ref, kb_ref, vb_ref, k_ref, v_ref, lf_ref):
    nb, tt, d = x_ref.shape
    x = x_ref[...]
    mod = mod_ref[...]
    hn = _rms(x, g_ref[...]) * (1.0 + mod[:, 1:2, :]) + mod[:, 0:1, :]
    hb = hn.reshape(nb * tt, d).astype(BF16)

    def proj(col):
        return _dot(hb, w_ref[:, col * WL:(col + 1) * WL]).reshape(nb, tt, WL)

    xl_ref[...] = proj(0)
    gy_ref[...] = _gelu_tanh(proj(1)).astype(BF16)
    qb_ref[...] = (proj(2) * (HEAD_DIM ** -0.5)).astype(BF16)
    k = proj(3)
    k_ref[...] = k
    kb_ref[...] = k.astype(BF16)
    v = proj(4)
    v_ref[...] = v
    vb_ref[...] = v.astype(BF16)
    fl = _dot(hb, wf_ref[...]) + bf_ref[...]
    lf_ref[...] = _log_sigmoid(fl).reshape(nb, tt, LANES)[:, :, :N_HEADS]


def _inproj(x, mod, g_pre, w_main, w_f, b_f):
    n, t, d = x.shape
    nb, tt = _seq_blocks(n, t)
    blk = lambda w: pl.BlockSpec((nb, tt, w), lambda i, j: (i, j, 0))
    const = lambda shape: pl.BlockSpec(shape, lambda i, j: (0,) * len(shape))
    f32 = lambda w: jax.ShapeDtypeStruct((n, t, w), F32)
    b16 = lambda w: jax.ShapeDtypeStruct((n, t, w), BF16)
    return pl.pallas_call(
        _inproj_kernel,
        grid=(n // nb, t // tt),
        in_specs=[blk(d),
                  pl.BlockSpec((nb, 6, d), lambda i, j: (i, 0, 0)),
                  const((1, d)), const(w_main.shape), const(w_f.shape), const((1, LANES))],
        out_specs=[blk(WL), blk(WL), blk(WA), blk(WA), blk(WA), blk(WA), blk(WA), blk(N_HEADS)],
        out_shape=[f32(WL), b16(WL), b16(WA), b16(WA), b16(WA), f32(WA), f32(WA), f32(N_HEADS)],
        compiler_params=_params("parallel", "arbitrary"),
        name="inproj",
    )(x, mod, g_pre, w_main, w_f, b_f)


def _aug_lane(h):
    return HEAD_DIM if h % 2 == 0 else 0


def _inproj_prompt_kernel(x_ref, mod_ref, g_ref, w_ref, wf_ref, bf_ref, place_ref,
                          xl_ref, gy_ref, k_ref, v_ref, lf_ref, qt_ref, ka_ref, vt_ref, bpre_ref, carry_ref):
    _, tt, d = x_ref.shape

    @pl.when(pl.program_id(1) == 0)
    def _():
        carry_ref[...] = jnp.zeros_like(carry_ref)

    mod = mod_ref[0]
    hb = (_rms(x_ref[0], g_ref[...]) * (1.0 + mod[1:2, :]) + mod[0:1, :]).astype(BF16)

    def proj(col):
        return _dot(hb, w_ref[:, col * WL:(col + 1) * WL])

    xl_ref[0] = proj(0)
    gy_ref[0] = _gelu_tanh(proj(1)).astype(BF16)
    q = proj(2) * (HEAD_DIM ** -0.5 * LOG2E)
    k = proj(3)
    k_ref[0] = k
    v = proj(4)
    v_ref[0] = v
    fl = _dot(hb, wf_ref[...]) + bf_ref[...]
    lane = lax.broadcasted_iota(jnp.int32, (tt, LANES), 1)
    lf = jnp.where(lane < N_HEADS, _log_sigmoid(fl), 0.0)
    lf_ref[0] = lf[:, :N_HEADS]

    row = lax.broadcasted_iota(jnp.int32, (tt, tt), 0)
    col = lax.broadcasted_iota(jnp.int32, (tt, tt), 1)
    tril = (col <= row).astype(BF16)
    hi, mid, lo = _split3(lf)
    e = _dot(tril, hi) + _dot(tril, mid) + _dot(tril, lo)
    bpre_ref[0, 0] = carry_ref[...]
    carry_ref[...] += e[tt - 1:tt, :]
    e_hi, e_mid, e_lo = _split3(e * (-LOG2E))
    aug_k = _dot(jnp.concatenate([e_hi, e_mid, e_lo], axis=1), place_ref[...])

    for h in range(N_HEADS):
        pair = slice((h // 2) * PAIR_W, (h // 2 + 1) * PAIR_W)
        dims = (lane < HEAD_DIM) if h % 2 == 0 else (lane >= HEAD_DIM)
        a0 = _aug_lane(h)
        ones3 = ((lane >= a0) & (lane < a0 + 3)).astype(F32)
        qt_ref[0, h] = (jnp.where(dims, q[:, pair], 0.0) + ones3).T.astype(BF16)
        ka_ref[0, h] = (jnp.where(dims, k[:, pair], 0.0) + aug_k[:, h * LANES:(h + 1) * LANES]).astype(BF16)
        one1 = (lane == a0).astype(F32)
        vt_ref[0, h] = (jnp.where(dims, v[:, pair], 0.0) + one1).T.astype(BF16)


def _placement():
    pl_mat = np.zeros((3 * LANES, N_HEADS * LANES), np.float32)
    for p in range(3):
        for h in range(N_HEADS):
            pl_mat[p * LANES + h, h * LANES + _aug_lane(h) + p] = 1.0
    return jnp.asarray(pl_mat, BF16)


def _inproj_prompt(x, mod, g_pre, w_main, w_f, b_f):
    n, t, d = x.shape
    tt = ROW_TILE
    assert t % tt == 0
    nblk = t // tt
    blk = lambda w: pl.BlockSpec((1, tt, w), lambda i, j: (i, j, 0))
    const = lambda shape: pl.BlockSpec(shape, lambda i, j: (0,) * len(shape))
    f32 = lambda w: jax.ShapeDtypeStruct((n, t, w), F32)
    place = _placement()
    return pl.pallas_call(
        _inproj_prompt_kernel,
        grid=(n, nblk),
        in_specs=[blk(d), pl.BlockSpec((1, 6, d), lambda i, j: (i, 0, 0)),
                  const((1, d)), const(w_main.shape), const(w_f.shape), const((1, LANES)), const(place.shape)],
        out_specs=[blk(WL), blk(WL), blk(WA), blk(WA), blk(N_HEADS),
                   pl.BlockSpec((1, N_HEADS, LANES, tt), lambda i, j: (i, 0, 0, j)),
                   pl.BlockSpec((1, N_HEADS, tt, LANES), lambda i, j: (i, 0, j, 0)),
                   pl.BlockSpec((1, N_HEADS, LANES, tt), lambda i, j: (i, 0, 0, j)),
                   pl.BlockSpec((1, 1, 1, LANES), lambda i, j: (i, j, 0, 0))],
        out_shape=[f32(WL), jax.ShapeDtypeStruct((n, t, WL), BF16), f32(WA), f32(WA), f32(N_HEADS),
                   jax.ShapeDtypeStruct((n, N_HEADS, LANES, t), BF16),
                   jax.ShapeDtypeStruct((n, N_HEADS, t, LANES), BF16),
                   jax.ShapeDtypeStruct((n, N_HEADS, LANES, t), BF16),
                   jax.ShapeDtypeStruct((n, nblk, 1, LANES), F32)],
        scratch_shapes=[pltpu.VMEM((1, LANES), F32)],
        compiler_params=_params("parallel", "arbitrary"),
        name="inproj_prompt",
    )(x, mod, g_pre, w_main, w_f, b_f, place)


def _expm1_neg(x):
    poly = x * (1.0 + x * (0.5 + x * (1.0 / 6.0 + x * (1.0 / 24.0 + x * (1.0 / 120.0)))))
    return jnp.where(x > -0.1, poly, jnp.exp(x) - 1.0)


def _lru_kernel(xl_ref, gy_ref, conv0_ref, h0_ref, cw_ref, cb_ref, wr_ref, br_ref, wi_ref, bi_ref,
                lam_ref, g_ref, out_ref, conv_ref, hlast_ref, tail_ref, carry_ref):
    nb, tt, w = xl_ref.shape
    j = pl.program_id(1)

    @pl.when(j == 0)
    def _():
        tail_ref[:, 8 - (CONV_W - 1):, :] = conv0_ref[...]
        carry_ref[...] = h0_ref[...]

    xl = xl_ref[...]
    xpad = jnp.concatenate([tail_ref[...], xl], axis=1)
    cw = cw_ref[...]
    xc = jnp.zeros_like(xl) + cb_ref[...]
    for k in range(CONV_W):
        off = 8 - (CONV_W - 1) + k
        xc = xc + xpad[:, off:off + tt, :] * cw[k:k + 1, :]
    conv_ref[...] = xpad[:, tt + 8 - (CONV_W - 1):, :]
    tail_ref[...] = xpad[:, tt:, :]

    m = nb * tt
    xf = xc.reshape(m, w)
    xb = xf.astype(BF16)
    r = _sigmoid(_dot(xb, wr_ref[...]) + br_ref[...])
    gi = _sigmoid(_dot(xb, wi_ref[...]) + bi_ref[...])
    lam = lam_ref[...]
    softplus = jnp.maximum(-lam, 0.0) + jnp.log1p(jnp.exp(-jnp.abs(lam)))
    log_a = (-LRU_C) * r * softplus
    a = jnp.exp(log_a)
    b = jnp.sqrt(-_expm1_neg(2.0 * log_a)) * (gi * xf)

    groups = m // 8
    a = a.reshape(groups, 8, w)
    b = b.reshape(groups, 8, w)
    sub = lax.broadcasted_iota(jnp.int32, (groups, 8, w), 1)
    for d in (1, 2, 4):
        keep = sub >= d
        a_prev = jnp.where(keep, pltpu.roll(a, d, 1), 1.0)
        b_prev = jnp.where(keep, pltpu.roll(b, d, 1), 0.0)
        b = a * b_prev + b
        a = a * a_prev
    carry = carry_ref[...]
    groups_per_seq = tt // 8
    rows = []
    for g in range(groups):
        if g % groups_per_seq == 0:
            prev = carry[g // groups_per_seq]
        h_g = a[g] * prev + b[g]
        prev = h_g[7:8]
        rows.append(h_g)
    h = jnp.concatenate(rows, axis=0).reshape(nb, tt, w)
    h_last = h[:, tt - 1:tt, :]
    carry_ref[...] = h_last
    hlast_ref[...] = h_last
    out_ref[...] = _rms(h * gy_ref[...].astype(F32), g_ref[...]).astype(BF16)


def _lru(xl, gy, conv0, h0, conv_w, conv_b, wr_bd, b_r, wi_bd, b_i, lam, g_lru):
    n, t, w = xl.shape
    nb, tt = _seq_blocks(n, t)
    blk = pl.BlockSpec((nb, tt, w), lambda i, j: (i, j, 0))
    per_seq = lambda rows: pl.BlockSpec((nb, rows, w), lambda i, j: (i, 0, 0))
    const = lambda shape: pl.BlockSpec(shape, lambda i, j: (0,) * len(shape))
    row = const((1, w))
    return pl.pallas_call(
        _lru_kernel,
        grid=(n // nb, t // tt),
        in_specs=[blk, blk, per_seq(CONV_W - 1), per_seq(1),
                  const((CONV_W, w)), row, const((w, w)), row, const((w, w)), row, row, row],
        out_specs=[blk, per_seq(CONV_W - 1), per_seq(1)],
        out_shape=[jax.ShapeDtypeStruct((n, t, w), BF16),
                   jax.ShapeDtypeStruct((n, CONV_W - 1, w), F32),
                   jax.ShapeDtypeStruct((n, 1, w), F32)],
        scratch_shapes=[pltpu.VMEM((nb, 8, w), F32), pltpu.VMEM((nb, 1, w), F32)],
        compiler_params=_params("parallel", "arbitrary"),
        name="rglru",
    )(xl, gy, conv0, h0, conv_w, conv_b, wr_bd, b_r, wi_bd, b_i, lam, g_lru)


def _cumsum_rows_kernel(x_ref, upper_ref, o_ref, carry_ref):
    tb = x_ref.shape[1]

    @pl.when(pl.program_id(0) == 0)
    def _():
        carry_ref[...] = jnp.zeros_like(carry_ref)

    upper = upper_ref[...]
    hi, mid, lo = _split3(x_ref[...])
    d = _dot(hi, upper) + _dot(mid, upper) + _dot(lo, upper) + carry_ref[...]
    carry_ref[...] = d[:, tb - 1:tb]
    o_ref[...] = d


def _cumsum_rows(x, tb):
    rows, t = x.shape
    upper = jnp.asarray(np.triu(np.ones((tb, tb), np.float32)), BF16)
    return pl.pallas_call(
        _cumsum_rows_kernel,
        grid=(t // tb,),
        in_specs=[pl.BlockSpec((rows, tb), lambda j: (0, j)), pl.BlockSpec((tb, tb), lambda j: (0, 0))],
        out_specs=pl.BlockSpec((rows, tb), lambda j: (0, j)),
        out_shape=jax.ShapeDtypeStruct((rows, t), F32),
        scratch_shapes=[pltpu.VMEM((rows, 1), F32)],
        compiler_params=_params("arbitrary"),
        name="logf_cumsum",
    )(x, upper)


def _online_update(s, m_prev, l_prev):
    m_new = jnp.maximum(m_prev, jnp.max(s, axis=1, keepdims=True))
    alpha = jnp.exp(m_prev - m_new)
    p = jnp.exp(s - m_new)
    l_new = alpha * l_prev + jnp.sum(p, axis=1, keepdims=True)
    return p, alpha, m_new, l_new


def _att_prompt_kernel(bpre_ref, qt_ref, ka_ref, vt_ref, o_ref, s_ref, p_ref, acc_ref):
    tq = qt_ref.shape[2]
    tk = ATT_TK
    nblk = ka_ref.shape[1] // tk
    b, hp, i = pl.program_id(0), pl.program_id(1), pl.program_id(2)
    q0 = i * tq
    jd = q0 // tk
    kpos = lax.broadcasted_iota(jnp.int32, (tk, tq), 0)
    qpos = lax.broadcasted_iota(jnp.int32, (tk, tq), 1)
    rows = lax.broadcasted_iota(jnp.int32, (LANES, tq), 0)
    nh = qt_ref.shape[0]
    heads = range(nh)
    base = [((b * (N_HEADS // nh) + hp) * nh + hh) * nblk for hh in heads]

    def scores(j, masked):
        start = pl.multiple_of(j * tk, tk)
        col_max = []
        for hh in heads:
            s = _dot(ka_ref[hh, pl.ds(start, tk), :], qt_ref[hh])
            if masked:
                s = jnp.where(kpos + start <= qpos + q0, s, NEG_INF)
            s_ref[hh] = s
            col_max.append(jnp.max(s, axis=0, keepdims=True))
        return tuple(col_max)

    def softmax_pv(j, col_max, m):
        start = pl.multiple_of(j * tk, tk)
        m_out = []
        for hh in heads:
            c = (bpre_ref[base[hh] + jd] - bpre_ref[base[hh] + j]) * LOG2E
            m_new = jnp.maximum(m[hh], col_max[hh] + c)
            alpha = jnp.exp2(m[hh] - m_new)
            shift = m_new - c
            for ch in range(tk // ATT_CHUNK):
                sl = slice(ch * ATT_CHUNK, (ch + 1) * ATT_CHUNK)
                p_ref[hh, sl, :] = jnp.exp2(s_ref[hh, sl, :] - shift).astype(BF16)
            m_out.append((m_new, alpha))
        return tuple(m_out), start

    def accumulate(m_alpha, start):
        for hh in heads:
            pv = _dot(vt_ref[hh, :, pl.ds(start, tk)], p_ref[hh])
            acc_ref[hh] = m_alpha[hh][1] * acc_ref[hh] + pv
        return tuple(ma[0] for ma in m_alpha)

    def step(j, next_masked, carry):
        col_max, m = carry
        m_alpha, start = softmax_pv(j, col_max, m)
        col_max_next = scores(j + 1, next_masked)
        return col_max_next, accumulate(m_alpha, start)

    acc_ref[...] = jnp.zeros_like(acc_ref)
    neg = jnp.full((1, tq), NEG_INF, F32)
    n_masked = max(1, tq // tk)
    carry = (scores(0, True), (neg,) * nh)
    carry = lax.fori_loop(0, jd - 1, lambda j, cr: step(j, False, cr), carry)
    carry = lax.cond(jd > 0, lambda cr: step(jd - 1, True, cr), lambda cr: cr, carry)
    for extra in range(n_masked - 1):
        carry = step(jd + extra, True, carry)
    col_max, m = carry
    m_alpha, start = softmax_pv(jd + n_masked - 1, col_max, m)
    accumulate(m_alpha, start)

    for pair in range(nh // 2):
        acc_a, acc_b = acc_ref[2 * pair], acc_ref[2 * pair + 1]
        out_a = acc_a / acc_a[_aug_lane(0):_aug_lane(0) + 1, :]
        out_b = acc_b / acc_b[_aug_lane(1):_aug_lane(1) + 1, :]
        o_ref[0, :, pair * PAIR_W:(pair + 1) * PAIR_W] = jnp.where(rows < HEAD_DIM, out_a, out_b).T


def _att_prompt(qt, ka, vt, bpre):
    n, _, _, t = qt.shape
    assert t % ATT_TQ == 0 and (ATT_TK % ATT_TQ == 0 or ATT_TQ % ATT_TK == 0) and ATT_TK == ROW_TILE
    nh = ATT_HEADS
    grouped = lambda q: (N_HEADS // nh, nh) + q.shape[2:]
    qt, ka, vt = (a.reshape((n,) + grouped(a)) for a in (qt, ka, vt))
    bflat = jnp.transpose(bpre[:, :, 0, :N_HEADS], (0, 2, 1)).reshape(-1)
    return pl.pallas_call(
        _att_prompt_kernel,
        grid=(n, N_HEADS // nh, t // ATT_TQ),
        in_specs=[pl.BlockSpec(memory_space=pltpu.SMEM),
                  pl.BlockSpec((None, None, nh, LANES, ATT_TQ), lambda b, h, i: (b, h, 0, 0, i)),
                  pl.BlockSpec((None, None, nh, t, LANES), lambda b, h, i: (b, h, 0, 0, 0),
                               pipeline_mode=pl.Buffered(1)),
                  pl.BlockSpec((None, None, nh, LANES, t), lambda b, h, i: (b, h, 0, 0, 0),
                               pipeline_mode=pl.Buffered(1))],
        out_specs=pl.BlockSpec((1, ATT_TQ, nh * HEAD_DIM), lambda b, h, i: (b, i, h)),
        out_shape=jax.ShapeDtypeStruct((n, t, WA), F32),
        scratch_shapes=[pltpu.VMEM((nh, ATT_TK, ATT_TQ), F32), pltpu.VMEM((nh, ATT_TK, ATT_TQ), BF16),
                        pltpu.VMEM((nh, LANES, ATT_TQ), F32)],
        compiler_params=_params("parallel", "parallel", "arbitrary"),
        name="att_prompt",
    )(bflat, qt, ka, vt)


def _att_sample_kernel(q_ref, kp_ref, vp_ref, dp_ref, dend_ref, kn_ref, vn_ref, dn_ref, o_ref,
                       m_ref, l_ref, acc_ref):
    t = q_ref.shape[1]
    j = pl.program_id(1)

    @pl.when(j == 0)
    def _():
        m_ref[...] = jnp.full_like(m_ref, NEG_INF)
        l_ref[...] = jnp.zeros_like(l_ref)
        acc_ref[...] = jnp.zeros_like(acc_ref)

    def head(h, k, v, time_minor, bias, mask):
        sl = slice(h * HEAD_DIM, (h + 1) * HEAD_DIM)
        q_h = q_ref[0, :, sl]
        s = (_dot(q_h, k) if time_minor else _dot_nt(q_h, k)) + bias[h:h + 1]
        if mask is not None:
            s = jnp.where(mask, s, NEG_INF)
        p, alpha, m_new, l_new = _online_update(s, m_ref[h], l_ref[h])
        p = p.astype(BF16)
        acc = alpha * acc_ref[:, sl] + (_dot_nt(p, v) if time_minor else _dot(p, v))
        m_ref[h], l_ref[h], acc_ref[:, sl] = m_new, l_new, acc
        return acc / l_new

    bias_past = dend_ref[0] - dp_ref[0]
    for h in range(N_HEADS):
        head(h, kp_ref[0, h].astype(BF16), vp_ref[0, h].astype(BF16), True, bias_past, None)

    @pl.when(j == pl.num_programs(1) - 1)
    def _():
        bias_new = -dn_ref[0]
        qpos = lax.broadcasted_iota(jnp.int32, (t, t), 0)
        kpos = lax.broadcasted_iota(jnp.int32, (t, t), 1)
        for h in range(N_HEADS):
            sl = slice(h * HEAD_DIM, (h + 1) * HEAD_DIM)
            o_ref[0, :, sl] = head(h, kn_ref[0, :, sl], vn_ref[0, :, sl], False, bias_new, kpos <= qpos)


def _att_sample(qb, k_past, v_past, d_past, kb_new, vb_new, d_new):
    n, t, _ = qb.shape
    past = k_past.shape[1]
    tk = min(SAMPLE_TK, past)
    assert past % tk == 0
    d_end = d_past[:, :, past - 1:]
    k_past = jnp.transpose(k_past, (0, 2, 3, 1))
    v_past = jnp.transpose(v_past, (0, 2, 3, 1))
    new = lambda dt: pl.BlockSpec((1, t, WA), lambda b, j: (b, 0, 0))
    cache = pl.BlockSpec((1, N_HEADS, HEAD_DIM, tk), lambda b, j: (b, 0, 0, j))
    return pl.pallas_call(
        _att_sample_kernel,
        grid=(n, past // tk),
        in_specs=[new(BF16), cache, cache,
                  pl.BlockSpec((1, N_HEADS, tk), lambda b, j: (b, 0, j)),
                  pl.BlockSpec((1, N_HEADS, 1), lambda b, j: (b, 0, 0)),
                  new(BF16), new(BF16),
                  pl.BlockSpec((1, N_HEADS, t), lambda b, j: (b, 0, 0))],
        out_specs=pl.BlockSpec((1, t, WA), lambda b, j: (b, 0, 0)),
        out_shape=jax.ShapeDtypeStruct((n, t, WA), F32),
        scratch_shapes=[pltpu.VMEM((N_HEADS, t, 1), F32), pltpu.VMEM((N_HEADS, t, 1), F32),
                        pltpu.VMEM((t, WA), F32)],
        compiler_params=_params("parallel", "arbitrary"),
        name="att_sample",
    )(qb, k_past, v_past, d_past, d_end, kb_new, vb_new, d_new)


def _first_index_of_max(x, axis):
    mx = jnp.max(x, axis=axis, keepdims=True)
    idx = lax.broadcasted_iota(jnp.int32, x.shape, axis)
    first = jnp.min(jnp.where(x == mx, idx, x.shape[axis]), axis=axis, keepdims=True)
    return mx, idx == first


def _route(s, bias):
    m = s.shape[1]
    sb = (s + bias).reshape(N_GROUPS, GROUP_SIZE, m)
    top1, is_top1 = _first_index_of_max(sb, 1)
    top2 = jnp.max(jnp.where(is_top1, NEG_INF, sb), axis=1, keepdims=True)
    grp = (top1 + top2).reshape(N_GROUPS, m)
    gi = lax.broadcasted_iota(jnp.int32, (N_GROUPS, N_GROUPS, m), 0)
    gj = lax.broadcasted_iota(jnp.int32, (N_GROUPS, N_GROUPS, m), 1)
    other, mine = grp[None, :, :], grp[:, None, :]
    beats = (other > mine) | ((other == mine) & (gj < gi))
    g_rank = jnp.sum(beats.astype(jnp.int32), axis=1)
    g_keep = (g_rank < TOPK_GROUPS)[:, None, :]
    cand = jnp.where(g_keep, sb, NEG_INF).reshape(N_EXPERTS, m)
    picks = []
    for _ in range(TOP_K):
        _, pick = _first_index_of_max(cand, 0)
        picks.append(pick)
        cand = jnp.where(pick, NEG_INF, cand)
    w = jnp.concatenate([jnp.sum(jnp.where(pk, s, 0.0), axis=0, keepdims=True) for pk in picks], axis=0)
    return picks, w / jnp.sum(w, axis=0, keepdims=True) * ROUTED_SCALE


HALF_MASK = 0xFFFF0000


def _pack_halves(x):
    c = x.shape[1] // 2
    lo = lax.bitcast_convert_type(x[:, :c].astype(BF16).astype(F32), jnp.uint32) >> jnp.uint32(16)
    hi = lax.bitcast_convert_type(x[:, c:].astype(BF16).astype(F32), jnp.uint32) & jnp.uint32(HALF_MASK)
    return lax.bitcast_convert_type(lo | hi, jnp.int32)


def _unpack_halves(words):
    w = lax.bitcast_convert_type(words, jnp.uint32)
    lo = lax.bitcast_convert_type(w << jnp.uint32(16), F32)
    hi = lax.bitcast_convert_type(w & jnp.uint32(HALF_MASK), F32)
    return lo, hi


def _outproj_kernel(x_ref, lru_ref, att_ref, mod_ref, gatt_ref, wtop_ref, wbot_ref, gpost_ref, gpre_ref,
                    rwh_ref, rwl_ref, rb_ref, cnt_in_ref, before_ref,
                    x1_ref, hf_ref, xw_ref, ids_ref, ranks_ref, gates_ref, cnt_out_ref, carry_ref, *, split):
    nb, tt, d = x_ref.shape
    m = nb * tt
    first = (pl.program_id(0) == 0) & (pl.program_id(1) == 0)

    @pl.when(first)
    def _():
        carry_ref[...] = cnt_in_ref[...]

    if split is not None:
        @pl.when((pl.program_id(0) == split) & (pl.program_id(1) == 0))
        def _():
            carry_ref[...] = jnp.zeros_like(carry_ref)

    mod = mod_ref[...]
    att_n = _rms(att_ref[...], gatt_ref[...]).reshape(m, WA).astype(BF16)
    mix = _dot(lru_ref[...].reshape(m, WL), wtop_ref[...]) + _dot(att_n, wbot_ref[...])
    x1 = x_ref[...] + mod[:, 2:3, :] * _rms(mix, gpost_ref[...]).reshape(nb, tt, d)
    x1_ref[...] = x1
    hf = (_rms(x1, gpre_ref[...]) * (1.0 + mod[:, 4:5, :]) + mod[:, 3:4, :]).reshape(m, d)
    hf_hi = hf.astype(BF16)
    hf_ref[...] = hf_hi.reshape(nb, tt, d)
    hf_lo = (hf - hf_hi.astype(F32)).astype(BF16)
    rwh = rwh_ref[...]
    logits = _dot_nt(rwh, hf_hi) + _dot_nt(rwh, hf_lo) + _dot_nt(rwl_ref[...], hf_hi)
    picks, gates = _route(_sigmoid(logits), rb_ref[...])
    xw_ref[...] = _pack_halves(hf).reshape(nb, tt, d // 2)

    sel = jnp.zeros((N_EXPERTS, m), F32)
    for pk in picks:
        sel = sel + pk.astype(F32)
    prior = _dot(sel.astype(BF16), before_ref[...]) + carry_ref[...]
    expert = lax.broadcasted_iota(jnp.int32, (N_EXPERTS, m), 0).astype(F32)
    take = lambda pk, v: jnp.sum(jnp.where(pk, v, 0.0), axis=0, keepdims=True)
    ids_ref[...] = jnp.concatenate([take(pk, expert) for pk in picks], axis=0).astype(jnp.int32)
    ranks_ref[...] = jnp.concatenate([take(pk, prior) for pk in picks], axis=0).astype(jnp.int32)
    carry_ref[...] += jnp.sum(sel, axis=1, keepdims=True)
    cnt_out_ref[0] = carry_ref[...]
    gates = jnp.concatenate([gates, jnp.zeros((LANES - TOP_K, m), F32)], axis=0)
    gates_ref[...] = gates.T.reshape(nb, tt, LANES)


def _outproj(x, lru_n, att, mod, g_att, w_top, w_bot, g_post, g_pre, rw_hi, rw_lo, r_bias, cnt_in, split=None):
    n, t, d = x.shape
    nb, tt = _seq_blocks(n, t)
    m = nb * tt
    steps_t = t // tt
    n_groups = 1 if split is None else 2
    blk = lambda w: pl.BlockSpec((nb, tt, w), lambda i, j: (i, j, 0))
    const = lambda shape: pl.BlockSpec(shape, lambda i, j: (0,) * len(shape))
    per_tok = pl.BlockSpec((TOP_K, m), lambda i, j: (0, i * steps_t + j))
    group_of = (lambda i: 0) if split is None else (lambda i: jnp.where(i < split, 0, 1))
    before = jnp.asarray(np.triu(np.ones((m, m), np.float32), k=1), BF16)
    return pl.pallas_call(
        functools.partial(_outproj_kernel, split=split),
        grid=(n // nb, steps_t),
        in_specs=[blk(d), blk(WL), blk(WA), pl.BlockSpec((nb, 6, d), lambda i, j: (i, 0, 0)),
                  const((1, WA)), const((WL, d)), const((WA, d)), const((1, d)), const((1, d)),
                  const((N_EXPERTS, d)), const((N_EXPERTS, d)), const((N_EXPERTS, 1)), const((N_EXPERTS, 1)),
                  const((m, m))],
        out_specs=[blk(d), blk(d), blk(d // 2), per_tok, per_tok, blk(LANES),
                   pl.BlockSpec((1, N_EXPERTS, 1), lambda i, j: (group_of(i), 0, 0))],
        out_shape=[jax.ShapeDtypeStruct((n, t, d), F32), jax.ShapeDtypeStruct((n, t, d), BF16),
                   jax.ShapeDtypeStruct((n, t, d // 2), jnp.int32),
                   jax.ShapeDtypeStruct((TOP_K, n * t), jnp.int32), jax.ShapeDtypeStruct((TOP_K, n * t), jnp.int32),
                   jax.ShapeDtypeStruct((n, t, LANES), F32), jax.ShapeDtypeStruct((n_groups, N_EXPERTS, 1), F32)],
        scratch_shapes=[pltpu.VMEM((N_EXPERTS, 1), F32)],
        compiler_params=_params("arbitrary", "arbitrary"),
        name="outproj_router",
    )(x, lru_n, att, mod, g_att, w_top, w_bot, g_post, g_pre, rw_hi, rw_lo, r_bias, cnt_in, before)


def _subcore_ranges(n_items):
    info = plsc.get_sparse_core_info()
    n_workers = info.num_cores * info.num_subcores
    per_worker = n_items // n_workers
    assert per_worker * n_workers == n_items and per_worker % GATHER_ROWS == 0
    return info, plsc.VectorSubcoreMesh(core_axis_name="c", subcore_axis_name="s"), per_worker


def _scatter_rows(sources, pos, n_slots):
    m_tot, c = pos.shape[1], sources[0][0].shape[1]
    assert sum(cnt for _, _, cnt in sources) == m_tot
    assert all(first % GATHER_ROWS == 0 and cnt % GATHER_ROWS == 0 for _, first, cnt in sources)
    info, mesh, per_worker = _subcore_ranges(m_tot)
    pos_flat = pos.reshape(-1)

    @functools.partial(
        pl.kernel, mesh=mesh, out_type=jax.ShapeDtypeStruct((n_slots, c), jnp.int32),
        scratch_types=[pltpu.VMEM((GATHER_ROWS,), jnp.int32), pltpu.VMEM((GATHER_ROWS, c), jnp.int32),
                       pltpu.SemaphoreType.DMA])
    def scatter(*refs):
        src_refs, (pos_hbm, out_hbm, idx_v, rows_v, sem) = refs[:len(sources)], refs[len(sources):]
        worker = lax.axis_index("s") * info.num_cores + lax.axis_index("c")
        base = worker * per_worker

        @pl.loop(0, per_worker // GATHER_ROWS)
        def _(step):
            off = pl.multiple_of(base + step * GATHER_ROWS, GATHER_ROWS)
            token0 = 0
            for src_hbm, (_, first, cnt) in zip(src_refs, sources):
                @pl.when((off >= token0) & (off < token0 + cnt))
                def _(src_hbm=src_hbm, shift=first - token0):
                    pltpu.sync_copy(src_hbm.at[pl.ds(pl.multiple_of(off + shift, GATHER_ROWS), GATHER_ROWS)], rows_v)
                token0 += cnt
            for r in range(TOP_K):
                pltpu.sync_copy(pos_hbm.at[pl.ds(pl.multiple_of(r * m_tot + off, GATHER_ROWS), GATHER_ROWS)], idx_v)
                pltpu.async_copy(rows_v, out_hbm.at[idx_v], sem).wait()

    return scatter(*[a for a, _, _ in sources], pos_flat)


def _swiglu_halves(lo, hi, wg, wu, wd):
    c = lo.shape[1]
    hg = _dot(lo, wg[:c]) + _dot(hi, wg[c:])
    hu = _dot(lo, wu[:c]) + _dot(hi, wu[c:])
    return _dot((_silu(hg) * hu).astype(BF16), wd)


def _expert_kernel(te_ref, valid_ref, x_ref, wg_ref, wu_ref, wd_ref, y_ref, wg_bf, wu_bf, wd_bf):
    i = pl.program_id(0)
    valid = valid_ref[i]

    @pl.when((i == 0) | (te_ref[i] != te_ref[jnp.maximum(i - 1, 0)]))
    def _():
        wg_bf[...] = wg_ref[0].astype(BF16)
        wu_bf[...] = wu_ref[0].astype(BF16)
        wd_bf[...] = wd_ref[0].astype(BF16)

    @pl.when(valid > 0)
    def _():
        w = x_ref[...]
        row = lax.broadcasted_iota(jnp.int32, w.shape, 0)
        lo, hi = _unpack_halves(jnp.where(row < valid, w, 0))
        y_ref[...] = _pack_halves(_swiglu_halves(lo.astype(BF16), hi.astype(BF16), wg_bf[...], wu_bf[...], wd_bf[...]))


def _experts(xs, tile_expert, tile_valid, wg, wu, wd):
    n_slots, c = xs.shape
    n_tiles = n_slots // EXPERT_TILE
    d = 2 * c
    rows = pl.BlockSpec((EXPERT_TILE, c), lambda i, te, tv: (i, 0))
    weight = lambda shape: pl.BlockSpec((1,) + shape, lambda i, te, tv: (te[i], 0, 0))
    return pl.pallas_call(
        _expert_kernel,
        grid_spec=pltpu.PrefetchScalarGridSpec(
            num_scalar_prefetch=2, grid=(n_tiles,),
            in_specs=[rows, weight((d, D_EXPERT)), weight((d, D_EXPERT)), weight((D_EXPERT, d))],
            out_specs=rows,
            scratch_shapes=[pltpu.VMEM((d, D_EXPERT), BF16), pltpu.VMEM((d, D_EXPERT), BF16),
                            pltpu.VMEM((D_EXPERT, d), BF16)]),
        out_shape=jax.ShapeDtypeStruct((n_slots, c), jnp.int32),
        compiler_params=_params("arbitrary"),
        name="moe_experts",
    )(tile_expert, tile_valid, xs, wg, wu, wd)


def _gather_rows(table, idx):
    b, c = idx.shape[0], table.shape[1]
    info, mesh, per_worker = _subcore_ranges(b)

    n_steps = per_worker // GATHER_ROWS
    assert n_steps % 2 == 0

    @functools.partial(
        pl.kernel, mesh=mesh, out_type=jax.ShapeDtypeStruct((b, c), jnp.int32),
        scratch_types=[pltpu.VMEM((2, GATHER_ROWS), jnp.int32), pltpu.VMEM((2, GATHER_ROWS, c), jnp.int32),
                       pltpu.SemaphoreType.DMA((2,))])
    def gather(table_hbm, idx_hbm, out_hbm, idx_v, rows_v, sems):
        worker = lax.axis_index("s") * info.num_cores + lax.axis_index("c")
        base = worker * per_worker

        def chunk(step):
            return pl.ds(pl.multiple_of(base + step * GATHER_ROWS, GATHER_ROWS), GATHER_ROWS)

        def stream(buf):
            return pltpu.make_async_copy(table_hbm.at[idx_v.at[buf]], rows_v.at[buf], sems.at[buf])

        def start(step, buf):
            pltpu.sync_copy(idx_hbm.at[chunk(step)], idx_v.at[buf])
            stream(buf).start()

        def finish(step, buf):
            stream(buf).wait()
            pltpu.sync_copy(rows_v.at[buf], out_hbm.at[chunk(step)])

        start(0, 0)

        @pl.loop(0, n_steps, step=2)
        def _(step):
            start(step + 1, 1)
            finish(step, 0)

            @pl.when(step + 2 < n_steps)
            def _():
                start(step + 2, 0)

            finish(step + 1, 1)

    return gather(table, idx)


def _combine_kernel(rows_ref, gates_ref, hf_ref, x1_ref, mod_ref, sg_ref, su_ref, sd_ref, gpost_ref, y_ref):
    nb, tt, d = hf_ref.shape
    m = nb * tt
    c = d // 2
    x = hf_ref[...].reshape(m, d)
    shared = _swiglu_halves(x[:, :c], x[:, c:], sg_ref[...], su_ref[...], sd_ref[...])
    gates = gates_ref[...].reshape(m, LANES)
    acc_lo = shared[:, :c]
    acc_hi = shared[:, c:]
    for r in range(TOP_K):
        lo, hi = _unpack_halves(rows_ref[r])
        g = gates[:, r:r + 1]
        acc_lo = acc_lo + g * lo
        acc_hi = acc_hi + g * hi
    z = _rms(jnp.concatenate([acc_lo, acc_hi], axis=1), gpost_ref[...]).reshape(nb, tt, d)
    y_ref[...] = x1_ref[...] + mod_ref[...][:, 5:6, :] * z


def _combine(rows, first_tile, gates_t, hf, x1, mod, sg, su, sd, g_post, blocks=None, y_prev=None):
    n, t, d = hf.shape
    nb, tt = _seq_blocks(n, t)
    m = nb * tt
    steps_t = t // tt
    b0, nblocks = (0, n // nb) if blocks is None else blocks
    blk = lambda w: pl.BlockSpec((nb, tt, w), lambda i, j: (b0 + i, j, 0))
    const = lambda shape: pl.BlockSpec(shape, lambda i, j: (0,) * len(shape))
    in_specs = [pl.BlockSpec((TOP_K, m, d // 2), lambda i, j: (0, first_tile + i * steps_t + j, 0)),
                blk(LANES), blk(d), blk(d), pl.BlockSpec((nb, 6, d), lambda i, j: (b0 + i, 0, 0)),
                const((d, D_EXPERT)), const((d, D_EXPERT)), const((D_EXPERT, d)), const((1, d))]
    args = (rows, gates_t, hf, x1, mod, sg, su, sd, g_post)
    kernel_fn, aliases = _combine_kernel, {}
    if y_prev is not None:
        in_specs.append(pl.BlockSpec(memory_space=pl.ANY))
        args += (y_prev,)
        aliases = {len(args) - 1: 0}
        kernel_fn = lambda *refs: _combine_kernel(*refs[:len(args) - 1], refs[-1])
    return pl.pallas_call(
        kernel_fn,
        grid=(nblocks, steps_t),
        in_specs=in_specs,
        out_specs=blk(d),
        out_shape=jax.ShapeDtypeStruct((n, t, d), F32),
        input_output_aliases=aliases,
        compiler_params=_params("parallel", "parallel"),
        name="moe_combine",
    )(*args)


def _slots_kernel(starts_ref, ids_ref, ranks_ref, pos_ref):
    ids = ids_ref[...]

    def add_start(e, pos):
        return pos + jnp.where(ids == e, starts_ref[e], 0)

    pos_ref[...] = lax.fori_loop(0, N_EXPERTS, add_start, ranks_ref[...])


def _slots(starts, ids, ranks):
    k, m_tot = ids.shape
    cols = math.gcd(m_tot, SLOT_COLS)
    assert cols % LANES == 0
    blk = pl.BlockSpec((k, cols), lambda i: (0, i))
    return pl.pallas_call(
        _slots_kernel,
        grid=(m_tot // cols,),
        in_specs=[pl.BlockSpec(memory_space=pltpu.SMEM), blk, blk],
        out_specs=blk,
        out_shape=jax.ShapeDtypeStruct((k, m_tot), jnp.int32),
        compiler_params=_params("arbitrary"),
        name="moe_slots",
    )(starts, ids, ranks)


def _slot_plan(ids, ranks, counts):
    n_pairs = ids.shape[0] * ids.shape[1]
    n_tiles = -(-(n_pairs + N_EXPERTS * (EXPERT_TILE - 1)) // EXPERT_TILE)
    cnt = counts.reshape(N_EXPERTS).astype(jnp.int32)
    padded = (cnt + EXPERT_TILE - 1) // EXPERT_TILE * EXPERT_TILE
    ends = jnp.cumsum(padded)
    starts = ends - padded
    pos = _slots(starts, ids, ranks)
    tile_start = jnp.arange(n_tiles, dtype=jnp.int32) * EXPERT_TILE
    in_expert = (tile_start[:, None] >= starts[None, :]) & (tile_start[:, None] < ends[None, :])
    tile_expert = jnp.sum(jnp.where(in_expert, jnp.arange(N_EXPERTS, dtype=jnp.int32)[None, :], 0), axis=1)
    tile_fill = jnp.sum(jnp.where(in_expert, (starts + cnt)[None, :] - tile_start[:, None], 0), axis=1)
    tile_valid = jnp.clip(tile_fill, 0, EXPERT_TILE).astype(jnp.int32)
    return pos, tile_expert.astype(jnp.int32), tile_valid, n_tiles * EXPERT_TILE


def _block_diag(w):
    g, bw, _ = w.shape
    eye = jnp.eye(g, dtype=w.dtype)
    return (eye[:, None, :, None] * w[:, :, None, :]).reshape(g * bw, g * bw)


def _prep_weights(p):
    d_main = 2 * WL + 3 * WA
    w_in = p["w_in"]
    rw_t = p["router_w"].T
    rw_hi = rw_t.astype(BF16)
    row = lambda v: v.reshape(1, -1)
    return dict(
        w_mod=p["w_mod"], b_mod=p["b_mod"],
        g_pre_mix=row(p["g_pre_mix"]), g_post_mix=row(p["g_post_mix"]),
        g_pre_ffn=row(p["g_pre_ffn"]), g_post_ffn=row(p["g_post_ffn"]),
        w_main=w_in[:, :d_main].astype(BF16),
        w_f=jnp.pad(w_in[:, d_main:], ((0, 0), (0, LANES - N_HEADS))).astype(BF16),
        b_f=jnp.pad(p["b_f"], (0, LANES - N_HEADS)).reshape(1, LANES),
        conv_w=p["conv_w"], conv_b=row(p["conv_b"]),
        wr_bd=_block_diag(p["w_r"]).astype(BF16), b_r=row(p["b_r"]),
        wi_bd=_block_diag(p["w_i"]).astype(BF16), b_i=row(p["b_i"]),
        lam=row(p["lru_lambda"]), g_lru=row(p["g_lru_out"]), g_att=row(p["g_att_out"]),
        w_top=p["w_out"][:WL].astype(BF16), w_bot=p["w_out"][WL:].astype(BF16),
        rw_hi=rw_hi, rw_lo=(rw_t - rw_hi.astype(F32)).astype(BF16),
        r_bias=p["router_bias"].reshape(N_EXPERTS, 1),
        wg=p["w_gate"], wu=p["w_up"], wd=p["w_down"],
        sg=p["ws_gate"].astype(BF16), su=p["ws_up"].astype(BF16), sd=p["ws_down"].astype(BF16),
    )


def _mixers(x, mod, conv0, h0, past, w, cnt_in, split=None):
    n, t, _ = x.shape
    proj_args = (x, mod, w["g_pre_mix"], w["w_main"], w["w_f"], w["b_f"])
    if past is None:
        xl, gy, k, v, lf, qt, ka, vt, bpre = _inproj_prompt(*proj_args)
        att = _att_prompt(qt, ka, vt, bpre)
    else:
        xl, gy, qb, kb, vb, k, v, lf = _inproj(*proj_args)
        k_past, v_past, lf_past = past
        plen = k_past.shape[1]
        by_head = lambda a: jnp.transpose(a, (0, 2, 1)).reshape(n * N_HEADS, a.shape[1])
        d_new = _cumsum_rows(by_head(lf), t).reshape(n, N_HEADS, t)
        d_past = _cumsum_rows(by_head(lf_past), min(CUMSUM_COLS, plen)).reshape(n, N_HEADS, plen)
        att = _att_sample(qb, k_past, v_past, d_past, kb, vb, d_new)
    lru_n, conv_new, h_new = _lru(xl, gy, conv0, h0.reshape(n, 1, WL), w["conv_w"], w["conv_b"],
                                  w["wr_bd"], w["b_r"], w["wi_bd"], w["b_i"], w["lam"], w["g_lru"])
    routed = _outproj(x, lru_n, att, mod, w["g_att"], w["w_top"], w["w_bot"], w["g_post_mix"],
                      w["g_pre_ffn"], w["rw_hi"], w["rw_lo"], w["r_bias"], cnt_in, split)
    state = (k.reshape(n, t, N_HEADS, HEAD_DIM), v.reshape(n, t, N_HEADS, HEAD_DIM), lf,
             conv_new, h_new.reshape(n, WL))
    return routed, state


def _layer(xp, xs, mod_p, mod_s, conv_s, h_s, past_s, w):
    n_p = xp.shape[0]
    conv0 = jnp.zeros((n_p, CONV_W - 1, WL), F32)
    h0 = jnp.zeros((n_p, WL), F32)
    zero_cnt = jnp.zeros((N_EXPERTS, 1), F32)
    t_p = xp.shape[1]
    split = 1 if n_p > 1 else None
    (x1_p, hf_p, xw_p, ids_p, rk_p, g_p, cnt_p), st_p = _mixers(xp, mod_p, conv0, h0, None, w, zero_cnt, split)
    cnt_tail = cnt_p[1] if n_p > 1 else zero_cnt
    (x1_s, hf_s, xw_s, ids_s, rk_s, g_s, cnt_s), st_s = _mixers(xs, mod_s, conv_s, h_s, past_s, w, cnt_tail)

    half = xw_p.shape[-1]
    xw_p = xw_p.reshape(-1, half)

    def routed(sources, ids, ranks, counts):
        pos, tile_expert, tile_valid, n_slots = _slot_plan(ids, ranks, counts)
        ys = _experts(_scatter_rows(sources, pos, n_slots), tile_expert, tile_valid, w["wg"], w["wu"], w["wd"])
        return _gather_rows(ys, pos.reshape(-1)).reshape(TOP_K, ids.shape[1], half)

    rows_a = routed([(xw_p, 0, t_p)], ids_p[:, :t_p], rk_p[:, :t_p], cnt_p[0])
    xw_s = xw_s.reshape(-1, half)
    sources_b = ([(xw_p, t_p, (n_p - 1) * t_p)] if n_p > 1 else []) + [(xw_s, 0, xw_s.shape[0])]
    rows_b = routed(sources_b, jnp.concatenate([ids_p[:, t_p:], ids_s], axis=1),
                    jnp.concatenate([rk_p[:, t_p:], rk_s], axis=1), cnt_s[0])
    shared = (w["sg"], w["su"], w["sd"], w["g_post_ffn"])
    yp = _combine(rows_a, 0, g_p, hf_p, x1_p, mod_p, *shared, blocks=(0, 1))
    if n_p > 1:
        yp = _combine(rows_b, 0, g_p, hf_p, x1_p, mod_p, *shared, blocks=(1, n_p - 1), y_prev=yp)
    ysmp = _combine(rows_b, (n_p - 1) * t_p // ROW_TILE, g_s, hf_s, x1_s, mod_s, *shared)
    return yp, ysmp, st_p, st_s


def kernel(x_prompt, x_sample, c_prompt, c_sample, cache_k, cache_v, cache_logf, state_conv, state_lru, w_mod, b_mod, g_pre_mix, g_post_mix, g_pre_ffn, g_post_ffn, w_in, conv_w, conv_b, w_r, b_r, w_i, b_i, lru_lambda, b_f, g_lru_out, g_att_out, w_out, router_w, router_bias, w_gate, w_up, w_down, ws_gate, ws_up, ws_down):
    names = ("w_mod", "b_mod", "g_pre_mix", "g_post_mix", "g_pre_ffn", "g_post_ffn", "w_in", "conv_w", "conv_b",
             "w_r", "b_r", "w_i", "b_i", "lru_lambda", "b_f", "g_lru_out", "g_att_out", "w_out", "router_w",
             "router_bias", "w_gate", "w_up", "w_down", "ws_gate", "ws_up", "ws_down")
    stacked = (w_mod, b_mod, g_pre_mix, g_post_mix, g_pre_ffn, g_post_ffn, w_in, conv_w, conv_b, w_r, b_r, w_i, b_i,
               lru_lambda, b_f, g_lru_out, g_att_out, w_out, router_w, router_bias, w_gate, w_up, w_down,
               ws_gate, ws_up, ws_down)
    depth = w_mod.shape[0]
    n_p, n_s = x_prompt.shape[0], x_sample.shape[0]
    yp, ys = x_prompt, x_sample
    st_p, st_s = [], []
    for l in range(depth):
        w = _prep_weights({k: v[l] for k, v in zip(names, stacked)})
        mod = _modulation(jnp.concatenate([c_prompt, c_sample], axis=0), w["w_mod"], w["b_mod"])
        mod = mod.reshape(n_p + n_s, 6, D_MODEL)
        yp, ys, sp, ss = _layer(yp, ys, mod[:n_p], mod[n_p:], state_conv[l], state_lru[l],
                                (cache_k[l], cache_v[l], cache_logf[l]), w)
        st_p.append(sp)
        st_s.append(ss)
    stack = lambda sts, i: jnp.stack([s[i] for s in sts])
    return (yp, ys) + tuple(stack(st_p, i) for i in range(5)) + tuple(stack(st_s, i) for i in range(5))
```

```python
import functools
import math

import jax
import jax.numpy as jnp
import numpy as np
from jax import lax
from jax.experimental import pallas as pl
from jax.experimental.pallas import tpu as pltpu
from jax.experimental.pallas import tpu_sc as plsc

F32 = jnp.float32
BF16 = jnp.bfloat16

D_MODEL = 1024
WL = 512
WA = 512
N_HEADS = 8
HEAD_DIM = 64
N_PAIRS = N_HEADS // 2
PAIR_W = 2 * HEAD_DIM
LANES = 128
CONV_W = 4
LRU_BLOCKS = 8
LRU_C = 8.0
N_EXPERTS = 64
N_GROUPS = 8
GROUP_SIZE = N_EXPERTS // N_GROUPS
TOPK_GROUPS = 4
TOP_K = 8
D_EXPERT = 256
ROUTED_SCALE = 2.5
EPS = 1e-6
NEG_INF = float("-inf")
LOG2E = 1.4426950408889634

ROW_TILE = 512
ATT_TQ = 512
ATT_TK = 512
ATT_CHUNK = 64
ATT_HEADS = 4
V_EXTRA = 16
SAMPLE_TK = 4096
CUMSUM_COLS = 1024
EXPERT_TILE = 512
SLOT_COLS = 2048
GATHER_ROWS = 64
VMEM_LIMIT = 56 * 1024 * 1024


def _params(*sem):
    return pltpu.CompilerParams(dimension_semantics=sem, vmem_limit_bytes=VMEM_LIMIT)


def _dot(a, b):
    return jnp.dot(a, b, preferred_element_type=F32)


def _dot_nt(a, b):
    return lax.dot_general(a, b, (((1,), (1,)), ((), ())), preferred_element_type=F32)


def _split3(x):
    hi = x.astype(BF16)
    r1 = x - hi.astype(F32)
    mid = r1.astype(BF16)
    lo = (r1 - mid.astype(F32)).astype(BF16)
    return hi, mid, lo


def _rms(x, g):
    return x * lax.rsqrt(jnp.mean(x * x, axis=-1, keepdims=True) + EPS) * g


def _sigmoid(x):
    return 1.0 / (1.0 + jnp.exp(-x))


def _silu(x):
    return x * _sigmoid(x)


def _gelu_tanh(x):
    return 0.5 * x * (1.0 + jnp.tanh(0.7978845608028654 * (x + 0.044715 * (x * x * x))))


def _log_sigmoid(x):
    return jnp.minimum(x, 0.0) - jnp.log1p(jnp.exp(-jnp.abs(x)))


def _seq_blocks(n, t):
    if t >= ROW_TILE:
        assert t % ROW_TILE == 0
        return 1, ROW_TILE
    nb = ROW_TILE // t
    assert nb * t == ROW_TILE and n % nb == 0
    return nb, t


def _mod_kernel(c_ref, w_ref, b_ref, o_ref):
    c = _silu(c_ref[...])
    c_hi = c.astype(BF16)
    c_lo = (c - c_hi.astype(F32)).astype(BF16)
    w = w_ref[...]
    w_hi = w.astype(BF16)
    w_lo = (w - w_hi.astype(F32)).astype(BF16)
    o_ref[...] = _dot(c_hi, w_hi) + _dot(c_lo, w_hi) + _dot(c_hi, w_lo) + b_ref[...]


def _modulation(c, w_mod, b_mod):
    rows = c.shape[0]
    n = -(-rows // 8) * 8
    c = jnp.pad(c, ((0, n - rows), (0, 0)))
    d6 = w_mod.shape[1]
    return pl.pallas_call(
        _mod_kernel,
        grid=(d6 // D_MODEL,),
        in_specs=[pl.BlockSpec((n, D_MODEL), lambda j: (0, 0)),
                  pl.BlockSpec((D_MODEL, D_MODEL), lambda j: (0, j)),
                  pl.BlockSpec((1, D_MODEL), lambda j: (0, j))],
        out_specs=pl.BlockSpec((n, D_MODEL), lambda j: (0, j)),
        out_shape=jax.ShapeDtypeStruct((n, d6), F32),
        compiler_params=_params("arbitrary"),
        name="modulation",
    )(c, w_mod, b_mod.reshape(1, d6))[:rows]


def _inproj_kernel(x_ref, mod_ref, g_ref, w_ref, wf_ref, bf_ref,
                   xl_ref, gy_ref, qb_ref, kb_ref, vb_ref, k_ref, v_ref, lf_ref):
    nb, tt, d = x_ref.shape
    x = x_ref[...]
    mod = mod_ref[...]
    hn = _rms(x, g_ref[...]) * (1.0 + mod[:, 1:2, :]) + mod[:, 0:1, :]
    hb = hn.reshape(nb * tt, d).astype(BF16)

    def proj(col):
        return _dot(hb, w_ref[:, col * WL:(col + 1) * WL]).reshape(nb, tt, WL)

    xl_ref[...] = proj(0)
    gy_ref[...] = _gelu_tanh(proj(1)).astype(BF16)
    qb_ref[...] = (proj(2) * (HEAD_DIM ** -0.5)).astype(BF16)
    k = proj(3)
    k_ref[...] = k
    kb_ref[...] = k.astype(BF16)
    v = proj(4)
    v_ref[...] = v
    vb_ref[...] = v.astype(BF16)
    fl = _dot(hb, wf_ref[...]) + bf_ref[...]
    lf_ref[...] = _log_sigmoid(fl).reshape(nb, tt, LANES)[:, :, :N_HEADS]


def _inproj(x, mod, g_pre, w_main, w_f, b_f):
    n, t, d = x.shape
    nb, tt = _seq_blocks(n, t)
    blk = lambda w: pl.BlockSpec((nb, tt, w), lambda i, j: (i, j, 0))
    const = lambda shape: pl.BlockSpec(shape, lambda i, j: (0,) * len(shape))
    f32 = lambda w: jax.ShapeDtypeStruct((n, t, w), F32)
    b16 = lambda w: jax.ShapeDtypeStruct((n, t, w), BF16)
    return pl.pallas_call(
        _inproj_kernel,
        grid=(n // nb, t // tt),
        in_specs=[blk(d),
                  pl.BlockSpec((nb, 6, d), lambda i, j: (i, 0, 0)),
                  const((1, d)), const(w_main.shape), const(w_f.shape), const((1, LANES))],
        out_specs=[blk(WL), blk(WL), blk(WA), blk(WA), blk(WA), blk(WA), blk(WA), blk(N_HEADS)],
        out_shape=[f32(WL), b16(WL), b16(WA), b16(WA), b16(WA), f32(WA), f32(WA), f32(N_HEADS)],
        compiler_params=_params("parallel", "arbitrary"),
        name="inproj",
    )(x, mod, g_pre, w_main, w_f, b_f)


def _aug_lane(h):
    return HEAD_DIM if h % 2 == 0 else HEAD_DIM - 3


def _ones_lane(h):
    return HEAD_DIM if h % 2 == 0 else HEAD_DIM - 1


def _v_rows(h):
    return slice(0, HEAD_DIM + V_EXTRA) if h % 2 == 0 else slice(HEAD_DIM - V_EXTRA, 2 * HEAD_DIM)


def _inproj_prompt_kernel(x_ref, mod_ref, g_ref, w_ref, wf_ref, bf_ref, place_ref,
                          xl_ref, gy_ref, k_ref, v_ref, lf_ref, qt_ref, ka_ref, vt_ref, bpre_ref, carry_ref):
    _, tt, d = x_ref.shape

    @pl.when(pl.program_id(1) == 0)
    def _():
        carry_ref[...] = jnp.zeros_like(carry_ref)

    mod = mod_ref[0]
    hb = (_rms(x_ref[0], g_ref[...]) * (1.0 + mod[1:2, :]) + mod[0:1, :]).astype(BF16)

    def proj(col):
        return _dot(hb, w_ref[:, col * WL:(col + 1) * WL])

    xl_ref[0] = proj(0)
    gy_ref[0] = _gelu_tanh(proj(1)).astype(BF16)
    q = proj(2) * (HEAD_DIM ** -0.5 * LOG2E)
    k = proj(3)
    k_ref[0] = k
    v = proj(4)
    v_ref[0] = v
    fl = _dot(hb, wf_ref[...]) + bf_ref[...]
    lane = lax.broadcasted_iota(jnp.int32, (tt, LANES), 1)
    lf = jnp.where(lane < N_HEADS, _log_sigmoid(fl), 0.0)
    lf_ref[0] = lf[:, :N_HEADS]

    row = lax.broadcasted_iota(jnp.int32, (tt, tt), 0)
    col = lax.broadcasted_iota(jnp.int32, (tt, tt), 1)
    tril = (col <= row).astype(BF16)
    hi, mid, lo = _split3(lf)
    e = _dot(tril, hi) + _dot(tril, mid) + _dot(tril, lo)
    bpre_ref[0, 0] = carry_ref[...]
    carry_ref[...] += e[tt - 1:tt, :]
    e_hi, e_mid, e_lo = _split3(e * (-LOG2E))
    aug_k = _dot(jnp.concatenate([e_hi, e_mid, e_lo], axis=1), place_ref[...])

    for h in range(N_HEADS):
        pair = slice((h // 2) * PAIR_W, (h // 2 + 1) * PAIR_W)
        dims = (lane < HEAD_DIM) if h % 2 == 0 else (lane >= HEAD_DIM)
        a0 = _aug_lane(h)
        ones3 = ((lane >= a0) & (lane < a0 + 3)).astype(F32)
        qt_ref[0, h] = (jnp.where(dims, q[:, pair], 0.0) + ones3).T.astype(BF16)
        ka_ref[0, h] = (jnp.where(dims, k[:, pair], 0.0) + aug_k[:, h * LANES:(h + 1) * LANES]).astype(BF16)
        one1 = (lane == _ones_lane(h)).astype(F32)
        vt_ref[0, h] = (jnp.where(dims, v[:, pair], 0.0) + one1).T.astype(BF16)


def _placement():
    pl_mat = np.zeros((3 * LANES, N_HEADS * LANES), np.float32)
    for p in range(3):
        for h in range(N_HEADS):
            pl_mat[p * LANES + h, h * LANES + _aug_lane(h) + p] = 1.0
    return jnp.asarray(pl_mat, BF16)


def _inproj_prompt(x, mod, g_pre, w_main, w_f, b_f):
    n, t, d = x.shape
    tt = ROW_TILE
    assert t % tt == 0
    nblk = t // tt
    blk = lambda w: pl.BlockSpec((1, tt, w), lambda i, j: (i, j, 0))
    const = lambda shape: pl.BlockSpec(shape, lambda i, j: (0,) * len(shape))
    f32 = lambda w: jax.ShapeDtypeStruct((n, t, w), F32)
    place = _placement()
    return pl.pallas_call(
        _inproj_prompt_kernel,
        grid=(n, nblk),
        in_specs=[blk(d), pl.BlockSpec((1, 6, d), lambda i, j: (i, 0, 0)),
                  const((1, d)), const(w_main.shape), const(w_f.shape), const((1, LANES)), const(place.shape)],
        out_specs=[blk(WL), blk(WL), blk(WA), blk(WA), blk(N_HEADS),
                   pl.BlockSpec((1, N_HEADS, LANES, tt), lambda i, j: (i, 0, 0, j)),
                   pl.BlockSpec((1, N_HEADS, tt, LANES), lambda i, j: (i, 0, j, 0)),
                   pl.BlockSpec((1, N_HEADS, LANES, tt), lambda i, j: (i, 0, 0, j)),
                   pl.BlockSpec((1, 1, 1, LANES), lambda i, j: (i, j, 0, 0))],
        out_shape=[f32(WL), jax.ShapeDtypeStruct((n, t, WL), BF16), f32(WA), f32(WA), f32(N_HEADS),
                   jax.ShapeDtypeStruct((n, N_HEADS, LANES, t), BF16),
                   jax.ShapeDtypeStruct((n, N_HEADS, t, LANES), BF16),
                   jax.ShapeDtypeStruct((n, N_HEADS, LANES, t), BF16),
                   jax.ShapeDtypeStruct((n, nblk, 1, LANES), F32)],
        scratch_shapes=[pltpu.VMEM((1, LANES), F32)],
        compiler_params=_params("parallel", "arbitrary"),
        name="inproj_prompt",
    )(x, mod, g_pre, w_main, w_f, b_f, place)


def _expm1_neg(x):
    poly = x * (1.0 + x * (0.5 + x * (1.0 / 6.0 + x * (1.0 / 24.0 + x * (1.0 / 120.0)))))
    return jnp.where(x > -0.1, poly, jnp.exp(x) - 1.0)


def _lru_kernel(xl_ref, gy_ref, conv0_ref, h0_ref, cw_ref, cb_ref, wr_ref, br_ref, wi_ref, bi_ref,
                lam_ref, g_ref, out_ref, conv_ref, hlast_ref, tail_ref, carry_ref):
    nb, tt, w = xl_ref.shape
    j = pl.program_id(1)

    @pl.when(j == 0)
    def _():
        tail_ref[:, 8 - (CONV_W - 1):, :] = conv0_ref[...]
        carry_ref[...] = h0_ref[...]

    xl = xl_ref[...]
    xpad = jnp.concatenate([tail_ref[...], xl], axis=1)
    cw = cw_ref[...]
    xc = jnp.zeros_like(xl) + cb_ref[...]
    for k in range(CONV_W):
        off = 8 - (CONV_W - 1) + k
        xc = xc + xpad[:, off:off + tt, :] * cw[k:k + 1, :]
    conv_ref[...] = xpad[:, tt + 8 - (CONV_W - 1):, :]
    tail_ref[...] = xpad[:, tt:, :]

    m = nb * tt
    xf = xc.reshape(m, w)
    xb = xf.astype(BF16)
    r = _sigmoid(_dot(xb, wr_ref[...]) + br_ref[...])
    gi = _sigmoid(_dot(xb, wi_ref[...]) + bi_ref[...])
    lam = lam_ref[...]
    softplus = jnp.maximum(-lam, 0.0) + jnp.log1p(jnp.exp(-jnp.abs(lam)))
    log_a = (-LRU_C) * r * softplus
    a = jnp.exp(log_a)
    b = jnp.sqrt(-_expm1_neg(2.0 * log_a)) * (gi * xf)

    groups = m // 8
    a = a.reshape(groups, 8, w)
    b = b.reshape(groups, 8, w)
    sub = lax.broadcasted_iota(jnp.int32, (groups, 8, w), 1)
    for d in (1, 2, 4):
        keep = sub >= d
        a_prev = jnp.where(keep, pltpu.roll(a, d, 1), 1.0)
        b_prev = jnp.where(keep, pltpu.roll(b, d, 1), 0.0)
        b = a * b_prev + b
        a = a * a_prev
    carry = carry_ref[...]
    groups_per_seq = tt // 8
    rows = []
    for g in range(groups):
        if g % groups_per_seq == 0:
            prev = carry[g // groups_per_seq]
        h_g = a[g] * prev + b[g]
        prev = h_g[7:8]
        rows.append(h_g)
    h = jnp.concatenate(rows, axis=0).reshape(nb, tt, w)
    h_last = h[:, tt - 1:tt, :]
    carry_ref[...] = h_last
    hlast_ref[...] = h_last
    out_ref[...] = _rms(h * gy_ref[...].astype(F32), g_ref[...]).astype(BF16)


def _lru(xl, gy, conv0, h0, conv_w, conv_b, wr_bd, b_r, wi_bd, b_i, lam, g_lru):
    n, t, w = xl.shape
    nb, tt = _seq_blocks(n, t)
    blk = pl.BlockSpec((nb, tt, w), lambda i, j: (i, j, 0))
    per_seq = lambda rows: pl.BlockSpec((nb, rows, w), lambda i, j: (i, 0, 0))
    const = lambda shape: pl.BlockSpec(shape, lambda i, j: (0,) * len(shape))
    row = const((1, w))
    return pl.pallas_call(
        _lru_kernel,
        grid=(n // nb, t // tt),
        in_specs=[blk, blk, per_seq(CONV_W - 1), per_seq(1),
                  const((CONV_W, w)), row, const((w, w)), row, const((w, w)), row, row, row],
        out_specs=[blk, per_seq(CONV_W - 1), per_seq(1)],
        out_shape=[jax.ShapeDtypeStruct((n, t, w), BF16),
                   jax.ShapeDtypeStruct((n, CONV_W - 1, w), F32),
                   jax.ShapeDtypeStruct((n, 1, w), F32)],
        scratch_shapes=[pltpu.VMEM((nb, 8, w), F32), pltpu.VMEM((nb, 1, w), F32)],
        compiler_params=_params("parallel", "arbitrary"),
        name="rglru",
    )(xl, gy, conv0, h0, conv_w, conv_b, wr_bd, b_r, wi_bd, b_i, lam, g_lru)


def _cumsum_rows_kernel(x_ref, upper_ref, o_ref, carry_ref):
    tb = x_ref.shape[1]

    @pl.when(pl.program_id(0) == 0)
    def _():
        carry_ref[...] = jnp.zeros_like(carry_ref)

    upper = upper_ref[...]
    hi, mid, lo = _split3(x_ref[...])
    d = _dot(hi, upper) + _dot(mid, upper) + _dot(lo, upper) + carry_ref[...]
    carry_ref[...] = d[:, tb - 1:tb]
    o_ref[...] = d


def _cumsum_rows(x, tb):
    rows, t = x.shape
    upper = jnp.asarray(np.triu(np.ones((tb, tb), np.float32)), BF16)
    return pl.pallas_call(
        _cumsum_rows_kernel,
        grid=(t // tb,),
        in_specs=[pl.BlockSpec((rows, tb), lambda j: (0, j)), pl.BlockSpec((tb, tb), lambda j: (0, 0))],
        out_specs=pl.BlockSpec((rows, tb), lambda j: (0, j)),
        out_shape=jax.ShapeDtypeStruct((rows, t), F32),
        scratch_shapes=[pltpu.VMEM((rows, 1), F32)],
        compiler_params=_params("arbitrary"),
        name="logf_cumsum",
    )(x, upper)


def _online_update(s, m_prev, l_prev):
    m_new = jnp.maximum(m_prev, jnp.max(s, axis=1, keepdims=True))
    alpha = jnp.exp(m_prev - m_new)
    p = jnp.exp(s - m_new)
    l_new = alpha * l_prev + jnp.sum(p, axis=1, keepdims=True)
    return p, alpha, m_new, l_new


def _att_prompt_kernel(bpre_ref, qt_ref, ka_ref, vt_ref, o_ref, s_ref, p_ref, acc_ref):
    tq = qt_ref.shape[2]
    tk = ATT_TK
    nblk = ka_ref.shape[1] // tk
    b, hp, i = pl.program_id(0), pl.program_id(1), pl.program_id(2)
    q0 = i * tq
    jd = q0 // tk
    kpos = lax.broadcasted_iota(jnp.int32, (tk, tq), 0)
    qpos = lax.broadcasted_iota(jnp.int32, (tk, tq), 1)
    nh = qt_ref.shape[0]
    heads = range(nh)
    base = [((b * (N_HEADS // nh) + hp) * nh + hh) * nblk for hh in heads]

    def scores(j, masked):
        start = pl.multiple_of(j * tk, tk)
        col_max = []
        for hh in heads:
            s = _dot(ka_ref[hh, pl.ds(start, tk), :], qt_ref[hh])
            if masked:
                s = jnp.where(kpos + start <= qpos + q0, s, NEG_INF)
            s_ref[hh] = s
            col_max.append(jnp.max(s, axis=0, keepdims=True))
        return tuple(col_max)

    def softmax_pv(j, col_max, m):
        start = pl.multiple_of(j * tk, tk)
        m_out = []
        for hh in heads:
            c = (bpre_ref[base[hh] + jd] - bpre_ref[base[hh] + j]) * LOG2E
            m_new = jnp.maximum(m[hh], col_max[hh] + c)
            alpha = jnp.exp2(m[hh] - m_new)
            shift = m_new - c
            for ch in range(tk // ATT_CHUNK):
                sl = slice(ch * ATT_CHUNK, (ch + 1) * ATT_CHUNK)
                p_ref[hh, sl, :] = jnp.exp2(s_ref[hh, sl, :] - shift).astype(BF16)
            m_out.append((m_new, alpha))
        return tuple(m_out), start

    def accumulate(m_alpha, start):
        for hh in heads:
            pv = _dot(vt_ref[hh, _v_rows(hh), pl.ds(start, tk)], p_ref[hh])
            acc_ref[hh] = m_alpha[hh][1] * acc_ref[hh] + pv
        return tuple(ma[0] for ma in m_alpha)

    def step(j, next_masked, carry):
        col_max, m = carry
        m_alpha, start = softmax_pv(j, col_max, m)
        col_max_next = scores(j + 1, next_masked)
        return col_max_next, accumulate(m_alpha, start)

    acc_ref[...] = jnp.zeros_like(acc_ref)
    neg = jnp.full((1, tq), NEG_INF, F32)
    n_masked = max(1, tq // tk)
    carry = (scores(0, True), (neg,) * nh)
    carry = lax.fori_loop(0, jd - 1, lambda j, cr: step(j, False, cr), carry)
    carry = lax.cond(jd > 0, lambda cr: step(jd - 1, True, cr), lambda cr: cr, carry)
    for extra in range(n_masked - 1):
        carry = step(jd + extra, True, carry)
    col_max, m = carry
    m_alpha, start = softmax_pv(jd + n_masked - 1, col_max, m)
    accumulate(m_alpha, start)

    for pair in range(nh // 2):
        acc_a, acc_b = acc_ref[2 * pair], acc_ref[2 * pair + 1]
        out_a = acc_a[:HEAD_DIM] / acc_a[HEAD_DIM:HEAD_DIM + 1]
        out_b = acc_b[V_EXTRA:] / acc_b[V_EXTRA - 1:V_EXTRA]
        o_ref[0, :, pair * PAIR_W:(pair + 1) * PAIR_W] = jnp.concatenate([out_a, out_b], axis=0).T


def _att_prompt(qt, ka, vt, bpre):
    n, _, _, t = qt.shape
    assert t % ATT_TQ == 0 and (ATT_TK % ATT_TQ == 0 or ATT_TQ % ATT_TK == 0) and ATT_TK == ROW_TILE
    nh = ATT_HEADS
    grouped = lambda q: (N_HEADS // nh, nh) + q.shape[2:]
    qt, ka, vt = (a.reshape((n,) + grouped(a)) for a in (qt, ka, vt))
    bflat = jnp.transpose(bpre[:, :, 0, :N_HEADS], (0, 2, 1)).reshape(-1)
    return pl.pallas_call(
        _att_prompt_kernel,
        grid=(n, N_HEADS // nh, t // ATT_TQ),
        in_specs=[pl.BlockSpec(memory_space=pltpu.SMEM),
                  pl.BlockSpec((None, None, nh, LANES, ATT_TQ), lambda b, h, i: (b, h, 0, 0, i)),
                  pl.BlockSpec((None, None, nh, t, LANES), lambda b, h, i: (b, h, 0, 0, 0),
                               pipeline_mode=pl.Buffered(1)),
                  pl.BlockSpec((None, None, nh, LANES, t), lambda b, h, i: (b, h, 0, 0, 0),
                               pipeline_mode=pl.Buffered(1))],
        out_specs=pl.BlockSpec((1, ATT_TQ, nh * HEAD_DIM), lambda b, h, i: (b, i, h)),
        out_shape=jax.ShapeDtypeStruct((n, t, WA), F32),
        scratch_shapes=[pltpu.VMEM((nh, ATT_TK, ATT_TQ), F32), pltpu.VMEM((nh, ATT_TK, ATT_TQ), BF16),
                        pltpu.VMEM((nh, HEAD_DIM + V_EXTRA, ATT_TQ), F32)],
        compiler_params=_params("parallel", "parallel", "arbitrary"),
        name="att_prompt",
    )(bflat, qt, ka, vt)


def _att_sample_kernel(q_ref, kp_ref, vp_ref, dp_ref, dend_ref, kn_ref, vn_ref, dn_ref, o_ref,
                       m_ref, l_ref, acc_ref):
    t = q_ref.shape[1]
    j = pl.program_id(1)

    @pl.when(j == 0)
    def _():
        m_ref[...] = jnp.full_like(m_ref, NEG_INF)
        l_ref[...] = jnp.zeros_like(l_ref)
        acc_ref[...] = jnp.zeros_like(acc_ref)

    def head(h, k, v, time_minor, bias, mask):
        sl = slice(h * HEAD_DIM, (h + 1) * HEAD_DIM)
        q_h = q_ref[0, :, sl]
        s = (_dot(q_h, k) if time_minor else _dot_nt(q_h, k)) + bias[h:h + 1]
        if mask is not None:
            s = jnp.where(mask, s, NEG_INF)
        p, alpha, m_new, l_new = _online_update(s, m_ref[h], l_ref[h])
        p = p.astype(BF16)
        acc = alpha * acc_ref[:, sl] + (_dot_nt(p, v) if time_minor else _dot(p, v))
        m_ref[h], l_ref[h], acc_ref[:, sl] = m_new, l_new, acc
        return acc / l_new

    bias_past = dend_ref[0] - dp_ref[0]
    for h in range(N_HEADS):
        head(h, kp_ref[0, h].astype(BF16), vp_ref[0, h].astype(BF16), True, bias_past, None)

    @pl.when(j == pl.num_programs(1) - 1)
    def _():
        bias_new = -dn_ref[0]
        qpos = lax.broadcasted_iota(jnp.int32, (t, t), 0)
        kpos = lax.broadcasted_iota(jnp.int32, (t, t), 1)
        for h in range(N_HEADS):
            sl = slice(h * HEAD_DIM, (h + 1) * HEAD_DIM)
            o_ref[0, :, sl] = head(h, kn_ref[0, :, sl], vn_ref[0, :, sl], False, bias_new, kpos <= qpos)


def _att_sample(qb, k_past, v_past, d_past, kb_new, vb_new, d_new):
    n, t, _ = qb.shape
    past = k_past.shape[1]
    tk = min(SAMPLE_TK, past)
    assert past % tk == 0
    d_end = d_past[:, :, past - 1:]
    k_past = jnp.transpose(k_past, (0, 2, 3, 1))
    v_past = jnp.transpose(v_past, (0, 2, 3, 1))
    new = lambda dt: pl.BlockSpec((1, t, WA), lambda b, j: (b, 0, 0))
    cache = pl.BlockSpec((1, N_HEADS, HEAD_DIM, tk), lambda b, j: (b, 0, 0, j))
    return pl.pallas_call(
        _att_sample_kernel,
        grid=(n, past // tk),
        in_specs=[new(BF16), cache, cache,
                  pl.BlockSpec((1, N_HEADS, tk), lambda b, j: (b, 0, j)),
                  pl.BlockSpec((1, N_HEADS, 1), lambda b, j: (b, 0, 0)),
                  new(BF16), new(BF16),
                  pl.BlockSpec((1, N_HEADS, t), lambda b, j: (b, 0, 0))],
        out_specs=pl.BlockSpec((1, t, WA), lambda b, j: (b, 0, 0)),
        out_shape=jax.ShapeDtypeStruct((n, t, WA), F32),
        scratch_shapes=[pltpu.VMEM((N_HEADS, t, 1), F32), pltpu.VMEM((N_HEADS, t, 1), F32),
                        pltpu.VMEM((t, WA), F32)],
        compiler_params=_params("parallel", "arbitrary"),
        name="att_sample",
    )(qb, k_past, v_past, d_past, d_end, kb_new, vb_new, d_new)


def _first_index_of_max(x, axis):
    mx = jnp.max(x, axis=axis, keepdims=True)
    idx = lax.broadcasted_iota(jnp.int32, x.shape, axis)
    first = jnp.min(jnp.where(x == mx, idx, x.shape[axis]), axis=axis, keepdims=True)
    return mx, idx == first


def _route(s, bias):
    m = s.shape[1]
    sb = (s + bias).reshape(N_GROUPS, GROUP_SIZE, m)
    top1, is_top1 = _first_index_of_max(sb, 1)
    top2 = jnp.max(jnp.where(is_top1, NEG_INF, sb), axis=1, keepdims=True)
    grp = (top1 + top2).reshape(N_GROUPS, m)
    gi = lax.broadcasted_iota(jnp.int32, (N_GROUPS, N_GROUPS, m), 0)
    gj = lax.broadcasted_iota(jnp.int32, (N_GROUPS, N_GROUPS, m), 1)
    other, mine = grp[None, :, :], grp[:, None, :]
    beats = (other > mine) | ((other == mine) & (gj < gi))
    g_rank = jnp.sum(beats.astype(jnp.int32), axis=1)
    g_keep = (g_rank < TOPK_GROUPS)[:, None, :]
    cand = jnp.where(g_keep, sb, NEG_INF).reshape(N_EXPERTS, m)
    picks = []
    for _ in range(TOP_K):
        _, pick = _first_index_of_max(cand, 0)
        picks.append(pick)
        cand = jnp.where(pick, NEG_INF, cand)
    w = jnp.concatenate([jnp.sum(jnp.where(pk, s, 0.0), axis=0, keepdims=True) for pk in picks], axis=0)
    return picks, w / jnp.sum(w, axis=0, keepdims=True) * ROUTED_SCALE


HALF_MASK = 0xFFFF0000


def _pack_halves(x):
    c = x.shape[1] // 2
    lo = lax.bitcast_convert_type(x[:, :c].astype(BF16).astype(F32), jnp.uint32) >> jnp.uint32(16)
    hi = lax.bitcast_convert_type(x[:, c:].astype(BF16).astype(F32), jnp.uint32) & jnp.uint32(HALF_MASK)
    return lax.bitcast_convert_type(lo | hi, jnp.int32)


def _unpack_halves(words):
    w = lax.bitcast_convert_type(words, jnp.uint32)
    lo = lax.bitcast_convert_type(w << jnp.uint32(16), F32)
    hi = lax.bitcast_convert_type(w & jnp.uint32(HALF_MASK), F32)
    return lo, hi


def _outproj_kernel(x_ref, lru_ref, att_ref, mod_ref, gatt_ref, wtop_ref, wbot_ref, gpost_ref, gpre_ref,
                    rwh_ref, rwl_ref, rb_ref, cnt_in_ref, before_ref,
                    x1_ref, hf_ref, xw_ref, ids_ref, ranks_ref, gates_ref, cnt_out_ref, carry_ref, *, split):
    nb, tt, d = x_ref.shape
    m = nb * tt
    first = (pl.program_id(0) == 0) & (pl.program_id(1) == 0)

    @pl.when(first)
    def _():
        carry_ref[...] = cnt_in_ref[...]

    if split is not None:
        @pl.when((pl.program_id(0) == split) & (pl.program_id(1) == 0))
        def _():
            carry_ref[...] = jnp.zeros_like(carry_ref)

    mod = mod_ref[...]
    att_n = _rms(att_ref[...], gatt_ref[...]).reshape(m, WA).astype(BF16)
    mix = _dot(lru_ref[...].reshape(m, WL), wtop_ref[...]) + _dot(att_n, wbot_ref[...])
    x1 = x_ref[...] + mod[:, 2:3, :] * _rms(mix, gpost_ref[...]).reshape(nb, tt, d)
    x1_ref[...] = x1
    hf = (_rms(x1, gpre_ref[...]) * (1.0 + mod[:, 4:5, :]) + mod[:, 3:4, :]).reshape(m, d)
    hf_hi = hf.astype(BF16)
    hf_ref[...] = hf_hi.reshape(nb, tt, d)
    hf_lo = (hf - hf_hi.astype(F32)).astype(BF16)
    rwh = rwh_ref[...]
    logits = _dot_nt(rwh, hf_hi) + _dot_nt(rwh, hf_lo) + _dot_nt(rwl_ref[...], hf_hi)
    picks, gates = _route(_sigmoid(logits), rb_ref[...])
    xw_ref[...] = _pack_halves(hf).reshape(nb, tt, d // 2)

    sel = jnp.zeros((N_EXPERTS, m), F32)
    for pk in picks:
        sel = sel + pk.astype(F32)
    prior = _dot(sel.astype(BF16), before_ref[...]) + carry_ref[...]
    expert = lax.broadcasted_iota(jnp.int32, (N_EXPERTS, m), 0).astype(F32)
    take = lambda pk, v: jnp.sum(jnp.where(pk, v, 0.0), axis=0, keepdims=True)
    ids_ref[...] = jnp.concatenate([take(pk, expert) for pk in picks], axis=0).astype(jnp.int32)
    ranks_ref[...] = jnp.concatenate([take(pk, prior) for pk in picks], axis=0).astype(jnp.int32)
    carry_ref[...] += jnp.sum(sel, axis=1, keepdims=True)
    cnt_out_ref[0] = carry_ref[...]
    gates = jnp.concatenate([gates, jnp.zeros((LANES - TOP_K, m), F32)], axis=0)
    gates_ref[...] = gates.T.reshape(nb, tt, LANES)


def _outproj(x, lru_n, att, mod, g_att, w_top, w_bot, g_post, g_pre, rw_hi, rw_lo, r_bias, cnt_in, split=None):
    n, t, d = x.shape
    nb, tt = _seq_blocks(n, t)
    m = nb * tt
    steps_t = t // tt
    n_groups = 1 if split is None else 2
    blk = lambda w: pl.BlockSpec((nb, tt, w), lambda i, j: (i, j, 0))
    const = lambda shape: pl.BlockSpec(shape, lambda i, j: (0,) * len(shape))
    per_tok = pl.BlockSpec((TOP_K, m), lambda i, j: (0, i * steps_t + j))
    group_of = (lambda i: 0) if split is None else (lambda i: jnp.where(i < split, 0, 1))
    before = jnp.asarray(np.triu(np.ones((m, m), np.float32), k=1), BF16)
    return pl.pallas_call(
        functools.partial(_outproj_kernel, split=split),
        grid=(n // nb, steps_t),
        in_specs=[blk(d), blk(WL), blk(WA), pl.BlockSpec((nb, 6, d), lambda i, j: (i, 0, 0)),
                  const((1, WA)), const((WL, d)), const((WA, d)), const((1, d)), const((1, d)),
                  const((N_EXPERTS, d)), const((N_EXPERTS, d)), const((N_EXPERTS, 1)), const((N_EXPERTS, 1)),
                  const((m, m))],
        out_specs=[blk(d), blk(d), blk(d // 2), per_tok, per_tok, blk(LANES),
                   pl.BlockSpec((1, N_EXPERTS, 1), lambda i, j: (group_of(i), 0, 0))],
        out_shape=[jax.ShapeDtypeStruct((n, t, d), F32), jax.ShapeDtypeStruct((n, t, d), BF16),
                   jax.ShapeDtypeStruct((n, t, d // 2), jnp.int32),
                   jax.ShapeDtypeStruct((TOP_K, n * t), jnp.int32), jax.ShapeDtypeStruct((TOP_K, n * t), jnp.int32),
                   jax.ShapeDtypeStruct((n, t, LANES), F32), jax.ShapeDtypeStruct((n_groups, N_EXPERTS, 1), F32)],
        scratch_shapes=[pltpu.VMEM((N_EXPERTS, 1), F32)],
        compiler_params=_params("arbitrary", "arbitrary"),
        name="outproj_router",
    )(x, lru_n, att, mod, g_att, w_top, w_bot, g_post, g_pre, rw_hi, rw_lo, r_bias, cnt_in, before)


def _subcore_ranges(n_items):
    info = plsc.get_sparse_core_info()
    n_workers = info.num_cores * info.num_subcores
    per_worker = n_items // n_workers
    assert per_worker * n_workers == n_items and per_worker % GATHER_ROWS == 0
    return info, plsc.VectorSubcoreMesh(core_axis_name="c", subcore_axis_name="s"), per_worker


def _scatter_rows(sources, pos, n_slots):
    m_tot, c = pos.shape[1], sources[0][0].shape[1]
    assert sum(cnt for _, _, cnt in sources) == m_tot
    assert all(first % GATHER_ROWS == 0 and cnt % GATHER_ROWS == 0 for _, first, cnt in sources)
    info, mesh, per_worker = _subcore_ranges(m_tot)
    pos_flat = pos.reshape(-1)

    @functools.partial(
        pl.kernel, mesh=mesh, out_type=jax.ShapeDtypeStruct((n_slots, c), jnp.int32),
        scratch_types=[pltpu.VMEM((GATHER_ROWS,), jnp.int32), pltpu.VMEM((GATHER_ROWS, c), jnp.int32),
                       pltpu.SemaphoreType.DMA])
    def scatter(*refs):
        src_refs, (pos_hbm, out_hbm, idx_v, rows_v, sem) = refs[:len(sources)], refs[len(sources):]
        worker = lax.axis_index("s") * info.num_cores + lax.axis_index("c")
        base = worker * per_worker

        @pl.loop(0, per_worker // GATHER_ROWS)
        def _(step):
            off = pl.multiple_of(base + step * GATHER_ROWS, GATHER_ROWS)
            token0 = 0
            for src_hbm, (_, first, cnt) in zip(src_refs, sources):
                @pl.when((off >= token0) & (off < token0 + cnt))
                def _(src_hbm=src_hbm, shift=first - token0):
                    pltpu.sync_copy(src_hbm.at[pl.ds(pl.multiple_of(off + shift, GATHER_ROWS), GATHER_ROWS)], rows_v)
                token0 += cnt
            for r in range(TOP_K):
                pltpu.sync_copy(pos_hbm.at[pl.ds(pl.multiple_of(r * m_tot + off, GATHER_ROWS), GATHER_ROWS)], idx_v)
                pltpu.async_copy(rows_v, out_hbm.at[idx_v], sem).wait()

    return scatter(*[a for a, _, _ in sources], pos_flat)


def _swiglu_halves(lo, hi, wg, wu, wd):
    c = lo.shape[1]
    hg = _dot(lo, wg[:c]) + _dot(hi, wg[c:])
    hu = _dot(lo, wu[:c]) + _dot(hi, wu[c:])
    return _dot((_silu(hg) * hu).astype(BF16), wd)


def _expert_kernel(te_ref, valid_ref, x_ref, wg_ref, wu_ref, wd_ref, y_ref, wg_bf, wu_bf, wd_bf):
    i = pl.program_id(0)
    valid = valid_ref[i]

    @pl.when((i == 0) | (te_ref[i] != te_ref[jnp.maximum(i - 1, 0)]))
    def _():
        wg_bf[...] = wg_ref[0].astype(BF16)
        wu_bf[...] = wu_ref[0].astype(BF16)
        wd_bf[...] = wd_ref[0].astype(BF16)

    @pl.when(valid > 0)
    def _():
        w = x_ref[...]
        row = lax.broadcasted_iota(jnp.int32, w.shape, 0)
        lo, hi = _unpack_halves(jnp.where(row < valid, w, 0))
        y_ref[...] = _pack_halves(_swiglu_halves(lo.astype(BF16), hi.astype(BF16), wg_bf[...], wu_bf[...], wd_bf[...]))


def _experts(xs, tile_expert, tile_valid, wg, wu, wd):
    n_slots, c = xs.shape
    n_tiles = n_slots // EXPERT_TILE
    d = 2 * c
    rows = pl.BlockSpec((EXPERT_TILE, c), lambda i, te, tv: (i, 0))
    weight = lambda shape: pl.BlockSpec((1,) + shape, lambda i, te, tv: (te[i], 0, 0))
    return pl.pallas_call(
        _expert_kernel,
        grid_spec=pltpu.PrefetchScalarGridSpec(
            num_scalar_prefetch=2, grid=(n_tiles,),
            in_specs=[rows, weight((d, D_EXPERT)), weight((d, D_EXPERT)), weight((D_EXPERT, d))],
            out_specs=rows,
            scratch_shapes=[pltpu.VMEM((d, D_EXPERT), BF16), pltpu.VMEM((d, D_EXPERT), BF16),
                            pltpu.VMEM((D_EXPERT, d), BF16)]),
        out_shape=jax.ShapeDtypeStruct((n_slots, c), jnp.int32),
        compiler_params=_params("arbitrary"),
        name="moe_experts",
    )(tile_expert, tile_valid, xs, wg, wu, wd)


def _gather_rows(table, idx):
    b, c = idx.shape[0], table.shape[1]
    info, mesh, per_worker = _subcore_ranges(b)

    n_steps = per_worker // GATHER_ROWS
    assert n_steps % 2 == 0

    @functools.partial(
        pl.kernel, mesh=mesh, out_type=jax.ShapeDtypeStruct((b, c), jnp.int32),
        scratch_types=[pltpu.VMEM((2, GATHER_ROWS), jnp.int32), pltpu.VMEM((2, GATHER_ROWS, c), jnp.int32),
                       pltpu.SemaphoreType.DMA((2,))])
    def gather(table_hbm, idx_hbm, out_hbm, idx_v, rows_v, sems):
        worker = lax.axis_index("s") * info.num_cores + lax.axis_index("c")
        base = worker * per_worker

        def chunk(step):
            return pl.ds(pl.multiple_of(base + step * GATHER_ROWS, GATHER_ROWS), GATHER_ROWS)

        def stream(buf):
            return pltpu.make_async_copy(table_hbm.at[idx_v.at[buf]], rows_v.at[buf], sems.at[buf])

        def start(step, buf):
            pltpu.sync_copy(idx_hbm.at[chunk(step)], idx_v.at[buf])
            stream(buf).start()

        def finish(step, buf):
            stream(buf).wait()
            pltpu.sync_copy(rows_v.at[buf], out_hbm.at[chunk(step)])

        start(0, 0)

        @pl.loop(0, n_steps, step=2)
        def _(step):
            start(step + 1, 1)
            finish(step, 0)

            @pl.when(step + 2 < n_steps)
            def _():
                start(step + 2, 0)

            finish(step + 1, 1)

    return gather(table, idx)


def _combine_kernel(rows_ref, gates_ref, hf_ref, x1_ref, mod_ref, sg_ref, su_ref, sd_ref, gpost_ref, y_ref):
    nb, tt, d = hf_ref.shape
    m = nb * tt
    c = d // 2
    x = hf_ref[...].reshape(m, d)
    shared = _swiglu_halves(x[:, :c], x[:, c:], sg_ref[...], su_ref[...], sd_ref[...])
    gates = gates_ref[...].reshape(m, LANES)
    acc_lo = shared[:, :c]
    acc_hi = shared[:, c:]
    for r in range(TOP_K):
        lo, hi = _unpack_halves(rows_ref[r])
        g = gates[:, r:r + 1]
        acc_lo = acc_lo + g * lo
        acc_hi = acc_hi + g * hi
    z = _rms(jnp.concatenate([acc_lo, acc_hi], axis=1), gpost_ref[...]).reshape(nb, tt, d)
    y_ref[...] = x1_ref[...] + mod_ref[...][:, 5:6, :] * z


def _combine(rows, first_tile, gates_t, hf, x1, mod, sg, su, sd, g_post, blocks=None, y_prev=None):
    n, t, d = hf.shape
    nb, tt = _seq_blocks(n, t)
    m = nb * tt
    steps_t = t // tt
    b0, nblocks = (0, n // nb) if blocks is None else blocks
    blk = lambda w: pl.BlockSpec((nb, tt, w), lambda i, j: (b0 + i, j, 0))
    const = lambda shape: pl.BlockSpec(shape, lambda i, j: (0,) * len(shape))
    in_specs = [pl.BlockSpec((TOP_K, m, d // 2), lambda i, j: (0, first_tile + i * steps_t + j, 0)),
                blk(LANES), blk(d), blk(d), pl.BlockSpec((nb, 6, d), lambda i, j: (b0 + i, 0, 0)),
                const((d, D_EXPERT)), const((d, D_EXPERT)), const((D_EXPERT, d)), const((1, d))]
    args = (rows, gates_t, hf, x1, mod, sg, su, sd, g_post)
    kernel_fn, aliases = _combine_kernel, {}
    if y_prev is not None:
        in_specs.append(pl.BlockSpec(memory_space=pl.ANY))
        args += (y_prev,)
        aliases = {len(args) - 1: 0}
        kernel_fn = lambda *refs: _combine_kernel(*refs[:len(args) - 1], refs[-1])
    return pl.pallas_call(
        kernel_fn,
        grid=(nblocks, steps_t),
        in_specs=in_specs,
        out_specs=blk(d),
        out_shape=jax.ShapeDtypeStruct((n, t, d), F32),
        input_output_aliases=aliases,
        compiler_params=_params("parallel", "parallel"),
        name="moe_combine",
    )(*args)


def _slots_kernel(starts_ref, ids_ref, ranks_ref, pos_ref):
    ids = ids_ref[...]

    def add_start(e, pos):
        return pos + jnp.where(ids == e, starts_ref[e], 0)

    pos_ref[...] = lax.fori_loop(0, N_EXPERTS, add_start, ranks_ref[...])


def _slots(starts, ids, ranks):
    k, m_tot = ids.shape
    cols = math.gcd(m_tot, SLOT_COLS)
    assert cols % LANES == 0
    blk = pl.BlockSpec((k, cols), lambda i: (0, i))
    return pl.pallas_call(
        _slots_kernel,
        grid=(m_tot // cols,),
        in_specs=[pl.BlockSpec(memory_space=pltpu.SMEM), blk, blk],
        out_specs=blk,
        out_shape=jax.ShapeDtypeStruct((k, m_tot), jnp.int32),
        compiler_params=_params("arbitrary"),
        name="moe_slots",
    )(starts, ids, ranks)


def _slot_plan(ids, ranks, counts):
    n_pairs = ids.shape[0] * ids.shape[1]
    n_tiles = -(-(n_pairs + N_EXPERTS * (EXPERT_TILE - 1)) // EXPERT_TILE)
    cnt = counts.reshape(N_EXPERTS).astype(jnp.int32)
    padded = (cnt + EXPERT_TILE - 1) // EXPERT_TILE * EXPERT_TILE
    ends = jnp.cumsum(padded)
    starts = ends - padded
    pos = _slots(starts, ids, ranks)
    tile_start = jnp.arange(n_tiles, dtype=jnp.int32) * EXPERT_TILE
    in_expert = (tile_start[:, None] >= starts[None, :]) & (tile_start[:, None] < ends[None, :])
    tile_expert = jnp.sum(jnp.where(in_expert, jnp.arange(N_EXPERTS, dtype=jnp.int32)[None, :], 0), axis=1)
    tile_fill = jnp.sum(jnp.where(in_expert, (starts + cnt)[None, :] - tile_start[:, None], 0), axis=1)
    tile_valid = jnp.clip(tile_fill, 0, EXPERT_TILE).astype(jnp.int32)
    return pos, tile_expert.astype(jnp.int32), tile_valid, n_tiles * EXPERT_TILE


def _block_diag(w):
    g, bw, _ = w.shape
    eye = jnp.eye(g, dtype=w.dtype)
    return (eye[:, None, :, None] * w[:, :, None, :]).reshape(g * bw, g * bw)


def _prep_weights(p):
    d_main = 2 * WL + 3 * WA
    w_in = p["w_in"]
    rw_t = p["router_w"].T
    rw_hi = rw_t.astype(BF16)
    row = lambda v: v.reshape(1, -1)
    return dict(
        w_mod=p["w_mod"], b_mod=p["b_mod"],
        g_pre_mix=row(p["g_pre_mix"]), g_post_mix=row(p["g_post_mix"]),
        g_pre_ffn=row(p["g_pre_ffn"]), g_post_ffn=row(p["g_post_ffn"]),
        w_main=w_in[:, :d_main].astype(BF16),
        w_f=jnp.pad(w_in[:, d_main:], ((0, 0), (0, LANES - N_HEADS))).astype(BF16),
        b_f=jnp.pad(p["b_f"], (0, LANES - N_HEADS)).reshape(1, LANES),
        conv_w=p["conv_w"], conv_b=row(p["conv_b"]),
        wr_bd=_block_diag(p["w_r"]).astype(BF16), b_r=row(p["b_r"]),
        wi_bd=_block_diag(p["w_i"]).astype(BF16), b_i=row(p["b_i"]),
        lam=row(p["lru_lambda"]), g_lru=row(p["g_lru_out"]), g_att=row(p["g_att_out"]),
        w_top=p["w_out"][:WL].astype(BF16), w_bot=p["w_out"][WL:].astype(BF16),
        rw_hi=rw_hi, rw_lo=(rw_t - rw_hi.astype(F32)).astype(BF16),
        r_bias=p["router_bias"].reshape(N_EXPERTS, 1),
        wg=p["w_gate"], wu=p["w_up"], wd=p["w_down"],
        sg=p["ws_gate"].astype(BF16), su=p["ws_up"].astype(BF16), sd=p["ws_down"].astype(BF16),
    )


def _mixers(x, mod, conv0, h0, past, w, cnt_in, split=None):
    n, t, _ = x.shape
    proj_args = (x, mod, w["g_pre_mix"], w["w_main"], w["w_f"], w["b_f"])
    if past is None:
        xl, gy, k, v, lf, qt, ka, vt, bpre = _inproj_prompt(*proj_args)
        att = _att_prompt(qt, ka, vt, bpre)
    else:
        xl, gy, qb, kb, vb, k, v, lf = _inproj(*proj_args)
        k_past, v_past, lf_past = past
        plen = k_past.shape[1]
        by_head = lambda a: jnp.transpose(a, (0, 2, 1)).reshape(n * N_HEADS, a.shape[1])
        d_new = _cumsum_rows(by_head(lf), t).reshape(n, N_HEADS, t)
        d_past = _cumsum_rows(by_head(lf_past), min(CUMSUM_COLS, plen)).reshape(n, N_HEADS, plen)
        att = _att_sample(qb, k_past, v_past, d_past, kb, vb, d_new)
    lru_n, conv_new, h_new = _lru(xl, gy, conv0, h0.reshape(n, 1, WL), w["conv_w"], w["conv_b"],
                                  w["wr_bd"], w["b_r"], w["wi_bd"], w["b_i"], w["lam"], w["g_lru"])
    routed = _outproj(x, lru_n, att, mod, w["g_att"], w["w_top"], w["w_bot"], w["g_post_mix"],
                      w["g_pre_ffn"], w["rw_hi"], w["rw_lo"], w["r_bias"], cnt_in, split)
    state = (k.reshape(n, t, N_HEADS, HEAD_DIM), v.reshape(n, t, N_HEADS, HEAD_DIM), lf,
             conv_new, h_new.reshape(n, WL))
    return routed, state


def _layer(xp, xs, mod_p, mod_s, conv_s, h_s, past_s, w):
    n_p = xp.shape[0]
    conv0 = jnp.zeros((n_p, CONV_W - 1, WL), F32)
    h0 = jnp.zeros((n_p, WL), F32)
    zero_cnt = jnp.zeros((N_EXPERTS, 1), F32)
    t_p = xp.shape[1]
    split = 1 if n_p > 1 else None
    (x1_p, hf_p, xw_p, ids_p, rk_p, g_p, cnt_p), st_p = _mixers(xp, mod_p, conv0, h0, None, w, zero_cnt, split)
    cnt_tail = cnt_p[1] if n_p > 1 else zero_cnt
    (x1_s, hf_s, xw_s, ids_s, rk_s, g_s, cnt_s), st_s = _mixers(xs, mod_s, conv_s, h_s, past_s, w, cnt_tail)

    half = xw_p.shape[-1]
    xw_p = xw_p.reshape(-1, half)

    def routed(sources, ids, ranks, counts):
        pos, tile_expert, tile_valid, n_slots = _slot_plan(ids, ranks, counts)
        ys = _experts(_scatter_rows(sources, pos, n_slots), tile_expert, tile_valid, w["wg"], w["wu"], w["wd"])
        return _gather_rows(ys, pos.reshape(-1)).reshape(TOP_K, ids.shape[1], half)

    rows_a = routed([(xw_p, 0, t_p)], ids_p[:, :t_p], rk_p[:, :t_p], cnt_p[0])
    xw_s = xw_s.reshape(-1, half)
    sources_b = ([(xw_p, t_p, (n_p - 1) * t_p)] if n_p > 1 else []) + [(xw_s, 0, xw_s.shape[0])]
    rows_b = routed(sources_b, jnp.concatenate([ids_p[:, t_p:], ids_s], axis=1),
                    jnp.concatenate([rk_p[:, t_p:], rk_s], axis=1), cnt_s[0])
    shared = (w["sg"], w["su"], w["sd"], w["g_post_ffn"])
    yp = _combine(rows_a, 0, g_p, hf_p, x1_p, mod_p, *shared, blocks=(0, 1))
    if n_p > 1:
        yp = _combine(rows_b, 0, g_p, hf_p, x1_p, mod_p, *shared, blocks=(1, n_p - 1), y_prev=yp)
    ysmp = _combine(rows_b, (n_p - 1) * t_p // ROW_TILE, g_s, hf_s, x1_s, mod_s, *shared)
    return yp, ysmp, st_p, st_s


def kernel(x_prompt, x_sample, c_prompt, c_sample, cache_k, cache_v, cache_logf, state_conv, state_lru, w_mod, b_mod, g_pre_mix, g_post_mix, g_pre_ffn, g_post_ffn, w_in, conv_w, conv_b, w_r, b_r, w_i, b_i, lru_lambda, b_f, g_lru_out, g_att_out, w_out, router_w, router_bias, w_gate, w_up, w_down, ws_gate, ws_up, ws_down):
    names = ("w_mod", "b_mod", "g_pre_mix", "g_post_mix", "g_pre_ffn", "g_post_ffn", "w_in", "conv_w", "conv_b",
             "w_r", "b_r", "w_i", "b_i", "lru_lambda", "b_f", "g_lru_out", "g_att_out", "w_out", "router_w",
             "router_bias", "w_gate", "w_up", "w_down", "ws_gate", "ws_up", "ws_down")
    stacked = (w_mod, b_mod, g_pre_mix, g_post_mix, g_pre_ffn, g_post_ffn, w_in, conv_w, conv_b, w_r, b_r, w_i, b_i,
               lru_lambda, b_f, g_lru_out, g_att_out, w_out, router_w, router_bias, w_gate, w_up, w_down,
               ws_gate, ws_up, ws_down)
    depth = w_mod.shape[0]
    n_p, n_s = x_prompt.shape[0], x_sample.shape[0]
    yp, ys = x_prompt, x_sample
    st_p, st_s = [], []
    for l in range(depth):
        w = _prep_weights({k: v[l] for k, v in zip(names, stacked)})
        mod = _modulation(jnp.concatenate([c_prompt, c_sample], axis=0), w["w_mod"], w["b_mod"])
        mod = mod.reshape(n_p + n_s, 6, D_MODEL)
        yp, ys, sp, ss = _layer(yp, ys, mod[:n_p], mod[n_p:], state_conv[l], state_lru[l],
                                (cache_k[l], cache_v[l], cache_logf[l]), w)
        st_p.append(sp)
        st_s.append(ss)
    stack = lambda sts, i: jnp.stack([s[i] for s in sts])
    return (yp, ys) + tuple(stack(st_p, i) for i in range(5)) + tuple(stack(st_s, i) for i in range(5))
```

```python
import functools
import math

import jax
import jax.numpy as jnp
import numpy as np
from jax import lax
from jax.experimental import pallas as pl
from jax.experimental.pallas import tpu as pltpu
from jax.experimental.pallas import tpu_sc as plsc

F32 = jnp.float32
BF16 = jnp.bfloat16

D_MODEL = 1024
WL = 512
WA = 512
N_HEADS = 8
HEAD_DIM = 64
N_PAIRS = N_HEADS // 2
PAIR_W = 2 * HEAD_DIM
LANES = 128
CONV_W = 4
LRU_BLOCKS = 8
LRU_C = 8.0
N_EXPERTS = 64
N_GROUPS = 8
GROUP_SIZE = N_EXPERTS // N_GROUPS
TOPK_GROUPS = 4
TOP_K = 8
D_EXPERT = 256
ROUTED_SCALE = 2.5
EPS = 1e-6
NEG_INF = float("-inf")
LOG2E = 1.4426950408889634

ROW_TILE = 512
ATT_TQ = 512
ATT_TK = 512
ATT_CHUNK = 64
ATT_HEADS = 4
SAMPLE_TK = 4096
CUMSUM_COLS = 1024
EXPERT_TILE = 1024
SLOT_COLS = 2048
GATHER_ROWS = 64
VMEM_LIMIT = 56 * 1024 * 1024


def _params(*sem):
    return pltpu.CompilerParams(dimension_semantics=sem, vmem_limit_bytes=VMEM_LIMIT)


def _dot(a, b):
    return jnp.dot(a, b, preferred_element_type=F32)


def _dot_nt(a, b):
    return lax.dot_general(a, b, (((1,), (1,)), ((), ())), preferred_element_type=F32)


def _split3(x):
    hi = x.astype(BF16)
    r1 = x - hi.astype(F32)
    mid = r1.astype(BF16)
    lo = (r1 - mid.astype(F32)).astype(BF16)
    return hi, mid, lo


def _rms(x, g):
    return x * lax.rsqrt(jnp.mean(x * x, axis=-1, keepdims=True) + EPS) * g


def _sigmoid(x):
    return 1.0 / (1.0 + jnp.exp(-x))


def _silu(x):
    return x * _sigmoid(x)


def _gelu_tanh(x):
    return 0.5 * x * (1.0 + jnp.tanh(0.7978845608028654 * (x + 0.044715 * (x * x * x))))


def _log_sigmoid(x):
    return jnp.minimum(x, 0.0) - jnp.log1p(jnp.exp(-jnp.abs(x)))


def _seq_blocks(n, t):
    if t >= ROW_TILE:
        assert t % ROW_TILE == 0
        return 1, ROW_TILE
    nb = ROW_TILE // t
    assert nb * t == ROW_TILE and n % nb == 0
    return nb, t


def _mod_kernel(c_ref, w_ref, b_ref, o_ref):
    c = _silu(c_ref[...])
    c_hi = c.astype(BF16)
    c_lo = (c - c_hi.astype(F32)).astype(BF16)
    w = w_ref[...]
    w_hi = w.astype(BF16)
    w_lo = (w - w_hi.astype(F32)).astype(BF16)
    o_ref[...] = _dot(c_hi, w_hi) + _dot(c_lo, w_hi) + _dot(c_hi, w_lo) + b_ref[...]


def _modulation(c, w_mod, b_mod):
    rows = c.shape[0]
    n = -(-rows // 8) * 8
    c = jnp.pad(c, ((0, n - rows), (0, 0)))
    d6 = w_mod.shape[1]
    return pl.pallas_call(
        _mod_kernel,
        grid=(d6 // D_MODEL,),
        in_specs=[pl.BlockSpec((n, D_MODEL), lambda j: (0, 0)),
                  pl.BlockSpec((D_MODEL, D_MODEL), lambda j: (0, j)),
                  pl.BlockSpec((1, D_MODEL), lambda j: (0, j))],
        out_specs=pl.BlockSpec((n, D_MODEL), lambda j: (0, j)),
        out_shape=jax.ShapeDtypeStruct((n, d6), F32),
        compiler_params=_params("arbitrary"),
        name="modulation",
    )(c, w_mod, b_mod.reshape(1, d6))[:rows]


def _inproj_kernel(x_ref, mod_ref, g_ref, w_ref, wf_ref, bf_ref,
                   xl_ref, gy_ref, qb_ref, kb_ref, vb_ref, k_ref, v_ref, lf_ref):
    nb, tt, d = x_ref.shape
    x = x_ref[...]
    mod = mod_ref[...]
    hn = _rms(x, g_ref[...]) * (1.0 + mod[:, 1:2, :]) + mod[:, 0:1, :]
    hb = hn.reshape(nb * tt, d).astype(BF16)

    def proj(col):
        return _dot(hb, w_ref[:, col * WL:(col + 1) * WL]).reshape(nb, tt, WL)

    xl_ref[...] = proj(0)
    gy_ref[...] = _gelu_tanh(proj(1)).astype(BF16)
    qb_ref[...] = (proj(2) * (HEAD_DIM ** -0.5)).astype(BF16)
    k = proj(3)
    k_ref[...] = k
    kb_ref[...] = k.astype(BF16)
    v = proj(4)
    v_ref[...] = v
    vb_ref[...] = v.astype(BF16)
    fl = _dot(hb, wf_ref[...]) + bf_ref[...]
    lf_ref[...] = _log_sigmoid(fl).reshape(nb, tt, LANES)[:, :, :N_HEADS]


def _inproj(x, mod, g_pre, w_main, w_f, b_f):
    n, t, d = x.shape
    nb, tt = _seq_blocks(n, t)
    blk = lambda w: pl.BlockSpec((nb, tt, w), lambda i, j: (i, j, 0))
    const = lambda shape: pl.BlockSpec(shape, lambda i, j: (0,) * len(shape))
    f32 = lambda w: jax.ShapeDtypeStruct((n, t, w), F32)
    b16 = lambda w: jax.ShapeDtypeStruct((n, t, w), BF16)
    return pl.pallas_call(
        _inproj_kernel,
        grid=(n // nb, t // tt),
        in_specs=[blk(d),
                  pl.BlockSpec((nb, 6, d), lambda i, j: (i, 0, 0)),
                  const((1, d)), const(w_main.shape), const(w_f.shape), const((1, LANES))],
        out_specs=[blk(WL), blk(WL), blk(WA), blk(WA), blk(WA), blk(WA), blk(WA), blk(N_HEADS)],
        out_shape=[f32(WL), b16(WL), b16(WA), b16(WA), b16(WA), f32(WA), f32(WA), f32(N_HEADS)],
        compiler_params=_params("parallel", "arbitrary"),
        name="inproj",
    )(x, mod, g_pre, w_main, w_f, b_f)


def _aug_lane(h):
    return HEAD_DIM if h % 2 == 0 else 0


def _inproj_prompt_kernel(x_ref, mod_ref, g_ref, w_ref, wf_ref, bf_ref, place_ref,
                          xl_ref, gy_ref, k_ref, v_ref, lf_ref, qt_ref, ka_ref, vt_ref, bpre_ref, carry_ref):
    _, tt, d = x_ref.shape

    @pl.when(pl.program_id(1) == 0)
    def _():
        carry_ref[...] = jnp.zeros_like(carry_ref)

    mod = mod_ref[0]
    hb = (_rms(x_ref[0], g_ref[...]) * (1.0 + mod[1:2, :]) + mod[0:1, :]).astype(BF16)

    def proj(col):
        return _dot(hb, w_ref[:, col * WL:(col + 1) * WL])

    xl_ref[0] = proj(0)
    gy_ref[0] = _gelu_tanh(proj(1)).astype(BF16)
    q = proj(2) * (HEAD_DIM ** -0.5 * LOG2E)
    k = proj(3)
    k_ref[0] = k
    v = proj(4)
    v_ref[0] = v
    fl = _dot(hb, wf_ref[...]) + bf_ref[...]
    lane = lax.broadcasted_iota(jnp.int32, (tt, LANES), 1)
    lf = jnp.where(lane < N_HEADS, _log_sigmoid(fl), 0.0)
    lf_ref[0] = lf[:, :N_HEADS]

    row = lax.broadcasted_iota(jnp.int32, (tt, tt), 0)
    col = lax.broadcasted_iota(jnp.int32, (tt, tt), 1)
    tril = (col <= row).astype(BF16)
    hi, mid, lo = _split3(lf)
    e = _dot(tril, hi) + _dot(tril, mid) + _dot(tril, lo)
    bpre_ref[0, 0] = carry_ref[...]
    carry_ref[...] += e[tt - 1:tt, :]
    e_hi, e_mid, e_lo = _split3(e * (-LOG2E))
    aug_k = _dot(jnp.concatenate([e_hi, e_mid, e_lo], axis=1), place_ref[...])

    for h in range(N_HEADS):
        pair = slice((h // 2) * PAIR_W, (h // 2 + 1) * PAIR_W)
        dims = (lane < HEAD_DIM) if h % 2 == 0 else (lane >= HEAD_DIM)
        a0 = _aug_lane(h)
        ones3 = ((lane >= a0) & (lane < a0 + 3)).astype(F32)
        qt_ref[0, h] = (jnp.where(dims, q[:, pair], 0.0) + ones3).T.astype(BF16)
        ka_ref[0, h] = (jnp.where(dims, k[:, pair], 0.0) + aug_k[:, h * LANES:(h + 1) * LANES]).astype(BF16)
        one1 = (lane == a0).astype(F32)
        vt_ref[0, h] = (jnp.where(dims, v[:, pair], 0.0) + one1).T.astype(BF16)


def _placement():
    pl_mat = np.zeros((3 * LANES, N_HEADS * LANES), np.float32)
    for p in range(3):
        for h in range(N_HEADS):
            pl_mat[p * LANES + h, h * LANES + _aug_lane(h) + p] = 1.0
    return jnp.asarray(pl_mat, BF16)


def _inproj_prompt(x, mod, g_pre, w_main, w_f, b_f):
    n, t, d = x.shape
    tt = ROW_TILE
    assert t % tt == 0
    nblk = t // tt
    blk = lambda w: pl.BlockSpec((1, tt, w), lambda i, j: (i, j, 0))
    const = lambda shape: pl.BlockSpec(shape, lambda i, j: (0,) * len(shape))
    f32 = lambda w: jax.ShapeDtypeStruct((n, t, w), F32)
    place = _placement()
    return pl.pallas_call(
        _inproj_prompt_kernel,
        grid=(n, nblk),
        in_specs=[blk(d), pl.BlockSpec((1, 6, d), lambda i, j: (i, 0, 0)),
                  const((1, d)), const(w_main.shape), const(w_f.shape), const((1, LANES)), const(place.shape)],
        out_specs=[blk(WL), blk(WL), blk(WA), blk(WA), blk(N_HEADS),
                   pl.BlockSpec((1, N_HEADS, LANES, tt), lambda i, j: (i, 0, 0, j)),
                   pl.BlockSpec((1, N_HEADS, tt, LANES), lambda i, j: (i, 0, j, 0)),
                   pl.BlockSpec((1, N_HEADS, LANES, tt), lambda i, j: (i, 0, 0, j)),
                   pl.BlockSpec((1, 1, 1, LANES), lambda i, j: (i, j, 0, 0))],
        out_shape=[f32(WL), jax.ShapeDtypeStruct((n, t, WL), BF16), f32(WA), f32(WA), f32(N_HEADS),
                   jax.ShapeDtypeStruct((n, N_HEADS, LANES, t), BF16),
                   jax.ShapeDtypeStruct((n, N_HEADS, t, LANES), BF16),
                   jax.ShapeDtypeStruct((n, N_HEADS, LANES, t), BF16),
                   jax.ShapeDtypeStruct((n, nblk, 1, LANES), F32)],
        scratch_shapes=[pltpu.VMEM((1, LANES), F32)],
        compiler_params=_params("parallel", "arbitrary"),
        name="inproj_prompt",
    )(x, mod, g_pre, w_main, w_f, b_f, place)


def _expm1_neg(x):
    poly = x * (1.0 + x * (0.5 + x * (1.0 / 6.0 + x * (1.0 / 24.0 + x * (1.0 / 120.0)))))
    return jnp.where(x > -0.1, poly, jnp.exp(x) - 1.0)


def _lru_kernel(xl_ref, gy_ref, conv0_ref, h0_ref, cw_ref, cb_ref, wr_ref, br_ref, wi_ref, bi_ref,
                lam_ref, g_ref, out_ref, conv_ref, hlast_ref, tail_ref, carry_ref):
    nb, tt, w = xl_ref.shape
    j = pl.program_id(1)

    @pl.when(j == 0)
    def _():
        tail_ref[:, 8 - (CONV_W - 1):, :] = conv0_ref[...]
        carry_ref[...] = h0_ref[...]

    xl = xl_ref[...]
    xpad = jnp.concatenate([tail_ref[...], xl], axis=1)
    cw = cw_ref[...]
    xc = jnp.zeros_like(xl) + cb_ref[...]
    for k in range(CONV_W):
        off = 8 - (CONV_W - 1) + k
        xc = xc + xpad[:, off:off + tt, :] * cw[k:k + 1, :]
    conv_ref[...] = xpad[:, tt + 8 - (CONV_W - 1):, :]
    tail_ref[...] = xpad[:, tt:, :]

    m = nb * tt
    xf = xc.reshape(m, w)
    xb = xf.astype(BF16)
    r = _sigmoid(_dot(xb, wr_ref[...]) + br_ref[...])
    gi = _sigmoid(_dot(xb, wi_ref[...]) + bi_ref[...])
    lam = lam_ref[...]
    softplus = jnp.maximum(-lam, 0.0) + jnp.log1p(jnp.exp(-jnp.abs(lam)))
    log_a = (-LRU_C) * r * softplus
    a = jnp.exp(log_a)
    b = jnp.sqrt(-_expm1_neg(2.0 * log_a)) * (gi * xf)

    groups = m // 8
    a = a.reshape(groups, 8, w)
    b = b.reshape(groups, 8, w)
    sub = lax.broadcasted_iota(jnp.int32, (groups, 8, w), 1)
    for d in (1, 2, 4):
        keep = sub >= d
        a_prev = jnp.where(keep, pltpu.roll(a, d, 1), 1.0)
        b_prev = jnp.where(keep, pltpu.roll(b, d, 1), 0.0)
        b = a * b_prev + b
        a = a * a_prev
    carry = carry_ref[...]
    groups_per_seq = tt // 8
    rows = []
    for g in range(groups):
        if g % groups_per_seq == 0:
            prev = carry[g // groups_per_seq]
        h_g = a[g] * prev + b[g]
        prev = h_g[7:8]
        rows.append(h_g)
    h = jnp.concatenate(rows, axis=0).reshape(nb, tt, w)
    h_last = h[:, tt - 1:tt, :]
    carry_ref[...] = h_last
    hlast_ref[...] = h_last
    out_ref[...] = _rms(h * gy_ref[...].astype(F32), g_ref[...]).astype(BF16)


def _lru(xl, gy, conv0, h0, conv_w, conv_b, wr_bd, b_r, wi_bd, b_i, lam, g_lru):
    n, t, w = xl.shape
    nb, tt = _seq_blocks(n, t)
    blk = pl.BlockSpec((nb, tt, w), lambda i, j: (i, j, 0))
    per_seq = lambda rows: pl.BlockSpec((nb, rows, w), lambda i, j: (i, 0, 0))
    const = lambda shape: pl.BlockSpec(shape, lambda i, j: (0,) * len(shape))
    row = const((1, w))
    return pl.pallas_call(
        _lru_kernel,
        grid=(n // nb, t // tt),
        in_specs=[blk, blk, per_seq(CONV_W - 1), per_seq(1),
                  const((CONV_W, w)), row, const((w, w)), row, const((w, w)), row, row, row],
        out_specs=[blk, per_seq(CONV_W - 1), per_seq(1)],
        out_shape=[jax.ShapeDtypeStruct((n, t, w), BF16),
                   jax.ShapeDtypeStruct((n, CONV_W - 1, w), F32),
                   jax.ShapeDtypeStruct((n, 1, w), F32)],
        scratch_shapes=[pltpu.VMEM((nb, 8, w), F32), pltpu.VMEM((nb, 1, w), F32)],
        compiler_params=_params("parallel", "arbitrary"),
        name="rglru",
    )(xl, gy, conv0, h0, conv_w, conv_b, wr_bd, b_r, wi_bd, b_i, lam, g_lru)


def _cumsum_rows_kernel(x_ref, upper_ref, o_ref, carry_ref):
    tb = x_ref.shape[1]

    @pl.when(pl.program_id(0) == 0)
    def _():
        carry_ref[...] = jnp.zeros_like(carry_ref)

    upper = upper_ref[...]
    hi, mid, lo = _split3(x_ref[...])
    d = _dot(hi, upper) + _dot(mid, upper) + _dot(lo, upper) + carry_ref[...]
    carry_ref[...] = d[:, tb - 1:tb]
    o_ref[...] = d


def _cumsum_rows(x, tb):
    rows, t = x.shape
    upper = jnp.asarray(np.triu(np.ones((tb, tb), np.float32)), BF16)
    return pl.pallas_call(
        _cumsum_rows_kernel,
        grid=(t // tb,),
        in_specs=[pl.BlockSpec((rows, tb), lambda j: (0, j)), pl.BlockSpec((tb, tb), lambda j: (0, 0))],
        out_specs=pl.BlockSpec((rows, tb), lambda j: (0, j)),
        out_shape=jax.ShapeDtypeStruct((rows, t), F32),
        scratch_shapes=[pltpu.VMEM((rows, 1), F32)],
        compiler_params=_params("arbitrary"),
        name="logf_cumsum",
    )(x, upper)


def _online_update(s, m_prev, l_prev):
    m_new = jnp.maximum(m_prev, jnp.max(s, axis=1, keepdims=True))
    alpha = jnp.exp(m_prev - m_new)
    p = jnp.exp(s - m_new)
    l_new = alpha * l_prev + jnp.sum(p, axis=1, keepdims=True)
    return p, alpha, m_new, l_new


def _att_prompt_kernel(bpre_ref, qt_ref, ka_ref, vt_ref, o_ref, s_ref, p_ref, acc_ref):
    tq = qt_ref.shape[2]
    tk = ATT_TK
    nblk = ka_ref.shape[1] // tk
    b, hp, i = pl.program_id(0), pl.program_id(1), pl.program_id(2)
    q0 = i * tq
    jd = q0 // tk
    kpos = lax.broadcasted_iota(jnp.int32, (tk, tq), 0)
    qpos = lax.broadcasted_iota(jnp.int32, (tk, tq), 1)
    rows = lax.broadcasted_iota(jnp.int32, (LANES, tq), 0)
    nh = qt_ref.shape[0]
    heads = range(nh)
    base = [((b * (N_HEADS // nh) + hp) * nh + hh) * nblk for hh in heads]

    def scores(j, masked):
        start = pl.multiple_of(j * tk, tk)
        col_max = []
        for hh in heads:
            s = _dot(ka_ref[hh, pl.ds(start, tk), :], qt_ref[hh])
            if masked:
                s = jnp.where(kpos + start <= qpos + q0, s, NEG_INF)
            s_ref[hh] = s
            col_max.append(jnp.max(s, axis=0, keepdims=True))
        return tuple(col_max)

    def softmax_pv(j, col_max, m):
        start = pl.multiple_of(j * tk, tk)
        m_out = []
        for hh in heads:
            c = (bpre_ref[base[hh] + jd] - bpre_ref[base[hh] + j]) * LOG2E
            m_new = jnp.maximum(m[hh], col_max[hh] + c)
            alpha = jnp.exp2(m[hh] - m_new)
            shift = m_new - c
            for ch in range(tk // ATT_CHUNK):
                sl = slice(ch * ATT_CHUNK, (ch + 1) * ATT_CHUNK)
                p_ref[hh, sl, :] = jnp.exp2(s_ref[hh, sl, :] - shift).astype(BF16)
            m_out.append((m_new, alpha))
        return tuple(m_out), start

    def accumulate(m_alpha, start):
        for hh in heads:
            pv = _dot(vt_ref[hh, :, pl.ds(start, tk)], p_ref[hh])
            acc_ref[hh] = m_alpha[hh][1] * acc_ref[hh] + pv
        return tuple(ma[0] for ma in m_alpha)

    def step(j, next_masked, carry):
        col_max, m = carry
        m_alpha, start = softmax_pv(j, col_max, m)
        col_max_next = scores(j + 1, next_masked)
        return col_max_next, accumulate(m_alpha, start)

    acc_ref[...] = jnp.zeros_like(acc_ref)
    neg = jnp.full((1, tq), NEG_INF, F32)
    n_masked = max(1, tq // tk)
    carry = (scores(0, True), (neg,) * nh)
    carry = lax.fori_loop(0, jd - 1, lambda j, cr: step(j, False, cr), carry)
    carry = lax.cond(jd > 0, lambda cr: step(jd - 1, True, cr), lambda cr: cr, carry)
    for extra in range(n_masked - 1):
        carry = step(jd + extra, True, carry)
    col_max, m = carry
    m_alpha, start = softmax_pv(jd + n_masked - 1, col_max, m)
    accumulate(m_alpha, start)

    for pair in range(nh // 2):
        acc_a, acc_b = acc_ref[2 * pair], acc_ref[2 * pair + 1]
        out_a = acc_a / acc_a[_aug_lane(0):_aug_lane(0) + 1, :]
        out_b = acc_b / acc_b[_aug_lane(1):_aug_lane(1) + 1, :]
        o_ref[0, :, pair * PAIR_W:(pair + 1) * PAIR_W] = jnp.where(rows < HEAD_DIM, out_a, out_b).T


def _att_prompt(qt, ka, vt, bpre):
    n, _, _, t = qt.shape
    assert t % ATT_TQ == 0 and (ATT_TK % ATT_TQ == 0 or ATT_TQ % ATT_TK == 0) and ATT_TK == ROW_TILE
    nh = ATT_HEADS
    grouped = lambda q: (N_HEADS // nh, nh) + q.shape[2:]
    qt, ka, vt = (a.reshape((n,) + grouped(a)) for a in (qt, ka, vt))
    bflat = jnp.transpose(bpre[:, :, 0, :N_HEADS], (0, 2, 1)).reshape(-1)
    return pl.pallas_call(
        _att_prompt_kernel,
        grid=(n, N_HEADS // nh, t // ATT_TQ),
        in_specs=[pl.BlockSpec(memory_space=pltpu.SMEM),
                  pl.BlockSpec((None, None, nh, LANES, ATT_TQ), lambda b, h, i: (b, h, 0, 0, i)),
                  pl.BlockSpec((None, None, nh, t, LANES), lambda b, h, i: (b, h, 0, 0, 0),
                               pipeline_mode=pl.Buffered(1)),
                  pl.BlockSpec((None, None, nh, LANES, t), lambda b, h, i: (b, h, 0, 0, 0),
                               pipeline_mode=pl.Buffered(1))],
        out_specs=pl.BlockSpec((1, ATT_TQ, nh * HEAD_DIM), lambda b, h, i: (b, i, h)),
        out_shape=jax.ShapeDtypeStruct((n, t, WA), F32),
        scratch_shapes=[pltpu.VMEM((nh, ATT_TK, ATT_TQ), F32), pltpu.VMEM((nh, ATT_TK, ATT_TQ), BF16),
                        pltpu.VMEM((nh, LANES, ATT_TQ), F32)],
        compiler_params=_params("parallel", "parallel", "arbitrary"),
        name="att_prompt",
    )(bflat, qt, ka, vt)


def _att_sample_kernel(q_ref, kp_ref, vp_ref, dp_ref, dend_ref, kn_ref, vn_ref, dn_ref, o_ref,
                       m_ref, l_ref, acc_ref):
    t = q_ref.shape[1]
    j = pl.program_id(1)

    @pl.when(j == 0)
    def _():
        m_ref[...] = jnp.full_like(m_ref, NEG_INF)
        l_ref[...] = jnp.zeros_like(l_ref)
        acc_ref[...] = jnp.zeros_like(acc_ref)

    def head(h, k, v, time_minor, bias, mask):
        sl = slice(h * HEAD_DIM, (h + 1) * HEAD_DIM)
        q_h = q_ref[0, :, sl]
        s = (_dot(q_h, k) if time_minor else _dot_nt(q_h, k)) + bias[h:h + 1]
        if mask is not None:
            s = jnp.where(mask, s, NEG_INF)
        p, alpha, m_new, l_new = _online_update(s, m_ref[h], l_ref[h])
        p = p.astype(BF16)
        acc = alpha * acc_ref[:, sl] + (_dot_nt(p, v) if time_minor else _dot(p, v))
        m_ref[h], l_ref[h], acc_ref[:, sl] = m_new, l_new, acc
        return acc / l_new

    bias_past = dend_ref[0] - dp_ref[0]
    for h in range(N_HEADS):
        head(h, kp_ref[0, h].astype(BF16), vp_ref[0, h].astype(BF16), True, bias_past, None)

    @pl.when(j == pl.num_programs(1) - 1)
    def _():
        bias_new = -dn_ref[0]
        qpos = lax.broadcasted_iota(jnp.int32, (t, t), 0)
        kpos = lax.broadcasted_iota(jnp.int32, (t, t), 1)
        for h in range(N_HEADS):
            sl = slice(h * HEAD_DIM, (h + 1) * HEAD_DIM)
            o_ref[0, :, sl] = head(h, kn_ref[0, :, sl], vn_ref[0, :, sl], False, bias_new, kpos <= qpos)


def _att_sample(qb, k_past, v_past, d_past, kb_new, vb_new, d_new):
    n, t, _ = qb.shape
    past = k_past.shape[1]
    tk = min(SAMPLE_TK, past)
    assert past % tk == 0
    d_end = d_past[:, :, past - 1:]
    k_past = jnp.transpose(k_past, (0, 2, 3, 1))
    v_past = jnp.transpose(v_past, (0, 2, 3, 1))
    new = lambda dt: pl.BlockSpec((1, t, WA), lambda b, j: (b, 0, 0))
    cache = pl.BlockSpec((1, N_HEADS, HEAD_DIM, tk), lambda b, j: (b, 0, 0, j))
    return pl.pallas_call(
        _att_sample_kernel,
        grid=(n, past // tk),
        in_specs=[new(BF16), cache, cache,
                  pl.BlockSpec((1, N_HEADS, tk), lambda b, j: (b, 0, j)),
                  pl.BlockSpec((1, N_HEADS, 1), lambda b, j: (b, 0, 0)),
                  new(BF16), new(BF16),
                  pl.BlockSpec((1, N_HEADS, t), lambda b, j: (b, 0, 0))],
        out_specs=pl.BlockSpec((1, t, WA), lambda b, j: (b, 0, 0)),
        out_shape=jax.ShapeDtypeStruct((n, t, WA), F32),
        scratch_shapes=[pltpu.VMEM((N_HEADS, t, 1), F32), pltpu.VMEM((N_HEADS, t, 1), F32),
                        pltpu.VMEM((t, WA), F32)],
        compiler_params=_params("parallel", "arbitrary"),
        name="att_sample",
    )(qb, k_past, v_past, d_past, d_end, kb_new, vb_new, d_new)


def _first_index_of_max(x, axis):
    mx = jnp.max(x, axis=axis, keepdims=True)
    idx = lax.broadcasted_iota(jnp.int32, x.shape, axis)
    first = jnp.min(jnp.where(x == mx, idx, x.shape[axis]), axis=axis, keepdims=True)
    return mx, idx == first


def _route(s, bias):
    m = s.shape[1]
    sb = (s + bias).reshape(N_GROUPS, GROUP_SIZE, m)
    top1, is_top1 = _first_index_of_max(sb, 1)
    top2 = jnp.max(jnp.where(is_top1, NEG_INF, sb), axis=1, keepdims=True)
    grp = (top1 + top2).reshape(N_GROUPS, m)
    gi = lax.broadcasted_iota(jnp.int32, (N_GROUPS, N_GROUPS, m), 0)
    gj = lax.broadcasted_iota(jnp.int32, (N_GROUPS, N_GROUPS, m), 1)
    other, mine = grp[None, :, :], grp[:, None, :]
    beats = (other > mine) | ((other == mine) & (gj < gi))
    g_rank = jnp.sum(beats.astype(jnp.int32), axis=1)
    g_keep = (g_rank < TOPK_GROUPS)[:, None, :]
    cand = jnp.where(g_keep, sb, NEG_INF).reshape(N_EXPERTS, m)
    picks = []
    for _ in range(TOP_K):
        _, pick = _first_index_of_max(cand, 0)
        picks.append(pick)
        cand = jnp.where(pick, NEG_INF, cand)
    w = jnp.concatenate([jnp.sum(jnp.where(pk, s, 0.0), axis=0, keepdims=True) for pk in picks], axis=0)
    return picks, w / jnp.sum(w, axis=0, keepdims=True) * ROUTED_SCALE


HALF_MASK = 0xFFFF0000


def _pack_halves(x):
    c = x.shape[1] // 2
    lo = lax.bitcast_convert_type(x[:, :c].astype(BF16).astype(F32), jnp.uint32) >> jnp.uint32(16)
    hi = lax.bitcast_convert_type(x[:, c:].astype(BF16).astype(F32), jnp.uint32) & jnp.uint32(HALF_MASK)
    return lax.bitcast_convert_type(lo | hi, jnp.int32)


def _unpack_halves(words):
    w = lax.bitcast_convert_type(words, jnp.uint32)
    lo = lax.bitcast_convert_type(w << jnp.uint32(16), F32)
    hi = lax.bitcast_convert_type(w & jnp.uint32(HALF_MASK), F32)
    return lo, hi


def _outproj_kernel(x_ref, lru_ref, att_ref, mod_ref, gatt_ref, wtop_ref, wbot_ref, gpost_ref, gpre_ref,
                    rwh_ref, rwl_ref, rb_ref, cnt_in_ref, before_ref,
                    x1_ref, hf_ref, xw_ref, ids_ref, ranks_ref, gates_ref, cnt_out_ref, carry_ref, *, split):
    nb, tt, d = x_ref.shape
    m = nb * tt
    first = (pl.program_id(0) == 0) & (pl.program_id(1) == 0)

    @pl.when(first)
    def _():
        carry_ref[...] = cnt_in_ref[...]

    if split is not None:
        @pl.when((pl.program_id(0) == split) & (pl.program_id(1) == 0))
        def _():
            carry_ref[...] = jnp.zeros_like(carry_ref)

    mod = mod_ref[...]
    att_n = _rms(att_ref[...], gatt_ref[...]).reshape(m, WA).astype(BF16)
    mix = _dot(lru_ref[...].reshape(m, WL), wtop_ref[...]) + _dot(att_n, wbot_ref[...])
    x1 = x_ref[...] + mod[:, 2:3, :] * _rms(mix, gpost_ref[...]).reshape(nb, tt, d)
    x1_ref[...] = x1
    hf = (_rms(x1, gpre_ref[...]) * (1.0 + mod[:, 4:5, :]) + mod[:, 3:4, :]).reshape(m, d)
    hf_hi = hf.astype(BF16)
    hf_ref[...] = hf_hi.reshape(nb, tt, d)
    hf_lo = (hf - hf_hi.astype(F32)).astype(BF16)
    rwh = rwh_ref[...]
    logits = _dot_nt(rwh, hf_hi) + _dot_nt(rwh, hf_lo) + _dot_nt(rwl_ref[...], hf_hi)
    picks, gates = _route(_sigmoid(logits), rb_ref[...])
    xw_ref[...] = _pack_halves(hf).reshape(nb, tt, d // 2)

    sel = jnp.zeros((N_EXPERTS, m), F32)
    for pk in picks:
        sel = sel + pk.astype(F32)
    prior = _dot(sel.astype(BF16), before_ref[...]) + carry_ref[...]
    expert = lax.broadcasted_iota(jnp.int32, (N_EXPERTS, m), 0).astype(F32)
    take = lambda pk, v: jnp.sum(jnp.where(pk, v, 0.0), axis=0, keepdims=True)
    ids_ref[...] = jnp.concatenate([take(pk, expert) for pk in picks], axis=0).astype(jnp.int32)
    ranks_ref[...] = jnp.concatenate([take(pk, prior) for pk in picks], axis=0).astype(jnp.int32)
    carry_ref[...] += jnp.sum(sel, axis=1, keepdims=True)
    cnt_out_ref[0] = carry_ref[...]
    gates = jnp.concatenate([gates, jnp.zeros((LANES - TOP_K, m), F32)], axis=0)
    gates_ref[...] = gates.T.reshape(nb, tt, LANES)


def _outproj(x, lru_n, att, mod, g_att, w_top, w_bot, g_post, g_pre, rw_hi, rw_lo, r_bias, cnt_in, split=None):
    n, t, d = x.shape
    nb, tt = _seq_blocks(n, t)
    m = nb * tt
    steps_t = t // tt
    n_groups = 1 if split is None else 2
    blk = lambda w: pl.BlockSpec((nb, tt, w), lambda i, j: (i, j, 0))
    const = lambda shape: pl.BlockSpec(shape, lambda i, j: (0,) * len(shape))
    per_tok = pl.BlockSpec((TOP_K, m), lambda i, j: (0, i * steps_t + j))
    group_of = (lambda i: 0) if split is None else (lambda i: jnp.where(i < split, 0, 1))
    before = jnp.asarray(np.triu(np.ones((m, m), np.float32), k=1), BF16)
    return pl.pallas_call(
        functools.partial(_outproj_kernel, split=split),
        grid=(n // nb, steps_t),
        in_specs=[blk(d), blk(WL), blk(WA), pl.BlockSpec((nb, 6, d), lambda i, j: (i, 0, 0)),
                  const((1, WA)), const((WL, d)), const((WA, d)), const((1, d)), const((1, d)),
                  const((N_EXPERTS, d)), const((N_EXPERTS, d)), const((N_EXPERTS, 1)), const((N_EXPERTS, 1)),
                  const((m, m))],
        out_specs=[blk(d), blk(d), blk(d // 2), per_tok, per_tok, blk(LANES),
                   pl.BlockSpec((1, N_EXPERTS, 1), lambda i, j: (group_of(i), 0, 0))],
        out_shape=[jax.ShapeDtypeStruct((n, t, d), F32), jax.ShapeDtypeStruct((n, t, d), BF16),
                   jax.ShapeDtypeStruct((n, t, d // 2), jnp.int32),
                   jax.ShapeDtypeStruct((TOP_K, n * t), jnp.int32), jax.ShapeDtypeStruct((TOP_K, n * t), jnp.int32),
                   jax.ShapeDtypeStruct((n, t, LANES), F32), jax.ShapeDtypeStruct((n_groups, N_EXPERTS, 1), F32)],
        scratch_shapes=[pltpu.VMEM((N_EXPERTS, 1), F32)],
        compiler_params=_params("arbitrary", "arbitrary"),
        name="outproj_router",
    )(x, lru_n, att, mod, g_att, w_top, w_bot, g_post, g_pre, rw_hi, rw_lo, r_bias, cnt_in, before)


def _subcore_ranges(n_items):
    info = plsc.get_sparse_core_info()
    n_workers = info.num_cores * info.num_subcores
    per_worker = n_items // n_workers
    assert per_worker * n_workers == n_items and per_worker % GATHER_ROWS == 0
    return info, plsc.VectorSubcoreMesh(core_axis_name="c", subcore_axis_name="s"), per_worker


def _scatter_rows(sources, pos, n_slots):
    m_tot, c = pos.shape[1], sources[0][0].shape[1]
    assert sum(cnt for _, _, cnt in sources) == m_tot
    assert all(first % GATHER_ROWS == 0 and cnt % GATHER_ROWS == 0 for _, first, cnt in sources)
    info, mesh, per_worker = _subcore_ranges(m_tot)
    pos_flat = pos.reshape(-1)

    @functools.partial(
        pl.kernel, mesh=mesh, out_type=jax.ShapeDtypeStruct((n_slots, c), jnp.int32),
        scratch_types=[pltpu.VMEM((GATHER_ROWS,), jnp.int32), pltpu.VMEM((GATHER_ROWS, c), jnp.int32),
                       pltpu.SemaphoreType.DMA])
    def scatter(*refs):
        src_refs, (pos_hbm, out_hbm, idx_v, rows_v, sem) = refs[:len(sources)], refs[len(sources):]
        worker = lax.axis_index("s") * info.num_cores + lax.axis_index("c")
        base = worker * per_worker

        @pl.loop(0, per_worker // GATHER_ROWS)
        def _(step):
            off = pl.multiple_of(base + step * GATHER_ROWS, GATHER_ROWS)
            token0 = 0
            for src_hbm, (_, first, cnt) in zip(src_refs, sources):
                @pl.when((off >= token0) & (off < token0 + cnt))
                def _(src_hbm=src_hbm, shift=first - token0):
                    pltpu.sync_copy(src_hbm.at[pl.ds(pl.multiple_of(off + shift, GATHER_ROWS), GATHER_ROWS)], rows_v)
                token0 += cnt
            for r in range(TOP_K):
                pltpu.sync_copy(pos_hbm.at[pl.ds(pl.multiple_of(r * m_tot + off, GATHER_ROWS), GATHER_ROWS)], idx_v)
                pltpu.async_copy(rows_v, out_hbm.at[idx_v], sem).wait()

    return scatter(*[a for a, _, _ in sources], pos_flat)


def _swiglu_halves(lo, hi, wg, wu, wd):
    c = lo.shape[1]
    hg = _dot(lo, wg[:c]) + _dot(hi, wg[c:])
    hu = _dot(lo, wu[:c]) + _dot(hi, wu[c:])
    return _dot((_silu(hg) * hu).astype(BF16), wd)


def _expert_kernel(te_ref, valid_ref, x_ref, wg_ref, wu_ref, wd_ref, y_ref, wg_bf, wu_bf, wd_bf):
    i = pl.program_id(0)
    valid = valid_ref[i]

    @pl.when((i == 0) | (te_ref[i] != te_ref[jnp.maximum(i - 1, 0)]))
    def _():
        wg_bf[...] = wg_ref[0].astype(BF16)
        wu_bf[...] = wu_ref[0].astype(BF16)
        wd_bf[...] = wd_ref[0].astype(BF16)

    @pl.when(valid > 0)
    def _():
        w = x_ref[...]
        row = lax.broadcasted_iota(jnp.int32, w.shape, 0)
        lo, hi = _unpack_halves(jnp.where(row < valid, w, 0))
        y_ref[...] = _pack_halves(_swiglu_halves(lo.astype(BF16), hi.astype(BF16), wg_bf[...], wu_bf[...], wd_bf[...]))


def _experts(xs, tile_expert, tile_valid, wg, wu, wd):
    n_slots, c = xs.shape
    n_tiles = n_slots // EXPERT_TILE
    d = 2 * c
    rows = pl.BlockSpec((EXPERT_TILE, c), lambda i, te, tv: (i, 0))
    weight = lambda shape: pl.BlockSpec((1,) + shape, lambda i, te, tv: (te[i], 0, 0))
    return pl.pallas_call(
        _expert_kernel,
        grid_spec=pltpu.PrefetchScalarGridSpec(
            num_scalar_prefetch=2, grid=(n_tiles,),
            in_specs=[rows, weight((d, D_EXPERT)), weight((d, D_EXPERT)), weight((D_EXPERT, d))],
            out_specs=rows,
            scratch_shapes=[pltpu.VMEM((d, D_EXPERT), BF16), pltpu.VMEM((d, D_EXPERT), BF16),
                            pltpu.VMEM((D_EXPERT, d), BF16)]),
        out_shape=jax.ShapeDtypeStruct((n_slots, c), jnp.int32),
        compiler_params=_params("arbitrary"),
        name="moe_experts",
    )(tile_expert, tile_valid, xs, wg, wu, wd)


def _gather_rows(table, idx):
    b, c = idx.shape[0], table.shape[1]
    info, mesh, per_worker = _subcore_ranges(b)

    n_steps = per_worker // GATHER_ROWS
    assert n_steps % 2 == 0

    @functools.partial(
        pl.kernel, mesh=mesh, out_type=jax.ShapeDtypeStruct((b, c), jnp.int32),
        scratch_types=[pltpu.VMEM((2, GATHER_ROWS), jnp.int32), pltpu.VMEM((2, GATHER_ROWS, c), jnp.int32),
                       pltpu.SemaphoreType.DMA((2,))])
    def gather(table_hbm, idx_hbm, out_hbm, idx_v, rows_v, sems):
        worker = lax.axis_index("s") * info.num_cores + lax.axis_index("c")
        base = worker * per_worker

        def chunk(step):
            return pl.ds(pl.multiple_of(base + step * GATHER_ROWS, GATHER_ROWS), GATHER_ROWS)

        def stream(buf):
            return pltpu.make_async_copy(table_hbm.at[idx_v.at[buf]], rows_v.at[buf], sems.at[buf])

        def start(step, buf):
            pltpu.sync_copy(idx_hbm.at[chunk(step)], idx_v.at[buf])
            stream(buf).start()

        def finish(step, buf):
            stream(buf).wait()
            pltpu.sync_copy(rows_v.at[buf], out_hbm.at[chunk(step)])

        start(0, 0)

        @pl.loop(0, n_steps, step=2)
        def _(step):
            start(step + 1, 1)
            finish(step, 0)

            @pl.when(step + 2 < n_steps)
            def _():
                start(step + 2, 0)

            finish(step + 1, 1)

    return gather(table, idx)


def _combine_kernel(rows_ref, gates_ref, hf_ref, x1_ref, mod_ref, sg_ref, su_ref, sd_ref, gpost_ref, y_ref):
    nb, tt, d = hf_ref.shape
    m = nb * tt
    c = d // 2
    x = hf_ref[...].reshape(m, d)
    shared = _swiglu_halves(x[:, :c], x[:, c:], sg_ref[...], su_ref[...], sd_ref[...])
    gates = gates_ref[...].reshape(m, LANES)
    acc_lo = shared[:, :c]
    acc_hi = shared[:, c:]
    for r in range(TOP_K):
        lo, hi = _unpack_halves(rows_ref[r])
        g = gates[:, r:r + 1]
        acc_lo = acc_lo + g * lo
        acc_hi = acc_hi + g * hi
    z = _rms(jnp.concatenate([acc_lo, acc_hi], axis=1), gpost_ref[...]).reshape(nb, tt, d)
    y_ref[...] = x1_ref[...] + mod_ref[...][:, 5:6, :] * z


def _combine(rows, first_tile, gates_t, hf, x1, mod, sg, su, sd, g_post, blocks=None, y_prev=None):
    n, t, d = hf.shape
    nb, tt = _seq_blocks(n, t)
    m = nb * tt
    steps_t = t // tt
    b0, nblocks = (0, n // nb) if blocks is None else blocks
    blk = lambda w: pl.BlockSpec((nb, tt, w), lambda i, j: (b0 + i, j, 0))
    const = lambda shape: pl.BlockSpec(shape, lambda i, j: (0,) * len(shape))
    in_specs = [pl.BlockSpec((TOP_K, m, d // 2), lambda i, j: (0, first_tile + i * steps_t + j, 0)),
                blk(LANES), blk(d), blk(d), pl.BlockSpec((nb, 6, d), lambda i, j: (b0 + i, 0, 0)),
                const((d, D_EXPERT)), const((d, D_EXPERT)), const((D_EXPERT, d)), const((1, d))]
    args = (rows, gates_t, hf, x1, mod, sg, su, sd, g_post)
    kernel_fn, aliases = _combine_kernel, {}
    if y_prev is not None:
        in_specs.append(pl.BlockSpec(memory_space=pl.ANY))
        args += (y_prev,)
        aliases = {len(args) - 1: 0}
        kernel_fn = lambda *refs: _combine_kernel(*refs[:len(args) - 1], refs[-1])
    return pl.pallas_call(
        kernel_fn,
        grid=(nblocks, steps_t),
        in_specs=in_specs,
        out_specs=blk(d),
        out_shape=jax.ShapeDtypeStruct((n, t, d), F32),
        input_output_aliases=aliases,
        compiler_params=_params("parallel", "parallel"),
        name="moe_combine",
    )(*args)


def _slots_kernel(starts_ref, ids_ref, ranks_ref, pos_ref):
    ids = ids_ref[...]

    def add_start(e, pos):
        return pos + jnp.where(ids == e, starts_ref[e], 0)

    pos_ref[...] = lax.fori_loop(0, N_EXPERTS, add_start, ranks_ref[...])


def _slots(starts, ids, ranks):
    k, m_tot = ids.shape
    cols = math.gcd(m_tot, SLOT_COLS)
    assert cols % LANES == 0
    blk = pl.BlockSpec((k, cols), lambda i: (0, i))
    return pl.pallas_call(
        _slots_kernel,
        grid=(m_tot // cols,),
        in_specs=[pl.BlockSpec(memory_space=pltpu.SMEM), blk, blk],
        out_specs=blk,
        out_shape=jax.ShapeDtypeStruct((k, m_tot), jnp.int32),
        compiler_params=_params("arbitrary"),
        name="moe_slots",
    )(starts, ids, ranks)


def _slot_plan(ids, ranks, counts):
    n_pairs = ids.shape[0] * ids.shape[1]
    n_tiles = -(-(n_pairs + N_EXPERTS * (EXPERT_TILE - 1)) // EXPERT_TILE)
    cnt = counts.reshape(N_EXPERTS).astype(jnp.int32)
    padded = (cnt + EXPERT_TILE - 1) // EXPERT_TILE * EXPERT_TILE
    ends = jnp.cumsum(padded)
    starts = ends - padded
    pos = _slots(starts, ids, ranks)
    tile_start = jnp.arange(n_tiles, dtype=jnp.int32) * EXPERT_TILE
    in_expert = (tile_start[:, None] >= starts[None, :]) & (tile_start[:, None] < ends[None, :])
    tile_expert = jnp.sum(jnp.where(in_expert, jnp.arange(N_EXPERTS, dtype=jnp.int32)[None, :], 0), axis=1)
    tile_fill = jnp.sum(jnp.where(in_expert, (starts + cnt)[None, :] - tile_start[:, None], 0), axis=1)
    tile_valid = jnp.clip(tile_fill, 0, EXPERT_TILE).astype(jnp.int32)
    return pos, tile_expert.astype(jnp.int32), tile_valid, n_tiles * EXPERT_TILE


def _block_diag(w):
    g, bw, _ = w.shape
    eye = jnp.eye(g, dtype=w.dtype)
    return (eye[:, None, :, None] * w[:, :, None, :]).reshape(g * bw, g * bw)


def _prep_weights(p):
    d_main = 2 * WL + 3 * WA
    w_in = p["w_in"]
    rw_t = p["router_w"].T
    rw_hi = rw_t.astype(BF16)
    row = lambda v: v.reshape(1, -1)
    return dict(
        w_mod=p["w_mod"], b_mod=p["b_mod"],
        g_pre_mix=row(p["g_pre_mix"]), g_post_mix=row(p["g_post_mix"]),
        g_pre_ffn=row(p["g_pre_ffn"]), g_post_ffn=row(p["g_post_ffn"]),
        w_main=w_in[:, :d_main].astype(BF16),
        w_f=jnp.pad(w_in[:, d_main:], ((0, 0), (0, LANES - N_HEADS))).astype(BF16),
        b_f=jnp.pad(p["b_f"], (0, LANES - N_HEADS)).reshape(1, LANES),
        conv_w=p["conv_w"], conv_b=row(p["conv_b"]),
        wr_bd=_block_diag(p["w_r"]).astype(BF16), b_r=row(p["b_r"]),
        wi_bd=_block_diag(p["w_i"]).astype(BF16), b_i=row(p["b_i"]),
        lam=row(p["lru_lambda"]), g_lru=row(p["g_lru_out"]), g_att=row(p["g_att_out"]),
        w_top=p["w_out"][:WL].astype(BF16), w_bot=p["w_out"][WL:].astype(BF16),
        rw_hi=rw_hi, rw_lo=(rw_t - rw_hi.astype(F32)).astype(BF16),
        r_bias=p["router_bias"].reshape(N_EXPERTS, 1),
        wg=p["w_gate"], wu=p["w_up"], wd=p["w_down"],
        sg=p["ws_gate"].astype(BF16), su=p["ws_up"].astype(BF16), sd=p["ws_down"].astype(BF16),
    )


def _mixers(x, mod, conv0, h0, past, w, cnt_in, split=None):
    n, t, _ = x.shape
    proj_args = (x, mod, w["g_pre_mix"], w["w_main"], w["w_f"], w["b_f"])
    if past is None:
        xl, gy, k, v, lf, qt, ka, vt, bpre = _inproj_prompt(*proj_args)
        att = _att_prompt(qt, ka, vt, bpre)
    else:
        xl, gy, qb, kb, vb, k, v, lf = _inproj(*proj_args)
        k_past, v_past, lf_past = past
        plen = k_past.shape[1]
        by_head = lambda a: jnp.transpose(a, (0, 2, 1)).reshape(n * N_HEADS, a.shape[1])
        d_new = _cumsum_rows(by_head(lf), t).reshape(n, N_HEADS, t)
        d_past = _cumsum_rows(by_head(lf_past), min(CUMSUM_COLS, plen)).reshape(n, N_HEADS, plen)
        att = _att_sample(qb, k_past, v_past, d_past, kb, vb, d_new)
    lru_n, conv_new, h_new = _lru(xl, gy, conv0, h0.reshape(n, 1, WL), w["conv_w"], w["conv_b"],
                                  w["wr_bd"], w["b_r"], w["wi_bd"], w["b_i"], w["lam"], w["g_lru"])
    routed = _outproj(x, lru_n, att, mod, w["g_att"], w["w_top"], w["w_bot"], w["g_post_mix"],
                      w["g_pre_ffn"], w["rw_hi"], w["rw_lo"], w["r_bias"], cnt_in, split)
    state = (k.reshape(n, t, N_HEADS, HEAD_DIM), v.reshape(n, t, N_HEADS, HEAD_DIM), lf,
             conv_new, h_new.reshape(n, WL))
    return routed, state


def _layer(xp, xs, mod_p, mod_s, conv_s, h_s, past_s, w):
    n_p = xp.shape[0]
    conv0 = jnp.zeros((n_p, CONV_W - 1, WL), F32)
    h0 = jnp.zeros((n_p, WL), F32)
    zero_cnt = jnp.zeros((N_EXPERTS, 1), F32)
    t_p = xp.shape[1]
    split = 1 if n_p > 1 else None
    (x1_p, hf_p, xw_p, ids_p, rk_p, g_p, cnt_p), st_p = _mixers(xp, mod_p, conv0, h0, None, w, zero_cnt, split)
    cnt_tail = cnt_p[1] if n_p > 1 else zero_cnt
    (x1_s, hf_s, xw_s, ids_s, rk_s, g_s, cnt_s), st_s = _mixers(xs, mod_s, conv_s, h_s, past_s, w, cnt_tail)

    half = xw_p.shape[-1]
    xw_p = xw_p.reshape(-1, half)

    def routed(sources, ids, ranks, counts):
        pos, tile_expert, tile_valid, n_slots = _slot_plan(ids, ranks, counts)
        ys = _experts(_scatter_rows(sources, pos, n_slots), tile_expert, tile_valid, w["wg"], w["wu"], w["wd"])
        return _gather_rows(ys, pos.reshape(-1)).reshape(TOP_K, ids.shape[1], half)

    rows_a = routed([(xw_p, 0, t_p)], ids_p[:, :t_p], rk_p[:, :t_p], cnt_p[0])
    xw_s = xw_s.reshape(-1, half)
    sources_b = ([(xw_p, t_p, (n_p - 1) * t_p)] if n_p > 1 else []) + [(xw_s, 0, xw_s.shape[0])]
    rows_b = routed(sources_b, jnp.concatenate([ids_p[:, t_p:], ids_s], axis=1),
                    jnp.concatenate([rk_p[:, t_p:], rk_s], axis=1), cnt_s[0])
    shared = (w["sg"], w["su"], w["sd"], w["g_post_ffn"])
    yp = _combine(rows_a, 0, g_p, hf_p, x1_p, mod_p, *shared, blocks=(0, 1))
    if n_p > 1:
        yp = _combine(rows_b, 0, g_p, hf_p, x1_p, mod_p, *shared, blocks=(1, n_p - 1), y_prev=yp)
    ysmp = _combine(rows_b, (n_p - 1) * t_p // ROW_TILE, g_s, hf_s, x1_s, mod_s, *shared)
    return yp, ysmp, st_p, st_s


def kernel(x_prompt, x_sample, c_prompt, c_sample, cache_k, cache_v, cache_logf, state_conv, state_lru, w_mod, b_mod, g_pre_mix, g_post_mix, g_pre_ffn, g_post_ffn, w_in, conv_w, conv_b, w_r, b_r, w_i, b_i, lru_lambda, b_f, g_lru_out, g_att_out, w_out, router_w, router_bias, w_gate, w_up, w_down, ws_gate, ws_up, ws_down):
    names = ("w_mod", "b_mod", "g_pre_mix", "g_post_mix", "g_pre_ffn", "g_post_ffn", "w_in", "conv_w", "conv_b",
             "w_r", "b_r", "w_i", "b_i", "lru_lambda", "b_f", "g_lru_out", "g_att_out", "w_out", "router_w",
             "router_bias", "w_gate", "w_up", "w_down", "ws_gate", "ws_up", "ws_down")
    stacked = (w_mod, b_mod, g_pre_mix, g_post_mix, g_pre_ffn, g_post_ffn, w_in, conv_w, conv_b, w_r, b_r, w_i, b_i,
               lru_lambda, b_f, g_lru_out, g_att_out, w_out, router_w, router_bias, w_gate, w_up, w_down,
               ws_gate, ws_up, ws_down)
    depth = w_mod.shape[0]
    n_p, n_s = x_prompt.shape[0], x_sample.shape[0]
    yp, ys = x_prompt, x_sample
    st_p, st_s = [], []
    for l in range(depth):
        w = _prep_weights({k: v[l] for k, v in zip(names, stacked)})
        mod = _modulation(jnp.concatenate([c_prompt, c_sample], axis=0), w["w_mod"], w["b_mod"])
        mod = mod.reshape(n_p + n_s, 6, D_MODEL)
        yp, ys, sp, ss = _layer(yp, ys, mod[:n_p], mod[n_p:], state_conv[l], state_lru[l],
                                (cache_k[l], cache_v[l], cache_logf[l]), w)
        st_p.append(sp)
        st_s.append(ss)
    stack = lambda sts, i: jnp.stack([s[i] for s in sts])
    return (yp, ys) + tuple(stack(st_p, i) for i in range(5)) + tuple(stack(st_s, i) for i in range(5))
```

```python
import functools
import math

import jax
import jax.numpy as jnp
import numpy as np
from jax import lax
from jax.experimental import pallas as pl
from jax.experimental.pallas import tpu as pltpu
from jax.experimental.pallas import tpu_sc as plsc

F32 = jnp.float32
BF16 = jnp.bfloat16

D_MODEL = 1024
WL = 512
WA = 512
N_HEADS = 8
HEAD_DIM = 64
N_PAIRS = N_HEADS // 2
PAIR_W = 2 * HEAD_DIM
LANES = 128
CONV_W = 4
LRU_BLOCKS = 8
LRU_C = 8.0
N_EXPERTS = 64
N_GROUPS = 8
GROUP_SIZE = N_EXPERTS // N_GROUPS
TOPK_GROUPS = 4
TOP_K = 8
D_EXPERT = 256
ROUTED_SCALE = 2.5
EPS = 1e-6
NEG_INF = float("-inf")
LOG2E = 1.4426950408889634

ROW_TILE = 512
ATT_TQ = 512
ATT_TK = 512
ATT_CHUNK = 64
ATT_HEADS = 4
SAMPLE_TK = 4096
CUMSUM_COLS = 1024
EXPERT_TILE = 1024
SLOT_COLS = 2048
GATHER_ROWS = 64
VMEM_LIMIT = 56 * 1024 * 1024


def _params(*sem):
    return pltpu.CompilerParams(dimension_semantics=sem, vmem_limit_bytes=VMEM_LIMIT)


def _dot(a, b):
    return jnp.dot(a, b, preferred_element_type=F32)


def _dot_nt(a, b):
    return lax.dot_general(a, b, (((1,), (1,)), ((), ())), preferred_element_type=F32)


def _split3(x):
    hi = x.astype(BF16)
    r1 = x - hi.astype(F32)
    mid = r1.astype(BF16)
    lo = (r1 - mid.astype(F32)).astype(BF16)
    return hi, mid, lo


def _rms(x, g):
    return x * lax.rsqrt(jnp.mean(x * x, axis=-1, keepdims=True) + EPS) * g


def _sigmoid(x):
    return 1.0 / (1.0 + jnp.exp(-x))


def _silu(x):
    return x * _sigmoid(x)


def _gelu_tanh(x):
    return 0.5 * x * (1.0 + jnp.tanh(0.7978845608028654 * (x + 0.044715 * (x * x * x))))


def _log_sigmoid(x):
    return jnp.minimum(x, 0.0) - jnp.log1p(jnp.exp(-jnp.abs(x)))


def _seq_blocks(n, t):
    if t >= ROW_TILE:
        assert t % ROW_TILE == 0
        return 1, ROW_TILE
    nb = ROW_TILE // t
    assert nb * t == ROW_TILE and n % nb == 0
    return nb, t


def _mod_kernel(c_ref, w_ref, b_ref, o_ref):
    c = _silu(c_ref[...])
    c_hi = c.astype(BF16)
    c_lo = (c - c_hi.astype(F32)).astype(BF16)
    w = w_ref[...]
    w_hi = w.astype(BF16)
    w_lo = (w - w_hi.astype(F32)).astype(BF16)
    o_ref[...] = _dot(c_hi, w_hi) + _dot(c_lo, w_hi) + _dot(c_hi, w_lo) + b_ref[...]


def _modulation(c, w_mod, b_mod):
    rows = c.shape[0]
    n = -(-rows // 8) * 8
    c = jnp.pad(c, ((0, n - rows), (0, 0)))
    d6 = w_mod.shape[1]
    return pl.pallas_call(
        _mod_kernel,
        grid=(d6 // D_MODEL,),
        in_specs=[pl.BlockSpec((n, D_MODEL), lambda j: (0, 0)),
                  pl.BlockSpec((D_MODEL, D_MODEL), lambda j: (0, j)),
                  pl.BlockSpec((1, D_MODEL), lambda j: (0, j))],
        out_specs=pl.BlockSpec((n, D_MODEL), lambda j: (0, j)),
        out_shape=jax.ShapeDtypeStruct((n, d6), F32),
        compiler_params=_params("arbitrary"),
        name="modulation",
    )(c, w_mod, b_mod.reshape(1, d6))[:rows]


def _inproj_kernel(x_ref, mod_ref, g_ref, w_ref, wf_ref, bf_ref,
                   xl_ref, gy_ref, qb_ref, kb_ref, vb_ref, k_ref, v_ref, lf_ref):
    nb, tt, d = x_ref.shape
    x = x_ref[...]
    mod = mod_ref[...]
    hn = _rms(x, g_ref[...]) * (1.0 + mod[:, 1:2, :]) + mod[:, 0:1, :]
    hb = hn.reshape(nb * tt, d).astype(BF16)

    def proj(col):
        return _dot(hb, w_ref[:, col * WL:(col + 1) * WL]).reshape(nb, tt, WL)

    xl_ref[...] = proj(0)
    gy_ref[...] = _gelu_tanh(proj(1)).astype(BF16)
    qb_ref[...] = (proj(2) * (HEAD_DIM ** -0.5)).astype(BF16)
    k = proj(3)
    k_ref[...] = k
    kb_ref[...] = k.astype(BF16)
    v = proj(4)
    v_ref[...] = v
    vb_ref[...] = v.astype(BF16)
    fl = _dot(hb, wf_ref[...]) + bf_ref[...]
    lf_ref[...] = _log_sigmoid(fl).reshape(nb, tt, LANES)[:, :, :N_HEADS]


def _inproj(x, mod, g_pre, w_main, w_f, b_f):
    n, t, d = x.shape
    nb, tt = _seq_blocks(n, t)
    blk = lambda w: pl.BlockSpec((nb, tt, w), lambda i, j: (i, j, 0))
    const = lambda shape: pl.BlockSpec(shape, lambda i, j: (0,) * len(shape))
    f32 = lambda w: jax.ShapeDtypeStruct((n, t, w), F32)
    b16 = lambda w: jax.ShapeDtypeStruct((n, t, w), BF16)
    return pl.pallas_call(
        _inproj_kernel,
        grid=(n // nb, t // tt),
        in_specs=[blk(d),
                  pl.BlockSpec((nb, 6, d), lambda i, j: (i, 0, 0)),
                  const((1, d)), const(w_main.shape), const(w_f.shape), const((1, LANES))],
        out_specs=[blk(WL), blk(WL), blk(WA), blk(WA), blk(WA), blk(WA), blk(WA), blk(N_HEADS)],
        out_shape=[f32(WL), b16(WL), b16(WA), b16(WA), b16(WA), f32(WA), f32(WA), f32(N_HEADS)],
        compiler_params=_params("parallel", "arbitrary"),
        name="inproj",
    )(x, mod, g_pre, w_main, w_f, b_f)


def _aug_lane(h):
    return HEAD_DIM if h % 2 == 0 else 0


def _inproj_prompt_kernel(x_ref, mod_ref, g_ref, w_ref, wf_ref, bf_ref, place_ref,
                          xl_ref, gy_ref, k_ref, v_ref, lf_ref, qt_ref, ka_ref, vt_ref, bpre_ref, carry_ref):
    _, tt, d = x_ref.shape

    @pl.when(pl.program_id(1) == 0)
    def _():
        carry_ref[...] = jnp.zeros_like(carry_ref)

    mod = mod_ref[0]
    hb = (_rms(x_ref[0], g_ref[...]) * (1.0 + mod[1:2, :]) + mod[0:1, :]).astype(BF16)

    def proj(col):
        return _dot(hb, w_ref[:, col * WL:(col + 1) * WL])

    xl_ref[0] = proj(0)
    gy_ref[0] = _gelu_tanh(proj(1)).astype(BF16)
    q = proj(2) * (HEAD_DIM ** -0.5 * LOG2E)
    k = proj(3)
    k_ref[0] = k
    v = proj(4)
    v_ref[0] = v
    fl = _dot(hb, wf_ref[...]) + bf_ref[...]
    lane = lax.broadcasted_iota(jnp.int32, (tt, LANES), 1)
    lf = jnp.where(lane < N_HEADS, _log_sigmoid(fl), 0.0)
    lf_ref[0] = lf.T[:N_HEADS, :]

    row = lax.broadcasted_iota(jnp.int32, (tt, tt), 0)
    col = lax.broadcasted_iota(jnp.int32, (tt, tt), 1)
    tril = (col <= row).astype(BF16)
    hi, mid, lo = _split3(lf)
    e = _dot(tril, hi) + _dot(tril, mid) + _dot(tril, lo)
    bpre_ref[0, 0] = carry_ref[...]
    carry_ref[...] += e[tt - 1:tt, :]
    e_hi, e_mid, e_lo = _split3(e * (-LOG2E))
    aug_k = _dot(jnp.concatenate([e_hi, e_mid, e_lo], axis=1), place_ref[...])

    for h in range(N_HEADS):
        pair = slice((h // 2) * PAIR_W, (h // 2 + 1) * PAIR_W)
        dims = (lane < HEAD_DIM) if h % 2 == 0 else (lane >= HEAD_DIM)
        a0 = _aug_lane(h)
        ones3 = ((lane >= a0) & (lane < a0 + 3)).astype(F32)
        qt_ref[0, h] = (jnp.where(dims, q[:, pair], 0.0) + ones3).T.astype(BF16)
        ka_ref[0, h] = (jnp.where(dims, k[:, pair], 0.0) + aug_k[:, h * LANES:(h + 1) * LANES]).astype(BF16)
        one1 = (lane == a0).astype(F32)
        vt_ref[0, h] = (jnp.where(dims, v[:, pair], 0.0) + one1).T.astype(BF16)


def _placement():
    pl_mat = np.zeros((3 * LANES, N_HEADS * LANES), np.float32)
    for p in range(3):
        for h in range(N_HEADS):
            pl_mat[p * LANES + h, h * LANES + _aug_lane(h) + p] = 1.0
    return jnp.asarray(pl_mat, BF16)


def _inproj_prompt(x, mod, g_pre, w_main, w_f, b_f):
    n, t, d = x.shape
    tt = ROW_TILE
    assert t % tt == 0
    nblk = t // tt
    blk = lambda w: pl.BlockSpec((1, tt, w), lambda i, j: (i, j, 0))
    const = lambda shape: pl.BlockSpec(shape, lambda i, j: (0,) * len(shape))
    f32 = lambda w: jax.ShapeDtypeStruct((n, t, w), F32)
    place = _placement()
    return pl.pallas_call(
        _inproj_prompt_kernel,
        grid=(n, nblk),
        in_specs=[blk(d), pl.BlockSpec((1, 6, d), lambda i, j: (i, 0, 0)),
                  const((1, d)), const(w_main.shape), const(w_f.shape), const((1, LANES)), const(place.shape)],
        out_specs=[blk(WL), blk(WL), blk(WA), blk(WA), pl.BlockSpec((1, N_HEADS, tt), lambda i, j: (i, 0, j)),
                   pl.BlockSpec((1, N_HEADS, LANES, tt), lambda i, j: (i, 0, 0, j)),
                   pl.BlockSpec((1, N_HEADS, tt, LANES), lambda i, j: (i, 0, j, 0)),
                   pl.BlockSpec((1, N_HEADS, LANES, tt), lambda i, j: (i, 0, 0, j)),
                   pl.BlockSpec((1, 1, 1, LANES), lambda i, j: (i, j, 0, 0))],
        out_shape=[f32(WL), jax.ShapeDtypeStruct((n, t, WL), BF16), f32(WA), f32(WA),
                   jax.ShapeDtypeStruct((n, N_HEADS, t), F32),
                   jax.ShapeDtypeStruct((n, N_HEADS, LANES, t), BF16),
                   jax.ShapeDtypeStruct((n, N_HEADS, t, LANES), BF16),
                   jax.ShapeDtypeStruct((n, N_HEADS, LANES, t), BF16),
                   jax.ShapeDtypeStruct((n, nblk, 1, LANES), F32)],
        scratch_shapes=[pltpu.VMEM((1, LANES), F32)],
        compiler_params=_params("parallel", "arbitrary"),
        name="inproj_prompt",
    )(x, mod, g_pre, w_main, w_f, b_f, place)


def _expm1_neg(x):
    poly = x * (1.0 + x * (0.5 + x * (1.0 / 6.0 + x * (1.0 / 24.0 + x * (1.0 / 120.0)))))
    return jnp.where(x > -0.1, poly, jnp.exp(x) - 1.0)


def _lru_kernel(xl_ref, gy_ref, conv0_ref, h0_ref, cw_ref, cb_ref, wr_ref, br_ref, wi_ref, bi_ref,
                lam_ref, g_ref, out_ref, conv_ref, hlast_ref, tail_ref, carry_ref):
    nb, tt, w = xl_ref.shape
    j = pl.program_id(1)

    @pl.when(j == 0)
    def _():
        tail_ref[:, 8 - (CONV_W - 1):, :] = conv0_ref[...]
        carry_ref[...] = h0_ref[...]

    xl = xl_ref[...]
    xpad = jnp.concatenate([tail_ref[...], xl], axis=1)
    cw = cw_ref[...]
    xc = jnp.zeros_like(xl) + cb_ref[...]
    for k in range(CONV_W):
        off = 8 - (CONV_W - 1) + k
        xc = xc + xpad[:, off:off + tt, :] * cw[k:k + 1, :]
    conv_ref[...] = xpad[:, tt + 8 - (CONV_W - 1):, :]
    tail_ref[...] = xpad[:, tt:, :]

    m = nb * tt
    xf = xc.reshape(m, w)
    xb = xf.astype(BF16)
    r = _sigmoid(_dot(xb, wr_ref[...]) + br_ref[...])
    gi = _sigmoid(_dot(xb, wi_ref[...]) + bi_ref[...])
    lam = lam_ref[...]
    softplus = jnp.maximum(-lam, 0.0) + jnp.log1p(jnp.exp(-jnp.abs(lam)))
    log_a = (-LRU_C) * r * softplus
    a = jnp.exp(log_a)
    b = jnp.sqrt(-_expm1_neg(2.0 * log_a)) * (gi * xf)

    groups = m // 8
    a = a.reshape(groups, 8, w)
    b = b.reshape(groups, 8, w)
    sub = lax.broadcasted_iota(jnp.int32, (groups, 8, w), 1)
    for d in (1, 2, 4):
        keep = sub >= d
        a_prev = jnp.where(keep, pltpu.roll(a, d, 1), 1.0)
        b_prev = jnp.where(keep, pltpu.roll(b, d, 1), 0.0)
        b = a * b_prev + b
        a = a * a_prev
    carry = carry_ref[...]
    groups_per_seq = tt // 8
    rows = []
    for g in range(groups):
        if g % groups_per_seq == 0:
            prev = carry[g // groups_per_seq]
        h_g = a[g] * prev + b[g]
        prev = h_g[7:8]
        rows.append(h_g)
    h = jnp.concatenate(rows, axis=0).reshape(nb, tt, w)
    h_last = h[:, tt - 1:tt, :]
    carry_ref[...] = h_last
    hlast_ref[...] = h_last
    out_ref[...] = _rms(h * gy_ref[...].astype(F32), g_ref[...]).astype(BF16)


def _lru(xl, gy, conv0, h0, conv_w, conv_b, wr_bd, b_r, wi_bd, b_i, lam, g_lru):
    n, t, w = xl.shape
    nb, tt = _seq_blocks(n, t)
    blk = pl.BlockSpec((nb, tt, w), lambda i, j: (i, j, 0))
    per_seq = lambda rows: pl.BlockSpec((nb, rows, w), lambda i, j: (i, 0, 0))
    const = lambda shape: pl.BlockSpec(shape, lambda i, j: (0,) * len(shape))
    row = const((1, w))
    return pl.pallas_call(
        _lru_kernel,
        grid=(n // nb, t // tt),
        in_specs=[blk, blk, per_seq(CONV_W - 1), per_seq(1),
                  const((CONV_W, w)), row, const((w, w)), row, const((w, w)), row, row, row],
        out_specs=[blk, per_seq(CONV_W - 1), per_seq(1)],
        out_shape=[jax.ShapeDtypeStruct((n, t, w), BF16),
                   jax.ShapeDtypeStruct((n, CONV_W - 1, w), F32),
                   jax.ShapeDtypeStruct((n, 1, w), F32)],
        scratch_shapes=[pltpu.VMEM((nb, 8, w), F32), pltpu.VMEM((nb, 1, w), F32)],
        compiler_params=_params("parallel", "arbitrary"),
        name="rglru",
    )(xl, gy, conv0, h0, conv_w, conv_b, wr_bd, b_r, wi_bd, b_i, lam, g_lru)


def _cumsum_rows_kernel(x_ref, upper_ref, o_ref, carry_ref):
    tb = x_ref.shape[1]

    @pl.when(pl.program_id(0) == 0)
    def _():
        carry_ref[...] = jnp.zeros_like(carry_ref)

    upper = upper_ref[...]
    hi, mid, lo = _split3(x_ref[...])
    d = _dot(hi, upper) + _dot(mid, upper) + _dot(lo, upper) + carry_ref[...]
    carry_ref[...] = d[:, tb - 1:tb]
    o_ref[...] = d


def _cumsum_rows(x, tb):
    rows, t = x.shape
    upper = jnp.asarray(np.triu(np.ones((tb, tb), np.float32)), BF16)
    return pl.pallas_call(
        _cumsum_rows_kernel,
        grid=(t // tb,),
        in_specs=[pl.BlockSpec((rows, tb), lambda j: (0, j)), pl.BlockSpec((tb, tb), lambda j: (0, 0))],
        out_specs=pl.BlockSpec((rows, tb), lambda j: (0, j)),
        out_shape=jax.ShapeDtypeStruct((rows, t), F32),
        scratch_shapes=[pltpu.VMEM((rows, 1), F32)],
        compiler_params=_params("arbitrary"),
        name="logf_cumsum",
    )(x, upper)


def _online_update(s, m_prev, l_prev):
    m_new = jnp.maximum(m_prev, jnp.max(s, axis=1, keepdims=True))
    alpha = jnp.exp(m_prev - m_new)
    p = jnp.exp(s - m_new)
    l_new = alpha * l_prev + jnp.sum(p, axis=1, keepdims=True)
    return p, alpha, m_new, l_new


def _att_prompt_kernel(bpre_ref, qt_ref, ka_ref, vt_ref, o_ref, s_ref, p_ref, acc_ref):
    tq = qt_ref.shape[2]
    tk = ATT_TK
    nblk = ka_ref.shape[1] // tk
    b, hp, i = pl.program_id(0), pl.program_id(1), pl.program_id(2)
    q0 = i * tq
    jd = q0 // tk
    kpos = lax.broadcasted_iota(jnp.int32, (tk, tq), 0)
    qpos = lax.broadcasted_iota(jnp.int32, (tk, tq), 1)
    rows = lax.broadcasted_iota(jnp.int32, (LANES, tq), 0)
    nh = qt_ref.shape[0]
    heads = range(nh)
    base = [((b * (N_HEADS // nh) + hp) * nh + hh) * nblk for hh in heads]

    def scores(j, masked):
        start = pl.multiple_of(j * tk, tk)
        col_max = []
        for hh in heads:
            s = _dot(ka_ref[hh, pl.ds(start, tk), :], qt_ref[hh])
            if masked:
                s = jnp.where(kpos + start <= qpos + q0, s, NEG_INF)
            s_ref[hh] = s
            col_max.append(jnp.max(s, axis=0, keepdims=True))
        return tuple(col_max)

    def softmax_pv(j, col_max, m):
        start = pl.multiple_of(j * tk, tk)
        m_out = []
        for hh in heads:
            c = (bpre_ref[base[hh] + jd] - bpre_ref[base[hh] + j]) * LOG2E
            m_new = jnp.maximum(m[hh], col_max[hh] + c)
            alpha = jnp.exp2(m[hh] - m_new)
            shift = m_new - c
            for ch in range(tk // ATT_CHUNK):
                sl = slice(ch * ATT_CHUNK, (ch + 1) * ATT_CHUNK)
                p_ref[hh, sl, :] = jnp.exp2(s_ref[hh, sl, :] - shift).astype(BF16)
            m_out.append((m_new, alpha))
        return tuple(m_out), start

    def accumulate(m_alpha, start):
        for hh in heads:
            pv = _dot(vt_ref[hh, :, pl.ds(start, tk)], p_ref[hh])
            acc_ref[hh] = m_alpha[hh][1] * acc_ref[hh] + pv
        return tuple(ma[0] for ma in m_alpha)

    def step(j, next_masked, carry):
        col_max, m = carry
        m_alpha, start = softmax_pv(j, col_max, m)
        col_max_next = scores(j + 1, next_masked)
        return col_max_next, accumulate(m_alpha, start)

    acc_ref[...] = jnp.zeros_like(acc_ref)
    neg = jnp.full((1, tq), NEG_INF, F32)
    n_masked = max(1, tq // tk)
    carry = (scores(0, True), (neg,) * nh)
    carry = lax.fori_loop(0, jd - 1, lambda j, cr: step(j, False, cr), carry)
    carry = lax.cond(jd > 0, lambda cr: step(jd - 1, True, cr), lambda cr: cr, carry)
    for extra in range(n_masked - 1):
        carry = step(jd + extra, True, carry)
    col_max, m = carry
    m_alpha, start = softmax_pv(jd + n_masked - 1, col_max, m)
    accumulate(m_alpha, start)

    for pair in range(nh // 2):
        acc_a, acc_b = acc_ref[2 * pair], acc_ref[2 * pair + 1]
        out_a = acc_a / acc_a[_aug_lane(0):_aug_lane(0) + 1, :]
        out_b = acc_b / acc_b[_aug_lane(1):_aug_lane(1) + 1, :]
        o_ref[0, :, pair * PAIR_W:(pair + 1) * PAIR_W] = jnp.where(rows < HEAD_DIM, out_a, out_b).T


def _att_prompt(qt, ka, vt, bpre):
    n, _, _, t = qt.shape
    assert t % ATT_TQ == 0 and (ATT_TK % ATT_TQ == 0 or ATT_TQ % ATT_TK == 0) and ATT_TK == ROW_TILE
    nh = ATT_HEADS
    grouped = lambda q: (N_HEADS // nh, nh) + q.shape[2:]
    qt, ka, vt = (a.reshape((n,) + grouped(a)) for a in (qt, ka, vt))
    bflat = jnp.transpose(bpre[:, :, 0, :N_HEADS], (0, 2, 1)).reshape(-1)
    return pl.pallas_call(
        _att_prompt_kernel,
        grid=(n, N_HEADS // nh, t // ATT_TQ),
        in_specs=[pl.BlockSpec(memory_space=pltpu.SMEM),
                  pl.BlockSpec((None, None, nh, LANES, ATT_TQ), lambda b, h, i: (b, h, 0, 0, i)),
                  pl.BlockSpec((None, None, nh, t, LANES), lambda b, h, i: (b, h, 0, 0, 0),
                               pipeline_mode=pl.Buffered(1)),
                  pl.BlockSpec((None, None, nh, LANES, t), lambda b, h, i: (b, h, 0, 0, 0),
                               pipeline_mode=pl.Buffered(1))],
        out_specs=pl.BlockSpec((1, ATT_TQ, nh * HEAD_DIM), lambda b, h, i: (b, i, h)),
        out_shape=jax.ShapeDtypeStruct((n, t, WA), F32),
        scratch_shapes=[pltpu.VMEM((nh, ATT_TK, ATT_TQ), F32), pltpu.VMEM((nh, ATT_TK, ATT_TQ), BF16),
                        pltpu.VMEM((nh, LANES, ATT_TQ), F32)],
        compiler_params=_params("parallel", "parallel", "arbitrary"),
        name="att_prompt",
    )(bflat, qt, ka, vt)


def _att_sample_kernel(q_ref, kp_ref, vp_ref, dp_ref, dend_ref, kn_ref, vn_ref, dn_ref, o_ref,
                       m_ref, l_ref, acc_ref):
    t = q_ref.shape[1]
    j = pl.program_id(1)

    @pl.when(j == 0)
    def _():
        m_ref[...] = jnp.full_like(m_ref, NEG_INF)
        l_ref[...] = jnp.zeros_like(l_ref)
        acc_ref[...] = jnp.zeros_like(acc_ref)

    def head(h, k, v, time_minor, bias, mask):
        sl = slice(h * HEAD_DIM, (h + 1) * HEAD_DIM)
        q_h = q_ref[0, :, sl]
        s = (_dot(q_h, k) if time_minor else _dot_nt(q_h, k)) + bias[h:h + 1]
        if mask is not None:
            s = jnp.where(mask, s, NEG_INF)
        p, alpha, m_new, l_new = _online_update(s, m_ref[h], l_ref[h])
        p = p.astype(BF16)
        acc = alpha * acc_ref[:, sl] + (_dot_nt(p, v) if time_minor else _dot(p, v))
        m_ref[h], l_ref[h], acc_ref[:, sl] = m_new, l_new, acc
        return acc / l_new

    bias_past = dend_ref[0] - dp_ref[0]
    for h in range(N_HEADS):
        head(h, kp_ref[0, h].astype(BF16), vp_ref[0, h].astype(BF16), True, bias_past, None)

    @pl.when(j == pl.num_programs(1) - 1)
    def _():
        bias_new = -dn_ref[0]
        qpos = lax.broadcasted_iota(jnp.int32, (t, t), 0)
        kpos = lax.broadcasted_iota(jnp.int32, (t, t), 1)
        for h in range(N_HEADS):
            sl = slice(h * HEAD_DIM, (h + 1) * HEAD_DIM)
            o_ref[0, :, sl] = head(h, kn_ref[0, :, sl], vn_ref[0, :, sl], False, bias_new, kpos <= qpos)


def _att_sample(qb, k_past, v_past, d_past, kb_new, vb_new, d_new):
    n, t, _ = qb.shape
    past = k_past.shape[1]
    tk = min(SAMPLE_TK, past)
    assert past % tk == 0
    d_end = d_past[:, :, past - 1:]
    k_past = jnp.transpose(k_past, (0, 2, 3, 1))
    v_past = jnp.transpose(v_past, (0, 2, 3, 1))
    new = lambda dt: pl.BlockSpec((1, t, WA), lambda b, j: (b, 0, 0))
    cache = pl.BlockSpec((1, N_HEADS, HEAD_DIM, tk), lambda b, j: (b, 0, 0, j))
    return pl.pallas_call(
        _att_sample_kernel,
        grid=(n, past // tk),
        in_specs=[new(BF16), cache, cache,
                  pl.BlockSpec((1, N_HEADS, tk), lambda b, j: (b, 0, j)),
                  pl.BlockSpec((1, N_HEADS, 1), lambda b, j: (b, 0, 0)),
                  new(BF16), new(BF16),
                  pl.BlockSpec((1, N_HEADS, t), lambda b, j: (b, 0, 0))],
        out_specs=pl.BlockSpec((1, t, WA), lambda b, j: (b, 0, 0)),
        out_shape=jax.ShapeDtypeStruct((n, t, WA), F32),
        scratch_shapes=[pltpu.VMEM((N_HEADS, t, 1), F32), pltpu.VMEM((N_HEADS, t, 1), F32),
                        pltpu.VMEM((t, WA), F32)],
        compiler_params=_params("parallel", "arbitrary"),
        name="att_sample",
    )(qb, k_past, v_past, d_past, d_end, kb_new, vb_new, d_new)


def _first_index_of_max(x, axis):
    mx = jnp.max(x, axis=axis, keepdims=True)
    idx = lax.broadcasted_iota(jnp.int32, x.shape, axis)
    first = jnp.min(jnp.where(x == mx, idx, x.shape[axis]), axis=axis, keepdims=True)
    return mx, idx == first


def _route(s, bias):
    m = s.shape[1]
    sb = (s + bias).reshape(N_GROUPS, GROUP_SIZE, m)
    top1, is_top1 = _first_index_of_max(sb, 1)
    top2 = jnp.max(jnp.where(is_top1, NEG_INF, sb), axis=1, keepdims=True)
    grp = (top1 + top2).reshape(N_GROUPS, m)
    gi = lax.broadcasted_iota(jnp.int32, (N_GROUPS, N_GROUPS, m), 0)
    gj = lax.broadcasted_iota(jnp.int32, (N_GROUPS, N_GROUPS, m), 1)
    other, mine = grp[None, :, :], grp[:, None, :]
    beats = (other > mine) | ((other == mine) & (gj < gi))
    g_rank = jnp.sum(beats.astype(jnp.int32), axis=1)
    g_keep = (g_rank < TOPK_GROUPS)[:, None, :]
    cand = jnp.where(g_keep, sb, NEG_INF).reshape(N_EXPERTS, m)
    picks = []
    for _ in range(TOP_K):
        _, pick = _first_index_of_max(cand, 0)
        picks.append(pick)
        cand = jnp.where(pick, NEG_INF, cand)
    w = jnp.concatenate([jnp.sum(jnp.where(pk, s, 0.0), axis=0, keepdims=True) for pk in picks], axis=0)
    return picks, w / jnp.sum(w, axis=0, keepdims=True) * ROUTED_SCALE


HALF_MASK = 0xFFFF0000


def _pack_halves(x):
    c = x.shape[1] // 2
    lo = lax.bitcast_convert_type(x[:, :c].astype(BF16).astype(F32), jnp.uint32) >> jnp.uint32(16)
    hi = lax.bitcast_convert_type(x[:, c:].astype(BF16).astype(F32), jnp.uint32) & jnp.uint32(HALF_MASK)
    return lax.bitcast_convert_type(lo | hi, jnp.int32)


def _unpack_halves(words):
    w = lax.bitcast_convert_type(words, jnp.uint32)
    lo = lax.bitcast_convert_type(w << jnp.uint32(16), F32)
    hi = lax.bitcast_convert_type(w & jnp.uint32(HALF_MASK), F32)
    return lo, hi


def _outproj_kernel(x_ref, lru_ref, att_ref, mod_ref, gatt_ref, wtop_ref, wbot_ref, gpost_ref, gpre_ref,
                    rwh_ref, rwl_ref, rb_ref, cnt_in_ref, before_ref,
                    x1_ref, hf_ref, xw_ref, ids_ref, ranks_ref, gates_ref, cnt_out_ref, carry_ref, *, split):
    nb, tt, d = x_ref.shape
    m = nb * tt
    first = (pl.program_id(0) == 0) & (pl.program_id(1) == 0)

    @pl.when(first)
    def _():
        carry_ref[...] = cnt_in_ref[...]

    if split is not None:
        @pl.when((pl.program_id(0) == split) & (pl.program_id(1) == 0))
        def _():
            carry_ref[...] = jnp.zeros_like(carry_ref)

    mod = mod_ref[...]
    att_n = _rms(att_ref[...], gatt_ref[...]).reshape(m, WA).astype(BF16)
    mix = _dot(lru_ref[...].reshape(m, WL), wtop_ref[...]) + _dot(att_n, wbot_ref[...])
    x1 = x_ref[...] + mod[:, 2:3, :] * _rms(mix, gpost_ref[...]).reshape(nb, tt, d)
    x1_ref[...] = x1
    hf = (_rms(x1, gpre_ref[...]) * (1.0 + mod[:, 4:5, :]) + mod[:, 3:4, :]).reshape(m, d)
    hf_hi = hf.astype(BF16)
    hf_ref[...] = hf_hi.reshape(nb, tt, d)
    hf_lo = (hf - hf_hi.astype(F32)).astype(BF16)
    rwh = rwh_ref[...]
    both = _dot_nt(jnp.concatenate([rwh, rwl_ref[...]], axis=0), hf_hi)
    logits = both[:N_EXPERTS] + both[N_EXPERTS:] + _dot_nt(rwh, hf_lo)
    picks, gates = _route(_sigmoid(logits), rb_ref[...])
    xw_ref[...] = _pack_halves(hf).reshape(nb, tt, d // 2)

    sel = jnp.zeros((N_EXPERTS, m), F32)
    for pk in picks:
        sel = sel + pk.astype(F32)
    prior = _dot(sel.astype(BF16), before_ref[...]) + carry_ref[...]
    expert = lax.broadcasted_iota(jnp.int32, (N_EXPERTS, m), 0).astype(F32)
    take = lambda pk, v: jnp.sum(jnp.where(pk, v, 0.0), axis=0, keepdims=True)
    ids_ref[...] = jnp.concatenate([take(pk, expert) for pk in picks], axis=0).astype(jnp.int32)
    ranks_ref[...] = jnp.concatenate([take(pk, prior) for pk in picks], axis=0).astype(jnp.int32)
    carry_ref[...] += jnp.sum(sel, axis=1, keepdims=True)
    cnt_out_ref[0] = carry_ref[...]
    gates = jnp.concatenate([gates, jnp.zeros((LANES - TOP_K, m), F32)], axis=0)
    gates_ref[...] = gates.T.reshape(nb, tt, LANES)


def _outproj(x, lru_n, att, mod, g_att, w_top, w_bot, g_post, g_pre, rw_hi, rw_lo, r_bias, cnt_in, split=None):
    n, t, d = x.shape
    nb, tt = _seq_blocks(n, t)
    m = nb * tt
    steps_t = t // tt
    n_groups = 1 if split is None else 2
    blk = lambda w: pl.BlockSpec((nb, tt, w), lambda i, j: (i, j, 0))
    const = lambda shape: pl.BlockSpec(shape, lambda i, j: (0,) * len(shape))
    per_tok = pl.BlockSpec((TOP_K, m), lambda i, j: (0, i * steps_t + j))
    group_of = (lambda i: 0) if split is None else (lambda i: jnp.where(i < split, 0, 1))
    before = jnp.asarray(np.triu(np.ones((m, m), np.float32), k=1), BF16)
    return pl.pallas_call(
        functools.partial(_outproj_kernel, split=split),
        grid=(n // nb, steps_t),
        in_specs=[blk(d), blk(WL), blk(WA), pl.BlockSpec((nb, 6, d), lambda i, j: (i, 0, 0)),
                  const((1, WA)), const((WL, d)), const((WA, d)), const((1, d)), const((1, d)),
                  const((N_EXPERTS, d)), const((N_EXPERTS, d)), const((N_EXPERTS, 1)), const((N_EXPERTS, 1)),
                  const((m, m))],
        out_specs=[blk(d), blk(d), blk(d // 2), per_tok, per_tok, blk(LANES),
                   pl.BlockSpec((1, N_EXPERTS, 1), lambda i, j: (group_of(i), 0, 0))],
        out_shape=[jax.ShapeDtypeStruct((n, t, d), F32), jax.ShapeDtypeStruct((n, t, d), BF16),
                   jax.ShapeDtypeStruct((n, t, d // 2), jnp.int32),
                   jax.ShapeDtypeStruct((TOP_K, n * t), jnp.int32), jax.ShapeDtypeStruct((TOP_K, n * t), jnp.int32),
                   jax.ShapeDtypeStruct((n, t, LANES), F32), jax.ShapeDtypeStruct((n_groups, N_EXPERTS, 1), F32)],
        scratch_shapes=[pltpu.VMEM((N_EXPERTS, 1), F32)],
        compiler_params=_params("arbitrary", "arbitrary"),
        name="outproj_router",
    )(x, lru_n, att, mod, g_att, w_top, w_bot, g_post, g_pre, rw_hi, rw_lo, r_bias, cnt_in, before)


def _subcore_ranges(n_items):
    info = plsc.get_sparse_core_info()
    n_workers = info.num_cores * info.num_subcores
    per_worker = n_items // n_workers
    assert per_worker * n_workers == n_items and per_worker % GATHER_ROWS == 0
    return info, plsc.VectorSubcoreMesh(core_axis_name="c", subcore_axis_name="s"), per_worker


def _scatter_rows(sources, pos, n_slots):
    m_tot, c = pos.shape[1], sources[0][0].shape[1]
    assert sum(cnt for _, _, cnt in sources) == m_tot
    assert all(first % GATHER_ROWS == 0 and cnt % GATHER_ROWS == 0 for _, first, cnt in sources)
    info, mesh, per_worker = _subcore_ranges(m_tot)
    pos_flat = pos.reshape(-1)

    @functools.partial(
        pl.kernel, mesh=mesh, out_type=jax.ShapeDtypeStruct((n_slots, c), jnp.int32),
        scratch_types=[pltpu.VMEM((GATHER_ROWS,), jnp.int32), pltpu.VMEM((GATHER_ROWS, c), jnp.int32),
                       pltpu.SemaphoreType.DMA])
    def scatter(*refs):
        src_refs, (pos_hbm, out_hbm, idx_v, rows_v, sem) = refs[:len(sources)], refs[len(sources):]
        worker = lax.axis_index("s") * info.num_cores + lax.axis_index("c")
        base = worker * per_worker

        @pl.loop(0, per_worker // GATHER_ROWS)
        def _(step):
            off = pl.multiple_of(base + step * GATHER_ROWS, GATHER_ROWS)
            token0 = 0
            for src_hbm, (_, first, cnt) in zip(src_refs, sources):
                @pl.when((off >= token0) & (off < token0 + cnt))
                def _(src_hbm=src_hbm, shift=first - token0):
                    pltpu.sync_copy(src_hbm.at[pl.ds(pl.multiple_of(off + shift, GATHER_ROWS), GATHER_ROWS)], rows_v)
                token0 += cnt
            for r in range(TOP_K):
                pltpu.sync_copy(pos_hbm.at[pl.ds(pl.multiple_of(r * m_tot + off, GATHER_ROWS), GATHER_ROWS)], idx_v)
                pltpu.async_copy(rows_v, out_hbm.at[idx_v], sem).wait()

    return scatter(*[a for a, _, _ in sources], pos_flat)


def _swiglu_halves(lo, hi, wg, wu, wd):
    c = lo.shape[1]
    hg = _dot(lo, wg[:c]) + _dot(hi, wg[c:])
    hu = _dot(lo, wu[:c]) + _dot(hi, wu[c:])
    return _dot((_silu(hg) * hu).astype(BF16), wd)


def _expert_kernel(te_ref, valid_ref, x_ref, wg_ref, wu_ref, wd_ref, y_ref, wg_bf, wu_bf, wd_bf):
    i = pl.program_id(0)
    valid = valid_ref[i]

    @pl.when((i == 0) | (te_ref[i] != te_ref[jnp.maximum(i - 1, 0)]))
    def _():
        wg_bf[...] = wg_ref[0].astype(BF16)
        wu_bf[...] = wu_ref[0].astype(BF16)
        wd_bf[...] = wd_ref[0].astype(BF16)

    @pl.when(valid > 0)
    def _():
        w = x_ref[...]
        row = lax.broadcasted_iota(jnp.int32, w.shape, 0)
        lo, hi = _unpack_halves(jnp.where(row < valid, w, 0))
        y_ref[...] = _pack_halves(_swiglu_halves(lo.astype(BF16), hi.astype(BF16), wg_bf[...], wu_bf[...], wd_bf[...]))


def _experts(xs, tile_expert, tile_valid, wg, wu, wd):
    n_slots, c = xs.shape
    n_tiles = n_slots // EXPERT_TILE
    d = 2 * c
    rows = pl.BlockSpec((EXPERT_TILE, c), lambda i, te, tv: (i, 0))
    weight = lambda shape: pl.BlockSpec((1,) + shape, lambda i, te, tv: (te[i], 0, 0))
    return pl.pallas_call(
        _expert_kernel,
        grid_spec=pltpu.PrefetchScalarGridSpec(
            num_scalar_prefetch=2, grid=(n_tiles,),
            in_specs=[rows, weight((d, D_EXPERT)), weight((d, D_EXPERT)), weight((D_EXPERT, d))],
            out_specs=rows,
            scratch_shapes=[pltpu.VMEM((d, D_EXPERT), BF16), pltpu.VMEM((d, D_EXPERT), BF16),
                            pltpu.VMEM((D_EXPERT, d), BF16)]),
        out_shape=jax.ShapeDtypeStruct((n_slots, c), jnp.int32),
        compiler_params=_params("arbitrary"),
        name="moe_experts",
    )(tile_expert, tile_valid, xs, wg, wu, wd)


def _gather_rows(table, idx):
    b, c = idx.shape[0], table.shape[1]
    info, mesh, per_worker = _subcore_ranges(b)

    n_steps = per_worker // GATHER_ROWS
    assert n_steps % 2 == 0

    @functools.partial(
        pl.kernel, mesh=mesh, out_type=jax.ShapeDtypeStruct((b, c), jnp.int32),
        scratch_types=[pltpu.VMEM((2, GATHER_ROWS), jnp.int32), pltpu.VMEM((2, GATHER_ROWS, c), jnp.int32),
                       pltpu.SemaphoreType.DMA((2,))])
    def gather(table_hbm, idx_hbm, out_hbm, idx_v, rows_v, sems):
        worker = lax.axis_index("s") * info.num_cores + lax.axis_index("c")
        base = worker * per_worker

        def chunk(step):
            return pl.ds(pl.multiple_of(base + step * GATHER_ROWS, GATHER_ROWS), GATHER_ROWS)

        def stream(buf):
            return pltpu.make_async_copy(table_hbm.at[idx_v.at[buf]], rows_v.at[buf], sems.at[buf])

        def start(step, buf):
            pltpu.sync_copy(idx_hbm.at[chunk(step)], idx_v.at[buf])
            stream(buf).start()

        def finish(step, buf):
            stream(buf).wait()
            pltpu.sync_copy(rows_v.at[buf], out_hbm.at[chunk(step)])

        start(0, 0)

        @pl.loop(0, n_steps, step=2)
        def _(step):
            start(step + 1, 1)
            finish(step, 0)

            @pl.when(step + 2 < n_steps)
            def _():
                start(step + 2, 0)

            finish(step + 1, 1)

    return gather(table, idx)


def _combine_kernel(rows_ref, gates_ref, hf_ref, x1_ref, mod_ref, sg_ref, su_ref, sd_ref, gpost_ref, y_ref):
    nb, tt, d = hf_ref.shape
    m = nb * tt
    c = d // 2
    x = hf_ref[...].reshape(m, d)
    shared = _swiglu_halves(x[:, :c], x[:, c:], sg_ref[...], su_ref[...], sd_ref[...])
    gates = gates_ref[...].reshape(m, LANES)
    acc_lo = shared[:, :c]
    acc_hi = shared[:, c:]
    for r in range(TOP_K):
        lo, hi = _unpack_halves(rows_ref[r])
        g = gates[:, r:r + 1]
        acc_lo = acc_lo + g * lo
        acc_hi = acc_hi + g * hi
    z = _rms(jnp.concatenate([acc_lo, acc_hi], axis=1), gpost_ref[...]).reshape(nb, tt, d)
    y_ref[...] = x1_ref[...] + mod_ref[...][:, 5:6, :] * z


def _combine(rows, first_tile, gates_t, hf, x1, mod, sg, su, sd, g_post, blocks=None, y_prev=None):
    n, t, d = hf.shape
    nb, tt = _seq_blocks(n, t)
    m = nb * tt
    steps_t = t // tt
    b0, nblocks = (0, n // nb) if blocks is None else blocks
    blk = lambda w: pl.BlockSpec((nb, tt, w), lambda i, j: (b0 + i, j, 0))
    const = lambda shape: pl.BlockSpec(shape, lambda i, j: (0,) * len(shape))
    in_specs = [pl.BlockSpec((TOP_K, m, d // 2), lambda i, j: (0, first_tile + i * steps_t + j, 0)),
                blk(LANES), blk(d), blk(d), pl.BlockSpec((nb, 6, d), lambda i, j: (b0 + i, 0, 0)),
                const((d, D_EXPERT)), const((d, D_EXPERT)), const((D_EXPERT, d)), const((1, d))]
    args = (rows, gates_t, hf, x1, mod, sg, su, sd, g_post)
    kernel_fn, aliases = _combine_kernel, {}
    if y_prev is not None:
        in_specs.append(pl.BlockSpec(memory_space=pl.ANY))
        args += (y_prev,)
        aliases = {len(args) - 1: 0}
        kernel_fn = lambda *refs: _combine_kernel(*refs[:len(args) - 1], refs[-1])
    return pl.pallas_call(
        kernel_fn,
        grid=(nblocks, steps_t),
        in_specs=in_specs,
        out_specs=blk(d),
        out_shape=jax.ShapeDtypeStruct((n, t, d), F32),
        input_output_aliases=aliases,
        compiler_params=_params("parallel", "parallel"),
        name="moe_combine",
    )(*args)


def _slots_kernel(starts_ref, ids_ref, ranks_ref, pos_ref):
    ids = ids_ref[...]

    def add_start(e, pos):
        return pos + jnp.where(ids == e, starts_ref[e], 0)

    pos_ref[...] = lax.fori_loop(0, N_EXPERTS, add_start, ranks_ref[...])


def _slots(starts, ids, ranks):
    k, m_tot = ids.shape
    cols = math.gcd(m_tot, SLOT_COLS)
    assert cols % LANES == 0
    blk = pl.BlockSpec((k, cols), lambda i: (0, i))
    return pl.pallas_call(
        _slots_kernel,
        grid=(m_tot // cols,),
        in_specs=[pl.BlockSpec(memory_space=pltpu.SMEM), blk, blk],
        out_specs=blk,
        out_shape=jax.ShapeDtypeStruct((k, m_tot), jnp.int32),
        compiler_params=_params("arbitrary"),
        name="moe_slots",
    )(starts, ids, ranks)


def _slot_plan(ids, ranks, counts):
    n_pairs = ids.shape[0] * ids.shape[1]
    n_tiles = -(-(n_pairs + N_EXPERTS * (EXPERT_TILE - 1)) // EXPERT_TILE)
    cnt = counts.reshape(N_EXPERTS).astype(jnp.int32)
    padded = (cnt + EXPERT_TILE - 1) // EXPERT_TILE * EXPERT_TILE
    ends = jnp.cumsum(padded)
    starts = ends - padded
    pos = _slots(starts, ids, ranks)
    tile_start = jnp.arange(n_tiles, dtype=jnp.int32) * EXPERT_TILE
    in_expert = (tile_start[:, None] >= starts[None, :]) & (tile_start[:, None] < ends[None, :])
    tile_expert = jnp.sum(jnp.where(in_expert, jnp.arange(N_EXPERTS, dtype=jnp.int32)[None, :], 0), axis=1)
    tile_fill = jnp.sum(jnp.where(in_expert, (starts + cnt)[None, :] - tile_start[:, None], 0), axis=1)
    tile_valid = jnp.clip(tile_fill, 0, EXPERT_TILE).astype(jnp.int32)
    return pos, tile_expert.astype(jnp.int32), tile_valid, n_tiles * EXPERT_TILE


def _block_diag(w):
    g, bw, _ = w.shape
    eye = jnp.eye(g, dtype=w.dtype)
    return (eye[:, None, :, None] * w[:, :, None, :]).reshape(g * bw, g * bw)


def _prep_weights(p):
    d_main = 2 * WL + 3 * WA
    w_in = p["w_in"]
    rw_t = p["router_w"].T
    rw_hi = rw_t.astype(BF16)
    row = lambda v: v.reshape(1, -1)
    return dict(
        w_mod=p["w_mod"], b_mod=p["b_mod"],
        g_pre_mix=row(p["g_pre_mix"]), g_post_mix=row(p["g_post_mix"]),
        g_pre_ffn=row(p["g_pre_ffn"]), g_post_ffn=row(p["g_post_ffn"]),
        w_main=w_in[:, :d_main].astype(BF16),
        w_f=jnp.pad(w_in[:, d_main:], ((0, 0), (0, LANES - N_HEADS))).astype(BF16),
        b_f=jnp.pad(p["b_f"], (0, LANES - N_HEADS)).reshape(1, LANES),
        conv_w=p["conv_w"], conv_b=row(p["conv_b"]),
        wr_bd=_block_diag(p["w_r"]).astype(BF16), b_r=row(p["b_r"]),
        wi_bd=_block_diag(p["w_i"]).astype(BF16), b_i=row(p["b_i"]),
        lam=row(p["lru_lambda"]), g_lru=row(p["g_lru_out"]), g_att=row(p["g_att_out"]),
        w_top=p["w_out"][:WL].astype(BF16), w_bot=p["w_out"][WL:].astype(BF16),
        rw_hi=rw_hi, rw_lo=(rw_t - rw_hi.astype(F32)).astype(BF16),
        r_bias=p["router_bias"].reshape(N_EXPERTS, 1),
        wg=p["w_gate"], wu=p["w_up"], wd=p["w_down"],
        sg=p["ws_gate"].astype(BF16), su=p["ws_up"].astype(BF16), sd=p["ws_down"].astype(BF16),
    )


def _mixers(x, mod, conv0, h0, past, w, cnt_in, split=None):
    n, t, _ = x.shape
    proj_args = (x, mod, w["g_pre_mix"], w["w_main"], w["w_f"], w["b_f"])
    if past is None:
        xl, gy, k, v, lf_t, qt, ka, vt, bpre = _inproj_prompt(*proj_args)
        lf = jnp.transpose(lf_t, (0, 2, 1))
        att = _att_prompt(qt, ka, vt, bpre)
    else:
        xl, gy, qb, kb, vb, k, v, lf = _inproj(*proj_args)
        k_past, v_past, lf_past = past
        plen = k_past.shape[1]
        by_head = lambda a: jnp.transpose(a, (0, 2, 1)).reshape(n * N_HEADS, a.shape[1])
        d_new = _cumsum_rows(by_head(lf), t).reshape(n, N_HEADS, t)
        d_past = _cumsum_rows(by_head(lf_past), min(CUMSUM_COLS, plen)).reshape(n, N_HEADS, plen)
        att = _att_sample(qb, k_past, v_past, d_past, kb, vb, d_new)
    lru_n, conv_new, h_new = _lru(xl, gy, conv0, h0.reshape(n, 1, WL), w["conv_w"], w["conv_b"],
                                  w["wr_bd"], w["b_r"], w["wi_bd"], w["b_i"], w["lam"], w["g_lru"])
    routed = _outproj(x, lru_n, att, mod, w["g_att"], w["w_top"], w["w_bot"], w["g_post_mix"],
                      w["g_pre_ffn"], w["rw_hi"], w["rw_lo"], w["r_bias"], cnt_in, split)
    state = (k.reshape(n, t, N_HEADS, HEAD_DIM), v.reshape(n, t, N_HEADS, HEAD_DIM), lf,
             conv_new, h_new.reshape(n, WL))
    return routed, state


def _layer(xp, xs, mod_p, mod_s, conv_s, h_s, past_s, w):
    n_p = xp.shape[0]
    conv0 = jnp.zeros((n_p, CONV_W - 1, WL), F32)
    h0 = jnp.zeros((n_p, WL), F32)
    zero_cnt = jnp.zeros((N_EXPERTS, 1), F32)
    t_p = xp.shape[1]
    split = 1 if n_p > 1 else None
    (x1_p, hf_p, xw_p, ids_p, rk_p, g_p, cnt_p), st_p = _mixers(xp, mod_p, conv0, h0, None, w, zero_cnt, split)
    cnt_tail = cnt_p[1] if n_p > 1 else zero_cnt
    (x1_s, hf_s, xw_s, ids_s, rk_s, g_s, cnt_s), st_s = _mixers(xs, mod_s, conv_s, h_s, past_s, w, cnt_tail)

    half = xw_p.shape[-1]
    xw_p = xw_p.reshape(-1, half)

    def routed(sources, ids, ranks, counts):
        pos, tile_expert, tile_valid, n_slots = _slot_plan(ids, ranks, counts)
        ys = _experts(_scatter_rows(sources, pos, n_slots), tile_expert, tile_valid, w["wg"], w["wu"], w["wd"])
        return _gather_rows(ys, pos.reshape(-1)).reshape(TOP_K, ids.shape[1], half)

    rows_a = routed([(xw_p, 0, t_p)], ids_p[:, :t_p], rk_p[:, :t_p], cnt_p[0])
    xw_s = xw_s.reshape(-1, half)
    sources_b = ([(xw_p, t_p, (n_p - 1) * t_p)] if n_p > 1 else []) + [(xw_s, 0, xw_s.shape[0])]
    rows_b = routed(sources_b, jnp.concatenate([ids_p[:, t_p:], ids_s], axis=1),
                    jnp.concatenate([rk_p[:, t_p:], rk_s], axis=1), cnt_s[0])
    shared = (w["sg"], w["su"], w["sd"], w["g_post_ffn"])
    yp = _combine(rows_a, 0, g_p, hf_p, x1_p, mod_p, *shared, blocks=(0, 1))
    if n_p > 1:
        yp = _combine(rows_b, 0, g_p, hf_p, x1_p, mod_p, *shared, blocks=(1, n_p - 1), y_prev=yp)
    ysmp = _combine(rows_b, (n_p - 1) * t_p // ROW_TILE, g_s, hf_s, x1_s, mod_s, *shared)
    return yp, ysmp, st_p, st_s


def kernel(x_prompt, x_sample, c_prompt, c_sample, cache_k, cache_v, cache_logf, state_conv, state_lru, w_mod, b_mod, g_pre_mix, g_post_mix, g_pre_ffn, g_post_ffn, w_in, conv_w, conv_b, w_r, b_r, w_i, b_i, lru_lambda, b_f, g_lru_out, g_att_out, w_out, router_w, router_bias, w_gate, w_up, w_down, ws_gate, ws_up, ws_down):
    names = ("w_mod", "b_mod", "g_pre_mix", "g_post_mix", "g_pre_ffn", "g_post_ffn", "w_in", "conv_w", "conv_b",
             "w_r", "b_r", "w_i", "b_i", "lru_lambda", "b_f", "g_lru_out", "g_att_out", "w_out", "router_w",
             "router_bias", "w_gate", "w_up", "w_down", "ws_gate", "ws_up", "ws_down")
    stacked = (w_mod, b_mod, g_pre_mix, g_post_mix, g_pre_ffn, g_post_ffn, w_in, conv_w, conv_b, w_r, b_r, w_i, b_i,
               lru_lambda, b_f, g_lru_out, g_att_out, w_out, router_w, router_bias, w_gate, w_up, w_down,
               ws_gate, ws_up, ws_down)
    depth = w_mod.shape[0]
    n_p, n_s = x_prompt.shape[0], x_sample.shape[0]
    yp, ys = x_prompt, x_sample
    st_p, st_s = [], []
    for l in range(depth):
        w = _prep_weights({k: v[l] for k, v in zip(names, stacked)})
        mod = _modulation(jnp.concatenate([c_prompt, c_sample], axis=0), w["w_mod"], w["b_mod"])
        mod = mod.reshape(n_p + n_s, 6, D_MODEL)
        yp, ys, sp, ss = _layer(yp, ys, mod[:n_p], mod[n_p:], state_conv[l], state_lru[l],
                                (cache_k[l], cache_v[l], cache_logf[l]), w)
        st_p.append(sp)
        st_s.append(ss)
    stack = lambda sts, i: jnp.stack([s[i] for s in sts])
    return (yp, ys) + tuple(stack(st_p, i) for i in range(5)) + tuple(stack(st_s, i) for i in range(5))
```

```python
import functools
import math

import jax
import jax.numpy as jnp
import numpy as np
from jax import lax
from jax.experimental import pallas as pl
from jax.experimental.pallas import tpu as pltpu
from jax.experimental.pallas import tpu_sc as plsc

F32 = jnp.float32
BF16 = jnp.bfloat16

D_MODEL = 1024
WL = 512
WA = 512
N_HEADS = 8
HEAD_DIM = 64
N_PAIRS = N_HEADS // 2
PAIR_W = 2 * HEAD_DIM
LANES = 128
CONV_W = 4
LRU_BLOCKS = 8
LRU_C = 8.0
N_EXPERTS = 64
N_GROUPS = 8
GROUP_SIZE = N_EXPERTS // N_GROUPS
TOPK_GROUPS = 4
TOP_K = 8
D_EXPERT = 256
ROUTED_SCALE = 2.5
EPS = 1e-6
NEG_INF = float("-inf")
LOG2E = 1.4426950408889634

ROW_TILE = 512
ATT_TQ = 512
ATT_TK = 512
ATT_CHUNK = 64
ATT_HEADS = 4
SAMPLE_TK = 4096
CUMSUM_COLS = 1024
EXPERT_TILE = 1024
SLOT_COLS = 2048
GATHER_ROWS = 64
VMEM_LIMIT = 56 * 1024 * 1024


def _params(*sem):
    return pltpu.CompilerParams(dimension_semantics=sem, vmem_limit_bytes=VMEM_LIMIT)


def _dot(a, b):
    return jnp.dot(a, b, preferred_element_type=F32)


def _dot_nt(a, b):
    return lax.dot_general(a, b, (((1,), (1,)), ((), ())), preferred_element_type=F32)


def _split3(x):
    hi = x.astype(BF16)
    r1 = x - hi.astype(F32)
    mid = r1.astype(BF16)
    lo = (r1 - mid.astype(F32)).astype(BF16)
    return hi, mid, lo


def _rms(x, g):
    return x * lax.rsqrt(jnp.mean(x * x, axis=-1, keepdims=True) + EPS) * g


def _sigmoid(x):
    return 1.0 / (1.0 + jnp.exp(-x))


def _silu(x):
    return x * _sigmoid(x)


def _gelu_tanh(x):
    return 0.5 * x * (1.0 + jnp.tanh(0.7978845608028654 * (x + 0.044715 * (x * x * x))))


def _log_sigmoid(x):
    return jnp.minimum(x, 0.0) - jnp.log1p(jnp.exp(-jnp.abs(x)))


def _seq_blocks(n, t):
    if t >= ROW_TILE:
        assert t % ROW_TILE == 0
        return 1, ROW_TILE
    nb = ROW_TILE // t
    assert nb * t == ROW_TILE and n % nb == 0
    return nb, t


def _mod_kernel(c_ref, w_ref, b_ref, o_ref):
    c = _silu(c_ref[...])
    c_hi = c.astype(BF16)
    c_lo = (c - c_hi.astype(F32)).astype(BF16)
    w = w_ref[...]
    w_hi = w.astype(BF16)
    w_lo = (w - w_hi.astype(F32)).astype(BF16)
    o_ref[...] = _dot(c_hi, w_hi) + _dot(c_lo, w_hi) + _dot(c_hi, w_lo) + b_ref[...]


def _modulation(c, w_mod, b_mod):
    rows = c.shape[0]
    n = -(-rows // 8) * 8
    c = jnp.pad(c, ((0, n - rows), (0, 0)))
    d6 = w_mod.shape[1]
    return pl.pallas_call(
        _mod_kernel,
        grid=(d6 // D_MODEL,),
        in_specs=[pl.BlockSpec((n, D_MODEL), lambda j: (0, 0)),
                  pl.BlockSpec((D_MODEL, D_MODEL), lambda j: (0, j)),
                  pl.BlockSpec((1, D_MODEL), lambda j: (0, j))],
        out_specs=pl.BlockSpec((n, D_MODEL), lambda j: (0, j)),
        out_shape=jax.ShapeDtypeStruct((n, d6), F32),
        compiler_params=_params("arbitrary"),
        name="modulation",
    )(c, w_mod, b_mod.reshape(1, d6))[:rows]


def _inproj_kernel(x_ref, mod_ref, g_ref, w_ref, wf_ref, bf_ref,
                   xl_ref, gy_ref, qb_ref, kb_ref, vb_ref, k_ref, v_ref, lf_ref):
    nb, tt, d = x_ref.shape
    x = x_ref[...]
    mod = mod_ref[...]
    hn = _rms(x, g_ref[...]) * (1.0 + mod[:, 1:2, :]) + mod[:, 0:1, :]
    hb = hn.reshape(nb * tt, d).astype(BF16)

    def proj(col):
        return _dot(hb, w_ref[:, col * WL:(col + 1) * WL]).reshape(nb, tt, WL)

    xl_ref[...] = proj(0)
    gy_ref[...] = _gelu_tanh(proj(1)).astype(BF16)
    qb_ref[...] = (proj(2) * (HEAD_DIM ** -0.5)).astype(BF16)
    k = proj(3)
    k_ref[...] = k
    kb_ref[...] = k.astype(BF16)
    v = proj(4)
    v_ref[...] = v
    vb_ref[...] = v.astype(BF16)
    fl = _dot(hb, wf_ref[...]) + bf_ref[...]
    lf_ref[...] = _log_sigmoid(fl).reshape(nb, tt, LANES)[:, :, :N_HEADS]


def _inproj(x, mod, g_pre, w_main, w_f, b_f):
    n, t, d = x.shape
    nb, tt = _seq_blocks(n, t)
    blk = lambda w: pl.BlockSpec((nb, tt, w), lambda i, j: (i, j, 0))
    const = lambda shape: pl.BlockSpec(shape, lambda i, j: (0,) * len(shape))
    f32 = lambda w: jax.ShapeDtypeStruct((n, t, w), F32)
    b16 = lambda w: jax.ShapeDtypeStruct((n, t, w), BF16)
    return pl.pallas_call(
        _inproj_kernel,
        grid=(n // nb, t // tt),
        in_specs=[blk(d),
                  pl.BlockSpec((nb, 6, d), lambda i, j: (i, 0, 0)),
                  const((1, d)), const(w_main.shape), const(w_f.shape), const((1, LANES))],
        out_specs=[blk(WL), blk(WL), blk(WA), blk(WA), blk(WA), blk(WA), blk(WA), blk(N_HEADS)],
        out_shape=[f32(WL), b16(WL), b16(WA), b16(WA), b16(WA), f32(WA), f32(WA), f32(N_HEADS)],
        compiler_params=_params("parallel", "arbitrary"),
        name="inproj",
    )(x, mod, g_pre, w_main, w_f, b_f)


def _aug_lane(h):
    return HEAD_DIM if h % 2 == 0 else 0


def _inproj_prompt_kernel(x_ref, mod_ref, g_ref, w_ref, wf_ref, bf_ref, place_ref,
                          conv0_ref, h0_ref, cw_ref, cb_ref, wr_ref, br_ref, wi_ref, bi_ref, lam_ref, glru_ref,
                          lru_ref, conv_ref, hlast_ref, k_ref, v_ref, lf_ref, qt_ref, ka_ref, vt_ref, bpre_ref,
                          carry_ref, tail_ref, hcarry_ref):
    _, tt, d = x_ref.shape

    @pl.when(pl.program_id(1) == 0)
    def _():
        carry_ref[...] = jnp.zeros_like(carry_ref)

    mod = mod_ref[0]
    hb = (_rms(x_ref[0], g_ref[...]) * (1.0 + mod[1:2, :]) + mod[0:1, :]).astype(BF16)

    def proj(col):
        return _dot(hb, w_ref[:, col * WL:(col + 1) * WL])

    _lru_block(proj(0)[None], _gelu_tanh(proj(1))[None], conv0_ref, h0_ref, cw_ref, cb_ref, wr_ref, br_ref,
               wi_ref, bi_ref, lam_ref, glru_ref, lru_ref, conv_ref, hlast_ref, tail_ref, hcarry_ref)
    q = proj(2) * (HEAD_DIM ** -0.5 * LOG2E)
    k = proj(3)
    k_ref[0] = k
    v = proj(4)
    v_ref[0] = v
    fl = _dot(hb, wf_ref[...]) + bf_ref[...]
    lane = lax.broadcasted_iota(jnp.int32, (tt, LANES), 1)
    lf = jnp.where(lane < N_HEADS, _log_sigmoid(fl), 0.0)
    lf_ref[0] = lf.T[:N_HEADS, :]

    row = lax.broadcasted_iota(jnp.int32, (tt, tt), 0)
    col = lax.broadcasted_iota(jnp.int32, (tt, tt), 1)
    tril = (col <= row).astype(BF16)
    hi, mid, lo = _split3(lf)
    e = _dot(tril, hi) + _dot(tril, mid) + _dot(tril, lo)
    bpre_ref[0, 0] = carry_ref[...]
    carry_ref[...] += e[tt - 1:tt, :]
    e_hi, e_mid, e_lo = _split3(e * (-LOG2E))
    aug_k = _dot(jnp.concatenate([e_hi, e_mid, e_lo], axis=1), place_ref[...])

    for h in range(N_HEADS):
        pair = slice((h // 2) * PAIR_W, (h // 2 + 1) * PAIR_W)
        dims = (lane < HEAD_DIM) if h % 2 == 0 else (lane >= HEAD_DIM)
        a0 = _aug_lane(h)
        ones3 = ((lane >= a0) & (lane < a0 + 3)).astype(F32)
        qt_ref[0, h] = (jnp.where(dims, q[:, pair], 0.0) + ones3).T.astype(BF16)
        ka_ref[0, h] = (jnp.where(dims, k[:, pair], 0.0) + aug_k[:, h * LANES:(h + 1) * LANES]).astype(BF16)
        one1 = (lane == a0).astype(F32)
        vt_ref[0, h] = (jnp.where(dims, v[:, pair], 0.0) + one1).T.astype(BF16)


def _placement():
    pl_mat = np.zeros((3 * LANES, N_HEADS * LANES), np.float32)
    for p in range(3):
        for h in range(N_HEADS):
            pl_mat[p * LANES + h, h * LANES + _aug_lane(h) + p] = 1.0
    return jnp.asarray(pl_mat, BF16)


def _inproj_prompt(x, mod, g_pre, w_main, w_f, b_f, conv0, h0, lru_weights):
    n, t, d = x.shape
    tt = ROW_TILE
    assert t % tt == 0
    nblk = t // tt
    blk = lambda w: pl.BlockSpec((1, tt, w), lambda i, j: (i, j, 0))
    const = lambda shape: pl.BlockSpec(shape, lambda i, j: (0,) * len(shape))
    per_seq = lambda rows: pl.BlockSpec((1, rows, WL), lambda i, j: (i, 0, 0))
    f32 = lambda w: jax.ShapeDtypeStruct((n, t, w), F32)
    place = _placement()
    return pl.pallas_call(
        _inproj_prompt_kernel,
        grid=(n, nblk),
        in_specs=[blk(d), pl.BlockSpec((1, 6, d), lambda i, j: (i, 0, 0)),
                  const((1, d)), const(w_main.shape), const(w_f.shape), const((1, LANES)), const(place.shape),
                  per_seq(CONV_W - 1), per_seq(1)] + [const(a.shape) for a in lru_weights],
        out_specs=[blk(WL), per_seq(CONV_W - 1), per_seq(1),
                   blk(WA), blk(WA), pl.BlockSpec((1, N_HEADS, tt), lambda i, j: (i, 0, j)),
                   pl.BlockSpec((1, N_HEADS, LANES, tt), lambda i, j: (i, 0, 0, j)),
                   pl.BlockSpec((1, N_HEADS, tt, LANES), lambda i, j: (i, 0, j, 0)),
                   pl.BlockSpec((1, N_HEADS, LANES, tt), lambda i, j: (i, 0, 0, j)),
                   pl.BlockSpec((1, 1, 1, LANES), lambda i, j: (i, j, 0, 0))],
        out_shape=[jax.ShapeDtypeStruct((n, t, WL), BF16), jax.ShapeDtypeStruct((n, CONV_W - 1, WL), F32),
                   jax.ShapeDtypeStruct((n, 1, WL), F32), f32(WA), f32(WA),
                   jax.ShapeDtypeStruct((n, N_HEADS, t), F32),
                   jax.ShapeDtypeStruct((n, N_HEADS, LANES, t), BF16),
                   jax.ShapeDtypeStruct((n, N_HEADS, t, LANES), BF16),
                   jax.ShapeDtypeStruct((n, N_HEADS, LANES, t), BF16),
                   jax.ShapeDtypeStruct((n, nblk, 1, LANES), F32)],
        scratch_shapes=[pltpu.VMEM((1, LANES), F32), pltpu.VMEM((1, 8, WL), F32), pltpu.VMEM((1, 1, WL), F32)],
        compiler_params=_params("parallel", "arbitrary"),
        name="inproj_prompt",
    )(x, mod, g_pre, w_main, w_f, b_f, place, conv0, h0, *lru_weights)


def _expm1_neg(x):
    poly = x * (1.0 + x * (0.5 + x * (1.0 / 6.0 + x * (1.0 / 24.0 + x * (1.0 / 120.0)))))
    return jnp.where(x > -0.1, poly, jnp.exp(x) - 1.0)


def _lru_kernel(xl_ref, gy_ref, *refs):
    _lru_block(xl_ref[...], gy_ref[...].astype(F32), *refs)


def _lru_block(xl, gy, conv0_ref, h0_ref, cw_ref, cb_ref, wr_ref, br_ref, wi_ref, bi_ref,
               lam_ref, g_ref, out_ref, conv_ref, hlast_ref, tail_ref, carry_ref):
    nb, tt, w = xl.shape
    j = pl.program_id(1)

    @pl.when(j == 0)
    def _():
        tail_ref[:, 8 - (CONV_W - 1):, :] = conv0_ref[...]
        carry_ref[...] = h0_ref[...]

    xpad = jnp.concatenate([tail_ref[...], xl], axis=1)
    cw = cw_ref[...]
    xc = jnp.zeros_like(xl) + cb_ref[...]
    for k in range(CONV_W):
        off = 8 - (CONV_W - 1) + k
        xc = xc + xpad[:, off:off + tt, :] * cw[k:k + 1, :]
    conv_ref[...] = xpad[:, tt + 8 - (CONV_W - 1):, :]
    tail_ref[...] = xpad[:, tt:, :]

    m = nb * tt
    xf = xc.reshape(m, w)
    xb = xf.astype(BF16)
    r = _sigmoid(_dot(xb, wr_ref[...]) + br_ref[...])
    gi = _sigmoid(_dot(xb, wi_ref[...]) + bi_ref[...])
    lam = lam_ref[...]
    softplus = jnp.maximum(-lam, 0.0) + jnp.log1p(jnp.exp(-jnp.abs(lam)))
    log_a = (-LRU_C) * r * softplus
    a = jnp.exp(log_a)
    b = jnp.sqrt(-_expm1_neg(2.0 * log_a)) * (gi * xf)

    groups = m // 8
    a = a.reshape(groups, 8, w)
    b = b.reshape(groups, 8, w)
    sub = lax.broadcasted_iota(jnp.int32, (groups, 8, w), 1)
    for d in (1, 2, 4):
        keep = sub >= d
        a_prev = jnp.where(keep, pltpu.roll(a, d, 1), 1.0)
        b_prev = jnp.where(keep, pltpu.roll(b, d, 1), 0.0)
        b = a * b_prev + b
        a = a * a_prev
    carry = carry_ref[...]
    groups_per_seq = tt // 8
    rows = []
    for g in range(groups):
        if g % groups_per_seq == 0:
            prev = carry[g // groups_per_seq]
        h_g = a[g] * prev + b[g]
        prev = h_g[7:8]
        rows.append(h_g)
    h = jnp.concatenate(rows, axis=0).reshape(nb, tt, w)
    h_last = h[:, tt - 1:tt, :]
    carry_ref[...] = h_last
    hlast_ref[...] = h_last
    out_ref[...] = _rms(h * gy, g_ref[...]).astype(BF16)


def _lru(xl, gy, conv0, h0, conv_w, conv_b, wr_bd, b_r, wi_bd, b_i, lam, g_lru):
    n, t, w = xl.shape
    nb, tt = _seq_blocks(n, t)
    blk = pl.BlockSpec((nb, tt, w), lambda i, j: (i, j, 0))
    per_seq = lambda rows: pl.BlockSpec((nb, rows, w), lambda i, j: (i, 0, 0))
    const = lambda shape: pl.BlockSpec(shape, lambda i, j: (0,) * len(shape))
    row = const((1, w))
    return pl.pallas_call(
        _lru_kernel,
        grid=(n // nb, t // tt),
        in_specs=[blk, blk, per_seq(CONV_W - 1), per_seq(1),
                  const((CONV_W, w)), row, const((w, w)), row, const((w, w)), row, row, row],
        out_specs=[blk, per_seq(CONV_W - 1), per_seq(1)],
        out_shape=[jax.ShapeDtypeStruct((n, t, w), BF16),
                   jax.ShapeDtypeStruct((n, CONV_W - 1, w), F32),
                   jax.ShapeDtypeStruct((n, 1, w), F32)],
        scratch_shapes=[pltpu.VMEM((nb, 8, w), F32), pltpu.VMEM((nb, 1, w), F32)],
        compiler_params=_params("parallel", "arbitrary"),
        name="rglru",
    )(xl, gy, conv0, h0, conv_w, conv_b, wr_bd, b_r, wi_bd, b_i, lam, g_lru)


def _cumsum_rows_kernel(x_ref, upper_ref, o_ref, carry_ref):
    tb = x_ref.shape[1]

    @pl.when(pl.program_id(0) == 0)
    def _():
        carry_ref[...] = jnp.zeros_like(carry_ref)

    upper = upper_ref[...]
    hi, mid, lo = _split3(x_ref[...])
    d = _dot(hi, upper) + _dot(mid, upper) + _dot(lo, upper) + carry_ref[...]
    carry_ref[...] = d[:, tb - 1:tb]
    o_ref[...] = d


def _cumsum_rows(x, tb):
    rows, t = x.shape
    upper = jnp.asarray(np.triu(np.ones((tb, tb), np.float32)), BF16)
    return pl.pallas_call(
        _cumsum_rows_kernel,
        grid=(t // tb,),
        in_specs=[pl.BlockSpec((rows, tb), lambda j: (0, j)), pl.BlockSpec((tb, tb), lambda j: (0, 0))],
        out_specs=pl.BlockSpec((rows, tb), lambda j: (0, j)),
        out_shape=jax.ShapeDtypeStruct((rows, t), F32),
        scratch_shapes=[pltpu.VMEM((rows, 1), F32)],
        compiler_params=_params("arbitrary"),
        name="logf_cumsum",
    )(x, upper)


def _online_update(s, m_prev, l_prev):
    m_new = jnp.maximum(m_prev, jnp.max(s, axis=1, keepdims=True))
    alpha = jnp.exp(m_prev - m_new)
    p = jnp.exp(s - m_new)
    l_new = alpha * l_prev + jnp.sum(p, axis=1, keepdims=True)
    return p, alpha, m_new, l_new


def _att_prompt_kernel(bpre_ref, qt_ref, ka_ref, vt_ref, o_ref, s_ref, p_ref, acc_ref):
    tq = qt_ref.shape[2]
    tk = ATT_TK
    nblk = ka_ref.shape[1] // tk
    b, hp, i = pl.program_id(0), pl.program_id(1), pl.program_id(2)
    q0 = i * tq
    jd = q0 // tk
    kpos = lax.broadcasted_iota(jnp.int32, (tk, tq), 0)
    qpos = lax.broadcasted_iota(jnp.int32, (tk, tq), 1)
    rows = lax.broadcasted_iota(jnp.int32, (LANES, tq), 0)
    nh = qt_ref.shape[0]
    heads = range(nh)
    base = [((b * (N_HEADS // nh) + hp) * nh + hh) * nblk for hh in heads]

    def scores(j, masked):
        start = pl.multiple_of(j * tk, tk)
        col_max = []
        for hh in heads:
            s = _dot(ka_ref[hh, pl.ds(start, tk), :], qt_ref[hh])
            if masked:
                s = jnp.where(kpos + start <= qpos + q0, s, NEG_INF)
            s_ref[hh] = s
            col_max.append(jnp.max(s, axis=0, keepdims=True))
        return tuple(col_max)

    def softmax_pv(j, col_max, m):
        start = pl.multiple_of(j * tk, tk)
        m_out = []
        for hh in heads:
            c = (bpre_ref[base[hh] + jd] - bpre_ref[base[hh] + j]) * LOG2E
            m_new = jnp.maximum(m[hh], col_max[hh] + c)
            alpha = jnp.exp2(m[hh] - m_new)
            shift = m_new - c
            for ch in range(tk // ATT_CHUNK):
                sl = slice(ch * ATT_CHUNK, (ch + 1) * ATT_CHUNK)
                p_ref[hh, sl, :] = jnp.exp2(s_ref[hh, sl, :] - shift).astype(BF16)
            m_out.append((m_new, alpha))
        return tuple(m_out), start

    def accumulate(m_alpha, start):
        for hh in heads:
            pv = _dot(vt_ref[hh, :, pl.ds(start, tk)], p_ref[hh])
            acc_ref[hh] = m_alpha[hh][1] * acc_ref[hh] + pv
        return tuple(ma[0] for ma in m_alpha)

    def step(j, next_masked, carry):
        col_max, m = carry
        m_alpha, start = softmax_pv(j, col_max, m)
        col_max_next = scores(j + 1, next_masked)
        return col_max_next, accumulate(m_alpha, start)

    acc_ref[...] = jnp.zeros_like(acc_ref)
    neg = jnp.full((1, tq), NEG_INF, F32)
    n_masked = max(1, tq // tk)
    carry = (scores(0, True), (neg,) * nh)
    carry = lax.fori_loop(0, jd - 1, lambda j, cr: step(j, False, cr), carry)
    carry = lax.cond(jd > 0, lambda cr: step(jd - 1, True, cr), lambda cr: cr, carry)
    for extra in range(n_masked - 1):
        carry = step(jd + extra, True, carry)
    col_max, m = carry
    m_alpha, start = softmax_pv(jd + n_masked - 1, col_max, m)
    accumulate(m_alpha, start)

    for pair in range(nh // 2):
        acc_a, acc_b = acc_ref[2 * pair], acc_ref[2 * pair + 1]
        out_a = acc_a / acc_a[_aug_lane(0):_aug_lane(0) + 1, :]
        out_b = acc_b / acc_b[_aug_lane(1):_aug_lane(1) + 1, :]
        o_ref[0, :, pair * PAIR_W:(pair + 1) * PAIR_W] = jnp.where(rows < HEAD_DIM, out_a, out_b).T


def _att_prompt(qt, ka, vt, bpre):
    n, _, _, t = qt.shape
    assert t % ATT_TQ == 0 and (ATT_TK % ATT_TQ == 0 or ATT_TQ % ATT_TK == 0) and ATT_TK == ROW_TILE
    nh = ATT_HEADS
    grouped = lambda q: (N_HEADS // nh, nh) + q.shape[2:]
    qt, ka, vt = (a.reshape((n,) + grouped(a)) for a in (qt, ka, vt))
    bflat = jnp.transpose(bpre[:, :, 0, :N_HEADS], (0, 2, 1)).reshape(-1)
    return pl.pallas_call(
        _att_prompt_kernel,
        grid=(n, N_HEADS // nh, t // ATT_TQ),
        in_specs=[pl.BlockSpec(memory_space=pltpu.SMEM),
                  pl.BlockSpec((None, None, nh, LANES, ATT_TQ), lambda b, h, i: (b, h, 0, 0, i)),
                  pl.BlockSpec((None, None, nh, t, LANES), lambda b, h, i: (b, h, 0, 0, 0),
                               pipeline_mode=pl.Buffered(1)),
                  pl.BlockSpec((None, None, nh, LANES, t), lambda b, h, i: (b, h, 0, 0, 0),
                               pipeline_mode=pl.Buffered(1))],
        out_specs=pl.BlockSpec((1, ATT_TQ, nh * HEAD_DIM), lambda b, h, i: (b, i, h)),
        out_shape=jax.ShapeDtypeStruct((n, t, WA), F32),
        scratch_shapes=[pltpu.VMEM((nh, ATT_TK, ATT_TQ), F32), pltpu.VMEM((nh, ATT_TK, ATT_TQ), BF16),
                        pltpu.VMEM((nh, LANES, ATT_TQ), F32)],
        compiler_params=_params("parallel", "parallel", "arbitrary"),
        name="att_prompt",
    )(bflat, qt, ka, vt)


def _att_sample_kernel(q_ref, kp_ref, vp_ref, dp_ref, dend_ref, kn_ref, vn_ref, dn_ref, o_ref,
                       m_ref, l_ref, acc_ref):
    t = q_ref.shape[1]
    j = pl.program_id(1)

    @pl.when(j == 0)
    def _():
        m_ref[...] = jnp.full_like(m_ref, NEG_INF)
        l_ref[...] = jnp.zeros_like(l_ref)
        acc_ref[...] = jnp.zeros_like(acc_ref)

    def head(h, k, v, time_minor, bias, mask):
        sl = slice(h * HEAD_DIM, (h + 1) * HEAD_DIM)
        q_h = q_ref[0, :, sl]
        s = (_dot(q_h, k) if time_minor else _dot_nt(q_h, k)) + bias[h:h + 1]
        if mask is not None:
            s = jnp.where(mask, s, NEG_INF)
        p, alpha, m_new, l_new = _online_update(s, m_ref[h], l_ref[h])
        p = p.astype(BF16)
        acc = alpha * acc_ref[:, sl] + (_dot_nt(p, v) if time_minor else _dot(p, v))
        m_ref[h], l_ref[h], acc_ref[:, sl] = m_new, l_new, acc
        return acc / l_new

    bias_past = dend_ref[0] - dp_ref[0]
    for h in range(N_HEADS):
        head(h, kp_ref[0, h].astype(BF16), vp_ref[0, h].astype(BF16), True, bias_past, None)

    @pl.when(j == pl.num_programs(1) - 1)
    def _():
        bias_new = -dn_ref[0]
        qpos = lax.broadcasted_iota(jnp.int32, (t, t), 0)
        kpos = lax.broadcasted_iota(jnp.int32, (t, t), 1)
        for h in range(N_HEADS):
            sl = slice(h * HEAD_DIM, (h + 1) * HEAD_DIM)
            o_ref[0, :, sl] = head(h, kn_ref[0, :, sl], vn_ref[0, :, sl], False, bias_new, kpos <= qpos)


def _att_sample(qb, k_past, v_past, d_past, kb_new, vb_new, d_new):
    n, t, _ = qb.shape
    past = k_past.shape[1]
    tk = min(SAMPLE_TK, past)
    assert past % tk == 0
    d_end = d_past[:, :, past - 1:]
    k_past = jnp.transpose(k_past, (0, 2, 3, 1))
    v_past = jnp.transpose(v_past, (0, 2, 3, 1))
    new = lambda dt: pl.BlockSpec((1, t, WA), lambda b, j: (b, 0, 0))
    cache = pl.BlockSpec((1, N_HEADS, HEAD_DIM, tk), lambda b, j: (b, 0, 0, j))
    return pl.pallas_call(
        _att_sample_kernel,
        grid=(n, past // tk),
        in_specs=[new(BF16), cache, cache,
                  pl.BlockSpec((1, N_HEADS, tk), lambda b, j: (b, 0, j)),
                  pl.BlockSpec((1, N_HEADS, 1), lambda b, j: (b, 0, 0)),
                  new(BF16), new(BF16),
                  pl.BlockSpec((1, N_HEADS, t), lambda b, j: (b, 0, 0))],
        out_specs=pl.BlockSpec((1, t, WA), lambda b, j: (b, 0, 0)),
        out_shape=jax.ShapeDtypeStruct((n, t, WA), F32),
        scratch_shapes=[pltpu.VMEM((N_HEADS, t, 1), F32), pltpu.VMEM((N_HEADS, t, 1), F32),
                        pltpu.VMEM((t, WA), F32)],
        compiler_params=_params("parallel", "arbitrary"),
        name="att_sample",
    )(qb, k_past, v_past, d_past, d_end, kb_new, vb_new, d_new)


def _first_index_of_max(x, axis):
    mx = jnp.max(x, axis=axis, keepdims=True)
    idx = lax.broadcasted_iota(jnp.int32, x.shape, axis)
    first = jnp.min(jnp.where(x == mx, idx, x.shape[axis]), axis=axis, keepdims=True)
    return mx, idx == first


def _route(s, bias):
    m = s.shape[1]
    sb = (s + bias).reshape(N_GROUPS, GROUP_SIZE, m)
    top1, is_top1 = _first_index_of_max(sb, 1)
    top2 = jnp.max(jnp.where(is_top1, NEG_INF, sb), axis=1, keepdims=True)
    grp = (top1 + top2).reshape(N_GROUPS, m)
    gi = lax.broadcasted_iota(jnp.int32, (N_GROUPS, N_GROUPS, m), 0)
    gj = lax.broadcasted_iota(jnp.int32, (N_GROUPS, N_GROUPS, m), 1)
    other, mine = grp[None, :, :], grp[:, None, :]
    beats = (other > mine) | ((other == mine) & (gj < gi))
    g_rank = jnp.sum(beats.astype(jnp.int32), axis=1)
    g_keep = (g_rank < TOPK_GROUPS)[:, None, :]
    cand = jnp.where(g_keep, sb, NEG_INF).reshape(N_EXPERTS, m)
    picks = []
    for _ in range(TOP_K):
        _, pick = _first_index_of_max(cand, 0)
        picks.append(pick)
        cand = jnp.where(pick, NEG_INF, cand)
    w = jnp.concatenate([jnp.sum(jnp.where(pk, s, 0.0), axis=0, keepdims=True) for pk in picks], axis=0)
    return picks, w / jnp.sum(w, axis=0, keepdims=True) * ROUTED_SCALE


HALF_MASK = 0xFFFF0000


def _pack_halves(x):
    c = x.shape[1] // 2
    lo = lax.bitcast_convert_type(x[:, :c].astype(BF16).astype(F32), jnp.uint32) >> jnp.uint32(16)
    hi = lax.bitcast_convert_type(x[:, c:].astype(BF16).astype(F32), jnp.uint32) & jnp.uint32(HALF_MASK)
    return lax.bitcast_convert_type(lo | hi, jnp.int32)


def _unpack_halves(words):
    w = lax.bitcast_convert_type(words, jnp.uint32)
    lo = lax.bitcast_convert_type(w << jnp.uint32(16), F32)
    hi = lax.bitcast_convert_type(w & jnp.uint32(HALF_MASK), F32)
    return lo, hi


def _outproj_kernel(x_ref, lru_ref, att_ref, mod_ref, gatt_ref, wtop_ref, wbot_ref, gpost_ref, gpre_ref,
                    rwh_ref, rwl_ref, rb_ref, cnt_in_ref, before_ref,
                    x1_ref, hf_ref, xw_ref, ids_ref, ranks_ref, gates_ref, cnt_out_ref, carry_ref, *, split):
    nb, tt, d = x_ref.shape
    m = nb * tt
    first = (pl.program_id(0) == 0) & (pl.program_id(1) == 0)

    @pl.when(first)
    def _():
        carry_ref[...] = cnt_in_ref[...]

    if split is not None:
        @pl.when((pl.program_id(0) == split) & (pl.program_id(1) == 0))
        def _():
            carry_ref[...] = jnp.zeros_like(carry_ref)

    mod = mod_ref[...]
    att_n = _rms(att_ref[...], gatt_ref[...]).reshape(m, WA).astype(BF16)
    mix = _dot(lru_ref[...].reshape(m, WL), wtop_ref[...]) + _dot(att_n, wbot_ref[...])
    x1 = x_ref[...] + mod[:, 2:3, :] * _rms(mix, gpost_ref[...]).reshape(nb, tt, d)
    x1_ref[...] = x1
    hf = (_rms(x1, gpre_ref[...]) * (1.0 + mod[:, 4:5, :]) + mod[:, 3:4, :]).reshape(m, d)
    hf_hi = hf.astype(BF16)
    hf_ref[...] = hf_hi.reshape(nb, tt, d)
    hf_lo = (hf - hf_hi.astype(F32)).astype(BF16)
    rwh = rwh_ref[...]
    both = _dot_nt(jnp.concatenate([rwh, rwl_ref[...]], axis=0), hf_hi)
    logits = both[:N_EXPERTS] + both[N_EXPERTS:] + _dot_nt(rwh, hf_lo)
    picks, gates = _route(_sigmoid(logits), rb_ref[...])
    xw_ref[...] = _pack_halves(hf).reshape(nb, tt, d // 2)

    sel = jnp.zeros((N_EXPERTS, m), F32)
    for pk in picks:
        sel = sel + pk.astype(F32)
    prior = _dot(sel.astype(BF16), before_ref[...]) + carry_ref[...]
    expert = lax.broadcasted_iota(jnp.int32, (N_EXPERTS, m), 0).astype(F32)
    take = lambda pk, v: jnp.sum(jnp.where(pk, v, 0.0), axis=0, keepdims=True)
    ids_ref[...] = jnp.concatenate([take(pk, expert) for pk in picks], axis=0).astype(jnp.int32)
    ranks_ref[...] = jnp.concatenate([take(pk, prior) for pk in picks], axis=0).astype(jnp.int32)
    carry_ref[...] += jnp.sum(sel, axis=1, keepdims=True)
    cnt_out_ref[0] = carry_ref[...]
    gates = jnp.concatenate([gates, jnp.zeros((LANES - TOP_K, m), F32)], axis=0)
    gates_ref[...] = gates.T.reshape(nb, tt, LANES)


def _outproj(x, lru_n, att, mod, g_att, w_top, w_bot, g_post, g_pre, rw_hi, rw_lo, r_bias, cnt_in, split=None):
    n, t, d = x.shape
    nb, tt = _seq_blocks(n, t)
    m = nb * tt
    steps_t = t // tt
    n_groups = 1 if split is None else 2
    blk = lambda w: pl.BlockSpec((nb, tt, w), lambda i, j: (i, j, 0))
    const = lambda shape: pl.BlockSpec(shape, lambda i, j: (0,) * len(shape))
    per_tok = pl.BlockSpec((TOP_K, m), lambda i, j: (0, i * steps_t + j))
    group_of = (lambda i: 0) if split is None else (lambda i: jnp.where(i < split, 0, 1))
    before = jnp.asarray(np.triu(np.ones((m, m), np.float32), k=1), BF16)
    return pl.pallas_call(
        functools.partial(_outproj_kernel, split=split),
        grid=(n // nb, steps_t),
        in_specs=[blk(d), blk(WL), blk(WA), pl.BlockSpec((nb, 6, d), lambda i, j: (i, 0, 0)),
                  const((1, WA)), const((WL, d)), const((WA, d)), const((1, d)), const((1, d)),
                  const((N_EXPERTS, d)), const((N_EXPERTS, d)), const((N_EXPERTS, 1)), const((N_EXPERTS, 1)),
                  const((m, m))],
        out_specs=[blk(d), blk(d), blk(d // 2), per_tok, per_tok, blk(LANES),
                   pl.BlockSpec((1, N_EXPERTS, 1), lambda i, j: (group_of(i), 0, 0))],
        out_shape=[jax.ShapeDtypeStruct((n, t, d), F32), jax.ShapeDtypeStruct((n, t, d), BF16),
                   jax.ShapeDtypeStruct((n, t, d // 2), jnp.int32),
                   jax.ShapeDtypeStruct((TOP_K, n * t), jnp.int32), jax.ShapeDtypeStruct((TOP_K, n * t), jnp.int32),
                   jax.ShapeDtypeStruct((n, t, LANES), F32), jax.ShapeDtypeStruct((n_groups, N_EXPERTS, 1), F32)],
        scratch_shapes=[pltpu.VMEM((N_EXPERTS, 1), F32)],
        compiler_params=_params("arbitrary", "arbitrary"),
        name="outproj_router",
    )(x, lru_n, att, mod, g_att, w_top, w_bot, g_post, g_pre, rw_hi, rw_lo, r_bias, cnt_in, before)


def _subcore_ranges(n_items):
    info = plsc.get_sparse_core_info()
    n_workers = info.num_cores * info.num_subcores
    per_worker = n_items // n_workers
    assert per_worker * n_workers == n_items and per_worker % GATHER_ROWS == 0
    return info, plsc.VectorSubcoreMesh(core_axis_name="c", subcore_axis_name="s"), per_worker


def _scatter_rows(sources, pos, n_slots):
    m_tot, c = pos.shape[1], sources[0][0].shape[1]
    assert sum(cnt for _, _, cnt in sources) == m_tot
    assert all(first % GATHER_ROWS == 0 and cnt % GATHER_ROWS == 0 for _, first, cnt in sources)
    info, mesh, per_worker = _subcore_ranges(m_tot)
    pos_flat = pos.reshape(-1)

    @functools.partial(
        pl.kernel, mesh=mesh, out_type=jax.ShapeDtypeStruct((n_slots, c), jnp.int32),
        scratch_types=[pltpu.VMEM((GATHER_ROWS,), jnp.int32), pltpu.VMEM((GATHER_ROWS, c), jnp.int32),
                       pltpu.SemaphoreType.DMA])
    def scatter(*refs):
        src_refs, (pos_hbm, out_hbm, idx_v, rows_v, sem) = refs[:len(sources)], refs[len(sources):]
        worker = lax.axis_index("s") * info.num_cores + lax.axis_index("c")
        base = worker * per_worker

        @pl.loop(0, per_worker // GATHER_ROWS)
        def _(step):
            off = pl.multiple_of(base + step * GATHER_ROWS, GATHER_ROWS)
            token0 = 0
            for src_hbm, (_, first, cnt) in zip(src_refs, sources):
                @pl.when((off >= token0) & (off < token0 + cnt))
                def _(src_hbm=src_hbm, shift=first - token0):
                    pltpu.sync_copy(src_hbm.at[pl.ds(pl.multiple_of(off + shift, GATHER_ROWS), GATHER_ROWS)], rows_v)
                token0 += cnt
            for r in range(TOP_K):
                pltpu.sync_copy(pos_hbm.at[pl.ds(pl.multiple_of(r * m_tot + off, GATHER_ROWS), GATHER_ROWS)], idx_v)
                pltpu.async_copy(rows_v, out_hbm.at[idx_v], sem).wait()

    return scatter(*[a for a, _, _ in sources], pos_flat)


def _swiglu_halves(lo, hi, wg, wu, wd):
    c = lo.shape[1]
    hg = _dot(lo, wg[:c]) + _dot(hi, wg[c:])
    hu = _dot(lo, wu[:c]) + _dot(hi, wu[c:])
    return _dot((_silu(hg) * hu).astype(BF16), wd)


def _expert_kernel(te_ref, valid_ref, x_ref, wg_ref, wu_ref, wd_ref, y_ref, wg_bf, wu_bf, wd_bf):
    i = pl.program_id(0)
    valid = valid_ref[i]

    @pl.when((i == 0) | (te_ref[i] != te_ref[jnp.maximum(i - 1, 0)]))
    def _():
        wg_bf[...] = wg_ref[0].astype(BF16)
        wu_bf[...] = wu_ref[0].astype(BF16)
        wd_bf[...] = wd_ref[0].astype(BF16)

    @pl.when(valid > 0)
    def _():
        w = x_ref[...]
        row = lax.broadcasted_iota(jnp.int32, w.shape, 0)
        lo, hi = _unpack_halves(jnp.where(row < valid, w, 0))
        y_ref[...] = _pack_halves(_swiglu_halves(lo.astype(BF16), hi.astype(BF16), wg_bf[...], wu_bf[...], wd_bf[...]))


def _experts(xs, tile_expert, tile_valid, wg, wu, wd):
    n_slots, c = xs.shape
    n_tiles = n_slots // EXPERT_TILE
    d = 2 * c
    rows = pl.BlockSpec((EXPERT_TILE, c), lambda i, te, tv: (i, 0))
    weight = lambda shape: pl.BlockSpec((1,) + shape, lambda i, te, tv: (te[i], 0, 0))
    return pl.pallas_call(
        _expert_kernel,
        grid_spec=pltpu.PrefetchScalarGridSpec(
            num_scalar_prefetch=2, grid=(n_tiles,),
            in_specs=[rows, weight((d, D_EXPERT)), weight((d, D_EXPERT)), weight((D_EXPERT, d))],
            out_specs=rows,
            scratch_shapes=[pltpu.VMEM((d, D_EXPERT), BF16), pltpu.VMEM((d, D_EXPERT), BF16),
                            pltpu.VMEM((D_EXPERT, d), BF16)]),
        out_shape=jax.ShapeDtypeStruct((n_slots, c), jnp.int32),
        compiler_params=_params("arbitrary"),
        name="moe_experts",
    )(tile_expert, tile_valid, xs, wg, wu, wd)


def _gather_rows(table, idx):
    b, c = idx.shape[0], table.shape[1]
    info, mesh, per_worker = _subcore_ranges(b)

    n_steps = per_worker // GATHER_ROWS
    assert n_steps % 2 == 0

    @functools.partial(
        pl.kernel, mesh=mesh, out_type=jax.ShapeDtypeStruct((b, c), jnp.int32),
        scratch_types=[pltpu.VMEM((2, GATHER_ROWS), jnp.int32), pltpu.VMEM((2, GATHER_ROWS, c), jnp.int32),
                       pltpu.SemaphoreType.DMA((2,))])
    def gather(table_hbm, idx_hbm, out_hbm, idx_v, rows_v, sems):
        worker = lax.axis_index("s") * info.num_cores + lax.axis_index("c")
        base = worker * per_worker

        def chunk(step):
            return pl.ds(pl.multiple_of(base + step * GATHER_ROWS, GATHER_ROWS), GATHER_ROWS)

        def stream(buf):
            return pltpu.make_async_copy(table_hbm.at[idx_v.at[buf]], rows_v.at[buf], sems.at[buf])

        def start(step, buf):
            pltpu.sync_copy(idx_hbm.at[chunk(step)], idx_v.at[buf])
            stream(buf).start()

        def finish(step, buf):
            stream(buf).wait()
            pltpu.sync_copy(rows_v.at[buf], out_hbm.at[chunk(step)])

        start(0, 0)

        @pl.loop(0, n_steps, step=2)
        def _(step):
            start(step + 1, 1)
            finish(step, 0)

            @pl.when(step + 2 < n_steps)
            def _():
                start(step + 2, 0)

            finish(step + 1, 1)

    return gather(table, idx)


def _combine_kernel(rows_ref, gates_ref, hf_ref, x1_ref, mod_ref, sg_ref, su_ref, sd_ref, gpost_ref, y_ref):
    nb, tt, d = hf_ref.shape
    m = nb * tt
    c = d // 2
    x = hf_ref[...].reshape(m, d)
    shared = _swiglu_halves(x[:, :c], x[:, c:], sg_ref[...], su_ref[...], sd_ref[...])
    gates = gates_ref[...].reshape(m, LANES)
    acc_lo = shared[:, :c]
    acc_hi = shared[:, c:]
    for r in range(TOP_K):
        lo, hi = _unpack_halves(rows_ref[r])
        g = gates[:, r:r + 1]
        acc_lo = acc_lo + g * lo
        acc_hi = acc_hi + g * hi
    z = _rms(jnp.concatenate([acc_lo, acc_hi], axis=1), gpost_ref[...]).reshape(nb, tt, d)
    y_ref[...] = x1_ref[...] + mod_ref[...][:, 5:6, :] * z


def _combine(rows, first_tile, gates_t, hf, x1, mod, sg, su, sd, g_post, blocks=None, y_prev=None):
    n, t, d = hf.shape
    nb, tt = _seq_blocks(n, t)
    m = nb * tt
    steps_t = t // tt
    b0, nblocks = (0, n // nb) if blocks is None else blocks
    blk = lambda w: pl.BlockSpec((nb, tt, w), lambda i, j: (b0 + i, j, 0))
    const = lambda shape: pl.BlockSpec(shape, lambda i, j: (0,) * len(shape))
    in_specs = [pl.BlockSpec((TOP_K, m, d // 2), lambda i, j: (0, first_tile + i * steps_t + j, 0)),
                blk(LANES), blk(d), blk(d), pl.BlockSpec((nb, 6, d), lambda i, j: (b0 + i, 0, 0)),
                const((d, D_EXPERT)), const((d, D_EXPERT)), const((D_EXPERT, d)), const((1, d))]
    args = (rows, gates_t, hf, x1, mod, sg, su, sd, g_post)
    kernel_fn, aliases = _combine_kernel, {}
    if y_prev is not None:
        in_specs.append(pl.BlockSpec(memory_space=pl.ANY))
        args += (y_prev,)
        aliases = {len(args) - 1: 0}
        kernel_fn = lambda *refs: _combine_kernel(*refs[:len(args) - 1], refs[-1])
    return pl.pallas_call(
        kernel_fn,
        grid=(nblocks, steps_t),
        in_specs=in_specs,
        out_specs=blk(d),
        out_shape=jax.ShapeDtypeStruct((n, t, d), F32),
        input_output_aliases=aliases,
        compiler_params=_params("parallel", "parallel"),
        name="moe_combine",
    )(*args)


def _slots_kernel(starts_ref, ids_ref, ranks_ref, pos_ref):
    ids = ids_ref[...]

    def add_start(e, pos):
        return pos + jnp.where(ids == e, starts_ref[e], 0)

    pos_ref[...] = lax.fori_loop(0, N_EXPERTS, add_start, ranks_ref[...])


def _slots(starts, ids, ranks):
    k, m_tot = ids.shape
    cols = math.gcd(m_tot, SLOT_COLS)
    assert cols % LANES == 0
    blk = pl.BlockSpec((k, cols), lambda i: (0, i))
    return pl.pallas_call(
        _slots_kernel,
        grid=(m_tot // cols,),
        in_specs=[pl.BlockSpec(memory_space=pltpu.SMEM), blk, blk],
        out_specs=blk,
        out_shape=jax.ShapeDtypeStruct((k, m_tot), jnp.int32),
        compiler_params=_params("arbitrary"),
        name="moe_slots",
    )(starts, ids, ranks)


def _slot_plan(ids, ranks, counts):
    n_pairs = ids.shape[0] * ids.shape[1]
    n_tiles = -(-(n_pairs + N_EXPERTS * (EXPERT_TILE - 1)) // EXPERT_TILE)
    cnt = counts.reshape(N_EXPERTS).astype(jnp.int32)
    padded = (cnt + EXPERT_TILE - 1) // EXPERT_TILE * EXPERT_TILE
    ends = jnp.cumsum(padded)
    starts = ends - padded
    pos = _slots(starts, ids, ranks)
    tile_start = jnp.arange(n_tiles, dtype=jnp.int32) * EXPERT_TILE
    in_expert = (tile_start[:, None] >= starts[None, :]) & (tile_start[:, None] < ends[None, :])
    tile_expert = jnp.sum(jnp.where(in_expert, jnp.arange(N_EXPERTS, dtype=jnp.int32)[None, :], 0), axis=1)
    tile_fill = jnp.sum(jnp.where(in_expert, (starts + cnt)[None, :] - tile_start[:, None], 0), axis=1)
    tile_valid = jnp.clip(tile_fill, 0, EXPERT_TILE).astype(jnp.int32)
    return pos, tile_expert.astype(jnp.int32), tile_valid, n_tiles * EXPERT_TILE


def _block_diag(w):
    g, bw, _ = w.shape
    eye = jnp.eye(g, dtype=w.dtype)
    return (eye[:, None, :, None] * w[:, :, None, :]).reshape(g * bw, g * bw)


def _prep_weights(p):
    d_main = 2 * WL + 3 * WA
    w_in = p["w_in"]
    rw_t = p["router_w"].T
    rw_hi = rw_t.astype(BF16)
    row = lambda v: v.reshape(1, -1)
    return dict(
        w_mod=p["w_mod"], b_mod=p["b_mod"],
        g_pre_mix=row(p["g_pre_mix"]), g_post_mix=row(p["g_post_mix"]),
        g_pre_ffn=row(p["g_pre_ffn"]), g_post_ffn=row(p["g_post_ffn"]),
        w_main=w_in[:, :d_main].astype(BF16),
        w_f=jnp.pad(w_in[:, d_main:], ((0, 0), (0, LANES - N_HEADS))).astype(BF16),
        b_f=jnp.pad(p["b_f"], (0, LANES - N_HEADS)).reshape(1, LANES),
        conv_w=p["conv_w"], conv_b=row(p["conv_b"]),
        wr_bd=_block_diag(p["w_r"]).astype(BF16), b_r=row(p["b_r"]),
        wi_bd=_block_diag(p["w_i"]).astype(BF16), b_i=row(p["b_i"]),
        lam=row(p["lru_lambda"]), g_lru=row(p["g_lru_out"]), g_att=row(p["g_att_out"]),
        w_top=p["w_out"][:WL].astype(BF16), w_bot=p["w_out"][WL:].astype(BF16),
        rw_hi=rw_hi, rw_lo=(rw_t - rw_hi.astype(F32)).astype(BF16),
        r_bias=p["router_bias"].reshape(N_EXPERTS, 1),
        wg=p["w_gate"], wu=p["w_up"], wd=p["w_down"],
        sg=p["ws_gate"].astype(BF16), su=p["ws_up"].astype(BF16), sd=p["ws_down"].astype(BF16),
    )


def _mixers(x, mod, conv0, h0, past, w, cnt_in, split=None):
    n, t, _ = x.shape
    proj_args = (x, mod, w["g_pre_mix"], w["w_main"], w["w_f"], w["b_f"])
    lru_weights = (w["conv_w"], w["conv_b"], w["wr_bd"], w["b_r"], w["wi_bd"], w["b_i"], w["lam"], w["g_lru"])
    if past is None:
        lru_n, conv_new, h_new, k, v, lf_t, qt, ka, vt, bpre = _inproj_prompt(
            *proj_args, conv0, h0.reshape(n, 1, WL), lru_weights)
        lf = jnp.transpose(lf_t, (0, 2, 1))
        att = _att_prompt(qt, ka, vt, bpre)
    else:
        xl, gy, qb, kb, vb, k, v, lf = _inproj(*proj_args)
        k_past, v_past, lf_past = past
        plen = k_past.shape[1]
        by_head = lambda a: jnp.transpose(a, (0, 2, 1)).reshape(n * N_HEADS, a.shape[1])
        d_new = _cumsum_rows(by_head(lf), t).reshape(n, N_HEADS, t)
        d_past = _cumsum_rows(by_head(lf_past), min(CUMSUM_COLS, plen)).reshape(n, N_HEADS, plen)
        att = _att_sample(qb, k_past, v_past, d_past, kb, vb, d_new)
        lru_n, conv_new, h_new = _lru(xl, gy, conv0, h0.reshape(n, 1, WL), *lru_weights)
    routed = _outproj(x, lru_n, att, mod, w["g_att"], w["w_top"], w["w_bot"], w["g_post_mix"],
                      w["g_pre_ffn"], w["rw_hi"], w["rw_lo"], w["r_bias"], cnt_in, split)
    state = (k.reshape(n, t, N_HEADS, HEAD_DIM), v.reshape(n, t, N_HEADS, HEAD_DIM), lf,
             conv_new, h_new.reshape(n, WL))
    return routed, state


def _layer(xp, xs, mod_p, mod_s, conv_s, h_s, past_s, w):
    n_p = xp.shape[0]
    conv0 = jnp.zeros((n_p, CONV_W - 1, WL), F32)
    h0 = jnp.zeros((n_p, WL), F32)
    zero_cnt = jnp.zeros((N_EXPERTS, 1), F32)
    t_p = xp.shape[1]
    split = 1 if n_p > 1 else None
    (x1_p, hf_p, xw_p, ids_p, rk_p, g_p, cnt_p), st_p = _mixers(xp, mod_p, conv0, h0, None, w, zero_cnt, split)
    cnt_tail = cnt_p[1] if n_p > 1 else zero_cnt
    (x1_s, hf_s, xw_s, ids_s, rk_s, g_s, cnt_s), st_s = _mixers(xs, mod_s, conv_s, h_s, past_s, w, cnt_tail)

    half = xw_p.shape[-1]
    xw_p = xw_p.reshape(-1, half)

    def routed(sources, ids, ranks, counts):
        pos, tile_expert, tile_valid, n_slots = _slot_plan(ids, ranks, counts)
        ys = _experts(_scatter_rows(sources, pos, n_slots), tile_expert, tile_valid, w["wg"], w["wu"], w["wd"])
        return _gather_rows(ys, pos.reshape(-1)).reshape(TOP_K, ids.shape[1], half)

    rows_a = routed([(xw_p, 0, t_p)], ids_p[:, :t_p], rk_p[:, :t_p], cnt_p[0])
    xw_s = xw_s.reshape(-1, half)
    sources_b = ([(xw_p, t_p, (n_p - 1) * t_p)] if n_p > 1 else []) + [(xw_s, 0, xw_s.shape[0])]
    rows_b = routed(sources_b, jnp.concatenate([ids_p[:, t_p:], ids_s], axis=1),
                    jnp.concatenate([rk_p[:, t_p:], rk_s], axis=1), cnt_s[0])
    shared = (w["sg"], w["su"], w["sd"], w["g_post_ffn"])
    yp = _combine(rows_a, 0, g_p, hf_p, x1_p, mod_p, *shared, blocks=(0, 1))
    if n_p > 1:
        yp = _combine(rows_b, 0, g_p, hf_p, x1_p, mod_p, *shared, blocks=(1, n_p - 1), y_prev=yp)
    ysmp = _combine(rows_b, (n_p - 1) * t_p // ROW_TILE, g_s, hf_s, x1_s, mod_s, *shared)
    return yp, ysmp, st_p, st_s


def kernel(x_prompt, x_sample, c_prompt, c_sample, cache_k, cache_v, cache_logf, state_conv, state_lru, w_mod, b_mod, g_pre_mix, g_post_mix, g_pre_ffn, g_post_ffn, w_in, conv_w, conv_b, w_r, b_r, w_i, b_i, lru_lambda, b_f, g_lru_out, g_att_out, w_out, router_w, router_bias, w_gate, w_up, w_down, ws_gate, ws_up, ws_down):
    names = ("w_mod", "b_mod", "g_pre_mix", "g_post_mix", "g_pre_ffn", "g_post_ffn", "w_in", "conv_w", "conv_b",
             "w_r", "b_r", "w_i", "b_i", "lru_lambda", "b_f", "g_lru_out", "g_att_out", "w_out", "router_w",
             "router_bias", "w_gate", "w_up", "w_down", "ws_gate", "ws_up", "ws_down")
    stacked = (w_mod, b_mod, g_pre_mix, g_post_mix, g_pre_ffn, g_post_ffn, w_in, conv_w, conv_b, w_r, b_r, w_i, b_i,
               lru_lambda, b_f, g_lru_out, g_att_out, w_out, router_w, router_bias, w_gate, w_up, w_down,
               ws_gate, ws_up, ws_down)
    depth = w_mod.shape[0]
    n_p, n_s = x_prompt.shape[0], x_sample.shape[0]
    yp, ys = x_prompt, x_sample
    st_p, st_s = [], []
    for l in range(depth):
        w = _prep_weights({k: v[l] for k, v in zip(names, stacked)})
        mod = _modulation(jnp.concatenate([c_prompt, c_sample], axis=0), w["w_mod"], w["b_mod"])
        mod = mod.reshape(n_p + n_s, 6, D_MODEL)
        yp, ys, sp, ss = _layer(yp, ys, mod[:n_p], mod[n_p:], state_conv[l], state_lru[l],
                                (cache_k[l], cache_v[l], cache_logf[l]), w)
        st_p.append(sp)
        st_s.append(ss)
    stack = lambda sts, i: jnp.stack([s[i] for s in sts])
    return (yp, ys) + tuple(stack(st_p, i) for i in range(5)) + tuple(stack(st_s, i) for i in range(5))
```
